```python
import math
import jax, jax.numpy as jnp
from jax import lax
import numpy as np

D_MODEL = 2048
BATCH = 32
SEQ = 256
DEPTH = 1
DEC_BATCH = 2
DEC_SEQ = 4096
PAST_LEN = 512

GRID_W = 64
SSM_WIDTH = D_MODEL // 2
SSM_GROUP = 16
SSM_GROUPS = SSM_WIDTH // SSM_GROUP
SSM_STATE = 64
HYENA_WIDTH = D_MODEL // 2
SHORT_CONV = 3
FILTER_BANDS = 16
FILTER_EMB = 1 + 2 * FILTER_BANDS
FILTER_HIDDEN = 64
DECAY_TARGET = 1e-2
DECAY_FAST = 0.3
DECAY_SLOW = 1.5
N_BRANCH = 2
IN_COLS = SSM_WIDTH + 3 * HYENA_WIDTH + N_BRANCH * D_MODEL
N_EXPERT_GROUPS = 4
EXPERTS_PER_GROUP = 4
N_EXPERTS = N_EXPERT_GROUPS * EXPERTS_PER_GROUP
TOP_K_INNER = 2
EXPERT_FF = D_MODEL // 4
N_MOD = 6
EPS = 1e-6

kernel_name = "hybrid_s5_hyena_hmoe_diffusion_step"


def _rmsnorm(x, g):
    xf = x.astype(jnp.float32)
    y = xf * lax.rsqrt(jnp.mean(xf * xf, axis=-1, keepdims=True) + EPS)
    return (y * g.astype(jnp.float32)).astype(x.dtype)


def _grid_pos_embed(n_tokens, dim, dtype):
    ROWS = n_tokens // GRID_W
    rows = jnp.broadcast_to(jnp.arange(ROWS)[:, None], (ROWS, GRID_W)).reshape(-1)
    cols = jnp.broadcast_to(jnp.arange(GRID_W)[None, :], (ROWS, GRID_W)).reshape(-1)
    quarter = dim // 4
    omega = 1.0 / (10000.0 ** (jnp.arange(quarter, dtype=jnp.float32) / quarter))

    def emb(p):
        ang = p.astype(jnp.float32)[:, None] * omega[None, :]
        return jnp.concatenate([jnp.sin(ang), jnp.cos(ang)], axis=-1)

    return jnp.concatenate([emb(rows), emb(cols)], axis=-1).astype(dtype)


def _linear_combine(e1, e2):
    a1, b1 = e1
    a2, b2 = e2
    return a1 * a2, a2 * b1 + b2


def _ssm_mixer(u, s0, a_re, a_im, log_dt, b_re, b_im, c_re, c_im, d_skip, want_state):
    f32 = jnp.float32
    bsz, L, _ = u.shape
    uf = u.astype(f32)
    ug = uf.reshape(bsz, L, SSM_GROUPS, SSM_GROUP).astype(jnp.complex64)
    y = uf * d_skip.astype(f32)
    finals = []
    for dr in range(2):
        reverse = dr == 1
        lam = lax.complex(a_re[dr].astype(f32), a_im[dr].astype(f32))
        dt = jnp.exp(log_dt[dr].astype(f32))[:, None]
        lam_bar = jnp.exp(lam * dt)
        b_mat = lax.complex(b_re[dr].astype(f32), b_im[dr].astype(f32))
        b_bar = ((lam_bar - 1.0) / lam)[..., None] * b_mat
        c_mat = lax.complex(c_re[dr].astype(f32), c_im[dr].astype(f32))
        bu = jnp.einsum('gpc,blgc->blgp', b_bar, ug)
        if s0 is not None:
            init = lax.complex(s0[:, dr, 0].astype(f32), s0[:, dr, 1].astype(f32))
            edge = L - 1 if reverse else 0
            bu = bu.at[:, edge].add(lam_bar[None] * init)
        a = jnp.broadcast_to(lam_bar, bu.shape)
        _, states = lax.associative_scan(_linear_combine, (a, bu), axis=1, reverse=reverse)
        y = y + jnp.real(jnp.einsum('gcp,blgp->blgc', c_mat, states)).reshape(bsz, L, SSM_WIDTH)
        if want_state:
            fin = states[:, 0] if reverse else states[:, -1]
            finals.append(jnp.stack([jnp.real(fin), jnp.imag(fin)], axis=1))
    new_state = jnp.stack(finals, axis=1) if want_state else None
    return y, new_state


def _short_conv(x, w, b):
    L = x.shape[1]
    half = SHORT_CONV // 2
    xp = jnp.pad(x, ((0, 0), (half, half), (0, 0)))
    out = b
    for k in range(SHORT_CONV):
        out = out + xp[:, k:k + L] * w[k]
    return out


def _hyena_filters(L, w_f1, b_f1, freq1, w_f2, b_f2, freq2, w_f3, b_f3):
    f32 = jnp.float32
    t = jnp.arange(L, dtype=f32)[:, None] / L
    bands = jnp.arange(1, FILTER_BANDS + 1, dtype=f32)[None, :]
    ang = (2.0 * math.pi) * t * bands
    z = jnp.concatenate([t, jnp.cos(ang), jnp.sin(ang)], axis=-1)
    h = jnp.sin(freq1.astype(f32) * (z @ w_f1.astype(f32) + b_f1.astype(f32)))
    h = jnp.sin(freq2.astype(f32) * (h @ w_f2.astype(f32) + b_f2.astype(f32)))
    h = (h @ w_f3.astype(f32) + b_f3.astype(f32)).reshape(L, 2, HYENA_WIDTH)
    rates = jnp.abs(jnp.linspace(math.log(DECAY_TARGET) / DECAY_FAST,
                                 math.log(DECAY_TARGET) / DECAY_SLOW, HYENA_WIDTH, dtype=f32))
    h = h * jnp.exp(-t[:, :, None] * rates)
    return h / (jnp.sum(jnp.abs(h), axis=(0, 1), keepdims=True) + EPS)


def _bidir_long_conv(v, h, bias):
    L = v.shape[1]
    k = jnp.concatenate([h[:, 0], jnp.zeros((1, HYENA_WIDTH), h.dtype), h[:0:-1, 1]], axis=0)
    vf = jnp.fft.rfft(v, n=2 * L, axis=1)
    kf = jnp.fft.rfft(k, axis=0)
    y = jnp.fft.irfft(vf * kf[None], n=2 * L, axis=1)[:, :L]
    return y + v * bias


def _hier_moe(x, w_rg, b_rg, w_re, b_re, w_g, w_u, w_d):
    bsz, L, D = x.shape
    xt = x.reshape(bsz * L, D)
    g_logits = (xt @ w_rg + b_rg).astype(jnp.float32)
    g_prob = jax.nn.softmax(g_logits, axis=-1)
    _, g_idx = lax.top_k(g_logits, 1)
    g_w = jnp.take_along_axis(g_prob, g_idx, axis=-1)
    e_logits = (jnp.einsum('td,gde->tge', xt, w_re) + b_re).astype(jnp.float32)
    e_sel = jnp.take_along_axis(e_logits, g_idx[:, :, None], axis=1)[:, 0]
    e_top, e_idx = lax.top_k(e_sel, TOP_K_INNER)
    e_w = jax.nn.softmax(e_top, axis=-1) * g_w
    expert = g_idx * EXPERTS_PER_GROUP + e_idx
    gate = jnp.einsum('tk,tke->te', e_w, jax.nn.one_hot(expert, N_EXPERTS, dtype=jnp.float32))
    hid = jax.nn.silu(jnp.einsum('td,edf->tef', xt, w_g)) * jnp.einsum('td,edf->tef', xt, w_u)
    out = jnp.einsum('tef,efd->td', hid * gate[:, :, None].astype(hid.dtype), w_d)
    return out.reshape(bsz, L, D)


def setup_inputs(seed: int = 0) -> dict:
    key = jax.random.key(seed)
    ks = iter(jax.random.split(key, 64))
    f32 = jnp.float32
    D = D_MODEL

    def nrm(shape, scale):
        return jax.random.normal(next(ks), shape, f32) * scale

    n_idx = jnp.arange(SSM_STATE, dtype=f32)
    return {
        "x_prompt": nrm((BATCH, SEQ, D), 1.0),
        "x_sample": nrm((DEC_BATCH, DEC_SEQ, D), 1.0),
        "state_ssm": nrm((DEC_BATCH, DEPTH, 2, 2, SSM_GROUPS, SSM_STATE), 0.5),
        "c": nrm((DEC_BATCH, D), 1.0),
        "c_ctx": nrm((D,), 1.0),
        "w_ada": nrm((DEPTH, D, N_MOD * D), 0.5 * D ** -0.5),
        "b_ada": nrm((DEPTH, N_MOD * D), 0.01),
        "g_norm1": 1.0 + nrm((DEPTH, D), 0.01),
        "g_norm2": 1.0 + nrm((DEPTH, D), 0.01),
        "w_in": nrm((DEPTH, D, IN_COLS), D ** -0.5),
        "ssm_a_re": -0.5 + nrm((DEPTH, 2, SSM_GROUPS, SSM_STATE), 0.01),
        "ssm_a_im": math.pi * n_idx + nrm((DEPTH, 2, SSM_GROUPS, SSM_STATE), 0.01),
        "ssm_log_dt": jax.random.uniform(next(ks), (DEPTH, 2, SSM_GROUPS), f32,
                                         math.log(1e-3), math.log(1e-1)),
        "ssm_b_re": nrm((DEPTH, 2, SSM_GROUPS, SSM_STATE, SSM_GROUP), (2 * SSM_GROUP) ** -0.5),
        "ssm_b_im": nrm((DEPTH, 2, SSM_GROUPS, SSM_STATE, SSM_GROUP), (2 * SSM_GROUP) ** -0.5),
        "ssm_c_re": nrm((DEPTH, 2, SSM_GROUPS, SSM_GROUP, SSM_STATE), (2 * SSM_STATE) ** -0.5),
        "ssm_c_im": nrm((DEPTH, 2, SSM_GROUPS, SSM_GROUP, SSM_STATE), (2 * SSM_STATE) ** -0.5),
        "ssm_d": nrm((DEPTH, SSM_WIDTH), 0.5),
        "w_glu": nrm((DEPTH, SSM_WIDTH, SSM_WIDTH), SSM_WIDTH ** -0.5),
        "b_glu": nrm((DEPTH, SSM_WIDTH), 0.01),
        "w_short": nrm((DEPTH, SHORT_CONV, 3 * HYENA_WIDTH), SHORT_CONV ** -0.5),
        "b_short": nrm((DEPTH, 3 * HYENA_WIDTH), 0.01),
        "w_f1": nrm((DEPTH, FILTER_EMB, FILTER_HIDDEN), FILTER_EMB ** -0.5),
        "b_f1": nrm((DEPTH, FILTER_HIDDEN), 0.01),
        "freq1": 1.0 + nrm((DEPTH, FILTER_HIDDEN), 0.01),
        "w_f2": nrm((DEPTH, FILTER_HIDDEN, FILTER_HIDDEN), FILTER_HIDDEN ** -0.5),
        "b_f2": nrm((DEPTH, FILTER_HIDDEN), 0.01),
        "freq2": 1.0 + nrm((DEPTH, FILTER_HIDDEN), 0.01),
        "w_f3": nrm((DEPTH, FILTER_HIDDEN, 2 * HYENA_WIDTH), FILTER_HIDDEN ** -0.5),
        "b_f3": nrm((DEPTH, 2 * HYENA_WIDTH), 0.01),
        "filter_bias": nrm((DEPTH, HYENA_WIDTH), 0.5),
        "w_branch_a": nrm((DEPTH, SSM_WIDTH, D), SSM_WIDTH ** -0.5),
        "w_branch_b": nrm((DEPTH, HYENA_WIDTH, D), HYENA_WIDTH ** -0.5),
        "w_out": nrm((DEPTH, D, D), D ** -0.5),
        "w_router_group": nrm((DEPTH, D, N_EXPERT_GROUPS), D ** -0.5),
        "b_router_group": nrm((DEPTH, N_EXPERT_GROUPS), 0.01),
        "w_router_expert": nrm((DEPTH, N_EXPERT_GROUPS, D, EXPERTS_PER_GROUP), D ** -0.5),
        "b_router_expert": nrm((DEPTH, N_EXPERT_GROUPS, EXPERTS_PER_GROUP), 0.01),
        "w_exp_gate": nrm((DEPTH, N_EXPERTS, D, EXPERT_FF), D ** -0.5),
        "w_exp_up": nrm((DEPTH, N_EXPERTS, D, EXPERT_FF), D ** -0.5),
        "w_exp_down": nrm((DEPTH, N_EXPERTS, EXPERT_FF, D), EXPERT_FF ** -0.5),
        "g_final": 1.0 + nrm((D,), 0.01),
    }


def reference(x_prompt, x_sample, state_ssm, c, c_ctx, w_ada, b_ada, g_norm1, g_norm2, w_in,
              ssm_a_re, ssm_a_im, ssm_log_dt, ssm_b_re, ssm_b_im, ssm_c_re, ssm_c_im, ssm_d,
              w_glu, b_glu, w_short, b_short, w_f1, b_f1, freq1, w_f2, b_f2, freq2, w_f3, b_f3,
              filter_bias, w_branch_a, w_branch_b, w_out, w_router_group, b_router_group,
              w_router_expert, b_router_expert, w_exp_gate, w_exp_up, w_exp_down, g_final):

    def run_layer(x, l, cond, s0, want_state):
        mod = jax.nn.silu(cond) @ w_ada[l] + b_ada[l]
        shift1, scale1, gate1, shift2, scale2, gate2 = jnp.split(mod[:, None, :], N_MOD, axis=-1)
        bsz, L, _ = x.shape

        h = _rmsnorm(x, g_norm1[l]) * (1.0 + scale1) + shift1
        proj = h @ w_in[l]
        o1 = SSM_WIDTH
        o2 = o1 + 3 * HYENA_WIDTH
        u_a = proj[..., :o1]
        u_b = proj[..., o1:o2]
        gates = jax.nn.sigmoid(proj[..., o2:]).reshape(bsz, L, N_BRANCH, D_MODEL)

        y_a, s_fin = _ssm_mixer(u_a, s0, ssm_a_re[l], ssm_a_im[l], ssm_log_dt[l],
                                ssm_b_re[l], ssm_b_im[l], ssm_c_re[l], ssm_c_im[l],
                                ssm_d[l], want_state)
        y_a = jax.nn.gelu(y_a.astype(x.dtype))
        y_a = y_a * jax.nn.sigmoid(y_a @ w_glu[l] + b_glu[l])

        ub = _short_conv(u_b, w_short[l], b_short[l])
        v, x0, x1 = jnp.split(ub, 3, axis=-1)
        filt = _hyena_filters(L, w_f1[l], b_f1[l], freq1[l], w_f2[l], b_f2[l], freq2[l],
                              w_f3[l], b_f3[l])
        z = _bidir_long_conv((v * x1).astype(jnp.float32), filt, filter_bias[l].astype(jnp.float32))
        y_b = x0 * z.astype(x0.dtype)

        merged = gates[:, :, 0] * (y_a @ w_branch_a[l]) + gates[:, :, 1] * (y_b @ w_branch_b[l])
        x = x + gate1 * (merged @ w_out[l])

        h = _rmsnorm(x, g_norm2[l]) * (1.0 + scale2) + shift2
        x = x + gate2 * _hier_moe(h, w_router_group[l], b_router_group[l], w_router_expert[l],
                                  b_router_expert[l], w_exp_gate[l], w_exp_up[l], w_exp_down[l])
        return x, s_fin

    cond_ctx = c_ctx[None, :]
    xc = x_prompt
    ctx_states = []
    for l in range(DEPTH):
        xc, s_l = run_layer(xc, l, cond_ctx, None, True)
        ctx_states.append(s_l)
    y_prompt = _rmsnorm(xc, g_final)
    new_state_ssm = jnp.stack(ctx_states, axis=1)

    xs = x_sample + _grid_pos_embed(x_sample.shape[1], D_MODEL, x_sample.dtype)[None]
    for l in range(DEPTH):
        xs, _ = run_layer(xs, l, c, state_ssm[:, l], False)
    y_sample = _rmsnorm(xs, g_final)

    return (y_prompt, y_sample, new_state_ssm)
```

```python
import functools
import math

import jax
import jax.numpy as jnp
from jax import lax
from jax.experimental import pallas as pl
from jax.experimental.pallas import tpu as pltpu

F32 = jnp.float32
BF16 = jnp.bfloat16
EPS = 1e-6
GRID_W = 64
N_MOD = 6
TOP_K_INNER = 2
DECAY_TARGET = 1e-2
DECAY_FAST = 0.3
DECAY_SLOW = 1.5
LANES = 128
SUBLANES = 8
MXU_SIDE = 256
VMEM_LIMIT = 56 * 1024 * 1024


def _cp(sem, vmem=VMEM_LIMIT):
    return pltpu.CompilerParams(dimension_semantics=sem, vmem_limit_bytes=vmem)


def _dot(a, b):
    return jnp.dot(a, b, preferred_element_type=F32)


def _split(a):
    hi = a.astype(BF16)
    lo = (a - hi.astype(F32)).astype(BF16)
    return hi, lo


def _dot3(a, b):
    a_hi, a_lo = _split(a)
    b_hi, b_lo = _split(b)
    return _dot(a_hi, b_hi) + _dot(a_lo, b_hi) + _dot(a_hi, b_lo)


def _rms(x, g):
    ms = jnp.mean(x * x, axis=-1, keepdims=True)
    return x * lax.rsqrt(ms + EPS) * g


def _ada_kernel(c_ref, w_ref, b_ref, o_ref):
    c = c_ref[...]
    a = c * jax.nn.sigmoid(c)
    o_ref[...] = _dot3(a, w_ref[...]) + b_ref[...]


def _ada_mod(cond8, w_ada, b_ada):
    d, n = w_ada.shape
    tn = min(n, 1024)
    return pl.pallas_call(
        _ada_kernel,
        out_shape=jax.ShapeDtypeStruct((SUBLANES, n), F32),
        grid=(n // tn,),
        in_specs=[pl.BlockSpec((SUBLANES, d), lambda j: (0, 0)),
                  pl.BlockSpec((d, tn), lambda j: (0, j)),
                  pl.BlockSpec((1, tn), lambda j: (0, j))],
        out_specs=pl.BlockSpec((SUBLANES, tn), lambda j: (0, j)),
        compiler_params=_cp(("arbitrary",)),
        name="ada_mod",
    )(cond8, w_ada, b_ada)


class _Tok:
    def __init__(self, n_ctx, n_lat, lat_len, tm):
        assert n_ctx % tm == 0 and lat_len % tm == 0 and n_lat % lat_len == 0
        self.tm = tm
        self.nct = n_ctx // tm
        self.nst = n_lat // tm
        self.per_seq = lat_len // tm
        self.n = self.nct + self.nst

    def ctx_idx(self, i):
        return jnp.minimum(i, self.nct - 1)

    def lat_idx(self, i):
        return jnp.maximum(i - self.nct, 0)

    def pos_idx(self, i):
        return self.lat_idx(i) % self.per_seq

    def cond(self, i):
        return jnp.where(i < self.nct, 0, 1 + self.lat_idx(i) // self.per_seq)


def _norm_mod_kernel(tok, xp_ref, xs_ref, pos_ref, sh_ref, sc_ref, g_ref, h_ref):
    i = pl.program_id(0)

    def modulate(x):
        y = _rms(x, g_ref[...])
        return (y * (1.0 + sc_ref[0]) + sh_ref[0]).astype(h_ref.dtype)

    @pl.when(i < tok.nct)
    def _():
        h_ref[...] = modulate(xp_ref[...])

    @pl.when(i >= tok.nct)
    def _():
        h_ref[...] = modulate(xs_ref[...] + pos_ref[...])


def _norm_mod(xp, xs, pos, mod, g1, tok):
    d = xp.shape[1]
    tm = tok.tm
    t_all = xp.shape[0] + xs.shape[0]
    return pl.pallas_call(
        functools.partial(_norm_mod_kernel, tok),
        out_shape=jax.ShapeDtypeStruct((t_all, d), BF16),
        grid=(tok.n,),
        in_specs=[
            pl.BlockSpec((tm, d), lambda i: (tok.ctx_idx(i), 0)),
            pl.BlockSpec((tm, d), lambda i: (tok.lat_idx(i), 0)),
            pl.BlockSpec((tm, d), lambda i: (tok.pos_idx(i), 0)),
            pl.BlockSpec((1, 1, d), lambda i: (tok.cond(i) * N_MOD + 0, 0, 0)),
            pl.BlockSpec((1, 1, d), lambda i: (tok.cond(i) * N_MOD + 1, 0, 0)),
            pl.BlockSpec((1, d), lambda i: (0, 0)),
        ],
        out_specs=pl.BlockSpec((tm, d), lambda i: (i, 0)),
        compiler_params=_cp(("parallel",)),
        name="norm_mod",
    )(xp, xs, pos, mod, mod, g1)


def _in_proj_kernel(first_gate_tile, h_ref, w_ref, o_ref, wb_ref):
    n = pl.program_id(0)
    m = pl.program_id(1)

    @pl.when(m == 0)
    def _():
        wb_ref[...] = w_ref[...].astype(BF16)

    acc = _dot(h_ref[...], wb_ref[...])

    @pl.when(n < first_gate_tile)
    def _():
        o_ref[...] = acc.astype(o_ref.dtype)

    @pl.when(n >= first_gate_tile)
    def _():
        o_ref[...] = jax.nn.sigmoid(acc).astype(o_ref.dtype)


def _in_proj(h, w_in, gate_col0, tm, tn):
    t_all, d = h.shape
    ncols = w_in.shape[1]
    assert gate_col0 % tn == 0 and ncols % tn == 0 and t_all % tm == 0
    return pl.pallas_call(
        functools.partial(_in_proj_kernel, gate_col0 // tn),
        out_shape=jax.ShapeDtypeStruct((t_all, ncols), BF16),
        grid=(ncols // tn, t_all // tm),
        in_specs=[pl.BlockSpec((tm, d), lambda n, m: (m, 0)),
                  pl.BlockSpec((d, tn), lambda n, m: (0, n))],
        out_specs=pl.BlockSpec((tm, tn), lambda n, m: (m, n)),
        scratch_shapes=[pltpu.VMEM((d, tn), BF16)],
        compiler_params=_cp(("arbitrary", "arbitrary")),
        name="in_proj",
    )(h, w_in)


CHUNK = 16
GROUP_PAD = 8


def _ssm_operators(a_re, a_im, log_dt, b_re, b_im, c_re, c_im, d_skip):
    t = CHUNK
    g, p = a_re.shape[1:]
    c = b_re.shape[-1]
    lam = lax.complex(a_re.astype(F32), a_im.astype(F32))
    dt = jnp.exp(log_dt.astype(F32))[..., None]
    ld = lam * dt
    lam_bar = jnp.exp(ld)
    b_bar = ((lam_bar - 1.0) / lam)[..., None] * lax.complex(b_re.astype(F32), b_im.astype(F32))
    c_mat = lax.complex(c_re.astype(F32), c_im.astype(F32))
    taus = jnp.arange(t + 1, dtype=F32)
    e_pow = jnp.exp(taus[None, None, :, None] * ld[:, :, None, :])

    k_all = jnp.real(jnp.einsum('dgop,dgtp,dgpi->dgtoi', c_mat, e_pow[:, :, :t], b_bar))
    kf, kb = k_all[0], k_all[1]
    ii = jnp.arange(t)[:, None]
    jj = jnp.arange(t)[None, :]
    tf = jnp.clip(jj - ii, 0, t - 1)
    tb = jnp.clip(ii - jj, 0, t - 1)
    m_f = kf[:, tf]
    m_b = kb[:, tb]
    sel_f = (jj >= ii)[None, :, :, None, None]
    sel_b = (ii >= jj)[None, :, :, None, None]
    m = jnp.where(sel_f, m_f, 0.0) + jnp.where(sel_b, m_b, 0.0)
    eye_t = (ii == jj)[None, :, :, None, None]
    eye_c = jnp.eye(c, dtype=F32)[None, None, None]
    m = m + jnp.where(eye_t, eye_c * d_skip.astype(F32).reshape(g, 1, 1, c, 1), 0.0)
    m = jnp.transpose(m, (0, 1, 4, 2, 3)).reshape(g, t * c, t * c)

    e_f = e_pow[0][:, ::-1][:, 1:]
    e_b = e_pow[1][:, :t]
    ws_f = e_f[:, :, :, None] * b_bar[0][:, None]
    ws_b = e_b[:, :, :, None] * b_bar[1][:, None]

    def rows_ic(x):
        return jnp.transpose(x, (0, 1, 3, 2)).reshape(g, t * c, p)

    w_state = jnp.concatenate([rows_ic(jnp.real(ws_f)), rows_ic(jnp.real(ws_b)),
                               rows_ic(jnp.imag(ws_f)), rows_ic(jnp.imag(ws_b))], axis=-1)

    ce_f = c_mat[0][:, None] * e_pow[0][:, 1:, None, :]
    ce_b = c_mat[1][:, None] * e_pow[1][:, ::-1][:, :t, None, :]

    def cols_jc(x):
        return jnp.transpose(x, (0, 3, 1, 2)).reshape(g, p, t * c)

    w_so = jnp.concatenate([cols_jc(jnp.real(ce_f)), cols_jc(jnp.real(ce_b)),
                            cols_jc(-jnp.imag(ce_f)), cols_jc(-jnp.imag(ce_b))], axis=1)
    w_out = jnp.concatenate([m, w_so], axis=1)

    a_t = e_pow[:, :, t]
    a_step = jnp.stack([jnp.concatenate([jnp.real(a_t[0]), jnp.real(a_t[1])], axis=-1),
                        jnp.concatenate([jnp.imag(a_t[0]), jnp.imag(a_t[1])], axis=-1)])
    return w_state.astype(BF16), w_out.astype(BF16), a_step


def _ssm_kernel(n_seq, n_chunk, gb, g_all, x_ref, ws_ref, wo_ref, a_ref, s0_ref,
                y_ref, fin_ref, za_ref, zb_ref):
    phase = pl.program_id(1)
    j = pl.program_id(2)
    gp = g_all + GROUP_PAD
    nck = n_seq * n_chunk
    half = LANES // 2

    @pl.when(phase == 0)
    def _():
        for q in range(gb):
            z = _dot(x_ref[q], ws_ref[q])
            za_ref[pl.ds(j * gb + q, nck, stride=gp), :] = z[:, :LANES]
            zb_ref[pl.ds(j * gb + q, nck, stride=gp), :] = z[:, LANES:]

    @pl.when(jnp.logical_and(phase == 1, j == 0))
    def _():
        a_re = a_ref[0]
        a_im = a_ref[1]
        lane = lax.broadcasted_iota(jnp.int32, (g_all, LANES), 1)
        is_f = lane < half
        for s in range(n_seq):
            def step(k, carry):
                s_a, s_b = carry
                rf = pl.multiple_of((s * n_chunk + k) * gp, SUBLANES)
                rb = pl.multiple_of((s * n_chunk + n_chunk - 1 - k) * gp, SUBLANES)
                zf_a = za_ref[pl.ds(rf, g_all), :]
                zf_b = zb_ref[pl.ds(rf, g_all), :]
                zb_a = za_ref[pl.ds(rb, g_all), :]
                zb_b = zb_ref[pl.ds(rb, g_all), :]
                za_ref[pl.ds(rf, g_all), :] = jnp.where(is_f, s_a, zf_a)
                zb_ref[pl.ds(rf, g_all), :] = jnp.where(is_f, s_b, zf_b)
                za_ref[pl.ds(rb, g_all), :] = jnp.where(is_f, zb_a, s_a)
                zb_ref[pl.ds(rb, g_all), :] = jnp.where(is_f, zb_b, s_b)
                z_a = jnp.where(is_f, zf_a, zb_a)
                z_b = jnp.where(is_f, zf_b, zb_b)
                n_a = a_re * s_a - a_im * s_b + z_a
                n_b = a_re * s_b + a_im * s_a + z_b
                return n_a, n_b

            init = s0_ref[s]
            f_a, f_b = lax.fori_loop(0, n_chunk, step, (init[:, :LANES], init[:, LANES:]))
            fin_ref[s] = jnp.concatenate([f_a, f_b], axis=-1)

    @pl.when(phase == 1)
    def _():
        for q in range(gb):
            st_a = za_ref[pl.ds(j * gb + q, nck, stride=gp), :].astype(BF16)
            st_b = zb_ref[pl.ds(j * gb + q, nck, stride=gp), :].astype(BF16)
            lhs = jnp.concatenate([x_ref[q], st_a, st_b], axis=-1)
            y_ref[q] = _dot(lhs, wo_ref[q]).astype(y_ref.dtype)


def _ssm_call(xg, w_state, w_out, a_step, s0, n_seq, n_chunk, name):
    g_all, nck_total, w = xg.shape
    nck = n_seq * n_chunk
    n_blk = nck_total // nck
    gb = min(g_all, 8)
    ngb = g_all // gb
    gp = g_all + GROUP_PAD
    kern = functools.partial(_ssm_kernel, n_seq, n_chunk, gb, g_all)
    return pl.pallas_call(
        kern,
        out_shape=(jax.ShapeDtypeStruct((g_all, nck_total, w), F32),
                   jax.ShapeDtypeStruct((n_blk * n_seq, g_all, w), F32)),
        grid=(n_blk, 2, ngb),
        in_specs=[
            pl.BlockSpec((gb, nck, w), lambda b, ph, j: (j, b, 0)),
            pl.BlockSpec((gb, w, w), lambda b, ph, j: (jnp.where(ph == 0, j, ngb - 1), 0, 0)),
            pl.BlockSpec((gb, 2 * w, w), lambda b, ph, j: (jnp.where(ph == 0, 0, j), 0, 0)),
            pl.BlockSpec((2, g_all, LANES), lambda b, ph, j: (0, 0, 0)),
            pl.BlockSpec((n_seq, g_all, w), lambda b, ph, j: (b, 0, 0)),
        ],
        out_specs=(
            pl.BlockSpec((gb, nck, w), lambda b, ph, j: (jnp.where(ph == 0, 0, j), b, 0)),
            pl.BlockSpec((n_seq, g_all, w), lambda b, ph, j: (b, 0, 0)),
        ),
        scratch_shapes=[pltpu.VMEM((nck * gp, LANES), F32), pltpu.VMEM((nck * gp, LANES), F32)],
        compiler_params=_cp(("arbitrary", "arbitrary", "arbitrary")),
        name=name,
    )(xg, w_state, w_out, a_step, s0)


def _glu_kernel(y_ref, wg_ref, bg_ref, wb_ref, o_ref):
    y = jax.nn.gelu(y_ref[...].astype(F32))
    z = _dot(y.astype(BF16), wg_ref[...]) + bg_ref[...]
    ya = y * jax.nn.sigmoid(z)
    o_ref[...] = _dot(ya.astype(BF16), wb_ref[...]).astype(o_ref.dtype)


def _glu_branch(y_pre, w_glu, b_glu, w_ba, tm):
    t, c = y_pre.shape
    d = w_ba.shape[1]
    return pl.pallas_call(
        _glu_kernel,
        out_shape=jax.ShapeDtypeStruct((t, d), BF16),
        grid=(t // tm,),
        in_specs=[pl.BlockSpec((tm, c), lambda i: (i, 0)),
                  pl.BlockSpec((c, c), lambda i: (0, 0)),
                  pl.BlockSpec((1, c), lambda i: (0, 0)),
                  pl.BlockSpec((c, d), lambda i: (0, 0))],
        out_specs=pl.BlockSpec((tm, d), lambda i: (i, 0)),
        compiler_params=_cp(("parallel",)),
        name="glu_branch",
    )(y_pre, w_glu, b_glu, w_ba)


def _short_conv_kernel(nct, per_ctx, per_lat, v_ref, x0_ref, x1_ref, vp_ref, x0p_ref, x1p_ref,
                       vn_ref, x0n_ref, x1n_ref, w_ref, b_ref, vx_ref, x0c_ref):
    i = pl.program_id(0)
    tm = v_ref.shape[0]
    k = jnp.where(i < nct, i % per_ctx, (i - nct) % per_lat)
    per = jnp.where(i < nct, per_ctx, per_lat)
    first = k == 0
    last = k == per - 1
    row = lax.broadcasted_iota(jnp.int32, (tm, 1), 0)

    def conv(cur_ref, prev_ref, next_ref, part):
        x = cur_ref[...].astype(F32)
        hp = jnp.where(first, 0.0, prev_ref[SUBLANES - 1:SUBLANES, :].astype(F32))
        hn = jnp.where(last, 0.0, next_ref[0:1, :].astype(F32))
        xm = jnp.where(row == 0, hp, pltpu.roll(x, 1, axis=0))
        xq = jnp.where(row == tm - 1, hn, pltpu.roll(x, tm - 1, axis=0))
        w = w_ref[part]
        return b_ref[part] + xm * w[0:1, :] + x * w[1:2, :] + xq * w[2:3, :]

    v = conv(v_ref, vp_ref, vn_ref, 0)
    x0 = conv(x0_ref, x0p_ref, x0n_ref, 1)
    x1 = conv(x1_ref, x1p_ref, x1n_ref, 2)
    vx_ref[...] = (v * x1).astype(vx_ref.dtype)
    x0c_ref[...] = x0.astype(x0c_ref.dtype)


def _short_conv(proj, col_blk0, hw, w_short3, b_short3, n_ctx, seq, dseq, tm):
    t = proj.shape[0]
    assert seq % tm == 0 and dseq % tm == 0
    nt = t // tm
    r8 = tm // SUBLANES
    nb8 = t // SUBLANES

    def cur(k):
        return pl.BlockSpec((tm, hw), lambda i: (i, col_blk0 + k))

    def prev(k):
        return pl.BlockSpec((SUBLANES, hw), lambda i: (jnp.maximum(i * r8 - 1, 0), col_blk0 + k))

    def nxt(k):
        return pl.BlockSpec((SUBLANES, hw), lambda i: (jnp.minimum((i + 1) * r8, nb8 - 1), col_blk0 + k))

    kern = functools.partial(_short_conv_kernel, n_ctx // tm, seq // tm, dseq // tm)
    return pl.pallas_call(
        kern,
        out_shape=(jax.ShapeDtypeStruct((t, hw), BF16), jax.ShapeDtypeStruct((t, hw), BF16)),
        grid=(nt,),
        in_specs=[cur(0), cur(1), cur(2), prev(0), prev(1), prev(2), nxt(0), nxt(1), nxt(2),
                  pl.BlockSpec((3, 3, hw), lambda i: (0, 0, 0)),
                  pl.BlockSpec((3, 1, hw), lambda i: (0, 0, 0))],
        out_specs=(pl.BlockSpec((tm, hw), lambda i: (i, 0)), pl.BlockSpec((tm, hw), lambda i: (i, 0))),
        compiler_params=_cp(("parallel",)),
        name="short_conv",
    )(proj, proj, proj, proj, proj, proj, proj, proj, proj, w_short3, b_short3)


def _filt_gen_kernel(seq_len, n_bands, wt_ref, wc_ref, ws_ref, b1_ref, f1_ref, w2_ref, b2_ref, f2_ref,
                     w3_ref, b3_ref, rate_ref, a_ref, bm_ref, abs_ref, nyq_ref):
    i = pl.program_id(0)
    tl = a_ref.shape[0]
    hw = a_ref.shape[1]
    pos = lax.broadcasted_iota(jnp.int32, (tl, 1), 0) + i * tl
    t = pos.astype(F32) / seq_len
    bands = (lax.broadcasted_iota(jnp.int32, (1, wc_ref.shape[0]), 1) + 1).astype(F32)
    ang = (2.0 * math.pi) * t * bands
    pre = t * wt_ref[...] + _dot3(jnp.cos(ang), wc_ref[...]) + _dot3(jnp.sin(ang), ws_ref[...])
    h = jnp.sin(f1_ref[...] * (pre + b1_ref[...]))
    h = jnp.sin(f2_ref[...] * (_dot3(h, w2_ref[...]) + b2_ref[...]))
    h = _dot3(h, w3_ref[...]) + b3_ref[...]
    h = h * jnp.exp(-t * rate_ref[...])
    h_f = h[:, :hw]
    h_b = h[:, hw:]
    colabs = jnp.sum(jnp.abs(h_f) + jnp.abs(h_b), axis=0, keepdims=True)
    h_bp = jnp.where(pos == 0, 0.0, h_b)
    a = h_f + h_bp
    sgn = (1 - 2 * (pos % 2)).astype(F32)
    nyq = jnp.sum(a * sgn, axis=0, keepdims=True)
    a_ref[...] = a.astype(a_ref.dtype)
    bm_ref[...] = (h_bp - h_f).astype(bm_ref.dtype)

    @pl.when(i == 0)
    def _():
        abs_ref[...] = colabs
        nyq_ref[...] = nyq

    @pl.when(i > 0)
    def _():
        abs_ref[...] += colabs
        nyq_ref[...] += nyq


def _filt_gen(seq_len, w_f1, b_f1, freq1, w_f2, b_f2, freq2, w_f3, b_f3, rates2):
    n_emb, hid_raw = w_f1.shape
    n_bands = (n_emb - 1) // 2
    hw2 = w_f3.shape[1]
    hw = hw2 // 2
    tl = min(seq_len, 512)
    hid = LANES
    assert n_bands <= LANES and hid_raw <= LANES

    def pad(x, rows, cols):
        return jnp.zeros((rows, cols), F32).at[:x.shape[0], :x.shape[1]].set(x.astype(F32))

    wt = pad(w_f1[0:1], 1, hid)
    wc = pad(w_f1[1:1 + n_bands], LANES, hid)
    ws = pad(w_f1[1 + n_bands:], LANES, hid)
    b_f1, freq1, b_f2, freq2 = (pad(x, 1, hid) for x in (b_f1, freq1, b_f2, freq2))
    w_f2 = pad(w_f2, hid, hid)
    w_f3 = pad(w_f3, hid, hw2)
    full = lambda shape: pl.BlockSpec(shape, lambda i: (0,) * len(shape))
    kern = functools.partial(_filt_gen_kernel, seq_len, n_bands)
    return pl.pallas_call(
        kern,
        out_shape=(jax.ShapeDtypeStruct((seq_len, hw), BF16), jax.ShapeDtypeStruct((seq_len, hw), BF16),
                   jax.ShapeDtypeStruct((1, hw), F32), jax.ShapeDtypeStruct((1, hw), F32)),
        grid=(seq_len // tl,),
        in_specs=[full((1, hid)), full((LANES, hid)), full((LANES, hid)), full((1, hid)), full((1, hid)),
                  full((hid, hid)), full((1, hid)), full((1, hid)), full((hid, hw2)), full((1, hw2)),
                  full((1, hw2))],
        out_specs=(pl.BlockSpec((tl, hw), lambda i: (i, 0)), pl.BlockSpec((tl, hw), lambda i: (i, 0)),
                   full((1, hw)), full((1, hw))),
        compiler_params=_cp(("arbitrary",)),
        name="filt_gen",
    )(wt, wc, ws, b_f1, freq1, w_f2, b_f2, freq2, w_f3, b_f3, rates2)


def _filt_dft_kernel(seq_len, c_ref, s_ref, a_ref, bm_ref, abs_ref, nyq_ref, kr_ref, ki_ref, kn_ref):
    inv = 1.0 / (abs_ref[...] + EPS)
    scale = inv * (1.0 / seq_len)
    kr_ref[...] = _dot(c_ref[...], a_ref[...]) * scale
    ki_ref[...] = _dot(s_ref[...], bm_ref[...]) * scale
    kn_ref[...] = nyq_ref[...] * inv * (0.5 / seq_len)


def _filt_dft(ctab, stab, a, bm, colabs, nyq):
    seq_len, hw = a.shape
    tf = min(seq_len, 512)
    kern = functools.partial(_filt_dft_kernel, seq_len)
    one = pl.Buffered(1)
    return pl.pallas_call(
        kern,
        out_shape=(jax.ShapeDtypeStruct((seq_len, hw), F32), jax.ShapeDtypeStruct((seq_len, hw), F32),
                   jax.ShapeDtypeStruct((1, hw), F32)),
        grid=(seq_len // tf,),
        in_specs=[pl.BlockSpec((tf, seq_len), lambda i: (i, 0)),
                  pl.BlockSpec((tf, seq_len), lambda i: (i, 0)),
                  pl.BlockSpec((seq_len, hw), lambda i: (0, 0), pipeline_mode=one),
                  pl.BlockSpec((seq_len, hw), lambda i: (0, 0), pipeline_mode=one),
                  pl.BlockSpec((1, hw), lambda i: (0, 0)),
                  pl.BlockSpec((1, hw), lambda i: (0, 0))],
        out_specs=(pl.BlockSpec((tf, hw), lambda i: (i, 0)), pl.BlockSpec((tf, hw), lambda i: (i, 0)),
                   pl.BlockSpec((1, hw), lambda i: (0, 0))),
        compiler_params=_cp(("arbitrary",)),
        name="filt_dft",
    )(ctab, stab, a, bm, colabs, nyq)


def _dft_tables(seq_len):
    f = jnp.arange(seq_len, dtype=jnp.int32)
    m = (f[:, None] * f[None, :]) % (2 * seq_len)
    ang = m.astype(F32) * (math.pi / seq_len)
    return jnp.cos(ang).astype(BF16), jnp.sin(ang).astype(BF16)


def _hy_fwd_kernel(c_ref, s_ref, v_ref, kr_ref, ki_ref, kn_ref, yre_ref, yim_ref, nyq_ref):
    j = pl.program_id(1)
    tf = c_ref.shape[0]
    v = v_ref[...]
    p = _dot(c_ref[...], v)
    q = _dot(s_ref[...], v)
    kr = kr_ref[...]
    ki = ki_ref[...]
    row = lax.broadcasted_iota(jnp.int32, (tf, 1), 0) + j * tf
    dc = jnp.where(row == 0, 0.5, 1.0)
    yre_ref[0] = ((p * kr + q * ki) * dc).astype(yre_ref.dtype)
    yim_ref[0] = (p * ki - q * kr).astype(yim_ref.dtype)

    @pl.when(j == 0)
    def _():
        n = v.shape[0]
        sgn = (1 - 2 * (lax.broadcasted_iota(jnp.int32, (SUBLANES, n), 1) % 2)).astype(BF16)
        nyq_ref[0] = _dot(sgn, v)[0:1] * kn_ref[...]


def _hy_fwd(ctab, stab, vx, kr, ki, kn, n_seq, row0_blk):
    seq_len, hw = kr.shape
    tf = min(seq_len, 512)
    one = pl.Buffered(1)
    return pl.pallas_call(
        _hy_fwd_kernel,
        out_shape=(jax.ShapeDtypeStruct((n_seq, seq_len, hw), BF16),
                   jax.ShapeDtypeStruct((n_seq, seq_len, hw), BF16),
                   jax.ShapeDtypeStruct((n_seq, 1, hw), F32)),
        grid=(n_seq, seq_len // tf),
        in_specs=[pl.BlockSpec((tf, seq_len), lambda b, j: (j, 0)),
                  pl.BlockSpec((tf, seq_len), lambda b, j: (j, 0)),
                  pl.BlockSpec((seq_len, hw), lambda b, j: (row0_blk + b, 0), pipeline_mode=one),
                  pl.BlockSpec((tf, hw), lambda b, j: (j, 0)),
                  pl.BlockSpec((tf, hw), lambda b, j: (j, 0)),
                  pl.BlockSpec((1, hw), lambda b, j: (0, 0))],
        out_specs=(pl.BlockSpec((1, tf, hw), lambda b, j: (b, j, 0)),
                   pl.BlockSpec((1, tf, hw), lambda b, j: (b, j, 0)),
                   pl.BlockSpec((1, 1, hw), lambda b, j: (b, 0, 0))),
        compiler_params=_cp(("arbitrary", "arbitrary")),
        name="hy_fwd",
    )(ctab, stab, vx, kr, ki, kn)


def _hy_inv_kernel(c_ref, s_ref, yre_ref, yim_ref, nyq_ref, vx_ref, x0_ref, fb_ref, wb_ref, o_ref):
    j = pl.program_id(1)
    tt = c_ref.shape[0]
    z = _dot(c_ref[...], yre_ref[0]) - _dot(s_ref[...], yim_ref[0])
    row = lax.broadcasted_iota(jnp.int32, (tt, 1), 0) + j * tt
    sgn = (1 - 2 * (row % 2)).astype(F32)
    z = z + sgn * nyq_ref[0] + vx_ref[...].astype(F32) * fb_ref[...]
    yb = x0_ref[...].astype(F32) * z
    o_ref[...] = _dot(yb.astype(BF16), wb_ref[...]).astype(o_ref.dtype)


def _hy_inv(ctab, stab, yre, yim, nyq, vx, x0c, fbias, w_bb, t_out, row0_blk):
    n_seq, seq_len, hw = yre.shape
    d = w_bb.shape[1]
    tt = min(seq_len, 256)
    per = seq_len // tt
    one = pl.Buffered(1)
    return pl.pallas_call(
        _hy_inv_kernel,
        out_shape=jax.ShapeDtypeStruct((t_out, d), BF16),
        grid=(n_seq, per),
        in_specs=[pl.BlockSpec((tt, seq_len), lambda b, j: (j, 0)),
                  pl.BlockSpec((tt, seq_len), lambda b, j: (j, 0)),
                  pl.BlockSpec((1, seq_len, hw), lambda b, j: (b, 0, 0), pipeline_mode=one),
                  pl.BlockSpec((1, seq_len, hw), lambda b, j: (b, 0, 0), pipeline_mode=one),
                  pl.BlockSpec((1, 1, hw), lambda b, j: (b, 0, 0)),
                  pl.BlockSpec((tt, hw), lambda b, j: ((row0_blk + b) * per + j, 0)),
                  pl.BlockSpec((tt, hw), lambda b, j: ((row0_blk + b) * per + j, 0)),
                  pl.BlockSpec((1, hw), lambda b, j: (0, 0)),
                  pl.BlockSpec((hw, d), lambda b, j: (0, 0), pipeline_mode=one)],
        out_specs=pl.BlockSpec((tt, d), lambda b, j: (b * per + j, 0)),
        compiler_params=_cp(("arbitrary", "arbitrary")),
        name="hy_inv",
    )(ctab, stab, yre, yim, nyq, vx, x0c, fbias, w_bb)


def _merge_kernel(tok, n_groups, xp_ref, xs_ref, pos_ref, g0_ref, g1_ref, pa_ref, pbc_ref, pbs_ref,
                  gate1_ref, sh2_ref, sc2_ref, gn_ref, wo_ref, wr_ref, br_ref,
                  x2_ref, h2_ref, rt_ref):
    i = pl.program_id(0)
    is_ctx = i < tok.nct
    pb = jnp.where(is_ctx, pbc_ref[...].astype(F32), pbs_ref[...].astype(F32))
    merged = g0_ref[...].astype(F32) * pa_ref[...].astype(F32) + g1_ref[...].astype(F32) * pb
    att = _dot(merged.astype(BF16), wo_ref[...])
    x = jnp.where(is_ctx, xp_ref[...], xs_ref[...] + pos_ref[...])
    x2 = x + gate1_ref[0] * att
    x2_ref[...] = x2
    h2 = _rms(x2, gn_ref[...]) * (1.0 + sc2_ref[0]) + sh2_ref[0]
    h2b = h2.astype(BF16)
    h2_ref[...] = h2b.astype(h2_ref.dtype)
    logits = _dot(h2b, wr_ref[...]) + br_ref[...]
    lane = lax.broadcasted_iota(jnp.int32, logits.shape, 1)
    gl = jnp.where(lane < n_groups, logits, -jnp.inf)
    gmax = jnp.max(gl, axis=-1, keepdims=True)
    gidx = jnp.min(jnp.where(gl == gmax, lane, n_groups), axis=-1, keepdims=True)
    rt_ref[...] = jnp.where(lane == LANES - 1, gidx.astype(F32), logits)


def _merge_out(xp, xs, pos, proj, gate_blk0, pa, pb_ctx, pb_lat, mod, g2, w_out, w_r, b_r, n_groups, tok):
    d = xp.shape[1]
    tm = tok.tm
    t_all = xp.shape[0] + xs.shape[0]
    kern = functools.partial(_merge_kernel, tok, n_groups)
    one = pl.Buffered(1)

    def modspec(k):
        return pl.BlockSpec((1, 1, d), lambda i: (tok.cond(i) * N_MOD + k, 0, 0))

    return pl.pallas_call(
        kern,
        out_shape=(jax.ShapeDtypeStruct((t_all, d), F32), jax.ShapeDtypeStruct((t_all, d), F32),
                   jax.ShapeDtypeStruct((t_all, LANES), F32)),
        grid=(tok.n,),
        in_specs=[
            pl.BlockSpec((tm, d), lambda i: (tok.ctx_idx(i), 0)),
            pl.BlockSpec((tm, d), lambda i: (tok.lat_idx(i), 0)),
            pl.BlockSpec((tm, d), lambda i: (tok.pos_idx(i), 0)),
            pl.BlockSpec((tm, d), lambda i: (i, gate_blk0)),
            pl.BlockSpec((tm, d), lambda i: (i, gate_blk0 + 1)),
            pl.BlockSpec((tm, d), lambda i: (i, 0)),
            pl.BlockSpec((tm, d), lambda i: (tok.ctx_idx(i), 0)),
            pl.BlockSpec((tm, d), lambda i: (tok.lat_idx(i), 0)),
            modspec(2), modspec(3), modspec(4),
            pl.BlockSpec((1, d), lambda i: (0, 0)),
            pl.BlockSpec((d, d), lambda i: (0, 0), pipeline_mode=one),
            pl.BlockSpec((d, LANES), lambda i: (0, 0)),
            pl.BlockSpec((1, LANES), lambda i: (0, 0)),
        ],
        out_specs=(pl.BlockSpec((tm, d), lambda i: (i, 0)), pl.BlockSpec((tm, d), lambda i: (i, 0)),
                   pl.BlockSpec((tm, LANES), lambda i: (i, 0))),
        compiler_params=_cp(("parallel",)),
        name="merge_out",
    )(xp, xs, pos, proj, proj, pa, pb_ctx, pb_lat, mod, mod, mod, g2, w_out, w_r, b_r)


def _moe_kernel(n_tiles, n_groups, epg, tg_ref, tnv_ref, rt_ref,
                h2_hbm, wr_ref, br_ref, wg_ref, wu_ref, wd_ref, out_hbm,
                xbuf, obuf, acc, gws, gsem, ssem, pend):
    i = pl.program_id(0)
    j = pl.program_id(1)
    tm = obuf.shape[0]
    nv = tnv_ref[i]
    slot = i % 2

    def gather_copy(tok_row, slot_, r):
        return pltpu.make_async_copy(h2_hbm.at[pl.ds(tok_row, 1), :],
                                     xbuf.at[slot_, pl.ds(r, 1), :], gsem.at[slot_])

    def scatter_copy(tok_row, r):
        return pltpu.make_async_copy(obuf.at[pl.ds(r, 1), :],
                                     out_hbm.at[pl.ds(tok_row, 1), :], ssem.at[0])

    def start_gather(tile, slot_):
        def body(r, c):
            gather_copy(rt_ref[tile * tm + r], slot_, r).start()
            return c
        lax.fori_loop(0, tnv_ref[tile], body, 0)

    def wait_gather(tile, slot_):
        def body(r, c):
            gather_copy(0, slot_, r).wait()
            return c
        lax.fori_loop(0, tnv_ref[tile], body, 0)

    def wait_scatter(count):
        def body(r, c):
            scatter_copy(0, r).wait()
            return c
        lax.fori_loop(0, count, body, 0)

    @pl.when(j == 0)
    def _():
        @pl.when(i == 0)
        def _():
            xbuf[...] = jnp.zeros(xbuf.shape, xbuf.dtype)
            pend[0] = 0
            start_gather(0, 0)

        wait_gather(i, slot)

        @pl.when(i + 1 < n_tiles)
        def _():
            start_gather(i + 1, 1 - slot)

        grp = tg_ref[i]
        xb = xbuf[slot].astype(BF16)
        logits = _dot(xb, wr_ref[...]) + br_ref[...]
        lane = lax.broadcasted_iota(jnp.int32, logits.shape, 1)
        gl = jnp.where(lane < n_groups, logits, -jnp.inf)
        gmax = jnp.max(gl, axis=-1, keepdims=True)
        ge = jnp.exp(gl - gmax)
        g_w = jnp.sum(jnp.where(lane == grp, ge, 0.0), axis=-1, keepdims=True) / jnp.sum(ge, axis=-1, keepdims=True)
        base = n_groups + grp * epg
        e = [jnp.sum(jnp.where(lane == base + k, logits, 0.0), axis=-1, keepdims=True) for k in range(epg)]
        sel = []
        for k in range(epg):
            rank = jnp.zeros_like(e[k])
            for m in range(epg):
                if m == k:
                    continue
                ahead = (e[m] > e[k]) if m > k else (e[m] >= e[k])
                rank = rank + ahead.astype(F32)
            sel.append(rank < TOP_K_INNER)
        emax = functools.reduce(jnp.maximum, e)
        ex = [jnp.where(sel[k], jnp.exp(e[k] - emax), 0.0) for k in range(epg)]
        den = functools.reduce(lambda a, b: a + b, ex)
        for k in range(epg):
            gws[k] = ex[k] / den * g_w
        acc[...] = jnp.zeros(acc.shape, acc.dtype)

    @pl.when(nv > 0)
    def _():
        xb = xbuf[slot].astype(BF16)
        g = _dot(xb, wg_ref[...])
        u = _dot(xb, wu_ref[...])
        hid = (g * jax.nn.sigmoid(g)) * u * gws[j]
        acc[...] += _dot(hid.astype(BF16), wd_ref[...])

    @pl.when(j == epg - 1)
    def _():
        wait_scatter(pend[0])
        obuf[...] = acc[...]

        def body(r, c):
            scatter_copy(rt_ref[i * tm + r], r).start()
            return c
        lax.fori_loop(0, nv, body, 0)
        pend[0] = nv

        @pl.when(i == n_tiles - 1)
        def _():
            wait_scatter(nv)


def _moe(h2, w_r, b_r, w_g, w_u, w_d, tile_grp, tile_nv, row_tok, n_groups, epg, tm):
    t, d = h2.shape
    ff = w_g.shape[-1]
    n_tiles = tile_grp.shape[0]
    kern = functools.partial(_moe_kernel, n_tiles, n_groups, epg)

    def widx(i, j, tg, tnv, rt):
        return tg[i] * epg + jnp.where(tnv[i] > 0, j, epg - 1)

    grid_spec = pltpu.PrefetchScalarGridSpec(
        num_scalar_prefetch=3,
        grid=(n_tiles, epg),
        in_specs=[
            pl.BlockSpec(memory_space=pl.ANY),
            pl.BlockSpec((d, LANES), lambda i, j, tg, tnv, rt: (0, 0)),
            pl.BlockSpec((1, LANES), lambda i, j, tg, tnv, rt: (0, 0)),
            pl.BlockSpec((None, d, ff), lambda i, j, tg, tnv, rt: (widx(i, j, tg, tnv, rt), 0, 0)),
            pl.BlockSpec((None, d, ff), lambda i, j, tg, tnv, rt: (widx(i, j, tg, tnv, rt), 0, 0)),
            pl.BlockSpec((None, ff, d), lambda i, j, tg, tnv, rt: (widx(i, j, tg, tnv, rt), 0, 0)),
        ],
        out_specs=pl.BlockSpec(memory_space=pl.ANY),
        scratch_shapes=[
            pltpu.VMEM((2, tm, d), F32),
            pltpu.VMEM((tm, d), F32),
            pltpu.VMEM((tm, d), F32),
            pltpu.VMEM((epg, tm, 1), F32),
            pltpu.SemaphoreType.DMA((2,)),
            pltpu.SemaphoreType.DMA((1,)),
            pltpu.SMEM((1,), jnp.int32),
        ],
    )
    return pl.pallas_call(
        kern,
        out_shape=jax.ShapeDtypeStruct((t, d), F32),
        grid_spec=grid_spec,
        compiler_params=_cp(("arbitrary", "arbitrary")),
        name="moe",
    )(tile_grp, tile_nv, row_tok, h2, w_r, b_r, w_g, w_u, w_d)


def _moe_plan(grp, n_groups, tm, n_tiles):
    t = grp.shape[0]
    onehot = (grp[:, None] == jnp.arange(n_groups, dtype=jnp.int32)[None, :]).astype(jnp.int32)
    csum = jnp.cumsum(onehot, axis=0)
    counts = csum[-1]
    rank = jnp.take_along_axis(csum, grp[:, None], axis=1)[:, 0] - 1
    ntile_g = (counts + tm - 1) // tm
    tile_end = jnp.cumsum(ntile_g)
    tile_off = tile_end - ntile_g
    slot = tile_off[grp] * tm + rank
    row_tok = jnp.zeros((n_tiles * tm,), jnp.int32).at[slot].set(jnp.arange(t, dtype=jnp.int32))
    tiles = jnp.arange(n_tiles, dtype=jnp.int32)
    tile_grp = jnp.minimum(jnp.sum((tiles[:, None] >= tile_end[None, :]).astype(jnp.int32), axis=1),
                           n_groups - 1)
    tile_nv = jnp.clip(counts[tile_grp] - (tiles - tile_off[tile_grp]) * tm, 0, tm)
    tile_nv = jnp.where(tiles < tile_end[-1], tile_nv, 0)
    return tile_grp.astype(jnp.int32), tile_nv.astype(jnp.int32), row_tok


def _final_kernel(x2_ref, m_ref, gate2_ref, gf_ref, o_ref):
    x3 = x2_ref[...] + gate2_ref[0] * m_ref[...]
    o_ref[...] = _rms(x3, gf_ref[...])


def _final(x2, moe, mod, g_final, row_blk0, n_rows, cond_of_tile, tm):
    d = x2.shape[1]
    return pl.pallas_call(
        _final_kernel,
        out_shape=jax.ShapeDtypeStruct((n_rows, d), F32),
        grid=(n_rows // tm,),
        in_specs=[pl.BlockSpec((tm, d), lambda i: (row_blk0 + i, 0)),
                  pl.BlockSpec((tm, d), lambda i: (row_blk0 + i, 0)),
                  pl.BlockSpec((1, 1, d), lambda i: (cond_of_tile(i) * N_MOD + 5, 0, 0)),
                  pl.BlockSpec((1, d), lambda i: (0, 0))],
        out_specs=pl.BlockSpec((tm, d), lambda i: (i, 0)),
        compiler_params=_cp(("parallel",)),
        name="final",
    )(x2, moe, mod, g_final)


def _grid_pos_embed(n_tokens, dim):
    rows_n = n_tokens // GRID_W
    rows = jnp.broadcast_to(jnp.arange(rows_n)[:, None], (rows_n, GRID_W)).reshape(-1)
    cols = jnp.broadcast_to(jnp.arange(GRID_W)[None, :], (rows_n, GRID_W)).reshape(-1)
    quarter = dim // 4
    omega = 1.0 / (10000.0 ** (jnp.arange(quarter, dtype=F32) / quarter))

    def emb(p):
        ang = p.astype(F32)[:, None] * omega[None, :]
        return jnp.concatenate([jnp.sin(ang), jnp.cos(ang)], axis=-1)

    return jnp.concatenate([emb(rows), emb(cols)], axis=-1)


def _to_chunks(u, g_all):
    t, w = u.shape
    c = w // g_all
    x = u.reshape(t // CHUNK, CHUNK, g_all, c)
    return jnp.transpose(x, (2, 0, 1, 3)).reshape(g_all, t // CHUNK, CHUNK * c)


def _from_chunks(y):
    g_all, nck, w = y.shape
    c = w // CHUNK
    x = y.reshape(g_all, nck, CHUNK, c)
    return jnp.transpose(x, (1, 2, 0, 3)).reshape(nck * CHUNK, g_all * c)


def _state_to_lanes(s):
    b, _, _, g, p = s.shape
    return jnp.transpose(s, (0, 3, 2, 1, 4)).reshape(b, g, 4 * p)


def _lanes_to_state(x, p):
    b, g, _ = x.shape
    return jnp.transpose(x.reshape(b, g, 2, 2, p), (0, 3, 2, 1, 4))


def kernel(x_prompt, x_sample, state_ssm, c, c_ctx, w_ada, b_ada, g_norm1, g_norm2, w_in, ssm_a_re, ssm_a_im, ssm_log_dt, ssm_b_re, ssm_b_im, ssm_c_re, ssm_c_im, ssm_d, w_glu, b_glu, w_short, b_short, w_f1, b_f1, freq1, w_f2, b_f2, freq2, w_f3, b_f3, filter_bias, w_branch_a, w_branch_b, w_out, w_router_group, b_router_group, w_router_expert, b_router_expert, w_exp_gate, w_exp_up, w_exp_down, g_final):
    depth = w_ada.shape[0]
    assert depth == 1, "single-layer pipeline"
    bsz, seq, d = x_prompt.shape
    dbsz, dseq, _ = x_sample.shape
    g_all, p_state = ssm_a_re.shape[2:]
    ssm_w = ssm_d.shape[1]
    hw = filter_bias.shape[1]
    n_groups, _, epg = w_router_expert.shape[1:]
    ff = w_exp_gate.shape[-1]
    assert ssm_w // g_all * CHUNK == MXU_SIDE and 4 * p_state == MXU_SIDE
    assert ssm_w == hw and d == 2 * hw
    n_ctx = bsz * seq
    n_lat = dbsz * dseq
    t_all = n_ctx + n_lat

    xp = x_prompt.reshape(n_ctx, d)
    xs = x_sample.reshape(n_lat, d)
    pos = _grid_pos_embed(dseq, d)

    n_cond = 1 + dbsz
    cond8 = jnp.zeros((SUBLANES, d), F32).at[0].set(c_ctx).at[1:n_cond].set(c)
    mod = _ada_mod(cond8, w_ada[0], b_ada[0][None])
    mod = mod[:n_cond].reshape(n_cond * N_MOD, 1, d)

    tm_e = min(256, seq, dseq)
    tok_e = _Tok(n_ctx, n_lat, dseq, tm_e)
    gate_col0 = ssm_w + 3 * hw
    h1 = _norm_mod(xp, xs, pos, mod, g_norm1, tok_e)
    proj = _in_proj(h1, w_in[0], gate_col0, tm=min(1024, t_all), tn=min(1024, hw))

    w_state, w_so, a_step = _ssm_operators(ssm_a_re[0], ssm_a_im[0], ssm_log_dt[0], ssm_b_re[0], ssm_b_im[0],
                                           ssm_c_re[0], ssm_c_im[0], ssm_d[0])
    xg = _to_chunks(proj[:, :ssm_w], g_all)
    nck_ctx = n_ctx // CHUNK
    nc_ctx = seq // CHUNK
    nc_lat = dseq // CHUNK
    seq_per_blk = max(1, min(bsz, MXU_SIDE // nc_ctx))
    s0_ctx = jnp.zeros((bsz, g_all, 4 * p_state), F32)
    y_ctx, fin_ctx = _ssm_call(xg[:, :nck_ctx], w_state, w_so, a_step, s0_ctx, seq_per_blk, nc_ctx, "ssm_ctx")
    s0_lat = _state_to_lanes(state_ssm[:, 0].astype(F32))
    y_lat, _ = _ssm_call(xg[:, nck_ctx:], w_state, w_so, a_step, s0_lat, 1, nc_lat, "ssm_lat")
    y_pre = jnp.concatenate([_from_chunks(y_ctx), _from_chunks(y_lat)], axis=0)
    new_state = _lanes_to_state(fin_ctx, p_state)[:, None]

    pa = _glu_branch(y_pre, w_glu[0].astype(BF16), b_glu[0][None], w_branch_a[0].astype(BF16), tm_e)

    w_short3 = jnp.transpose(w_short[0].reshape(-1, 3, hw), (1, 0, 2))
    b_short3 = b_short[0].reshape(3, 1, hw)
    vx, x0c = _short_conv(proj, ssm_w // hw, hw, w_short3, b_short3, n_ctx, seq, dseq, tm_e)
    rates = jnp.abs(jnp.linspace(math.log(DECAY_TARGET) / DECAY_FAST, math.log(DECAY_TARGET) / DECAY_SLOW,
                                 hw, dtype=F32))
    rates2 = jnp.concatenate([rates, rates])[None]
    fbias = filter_bias[0][None].astype(F32)
    w_bb = w_branch_b[0].astype(BF16)
    pbs = []
    for seq_len, n_seq, row0 in ((seq, bsz, 0), (dseq, dbsz, n_ctx // dseq)):
        ctab, stab = _dft_tables(seq_len)
        a_f, bm_f, colabs, nyq_f = _filt_gen(seq_len, w_f1[0], b_f1[0][None], freq1[0][None], w_f2[0],
                                            b_f2[0][None], freq2[0][None], w_f3[0], b_f3[0][None], rates2)
        kr, ki, kn = _filt_dft(ctab, stab, a_f, bm_f, colabs, nyq_f)
        yre, yim, nyq = _hy_fwd(ctab, stab, vx, kr, ki, kn, n_seq, row0)
        pbs.append(_hy_inv(ctab, stab, yre, yim, nyq, vx, x0c, fbias, w_bb, n_seq * seq_len, row0))
    pb_ctx, pb_lat = pbs

    w_r = jnp.zeros((d, LANES), F32)
    w_r = w_r.at[:, :n_groups].set(w_router_group[0])
    w_r = w_r.at[:, n_groups:n_groups + n_groups * epg].set(
        jnp.transpose(w_router_expert[0], (1, 0, 2)).reshape(d, n_groups * epg))
    b_r = jnp.zeros((1, LANES), F32)
    b_r = b_r.at[0, :n_groups].set(b_router_group[0])
    b_r = b_r.at[0, n_groups:n_groups + n_groups * epg].set(b_router_expert[0].reshape(-1))
    w_r = w_r.astype(BF16)
    x2, h2, rout = _merge_out(xp, xs, pos, proj, gate_col0 // d, pa, pb_ctx, pb_lat, mod, g_norm2,
                              w_out[0].astype(BF16), w_r, b_r, n_groups, tok_e)

    tm_m = min(512, t_all // n_groups)
    n_tiles = t_all // tm_m + n_groups
    grp = rout[:, LANES - 1].astype(jnp.int32)
    tile_grp, tile_nv, row_tok = _moe_plan(grp, n_groups, tm_m, n_tiles)
    moe = _moe(h2, w_r, b_r, w_exp_gate[0].astype(BF16), w_exp_up[0].astype(BF16),
               w_exp_down[0].astype(BF16), tile_grp, tile_nv, row_tok, n_groups, epg, tm_m)

    gf = g_final[None]
    y_prompt = _final(x2, moe, mod, gf, 0, n_ctx, lambda i: 0, tm_e)
    lat_tiles = dseq // tm_e
    y_sample = _final(x2, moe, mod, gf, n_ctx // tm_e, n_lat, lambda i: 1 + i // lat_tiles, tm_e)
    return (y_prompt.reshape(bsz, seq, d), y_sample.reshape(dbsz, dseq, d), new_state)
```

```python
import functools
import math

import numpy as np
import jax
import jax.numpy as jnp
from jax import lax
from jax.experimental import pallas as pl
from jax.experimental.pallas import tpu as pltpu

F32 = jnp.float32
BF16 = jnp.bfloat16
EPS = 1e-6
GRID_W = 64
N_MOD = 6
TOP_K_INNER = 2
DECAY_TARGET = 1e-2
DECAY_FAST = 0.3
DECAY_SLOW = 1.5
LANES = 128
SUBLANES = 8
MXU_SIDE = 256
VMEM_LIMIT = 56 * 1024 * 1024


def _cp(sem, vmem=VMEM_LIMIT):
    return pltpu.CompilerParams(dimension_semantics=sem, vmem_limit_bytes=vmem)


def _dot(a, b):
    return jnp.dot(a, b, preferred_element_type=F32)


def _split(a):
    hi = a.astype(BF16)
    lo = (a - hi.astype(F32)).astype(BF16)
    return hi, lo


def _dot3(a, b):
    a_hi, a_lo = _split(a)
    b_hi, b_lo = _split(b)
    return _dot(a_hi, b_hi) + _dot(a_lo, b_hi) + _dot(a_hi, b_lo)


def _rms(x, g):
    ms = jnp.mean(x * x, axis=-1, keepdims=True)
    return x * lax.rsqrt(ms + EPS) * g


def _ada_kernel(c_ref, w_ref, b_ref, o_ref):
    c = c_ref[...]
    a = c * jax.nn.sigmoid(c)
    o_ref[...] = _dot3(a, w_ref[...]) + b_ref[...]


def _ada_mod(cond8, w_ada, b_ada):
    d, n = w_ada.shape
    tn = min(n, 1024)
    return pl.pallas_call(
        _ada_kernel,
        out_shape=jax.ShapeDtypeStruct((SUBLANES, n), F32),
        grid=(n // tn,),
        in_specs=[pl.BlockSpec((SUBLANES, d), lambda j: (0, 0)),
                  pl.BlockSpec((d, tn), lambda j: (0, j)),
                  pl.BlockSpec((1, tn), lambda j: (0, j))],
        out_specs=pl.BlockSpec((SUBLANES, tn), lambda j: (0, j)),
        compiler_params=_cp(("arbitrary",)),
        name="ada_mod",
    )(cond8, w_ada, b_ada)


class _Tok:
    def __init__(self, n_ctx, n_lat, lat_len, tm):
        assert n_ctx % tm == 0 and lat_len % tm == 0 and n_lat % lat_len == 0
        self.tm = tm
        self.nct = n_ctx // tm
        self.nst = n_lat // tm
        self.per_seq = lat_len // tm
        self.n = self.nct + self.nst

    def ctx_idx(self, i):
        return jnp.minimum(i, self.nct - 1)

    def lat_idx(self, i):
        return jnp.maximum(i - self.nct, 0)

    def pos_idx(self, i):
        return self.lat_idx(i) % self.per_seq

    def cond(self, i):
        return jnp.where(i < self.nct, 0, 1 + self.lat_idx(i) // self.per_seq)


def _norm_mod_kernel(tok, xp_ref, xs_ref, pos_ref, sh_ref, sc_ref, g_ref, h_ref):
    i = pl.program_id(0)

    def modulate(x):
        y = _rms(x, g_ref[...])
        return (y * (1.0 + sc_ref[0]) + sh_ref[0]).astype(h_ref.dtype)

    @pl.when(i < tok.nct)
    def _():
        h_ref[...] = modulate(xp_ref[...])

    @pl.when(i >= tok.nct)
    def _():
        h_ref[...] = modulate(xs_ref[...] + pos_ref[...])


def _norm_mod(xp, xs, pos, mod, g1, tok):
    d = xp.shape[1]
    tm = tok.tm
    t_all = xp.shape[0] + xs.shape[0]
    return pl.pallas_call(
        functools.partial(_norm_mod_kernel, tok),
        out_shape=jax.ShapeDtypeStruct((t_all, d), BF16),
        grid=(tok.n,),
        in_specs=[
            pl.BlockSpec((tm, d), lambda i: (tok.ctx_idx(i), 0)),
            pl.BlockSpec((tm, d), lambda i: (tok.lat_idx(i), 0)),
            pl.BlockSpec((tm, d), lambda i: (tok.pos_idx(i), 0)),
            pl.BlockSpec((1, 1, d), lambda i: (tok.cond(i) * N_MOD + 0, 0, 0)),
            pl.BlockSpec((1, 1, d), lambda i: (tok.cond(i) * N_MOD + 1, 0, 0)),
            pl.BlockSpec((1, d), lambda i: (0, 0)),
        ],
        out_specs=pl.BlockSpec((tm, d), lambda i: (i, 0)),
        compiler_params=_cp(("parallel",)),
        name="norm_mod",
    )(xp, xs, pos, mod, mod, g1)


def _in_proj_kernel(n_sig, h_ref, w_ref, o_ref, wb_ref):
    n = pl.program_id(0)
    m = pl.program_id(1)

    @pl.when(m == 0)
    def _():
        wb_ref[...] = w_ref[...].astype(BF16)

    acc = _dot(h_ref[...], wb_ref[...])

    @pl.when(n >= n_sig)
    def _():
        o_ref[...] = acc.astype(o_ref.dtype)

    @pl.when(n < n_sig)
    def _():
        o_ref[...] = jax.nn.sigmoid(acc).astype(o_ref.dtype)


def _in_proj(h, w_in, sig_cols, lin_cols, out_dtype, tm, tn, name):
    t_all, d = h.shape
    for c in sig_cols + lin_cols:
        assert c % tn == 0
    assert t_all % tm == 0
    n_sig = (sig_cols[1] - sig_cols[0]) // tn
    n_lin = (lin_cols[1] - lin_cols[0]) // tn
    sig0, lin0 = sig_cols[0] // tn, lin_cols[0] // tn

    def w_blk(n):
        return jnp.where(n < n_sig, sig0 + n, lin0 + n - n_sig)

    return pl.pallas_call(
        functools.partial(_in_proj_kernel, n_sig),
        out_shape=jax.ShapeDtypeStruct((t_all, (n_sig + n_lin) * tn), out_dtype),
        grid=(n_sig + n_lin, t_all // tm),
        in_specs=[pl.BlockSpec((tm, d), lambda n, m: (m, 0)),
                  pl.BlockSpec((d, tn), lambda n, m: (0, w_blk(n)))],
        out_specs=pl.BlockSpec((tm, tn), lambda n, m: (m, n)),
        scratch_shapes=[pltpu.VMEM((d, tn), BF16)],
        compiler_params=_cp(("arbitrary", "arbitrary")),
        name=name,
    )(h, w_in)


CHUNK = 16
GROUP_PAD = 8


def _ssm_operators(a_re, a_im, log_dt, b_re, b_im, c_re, c_im, d_skip):
    t = CHUNK
    g, p = a_re.shape[1:]
    c = b_re.shape[-1]
    lam = lax.complex(a_re.astype(F32), a_im.astype(F32))
    dt = jnp.exp(log_dt.astype(F32))[..., None]
    ld = lam * dt
    lam_bar = jnp.exp(ld)
    b_bar = ((lam_bar - 1.0) / lam)[..., None] * lax.complex(b_re.astype(F32), b_im.astype(F32))
    c_mat = lax.complex(c_re.astype(F32), c_im.astype(F32))
    taus = jnp.arange(t + 1, dtype=F32)
    e_pow = jnp.exp(taus[None, None, :, None] * ld[:, :, None, :])

    k_all = jnp.real(jnp.einsum('dgop,dgtp,dgpi->dgtoi', c_mat, e_pow[:, :, :t], b_bar))
    kf, kb = k_all[0], k_all[1]
    ii = np.arange(t)[None, :, None]
    jj = np.arange(t)[None, None, :]
    tt = np.arange(t)[:, None, None]
    place_f = jnp.asarray((jj - ii == tt).astype(np.float32))
    place_b = jnp.asarray((ii - jj == tt).astype(np.float32))
    kf = kf.at[:, 0].add(jnp.eye(c, dtype=F32)[None] * d_skip.astype(F32).reshape(g, c, 1))
    m = jnp.einsum('tij,gtoc->gicjo', place_f, kf) + jnp.einsum('tij,gtoc->gicjo', place_b, kb)
    m = m.reshape(g, t * c, t * c)

    e_f = e_pow[0][:, ::-1][:, 1:]
    e_b = e_pow[1][:, :t]
    ws_f = e_f[:, :, :, None] * b_bar[0][:, None]
    ws_b = e_b[:, :, :, None] * b_bar[1][:, None]

    def rows_ic(x):
        return jnp.transpose(x, (0, 1, 3, 2)).reshape(g, t * c, p)

    w_state = jnp.concatenate([rows_ic(jnp.real(ws_f)), rows_ic(jnp.real(ws_b)),
                               rows_ic(jnp.imag(ws_f)), rows_ic(jnp.imag(ws_b))], axis=-1)

    ce_f = c_mat[0][:, None] * e_pow[0][:, 1:, None, :]
    ce_b = c_mat[1][:, None] * e_pow[1][:, ::-1][:, :t, None, :]

    def cols_jc(x):
        return jnp.transpose(x, (0, 3, 1, 2)).reshape(g, p, t * c)

    w_so = jnp.concatenate([cols_jc(jnp.real(ce_f)), cols_jc(jnp.real(ce_b)),
                            cols_jc(-jnp.imag(ce_f)), cols_jc(-jnp.imag(ce_b))], axis=1)
    w_out = jnp.concatenate([m, w_so], axis=1)

    a_t = e_pow[:, :, t]
    a_step = jnp.stack([jnp.concatenate([jnp.real(a_t[0]), jnp.real(a_t[1])], axis=-1),
                        jnp.concatenate([jnp.imag(a_t[0]), jnp.imag(a_t[1])], axis=-1)])
    return w_state.astype(BF16), w_out.astype(BF16), a_step


def _ssm_kernel(n_seq, n_chunk, gb, g_all, x_ref, ws_ref, wo_ref, a_ref, s0_ref,
                y_ref, fin_ref, xq_ref, za_ref, zb_ref):
    phase = pl.program_id(1)
    j = pl.program_id(2)
    gp = g_all + GROUP_PAD
    nck = n_seq * n_chunk
    half = LANES // 2
    c = LANES // gb

    @pl.when(phase == 0)
    def _():
        xs = [x_ref[pl.ds(i, nck, stride=CHUNK), :] for i in range(CHUNK)]
        for q in range(gb):
            xq = jnp.concatenate([xs[i][:, q * c:(q + 1) * c] for i in range(CHUNK)], axis=-1).astype(BF16)
            xq_ref[j * gb + q] = xq
            z = _dot(xq, ws_ref[q])
            za_ref[pl.ds(j * gb + q, nck, stride=gp), :] = z[:, :LANES]
            zb_ref[pl.ds(j * gb + q, nck, stride=gp), :] = z[:, LANES:]

    @pl.when(jnp.logical_and(phase == 1, j == 0))
    def _():
        a_re = a_ref[0]
        a_im = a_ref[1]
        lane = lax.broadcasted_iota(jnp.int32, (g_all, LANES), 1)
        is_f = lane < half
        for s in range(n_seq):
            def step(k, carry):
                s_a, s_b = carry
                rf = pl.multiple_of((s * n_chunk + k) * gp, SUBLANES)
                rb = pl.multiple_of((s * n_chunk + n_chunk - 1 - k) * gp, SUBLANES)
                zf_a = za_ref[pl.ds(rf, g_all), :]
                zf_b = zb_ref[pl.ds(rf, g_all), :]
                zb_a = za_ref[pl.ds(rb, g_all), :]
                zb_b = zb_ref[pl.ds(rb, g_all), :]
                za_ref[pl.ds(rf, g_all), :] = jnp.where(is_f, s_a, zf_a)
                zb_ref[pl.ds(rf, g_all), :] = jnp.where(is_f, s_b, zf_b)
                za_ref[pl.ds(rb, g_all), :] = jnp.where(is_f, zb_a, s_a)
                zb_ref[pl.ds(rb, g_all), :] = jnp.where(is_f, zb_b, s_b)
                z_a = jnp.where(is_f, zf_a, zb_a)
                z_b = jnp.where(is_f, zf_b, zb_b)
                n_a = a_re * s_a - a_im * s_b + z_a
                n_b = a_re * s_b + a_im * s_a + z_b
                return n_a, n_b

            init = s0_ref[s]
            f_a, f_b = lax.fori_loop(0, n_chunk, step, (init[:, :LANES], init[:, LANES:]))
            fin_ref[s] = jnp.concatenate([f_a, f_b], axis=-1)

    @pl.when(phase == 1)
    def _():
        ys = []
        for q in range(gb):
            st_a = za_ref[pl.ds(j * gb + q, nck, stride=gp), :].astype(BF16)
            st_b = zb_ref[pl.ds(j * gb + q, nck, stride=gp), :].astype(BF16)
            lhs = jnp.concatenate([xq_ref[j * gb + q], st_a, st_b], axis=-1)
            ys.append(_dot(lhs, wo_ref[q]))
        for i in range(CHUNK):
            yi = jnp.concatenate([ys[q][:, i * c:(i + 1) * c] for q in range(gb)], axis=-1)
            y_ref[pl.ds(i, nck, stride=CHUNK), :] = yi


def _ssm_call(u, row_blk0, n_rows, w_state, w_out, a_step, s0, n_seq, n_chunk, name):
    g_all, w, _ = w_state.shape
    nck = n_seq * n_chunk
    blk = nck * CHUNK
    n_blk = n_rows // blk
    gb = LANES * CHUNK // w
    ngb = g_all // gb
    gp = g_all + GROUP_PAD
    kern = functools.partial(_ssm_kernel, n_seq, n_chunk, gb, g_all)
    return pl.pallas_call(
        kern,
        out_shape=(jax.ShapeDtypeStruct((n_rows, u.shape[1]), F32),
                   jax.ShapeDtypeStruct((n_blk * n_seq, g_all, w), F32)),
        grid=(n_blk, 2, ngb),
        in_specs=[
            pl.BlockSpec((blk, LANES), lambda b, ph, j: (row_blk0 + b, jnp.where(ph == 0, j, ngb - 1))),
            pl.BlockSpec((gb, w, w), lambda b, ph, j: (jnp.where(ph == 0, j, ngb - 1), 0, 0)),
            pl.BlockSpec((gb, 2 * w, w), lambda b, ph, j: (jnp.where(ph == 0, 0, j), 0, 0)),
            pl.BlockSpec((2, g_all, LANES), lambda b, ph, j: (0, 0, 0)),
            pl.BlockSpec((n_seq, g_all, w), lambda b, ph, j: (b, 0, 0)),
        ],
        out_specs=(
            pl.BlockSpec((blk, LANES), lambda b, ph, j: (b, jnp.where(ph == 0, 0, j))),
            pl.BlockSpec((n_seq, g_all, w), lambda b, ph, j: (b, 0, 0)),
        ),
        scratch_shapes=[pltpu.VMEM((g_all, nck, w), BF16),
                        pltpu.VMEM((nck * gp, LANES), F32), pltpu.VMEM((nck * gp, LANES), F32)],
        compiler_params=_cp(("arbitrary", "arbitrary", "arbitrary")),
        name=name,
    )(u, w_state, w_out, a_step, s0)


def _glu_kernel(nct, yc_ref, yl_ref, wg_ref, bg_ref, wb_ref, o_ref):
    i = pl.program_id(0)
    y = jax.nn.gelu(jnp.where(i < nct, yc_ref[...], yl_ref[...]))
    z = _dot(y.astype(BF16), wg_ref[...]) + bg_ref[...]
    ya = y * jax.nn.sigmoid(z)
    o_ref[...] = _dot(ya.astype(BF16), wb_ref[...]).astype(o_ref.dtype)


def _glu_branch(y_ctx, y_lat, w_glu, b_glu, w_ba, tok):
    c = y_ctx.shape[1]
    d = w_ba.shape[1]
    tm = tok.tm
    return pl.pallas_call(
        functools.partial(_glu_kernel, tok.nct),
        out_shape=jax.ShapeDtypeStruct((y_ctx.shape[0] + y_lat.shape[0], d), BF16),
        grid=(tok.n,),
        in_specs=[pl.BlockSpec((tm, c), lambda i: (tok.ctx_idx(i), 0)),
                  pl.BlockSpec((tm, c), lambda i: (tok.lat_idx(i), 0)),
                  pl.BlockSpec((c, c), lambda i: (0, 0)),
                  pl.BlockSpec((1, c), lambda i: (0, 0)),
                  pl.BlockSpec((c, d), lambda i: (0, 0))],
        out_specs=pl.BlockSpec((tm, d), lambda i: (i, 0)),
        compiler_params=_cp(("parallel",)),
        name="glu_branch",
    )(y_ctx, y_lat, w_glu, b_glu, w_ba)


def _short_conv_kernel(nct, per_ctx, per_lat, v_ref, x0_ref, x1_ref, vp_ref, x0p_ref, x1p_ref,
                       vn_ref, x0n_ref, x1n_ref, w_ref, b_ref, vx_ref, x0c_ref):
    i = pl.program_id(0)
    tm = v_ref.shape[0]
    k = jnp.where(i < nct, i % per_ctx, (i - nct) % per_lat)
    per = jnp.where(i < nct, per_ctx, per_lat)
    first = k == 0
    last = k == per - 1
    row = lax.broadcasted_iota(jnp.int32, (tm, 1), 0)

    def conv(cur_ref, prev_ref, next_ref, part):
        x = cur_ref[...].astype(F32)
        hp = jnp.where(first, 0.0, prev_ref[SUBLANES - 1:SUBLANES, :].astype(F32))
        hn = jnp.where(last, 0.0, next_ref[0:1, :].astype(F32))
        xm = jnp.where(row == 0, hp, pltpu.roll(x, 1, axis=0))
        xq = jnp.where(row == tm - 1, hn, pltpu.roll(x, tm - 1, axis=0))
        w = w_ref[part]
        return b_ref[part] + xm * w[0:1, :] + x * w[1:2, :] + xq * w[2:3, :]

    v = conv(v_ref, vp_ref, vn_ref, 0)
    x0 = conv(x0_ref, x0p_ref, x0n_ref, 1)
    x1 = conv(x1_ref, x1p_ref, x1n_ref, 2)
    vx_ref[...] = (v * x1).astype(vx_ref.dtype)
    x0c_ref[...] = x0.astype(x0c_ref.dtype)


def _short_conv(proj, col_blk0, hw, w_short3, b_short3, n_ctx, seq, dseq, tm):
    t = proj.shape[0]
    assert seq % tm == 0 and dseq % tm == 0
    nt = t // tm
    r8 = tm // SUBLANES
    nb8 = t // SUBLANES

    def cur(k):
        return pl.BlockSpec((tm, hw), lambda i: (i, col_blk0 + k))

    def prev(k):
        return pl.BlockSpec((SUBLANES, hw), lambda i: (jnp.maximum(i * r8 - 1, 0), col_blk0 + k))

    def nxt(k):
        return pl.BlockSpec((SUBLANES, hw), lambda i: (jnp.minimum((i + 1) * r8, nb8 - 1), col_blk0 + k))

    kern = functools.partial(_short_conv_kernel, n_ctx // tm, seq // tm, dseq // tm)
    return pl.pallas_call(
        kern,
        out_shape=(jax.ShapeDtypeStruct((t, hw), BF16), jax.ShapeDtypeStruct((t, hw), BF16)),
        grid=(nt,),
        in_specs=[cur(0), cur(1), cur(2), prev(0), prev(1), prev(2), nxt(0), nxt(1), nxt(2),
                  pl.BlockSpec((3, 3, hw), lambda i: (0, 0, 0)),
                  pl.BlockSpec((3, 1, hw), lambda i: (0, 0, 0))],
        out_specs=(pl.BlockSpec((tm, hw), lambda i: (i, 0)), pl.BlockSpec((tm, hw), lambda i: (i, 0))),
        compiler_params=_cp(("parallel",)),
        name="short_conv",
    )(proj, proj, proj, proj, proj, proj, proj, proj, proj, w_short3, b_short3)


def _filt_gen_kernel(seq_len, n_bands, wt_ref, wc_ref, ws_ref, b1_ref, f1_ref, w2_ref, b2_ref, f2_ref,
                     w3_ref, b3_ref, rate_ref, a_ref, bm_ref, abs_ref, nyq_ref):
    i = pl.program_id(0)
    tl = a_ref.shape[0]
    hw = a_ref.shape[1]
    pos = lax.broadcasted_iota(jnp.int32, (tl, 1), 0) + i * tl
    t = pos.astype(F32) / seq_len
    bands = (lax.broadcasted_iota(jnp.int32, (1, wc_ref.shape[0]), 1) + 1).astype(F32)
    ang = (2.0 * math.pi) * t * bands
    pre = t * wt_ref[...] + _dot3(jnp.cos(ang), wc_ref[...]) + _dot3(jnp.sin(ang), ws_ref[...])
    h = jnp.sin(f1_ref[...] * (pre + b1_ref[...]))
    h = jnp.sin(f2_ref[...] * (_dot3(h, w2_ref[...]) + b2_ref[...]))
    h = _dot3(h, w3_ref[...]) + b3_ref[...]
    h = h * jnp.exp(-t * rate_ref[...])
    h_f = h[:, :hw]
    h_b = h[:, hw:]
    colabs = jnp.sum(jnp.abs(h_f) + jnp.abs(h_b), axis=0, keepdims=True)
    h_bp = jnp.where(pos == 0, 0.0, h_b)
    a = h_f + h_bp
    sgn = (1 - 2 * (pos % 2)).astype(F32)
    nyq = jnp.sum(a * sgn, axis=0, keepdims=True)
    a_ref[...] = a.astype(a_ref.dtype)
    bm_ref[...] = (h_bp - h_f).astype(bm_ref.dtype)

    @pl.when(i == 0)
    def _():
        abs_ref[...] = colabs
        nyq_ref[...] = nyq

    @pl.when(i > 0)
    def _():
        abs_ref[...] += colabs
        nyq_ref[...] += nyq


def _filt_gen(seq_len, w_f1, b_f1, freq1, w_f2, b_f2, freq2, w_f3, b_f3, rates2):
    n_emb, hid_raw = w_f1.shape
    n_bands = (n_emb - 1) // 2
    hw2 = w_f3.shape[1]
    hw = hw2 // 2
    tl = min(seq_len, 512)
    hid = LANES
    assert n_bands <= LANES and hid_raw <= LANES

    def pad(x, rows, cols):
        return jnp.zeros((rows, cols), F32).at[:x.shape[0], :x.shape[1]].set(x.astype(F32))

    wt = pad(w_f1[0:1], 1, hid)
    wc = pad(w_f1[1:1 + n_bands], LANES, hid)
    ws = pad(w_f1[1 + n_bands:], LANES, hid)
    b_f1, freq1, b_f2, freq2 = (pad(x, 1, hid) for x in (b_f1, freq1, b_f2, freq2))
    w_f2 = pad(w_f2, hid, hid)
    w_f3 = pad(w_f3, hid, hw2)
    full = lambda shape: pl.BlockSpec(shape, lambda i: (0,) * len(shape))
    kern = functools.partial(_filt_gen_kernel, seq_len, n_bands)
    return pl.pallas_call(
        kern,
        out_shape=(jax.ShapeDtypeStruct((seq_len, hw), BF16), jax.ShapeDtypeStruct((seq_len, hw), BF16),
                   jax.ShapeDtypeStruct((1, hw), F32), jax.ShapeDtypeStruct((1, hw), F32)),
        grid=(seq_len // tl,),
        in_specs=[full((1, hid)), full((LANES, hid)), full((LANES, hid)), full((1, hid)), full((1, hid)),
                  full((hid, hid)), full((1, hid)), full((1, hid)), full((hid, hw2)), full((1, hw2)),
                  full((1, hw2))],
        out_specs=(pl.BlockSpec((tl, hw), lambda i: (i, 0)), pl.BlockSpec((tl, hw), lambda i: (i, 0)),
                   full((1, hw)), full((1, hw))),
        compiler_params=_cp(("arbitrary",)),
        name="filt_gen",
    )(wt, wc, ws, b_f1, freq1, w_f2, b_f2, freq2, w_f3, b_f3, rates2)


def _filt_dft_kernel(seq_len, c_ref, s_ref, a_ref, bm_ref, abs_ref, nyq_ref, kr_ref, ki_ref, kn_ref):
    inv = 1.0 / (abs_ref[...] + EPS)
    scale = inv * (1.0 / seq_len)
    kr_ref[...] = _dot(c_ref[...], a_ref[...]) * scale
    ki_ref[...] = _dot(s_ref[...], bm_ref[...]) * scale
    kn_ref[...] = nyq_ref[...] * inv * (0.5 / seq_len)


def _filt_dft(ctab, stab, a, bm, colabs, nyq):
    seq_len, hw = a.shape
    tf = min(seq_len, 512)
    kern = functools.partial(_filt_dft_kernel, seq_len)
    one = pl.Buffered(1)
    return pl.pallas_call(
        kern,
        out_shape=(jax.ShapeDtypeStruct((seq_len, hw), F32), jax.ShapeDtypeStruct((seq_len, hw), F32),
                   jax.ShapeDtypeStruct((1, hw), F32)),
        grid=(seq_len // tf,),
        in_specs=[pl.BlockSpec((tf, seq_len), lambda i: (i, 0)),
                  pl.BlockSpec((tf, seq_len), lambda i: (i, 0)),
                  pl.BlockSpec((seq_len, hw), lambda i: (0, 0), pipeline_mode=one),
                  pl.BlockSpec((seq_len, hw), lambda i: (0, 0), pipeline_mode=one),
                  pl.BlockSpec((1, hw), lambda i: (0, 0)),
                  pl.BlockSpec((1, hw), lambda i: (0, 0))],
        out_specs=(pl.BlockSpec((tf, hw), lambda i: (i, 0)), pl.BlockSpec((tf, hw), lambda i: (i, 0)),
                   pl.BlockSpec((1, hw), lambda i: (0, 0))),
        compiler_params=_cp(("arbitrary",)),
        name="filt_dft",
    )(ctab, stab, a, bm, colabs, nyq)


def _dft_tables(seq_len):
    r = 1
    while r * r < seq_len:
        r *= 2
    t = jnp.arange(seq_len, dtype=jnp.int32)[None, :]
    lo = jnp.arange(r, dtype=jnp.int32)[:, None]
    hi = jnp.arange(seq_len // r, dtype=jnp.int32)[:, None] * r
    ang_lo = ((lo * t) % (2 * seq_len)).astype(F32) * (math.pi / seq_len)
    ang_hi = ((hi * t) % (2 * seq_len)).astype(F32) * (math.pi / seq_len)
    c_lo, s_lo = jnp.cos(ang_lo)[None], jnp.sin(ang_lo)[None]
    c_hi, s_hi = jnp.cos(ang_hi)[:, None], jnp.sin(ang_hi)[:, None]
    ctab = (c_hi * c_lo - s_hi * s_lo).reshape(seq_len, seq_len)
    stab = (s_hi * c_lo + c_hi * s_lo).reshape(seq_len, seq_len)
    return ctab.astype(BF16), stab.astype(BF16)


def _hy_fwd_kernel(c_ref, s_ref, v_ref, kr_ref, ki_ref, kn_ref, yre_ref, yim_ref, nyq_ref):
    j = pl.program_id(1)
    tf = c_ref.shape[0]
    v = v_ref[...]
    p = _dot(c_ref[...], v)
    q = _dot(s_ref[...], v)
    kr = kr_ref[...]
    ki = ki_ref[...]
    row = lax.broadcasted_iota(jnp.int32, (tf, 1), 0) + j * tf
    dc = jnp.where(row == 0, 0.5, 1.0)
    yre_ref[0] = ((p * kr + q * ki) * dc).astype(yre_ref.dtype)
    yim_ref[0] = (p * ki - q * kr).astype(yim_ref.dtype)

    @pl.when(j == 0)
    def _():
        n = v.shape[0]
        sgn = (1 - 2 * (lax.broadcasted_iota(jnp.int32, (SUBLANES, n), 1) % 2)).astype(BF16)
        nyq_ref[0] = _dot(sgn, v)[0:1] * kn_ref[...]


def _hy_fwd(ctab, stab, vx, kr, ki, kn, n_seq, row0_blk):
    seq_len, hw = kr.shape
    tf = min(seq_len, 512)
    one = pl.Buffered(1)
    return pl.pallas_call(
        _hy_fwd_kernel,
        out_shape=(jax.ShapeDtypeStruct((n_seq, seq_len, hw), BF16),
                   jax.ShapeDtypeStruct((n_seq, seq_len, hw), BF16),
                   jax.ShapeDtypeStruct((n_seq, 1, hw), F32)),
        grid=(n_seq, seq_len // tf),
        in_specs=[pl.BlockSpec((tf, seq_len), lambda b, j: (j, 0)),
                  pl.BlockSpec((tf, seq_len), lambda b, j: (j, 0)),
                  pl.BlockSpec((seq_len, hw), lambda b, j: (row0_blk + b, 0), pipeline_mode=one),
                  pl.BlockSpec((tf, hw), lambda b, j: (j, 0)),
                  pl.BlockSpec((tf, hw), lambda b, j: (j, 0)),
                  pl.BlockSpec((1, hw), lambda b, j: (0, 0))],
        out_specs=(pl.BlockSpec((1, tf, hw), lambda b, j: (b, j, 0)),
                   pl.BlockSpec((1, tf, hw), lambda b, j: (b, j, 0)),
                   pl.BlockSpec((1, 1, hw), lambda b, j: (b, 0, 0))),
        compiler_params=_cp(("arbitrary", "arbitrary")),
        name="hy_fwd",
    )(ctab, stab, vx, kr, ki, kn)


def _hy_inv_kernel(c_ref, s_ref, yre_ref, yim_ref, nyq_ref, vx_ref, x0_ref, fb_ref, wb_ref, o_ref):
    j = pl.program_id(1)
    tt = c_ref.shape[0]
    z = _dot(c_ref[...], yre_ref[0]) - _dot(s_ref[...], yim_ref[0])
    row = lax.broadcasted_iota(jnp.int32, (tt, 1), 0) + j * tt
    sgn = (1 - 2 * (row % 2)).astype(F32)
    z = z + sgn * nyq_ref[0] + vx_ref[...].astype(F32) * fb_ref[...]
    yb = x0_ref[...].astype(F32) * z
    o_ref[...] = _dot(yb.astype(BF16), wb_ref[...]).astype(o_ref.dtype)


def _hy_inv(ctab, stab, yre, yim, nyq, vx, x0c, fbias, w_bb, t_out, row0_blk):
    n_seq, seq_len, hw = yre.shape
    d = w_bb.shape[1]
    tt = min(seq_len, 256)
    per = seq_len // tt
    one = pl.Buffered(1)
    return pl.pallas_call(
        _hy_inv_kernel,
        out_shape=jax.ShapeDtypeStruct((t_out, d), BF16),
        grid=(n_seq, per),
        in_specs=[pl.BlockSpec((tt, seq_len), lambda b, j: (j, 0)),
                  pl.BlockSpec((tt, seq_len), lambda b, j: (j, 0)),
                  pl.BlockSpec((1, seq_len, hw), lambda b, j: (b, 0, 0), pipeline_mode=one),
                  pl.BlockSpec((1, seq_len, hw), lambda b, j: (b, 0, 0), pipeline_mode=one),
                  pl.BlockSpec((1, 1, hw), lambda b, j: (b, 0, 0)),
                  pl.BlockSpec((tt, hw), lambda b, j: ((row0_blk + b) * per + j, 0)),
                  pl.BlockSpec((tt, hw), lambda b, j: ((row0_blk + b) * per + j, 0)),
                  pl.BlockSpec((1, hw), lambda b, j: (0, 0)),
                  pl.BlockSpec((hw, d), lambda b, j: (0, 0), pipeline_mode=one)],
        out_specs=pl.BlockSpec((tt, d), lambda b, j: (b * per + j, 0)),
        compiler_params=_cp(("arbitrary", "arbitrary")),
        name="hy_inv",
    )(ctab, stab, yre, yim, nyq, vx, x0c, fbias, w_bb)


def _merge_kernel(tok, n_groups, xp_ref, xs_ref, pos_ref, g0_ref, g1_ref, pa_ref, pbc_ref, pbs_ref,
                  gate1_ref, sh2_ref, sc2_ref, gn_ref, wo_ref, wr_ref, br_ref,
                  x2_ref, h2_ref, rt_ref):
    i = pl.program_id(0)
    is_ctx = i < tok.nct
    pb = jnp.where(is_ctx, pbc_ref[...].astype(F32), pbs_ref[...].astype(F32))
    merged = g0_ref[...].astype(F32) * pa_ref[...].astype(F32) + g1_ref[...].astype(F32) * pb
    att = _dot(merged.astype(BF16), wo_ref[...])
    x = jnp.where(is_ctx, xp_ref[...], xs_ref[...] + pos_ref[...])
    x2 = x + gate1_ref[0] * att
    x2_ref[...] = x2
    h2 = _rms(x2, gn_ref[...]) * (1.0 + sc2_ref[0]) + sh2_ref[0]
    h2b = h2.astype(BF16)
    h2_ref[...] = h2b.astype(h2_ref.dtype)
    logits = _dot(h2b, wr_ref[...]) + br_ref[...]
    lane = lax.broadcasted_iota(jnp.int32, logits.shape, 1)
    gl = jnp.where(lane < n_groups, logits, -jnp.inf)
    gmax = jnp.max(gl, axis=-1, keepdims=True)
    gidx = jnp.min(jnp.where(gl == gmax, lane, n_groups), axis=-1, keepdims=True)
    rt_ref[...] = jnp.where(lane == LANES - 1, gidx.astype(F32), logits)


def _merge_out(xp, xs, pos, proj, gate_blk0, pa, pb_ctx, pb_lat, mod, g2, w_out, w_r, b_r, n_groups, tok):
    d = xp.shape[1]
    tm = tok.tm
    t_all = xp.shape[0] + xs.shape[0]
    kern = functools.partial(_merge_kernel, tok, n_groups)
    one = pl.Buffered(1)

    def modspec(k):
        return pl.BlockSpec((1, 1, d), lambda i: (tok.cond(i) * N_MOD + k, 0, 0))

    return pl.pallas_call(
        kern,
        out_shape=(jax.ShapeDtypeStruct((t_all, d), F32), jax.ShapeDtypeStruct((t_all, d), F32),
                   jax.ShapeDtypeStruct((t_all, LANES), F32)),
        grid=(tok.n,),
        in_specs=[
            pl.BlockSpec((tm, d), lambda i: (tok.ctx_idx(i), 0)),
            pl.BlockSpec((tm, d), lambda i: (tok.lat_idx(i), 0)),
            pl.BlockSpec((tm, d), lambda i: (tok.pos_idx(i), 0)),
            pl.BlockSpec((tm, d), lambda i: (i, gate_blk0)),
            pl.BlockSpec((tm, d), lambda i: (i, gate_blk0 + 1)),
            pl.BlockSpec((tm, d), lambda i: (i, 0)),
            pl.BlockSpec((tm, d), lambda i: (tok.ctx_idx(i), 0)),
            pl.BlockSpec((tm, d), lambda i: (tok.lat_idx(i), 0)),
            modspec(2), modspec(3), modspec(4),
            pl.BlockSpec((1, d), lambda i: (0, 0)),
            pl.BlockSpec((d, d), lambda i: (0, 0), pipeline_mode=one),
            pl.BlockSpec((d, LANES), lambda i: (0, 0)),
            pl.BlockSpec((1, LANES), lambda i: (0, 0)),
        ],
        out_specs=(pl.BlockSpec((tm, d), lambda i: (i, 0)), pl.BlockSpec((tm, d), lambda i: (i, 0)),
                   pl.BlockSpec((tm, LANES), lambda i: (i, 0))),
        compiler_params=_cp(("parallel",)),
        name="merge_out",
    )(xp, xs, pos, proj, proj, pa, pb_ctx, pb_lat, mod, mod, mod, g2, w_out, w_r, b_r)


def _moe_kernel(n_tiles, n_groups, epg, tg_ref, tnv_ref, rt_ref,
                h2_hbm, wr_ref, br_ref, wg_ref, wu_ref, wd_ref, out_hbm,
                xbuf, obuf, acc, gws, gsem, ssem, pend):
    i = pl.program_id(0)
    j = pl.program_id(1)
    tm = obuf.shape[0]
    nv = tnv_ref[i]
    slot = i % 2

    def gather_copy(tok_row, slot_, r):
        return pltpu.make_async_copy(h2_hbm.at[pl.ds(tok_row, 1), :],
                                     xbuf.at[slot_, pl.ds(r, 1), :], gsem.at[slot_])

    def scatter_copy(tok_row, r):
        return pltpu.make_async_copy(obuf.at[pl.ds(r, 1), :],
                                     out_hbm.at[pl.ds(tok_row, 1), :], ssem.at[0])

    def start_gather(tile, slot_):
        def body(r, c):
            gather_copy(rt_ref[tile * tm + r], slot_, r).start()
            return c
        lax.fori_loop(0, tnv_ref[tile], body, 0)

    def wait_gather(tile, slot_):
        def body(r, c):
            gather_copy(0, slot_, r).wait()
            return c
        lax.fori_loop(0, tnv_ref[tile], body, 0)

    def wait_scatter(count):
        def body(r, c):
            scatter_copy(0, r).wait()
            return c
        lax.fori_loop(0, count, body, 0)

    @pl.when(j == 0)
    def _():
        @pl.when(i == 0)
        def _():
            xbuf[...] = jnp.zeros(xbuf.shape, xbuf.dtype)
            pend[0] = 0
            start_gather(0, 0)

        wait_gather(i, slot)

        @pl.when(i + 1 < n_tiles)
        def _():
            start_gather(i + 1, 1 - slot)

        grp = tg_ref[i]
        xb = xbuf[slot].astype(BF16)
        logits = _dot(xb, wr_ref[...]) + br_ref[...]
        lane = lax.broadcasted_iota(jnp.int32, logits.shape, 1)
        gl = jnp.where(lane < n_groups, logits, -jnp.inf)
        gmax = jnp.max(gl, axis=-1, keepdims=True)
        ge = jnp.exp(gl - gmax)
        g_w = jnp.sum(jnp.where(lane == grp, ge, 0.0), axis=-1, keepdims=True) / jnp.sum(ge, axis=-1, keepdims=True)
        base = n_groups + grp * epg
        e = [jnp.sum(jnp.where(lane == base + k, logits, 0.0), axis=-1, keepdims=True) for k in range(epg)]
        sel = []
        for k in range(epg):
            rank = jnp.zeros_like(e[k])
            for m in range(epg):
                if m == k:
                    continue
                ahead = (e[m] > e[k]) if m > k else (e[m] >= e[k])
                rank = rank + ahead.astype(F32)
            sel.append(rank < TOP_K_INNER)
        emax = functools.reduce(jnp.maximum, e)
        ex = [jnp.where(sel[k], jnp.exp(e[k] - emax), 0.0) for k in range(epg)]
        den = functools.reduce(lambda a, b: a + b, ex)
        for k in range(epg):
            gws[k] = ex[k] / den * g_w
        acc[...] = jnp.zeros(acc.shape, acc.dtype)

    @pl.when(nv > 0)
    def _():
        xb = xbuf[slot].astype(BF16)
        g = _dot(xb, wg_ref[...])
        u = _dot(xb, wu_ref[...])
        hid = (g * jax.nn.sigmoid(g)) * u * gws[j]
        acc[...] += _dot(hid.astype(BF16), wd_ref[...])

    @pl.when(j == epg - 1)
    def _():
        wait_scatter(pend[0])
        obuf[...] = acc[...]

        def body(r, c):
            scatter_copy(rt_ref[i * tm + r], r).start()
            return c
        lax.fori_loop(0, nv, body, 0)
        pend[0] = nv

        @pl.when(i == n_tiles - 1)
        def _():
            wait_scatter(nv)


def _moe(h2, w_r, b_r, w_g, w_u, w_d, tile_grp, tile_nv, row_tok, n_groups, epg, tm):
    t, d = h2.shape
    ff = w_g.shape[-1]
    n_tiles = tile_grp.shape[0]
    kern = functools.partial(_moe_kernel, n_tiles, n_groups, epg)

    def widx(i, j, tg, tnv, rt):
        return tg[i] * epg + jnp.where(tnv[i] > 0, j, epg - 1)

    grid_spec = pltpu.PrefetchScalarGridSpec(
        num_scalar_prefetch=3,
        grid=(n_tiles, epg),
        in_specs=[
            pl.BlockSpec(memory_space=pl.ANY),
            pl.BlockSpec((d, LANES), lambda i, j, tg, tnv, rt: (0, 0)),
            pl.BlockSpec((1, LANES), lambda i, j, tg, tnv, rt: (0, 0)),
            pl.BlockSpec((None, d, ff), lambda i, j, tg, tnv, rt: (widx(i, j, tg, tnv, rt), 0, 0)),
            pl.BlockSpec((None, d, ff), lambda i, j, tg, tnv, rt: (widx(i, j, tg, tnv, rt), 0, 0)),
            pl.BlockSpec((None, ff, d), lambda i, j, tg, tnv, rt: (widx(i, j, tg, tnv, rt), 0, 0)),
        ],
        out_specs=pl.BlockSpec(memory_space=pl.ANY),
        scratch_shapes=[
            pltpu.VMEM((2, tm, d), F32),
            pltpu.VMEM((tm, d), F32),
            pltpu.VMEM((tm, d), F32),
            pltpu.VMEM((epg, tm, 1), F32),
            pltpu.SemaphoreType.DMA((2,)),
            pltpu.SemaphoreType.DMA((1,)),
            pltpu.SMEM((1,), jnp.int32),
        ],
    )
    return pl.pallas_call(
        kern,
        out_shape=jax.ShapeDtypeStruct((t, d), F32),
        grid_spec=grid_spec,
        compiler_params=_cp(("arbitrary", "arbitrary")),
        name="moe",
    )(tile_grp, tile_nv, row_tok, h2, w_r, b_r, w_g, w_u, w_d)


def _moe_plan(grp, n_groups, tm, n_tiles):
    t = grp.shape[0]
    onehot = (grp[:, None] == jnp.arange(n_groups, dtype=jnp.int32)[None, :]).astype(jnp.int32)
    csum = jnp.cumsum(onehot, axis=0)
    counts = csum[-1]
    rank = jnp.take_along_axis(csum, grp[:, None], axis=1)[:, 0] - 1
    ntile_g = (counts + tm - 1) // tm
    tile_end = jnp.cumsum(ntile_g)
    tile_off = tile_end - ntile_g
    slot = tile_off[grp] * tm + rank
    row_tok = jnp.zeros((n_tiles * tm,), jnp.int32).at[slot].set(jnp.arange(t, dtype=jnp.int32))
    tiles = jnp.arange(n_tiles, dtype=jnp.int32)
    tile_grp = jnp.minimum(jnp.sum((tiles[:, None] >= tile_end[None, :]).astype(jnp.int32), axis=1),
                           n_groups - 1)
    tile_nv = jnp.clip(counts[tile_grp] - (tiles - tile_off[tile_grp]) * tm, 0, tm)
    tile_nv = jnp.where(tiles < tile_end[-1], tile_nv, 0)
    return tile_grp.astype(jnp.int32), tile_nv.astype(jnp.int32), row_tok


def _final_kernel(x2_ref, m_ref, gate2_ref, gf_ref, o_ref):
    x3 = x2_ref[...] + gate2_ref[0] * m_ref[...]
    o_ref[...] = _rms(x3, gf_ref[...])


def _final(x2, moe, mod, g_final, row_blk0, n_rows, cond_of_tile, tm):
    d = x2.shape[1]
    return pl.pallas_call(
        _final_kernel,
        out_shape=jax.ShapeDtypeStruct((n_rows, d), F32),
        grid=(n_rows // tm,),
        in_specs=[pl.BlockSpec((tm, d), lambda i: (row_blk0 + i, 0)),
                  pl.BlockSpec((tm, d), lambda i: (row_blk0 + i, 0)),
                  pl.BlockSpec((1, 1, d), lambda i: (cond_of_tile(i) * N_MOD + 5, 0, 0)),
                  pl.BlockSpec((1, d), lambda i: (0, 0))],
        out_specs=pl.BlockSpec((tm, d), lambda i: (i, 0)),
        compiler_params=_cp(("parallel",)),
        name="final",
    )(x2, moe, mod, g_final)


def _grid_pos_embed(n_tokens, dim):
    rows_n = n_tokens // GRID_W
    quarter = dim // 4
    omega = 1.0 / (10000.0 ** (jnp.arange(quarter, dtype=F32) / quarter))

    def emb(n):
        ang = jnp.arange(n).astype(F32)[:, None] * omega[None, :]
        return jnp.concatenate([jnp.sin(ang), jnp.cos(ang)], axis=-1)

    e_r = jnp.broadcast_to(emb(rows_n)[:, None, :], (rows_n, GRID_W, dim // 2))
    e_c = jnp.broadcast_to(emb(GRID_W)[None, :, :], (rows_n, GRID_W, dim // 2))
    return jnp.concatenate([e_r, e_c], axis=-1).reshape(n_tokens, dim)


def _state_to_lanes(s):
    b, _, _, g, p = s.shape
    return jnp.transpose(s, (0, 3, 2, 1, 4)).reshape(b, g, 4 * p)


def _lanes_to_state(x, p):
    b, g, _ = x.shape
    return jnp.transpose(x.reshape(b, g, 2, 2, p), (0, 3, 2, 1, 4))


def kernel(x_prompt, x_sample, state_ssm, c, c_ctx, w_ada, b_ada, g_norm1, g_norm2, w_in, ssm_a_re, ssm_a_im, ssm_log_dt, ssm_b_re, ssm_b_im, ssm_c_re, ssm_c_im, ssm_d, w_glu, b_glu, w_short, b_short, w_f1, b_f1, freq1, w_f2, b_f2, freq2, w_f3, b_f3, filter_bias, w_branch_a, w_branch_b, w_out, w_router_group, b_router_group, w_router_expert, b_router_expert, w_exp_gate, w_exp_up, w_exp_down, g_final):
    depth = w_ada.shape[0]
    assert depth == 1, "single-layer pipeline"
    bsz, seq, d = x_prompt.shape
    dbsz, dseq, _ = x_sample.shape
    g_all, p_state = ssm_a_re.shape[2:]
    ssm_w = ssm_d.shape[1]
    hw = filter_bias.shape[1]
    n_groups, _, epg = w_router_expert.shape[1:]
    ff = w_exp_gate.shape[-1]
    assert ssm_w // g_all * CHUNK == MXU_SIDE and 4 * p_state == MXU_SIDE
    assert ssm_w == hw and d == 2 * hw
    n_ctx = bsz * seq
    n_lat = dbsz * dseq
    t_all = n_ctx + n_lat

    xp = x_prompt.reshape(n_ctx, d)
    xs = x_sample.reshape(n_lat, d)
    pos = _grid_pos_embed(dseq, d)

    n_cond = 1 + dbsz
    cond8 = jnp.zeros((SUBLANES, d), F32).at[0].set(c_ctx).at[1:n_cond].set(c)
    mod = _ada_mod(cond8, w_ada[0], b_ada[0][None])
    mod = mod[:n_cond].reshape(n_cond * N_MOD, 1, d)

    tm_e = min(256, seq, dseq)
    tok_e = _Tok(n_ctx, n_lat, dseq, tm_e)
    gate_col0 = ssm_w + 3 * hw
    h1 = _norm_mod(xp, xs, pos, mod, g_norm1, tok_e)
    tm_p, tn_p = min(1024, t_all), min(1024, hw)
    n_in = w_in.shape[2]
    u_a = _in_proj(h1, w_in[0], (0, 0), (0, ssm_w), F32, tm_p, tn_p, "in_proj_a")
    proj = _in_proj(h1, w_in[0], (gate_col0, n_in), (ssm_w, gate_col0), BF16, tm_p, tn_p, "in_proj")
    n_gate_cols = n_in - gate_col0

    w_state, w_so, a_step = _ssm_operators(ssm_a_re[0], ssm_a_im[0], ssm_log_dt[0], ssm_b_re[0], ssm_b_im[0],
                                           ssm_c_re[0], ssm_c_im[0], ssm_d[0])
    nc_ctx = seq // CHUNK
    nc_lat = dseq // CHUNK
    seq_per_blk = max(1, min(bsz, MXU_SIDE // nc_ctx))
    assert bsz % seq_per_blk == 0 and n_ctx % dseq == 0
    s0_ctx = jnp.zeros((bsz, g_all, 4 * p_state), F32)
    y_ctx, fin_ctx = _ssm_call(u_a, 0, n_ctx, w_state, w_so, a_step, s0_ctx, seq_per_blk, nc_ctx, "ssm_ctx")
    s0_lat = _state_to_lanes(state_ssm[:, 0].astype(F32))
    y_lat, _ = _ssm_call(u_a, n_ctx // dseq, n_lat, w_state, w_so, a_step, s0_lat, 1, nc_lat, "ssm_lat")
    new_state = _lanes_to_state(fin_ctx, p_state)[:, None]

    pa = _glu_branch(y_ctx, y_lat, w_glu[0].astype(BF16), b_glu[0][None], w_branch_a[0].astype(BF16), tok_e)

    w_short3 = jnp.transpose(w_short[0].reshape(-1, 3, hw), (1, 0, 2))
    b_short3 = b_short[0].reshape(3, 1, hw)
    vx, x0c = _short_conv(proj, n_gate_cols // hw, hw, w_short3, b_short3, n_ctx, seq, dseq, tm_e)
    rates = jnp.abs(jnp.linspace(math.log(DECAY_TARGET) / DECAY_FAST, math.log(DECAY_TARGET) / DECAY_SLOW,
                                 hw, dtype=F32))
    rates2 = jnp.concatenate([rates, rates])[None]
    fbias = filter_bias[0][None].astype(F32)
    w_bb = w_branch_b[0].astype(BF16)
    pbs = []
    for seq_len, n_seq, row0 in ((seq, bsz, 0), (dseq, dbsz, n_ctx // dseq)):
        ctab, stab = _dft_tables(seq_len)
        a_f, bm_f, colabs, nyq_f = _filt_gen(seq_len, w_f1[0], b_f1[0][None], freq1[0][None], w_f2[0],
                                            b_f2[0][None], freq2[0][None], w_f3[0], b_f3[0][None], rates2)
        kr, ki, kn = _filt_dft(ctab, stab, a_f, bm_f, colabs, nyq_f)
        yre, yim, nyq = _hy_fwd(ctab, stab, vx, kr, ki, kn, n_seq, row0)
        pbs.append(_hy_inv(ctab, stab, yre, yim, nyq, vx, x0c, fbias, w_bb, n_seq * seq_len, row0))
    pb_ctx, pb_lat = pbs

    w_r = jnp.zeros((d, LANES), F32)
    w_r = w_r.at[:, :n_groups].set(w_router_group[0])
    w_r = w_r.at[:, n_groups:n_groups + n_groups * epg].set(
        jnp.transpose(w_router_expert[0], (1, 0, 2)).reshape(d, n_groups * epg))
    b_r = jnp.zeros((1, LANES), F32)
    b_r = b_r.at[0, :n_groups].set(b_router_group[0])
    b_r = b_r.at[0, n_groups:n_groups + n_groups * epg].set(b_router_expert[0].reshape(-1))
    w_r = w_r.astype(BF16)
    x2, h2, rout = _merge_out(xp, xs, pos, proj, 0, pa, pb_ctx, pb_lat, mod, g_norm2,
                              w_out[0].astype(BF16), w_r, b_r, n_groups, tok_e)

    tm_m = min(512, t_all // n_groups)
    n_tiles = t_all // tm_m + n_groups
    grp = rout[:, LANES - 1].astype(jnp.int32)
    tile_grp, tile_nv, row_tok = _moe_plan(grp, n_groups, tm_m, n_tiles)
    moe = _moe(h2, w_r, b_r, w_exp_gate[0].astype(BF16), w_exp_up[0].astype(BF16),
               w_exp_down[0].astype(BF16), tile_grp, tile_nv, row_tok, n_groups, epg, tm_m)

    gf = g_final[None]
    y_prompt = _final(x2, moe, mod, gf, 0, n_ctx, lambda i: 0, tm_e)
    lat_tiles = dseq // tm_e
    y_sample = _final(x2, moe, mod, gf, n_ctx // tm_e, n_lat, lambda i: 1 + i // lat_tiles, tm_e)
    return (y_prompt.reshape(bsz, seq, d), y_sample.reshape(dbsz, dseq, d), new_state)
```

```python
import functools
import math

import numpy as np
import jax
import jax.numpy as jnp
from jax import lax
from jax.experimental import pallas as pl
from jax.experimental.pallas import tpu as pltpu

F32 = jnp.float32
BF16 = jnp.bfloat16
EPS = 1e-6
GRID_W = 64
N_MOD = 6
TOP_K_INNER = 2
DECAY_TARGET = 1e-2
DECAY_FAST = 0.3
DECAY_SLOW = 1.5
LANES = 128
SUBLANES = 8
MXU_SIDE = 256
VMEM_LIMIT = 56 * 1024 * 1024


def _cp(sem, vmem=VMEM_LIMIT):
    return pltpu.CompilerParams(dimension_semantics=sem, vmem_limit_bytes=vmem)


def _dot(a, b):
    return jnp.dot(a, b, preferred_element_type=F32)


def _split(a):
    hi = a.astype(BF16)
    lo = (a - hi.astype(F32)).astype(BF16)
    return hi, lo


def _dot3(a, b):
    a_hi, a_lo = _split(a)
    b_hi, b_lo = _split(b)
    return _dot(a_hi, b_hi) + _dot(a_lo, b_hi) + _dot(a_hi, b_lo)


def _rms(x, g):
    ms = jnp.mean(x * x, axis=-1, keepdims=True)
    return x * lax.rsqrt(ms + EPS) * g


def _ada_kernel(c_ref, w_ref, b_ref, o_ref):
    c = c_ref[...]
    a = c * jax.nn.sigmoid(c)
    o_ref[...] = _dot3(a, w_ref[...]) + b_ref[...]


def _ada_mod(cond8, w_ada, b_ada):
    d, n = w_ada.shape
    tn = min(n, 1024)
    return pl.pallas_call(
        _ada_kernel,
        out_shape=jax.ShapeDtypeStruct((SUBLANES, n), F32),
        grid=(n // tn,),
        in_specs=[pl.BlockSpec((SUBLANES, d), lambda j: (0, 0)),
                  pl.BlockSpec((d, tn), lambda j: (0, j)),
                  pl.BlockSpec((1, tn), lambda j: (0, j))],
        out_specs=pl.BlockSpec((SUBLANES, tn), lambda j: (0, j)),
        compiler_params=_cp(("arbitrary",)),
        name="ada_mod",
    )(cond8, w_ada, b_ada)


class _Tok:
    def __init__(self, n_ctx, n_lat, lat_len, tm):
        assert n_ctx % tm == 0 and lat_len % tm == 0 and n_lat % lat_len == 0
        self.tm = tm
        self.nct = n_ctx // tm
        self.nst = n_lat // tm
        self.per_seq = lat_len // tm
        self.n = self.nct + self.nst

    def ctx_idx(self, i):
        return jnp.minimum(i, self.nct - 1)

    def lat_idx(self, i):
        return jnp.maximum(i - self.nct, 0)

    def pos_idx(self, i):
        return self.lat_idx(i) % self.per_seq

    def cond(self, i):
        return jnp.where(i < self.nct, 0, 1 + self.lat_idx(i) // self.per_seq)


def _norm_mod_kernel(tok, xp_ref, xs_ref, pos_ref, sh_ref, sc_ref, g_ref, h_ref):
    i = pl.program_id(0)

    def modulate(x):
        y = _rms(x, g_ref[...])
        return (y * (1.0 + sc_ref[0]) + sh_ref[0]).astype(h_ref.dtype)

    @pl.when(i < tok.nct)
    def _():
        h_ref[...] = modulate(xp_ref[...])

    @pl.when(i >= tok.nct)
    def _():
        h_ref[...] = modulate(xs_ref[...] + pos_ref[...])


def _norm_mod(xp, xs, pos, mod, g1, tok):
    d = xp.shape[1]
    tm = tok.tm
    t_all = xp.shape[0] + xs.shape[0]
    return pl.pallas_call(
        functools.partial(_norm_mod_kernel, tok),
        out_shape=jax.ShapeDtypeStruct((t_all, d), BF16),
        grid=(tok.n,),
        in_specs=[
            pl.BlockSpec((tm, d), lambda i: (tok.ctx_idx(i), 0)),
            pl.BlockSpec((tm, d), lambda i: (tok.lat_idx(i), 0)),
            pl.BlockSpec((tm, d), lambda i: (tok.pos_idx(i), 0)),
            pl.BlockSpec((1, 1, d), lambda i: (tok.cond(i) * N_MOD + 0, 0, 0)),
            pl.BlockSpec((1, 1, d), lambda i: (tok.cond(i) * N_MOD + 1, 0, 0)),
            pl.BlockSpec((1, d), lambda i: (0, 0)),
        ],
        out_specs=pl.BlockSpec((tm, d), lambda i: (i, 0)),
        compiler_params=_cp(("parallel",)),
        name="norm_mod",
    )(xp, xs, pos, mod, mod, g1)


def _in_proj_kernel(n_sig, h_ref, w_ref, o_ref, wb_ref):
    n = pl.program_id(0)
    m = pl.program_id(1)

    @pl.when(m == 0)
    def _():
        wb_ref[...] = w_ref[...].astype(BF16)

    acc = _dot(h_ref[...], wb_ref[...])

    @pl.when(n >= n_sig)
    def _():
        o_ref[...] = acc.astype(o_ref.dtype)

    @pl.when(n < n_sig)
    def _():
        o_ref[...] = jax.nn.sigmoid(acc).astype(o_ref.dtype)


def _in_proj(h, w_in, sig_cols, lin_cols, out_dtype, tm, tn, name):
    t_all, d = h.shape
    for c in sig_cols + lin_cols:
        assert c % tn == 0
    assert t_all % tm == 0
    n_sig = (sig_cols[1] - sig_cols[0]) // tn
    n_lin = (lin_cols[1] - lin_cols[0]) // tn
    sig0, lin0 = sig_cols[0] // tn, lin_cols[0] // tn

    def w_blk(n):
        return jnp.where(n < n_sig, sig0 + n, lin0 + n - n_sig)

    return pl.pallas_call(
        functools.partial(_in_proj_kernel, n_sig),
        out_shape=jax.ShapeDtypeStruct((t_all, (n_sig + n_lin) * tn), out_dtype),
        grid=(n_sig + n_lin, t_all // tm),
        in_specs=[pl.BlockSpec((tm, d), lambda n, m: (m, 0)),
                  pl.BlockSpec((d, tn), lambda n, m: (0, w_blk(n)))],
        out_specs=pl.BlockSpec((tm, tn), lambda n, m: (m, n)),
        scratch_shapes=[pltpu.VMEM((d, tn), BF16)],
        compiler_params=_cp(("arbitrary", "arbitrary")),
        name=name,
    )(h, w_in)


CHUNK = 16
GROUP_PAD = 8


def _ssm_operators(a_re, a_im, log_dt, b_re, b_im, c_re, c_im, d_skip):
    t = CHUNK
    g, p = a_re.shape[1:]
    c = b_re.shape[-1]
    lam = lax.complex(a_re.astype(F32), a_im.astype(F32))
    dt = jnp.exp(log_dt.astype(F32))[..., None]
    ld = lam * dt
    lam_bar = jnp.exp(ld)
    b_bar = ((lam_bar - 1.0) / lam)[..., None] * lax.complex(b_re.astype(F32), b_im.astype(F32))
    c_mat = lax.complex(c_re.astype(F32), c_im.astype(F32))
    taus = jnp.arange(t + 1, dtype=F32)
    e_pow = jnp.exp(taus[None, None, :, None] * ld[:, :, None, :])

    k_all = jnp.real(jnp.einsum('dgop,dgtp,dgpi->dgtoi', c_mat, e_pow[:, :, :t], b_bar))
    kf, kb = k_all[0], k_all[1]
    ii = np.arange(t)[None, :, None]
    jj = np.arange(t)[None, None, :]
    tt = np.arange(t)[:, None, None]
    place_f = jnp.asarray((jj - ii == tt).astype(np.float32))
    place_b = jnp.asarray((ii - jj == tt).astype(np.float32))
    kf = kf.at[:, 0].add(jnp.eye(c, dtype=F32)[None] * d_skip.astype(F32).reshape(g, c, 1))
    m = jnp.einsum('tij,gtoc->gicjo', place_f, kf) + jnp.einsum('tij,gtoc->gicjo', place_b, kb)
    m = m.reshape(g, t * c, t * c)

    e_f = e_pow[0][:, ::-1][:, 1:]
    e_b = e_pow[1][:, :t]
    ws_f = e_f[:, :, :, None] * b_bar[0][:, None]
    ws_b = e_b[:, :, :, None] * b_bar[1][:, None]

    def rows_ic(x):
        return jnp.transpose(x, (0, 1, 3, 2)).reshape(g, t * c, p)

    w_state = jnp.concatenate([rows_ic(jnp.real(ws_f)), rows_ic(jnp.real(ws_b)),
                               rows_ic(jnp.imag(ws_f)), rows_ic(jnp.imag(ws_b))], axis=-1)

    ce_f = c_mat[0][:, None] * e_pow[0][:, 1:, None, :]
    ce_b = c_mat[1][:, None] * e_pow[1][:, ::-1][:, :t, None, :]

    def cols_jc(x):
        return jnp.transpose(x, (0, 3, 1, 2)).reshape(g, p, t * c)

    w_so = jnp.concatenate([cols_jc(jnp.real(ce_f)), cols_jc(jnp.real(ce_b)),
                            cols_jc(-jnp.imag(ce_f)), cols_jc(-jnp.imag(ce_b))], axis=1)
    w_out = jnp.concatenate([m, w_so], axis=1)

    a_t = e_pow[:, :, t]
    a_step = jnp.stack([jnp.concatenate([jnp.real(a_t[0]), jnp.real(a_t[1])], axis=-1),
                        jnp.concatenate([jnp.imag(a_t[0]), jnp.imag(a_t[1])], axis=-1)])
    return w_state.astype(BF16), w_out.astype(BF16), a_step


def _ssm_kernel(n_seq, n_chunk, gb, g_all, x_ref, ws_ref, wo_ref, a_ref, s0_ref,
                y_ref, fin_ref, xq_ref, za_ref, zb_ref):
    phase = pl.program_id(1)
    j = pl.program_id(2)
    gp = g_all + GROUP_PAD
    nck = n_seq * n_chunk
    half = LANES // 2
    c = LANES // gb

    @pl.when(phase == 0)
    def _():
        xs = [x_ref[pl.ds(i, nck, stride=CHUNK), :] for i in range(CHUNK)]
        for q in range(gb):
            xq = jnp.concatenate([xs[i][:, q * c:(q + 1) * c] for i in range(CHUNK)], axis=-1).astype(BF16)
            xq_ref[j * gb + q] = xq
            z = _dot(xq, ws_ref[q])
            za_ref[pl.ds(j * gb + q, nck, stride=gp), :] = z[:, :LANES]
            zb_ref[pl.ds(j * gb + q, nck, stride=gp), :] = z[:, LANES:]

    @pl.when(jnp.logical_and(phase == 1, j == 0))
    def _():
        a_re = a_ref[0]
        a_im = a_ref[1]
        lane = lax.broadcasted_iota(jnp.int32, (g_all, LANES), 1)
        is_f = lane < half
        for s in range(n_seq):
            def step(k, carry):
                s_a, s_b = carry
                rf = pl.multiple_of((s * n_chunk + k) * gp, SUBLANES)
                rb = pl.multiple_of((s * n_chunk + n_chunk - 1 - k) * gp, SUBLANES)
                zf_a = za_ref[pl.ds(rf, g_all), :]
                zf_b = zb_ref[pl.ds(rf, g_all), :]
                zb_a = za_ref[pl.ds(rb, g_all), :]
                zb_b = zb_ref[pl.ds(rb, g_all), :]
                za_ref[pl.ds(rf, g_all), :] = jnp.where(is_f, s_a, zf_a)
                zb_ref[pl.ds(rf, g_all), :] = jnp.where(is_f, s_b, zf_b)
                za_ref[pl.ds(rb, g_all), :] = jnp.where(is_f, zb_a, s_a)
                zb_ref[pl.ds(rb, g_all), :] = jnp.where(is_f, zb_b, s_b)
                z_a = jnp.where(is_f, zf_a, zb_a)
                z_b = jnp.where(is_f, zf_b, zb_b)
                n_a = a_re * s_a - a_im * s_b + z_a
                n_b = a_re * s_b + a_im * s_a + z_b
                return n_a, n_b

            init = s0_ref[s]
            f_a, f_b = lax.fori_loop(0, n_chunk, step, (init[:, :LANES], init[:, LANES:]))
            fin_ref[s] = jnp.concatenate([f_a, f_b], axis=-1)

    @pl.when(phase == 1)
    def _():
        ys = []
        for q in range(gb):
            st_a = za_ref[pl.ds(j * gb + q, nck, stride=gp), :].astype(BF16)
            st_b = zb_ref[pl.ds(j * gb + q, nck, stride=gp), :].astype(BF16)
            lhs = jnp.concatenate([xq_ref[j * gb + q], st_a, st_b], axis=-1)
            ys.append(_dot(lhs, wo_ref[q]))
        for i in range(CHUNK):
            yi = jnp.concatenate([ys[q][:, i * c:(i + 1) * c] for q in range(gb)], axis=-1)
            y_ref[pl.ds(i, nck, stride=CHUNK), :] = yi


def _ssm_call(u, row_blk0, n_rows, w_state, w_out, a_step, s0, n_seq, n_chunk, name):
    g_all, w, _ = w_state.shape
    nck = n_seq * n_chunk
    blk = nck * CHUNK
    n_blk = n_rows // blk
    gb = LANES * CHUNK // w
    ngb = g_all // gb
    gp = g_all + GROUP_PAD
    kern = functools.partial(_ssm_kernel, n_seq, n_chunk, gb, g_all)
    return pl.pallas_call(
        kern,
        out_shape=(jax.ShapeDtypeStruct((n_rows, u.shape[1]), F32),
                   jax.ShapeDtypeStruct((n_blk * n_seq, g_all, w), F32)),
        grid=(n_blk, 2, ngb),
        in_specs=[
            pl.BlockSpec((blk, LANES), lambda b, ph, j: (row_blk0 + b, jnp.where(ph == 0, j, ngb - 1))),
            pl.BlockSpec((gb, w, w), lambda b, ph, j: (jnp.where(ph == 0, j, ngb - 1), 0, 0)),
            pl.BlockSpec((gb, 2 * w, w), lambda b, ph, j: (jnp.where(ph == 0, 0, j), 0, 0)),
            pl.BlockSpec((2, g_all, LANES), lambda b, ph, j: (0, 0, 0)),
            pl.BlockSpec((n_seq, g_all, w), lambda b, ph, j: (b, 0, 0)),
        ],
        out_specs=(
            pl.BlockSpec((blk, LANES), lambda b, ph, j: (b, jnp.where(ph == 0, 0, j))),
            pl.BlockSpec((n_seq, g_all, w), lambda b, ph, j: (b, 0, 0)),
        ),
        scratch_shapes=[pltpu.VMEM((g_all, nck, w), BF16),
                        pltpu.VMEM((nck * gp, LANES), F32), pltpu.VMEM((nck * gp, LANES), F32)],
        compiler_params=_cp(("arbitrary", "arbitrary", "arbitrary")),
        name=name,
    )(u, w_state, w_out, a_step, s0)


def _glu_kernel(nct, yc_ref, yl_ref, wg_ref, bg_ref, wb_ref, o_ref):
    i = pl.program_id(0)
    y = jax.nn.gelu(jnp.where(i < nct, yc_ref[...], yl_ref[...]))
    z = _dot(y.astype(BF16), wg_ref[...]) + bg_ref[...]
    ya = y * jax.nn.sigmoid(z)
    o_ref[...] = _dot(ya.astype(BF16), wb_ref[...]).astype(o_ref.dtype)


def _glu_branch(y_ctx, y_lat, w_glu, b_glu, w_ba, tok):
    c = y_ctx.shape[1]
    d = w_ba.shape[1]
    tm = tok.tm
    return pl.pallas_call(
        functools.partial(_glu_kernel, tok.nct),
        out_shape=jax.ShapeDtypeStruct((y_ctx.shape[0] + y_lat.shape[0], d), BF16),
        grid=(tok.n,),
        in_specs=[pl.BlockSpec((tm, c), lambda i: (tok.ctx_idx(i), 0)),
                  pl.BlockSpec((tm, c), lambda i: (tok.lat_idx(i), 0)),
                  pl.BlockSpec((c, c), lambda i: (0, 0)),
                  pl.BlockSpec((1, c), lambda i: (0, 0)),
                  pl.BlockSpec((c, d), lambda i: (0, 0))],
        out_specs=pl.BlockSpec((tm, d), lambda i: (i, 0)),
        compiler_params=_cp(("parallel",)),
        name="glu_branch",
    )(y_ctx, y_lat, w_glu, b_glu, w_ba)


def _short_conv_kernel(nct, per_ctx, per_lat, v_ref, x0_ref, x1_ref, vp_ref, x0p_ref, x1p_ref,
                       vn_ref, x0n_ref, x1n_ref, w_ref, b_ref, vx_ref, x0c_ref):
    i = pl.program_id(0)
    tm = v_ref.shape[0]
    k = jnp.where(i < nct, i % per_ctx, (i - nct) % per_lat)
    per = jnp.where(i < nct, per_ctx, per_lat)
    first = k == 0
    last = k == per - 1
    row = lax.broadcasted_iota(jnp.int32, (tm, 1), 0)

    def conv(cur_ref, prev_ref, next_ref, part):
        x = cur_ref[...].astype(F32)
        hp = jnp.where(first, 0.0, prev_ref[SUBLANES - 1:SUBLANES, :].astype(F32))
        hn = jnp.where(last, 0.0, next_ref[0:1, :].astype(F32))
        xm = jnp.where(row == 0, hp, pltpu.roll(x, 1, axis=0))
        xq = jnp.where(row == tm - 1, hn, pltpu.roll(x, tm - 1, axis=0))
        w = w_ref[part]
        return b_ref[part] + xm * w[0:1, :] + x * w[1:2, :] + xq * w[2:3, :]

    v = conv(v_ref, vp_ref, vn_ref, 0)
    x0 = conv(x0_ref, x0p_ref, x0n_ref, 1)
    x1 = conv(x1_ref, x1p_ref, x1n_ref, 2)
    vx_ref[...] = (v * x1).astype(vx_ref.dtype)
    x0c_ref[...] = x0.astype(x0c_ref.dtype)


def _short_conv(proj, col_blk0, hw, w_short3, b_short3, n_ctx, seq, dseq, tm):
    t = proj.shape[0]
    assert seq % tm == 0 and dseq % tm == 0
    nt = t // tm
    r8 = tm // SUBLANES
    nb8 = t // SUBLANES

    def cur(k):
        return pl.BlockSpec((tm, hw), lambda i: (i, col_blk0 + k))

    def prev(k):
        return pl.BlockSpec((SUBLANES, hw), lambda i: (jnp.maximum(i * r8 - 1, 0), col_blk0 + k))

    def nxt(k):
        return pl.BlockSpec((SUBLANES, hw), lambda i: (jnp.minimum((i + 1) * r8, nb8 - 1), col_blk0 + k))

    kern = functools.partial(_short_conv_kernel, n_ctx // tm, seq // tm, dseq // tm)
    return pl.pallas_call(
        kern,
        out_shape=(jax.ShapeDtypeStruct((t, hw), BF16), jax.ShapeDtypeStruct((t, hw), BF16)),
        grid=(nt,),
        in_specs=[cur(0), cur(1), cur(2), prev(0), prev(1), prev(2), nxt(0), nxt(1), nxt(2),
                  pl.BlockSpec((3, 3, hw), lambda i: (0, 0, 0)),
                  pl.BlockSpec((3, 1, hw), lambda i: (0, 0, 0))],
        out_specs=(pl.BlockSpec((tm, hw), lambda i: (i, 0)), pl.BlockSpec((tm, hw), lambda i: (i, 0))),
        compiler_params=_cp(("parallel",)),
        name="short_conv",
    )(proj, proj, proj, proj, proj, proj, proj, proj, proj, w_short3, b_short3)


def _filt_gen_kernel(seq_len, n_bands, wt_ref, wc_ref, ws_ref, b1_ref, f1_ref, w2_ref, b2_ref, f2_ref,
                     w3_ref, b3_ref, rate_ref, a_ref, bm_ref, abs_ref, nyq_ref):
    i = pl.program_id(0)
    tl = a_ref.shape[0]
    hw = a_ref.shape[1]
    pos = lax.broadcasted_iota(jnp.int32, (tl, 1), 0) + i * tl
    t = pos.astype(F32) / seq_len
    bands = (lax.broadcasted_iota(jnp.int32, (1, wc_ref.shape[0]), 1) + 1).astype(F32)
    ang = (2.0 * math.pi) * t * bands
    pre = t * wt_ref[...] + _dot3(jnp.cos(ang), wc_ref[...]) + _dot3(jnp.sin(ang), ws_ref[...])
    h = jnp.sin(f1_ref[...] * (pre + b1_ref[...]))
    h = jnp.sin(f2_ref[...] * (_dot3(h, w2_ref[...]) + b2_ref[...]))
    h = _dot3(h, w3_ref[...]) + b3_ref[...]
    h = h * jnp.exp(-t * rate_ref[...])
    h_f = h[:, :hw]
    h_b = h[:, hw:]
    colabs = jnp.sum(jnp.abs(h_f) + jnp.abs(h_b), axis=0, keepdims=True)
    h_bp = jnp.where(pos == 0, 0.0, h_b)
    a = h_f + h_bp
    sgn = (1 - 2 * (pos % 2)).astype(F32)
    nyq = jnp.sum(a * sgn, axis=0, keepdims=True)
    a_ref[...] = a.astype(a_ref.dtype)
    bm_ref[...] = (h_bp - h_f).astype(bm_ref.dtype)

    @pl.when(i == 0)
    def _():
        abs_ref[...] = colabs
        nyq_ref[...] = nyq

    @pl.when(i > 0)
    def _():
        abs_ref[...] += colabs
        nyq_ref[...] += nyq


def _filt_gen(seq_len, w_f1, b_f1, freq1, w_f2, b_f2, freq2, w_f3, b_f3, rates2):
    n_emb, hid_raw = w_f1.shape
    n_bands = (n_emb - 1) // 2
    hw2 = w_f3.shape[1]
    hw = hw2 // 2
    tl = min(seq_len, 512)
    hid = LANES
    assert n_bands <= LANES and hid_raw <= LANES

    def pad(x, rows, cols):
        return jnp.zeros((rows, cols), F32).at[:x.shape[0], :x.shape[1]].set(x.astype(F32))

    wt = pad(w_f1[0:1], 1, hid)
    wc = pad(w_f1[1:1 + n_bands], LANES, hid)
    ws = pad(w_f1[1 + n_bands:], LANES, hid)
    b_f1, freq1, b_f2, freq2 = (pad(x, 1, hid) for x in (b_f1, freq1, b_f2, freq2))
    w_f2 = pad(w_f2, hid, hid)
    w_f3 = pad(w_f3, hid, hw2)
    full = lambda shape: pl.BlockSpec(shape, lambda i: (0,) * len(shape))
    kern = functools.partial(_filt_gen_kernel, seq_len, n_bands)
    return pl.pallas_call(
        kern,
        out_shape=(jax.ShapeDtypeStruct((seq_len, hw), BF16), jax.ShapeDtypeStruct((seq_len, hw), BF16),
                   jax.ShapeDtypeStruct((1, hw), F32), jax.ShapeDtypeStruct((1, hw), F32)),
        grid=(seq_len // tl,),
        in_specs=[full((1, hid)), full((LANES, hid)), full((LANES, hid)), full((1, hid)), full((1, hid)),
                  full((hid, hid)), full((1, hid)), full((1, hid)), full((hid, hw2)), full((1, hw2)),
                  full((1, hw2))],
        out_specs=(pl.BlockSpec((tl, hw), lambda i: (i, 0)), pl.BlockSpec((tl, hw), lambda i: (i, 0)),
                   full((1, hw)), full((1, hw))),
        compiler_params=_cp(("arbitrary",)),
        name="filt_gen",
    )(wt, wc, ws, b_f1, freq1, w_f2, b_f2, freq2, w_f3, b_f3, rates2)


def _filt_dft_kernel(seq_len, c_ref, s_ref, a_ref, bm_ref, abs_ref, nyq_ref, kr_ref, ki_ref, kn_ref):
    inv = 1.0 / (abs_ref[...] + EPS)
    scale = inv * (1.0 / seq_len)
    kr_ref[...] = _dot(c_ref[...], a_ref[...]) * scale
    ki_ref[...] = _dot(s_ref[...], bm_ref[...]) * scale
    kn_ref[...] = nyq_ref[...] * inv * (0.5 / seq_len)


def _filt_dft(ctab, stab, a, bm, colabs, nyq):
    seq_len, hw = a.shape
    tf = min(seq_len, 512)
    kern = functools.partial(_filt_dft_kernel, seq_len)
    one = pl.Buffered(1)
    return pl.pallas_call(
        kern,
        out_shape=(jax.ShapeDtypeStruct((seq_len, hw), F32), jax.ShapeDtypeStruct((seq_len, hw), F32),
                   jax.ShapeDtypeStruct((1, hw), F32)),
        grid=(seq_len // tf,),
        in_specs=[pl.BlockSpec((tf, seq_len), lambda i: (i, 0)),
                  pl.BlockSpec((tf, seq_len), lambda i: (i, 0)),
                  pl.BlockSpec((seq_len, hw), lambda i: (0, 0), pipeline_mode=one),
                  pl.BlockSpec((seq_len, hw), lambda i: (0, 0), pipeline_mode=one),
                  pl.BlockSpec((1, hw), lambda i: (0, 0)),
                  pl.BlockSpec((1, hw), lambda i: (0, 0))],
        out_specs=(pl.BlockSpec((tf, hw), lambda i: (i, 0)), pl.BlockSpec((tf, hw), lambda i: (i, 0)),
                   pl.BlockSpec((1, hw), lambda i: (0, 0))),
        compiler_params=_cp(("arbitrary",)),
        name="filt_dft",
    )(ctab, stab, a, bm, colabs, nyq)


def _dft_tables(seq_len):
    r = 1
    while r * r < seq_len:
        r *= 2
    t = jnp.arange(seq_len, dtype=jnp.int32)[None, :]
    lo = jnp.arange(r, dtype=jnp.int32)[:, None]
    hi = jnp.arange(seq_len // r, dtype=jnp.int32)[:, None] * r
    ang_lo = ((lo * t) % (2 * seq_len)).astype(F32) * (math.pi / seq_len)
    ang_hi = ((hi * t) % (2 * seq_len)).astype(F32) * (math.pi / seq_len)
    c_lo, s_lo = jnp.cos(ang_lo)[None], jnp.sin(ang_lo)[None]
    c_hi, s_hi = jnp.cos(ang_hi)[:, None], jnp.sin(ang_hi)[:, None]
    ctab = (c_hi * c_lo - s_hi * s_lo).reshape(seq_len, seq_len)
    stab = (s_hi * c_lo + c_hi * s_lo).reshape(seq_len, seq_len)
    return ctab.astype(BF16), stab.astype(BF16)


def _hy_fwd_kernel(c_ref, s_ref, v_ref, kr_ref, ki_ref, kn_ref, yre_ref, yim_ref, nyq_ref):
    j = pl.program_id(1)
    tf = c_ref.shape[0]
    v = v_ref[...]
    p = _dot(c_ref[...], v)
    q = _dot(s_ref[...], v)
    kr = kr_ref[...]
    ki = ki_ref[...]
    row = lax.broadcasted_iota(jnp.int32, (tf, 1), 0) + j * tf
    dc = jnp.where(row == 0, 0.5, 1.0)
    yre_ref[0] = ((p * kr + q * ki) * dc).astype(yre_ref.dtype)
    yim_ref[0] = (p * ki - q * kr).astype(yim_ref.dtype)

    @pl.when(j == 0)
    def _():
        n = v.shape[0]
        sgn = (1 - 2 * (lax.broadcasted_iota(jnp.int32, (SUBLANES, n), 1) % 2)).astype(BF16)
        nyq_ref[0] = _dot(sgn, v)[0:1] * kn_ref[...]


def _hy_fwd(ctab, stab, vx, kr, ki, kn, n_seq, row0_blk):
    seq_len, hw = kr.shape
    tf = min(seq_len, 512)
    one = pl.Buffered(1)
    return pl.pallas_call(
        _hy_fwd_kernel,
        out_shape=(jax.ShapeDtypeStruct((n_seq, seq_len, hw), BF16),
                   jax.ShapeDtypeStruct((n_seq, seq_len, hw), BF16),
                   jax.ShapeDtypeStruct((n_seq, 1, hw), F32)),
        grid=(n_seq, seq_len // tf),
        in_specs=[pl.BlockSpec((tf, seq_len), lambda b, j: (j, 0)),
                  pl.BlockSpec((tf, seq_len), lambda b, j: (j, 0)),
                  pl.BlockSpec((seq_len, hw), lambda b, j: (row0_blk + b, 0), pipeline_mode=one),
                  pl.BlockSpec((tf, hw), lambda b, j: (j, 0)),
                  pl.BlockSpec((tf, hw), lambda b, j: (j, 0)),
                  pl.BlockSpec((1, hw), lambda b, j: (0, 0))],
        out_specs=(pl.BlockSpec((1, tf, hw), lambda b, j: (b, j, 0)),
                   pl.BlockSpec((1, tf, hw), lambda b, j: (b, j, 0)),
                   pl.BlockSpec((1, 1, hw), lambda b, j: (b, 0, 0))),
        compiler_params=_cp(("arbitrary", "arbitrary")),
        name="hy_fwd",
    )(ctab, stab, vx, kr, ki, kn)


def _hy_inv_kernel(c_ref, s_ref, yre_ref, yim_ref, nyq_ref, vx_ref, x0_ref, fb_ref, wb_ref, o_ref):
    j = pl.program_id(1)
    tt = c_ref.shape[0]
    z = _dot(c_ref[...], yre_ref[0]) - _dot(s_ref[...], yim_ref[0])
    row = lax.broadcasted_iota(jnp.int32, (tt, 1), 0) + j * tt
    sgn = (1 - 2 * (row % 2)).astype(F32)
    z = z + sgn * nyq_ref[0] + vx_ref[...].astype(F32) * fb_ref[...]
    yb = x0_ref[...].astype(F32) * z
    o_ref[...] = _dot(yb.astype(BF16), wb_ref[...]).astype(o_ref.dtype)


def _hy_inv(ctab, stab, yre, yim, nyq, vx, x0c, fbias, w_bb, t_out, row0_blk):
    n_seq, seq_len, hw = yre.shape
    d = w_bb.shape[1]
    tt = min(seq_len, 256)
    per = seq_len // tt
    one = pl.Buffered(1)
    return pl.pallas_call(
        _hy_inv_kernel,
        out_shape=jax.ShapeDtypeStruct((t_out, d), BF16),
        grid=(n_seq, per),
        in_specs=[pl.BlockSpec((tt, seq_len), lambda b, j: (j, 0)),
                  pl.BlockSpec((tt, seq_len), lambda b, j: (j, 0)),
                  pl.BlockSpec((1, seq_len, hw), lambda b, j: (b, 0, 0), pipeline_mode=one),
                  pl.BlockSpec((1, seq_len, hw), lambda b, j: (b, 0, 0), pipeline_mode=one),
                  pl.BlockSpec((1, 1, hw), lambda b, j: (b, 0, 0)),
                  pl.BlockSpec((tt, hw), lambda b, j: ((row0_blk + b) * per + j, 0)),
                  pl.BlockSpec((tt, hw), lambda b, j: ((row0_blk + b) * per + j, 0)),
                  pl.BlockSpec((1, hw), lambda b, j: (0, 0)),
                  pl.BlockSpec((hw, d), lambda b, j: (0, 0), pipeline_mode=one)],
        out_specs=pl.BlockSpec((tt, d), lambda b, j: (b * per + j, 0)),
        compiler_params=_cp(("arbitrary", "arbitrary")),
        name="hy_inv",
    )(ctab, stab, yre, yim, nyq, vx, x0c, fbias, w_bb)


def _merge_kernel(tok, n_groups, xp_ref, xs_ref, pos_ref, g0_ref, g1_ref, pa_ref, pbc_ref, pbs_ref,
                  gate1_ref, sh2_ref, sc2_ref, gn_ref, wo_ref, wr_ref, br_ref,
                  x2_ref, h2_ref, rt_ref):
    i = pl.program_id(0)
    is_ctx = i < tok.nct
    pb = jnp.where(is_ctx, pbc_ref[...].astype(F32), pbs_ref[...].astype(F32))
    merged = g0_ref[...].astype(F32) * pa_ref[...].astype(F32) + g1_ref[...].astype(F32) * pb
    att = _dot(merged.astype(BF16), wo_ref[...])
    x = jnp.where(is_ctx, xp_ref[...], xs_ref[...] + pos_ref[...])
    x2 = x + gate1_ref[0] * att
    x2_ref[...] = x2
    h2 = _rms(x2, gn_ref[...]) * (1.0 + sc2_ref[0]) + sh2_ref[0]
    h2b = h2.astype(BF16)
    h2_ref[...] = h2b.astype(h2_ref.dtype)
    logits = _dot(h2b, wr_ref[...]) + br_ref[...]
    lane = lax.broadcasted_iota(jnp.int32, logits.shape, 1)
    gl = jnp.where(lane < n_groups, logits, -jnp.inf)
    gmax = jnp.max(gl, axis=-1, keepdims=True)
    gidx = jnp.min(jnp.where(gl == gmax, lane, n_groups), axis=-1, keepdims=True)
    rt_ref[...] = jnp.where(lane == LANES - 1, gidx.astype(F32), logits)


def _merge_out(xp, xs, pos, proj, gate_blk0, pa, pb_ctx, pb_lat, mod, g2, w_out, w_r, b_r, n_groups, tok):
    d = xp.shape[1]
    tm = tok.tm
    t_all = xp.shape[0] + xs.shape[0]
    kern = functools.partial(_merge_kernel, tok, n_groups)
    one = pl.Buffered(1)

    def modspec(k):
        return pl.BlockSpec((1, 1, d), lambda i: (tok.cond(i) * N_MOD + k, 0, 0))

    return pl.pallas_call(
        kern,
        out_shape=(jax.ShapeDtypeStruct((t_all, d), F32), jax.ShapeDtypeStruct((t_all, d), F32),
                   jax.ShapeDtypeStruct((t_all, LANES), F32)),
        grid=(tok.n,),
        in_specs=[
            pl.BlockSpec((tm, d), lambda i: (tok.ctx_idx(i), 0)),
            pl.BlockSpec((tm, d), lambda i: (tok.lat_idx(i), 0)),
            pl.BlockSpec((tm, d), lambda i: (tok.pos_idx(i), 0)),
            pl.BlockSpec((tm, d), lambda i: (i, gate_blk0)),
            pl.BlockSpec((tm, d), lambda i: (i, gate_blk0 + 1)),
            pl.BlockSpec((tm, d), lambda i: (i, 0)),
            pl.BlockSpec((tm, d), lambda i: (tok.ctx_idx(i), 0)),
            pl.BlockSpec((tm, d), lambda i: (tok.lat_idx(i), 0)),
            modspec(2), modspec(3), modspec(4),
            pl.BlockSpec((1, d), lambda i: (0, 0)),
            pl.BlockSpec((d, d), lambda i: (0, 0), pipeline_mode=one),
            pl.BlockSpec((d, LANES), lambda i: (0, 0)),
            pl.BlockSpec((1, LANES), lambda i: (0, 0)),
        ],
        out_specs=(pl.BlockSpec((tm, d), lambda i: (i, 0)), pl.BlockSpec((tm, d), lambda i: (i, 0)),
                   pl.BlockSpec((tm, LANES), lambda i: (i, 0))),
        compiler_params=_cp(("parallel",)),
        name="merge_out",
    )(xp, xs, pos, proj, proj, pa, pb_ctx, pb_lat, mod, mod, mod, g2, w_out, w_r, b_r)


ROW_UNROLL = 8

def _moe_kernel(n_tiles, n_groups, epg, tg_ref, tnv_ref, rt_ref,
                h2_hbm, wr_ref, br_ref, wg_ref, wu_ref, wd_ref, out_hbm,
                xbuf, xb_ref, acc, gws, gsem, ssem, pend):
    i = pl.program_id(0)
    j = pl.program_id(1)
    tm = xb_ref.shape[0]
    nv = tnv_ref[i]
    slot = i % 2

    def gather_copy(tok_row, slot_, r, rows=1):
        return pltpu.make_async_copy(h2_hbm.at[pl.ds(tok_row, rows), :],
                                     xbuf.at[slot_, pl.ds(r, rows), :], gsem.at[slot_])

    def scatter_copy(tok_row, slot_, r, rows=1):
        return pltpu.make_async_copy(acc.at[slot_, pl.ds(r, rows), :],
                                     out_hbm.at[pl.ds(tok_row, rows), :], ssem.at[0])

    def for_rows(count, fn):
        n_full = count // ROW_UNROLL

        def body(b, c):
            for k in range(ROW_UNROLL):
                fn(b * ROW_UNROLL + k)
            return c

        def tail(r, c):
            fn(r)
            return c

        lax.fori_loop(0, n_full, body, 0)
        lax.fori_loop(n_full * ROW_UNROLL, count, tail, 0)

    def wait_rows(count, make):
        p = tm
        while p >= 1:
            @pl.when((count & p) != 0)
            def _(p=p):
                make(p).wait()
            p //= 2

    def wait_scatter(count, slot_):
        wait_rows(count, lambda p: scatter_copy(0, slot_, 0, p))

    @pl.when(j == 0)
    def _():
        @pl.when(i == 0)
        def _():
            pend[0] = 0
            for_rows(tm, lambda r: gather_copy(rt_ref[r], 0, r).start())

        @pl.when(jnp.logical_or(i == 0, tnv_ref[jnp.maximum(i - 1, 0)] > 0))
        def _():
            gather_copy(0, slot, 0, tm).wait()

    @pl.when(jnp.logical_and(j == 0, nv > 0))
    def _():
        grp = tg_ref[i]
        xb = xbuf[slot].astype(BF16)
        xb_ref[...] = xb
        logits = _dot(xb, wr_ref[...]) + br_ref[...]
        lane = lax.broadcasted_iota(jnp.int32, logits.shape, 1)
        gl = jnp.where(lane < n_groups, logits, -jnp.inf)
        gmax = jnp.max(gl, axis=-1, keepdims=True)
        ge = jnp.exp(gl - gmax)
        g_w = jnp.sum(jnp.where(lane == grp, ge, 0.0), axis=-1, keepdims=True) / jnp.sum(ge, axis=-1, keepdims=True)
        base = n_groups + grp * epg
        e = [jnp.sum(jnp.where(lane == base + k, logits, 0.0), axis=-1, keepdims=True) for k in range(epg)]
        sel = []
        for k in range(epg):
            rank = jnp.zeros_like(e[k])
            for m in range(epg):
                if m == k:
                    continue
                ahead = (e[m] > e[k]) if m > k else (e[m] >= e[k])
                rank = rank + ahead.astype(F32)
            sel.append(rank < TOP_K_INNER)
        emax = functools.reduce(jnp.maximum, e)
        ex = [jnp.where(sel[k], jnp.exp(e[k] - emax), 0.0) for k in range(epg)]
        den = functools.reduce(lambda a, b: a + b, ex)
        for k in range(epg):
            gws[k] = ex[k] / den * g_w
        acc[slot] = jnp.zeros(acc.shape[1:], acc.dtype)

    @pl.when(nv > 0)
    def _():
        per = tm // epg
        for k in range(per):
            r = j * per + k
            gather_copy(rt_ref[(i + 1) * tm + r], 1 - slot, r).start()
        xb = xb_ref[...]
        g = _dot(xb, wg_ref[...])
        u = _dot(xb, wu_ref[...])
        hid = (g * jax.nn.sigmoid(g)) * u * gws[j]
        acc[slot] += _dot(hid.astype(BF16), wd_ref[...])

    @pl.when(j == epg - 1)
    def _():
        wait_scatter(pend[0], 1 - slot)
        for_rows(nv, lambda r: scatter_copy(rt_ref[i * tm + r], slot, r).start())
        pend[0] = nv

        @pl.when(i == n_tiles - 1)
        def _():
            wait_scatter(nv, slot)


def _moe(h2, w_r, b_r, w_g, w_u, w_d, tile_grp, tile_nv, row_tok, n_groups, epg, tm):
    t, d = h2.shape
    ff = w_g.shape[-1]
    n_tiles = tile_grp.shape[0]
    assert tm & (tm - 1) == 0, "row-count waits decompose tm in binary"
    kern = functools.partial(_moe_kernel, n_tiles, n_groups, epg)

    def widx(i, j, tg, tnv, rt):
        return tg[i] * epg + jnp.where(tnv[i] > 0, j, epg - 1)

    grid_spec = pltpu.PrefetchScalarGridSpec(
        num_scalar_prefetch=3,
        grid=(n_tiles, epg),
        in_specs=[
            pl.BlockSpec(memory_space=pl.ANY),
            pl.BlockSpec((d, LANES), lambda i, j, tg, tnv, rt: (0, 0)),
            pl.BlockSpec((1, LANES), lambda i, j, tg, tnv, rt: (0, 0)),
            pl.BlockSpec((None, d, ff), lambda i, j, tg, tnv, rt: (widx(i, j, tg, tnv, rt), 0, 0)),
            pl.BlockSpec((None, d, ff), lambda i, j, tg, tnv, rt: (widx(i, j, tg, tnv, rt), 0, 0)),
            pl.BlockSpec((None, ff, d), lambda i, j, tg, tnv, rt: (widx(i, j, tg, tnv, rt), 0, 0)),
        ],
        out_specs=pl.BlockSpec(memory_space=pl.ANY),
        scratch_shapes=[
            pltpu.VMEM((2, tm, d), F32),
            pltpu.VMEM((tm, d), BF16),
            pltpu.VMEM((2, tm, d), F32),
            pltpu.VMEM((epg, tm, 1), F32),
            pltpu.SemaphoreType.DMA((2,)),
            pltpu.SemaphoreType.DMA((1,)),
            pltpu.SMEM((1,), jnp.int32),
        ],
    )
    return pl.pallas_call(
        kern,
        out_shape=jax.ShapeDtypeStruct((t, d), F32),
        grid_spec=grid_spec,
        compiler_params=_cp(("arbitrary", "arbitrary")),
        name="moe",
    )(tile_grp, tile_nv, row_tok, h2, w_r, b_r, w_g, w_u, w_d)


def _moe_plan(grp, n_groups, tm, n_tiles):
    t = grp.shape[0]
    onehot = (grp[:, None] == jnp.arange(n_groups, dtype=jnp.int32)[None, :]).astype(jnp.int32)
    csum = jnp.cumsum(onehot, axis=0)
    counts = csum[-1]
    rank = jnp.take_along_axis(csum, grp[:, None], axis=1)[:, 0] - 1
    ntile_g = (counts + tm - 1) // tm
    tile_end = jnp.cumsum(ntile_g)
    tile_off = tile_end - ntile_g
    slot = tile_off[grp] * tm + rank
    row_tok = jnp.zeros((n_tiles * tm,), jnp.int32).at[slot].set(jnp.arange(t, dtype=jnp.int32))
    tiles = jnp.arange(n_tiles, dtype=jnp.int32)
    tile_grp = jnp.minimum(jnp.sum((tiles[:, None] >= tile_end[None, :]).astype(jnp.int32), axis=1),
                           n_groups - 1)
    tile_nv = jnp.clip(counts[tile_grp] - (tiles - tile_off[tile_grp]) * tm, 0, tm)
    tile_nv = jnp.where(tiles < tile_end[-1], tile_nv, 0)
    return tile_grp.astype(jnp.int32), tile_nv.astype(jnp.int32), row_tok


def _final_kernel(x2_ref, m_ref, gate2_ref, gf_ref, o_ref):
    x3 = x2_ref[...] + gate2_ref[0] * m_ref[...]
    o_ref[...] = _rms(x3, gf_ref[...])


def _final(x2, moe, mod, g_final, row_blk0, n_rows, cond_of_tile, tm):
    d = x2.shape[1]
    return pl.pallas_call(
        _final_kernel,
        out_shape=jax.ShapeDtypeStruct((n_rows, d), F32),
        grid=(n_rows // tm,),
        in_specs=[pl.BlockSpec((tm, d), lambda i: (row_blk0 + i, 0)),
                  pl.BlockSpec((tm, d), lambda i: (row_blk0 + i, 0)),
                  pl.BlockSpec((1, 1, d), lambda i: (cond_of_tile(i) * N_MOD + 5, 0, 0)),
                  pl.BlockSpec((1, d), lambda i: (0, 0))],
        out_specs=pl.BlockSpec((tm, d), lambda i: (i, 0)),
        compiler_params=_cp(("parallel",)),
        name="final",
    )(x2, moe, mod, g_final)


def _grid_pos_embed(n_tokens, dim):
    rows_n = n_tokens // GRID_W
    quarter = dim // 4
    omega = 1.0 / (10000.0 ** (jnp.arange(quarter, dtype=F32) / quarter))

    def emb(n):
        ang = jnp.arange(n).astype(F32)[:, None] * omega[None, :]
        return jnp.concatenate([jnp.sin(ang), jnp.cos(ang)], axis=-1)

    e_r = jnp.broadcast_to(emb(rows_n)[:, None, :], (rows_n, GRID_W, dim // 2))
    e_c = jnp.broadcast_to(emb(GRID_W)[None, :, :], (rows_n, GRID_W, dim // 2))
    return jnp.concatenate([e_r, e_c], axis=-1).reshape(n_tokens, dim)


def _state_to_lanes(s):
    b, _, _, g, p = s.shape
    return jnp.transpose(s, (0, 3, 2, 1, 4)).reshape(b, g, 4 * p)


def _lanes_to_state(x, p):
    b, g, _ = x.shape
    return jnp.transpose(x.reshape(b, g, 2, 2, p), (0, 3, 2, 1, 4))


def kernel(x_prompt, x_sample, state_ssm, c, c_ctx, w_ada, b_ada, g_norm1, g_norm2, w_in, ssm_a_re, ssm_a_im, ssm_log_dt, ssm_b_re, ssm_b_im, ssm_c_re, ssm_c_im, ssm_d, w_glu, b_glu, w_short, b_short, w_f1, b_f1, freq1, w_f2, b_f2, freq2, w_f3, b_f3, filter_bias, w_branch_a, w_branch_b, w_out, w_router_group, b_router_group, w_router_expert, b_router_expert, w_exp_gate, w_exp_up, w_exp_down, g_final):
    depth = w_ada.shape[0]
    assert depth == 1, "single-layer pipeline"
    bsz, seq, d = x_prompt.shape
    dbsz, dseq, _ = x_sample.shape
    g_all, p_state = ssm_a_re.shape[2:]
    ssm_w = ssm_d.shape[1]
    hw = filter_bias.shape[1]
    n_groups, _, epg = w_router_expert.shape[1:]
    ff = w_exp_gate.shape[-1]
    assert ssm_w // g_all * CHUNK == MXU_SIDE and 4 * p_state == MXU_SIDE
    assert ssm_w == hw and d == 2 * hw
    n_ctx = bsz * seq
    n_lat = dbsz * dseq
    t_all = n_ctx + n_lat

    xp = x_prompt.reshape(n_ctx, d)
    xs = x_sample.reshape(n_lat, d)
    pos = _grid_pos_embed(dseq, d)

    n_cond = 1 + dbsz
    cond8 = jnp.zeros((SUBLANES, d), F32).at[0].set(c_ctx).at[1:n_cond].set(c)
    mod = _ada_mod(cond8, w_ada[0], b_ada[0][None])
    mod = mod[:n_cond].reshape(n_cond * N_MOD, 1, d)

    tm_e = min(256, seq, dseq)
    tok_e = _Tok(n_ctx, n_lat, dseq, tm_e)
    gate_col0 = ssm_w + 3 * hw
    h1 = _norm_mod(xp, xs, pos, mod, g_norm1, tok_e)
    tm_p, tn_p = min(1024, t_all), min(1024, hw)
    n_in = w_in.shape[2]
    u_a = _in_proj(h1, w_in[0], (0, 0), (0, ssm_w), F32, tm_p, tn_p, "in_proj_a")
    proj = _in_proj(h1, w_in[0], (gate_col0, n_in), (ssm_w, gate_col0), BF16, tm_p, tn_p, "in_proj")
    n_gate_cols = n_in - gate_col0

    w_state, w_so, a_step = _ssm_operators(ssm_a_re[0], ssm_a_im[0], ssm_log_dt[0], ssm_b_re[0], ssm_b_im[0],
                                           ssm_c_re[0], ssm_c_im[0], ssm_d[0])
    nc_ctx = seq // CHUNK
    nc_lat = dseq // CHUNK
    seq_per_blk = max(1, min(bsz, MXU_SIDE // nc_ctx))
    assert bsz % seq_per_blk == 0 and n_ctx % dseq == 0
    s0_ctx = jnp.zeros((bsz, g_all, 4 * p_state), F32)
    y_ctx, fin_ctx = _ssm_call(u_a, 0, n_ctx, w_state, w_so, a_step, s0_ctx, seq_per_blk, nc_ctx, "ssm_ctx")
    s0_lat = _state_to_lanes(state_ssm[:, 0].astype(F32))
    y_lat, _ = _ssm_call(u_a, n_ctx // dseq, n_lat, w_state, w_so, a_step, s0_lat, 1, nc_lat, "ssm_lat")
    new_state = _lanes_to_state(fin_ctx, p_state)[:, None]

    pa = _glu_branch(y_ctx, y_lat, w_glu[0].astype(BF16), b_glu[0][None], w_branch_a[0].astype(BF16), tok_e)

    w_short3 = jnp.transpose(w_short[0].reshape(-1, 3, hw), (1, 0, 2))
    b_short3 = b_short[0].reshape(3, 1, hw)
    vx, x0c = _short_conv(proj, n_gate_cols // hw, hw, w_short3, b_short3, n_ctx, seq, dseq, tm_e)
    rates = jnp.abs(jnp.linspace(math.log(DECAY_TARGET) / DECAY_FAST, math.log(DECAY_TARGET) / DECAY_SLOW,
                                 hw, dtype=F32))
    rates2 = jnp.concatenate([rates, rates])[None]
    fbias = filter_bias[0][None].astype(F32)
    w_bb = w_branch_b[0].astype(BF16)
    pbs = []
    for seq_len, n_seq, row0 in ((seq, bsz, 0), (dseq, dbsz, n_ctx // dseq)):
        ctab, stab = _dft_tables(seq_len)
        a_f, bm_f, colabs, nyq_f = _filt_gen(seq_len, w_f1[0], b_f1[0][None], freq1[0][None], w_f2[0],
                                            b_f2[0][None], freq2[0][None], w_f3[0], b_f3[0][None], rates2)
        kr, ki, kn = _filt_dft(ctab, stab, a_f, bm_f, colabs, nyq_f)
        yre, yim, nyq = _hy_fwd(ctab, stab, vx, kr, ki, kn, n_seq, row0)
        pbs.append(_hy_inv(ctab, stab, yre, yim, nyq, vx, x0c, fbias, w_bb, n_seq * seq_len, row0))
    pb_ctx, pb_lat = pbs

    w_r = jnp.zeros((d, LANES), F32)
    w_r = w_r.at[:, :n_groups].set(w_router_group[0])
    w_r = w_r.at[:, n_groups:n_groups + n_groups * epg].set(
        jnp.transpose(w_router_expert[0], (1, 0, 2)).reshape(d, n_groups * epg))
    b_r = jnp.zeros((1, LANES), F32)
    b_r = b_r.at[0, :n_groups].set(b_router_group[0])
    b_r = b_r.at[0, n_groups:n_groups + n_groups * epg].set(b_router_expert[0].reshape(-1))
    w_r = w_r.astype(BF16)
    x2, h2, rout = _merge_out(xp, xs, pos, proj, 0, pa, pb_ctx, pb_lat, mod, g_norm2,
                              w_out[0].astype(BF16), w_r, b_r, n_groups, tok_e)

    tm_m = min(512, t_all // n_groups)
    n_tiles = t_all // tm_m + n_groups
    grp = rout[:, LANES - 1].astype(jnp.int32)
    tile_grp, tile_nv, row_tok = _moe_plan(grp, n_groups, tm_m, n_tiles)
    moe = _moe(h2, w_r, b_r, w_exp_gate[0].astype(BF16), w_exp_up[0].astype(BF16),
               w_exp_down[0].astype(BF16), tile_grp, tile_nv, row_tok, n_groups, epg, tm_m)

    gf = g_final[None]
    y_prompt = _final(x2, moe, mod, gf, 0, n_ctx, lambda i: 0, tm_e)
    lat_tiles = dseq // tm_e
    y_sample = _final(x2, moe, mod, gf, n_ctx // tm_e, n_lat, lambda i: 1 + i // lat_tiles, tm_e)
    return (y_prompt.reshape(bsz, seq, d), y_sample.reshape(dbsz, dseq, d), new_state)
```

```python
import functools
import math

import numpy as np
import jax
import jax.numpy as jnp
from jax import lax
from jax.experimental import pallas as pl
from jax.experimental.pallas import tpu as pltpu

F32 = jnp.float32
BF16 = jnp.bfloat16
EPS = 1e-6
GRID_W = 64
N_MOD = 6
TOP_K_INNER = 2
DECAY_TARGET = 1e-2
DECAY_FAST = 0.3
DECAY_SLOW = 1.5
LANES = 128
SUBLANES = 8
MXU_SIDE = 256
VMEM_LIMIT = 56 * 1024 * 1024


def _cp(sem, vmem=VMEM_LIMIT):
    return pltpu.CompilerParams(dimension_semantics=sem, vmem_limit_bytes=vmem)


def _dot(a, b):
    return jnp.dot(a, b, preferred_element_type=F32)


def _split(a):
    hi = a.astype(BF16)
    lo = (a - hi.astype(F32)).astype(BF16)
    return hi, lo


def _dot3(a, b):
    a_hi, a_lo = _split(a)
    b_hi, b_lo = _split(b)
    return _dot(a_hi, b_hi) + _dot(a_lo, b_hi) + _dot(a_hi, b_lo)


def _rms(x, g):
    ms = jnp.mean(x * x, axis=-1, keepdims=True)
    return x * lax.rsqrt(ms + EPS) * g


def _pack_halves(xb):
    n = xb.shape[1] // 2
    lo = lax.bitcast_convert_type(xb[:, :n].astype(F32), jnp.uint32)
    hi = lax.bitcast_convert_type(xb[:, n:].astype(F32), jnp.uint32)
    return (lo >> 16) | (hi & jnp.uint32(0xFFFF0000))


def _unpack_halves(w):
    lo = lax.bitcast_convert_type(w << 16, F32)
    hi = lax.bitcast_convert_type(w & jnp.uint32(0xFFFF0000), F32)
    return jnp.concatenate([lo, hi], axis=1).astype(BF16)


def _ada_kernel(c_ref, w_ref, b_ref, o_ref):
    c = c_ref[...]
    a = c * jax.nn.sigmoid(c)
    o_ref[...] = _dot3(a, w_ref[...]) + b_ref[...]


def _ada_mod(cond8, w_ada, b_ada):
    d, n = w_ada.shape
    tn = min(n, 1024)
    return pl.pallas_call(
        _ada_kernel,
        out_shape=jax.ShapeDtypeStruct((SUBLANES, n), F32),
        grid=(n // tn,),
        in_specs=[pl.BlockSpec((SUBLANES, d), lambda j: (0, 0)),
                  pl.BlockSpec((d, tn), lambda j: (0, j)),
                  pl.BlockSpec((1, tn), lambda j: (0, j))],
        out_specs=pl.BlockSpec((SUBLANES, tn), lambda j: (0, j)),
        compiler_params=_cp(("arbitrary",)),
        name="ada_mod",
    )(cond8, w_ada, b_ada)


class _Tok:
    def __init__(self, n_ctx, n_lat, lat_len, tm):
        assert n_ctx % tm == 0 and lat_len % tm == 0 and n_lat % lat_len == 0
        self.tm = tm
        self.nct = n_ctx // tm
        self.nst = n_lat // tm
        self.per_seq = lat_len // tm
        self.n = self.nct + self.nst

    def ctx_idx(self, i):
        return jnp.minimum(i, self.nct - 1)

    def lat_idx(self, i):
        return jnp.maximum(i - self.nct, 0)

    def pos_idx(self, i):
        return self.lat_idx(i) % self.per_seq

    def cond(self, i):
        return jnp.where(i < self.nct, 0, 1 + self.lat_idx(i) // self.per_seq)


def _norm_mod_kernel(tok, xp_ref, xs_ref, pos_ref, sh_ref, sc_ref, g_ref, h_ref):
    i = pl.program_id(0)

    def modulate(x):
        y = _rms(x, g_ref[...])
        return (y * (1.0 + sc_ref[0]) + sh_ref[0]).astype(h_ref.dtype)

    @pl.when(i < tok.nct)
    def _():
        h_ref[...] = modulate(xp_ref[...])

    @pl.when(i >= tok.nct)
    def _():
        h_ref[...] = modulate(xs_ref[...] + pos_ref[...])


def _norm_mod(xp, xs, pos, mod, g1, tok):
    d = xp.shape[1]
    tm = tok.tm
    t_all = xp.shape[0] + xs.shape[0]
    return pl.pallas_call(
        functools.partial(_norm_mod_kernel, tok),
        out_shape=jax.ShapeDtypeStruct((t_all, d), BF16),
        grid=(tok.n,),
        in_specs=[
            pl.BlockSpec((tm, d), lambda i: (tok.ctx_idx(i), 0)),
            pl.BlockSpec((tm, d), lambda i: (tok.lat_idx(i), 0)),
            pl.BlockSpec((tm, d), lambda i: (tok.pos_idx(i), 0)),
            pl.BlockSpec((1, 1, d), lambda i: (tok.cond(i) * N_MOD + 0, 0, 0)),
            pl.BlockSpec((1, 1, d), lambda i: (tok.cond(i) * N_MOD + 1, 0, 0)),
            pl.BlockSpec((1, d), lambda i: (0, 0)),
        ],
        out_specs=pl.BlockSpec((tm, d), lambda i: (i, 0)),
        compiler_params=_cp(("parallel",)),
        name="norm_mod",
    )(xp, xs, pos, mod, mod, g1)


def _in_proj_kernel(n_sig, h_ref, w_ref, o_ref, wb_ref):
    n = pl.program_id(0)
    m = pl.program_id(1)

    @pl.when(m == 0)
    def _():
        wb_ref[...] = w_ref[...].astype(BF16)

    acc = _dot(h_ref[...], wb_ref[...])

    @pl.when(n >= n_sig)
    def _():
        o_ref[...] = acc.astype(o_ref.dtype)

    @pl.when(n < n_sig)
    def _():
        o_ref[...] = jax.nn.sigmoid(acc).astype(o_ref.dtype)


def _in_proj(h, w_in, sig_cols, lin_cols, out_dtype, tm, tn, name):
    t_all, d = h.shape
    for c in sig_cols + lin_cols:
        assert c % tn == 0
    assert t_all % tm == 0
    n_sig = (sig_cols[1] - sig_cols[0]) // tn
    n_lin = (lin_cols[1] - lin_cols[0]) // tn
    sig0, lin0 = sig_cols[0] // tn, lin_cols[0] // tn

    def w_blk(n):
        return jnp.where(n < n_sig, sig0 + n, lin0 + n - n_sig)

    return pl.pallas_call(
        functools.partial(_in_proj_kernel, n_sig),
        out_shape=jax.ShapeDtypeStruct((t_all, (n_sig + n_lin) * tn), out_dtype),
        grid=(n_sig + n_lin, t_all // tm),
        in_specs=[pl.BlockSpec((tm, d), lambda n, m: (m, 0)),
                  pl.BlockSpec((d, tn), lambda n, m: (0, w_blk(n)))],
        out_specs=pl.BlockSpec((tm, tn), lambda n, m: (m, n)),
        scratch_shapes=[pltpu.VMEM((d, tn), BF16)],
        compiler_params=_cp(("arbitrary", "arbitrary")),
        name=name,
    )(h, w_in)


CHUNK = 16
GROUP_PAD = 8


def _ssm_operators(a_re, a_im, log_dt, b_re, b_im, c_re, c_im, d_skip):
    t = CHUNK
    g, p = a_re.shape[1:]
    c = b_re.shape[-1]
    lam = lax.complex(a_re.astype(F32), a_im.astype(F32))
    dt = jnp.exp(log_dt.astype(F32))[..., None]
    ld = lam * dt
    lam_bar = jnp.exp(ld)
    b_bar = ((lam_bar - 1.0) / lam)[..., None] * lax.complex(b_re.astype(F32), b_im.astype(F32))
    c_mat = lax.complex(c_re.astype(F32), c_im.astype(F32))
    taus = jnp.arange(t + 1, dtype=F32)
    e_pow = jnp.exp(taus[None, None, :, None] * ld[:, :, None, :])

    k_all = jnp.real(jnp.einsum('dgop,dgtp,dgpi->dgtoi', c_mat, e_pow[:, :, :t], b_bar))
    kf, kb = k_all[0], k_all[1]
    ii = np.arange(t)[None, :, None]
    jj = np.arange(t)[None, None, :]
    tt = np.arange(t)[:, None, None]
    place_f = jnp.asarray((jj - ii == tt).astype(np.float32))
    place_b = jnp.asarray((ii - jj == tt).astype(np.float32))
    kf = kf.at[:, 0].add(jnp.eye(c, dtype=F32)[None] * d_skip.astype(F32).reshape(g, c, 1))
    m = jnp.einsum('tij,gtoc->gicjo', place_f, kf) + jnp.einsum('tij,gtoc->gicjo', place_b, kb)
    m = m.reshape(g, t * c, t * c)

    e_f = e_pow[0][:, ::-1][:, 1:]
    e_b = e_pow[1][:, :t]
    ws_f = e_f[:, :, :, None] * b_bar[0][:, None]
    ws_b = e_b[:, :, :, None] * b_bar[1][:, None]

    def rows_ic(x):
        return jnp.transpose(x, (0, 1, 3, 2)).reshape(g, t * c, p)

    w_state = jnp.concatenate([rows_ic(jnp.real(ws_f)), rows_ic(jnp.real(ws_b)),
                               rows_ic(jnp.imag(ws_f)), rows_ic(jnp.imag(ws_b))], axis=-1)

    ce_f = c_mat[0][:, None] * e_pow[0][:, 1:, None, :]
    ce_b = c_mat[1][:, None] * e_pow[1][:, ::-1][:, :t, None, :]

    def cols_jc(x):
        return jnp.transpose(x, (0, 3, 1, 2)).reshape(g, p, t * c)

    w_so = jnp.concatenate([cols_jc(jnp.real(ce_f)), cols_jc(jnp.real(ce_b)),
                            cols_jc(-jnp.imag(ce_f)), cols_jc(-jnp.imag(ce_b))], axis=1)
    w_out = jnp.concatenate([m, w_so], axis=1)

    a_t = e_pow[:, :, t]
    a_step = jnp.stack([jnp.concatenate([jnp.real(a_t[0]), jnp.real(a_t[1])], axis=-1),
                        jnp.concatenate([jnp.imag(a_t[0]), jnp.imag(a_t[1])], axis=-1)])
    return w_state.astype(BF16), w_out.astype(BF16), a_step


def _ssm_kernel(n_seq, n_chunk, gb, g_all, x_ref, ws_ref, wo_ref, a_ref, s0_ref,
                y_ref, fin_ref, xq_ref, za_ref, zb_ref):
    phase = pl.program_id(1)
    j = pl.program_id(2)
    gp = g_all + GROUP_PAD
    nck = n_seq * n_chunk
    half = LANES // 2
    c = LANES // gb

    @pl.when(phase == 0)
    def _():
        xs = [x_ref[pl.ds(i, nck, stride=CHUNK), :] for i in range(CHUNK)]
        for q in range(gb):
            xq = jnp.concatenate([xs[i][:, q * c:(q + 1) * c] for i in range(CHUNK)], axis=-1).astype(BF16)
            xq_ref[j * gb + q] = xq
            z = _dot(xq, ws_ref[q])
            za_ref[pl.ds(j * gb + q, nck, stride=gp), :] = z[:, :LANES]
            zb_ref[pl.ds(j * gb + q, nck, stride=gp), :] = z[:, LANES:]

    @pl.when(jnp.logical_and(phase == 1, j == 0))
    def _():
        a_re = a_ref[0]
        a_im = a_ref[1]
        lane = lax.broadcasted_iota(jnp.int32, (g_all, LANES), 1)
        is_f = lane < half
        for s in range(n_seq):
            def step(k, carry):
                s_a, s_b = carry
                rf = pl.multiple_of((s * n_chunk + k) * gp, SUBLANES)
                rb = pl.multiple_of((s * n_chunk + n_chunk - 1 - k) * gp, SUBLANES)
                zf_a = za_ref[pl.ds(rf, g_all), :]
                zf_b = zb_ref[pl.ds(rf, g_all), :]
                zb_a = za_ref[pl.ds(rb, g_all), :]
                zb_b = zb_ref[pl.ds(rb, g_all), :]
                za_ref[pl.ds(rf, g_all), :] = jnp.where(is_f, s_a, zf_a)
                zb_ref[pl.ds(rf, g_all), :] = jnp.where(is_f, s_b, zf_b)
                za_ref[pl.ds(rb, g_all), :] = jnp.where(is_f, zb_a, s_a)
                zb_ref[pl.ds(rb, g_all), :] = jnp.where(is_f, zb_b, s_b)
                z_a = jnp.where(is_f, zf_a, zb_a)
                z_b = jnp.where(is_f, zf_b, zb_b)
                n_a = a_re * s_a - a_im * s_b + z_a
                n_b = a_re * s_b + a_im * s_a + z_b
                return n_a, n_b

            init = s0_ref[s]
            f_a, f_b = lax.fori_loop(0, n_chunk, step, (init[:, :LANES], init[:, LANES:]))
            fin_ref[s] = jnp.concatenate([f_a, f_b], axis=-1)

    @pl.when(phase == 1)
    def _():
        ys = []
        for q in range(gb):
            st_a = za_ref[pl.ds(j * gb + q, nck, stride=gp), :].astype(BF16)
            st_b = zb_ref[pl.ds(j * gb + q, nck, stride=gp), :].astype(BF16)
            lhs = jnp.concatenate([xq_ref[j * gb + q], st_a, st_b], axis=-1)
            ys.append(_dot(lhs, wo_ref[q]))
        for i in range(CHUNK):
            yi = jnp.concatenate([ys[q][:, i * c:(i + 1) * c] for q in range(gb)], axis=-1)
            y_ref[pl.ds(i, nck, stride=CHUNK), :] = yi


def _ssm_call(u, row_blk0, n_rows, w_state, w_out, a_step, s0, n_seq, n_chunk, name):
    g_all, w, _ = w_state.shape
    nck = n_seq * n_chunk
    blk = nck * CHUNK
    n_blk = n_rows // blk
    gb = LANES * CHUNK // w
    ngb = g_all // gb
    gp = g_all + GROUP_PAD
    kern = functools.partial(_ssm_kernel, n_seq, n_chunk, gb, g_all)
    return pl.pallas_call(
        kern,
        out_shape=(jax.ShapeDtypeStruct((n_rows, u.shape[1]), F32),
                   jax.ShapeDtypeStruct((n_blk * n_seq, g_all, w), F32)),
        grid=(n_blk, 2, ngb),
        in_specs=[
            pl.BlockSpec((blk, LANES), lambda b, ph, j: (row_blk0 + b, jnp.where(ph == 0, j, ngb - 1))),
            pl.BlockSpec((gb, w, w), lambda b, ph, j: (jnp.where(ph == 0, j, ngb - 1), 0, 0)),
            pl.BlockSpec((gb, 2 * w, w), lambda b, ph, j: (jnp.where(ph == 0, 0, j), 0, 0)),
            pl.BlockSpec((2, g_all, LANES), lambda b, ph, j: (0, 0, 0)),
            pl.BlockSpec((n_seq, g_all, w), lambda b, ph, j: (b, 0, 0)),
        ],
        out_specs=(
            pl.BlockSpec((blk, LANES), lambda b, ph, j: (b, jnp.where(ph == 0, 0, j))),
            pl.BlockSpec((n_seq, g_all, w), lambda b, ph, j: (b, 0, 0)),
        ),
        scratch_shapes=[pltpu.VMEM((g_all, nck, w), BF16),
                        pltpu.VMEM((nck * gp, LANES), F32), pltpu.VMEM((nck * gp, LANES), F32)],
        compiler_params=_cp(("arbitrary", "arbitrary", "arbitrary")),
        name=name,
    )(u, w_state, w_out, a_step, s0)


def _glu_kernel(nct, yc_ref, yl_ref, wg_ref, bg_ref, wb_ref, o_ref):
    i = pl.program_id(0)
    y = jax.nn.gelu(jnp.where(i < nct, yc_ref[...], yl_ref[...]))
    z = _dot(y.astype(BF16), wg_ref[...]) + bg_ref[...]
    ya = y * jax.nn.sigmoid(z)
    o_ref[...] = _dot(ya.astype(BF16), wb_ref[...]).astype(o_ref.dtype)


def _glu_branch(y_ctx, y_lat, w_glu, b_glu, w_ba, tok):
    c = y_ctx.shape[1]
    d = w_ba.shape[1]
    tm = tok.tm
    return pl.pallas_call(
        functools.partial(_glu_kernel, tok.nct),
        out_shape=jax.ShapeDtypeStruct((y_ctx.shape[0] + y_lat.shape[0], d), BF16),
        grid=(tok.n,),
        in_specs=[pl.BlockSpec((tm, c), lambda i: (tok.ctx_idx(i), 0)),
                  pl.BlockSpec((tm, c), lambda i: (tok.lat_idx(i), 0)),
                  pl.BlockSpec((c, c), lambda i: (0, 0)),
                  pl.BlockSpec((1, c), lambda i: (0, 0)),
                  pl.BlockSpec((c, d), lambda i: (0, 0))],
        out_specs=pl.BlockSpec((tm, d), lambda i: (i, 0)),
        compiler_params=_cp(("parallel",)),
        name="glu_branch",
    )(y_ctx, y_lat, w_glu, b_glu, w_ba)


def _short_conv_kernel(nct, per_ctx, per_lat, v_ref, x0_ref, x1_ref, vp_ref, x0p_ref, x1p_ref,
                       vn_ref, x0n_ref, x1n_ref, w_ref, b_ref, vx_ref, x0c_ref):
    i = pl.program_id(0)
    tm = v_ref.shape[0]
    k = jnp.where(i < nct, i % per_ctx, (i - nct) % per_lat)
    per = jnp.where(i < nct, per_ctx, per_lat)
    first = k == 0
    last = k == per - 1
    row = lax.broadcasted_iota(jnp.int32, (tm, 1), 0)

    def conv(cur_ref, prev_ref, next_ref, part):
        x = cur_ref[...].astype(F32)
        hp = jnp.where(first, 0.0, prev_ref[SUBLANES - 1:SUBLANES, :].astype(F32))
        hn = jnp.where(last, 0.0, next_ref[0:1, :].astype(F32))
        xm = jnp.where(row == 0, hp, pltpu.roll(x, 1, axis=0))
        xq = jnp.where(row == tm - 1, hn, pltpu.roll(x, tm - 1, axis=0))
        w = w_ref[part]
        return b_ref[part] + xm * w[0:1, :] + x * w[1:2, :] + xq * w[2:3, :]

    v = conv(v_ref, vp_ref, vn_ref, 0)
    x0 = conv(x0_ref, x0p_ref, x0n_ref, 1)
    x1 = conv(x1_ref, x1p_ref, x1n_ref, 2)
    vx_ref[...] = (v * x1).astype(vx_ref.dtype)
    x0c_ref[...] = x0.astype(x0c_ref.dtype)


def _short_conv(proj, col_blk0, hw, w_short3, b_short3, n_ctx, seq, dseq, tm):
    t = proj.shape[0]
    assert seq % tm == 0 and dseq % tm == 0
    nt = t // tm
    r8 = tm // SUBLANES
    nb8 = t // SUBLANES

    def cur(k):
        return pl.BlockSpec((tm, hw), lambda i: (i, col_blk0 + k))

    def prev(k):
        return pl.BlockSpec((SUBLANES, hw), lambda i: (jnp.maximum(i * r8 - 1, 0), col_blk0 + k))

    def nxt(k):
        return pl.BlockSpec((SUBLANES, hw), lambda i: (jnp.minimum((i + 1) * r8, nb8 - 1), col_blk0 + k))

    kern = functools.partial(_short_conv_kernel, n_ctx // tm, seq // tm, dseq // tm)
    return pl.pallas_call(
        kern,
        out_shape=(jax.ShapeDtypeStruct((t, hw), BF16), jax.ShapeDtypeStruct((t, hw), BF16)),
        grid=(nt,),
        in_specs=[cur(0), cur(1), cur(2), prev(0), prev(1), prev(2), nxt(0), nxt(1), nxt(2),
                  pl.BlockSpec((3, 3, hw), lambda i: (0, 0, 0)),
                  pl.BlockSpec((3, 1, hw), lambda i: (0, 0, 0))],
        out_specs=(pl.BlockSpec((tm, hw), lambda i: (i, 0)), pl.BlockSpec((tm, hw), lambda i: (i, 0))),
        compiler_params=_cp(("parallel",)),
        name="short_conv",
    )(proj, proj, proj, proj, proj, proj, proj, proj, proj, w_short3, b_short3)


def _filt_gen_kernel(seq_len, n_bands, wt_ref, wc_ref, ws_ref, b1_ref, f1_ref, w2_ref, b2_ref, f2_ref,
                     w3_ref, b3_ref, rate_ref, a_ref, bm_ref, abs_ref, nyq_ref):
    i = pl.program_id(0)
    tl = a_ref.shape[0]
    hw = a_ref.shape[1]
    pos = lax.broadcasted_iota(jnp.int32, (tl, 1), 0) + i * tl
    t = pos.astype(F32) / seq_len
    bands = (lax.broadcasted_iota(jnp.int32, (1, wc_ref.shape[0]), 1) + 1).astype(F32)
    ang = (2.0 * math.pi) * t * bands
    pre = t * wt_ref[...] + _dot3(jnp.cos(ang), wc_ref[...]) + _dot3(jnp.sin(ang), ws_ref[...])
    h = jnp.sin(f1_ref[...] * (pre + b1_ref[...]))
    h = jnp.sin(f2_ref[...] * (_dot3(h, w2_ref[...]) + b2_ref[...]))
    h = _dot3(h, w3_ref[...]) + b3_ref[...]
    h = h * jnp.exp(-t * rate_ref[...])
    h_f = h[:, :hw]
    h_b = h[:, hw:]
    colabs = jnp.sum(jnp.abs(h_f) + jnp.abs(h_b), axis=0, keepdims=True)
    h_bp = jnp.where(pos == 0, 0.0, h_b)
    a = h_f + h_bp
    sgn = (1 - 2 * (pos % 2)).astype(F32)
    nyq = jnp.sum(a * sgn, axis=0, keepdims=True)
    a_ref[...] = a.astype(a_ref.dtype)
    bm_ref[...] = (h_bp - h_f).astype(bm_ref.dtype)

    @pl.when(i == 0)
    def _():
        abs_ref[...] = colabs
        nyq_ref[...] = nyq

    @pl.when(i > 0)
    def _():
        abs_ref[...] += colabs
        nyq_ref[...] += nyq


def _filt_gen(seq_len, w_f1, b_f1, freq1, w_f2, b_f2, freq2, w_f3, b_f3, rates2):
    n_emb, hid_raw = w_f1.shape
    n_bands = (n_emb - 1) // 2
    hw2 = w_f3.shape[1]
    hw = hw2 // 2
    tl = min(seq_len, 512)
    hid = LANES
    assert n_bands <= LANES and hid_raw <= LANES

    def pad(x, rows, cols):
        return jnp.zeros((rows, cols), F32).at[:x.shape[0], :x.shape[1]].set(x.astype(F32))

    wt = pad(w_f1[0:1], 1, hid)
    wc = pad(w_f1[1:1 + n_bands], LANES, hid)
    ws = pad(w_f1[1 + n_bands:], LANES, hid)
    b_f1, freq1, b_f2, freq2 = (pad(x, 1, hid) for x in (b_f1, freq1, b_f2, freq2))
    w_f2 = pad(w_f2, hid, hid)
    w_f3 = pad(w_f3, hid, hw2)
    full = lambda shape: pl.BlockSpec(shape, lambda i: (0,) * len(shape))
    kern = functools.partial(_filt_gen_kernel, seq_len, n_bands)
    return pl.pallas_call(
        kern,
        out_shape=(jax.ShapeDtypeStruct((seq_len, hw), BF16), jax.ShapeDtypeStruct((seq_len, hw), BF16),
                   jax.ShapeDtypeStruct((1, hw), F32), jax.ShapeDtypeStruct((1, hw), F32)),
        grid=(seq_len // tl,),
        in_specs=[full((1, hid)), full((LANES, hid)), full((LANES, hid)), full((1, hid)), full((1, hid)),
                  full((hid, hid)), full((1, hid)), full((1, hid)), full((hid, hw2)), full((1, hw2)),
                  full((1, hw2))],
        out_specs=(pl.BlockSpec((tl, hw), lambda i: (i, 0)), pl.BlockSpec((tl, hw), lambda i: (i, 0)),
                   full((1, hw)), full((1, hw))),
        compiler_params=_cp(("arbitrary",)),
        name="filt_gen",
    )(wt, wc, ws, b_f1, freq1, w_f2, b_f2, freq2, w_f3, b_f3, rates2)


def _filt_dft_kernel(seq_len, c_ref, s_ref, a_ref, bm_ref, abs_ref, nyq_ref, kr_ref, ki_ref, kn_ref):
    inv = 1.0 / (abs_ref[...] + EPS)
    scale = inv * (1.0 / seq_len)
    kr_ref[...] = _dot(c_ref[...], a_ref[...]) * scale
    ki_ref[...] = _dot(s_ref[...], bm_ref[...]) * scale
    kn_ref[...] = nyq_ref[...] * inv * (0.5 / seq_len)


def _filt_dft(ctab, stab, a, bm, colabs, nyq):
    seq_len, hw = a.shape
    tf = min(seq_len, 512)
    kern = functools.partial(_filt_dft_kernel, seq_len)
    one = pl.Buffered(1)
    return pl.pallas_call(
        kern,
        out_shape=(jax.ShapeDtypeStruct((seq_len, hw), F32), jax.ShapeDtypeStruct((seq_len, hw), F32),
                   jax.ShapeDtypeStruct((1, hw), F32)),
        grid=(seq_len // tf,),
        in_specs=[pl.BlockSpec((tf, seq_len), lambda i: (i, 0)),
                  pl.BlockSpec((tf, seq_len), lambda i: (i, 0)),
                  pl.BlockSpec((seq_len, hw), lambda i: (0, 0), pipeline_mode=one),
                  pl.BlockSpec((seq_len, hw), lambda i: (0, 0), pipeline_mode=one),
                  pl.BlockSpec((1, hw), lambda i: (0, 0)),
                  pl.BlockSpec((1, hw), lambda i: (0, 0))],
        out_specs=(pl.BlockSpec((tf, hw), lambda i: (i, 0)), pl.BlockSpec((tf, hw), lambda i: (i, 0)),
                   pl.BlockSpec((1, hw), lambda i: (0, 0))),
        compiler_params=_cp(("arbitrary",)),
        name="filt_dft",
    )(ctab, stab, a, bm, colabs, nyq)


def _dft_tables(seq_len):
    r = 1
    while r * r < seq_len:
        r *= 2
    t = jnp.arange(seq_len, dtype=jnp.int32)[None, :]
    lo = jnp.arange(r, dtype=jnp.int32)[:, None]
    hi = jnp.arange(seq_len // r, dtype=jnp.int32)[:, None] * r
    ang_lo = ((lo * t) % (2 * seq_len)).astype(F32) * (math.pi / seq_len)
    ang_hi = ((hi * t) % (2 * seq_len)).astype(F32) * (math.pi / seq_len)
    c_lo, s_lo = jnp.cos(ang_lo)[None], jnp.sin(ang_lo)[None]
    c_hi, s_hi = jnp.cos(ang_hi)[:, None], jnp.sin(ang_hi)[:, None]
    ctab = (c_hi * c_lo - s_hi * s_lo).reshape(seq_len, seq_len)
    stab = (s_hi * c_lo + c_hi * s_lo).reshape(seq_len, seq_len)
    return ctab.astype(BF16), stab.astype(BF16)


def _hy_fwd_kernel(c_ref, s_ref, v_ref, kr_ref, ki_ref, kn_ref, yre_ref, yim_ref, nyq_ref):
    j = pl.program_id(1)
    tf, seq_len = c_ref.shape
    nb = yre_ref.shape[0]
    kr = kr_ref[...]
    ki = ki_ref[...]
    row = lax.broadcasted_iota(jnp.int32, (tf, 1), 0) + j * tf
    dc = jnp.where(row == 0, 0.5, 1.0)
    for s in range(nb):
        v = v_ref[s * seq_len:(s + 1) * seq_len, :]
        p = _dot(c_ref[...], v)
        q = _dot(s_ref[...], v)
        yre_ref[s] = ((p * kr + q * ki) * dc).astype(yre_ref.dtype)
        yim_ref[s] = (p * ki - q * kr).astype(yim_ref.dtype)

    @pl.when(j == 0)
    def _():
        sgn = (1 - 2 * (lax.broadcasted_iota(jnp.int32, (SUBLANES, seq_len), 1) % 2)).astype(BF16)
        for s in range(nb):
            v = v_ref[s * seq_len:(s + 1) * seq_len, :]
            nyq_ref[s] = _dot(sgn, v)[0:1] * kn_ref[...]


BIG_BLOCK_BYTES = 4 * 1024 * 1024


def _seq_batch(seq_len, n_seq):
    nb = max(1, min(n_seq, 4 * MXU_SIDE // seq_len))
    while n_seq % nb:
        nb -= 1
    return nb


def _hy_fwd(ctab, stab, vx, kr, ki, kn, n_seq, row0_blk):
    seq_len, hw = kr.shape
    tf = min(seq_len, 512)
    nb = _seq_batch(seq_len, n_seq)
    assert row0_blk % nb == 0
    mode = pl.Buffered(1) if nb * seq_len * hw * 2 > BIG_BLOCK_BYTES else None
    return pl.pallas_call(
        _hy_fwd_kernel,
        out_shape=(jax.ShapeDtypeStruct((n_seq, seq_len, hw), BF16),
                   jax.ShapeDtypeStruct((n_seq, seq_len, hw), BF16),
                   jax.ShapeDtypeStruct((n_seq, 1, hw), F32)),
        grid=(n_seq // nb, seq_len // tf),
        in_specs=[pl.BlockSpec((tf, seq_len), lambda b, j: (j, 0)),
                  pl.BlockSpec((tf, seq_len), lambda b, j: (j, 0)),
                  pl.BlockSpec((nb * seq_len, hw), lambda b, j: (row0_blk // nb + b, 0), pipeline_mode=mode),
                  pl.BlockSpec((tf, hw), lambda b, j: (j, 0)),
                  pl.BlockSpec((tf, hw), lambda b, j: (j, 0)),
                  pl.BlockSpec((1, hw), lambda b, j: (0, 0))],
        out_specs=(pl.BlockSpec((nb, tf, hw), lambda b, j: (b, j, 0)),
                   pl.BlockSpec((nb, tf, hw), lambda b, j: (b, j, 0)),
                   pl.BlockSpec((nb, 1, hw), lambda b, j: (b, 0, 0))),
        compiler_params=_cp(("arbitrary", "arbitrary")),
        name="hy_fwd",
    )(ctab, stab, vx, kr, ki, kn)


def _hy_inv_kernel(c_ref, s_ref, yre_ref, yim_ref, nyq_ref, vx_ref, x0_ref, fb_ref, wb_ref, o_ref):
    j = pl.program_id(1)
    tt = c_ref.shape[0]
    nb = yre_ref.shape[0]
    row = lax.broadcasted_iota(jnp.int32, (tt, 1), 0) + j * tt
    sgn = (1 - 2 * (row % 2)).astype(F32)
    for s in range(nb):
        rows = slice(s * tt, (s + 1) * tt)
        z = _dot(c_ref[...], yre_ref[s]) - _dot(s_ref[...], yim_ref[s])
        z = z + sgn * nyq_ref[s] + vx_ref[rows, :].astype(F32) * fb_ref[...]
        yb = x0_ref[rows, :].astype(F32) * z
        o_ref[rows, :] = _dot(yb.astype(BF16), wb_ref[...]).astype(o_ref.dtype)


def _hy_inv(ctab, stab, yre, yim, nyq, vx, x0c, fbias, w_bb, t_out, row0_blk):
    n_seq, seq_len, hw = yre.shape
    d = w_bb.shape[1]
    tt = min(seq_len, 256)
    per = seq_len // tt
    nb = _seq_batch(seq_len, n_seq) if per == 1 else 1
    assert row0_blk % nb == 0
    mode = pl.Buffered(1) if nb * seq_len * hw * 2 > BIG_BLOCK_BYTES else None
    return pl.pallas_call(
        _hy_inv_kernel,
        out_shape=jax.ShapeDtypeStruct((t_out, d), BF16),
        grid=(n_seq // nb, per),
        in_specs=[pl.BlockSpec((tt, seq_len), lambda b, j: (j, 0)),
                  pl.BlockSpec((tt, seq_len), lambda b, j: (j, 0)),
                  pl.BlockSpec((nb, seq_len, hw), lambda b, j: (b, 0, 0), pipeline_mode=mode),
                  pl.BlockSpec((nb, seq_len, hw), lambda b, j: (b, 0, 0), pipeline_mode=mode),
                  pl.BlockSpec((nb, 1, hw), lambda b, j: (b, 0, 0)),
                  pl.BlockSpec((nb * tt, hw), lambda b, j: ((row0_blk // nb + b) * per + j, 0)),
                  pl.BlockSpec((nb * tt, hw), lambda b, j: ((row0_blk // nb + b) * per + j, 0)),
                  pl.BlockSpec((1, hw), lambda b, j: (0, 0)),
                  pl.BlockSpec((hw, d), lambda b, j: (0, 0), pipeline_mode=pl.Buffered(1))],
        out_specs=pl.BlockSpec((nb * tt, d), lambda b, j: (b * per + j, 0)),
        compiler_params=_cp(("arbitrary", "arbitrary")),
        name="hy_inv",
    )(ctab, stab, yre, yim, nyq, vx, x0c, fbias, w_bb)


def _merge_kernel(tok, n_groups, xp_ref, xs_ref, pos_ref, g0_ref, g1_ref, pa_ref, pbc_ref, pbs_ref,
                  gate1_ref, sh2_ref, sc2_ref, gn_ref, wo_ref, wr_ref, br_ref,
                  x2_ref, h2_ref, rt_ref):
    i = pl.program_id(0)
    is_ctx = i < tok.nct
    pb = jnp.where(is_ctx, pbc_ref[...].astype(F32), pbs_ref[...].astype(F32))
    merged = g0_ref[...].astype(F32) * pa_ref[...].astype(F32) + g1_ref[...].astype(F32) * pb
    att = _dot(merged.astype(BF16), wo_ref[...])
    x = jnp.where(is_ctx, xp_ref[...], xs_ref[...] + pos_ref[...])
    x2 = x + gate1_ref[0] * att
    x2_ref[...] = x2
    h2 = _rms(x2, gn_ref[...]) * (1.0 + sc2_ref[0]) + sh2_ref[0]
    h2b = h2.astype(BF16)
    h2_ref[...] = _pack_halves(h2b)
    logits = _dot(h2b, wr_ref[...]) + br_ref[...]
    lane = lax.broadcasted_iota(jnp.int32, logits.shape, 1)
    gl = jnp.where(lane < n_groups, logits, -jnp.inf)
    gmax = jnp.max(gl, axis=-1, keepdims=True)
    gidx = jnp.min(jnp.where(gl == gmax, lane, n_groups), axis=-1, keepdims=True)
    rt_ref[...] = jnp.where(lane == LANES - 1, gidx.astype(F32), logits)


def _merge_out(xp, xs, pos, proj, gate_blk0, pa, pb_ctx, pb_lat, mod, g2, w_out, w_r, b_r, n_groups, tok):
    d = xp.shape[1]
    tm = tok.tm
    t_all = xp.shape[0] + xs.shape[0]
    kern = functools.partial(_merge_kernel, tok, n_groups)
    one = pl.Buffered(1)

    def modspec(k):
        return pl.BlockSpec((1, 1, d), lambda i: (tok.cond(i) * N_MOD + k, 0, 0))

    return pl.pallas_call(
        kern,
        out_shape=(jax.ShapeDtypeStruct((t_all, d), F32), jax.ShapeDtypeStruct((t_all, d // 2), jnp.uint32),
                   jax.ShapeDtypeStruct((t_all, LANES), F32)),
        grid=(tok.n,),
        in_specs=[
            pl.BlockSpec((tm, d), lambda i: (tok.ctx_idx(i), 0)),
            pl.BlockSpec((tm, d), lambda i: (tok.lat_idx(i), 0)),
            pl.BlockSpec((tm, d), lambda i: (tok.pos_idx(i), 0)),
            pl.BlockSpec((tm, d), lambda i: (i, gate_blk0)),
            pl.BlockSpec((tm, d), lambda i: (i, gate_blk0 + 1)),
            pl.BlockSpec((tm, d), lambda i: (i, 0)),
            pl.BlockSpec((tm, d), lambda i: (tok.ctx_idx(i), 0)),
            pl.BlockSpec((tm, d), lambda i: (tok.lat_idx(i), 0)),
            modspec(2), modspec(3), modspec(4),
            pl.BlockSpec((1, d), lambda i: (0, 0)),
            pl.BlockSpec((d, d), lambda i: (0, 0), pipeline_mode=one),
            pl.BlockSpec((d, LANES), lambda i: (0, 0)),
            pl.BlockSpec((1, LANES), lambda i: (0, 0)),
        ],
        out_specs=(pl.BlockSpec((tm, d), lambda i: (i, 0)), pl.BlockSpec((tm, d // 2), lambda i: (i, 0)),
                   pl.BlockSpec((tm, LANES), lambda i: (i, 0))),
        compiler_params=_cp(("parallel",)),
        name="merge_out",
    )(xp, xs, pos, proj, proj, pa, pb_ctx, pb_lat, mod, mod, mod, g2, w_out, w_r, b_r)


ROW_UNROLL = 8

def _moe_kernel(n_tiles, n_groups, epg, tg_ref, tnv_ref, rt_ref,
                h2_hbm, wr_ref, br_ref, wg_ref, wu_ref, wd_ref, out_hbm,
                xbuf, xb_ref, acc, gws, gsem, ssem):
    i = pl.program_id(0)
    j = pl.program_id(1)
    tm = xb_ref.shape[0]
    nv = tnv_ref[i]
    slot = i % 2

    def gather_copy(tok_row, slot_, r, rows=1):
        return pltpu.make_async_copy(h2_hbm.at[pl.ds(tok_row, rows), :],
                                     xbuf.at[slot_, pl.ds(r, rows), :], gsem.at[slot_])

    def scatter_copy(tok_row, slot_, r, rows=1):
        return pltpu.make_async_copy(acc.at[slot_, pl.ds(r, rows), :],
                                     out_hbm.at[pl.ds(tok_row, rows), :], ssem.at[0])

    def for_rows(count, fn):
        n_full = count // ROW_UNROLL

        def body(b, c):
            for k in range(ROW_UNROLL):
                fn(b * ROW_UNROLL + k)
            return c

        def tail(r, c):
            fn(r)
            return c

        lax.fori_loop(0, n_full, body, 0)
        lax.fori_loop(n_full * ROW_UNROLL, count, tail, 0)

    def wait_rows(count, make):
        p = tm
        while p >= 1:
            @pl.when((count & p) != 0)
            def _(p=p):
                make(p).wait()
            p //= 2

    def wait_scatter(count, slot_):
        wait_rows(count, lambda p: scatter_copy(0, slot_, 0, p))

    nv_prev = tnv_ref[jnp.maximum(i - 1, 0)]
    nv_next = tnv_ref[jnp.minimum(i + 1, n_tiles - 1)]
    prev_deferred = jnp.logical_and(i > 0, jnp.logical_and(nv_prev == tm, nv > 0))
    deferred = jnp.logical_and(nv == tm, nv_next > 0)

    @pl.when(j == 0)
    def _():
        @pl.when(i == 0)
        def _():
            for_rows(tm, lambda r: gather_copy(rt_ref[r], 0, r).start())

        @pl.when(jnp.logical_or(i == 0, nv_prev > 0))
        def _():
            gather_copy(0, slot, 0, tm).wait()

    @pl.when(jnp.logical_and(j == 0, nv > 0))
    def _():
        grp = tg_ref[i]
        xb = _unpack_halves(xbuf[slot])
        xb_ref[...] = xb
        logits = _dot(xb, wr_ref[...]) + br_ref[...]
        lane = lax.broadcasted_iota(jnp.int32, logits.shape, 1)
        gl = jnp.where(lane < n_groups, logits, -jnp.inf)
        gmax = jnp.max(gl, axis=-1, keepdims=True)
        ge = jnp.exp(gl - gmax)
        g_w = jnp.sum(jnp.where(lane == grp, ge, 0.0), axis=-1, keepdims=True) / jnp.sum(ge, axis=-1, keepdims=True)
        base = n_groups + grp * epg
        e = [jnp.sum(jnp.where(lane == base + k, logits, 0.0), axis=-1, keepdims=True) for k in range(epg)]
        sel = []
        for k in range(epg):
            rank = jnp.zeros_like(e[k])
            for m in range(epg):
                if m == k:
                    continue
                ahead = (e[m] > e[k]) if m > k else (e[m] >= e[k])
                rank = rank + ahead.astype(F32)
            sel.append(rank < TOP_K_INNER)
        emax = functools.reduce(jnp.maximum, e)
        ex = [jnp.where(sel[k], jnp.exp(e[k] - emax), 0.0) for k in range(epg)]
        den = functools.reduce(lambda a, b: a + b, ex)
        for k in range(epg):
            gws[k] = ex[k] / den * g_w
        acc[slot] = jnp.zeros(acc.shape[1:], acc.dtype)

    def expert_step(scatter_prev):
        per = tm // epg
        for k in range(per):
            r = j * per + k
            if scatter_prev:
                scatter_copy(rt_ref[(i - 1) * tm + r], 1 - slot, r).start()
            gather_copy(rt_ref[(i + 1) * tm + r], 1 - slot, r).start()
        xb = xb_ref[...]
        g = _dot(xb, wg_ref[...])
        u = _dot(xb, wu_ref[...])
        hid = (g * jax.nn.sigmoid(g)) * u * gws[j]
        acc[slot] += _dot(hid.astype(BF16), wd_ref[...])

    @pl.when(jnp.logical_and(nv > 0, prev_deferred))
    def _():
        expert_step(True)

    @pl.when(jnp.logical_and(nv > 0, jnp.logical_not(prev_deferred)))
    def _():
        expert_step(False)

    @pl.when(j == epg - 1)
    def _():
        @pl.when(prev_deferred)
        def _():
            scatter_copy(0, 1 - slot, 0, tm).wait()

        @pl.when(jnp.logical_and(nv > 0, jnp.logical_not(deferred)))
        def _():
            for_rows(nv, lambda r: scatter_copy(rt_ref[i * tm + r], slot, r).start())
            wait_scatter(nv, slot)


def _moe(h2p, w_r, b_r, w_g, w_u, w_d, tile_grp, tile_nv, row_tok, n_groups, epg, tm):
    t, dh = h2p.shape
    d = 2 * dh
    ff = w_g.shape[-1]
    n_tiles = tile_grp.shape[0]
    assert tm & (tm - 1) == 0, "row-count waits decompose tm in binary"
    kern = functools.partial(_moe_kernel, n_tiles, n_groups, epg)

    def widx(i, j, tg, tnv, rt):
        return tg[i] * epg + jnp.where(tnv[i] > 0, j, epg - 1)

    grid_spec = pltpu.PrefetchScalarGridSpec(
        num_scalar_prefetch=3,
        grid=(n_tiles, epg),
        in_specs=[
            pl.BlockSpec(memory_space=pl.ANY),
            pl.BlockSpec((d, LANES), lambda i, j, tg, tnv, rt: (0, 0)),
            pl.BlockSpec((1, LANES), lambda i, j, tg, tnv, rt: (0, 0)),
            pl.BlockSpec((None, d, ff), lambda i, j, tg, tnv, rt: (widx(i, j, tg, tnv, rt), 0, 0)),
            pl.BlockSpec((None, d, ff), lambda i, j, tg, tnv, rt: (widx(i, j, tg, tnv, rt), 0, 0)),
            pl.BlockSpec((None, ff, d), lambda i, j, tg, tnv, rt: (widx(i, j, tg, tnv, rt), 0, 0)),
        ],
        out_specs=pl.BlockSpec(memory_space=pl.ANY),
        scratch_shapes=[
            pltpu.VMEM((2, tm, dh), jnp.uint32),
            pltpu.VMEM((tm, d), BF16),
            pltpu.VMEM((2, tm, d), F32),
            pltpu.VMEM((epg, tm, 1), F32),
            pltpu.SemaphoreType.DMA((2,)),
            pltpu.SemaphoreType.DMA((1,)),
        ],
    )
    return pl.pallas_call(
        kern,
        out_shape=jax.ShapeDtypeStruct((t, d), F32),
        grid_spec=grid_spec,
        compiler_params=_cp(("arbitrary", "arbitrary")),
        name="moe",
    )(tile_grp, tile_nv, row_tok, h2p, w_r, b_r, w_g, w_u, w_d)


def _moe_plan(grp, n_groups, tm, n_tiles):
    t = grp.shape[0]
    onehot = (grp[:, None] == jnp.arange(n_groups, dtype=jnp.int32)[None, :]).astype(jnp.int32)
    csum = jnp.cumsum(onehot, axis=0)
    counts = csum[-1]
    rank = jnp.take_along_axis(csum, grp[:, None], axis=1)[:, 0] - 1
    ntile_g = (counts + tm - 1) // tm
    tile_end = jnp.cumsum(ntile_g)
    tile_off = tile_end - ntile_g
    slot = tile_off[grp] * tm + rank
    row_tok = jnp.zeros((n_tiles * tm,), jnp.int32).at[slot].set(jnp.arange(t, dtype=jnp.int32))
    tiles = jnp.arange(n_tiles, dtype=jnp.int32)
    tile_grp = jnp.minimum(jnp.sum((tiles[:, None] >= tile_end[None, :]).astype(jnp.int32), axis=1),
                           n_groups - 1)
    tile_nv = jnp.clip(counts[tile_grp] - (tiles - tile_off[tile_grp]) * tm, 0, tm)
    tile_nv = jnp.where(tiles < tile_end[-1], tile_nv, 0)
    return tile_grp.astype(jnp.int32), tile_nv.astype(jnp.int32), row_tok


def _final_kernel(x2_ref, m_ref, gate2_ref, gf_ref, o_ref):
    x3 = x2_ref[...] + gate2_ref[0] * m_ref[...]
    o_ref[...] = _rms(x3, gf_ref[...])


def _final(x2, moe, mod, g_final, row_blk0, n_rows, cond_of_tile, tm):
    d = x2.shape[1]
    return pl.pallas_call(
        _final_kernel,
        out_shape=jax.ShapeDtypeStruct((n_rows, d), F32),
        grid=(n_rows // tm,),
        in_specs=[pl.BlockSpec((tm, d), lambda i: (row_blk0 + i, 0)),
                  pl.BlockSpec((tm, d), lambda i: (row_blk0 + i, 0)),
                  pl.BlockSpec((1, 1, d), lambda i: (cond_of_tile(i) * N_MOD + 5, 0, 0)),
                  pl.BlockSpec((1, d), lambda i: (0, 0))],
        out_specs=pl.BlockSpec((tm, d), lambda i: (i, 0)),
        compiler_params=_cp(("parallel",)),
        name="final",
    )(x2, moe, mod, g_final)


def _grid_pos_embed(n_tokens, dim):
    rows_n = n_tokens // GRID_W
    quarter = dim // 4
    omega = 1.0 / (10000.0 ** (jnp.arange(quarter, dtype=F32) / quarter))

    def emb(n):
        ang = jnp.arange(n).astype(F32)[:, None] * omega[None, :]
        return jnp.concatenate([jnp.sin(ang), jnp.cos(ang)], axis=-1)

    e_r = jnp.broadcast_to(emb(rows_n)[:, None, :], (rows_n, GRID_W, dim // 2))
    e_c = jnp.broadcast_to(emb(GRID_W)[None, :, :], (rows_n, GRID_W, dim // 2))
    return jnp.concatenate([e_r, e_c], axis=-1).reshape(n_tokens, dim)


def _state_to_lanes(s):
    b, _, _, g, p = s.shape
    return jnp.transpose(s, (0, 3, 2, 1, 4)).reshape(b, g, 4 * p)


def _lanes_to_state(x, p):
    b, g, _ = x.shape
    return jnp.transpose(x.reshape(b, g, 2, 2, p), (0, 3, 2, 1, 4))


def kernel(x_prompt, x_sample, state_ssm, c, c_ctx, w_ada, b_ada, g_norm1, g_norm2, w_in, ssm_a_re, ssm_a_im, ssm_log_dt, ssm_b_re, ssm_b_im, ssm_c_re, ssm_c_im, ssm_d, w_glu, b_glu, w_short, b_short, w_f1, b_f1, freq1, w_f2, b_f2, freq2, w_f3, b_f3, filter_bias, w_branch_a, w_branch_b, w_out, w_router_group, b_router_group, w_router_expert, b_router_expert, w_exp_gate, w_exp_up, w_exp_down, g_final):
    depth = w_ada.shape[0]
    assert depth == 1, "single-layer pipeline"
    bsz, seq, d = x_prompt.shape
    dbsz, dseq, _ = x_sample.shape
    g_all, p_state = ssm_a_re.shape[2:]
    ssm_w = ssm_d.shape[1]
    hw = filter_bias.shape[1]
    n_groups, _, epg = w_router_expert.shape[1:]
    ff = w_exp_gate.shape[-1]
    assert ssm_w // g_all * CHUNK == MXU_SIDE and 4 * p_state == MXU_SIDE
    assert ssm_w == hw and d == 2 * hw
    n_ctx = bsz * seq
    n_lat = dbsz * dseq
    t_all = n_ctx + n_lat

    xp = x_prompt.reshape(n_ctx, d)
    xs = x_sample.reshape(n_lat, d)
    pos = _grid_pos_embed(dseq, d)

    n_cond = 1 + dbsz
    cond8 = jnp.zeros((SUBLANES, d), F32).at[0].set(c_ctx).at[1:n_cond].set(c)
    mod = _ada_mod(cond8, w_ada[0], b_ada[0][None])
    mod = mod[:n_cond].reshape(n_cond * N_MOD, 1, d)

    tm_e = min(256, seq, dseq)
    tok_e = _Tok(n_ctx, n_lat, dseq, tm_e)
    gate_col0 = ssm_w + 3 * hw
    h1 = _norm_mod(xp, xs, pos, mod, g_norm1, tok_e)
    tm_p, tn_p = min(1024, t_all), min(1024, hw)
    n_in = w_in.shape[2]
    u_a = _in_proj(h1, w_in[0], (0, 0), (0, ssm_w), F32, tm_p, tn_p, "in_proj_a")
    proj = _in_proj(h1, w_in[0], (gate_col0, n_in), (ssm_w, gate_col0), BF16, tm_p, tn_p, "in_proj")
    n_gate_cols = n_in - gate_col0

    w_state, w_so, a_step = _ssm_operators(ssm_a_re[0], ssm_a_im[0], ssm_log_dt[0], ssm_b_re[0], ssm_b_im[0],
                                           ssm_c_re[0], ssm_c_im[0], ssm_d[0])
    nc_ctx = seq // CHUNK
    nc_lat = dseq // CHUNK
    seq_per_blk = max(1, min(bsz, MXU_SIDE // nc_ctx))
    assert bsz % seq_per_blk == 0 and n_ctx % dseq == 0
    s0_ctx = jnp.zeros((bsz, g_all, 4 * p_state), F32)
    y_ctx, fin_ctx = _ssm_call(u_a, 0, n_ctx, w_state, w_so, a_step, s0_ctx, seq_per_blk, nc_ctx, "ssm_ctx")
    s0_lat = _state_to_lanes(state_ssm[:, 0].astype(F32))
    y_lat, _ = _ssm_call(u_a, n_ctx // dseq, n_lat, w_state, w_so, a_step, s0_lat, 1, nc_lat, "ssm_lat")
    new_state = _lanes_to_state(fin_ctx, p_state)[:, None]

    pa = _glu_branch(y_ctx, y_lat, w_glu[0].astype(BF16), b_glu[0][None], w_branch_a[0].astype(BF16), tok_e)

    w_short3 = jnp.transpose(w_short[0].reshape(-1, 3, hw), (1, 0, 2))
    b_short3 = b_short[0].reshape(3, 1, hw)
    vx, x0c = _short_conv(proj, n_gate_cols // hw, hw, w_short3, b_short3, n_ctx, seq, dseq, tm_e)
    rates = jnp.abs(jnp.linspace(math.log(DECAY_TARGET) / DECAY_FAST, math.log(DECAY_TARGET) / DECAY_SLOW,
                                 hw, dtype=F32))
    rates2 = jnp.concatenate([rates, rates])[None]
    fbias = filter_bias[0][None].astype(F32)
    w_bb = w_branch_b[0].astype(BF16)
    pbs = []
    for seq_len, n_seq, row0 in ((seq, bsz, 0), (dseq, dbsz, n_ctx // dseq)):
        ctab, stab = _dft_tables(seq_len)
        a_f, bm_f, colabs, nyq_f = _filt_gen(seq_len, w_f1[0], b_f1[0][None], freq1[0][None], w_f2[0],
                                            b_f2[0][None], freq2[0][None], w_f3[0], b_f3[0][None], rates2)
        kr, ki, kn = _filt_dft(ctab, stab, a_f, bm_f, colabs, nyq_f)
        yre, yim, nyq = _hy_fwd(ctab, stab, vx, kr, ki, kn, n_seq, row0)
        pbs.append(_hy_inv(ctab, stab, yre, yim, nyq, vx, x0c, fbias, w_bb, n_seq * seq_len, row0))
    pb_ctx, pb_lat = pbs

    w_r = jnp.zeros((d, LANES), F32)
    w_r = w_r.at[:, :n_groups].set(w_router_group[0])
    w_r = w_r.at[:, n_groups:n_groups + n_groups * epg].set(
        jnp.transpose(w_router_expert[0], (1, 0, 2)).reshape(d, n_groups * epg))
    b_r = jnp.zeros((1, LANES), F32)
    b_r = b_r.at[0, :n_groups].set(b_router_group[0])
    b_r = b_r.at[0, n_groups:n_groups + n_groups * epg].set(b_router_expert[0].reshape(-1))
    w_r = w_r.astype(BF16)
    x2, h2, rout = _merge_out(xp, xs, pos, proj, 0, pa, pb_ctx, pb_lat, mod, g_norm2,
                              w_out[0].astype(BF16), w_r, b_r, n_groups, tok_e)

    tm_m = min(512, t_all // n_groups)
    n_tiles = t_all // tm_m + n_groups
    grp = rout[:, LANES - 1].astype(jnp.int32)
    tile_grp, tile_nv, row_tok = _moe_plan(grp, n_groups, tm_m, n_tiles)
    moe = _moe(h2, w_r, b_r, w_exp_gate[0].astype(BF16), w_exp_up[0].astype(BF16),
               w_exp_down[0].astype(BF16), tile_grp, tile_nv, row_tok, n_groups, epg, tm_m)

    gf = g_final[None]
    y_prompt = _final(x2, moe, mod, gf, 0, n_ctx, lambda i: 0, tm_e)
    lat_tiles = dseq // tm_e
    y_sample = _final(x2, moe, mod, gf, n_ctx // tm_e, n_lat, lambda i: 1 + i // lat_tiles, tm_e)
    return (y_prompt.reshape(bsz, seq, d), y_sample.reshape(dbsz, dseq, d), new_state)
```

```python
import functools
import math

import numpy as np
import jax
import jax.numpy as jnp
from jax import lax
from jax.experimental import pallas as pl
from jax.experimental.pallas import tpu as pltpu

F32 = jnp.float32
BF16 = jnp.bfloat16
EPS = 1e-6
GRID_W = 64
N_MOD = 6
TOP_K_INNER = 2
DECAY_TARGET = 1e-2
DECAY_FAST = 0.3
DECAY_SLOW = 1.5
LANES = 128
SUBLANES = 8
MXU_SIDE = 256
VMEM_LIMIT = 56 * 1024 * 1024


def _cp(sem, vmem=VMEM_LIMIT):
    return pltpu.CompilerParams(dimension_semantics=sem, vmem_limit_bytes=vmem)


def _dot(a, b):
    return jnp.dot(a, b, preferred_element_type=F32)


def _split(a):
    hi = a.astype(BF16)
    lo = (a - hi.astype(F32)).astype(BF16)
    return hi, lo


def _dot3(a, b):
    a_hi, a_lo = _split(a)
    b_hi, b_lo = _split(b)
    return _dot(a_hi, b_hi) + _dot(a_lo, b_hi) + _dot(a_hi, b_lo)


def _rms(x, g):
    ms = jnp.mean(x * x, axis=-1, keepdims=True)
    return x * lax.rsqrt(ms + EPS) * g


def _pack_halves(xb):
    n = xb.shape[1] // 2
    lo = lax.bitcast_convert_type(xb[:, :n].astype(F32), jnp.uint32)
    hi = lax.bitcast_convert_type(xb[:, n:].astype(F32), jnp.uint32)
    return (lo >> 16) | (hi & jnp.uint32(0xFFFF0000))


def _unpack_halves(w):
    lo = lax.bitcast_convert_type(w << 16, F32)
    hi = lax.bitcast_convert_type(w & jnp.uint32(0xFFFF0000), F32)
    return jnp.concatenate([lo, hi], axis=1).astype(BF16)


def _store_row_tiles(ref, row0, xb):
    w = _pack_halves(xb)
    m, n = w.shape
    ns = n // LANES
    for c in range(ns):
        ref[pl.ds(row0 + c, m, stride=ns), :] = w[:, c * LANES:(c + 1) * LANES]


def _load_row_tiles(ref, row0, m, ns):
    w = jnp.concatenate([ref[pl.ds(row0 + c, m, stride=ns), :] for c in range(ns)], axis=1)
    return _unpack_halves(w)


def _ada_kernel(c_ref, w_ref, b_ref, o_ref):
    c = c_ref[...]
    a = c * jax.nn.sigmoid(c)
    o_ref[...] = _dot3(a, w_ref[...]) + b_ref[...]


def _ada_mod(cond8, w_ada, b_ada):
    d, n = w_ada.shape
    tn = min(n, 1024)
    return pl.pallas_call(
        _ada_kernel,
        out_shape=jax.ShapeDtypeStruct((SUBLANES, n), F32),
        grid=(n // tn,),
        in_specs=[pl.BlockSpec((SUBLANES, d), lambda j: (0, 0)),
                  pl.BlockSpec((d, tn), lambda j: (0, j)),
                  pl.BlockSpec((1, tn), lambda j: (0, j))],
        out_specs=pl.BlockSpec((SUBLANES, tn), lambda j: (0, j)),
        compiler_params=_cp(("arbitrary",)),
        name="ada_mod",
    )(cond8, w_ada, b_ada)


class _Tok:
    def __init__(self, n_ctx, n_lat, lat_len, tm):
        assert n_ctx % tm == 0 and lat_len % tm == 0 and n_lat % lat_len == 0
        self.tm = tm
        self.nct = n_ctx // tm
        self.nst = n_lat // tm
        self.per_seq = lat_len // tm
        self.n = self.nct + self.nst

    def ctx_idx(self, i):
        return jnp.minimum(i, self.nct - 1)

    def lat_idx(self, i):
        return jnp.maximum(i - self.nct, 0)

    def pos_idx(self, i):
        return self.lat_idx(i) % self.per_seq

    def cond(self, i):
        return jnp.where(i < self.nct, 0, 1 + self.lat_idx(i) // self.per_seq)


def _norm_mod_kernel(tok, xp_ref, xs_ref, pos_ref, sh_ref, sc_ref, g_ref, h_ref):
    i = pl.program_id(0)

    def modulate(x):
        y = _rms(x, g_ref[...])
        return (y * (1.0 + sc_ref[0]) + sh_ref[0]).astype(h_ref.dtype)

    @pl.when(i < tok.nct)
    def _():
        h_ref[...] = modulate(xp_ref[...])

    @pl.when(i >= tok.nct)
    def _():
        h_ref[...] = modulate(xs_ref[...] + pos_ref[...])


def _norm_mod(xp, xs, pos, mod, g1, tok):
    d = xp.shape[1]
    tm = tok.tm
    t_all = xp.shape[0] + xs.shape[0]
    return pl.pallas_call(
        functools.partial(_norm_mod_kernel, tok),
        out_shape=jax.ShapeDtypeStruct((t_all, d), BF16),
        grid=(tok.n,),
        in_specs=[
            pl.BlockSpec((tm, d), lambda i: (tok.ctx_idx(i), 0)),
            pl.BlockSpec((tm, d), lambda i: (tok.lat_idx(i), 0)),
            pl.BlockSpec((tm, d), lambda i: (tok.pos_idx(i), 0)),
            pl.BlockSpec((1, 1, d), lambda i: (tok.cond(i) * N_MOD + 0, 0, 0)),
            pl.BlockSpec((1, 1, d), lambda i: (tok.cond(i) * N_MOD + 1, 0, 0)),
            pl.BlockSpec((1, d), lambda i: (0, 0)),
        ],
        out_specs=pl.BlockSpec((tm, d), lambda i: (i, 0)),
        compiler_params=_cp(("parallel",)),
        name="norm_mod",
    )(xp, xs, pos, mod, mod, g1)


def _in_proj_kernel(n_sig, h_ref, w_ref, o_ref, wb_ref):
    n = pl.program_id(0)
    m = pl.program_id(1)

    @pl.when(m == 0)
    def _():
        wb_ref[...] = w_ref[...].astype(BF16)

    acc = _dot(h_ref[...], wb_ref[...])

    @pl.when(n >= n_sig)
    def _():
        o_ref[...] = acc.astype(o_ref.dtype)

    @pl.when(n < n_sig)
    def _():
        o_ref[...] = jax.nn.sigmoid(acc).astype(o_ref.dtype)


def _in_proj(h, w_in, sig_cols, lin_cols, out_dtype, tm, tn, name):
    t_all, d = h.shape
    for c in sig_cols + lin_cols:
        assert c % tn == 0
    assert t_all % tm == 0
    n_sig = (sig_cols[1] - sig_cols[0]) // tn
    n_lin = (lin_cols[1] - lin_cols[0]) // tn
    sig0, lin0 = sig_cols[0] // tn, lin_cols[0] // tn

    def w_blk(n):
        return jnp.where(n < n_sig, sig0 + n, lin0 + n - n_sig)

    return pl.pallas_call(
        functools.partial(_in_proj_kernel, n_sig),
        out_shape=jax.ShapeDtypeStruct((t_all, (n_sig + n_lin) * tn), out_dtype),
        grid=(n_sig + n_lin, t_all // tm),
        in_specs=[pl.BlockSpec((tm, d), lambda n, m: (m, 0)),
                  pl.BlockSpec((d, tn), lambda n, m: (0, w_blk(n)))],
        out_specs=pl.BlockSpec((tm, tn), lambda n, m: (m, n)),
        scratch_shapes=[pltpu.VMEM((d, tn), BF16)],
        compiler_params=_cp(("arbitrary", "arbitrary")),
        name=name,
    )(h, w_in)


CHUNK = 16
GROUP_PAD = 8


def _ssm_operators(a_re, a_im, log_dt, b_re, b_im, c_re, c_im, d_skip):
    t = CHUNK
    g, p = a_re.shape[1:]
    c = b_re.shape[-1]
    lam = lax.complex(a_re.astype(F32), a_im.astype(F32))
    dt = jnp.exp(log_dt.astype(F32))[..., None]
    ld = lam * dt
    lam_bar = jnp.exp(ld)
    b_bar = ((lam_bar - 1.0) / lam)[..., None] * lax.complex(b_re.astype(F32), b_im.astype(F32))
    c_mat = lax.complex(c_re.astype(F32), c_im.astype(F32))
    taus = jnp.arange(t + 1, dtype=F32)
    e_pow = jnp.exp(taus[None, None, :, None] * ld[:, :, None, :])

    k_all = jnp.real(jnp.einsum('dgop,dgtp,dgpi->dgtoi', c_mat, e_pow[:, :, :t], b_bar))
    kf, kb = k_all[0], k_all[1]
    ii = np.arange(t)[None, :, None]
    jj = np.arange(t)[None, None, :]
    tt = np.arange(t)[:, None, None]
    place_f = jnp.asarray((jj - ii == tt).astype(np.float32))
    place_b = jnp.asarray((ii - jj == tt).astype(np.float32))
    kf = kf.at[:, 0].add(jnp.eye(c, dtype=F32)[None] * d_skip.astype(F32).reshape(g, c, 1))
    m = jnp.einsum('tij,gtoc->gicjo', place_f, kf) + jnp.einsum('tij,gtoc->gicjo', place_b, kb)
    m = m.reshape(g, t * c, t * c)

    e_f = e_pow[0][:, ::-1][:, 1:]
    e_b = e_pow[1][:, :t]
    ws_f = e_f[:, :, :, None] * b_bar[0][:, None]
    ws_b = e_b[:, :, :, None] * b_bar[1][:, None]

    def rows_ic(x):
        return jnp.transpose(x, (0, 1, 3, 2)).reshape(g, t * c, p)

    w_state = jnp.concatenate([rows_ic(jnp.real(ws_f)), rows_ic(jnp.real(ws_b)),
                               rows_ic(jnp.imag(ws_f)), rows_ic(jnp.imag(ws_b))], axis=-1)

    ce_f = c_mat[0][:, None] * e_pow[0][:, 1:, None, :]
    ce_b = c_mat[1][:, None] * e_pow[1][:, ::-1][:, :t, None, :]

    def cols_jc(x):
        return jnp.transpose(x, (0, 3, 1, 2)).reshape(g, p, t * c)

    w_so = jnp.concatenate([cols_jc(jnp.real(ce_f)), cols_jc(jnp.real(ce_b)),
                            cols_jc(-jnp.imag(ce_f)), cols_jc(-jnp.imag(ce_b))], axis=1)
    w_out = jnp.concatenate([m, w_so], axis=1)

    a_t = e_pow[:, :, t]
    a_step = jnp.stack([jnp.concatenate([jnp.real(a_t[0]), jnp.real(a_t[1])], axis=-1),
                        jnp.concatenate([jnp.imag(a_t[0]), jnp.imag(a_t[1])], axis=-1)])
    return w_state.astype(BF16), w_out.astype(BF16), a_step


def _ssm_kernel(n_seq, n_chunk, gb, g_all, x_ref, ws_ref, wo_ref, a_ref, s0_ref,
                y_ref, fin_ref, xq_ref, za_ref, zb_ref):
    phase = pl.program_id(1)
    j = pl.program_id(2)
    gp = g_all + GROUP_PAD
    nck = n_seq * n_chunk
    half = LANES // 2
    c = LANES // gb

    @pl.when(phase == 0)
    def _():
        xs = [x_ref[pl.ds(i, nck, stride=CHUNK), :] for i in range(CHUNK)]
        for q in range(gb):
            xq = jnp.concatenate([xs[i][:, q * c:(q + 1) * c] for i in range(CHUNK)], axis=-1).astype(BF16)
            xq_ref[j * gb + q] = xq
            z = _dot(xq, ws_ref[q])
            za_ref[pl.ds(j * gb + q, nck, stride=gp), :] = z[:, :LANES]
            zb_ref[pl.ds(j * gb + q, nck, stride=gp), :] = z[:, LANES:]

    @pl.when(jnp.logical_and(phase == 1, j == 0))
    def _():
        a_re = a_ref[0]
        a_im = a_ref[1]
        lane = lax.broadcasted_iota(jnp.int32, (g_all, LANES), 1)
        is_f = lane < half
        for s in range(n_seq):
            def step(k, carry):
                s_a, s_b = carry
                rf = pl.multiple_of((s * n_chunk + k) * gp, SUBLANES)
                rb = pl.multiple_of((s * n_chunk + n_chunk - 1 - k) * gp, SUBLANES)
                zf_a = za_ref[pl.ds(rf, g_all), :]
                zf_b = zb_ref[pl.ds(rf, g_all), :]
                zb_a = za_ref[pl.ds(rb, g_all), :]
                zb_b = zb_ref[pl.ds(rb, g_all), :]
                za_ref[pl.ds(rf, g_all), :] = jnp.where(is_f, s_a, zf_a)
                zb_ref[pl.ds(rf, g_all), :] = jnp.where(is_f, s_b, zf_b)
                za_ref[pl.ds(rb, g_all), :] = jnp.where(is_f, zb_a, s_a)
                zb_ref[pl.ds(rb, g_all), :] = jnp.where(is_f, zb_b, s_b)
                z_a = jnp.where(is_f, zf_a, zb_a)
                z_b = jnp.where(is_f, zf_b, zb_b)
                n_a = a_re * s_a - a_im * s_b + z_a
                n_b = a_re * s_b + a_im * s_a + z_b
                return n_a, n_b

            init = s0_ref[s]
            f_a, f_b = lax.fori_loop(0, n_chunk, step, (init[:, :LANES], init[:, LANES:]))
            fin_ref[s] = jnp.concatenate([f_a, f_b], axis=-1)

    @pl.when(phase == 1)
    def _():
        ys = []
        for q in range(gb):
            st_a = za_ref[pl.ds(j * gb + q, nck, stride=gp), :].astype(BF16)
            st_b = zb_ref[pl.ds(j * gb + q, nck, stride=gp), :].astype(BF16)
            lhs = jnp.concatenate([xq_ref[j * gb + q], st_a, st_b], axis=-1)
            ys.append(_dot(lhs, wo_ref[q]))
        for i in range(CHUNK):
            yi = jnp.concatenate([ys[q][:, i * c:(i + 1) * c] for q in range(gb)], axis=-1)
            y_ref[pl.ds(i, nck, stride=CHUNK), :] = yi


def _ssm_call(u, row_blk0, n_rows, w_state, w_out, a_step, s0, n_seq, n_chunk, name):
    g_all, w, _ = w_state.shape
    nck = n_seq * n_chunk
    blk = nck * CHUNK
    n_blk = n_rows // blk
    gb = LANES * CHUNK // w
    ngb = g_all // gb
    gp = g_all + GROUP_PAD
    kern = functools.partial(_ssm_kernel, n_seq, n_chunk, gb, g_all)
    return pl.pallas_call(
        kern,
        out_shape=(jax.ShapeDtypeStruct((n_rows, u.shape[1]), F32),
                   jax.ShapeDtypeStruct((n_blk * n_seq, g_all, w), F32)),
        grid=(n_blk, 2, ngb),
        in_specs=[
            pl.BlockSpec((blk, LANES), lambda b, ph, j: (row_blk0 + b, jnp.where(ph == 0, j, ngb - 1))),
            pl.BlockSpec((gb, w, w), lambda b, ph, j: (jnp.where(ph == 0, j, ngb - 1), 0, 0)),
            pl.BlockSpec((gb, 2 * w, w), lambda b, ph, j: (jnp.where(ph == 0, 0, j), 0, 0)),
            pl.BlockSpec((2, g_all, LANES), lambda b, ph, j: (0, 0, 0)),
            pl.BlockSpec((n_seq, g_all, w), lambda b, ph, j: (b, 0, 0)),
        ],
        out_specs=(
            pl.BlockSpec((blk, LANES), lambda b, ph, j: (b, jnp.where(ph == 0, 0, j))),
            pl.BlockSpec((n_seq, g_all, w), lambda b, ph, j: (b, 0, 0)),
        ),
        scratch_shapes=[pltpu.VMEM((g_all, nck, w), BF16),
                        pltpu.VMEM((nck * gp, LANES), F32), pltpu.VMEM((nck * gp, LANES), F32)],
        compiler_params=_cp(("arbitrary", "arbitrary", "arbitrary")),
        name=name,
    )(u, w_state, w_out, a_step, s0)


def _glu_kernel(nct, yc_ref, yl_ref, wg_ref, bg_ref, wb_ref, o_ref):
    i = pl.program_id(0)
    y = jax.nn.gelu(jnp.where(i < nct, yc_ref[...], yl_ref[...]))
    z = _dot(y.astype(BF16), wg_ref[...]) + bg_ref[...]
    ya = y * jax.nn.sigmoid(z)
    o_ref[...] = _dot(ya.astype(BF16), wb_ref[...]).astype(o_ref.dtype)


def _glu_branch(y_ctx, y_lat, w_glu, b_glu, w_ba, tok):
    c = y_ctx.shape[1]
    d = w_ba.shape[1]
    tm = tok.tm
    return pl.pallas_call(
        functools.partial(_glu_kernel, tok.nct),
        out_shape=jax.ShapeDtypeStruct((y_ctx.shape[0] + y_lat.shape[0], d), BF16),
        grid=(tok.n,),
        in_specs=[pl.BlockSpec((tm, c), lambda i: (tok.ctx_idx(i), 0)),
                  pl.BlockSpec((tm, c), lambda i: (tok.lat_idx(i), 0)),
                  pl.BlockSpec((c, c), lambda i: (0, 0)),
                  pl.BlockSpec((1, c), lambda i: (0, 0)),
                  pl.BlockSpec((c, d), lambda i: (0, 0))],
        out_specs=pl.BlockSpec((tm, d), lambda i: (i, 0)),
        compiler_params=_cp(("parallel",)),
        name="glu_branch",
    )(y_ctx, y_lat, w_glu, b_glu, w_ba)


def _short_conv_kernel(nct, per_ctx, per_lat, v_ref, x0_ref, x1_ref, vp_ref, x0p_ref, x1p_ref,
                       vn_ref, x0n_ref, x1n_ref, w_ref, b_ref, vx_ref, x0c_ref):
    i = pl.program_id(0)
    tm = v_ref.shape[0]
    k = jnp.where(i < nct, i % per_ctx, (i - nct) % per_lat)
    per = jnp.where(i < nct, per_ctx, per_lat)
    first = k == 0
    last = k == per - 1
    row = lax.broadcasted_iota(jnp.int32, (tm, 1), 0)

    def conv(cur_ref, prev_ref, next_ref, part):
        x = cur_ref[...].astype(F32)
        hp = jnp.where(first, 0.0, prev_ref[SUBLANES - 1:SUBLANES, :].astype(F32))
        hn = jnp.where(last, 0.0, next_ref[0:1, :].astype(F32))
        xm = jnp.where(row == 0, hp, pltpu.roll(x, 1, axis=0))
        xq = jnp.where(row == tm - 1, hn, pltpu.roll(x, tm - 1, axis=0))
        w = w_ref[part]
        return b_ref[part] + xm * w[0:1, :] + x * w[1:2, :] + xq * w[2:3, :]

    v = conv(v_ref, vp_ref, vn_ref, 0)
    x0 = conv(x0_ref, x0p_ref, x0n_ref, 1)
    x1 = conv(x1_ref, x1p_ref, x1n_ref, 2)
    vx_ref[...] = (v * x1).astype(vx_ref.dtype)
    x0c_ref[...] = x0.astype(x0c_ref.dtype)


def _short_conv(proj, col_blk0, hw, w_short3, b_short3, n_ctx, seq, dseq, tm):
    t = proj.shape[0]
    assert seq % tm == 0 and dseq % tm == 0
    nt = t // tm
    r8 = tm // SUBLANES
    nb8 = t // SUBLANES

    def cur(k):
        return pl.BlockSpec((tm, hw), lambda i: (i, col_blk0 + k))

    def prev(k):
        return pl.BlockSpec((SUBLANES, hw), lambda i: (jnp.maximum(i * r8 - 1, 0), col_blk0 + k))

    def nxt(k):
        return pl.BlockSpec((SUBLANES, hw), lambda i: (jnp.minimum((i + 1) * r8, nb8 - 1), col_blk0 + k))

    kern = functools.partial(_short_conv_kernel, n_ctx // tm, seq // tm, dseq // tm)
    return pl.pallas_call(
        kern,
        out_shape=(jax.ShapeDtypeStruct((t, hw), BF16), jax.ShapeDtypeStruct((t, hw), BF16)),
        grid=(nt,),
        in_specs=[cur(0), cur(1), cur(2), prev(0), prev(1), prev(2), nxt(0), nxt(1), nxt(2),
                  pl.BlockSpec((3, 3, hw), lambda i: (0, 0, 0)),
                  pl.BlockSpec((3, 1, hw), lambda i: (0, 0, 0))],
        out_specs=(pl.BlockSpec((tm, hw), lambda i: (i, 0)), pl.BlockSpec((tm, hw), lambda i: (i, 0))),
        compiler_params=_cp(("parallel",)),
        name="short_conv",
    )(proj, proj, proj, proj, proj, proj, proj, proj, proj, w_short3, b_short3)


def _filt_gen_kernel(seq_len, n_bands, wt_ref, wc_ref, ws_ref, b1_ref, f1_ref, w2_ref, b2_ref, f2_ref,
                     w3_ref, b3_ref, rate_ref, a_ref, bm_ref, abs_ref, nyq_ref):
    i = pl.program_id(0)
    tl = a_ref.shape[0]
    hw = a_ref.shape[1]
    pos = lax.broadcasted_iota(jnp.int32, (tl, 1), 0) + i * tl
    t = pos.astype(F32) / seq_len
    bands = (lax.broadcasted_iota(jnp.int32, (1, wc_ref.shape[0]), 1) + 1).astype(F32)
    ang = (2.0 * math.pi) * t * bands
    pre = t * wt_ref[...] + _dot3(jnp.cos(ang), wc_ref[...]) + _dot3(jnp.sin(ang), ws_ref[...])
    h = jnp.sin(f1_ref[...] * (pre + b1_ref[...]))
    h = jnp.sin(f2_ref[...] * (_dot3(h, w2_ref[...]) + b2_ref[...]))
    h = _dot3(h, w3_ref[...]) + b3_ref[...]
    h = h * jnp.exp(-t * rate_ref[...])
    h_f = h[:, :hw]
    h_b = h[:, hw:]
    colabs = jnp.sum(jnp.abs(h_f) + jnp.abs(h_b), axis=0, keepdims=True)
    h_bp = jnp.where(pos == 0, 0.0, h_b)
    a = h_f + h_bp
    sgn = (1 - 2 * (pos % 2)).astype(F32)
    nyq = jnp.sum(a * sgn, axis=0, keepdims=True)
    a_ref[...] = a.astype(a_ref.dtype)
    bm_ref[...] = (h_bp - h_f).astype(bm_ref.dtype)

    @pl.when(i == 0)
    def _():
        abs_ref[...] = colabs
        nyq_ref[...] = nyq

    @pl.when(i > 0)
    def _():
        abs_ref[...] += colabs
        nyq_ref[...] += nyq


def _filt_gen(seq_len, w_f1, b_f1, freq1, w_f2, b_f2, freq2, w_f3, b_f3, rates2):
    n_emb, hid_raw = w_f1.shape
    n_bands = (n_emb - 1) // 2
    hw2 = w_f3.shape[1]
    hw = hw2 // 2
    tl = min(seq_len, 512)
    hid = LANES
    assert n_bands <= LANES and hid_raw <= LANES

    def pad(x, rows, cols):
        return jnp.zeros((rows, cols), F32).at[:x.shape[0], :x.shape[1]].set(x.astype(F32))

    wt = pad(w_f1[0:1], 1, hid)
    wc = pad(w_f1[1:1 + n_bands], LANES, hid)
    ws = pad(w_f1[1 + n_bands:], LANES, hid)
    b_f1, freq1, b_f2, freq2 = (pad(x, 1, hid) for x in (b_f1, freq1, b_f2, freq2))
    w_f2 = pad(w_f2, hid, hid)
    w_f3 = pad(w_f3, hid, hw2)
    full = lambda shape: pl.BlockSpec(shape, lambda i: (0,) * len(shape))
    kern = functools.partial(_filt_gen_kernel, seq_len, n_bands)
    return pl.pallas_call(
        kern,
        out_shape=(jax.ShapeDtypeStruct((seq_len, hw), BF16), jax.ShapeDtypeStruct((seq_len, hw), BF16),
                   jax.ShapeDtypeStruct((1, hw), F32), jax.ShapeDtypeStruct((1, hw), F32)),
        grid=(seq_len // tl,),
        in_specs=[full((1, hid)), full((LANES, hid)), full((LANES, hid)), full((1, hid)), full((1, hid)),
                  full((hid, hid)), full((1, hid)), full((1, hid)), full((hid, hw2)), full((1, hw2)),
                  full((1, hw2))],
        out_specs=(pl.BlockSpec((tl, hw), lambda i: (i, 0)), pl.BlockSpec((tl, hw), lambda i: (i, 0)),
                   full((1, hw)), full((1, hw))),
        compiler_params=_cp(("arbitrary",)),
        name="filt_gen",
    )(wt, wc, ws, b_f1, freq1, w_f2, b_f2, freq2, w_f3, b_f3, rates2)


def _filt_dft_kernel(seq_len, c_ref, s_ref, a_ref, bm_ref, abs_ref, nyq_ref, kr_ref, ki_ref, kn_ref):
    inv = 1.0 / (abs_ref[...] + EPS)
    scale = inv * (1.0 / seq_len)
    kr_ref[...] = _dot(c_ref[...], a_ref[...]) * scale
    ki_ref[...] = _dot(s_ref[...], bm_ref[...]) * scale
    kn_ref[...] = nyq_ref[...] * inv * (0.5 / seq_len)


def _filt_dft(ctab, stab, a, bm, colabs, nyq):
    seq_len, hw = a.shape
    tf = min(seq_len, 512)
    kern = functools.partial(_filt_dft_kernel, seq_len)
    one = pl.Buffered(1)
    return pl.pallas_call(
        kern,
        out_shape=(jax.ShapeDtypeStruct((seq_len, hw), F32), jax.ShapeDtypeStruct((seq_len, hw), F32),
                   jax.ShapeDtypeStruct((1, hw), F32)),
        grid=(seq_len // tf,),
        in_specs=[pl.BlockSpec((tf, seq_len), lambda i: (i, 0)),
                  pl.BlockSpec((tf, seq_len), lambda i: (i, 0)),
                  pl.BlockSpec((seq_len, hw), lambda i: (0, 0), pipeline_mode=one),
                  pl.BlockSpec((seq_len, hw), lambda i: (0, 0), pipeline_mode=one),
                  pl.BlockSpec((1, hw), lambda i: (0, 0)),
                  pl.BlockSpec((1, hw), lambda i: (0, 0))],
        out_specs=(pl.BlockSpec((tf, hw), lambda i: (i, 0)), pl.BlockSpec((tf, hw), lambda i: (i, 0)),
                   pl.BlockSpec((1, hw), lambda i: (0, 0))),
        compiler_params=_cp(("arbitrary",)),
        name="filt_dft",
    )(ctab, stab, a, bm, colabs, nyq)


def _dft_tables(seq_len):
    r = 1
    while r * r < seq_len:
        r *= 2
    t = jnp.arange(seq_len, dtype=jnp.int32)[None, :]
    lo = jnp.arange(r, dtype=jnp.int32)[:, None]
    hi = jnp.arange(seq_len // r, dtype=jnp.int32)[:, None] * r
    ang_lo = ((lo * t) % (2 * seq_len)).astype(F32) * (math.pi / seq_len)
    ang_hi = ((hi * t) % (2 * seq_len)).astype(F32) * (math.pi / seq_len)
    c_lo, s_lo = jnp.cos(ang_lo)[None], jnp.sin(ang_lo)[None]
    c_hi, s_hi = jnp.cos(ang_hi)[:, None], jnp.sin(ang_hi)[:, None]
    ctab = (c_hi * c_lo - s_hi * s_lo).reshape(seq_len, seq_len)
    stab = (s_hi * c_lo + c_hi * s_lo).reshape(seq_len, seq_len)
    return ctab.astype(BF16), stab.astype(BF16)


def _hy_fwd_kernel(c_ref, s_ref, v_ref, kr_ref, ki_ref, kn_ref, yre_ref, yim_ref, nyq_ref):
    j = pl.program_id(1)
    tf, seq_len = c_ref.shape
    nb = yre_ref.shape[0]
    kr = kr_ref[...]
    ki = ki_ref[...]
    row = lax.broadcasted_iota(jnp.int32, (tf, 1), 0) + j * tf
    dc = jnp.where(row == 0, 0.5, 1.0)
    for s in range(nb):
        v = v_ref[s * seq_len:(s + 1) * seq_len, :]
        p = _dot(c_ref[...], v)
        q = _dot(s_ref[...], v)
        yre_ref[s] = ((p * kr + q * ki) * dc).astype(yre_ref.dtype)
        yim_ref[s] = (p * ki - q * kr).astype(yim_ref.dtype)

    @pl.when(j == 0)
    def _():
        sgn = (1 - 2 * (lax.broadcasted_iota(jnp.int32, (SUBLANES, seq_len), 1) % 2)).astype(BF16)
        for s in range(nb):
            v = v_ref[s * seq_len:(s + 1) * seq_len, :]
            nyq_ref[s] = _dot(sgn, v)[0:1] * kn_ref[...]


BIG_BLOCK_BYTES = 4 * 1024 * 1024


def _seq_batch(seq_len, n_seq):
    nb = max(1, min(n_seq, 4 * MXU_SIDE // seq_len))
    while n_seq % nb:
        nb -= 1
    return nb


def _hy_fwd(ctab, stab, vx, kr, ki, kn, n_seq, row0_blk):
    seq_len, hw = kr.shape
    tf = min(seq_len, 512)
    nb = _seq_batch(seq_len, n_seq)
    assert row0_blk % nb == 0
    mode = pl.Buffered(1) if nb * seq_len * hw * 2 > BIG_BLOCK_BYTES else None
    return pl.pallas_call(
        _hy_fwd_kernel,
        out_shape=(jax.ShapeDtypeStruct((n_seq, seq_len, hw), BF16),
                   jax.ShapeDtypeStruct((n_seq, seq_len, hw), BF16),
                   jax.ShapeDtypeStruct((n_seq, 1, hw), F32)),
        grid=(n_seq // nb, seq_len // tf),
        in_specs=[pl.BlockSpec((tf, seq_len), lambda b, j: (j, 0)),
                  pl.BlockSpec((tf, seq_len), lambda b, j: (j, 0)),
                  pl.BlockSpec((nb * seq_len, hw), lambda b, j: (row0_blk // nb + b, 0), pipeline_mode=mode),
                  pl.BlockSpec((tf, hw), lambda b, j: (j, 0)),
                  pl.BlockSpec((tf, hw), lambda b, j: (j, 0)),
                  pl.BlockSpec((1, hw), lambda b, j: (0, 0))],
        out_specs=(pl.BlockSpec((nb, tf, hw), lambda b, j: (b, j, 0)),
                   pl.BlockSpec((nb, tf, hw), lambda b, j: (b, j, 0)),
                   pl.BlockSpec((nb, 1, hw), lambda b, j: (b, 0, 0))),
        compiler_params=_cp(("arbitrary", "arbitrary")),
        name="hy_fwd",
    )(ctab, stab, vx, kr, ki, kn)


def _hy_inv_kernel(c_ref, s_ref, yre_ref, yim_ref, nyq_ref, vx_ref, x0_ref, fb_ref, wb_ref, o_ref):
    j = pl.program_id(1)
    tt = c_ref.shape[0]
    nb = yre_ref.shape[0]
    row = lax.broadcasted_iota(jnp.int32, (tt, 1), 0) + j * tt
    sgn = (1 - 2 * (row % 2)).astype(F32)
    for s in range(nb):
        rows = slice(s * tt, (s + 1) * tt)
        z = _dot(c_ref[...], yre_ref[s]) - _dot(s_ref[...], yim_ref[s])
        z = z + sgn * nyq_ref[s] + vx_ref[rows, :].astype(F32) * fb_ref[...]
        yb = x0_ref[rows, :].astype(F32) * z
        o_ref[rows, :] = _dot(yb.astype(BF16), wb_ref[...]).astype(o_ref.dtype)


def _hy_inv(ctab, stab, yre, yim, nyq, vx, x0c, fbias, w_bb, t_out, row0_blk):
    n_seq, seq_len, hw = yre.shape
    d = w_bb.shape[1]
    tt = min(seq_len, 256)
    per = seq_len // tt
    nb = _seq_batch(seq_len, n_seq) if per == 1 else 1
    assert row0_blk % nb == 0
    mode = pl.Buffered(1) if nb * seq_len * hw * 2 > BIG_BLOCK_BYTES else None
    return pl.pallas_call(
        _hy_inv_kernel,
        out_shape=jax.ShapeDtypeStruct((t_out, d), BF16),
        grid=(n_seq // nb, per),
        in_specs=[pl.BlockSpec((tt, seq_len), lambda b, j: (j, 0)),
                  pl.BlockSpec((tt, seq_len), lambda b, j: (j, 0)),
                  pl.BlockSpec((nb, seq_len, hw), lambda b, j: (b, 0, 0), pipeline_mode=mode),
                  pl.BlockSpec((nb, seq_len, hw), lambda b, j: (b, 0, 0), pipeline_mode=mode),
                  pl.BlockSpec((nb, 1, hw), lambda b, j: (b, 0, 0)),
                  pl.BlockSpec((nb * tt, hw), lambda b, j: ((row0_blk // nb + b) * per + j, 0)),
                  pl.BlockSpec((nb * tt, hw), lambda b, j: ((row0_blk // nb + b) * per + j, 0)),
                  pl.BlockSpec((1, hw), lambda b, j: (0, 0)),
                  pl.BlockSpec((hw, d), lambda b, j: (0, 0), pipeline_mode=pl.Buffered(1))],
        out_specs=pl.BlockSpec((nb * tt, d), lambda b, j: (b * per + j, 0)),
        compiler_params=_cp(("arbitrary", "arbitrary")),
        name="hy_inv",
    )(ctab, stab, yre, yim, nyq, vx, x0c, fbias, w_bb)


def _merge_kernel(tok, n_groups, xp_ref, xs_ref, pos_ref, g0_ref, g1_ref, pa_ref, pbc_ref, pbs_ref,
                  gate1_ref, sh2_ref, sc2_ref, gn_ref, wo_ref, wr_ref, br_ref,
                  x2_ref, h2_ref, rt_ref):
    i = pl.program_id(0)
    is_ctx = i < tok.nct
    pb = jnp.where(is_ctx, pbc_ref[...].astype(F32), pbs_ref[...].astype(F32))
    merged = g0_ref[...].astype(F32) * pa_ref[...].astype(F32) + g1_ref[...].astype(F32) * pb
    att = _dot(merged.astype(BF16), wo_ref[...])
    x = jnp.where(is_ctx, xp_ref[...], xs_ref[...] + pos_ref[...])
    x2 = x + gate1_ref[0] * att
    x2_ref[...] = x2
    h2 = _rms(x2, gn_ref[...]) * (1.0 + sc2_ref[0]) + sh2_ref[0]
    h2b = h2.astype(BF16)
    _store_row_tiles(h2_ref, 0, h2b)
    logits = _dot(h2b, wr_ref[...]) + br_ref[...]
    lane = lax.broadcasted_iota(jnp.int32, logits.shape, 1)
    gl = jnp.where(lane < n_groups, logits, -jnp.inf)
    gmax = jnp.max(gl, axis=-1, keepdims=True)
    gidx = jnp.min(jnp.where(gl == gmax, lane, n_groups), axis=-1, keepdims=True)
    rt_ref[...] = jnp.where(lane == LANES - 1, gidx.astype(F32), logits)


def _merge_out(xp, xs, pos, proj, gate_blk0, pa, pb_ctx, pb_lat, mod, g2, w_out, w_r, b_r, n_groups, tok):
    d = xp.shape[1]
    tm = tok.tm
    t_all = xp.shape[0] + xs.shape[0]
    kern = functools.partial(_merge_kernel, tok, n_groups)
    one = pl.Buffered(1)
    ns = d // 2 // LANES

    def modspec(k):
        return pl.BlockSpec((1, 1, d), lambda i: (tok.cond(i) * N_MOD + k, 0, 0))

    return pl.pallas_call(
        kern,
        out_shape=(jax.ShapeDtypeStruct((t_all, d), F32), jax.ShapeDtypeStruct((t_all * ns, LANES), jnp.uint32),
                   jax.ShapeDtypeStruct((t_all, LANES), F32)),
        grid=(tok.n,),
        in_specs=[
            pl.BlockSpec((tm, d), lambda i: (tok.ctx_idx(i), 0)),
            pl.BlockSpec((tm, d), lambda i: (tok.lat_idx(i), 0)),
            pl.BlockSpec((tm, d), lambda i: (tok.pos_idx(i), 0)),
            pl.BlockSpec((tm, d), lambda i: (i, gate_blk0)),
            pl.BlockSpec((tm, d), lambda i: (i, gate_blk0 + 1)),
            pl.BlockSpec((tm, d), lambda i: (i, 0)),
            pl.BlockSpec((tm, d), lambda i: (tok.ctx_idx(i), 0)),
            pl.BlockSpec((tm, d), lambda i: (tok.lat_idx(i), 0)),
            modspec(2), modspec(3), modspec(4),
            pl.BlockSpec((1, d), lambda i: (0, 0)),
            pl.BlockSpec((d, d), lambda i: (0, 0), pipeline_mode=one),
            pl.BlockSpec((d, LANES), lambda i: (0, 0)),
            pl.BlockSpec((1, LANES), lambda i: (0, 0)),
        ],
        out_specs=(pl.BlockSpec((tm, d), lambda i: (i, 0)), pl.BlockSpec((tm * ns, LANES), lambda i: (i, 0)),
                   pl.BlockSpec((tm, LANES), lambda i: (i, 0))),
        compiler_params=_cp(("parallel",)),
        name="merge_out",
    )(xp, xs, pos, proj, proj, pa, pb_ctx, pb_lat, mod, mod, mod, g2, w_out, w_r, b_r)


ROW_UNROLL = 8

def _moe_kernel(n_tiles, n_groups, epg, tg_ref, tnv_ref, rt_ref,
                h2_hbm, wr_ref, br_ref, wg_ref, wu_ref, wd_ref, out_hbm,
                xbuf, xb_ref, acc, obuf, gws, gsem, ssem):
    i = pl.program_id(0)
    j = pl.program_id(1)
    tm = xb_ref.shape[0]
    ns = h2_hbm.shape[1]
    nv = tnv_ref[i]
    slot = i % 2

    def buf_rows(buf, slot_, r, rows):
        return buf.at[pl.ds(pl.multiple_of((slot_ * tm + r) * ns, ns), rows * ns), :]

    def gather_copy(tok_row, slot_, r):
        return pltpu.make_async_copy(h2_hbm.at[tok_row], buf_rows(xbuf, slot_, r, 1), gsem.at[slot_])

    def gather_wait(slot_, rows):
        dst = buf_rows(xbuf, slot_, 0, rows)
        pltpu.make_async_copy(dst, dst, gsem.at[slot_]).wait()

    def scatter_copy(tok_row, slot_, r):
        return pltpu.make_async_copy(buf_rows(obuf, slot_, r, 1), out_hbm.at[tok_row], ssem.at[0])

    def scatter_wait(slot_, rows):
        src = buf_rows(obuf, slot_, 0, rows)
        pltpu.make_async_copy(src, src, ssem.at[0]).wait()

    def for_rows(count, fn):
        n_full = count // ROW_UNROLL

        def body(b, c):
            for k in range(ROW_UNROLL):
                fn(b * ROW_UNROLL + k)
            return c

        def tail(r, c):
            fn(r)
            return c

        lax.fori_loop(0, n_full, body, 0)
        lax.fori_loop(n_full * ROW_UNROLL, count, tail, 0)

    def wait_scatter(count, slot_):
        p = tm
        while p >= 1:
            @pl.when((count & p) != 0)
            def _(p=p):
                scatter_wait(slot_, p)
            p //= 2

    nv_prev = tnv_ref[jnp.maximum(i - 1, 0)]
    nv_next = tnv_ref[jnp.minimum(i + 1, n_tiles - 1)]
    prev_deferred = jnp.logical_and(i > 0, jnp.logical_and(nv_prev == tm, nv > 0))
    deferred = jnp.logical_and(nv == tm, nv_next > 0)

    @pl.when(j == 0)
    def _():
        @pl.when(i == 0)
        def _():
            for_rows(tm, lambda r: gather_copy(rt_ref[r], 0, r).start())

        @pl.when(jnp.logical_or(i == 0, nv_prev > 0))
        def _():
            gather_wait(slot, tm)

    @pl.when(jnp.logical_and(j == 0, nv > 0))
    def _():
        grp = tg_ref[i]
        xb = _load_row_tiles(xbuf, slot * tm * ns, tm, ns)
        xb_ref[...] = xb
        logits = _dot(xb, wr_ref[...]) + br_ref[...]
        lane = lax.broadcasted_iota(jnp.int32, logits.shape, 1)
        gl = jnp.where(lane < n_groups, logits, -jnp.inf)
        gmax = jnp.max(gl, axis=-1, keepdims=True)
        ge = jnp.exp(gl - gmax)
        g_w = jnp.sum(jnp.where(lane == grp, ge, 0.0), axis=-1, keepdims=True) / jnp.sum(ge, axis=-1, keepdims=True)
        base = n_groups + grp * epg
        e = [jnp.sum(jnp.where(lane == base + k, logits, 0.0), axis=-1, keepdims=True) for k in range(epg)]
        sel = []
        for k in range(epg):
            rank = jnp.zeros_like(e[k])
            for m in range(epg):
                if m == k:
                    continue
                ahead = (e[m] > e[k]) if m > k else (e[m] >= e[k])
                rank = rank + ahead.astype(F32)
            sel.append(rank < TOP_K_INNER)
        emax = functools.reduce(jnp.maximum, e)
        ex = [jnp.where(sel[k], jnp.exp(e[k] - emax), 0.0) for k in range(epg)]
        den = functools.reduce(lambda a, b: a + b, ex)
        for k in range(epg):
            gws[k] = ex[k] / den * g_w
        acc[...] = jnp.zeros(acc.shape, acc.dtype)

    def expert_step(scatter_prev):
        per = tm // epg
        for k in range(per):
            r = j * per + k
            if scatter_prev:
                scatter_copy(rt_ref[(i - 1) * tm + r], 1 - slot, r).start()
            gather_copy(rt_ref[(i + 1) * tm + r], 1 - slot, r).start()
        xb = xb_ref[...]
        g = _dot(xb, wg_ref[...])
        u = _dot(xb, wu_ref[...])
        hid = (g * jax.nn.sigmoid(g)) * u * gws[j]
        acc[...] += _dot(hid.astype(BF16), wd_ref[...])

    @pl.when(jnp.logical_and(nv > 0, prev_deferred))
    def _():
        expert_step(True)

    @pl.when(jnp.logical_and(nv > 0, jnp.logical_not(prev_deferred)))
    def _():
        expert_step(False)

    @pl.when(j == epg - 1)
    def _():
        @pl.when(prev_deferred)
        def _():
            scatter_wait(1 - slot, tm)

        @pl.when(nv > 0)
        def _():
            _store_row_tiles(obuf, slot * tm * ns, acc[...].astype(BF16))

        @pl.when(jnp.logical_and(nv > 0, jnp.logical_not(deferred)))
        def _():
            for_rows(nv, lambda r: scatter_copy(rt_ref[i * tm + r], slot, r).start())
            wait_scatter(nv, slot)


def _moe(h2p, w_r, b_r, w_g, w_u, w_d, tile_grp, tile_nv, row_tok, n_groups, epg, tm):
    t, ns, _ = h2p.shape
    d = 2 * ns * LANES
    ff = w_g.shape[-1]
    n_tiles = tile_grp.shape[0]
    assert tm & (tm - 1) == 0, "row-count waits decompose tm in binary"
    kern = functools.partial(_moe_kernel, n_tiles, n_groups, epg)

    def widx(i, j, tg, tnv, rt):
        return tg[i] * epg + jnp.where(tnv[i] > 0, j, epg - 1)

    grid_spec = pltpu.PrefetchScalarGridSpec(
        num_scalar_prefetch=3,
        grid=(n_tiles, epg),
        in_specs=[
            pl.BlockSpec(memory_space=pl.ANY),
            pl.BlockSpec((d, LANES), lambda i, j, tg, tnv, rt: (0, 0)),
            pl.BlockSpec((1, LANES), lambda i, j, tg, tnv, rt: (0, 0)),
            pl.BlockSpec((None, d, ff), lambda i, j, tg, tnv, rt: (widx(i, j, tg, tnv, rt), 0, 0)),
            pl.BlockSpec((None, d, ff), lambda i, j, tg, tnv, rt: (widx(i, j, tg, tnv, rt), 0, 0)),
            pl.BlockSpec((None, ff, d), lambda i, j, tg, tnv, rt: (widx(i, j, tg, tnv, rt), 0, 0)),
        ],
        out_specs=pl.BlockSpec(memory_space=pl.ANY),
        scratch_shapes=[
            pltpu.VMEM((2 * tm * ns, LANES), jnp.uint32),
            pltpu.VMEM((tm, d), BF16),
            pltpu.VMEM((tm, d), F32),
            pltpu.VMEM((2 * tm * ns, LANES), jnp.uint32),
            pltpu.VMEM((epg, tm, 1), F32),
            pltpu.SemaphoreType.DMA((2,)),
            pltpu.SemaphoreType.DMA((1,)),
        ],
    )
    return pl.pallas_call(
        kern,
        out_shape=jax.ShapeDtypeStruct((t, ns, LANES), jnp.uint32),
        grid_spec=grid_spec,
        compiler_params=_cp(("arbitrary", "arbitrary")),
        name="moe",
    )(tile_grp, tile_nv, row_tok, h2p, w_r, b_r, w_g, w_u, w_d)


def _moe_plan(grp, n_groups, tm, n_tiles):
    t = grp.shape[0]
    onehot = (grp[:, None] == jnp.arange(n_groups, dtype=jnp.int32)[None, :]).astype(jnp.int32)
    csum = jnp.cumsum(onehot, axis=0)
    counts = csum[-1]
    rank = jnp.take_along_axis(csum, grp[:, None], axis=1)[:, 0] - 1
    ntile_g = (counts + tm - 1) // tm
    tile_end = jnp.cumsum(ntile_g)
    tile_off = tile_end - ntile_g
    slot = tile_off[grp] * tm + rank
    row_tok = jnp.zeros((n_tiles * tm,), jnp.int32).at[slot].set(jnp.arange(t, dtype=jnp.int32))
    tiles = jnp.arange(n_tiles, dtype=jnp.int32)
    tile_grp = jnp.minimum(jnp.sum((tiles[:, None] >= tile_end[None, :]).astype(jnp.int32), axis=1),
                           n_groups - 1)
    tile_nv = jnp.clip(counts[tile_grp] - (tiles - tile_off[tile_grp]) * tm, 0, tm)
    tile_nv = jnp.where(tiles < tile_end[-1], tile_nv, 0)
    return tile_grp.astype(jnp.int32), tile_nv.astype(jnp.int32), row_tok


def _final_kernel(x2_ref, m_ref, gate2_ref, gf_ref, o_ref):
    tm, d = x2_ref.shape
    ns = d // 2 // LANES
    m = _load_row_tiles(m_ref, 0, tm, ns).astype(F32)
    x3 = x2_ref[...] + gate2_ref[0] * m
    o_ref[...] = _rms(x3, gf_ref[...])


def _final(x2, moe, mod, g_final, row_blk0, n_rows, cond_of_tile, tm):
    d = x2.shape[1]
    ns = d // 2 // LANES
    return pl.pallas_call(
        _final_kernel,
        out_shape=jax.ShapeDtypeStruct((n_rows, d), F32),
        grid=(n_rows // tm,),
        in_specs=[pl.BlockSpec((tm, d), lambda i: (row_blk0 + i, 0)),
                  pl.BlockSpec((tm * ns, LANES), lambda i: (row_blk0 + i, 0)),
                  pl.BlockSpec((1, 1, d), lambda i: (cond_of_tile(i) * N_MOD + 5, 0, 0)),
                  pl.BlockSpec((1, d), lambda i: (0, 0))],
        out_specs=pl.BlockSpec((tm, d), lambda i: (i, 0)),
        compiler_params=_cp(("parallel",)),
        name="final",
    )(x2, moe, mod, g_final)


def _grid_pos_embed(n_tokens, dim):
    rows_n = n_tokens // GRID_W
    quarter = dim // 4
    omega = 1.0 / (10000.0 ** (jnp.arange(quarter, dtype=F32) / quarter))

    def emb(n):
        ang = jnp.arange(n).astype(F32)[:, None] * omega[None, :]
        return jnp.concatenate([jnp.sin(ang), jnp.cos(ang)], axis=-1)

    e_r = jnp.broadcast_to(emb(rows_n)[:, None, :], (rows_n, GRID_W, dim // 2))
    e_c = jnp.broadcast_to(emb(GRID_W)[None, :, :], (rows_n, GRID_W, dim // 2))
    return jnp.concatenate([e_r, e_c], axis=-1).reshape(n_tokens, dim)


def _state_to_lanes(s):
    b, _, _, g, p = s.shape
    return jnp.transpose(s, (0, 3, 2, 1, 4)).reshape(b, g, 4 * p)


def _lanes_to_state(x, p):
    b, g, _ = x.shape
    return jnp.transpose(x.reshape(b, g, 2, 2, p), (0, 3, 2, 1, 4))


def kernel(x_prompt, x_sample, state_ssm, c, c_ctx, w_ada, b_ada, g_norm1, g_norm2, w_in, ssm_a_re, ssm_a_im, ssm_log_dt, ssm_b_re, ssm_b_im, ssm_c_re, ssm_c_im, ssm_d, w_glu, b_glu, w_short, b_short, w_f1, b_f1, freq1, w_f2, b_f2, freq2, w_f3, b_f3, filter_bias, w_branch_a, w_branch_b, w_out, w_router_group, b_router_group, w_router_expert, b_router_expert, w_exp_gate, w_exp_up, w_exp_down, g_final):
    depth = w_ada.shape[0]
    assert depth == 1, "single-layer pipeline"
    bsz, seq, d = x_prompt.shape
    dbsz, dseq, _ = x_sample.shape
    g_all, p_state = ssm_a_re.shape[2:]
    ssm_w = ssm_d.shape[1]
    hw = filter_bias.shape[1]
    n_groups, _, epg = w_router_expert.shape[1:]
    ff = w_exp_gate.shape[-1]
    assert ssm_w // g_all * CHUNK == MXU_SIDE and 4 * p_state == MXU_SIDE
    assert ssm_w == hw and d == 2 * hw
    n_ctx = bsz * seq
    n_lat = dbsz * dseq
    t_all = n_ctx + n_lat

    xp = x_prompt.reshape(n_ctx, d)
    xs = x_sample.reshape(n_lat, d)
    pos = _grid_pos_embed(dseq, d)

    n_cond = 1 + dbsz
    cond8 = jnp.zeros((SUBLANES, d), F32).at[0].set(c_ctx).at[1:n_cond].set(c)
    mod = _ada_mod(cond8, w_ada[0], b_ada[0][None])
    mod = mod[:n_cond].reshape(n_cond * N_MOD, 1, d)

    tm_e = min(256, seq, dseq)
    tok_e = _Tok(n_ctx, n_lat, dseq, tm_e)
    tm_l = min(512, n_ctx, dseq)
    tok_l = _Tok(n_ctx, n_lat, dseq, tm_l)
    gate_col0 = ssm_w + 3 * hw
    h1 = _norm_mod(xp, xs, pos, mod, g_norm1, tok_l)
    tm_p, tn_p = min(1024, t_all), min(1024, hw)
    n_in = w_in.shape[2]
    u_a = _in_proj(h1, w_in[0], (0, 0), (0, ssm_w), F32, tm_p, tn_p, "in_proj_a")
    proj = _in_proj(h1, w_in[0], (gate_col0, n_in), (ssm_w, gate_col0), BF16, tm_p, tn_p, "in_proj")
    n_gate_cols = n_in - gate_col0

    w_state, w_so, a_step = _ssm_operators(ssm_a_re[0], ssm_a_im[0], ssm_log_dt[0], ssm_b_re[0], ssm_b_im[0],
                                           ssm_c_re[0], ssm_c_im[0], ssm_d[0])
    nc_ctx = seq // CHUNK
    nc_lat = dseq // CHUNK
    seq_per_blk = max(1, min(bsz, MXU_SIDE // nc_ctx))
    assert bsz % seq_per_blk == 0 and n_ctx % dseq == 0
    s0_ctx = jnp.zeros((bsz, g_all, 4 * p_state), F32)
    y_ctx, fin_ctx = _ssm_call(u_a, 0, n_ctx, w_state, w_so, a_step, s0_ctx, seq_per_blk, nc_ctx, "ssm_ctx")
    s0_lat = _state_to_lanes(state_ssm[:, 0].astype(F32))
    y_lat, _ = _ssm_call(u_a, n_ctx // dseq, n_lat, w_state, w_so, a_step, s0_lat, 1, nc_lat, "ssm_lat")
    new_state = _lanes_to_state(fin_ctx, p_state)[:, None]

    pa = _glu_branch(y_ctx, y_lat, w_glu[0].astype(BF16), b_glu[0][None], w_branch_a[0].astype(BF16), tok_l)

    w_short3 = jnp.transpose(w_short[0].reshape(-1, 3, hw), (1, 0, 2))
    b_short3 = b_short[0].reshape(3, 1, hw)
    vx, x0c = _short_conv(proj, n_gate_cols // hw, hw, w_short3, b_short3, n_ctx, seq, dseq, tm_e)
    rates = jnp.abs(jnp.linspace(math.log(DECAY_TARGET) / DECAY_FAST, math.log(DECAY_TARGET) / DECAY_SLOW,
                                 hw, dtype=F32))
    rates2 = jnp.concatenate([rates, rates])[None]
    fbias = filter_bias[0][None].astype(F32)
    w_bb = w_branch_b[0].astype(BF16)
    pbs = []
    for seq_len, n_seq, row0 in ((seq, bsz, 0), (dseq, dbsz, n_ctx // dseq)):
        ctab, stab = _dft_tables(seq_len)
        a_f, bm_f, colabs, nyq_f = _filt_gen(seq_len, w_f1[0], b_f1[0][None], freq1[0][None], w_f2[0],
                                            b_f2[0][None], freq2[0][None], w_f3[0], b_f3[0][None], rates2)
        kr, ki, kn = _filt_dft(ctab, stab, a_f, bm_f, colabs, nyq_f)
        yre, yim, nyq = _hy_fwd(ctab, stab, vx, kr, ki, kn, n_seq, row0)
        pbs.append(_hy_inv(ctab, stab, yre, yim, nyq, vx, x0c, fbias, w_bb, n_seq * seq_len, row0))
    pb_ctx, pb_lat = pbs

    w_r = jnp.zeros((d, LANES), F32)
    w_r = w_r.at[:, :n_groups].set(w_router_group[0])
    w_r = w_r.at[:, n_groups:n_groups + n_groups * epg].set(
        jnp.transpose(w_router_expert[0], (1, 0, 2)).reshape(d, n_groups * epg))
    b_r = jnp.zeros((1, LANES), F32)
    b_r = b_r.at[0, :n_groups].set(b_router_group[0])
    b_r = b_r.at[0, n_groups:n_groups + n_groups * epg].set(b_router_expert[0].reshape(-1))
    w_r = w_r.astype(BF16)
    x2, h2, rout = _merge_out(xp, xs, pos, proj, 0, pa, pb_ctx, pb_lat, mod, g_norm2,
                              w_out[0].astype(BF16), w_r, b_r, n_groups, tok_e)

    tm_m = min(512, t_all // n_groups)
    n_tiles = t_all // tm_m + n_groups
    grp = rout[:, LANES - 1].astype(jnp.int32)
    tile_grp, tile_nv, row_tok = _moe_plan(grp, n_groups, tm_m, n_tiles)
    ns = d // 2 // LANES
    moe = _moe(h2.reshape(t_all, ns, LANES), w_r, b_r, w_exp_gate[0].astype(BF16), w_exp_up[0].astype(BF16),
               w_exp_down[0].astype(BF16), tile_grp, tile_nv, row_tok, n_groups, epg, tm_m)
    moe = moe.reshape(t_all * ns, LANES)

    gf = g_final[None]
    y_prompt = _final(x2, moe, mod, gf, 0, n_ctx, lambda i: 0, tm_l)
    lat_tiles = dseq // tm_l
    y_sample = _final(x2, moe, mod, gf, n_ctx // tm_l, n_lat, lambda i: 1 + i // lat_tiles, tm_l)
    return (y_prompt.reshape(bsz, seq, d), y_sample.reshape(dbsz, dseq, d), new_state)
```

```python
import functools
import math

import numpy as np
import jax
import jax.numpy as jnp
from jax import lax
from jax.experimental import pallas as pl
from jax.experimental.pallas import tpu as pltpu

F32 = jnp.float32
BF16 = jnp.bfloat16
EPS = 1e-6
GRID_W = 64
N_MOD = 6
TOP_K_INNER = 2
DECAY_TARGET = 1e-2
DECAY_FAST = 0.3
DECAY_SLOW = 1.5
LANES = 128
SUBLANES = 8
MXU_SIDE = 256
VMEM_LIMIT = 56 * 1024 * 1024


def _cp(sem, vmem=VMEM_LIMIT):
    return pltpu.CompilerParams(dimension_semantics=sem, vmem_limit_bytes=vmem)


def _dot(a, b):
    return jnp.dot(a, b, preferred_element_type=F32)


def _split(a):
    hi = a.astype(BF16)
    lo = (a - hi.astype(F32)).astype(BF16)
    return hi, lo


def _dot3(a, b):
    a_hi, a_lo = _split(a)
    b_hi, b_lo = _split(b)
    return _dot(a_hi, b_hi) + _dot(a_lo, b_hi) + _dot(a_hi, b_lo)


def _rms(x, g):
    ms = jnp.mean(x * x, axis=-1, keepdims=True)
    return x * lax.rsqrt(ms + EPS) * g


def _pack_halves(xb):
    n = xb.shape[1] // 2
    lo = lax.bitcast_convert_type(xb[:, :n].astype(F32), jnp.uint32)
    hi = lax.bitcast_convert_type(xb[:, n:].astype(F32), jnp.uint32)
    return (lo >> 16) | (hi & jnp.uint32(0xFFFF0000))


def _unpack_halves(w):
    lo = lax.bitcast_convert_type(w << 16, F32)
    hi = lax.bitcast_convert_type(w & jnp.uint32(0xFFFF0000), F32)
    return jnp.concatenate([lo, hi], axis=1).astype(BF16)


def _store_row_tiles(ref, row0, xb):
    w = _pack_halves(xb)
    m, n = w.shape
    ns = n // LANES
    for c in range(ns):
        ref[pl.ds(row0 + c, m, stride=ns), :] = w[:, c * LANES:(c + 1) * LANES]


def _load_row_tiles(ref, row0, m, ns):
    w = jnp.concatenate([ref[pl.ds(row0 + c, m, stride=ns), :] for c in range(ns)], axis=1)
    return _unpack_halves(w)


def _ada_kernel(c_ref, w_ref, b_ref, o_ref):
    c = c_ref[...]
    a = c * jax.nn.sigmoid(c)
    o_ref[...] = _dot3(a, w_ref[...]) + b_ref[...]


def _ada_mod(cond8, w_ada, b_ada):
    d, n = w_ada.shape
    tn = min(n, 1024)
    return pl.pallas_call(
        _ada_kernel,
        out_shape=jax.ShapeDtypeStruct((SUBLANES, n), F32),
        grid=(n // tn,),
        in_specs=[pl.BlockSpec((SUBLANES, d), lambda j: (0, 0)),
                  pl.BlockSpec((d, tn), lambda j: (0, j)),
                  pl.BlockSpec((1, tn), lambda j: (0, j))],
        out_specs=pl.BlockSpec((SUBLANES, tn), lambda j: (0, j)),
        compiler_params=_cp(("arbitrary",)),
        name="ada_mod",
    )(cond8, w_ada, b_ada)


class _Tok:
    def __init__(self, n_ctx, n_lat, lat_len, tm):
        assert n_ctx % tm == 0 and lat_len % tm == 0 and n_lat % lat_len == 0
        self.tm = tm
        self.nct = n_ctx // tm
        self.nst = n_lat // tm
        self.per_seq = lat_len // tm
        self.n = self.nct + self.nst

    def ctx_idx(self, i):
        return jnp.minimum(i, self.nct - 1)

    def lat_idx(self, i):
        return jnp.maximum(i - self.nct, 0)

    def pos_idx(self, i):
        return self.lat_idx(i) % self.per_seq

    def cond(self, i):
        return jnp.where(i < self.nct, 0, 1 + self.lat_idx(i) // self.per_seq)


def _norm_mod_kernel(tok, xp_ref, xs_ref, pos_ref, sh_ref, sc_ref, g_ref, h_ref):
    i = pl.program_id(0)

    def modulate(x):
        y = _rms(x, g_ref[...])
        return (y * (1.0 + sc_ref[0]) + sh_ref[0]).astype(h_ref.dtype)

    @pl.when(i < tok.nct)
    def _():
        h_ref[...] = modulate(xp_ref[...])

    @pl.when(i >= tok.nct)
    def _():
        h_ref[...] = modulate(xs_ref[...] + pos_ref[...])


def _norm_mod(xp, xs, pos, mod, g1, tok):
    d = xp.shape[1]
    tm = tok.tm
    t_all = xp.shape[0] + xs.shape[0]
    return pl.pallas_call(
        functools.partial(_norm_mod_kernel, tok),
        out_shape=jax.ShapeDtypeStruct((t_all, d), BF16),
        grid=(tok.n,),
        in_specs=[
            pl.BlockSpec((tm, d), lambda i: (tok.ctx_idx(i), 0)),
            pl.BlockSpec((tm, d), lambda i: (tok.lat_idx(i), 0)),
            pl.BlockSpec((tm, d), lambda i: (tok.pos_idx(i), 0)),
            pl.BlockSpec((1, 1, d), lambda i: (tok.cond(i) * N_MOD + 0, 0, 0)),
            pl.BlockSpec((1, 1, d), lambda i: (tok.cond(i) * N_MOD + 1, 0, 0)),
            pl.BlockSpec((1, d), lambda i: (0, 0)),
        ],
        out_specs=pl.BlockSpec((tm, d), lambda i: (i, 0)),
        compiler_params=_cp(("parallel",)),
        name="norm_mod",
    )(xp, xs, pos, mod, mod, g1)


def _in_proj_kernel(n_sig, h_ref, w_ref, o_ref, wb_ref):
    n = pl.program_id(0)
    m = pl.program_id(1)

    @pl.when(m == 0)
    def _():
        wb_ref[...] = w_ref[...].astype(BF16)

    acc = _dot(h_ref[...], wb_ref[...])

    @pl.when(n >= n_sig)
    def _():
        o_ref[...] = acc.astype(o_ref.dtype)

    @pl.when(n < n_sig)
    def _():
        o_ref[...] = jax.nn.sigmoid(acc).astype(o_ref.dtype)


def _in_proj(h, w_in, sig_cols, lin_cols, out_dtype, tm, tn, name):
    t_all, d = h.shape
    for c in sig_cols + lin_cols:
        assert c % tn == 0
    assert t_all % tm == 0
    n_sig = (sig_cols[1] - sig_cols[0]) // tn
    n_lin = (lin_cols[1] - lin_cols[0]) // tn
    sig0, lin0 = sig_cols[0] // tn, lin_cols[0] // tn

    def w_blk(n):
        return jnp.where(n < n_sig, sig0 + n, lin0 + n - n_sig)

    return pl.pallas_call(
        functools.partial(_in_proj_kernel, n_sig),
        out_shape=jax.ShapeDtypeStruct((t_all, (n_sig + n_lin) * tn), out_dtype),
        grid=(n_sig + n_lin, t_all // tm),
        in_specs=[pl.BlockSpec((tm, d), lambda n, m: (m, 0)),
                  pl.BlockSpec((d, tn), lambda n, m: (0, w_blk(n)))],
        out_specs=pl.BlockSpec((tm, tn), lambda n, m: (m, n)),
        scratch_shapes=[pltpu.VMEM((d, tn), BF16)],
        compiler_params=_cp(("arbitrary", "arbitrary")),
        name=name,
    )(h, w_in)


CHUNK = 16
GROUP_PAD = 8


def _ssm_operators(a_re, a_im, log_dt, b_re, b_im, c_re, c_im, d_skip):
    t = CHUNK
    g, p = a_re.shape[1:]
    c = b_re.shape[-1]
    lam = lax.complex(a_re.astype(F32), a_im.astype(F32))
    dt = jnp.exp(log_dt.astype(F32))[..., None]
    ld = lam * dt
    lam_bar = jnp.exp(ld)
    b_bar = ((lam_bar - 1.0) / lam)[..., None] * lax.complex(b_re.astype(F32), b_im.astype(F32))
    c_mat = lax.complex(c_re.astype(F32), c_im.astype(F32))
    taus = jnp.arange(t + 1, dtype=F32)
    e_pow = jnp.exp(taus[None, None, :, None] * ld[:, :, None, :])

    k_all = jnp.real(jnp.einsum('dgop,dgtp,dgpi->dgtoi', c_mat, e_pow[:, :, :t], b_bar))
    kf, kb = k_all[0], k_all[1]
    ii = np.arange(t)[None, :, None]
    jj = np.arange(t)[None, None, :]
    tt = np.arange(t)[:, None, None]
    place_f = jnp.asarray((jj - ii == tt).astype(np.float32))
    place_b = jnp.asarray((ii - jj == tt).astype(np.float32))
    kf = kf.at[:, 0].add(jnp.eye(c, dtype=F32)[None] * d_skip.astype(F32).reshape(g, c, 1))
    m = jnp.einsum('tij,gtoc->gicjo', place_f, kf) + jnp.einsum('tij,gtoc->gicjo', place_b, kb)
    m = m.reshape(g, t * c, t * c)

    e_f = e_pow[0][:, ::-1][:, 1:]
    e_b = e_pow[1][:, :t]
    ws_f = e_f[:, :, :, None] * b_bar[0][:, None]
    ws_b = e_b[:, :, :, None] * b_bar[1][:, None]

    def rows_ic(x):
        return jnp.transpose(x, (0, 1, 3, 2)).reshape(g, t * c, p)

    w_state = jnp.concatenate([rows_ic(jnp.real(ws_f)), rows_ic(jnp.real(ws_b)),
                               rows_ic(jnp.imag(ws_f)), rows_ic(jnp.imag(ws_b))], axis=-1)

    ce_f = c_mat[0][:, None] * e_pow[0][:, 1:, None, :]
    ce_b = c_mat[1][:, None] * e_pow[1][:, ::-1][:, :t, None, :]

    def cols_jc(x):
        return jnp.transpose(x, (0, 3, 1, 2)).reshape(g, p, t * c)

    w_so = jnp.concatenate([cols_jc(jnp.real(ce_f)), cols_jc(jnp.real(ce_b)),
                            cols_jc(-jnp.imag(ce_f)), cols_jc(-jnp.imag(ce_b))], axis=1)
    w_out = jnp.concatenate([m, w_so], axis=1)

    a_t = e_pow[:, :, t]
    a_step = jnp.stack([jnp.concatenate([jnp.real(a_t[0]), jnp.real(a_t[1])], axis=-1),
                        jnp.concatenate([jnp.imag(a_t[0]), jnp.imag(a_t[1])], axis=-1)])
    return w_state.astype(BF16), w_out.astype(BF16), a_step


def _ssm_kernel(n_seq, n_chunk, gb, g_all, x_ref, ws_ref, wo_ref, a_ref, s0_ref,
                y_ref, fin_ref, xq_ref, za_ref, zb_ref):
    phase = pl.program_id(1)
    j = pl.program_id(2)
    gp = g_all + GROUP_PAD
    nck = n_seq * n_chunk
    half = LANES // 2
    c = LANES // gb

    @pl.when(phase == 0)
    def _():
        xs = [x_ref[pl.ds(i, nck, stride=CHUNK), :] for i in range(CHUNK)]
        for q in range(gb):
            xq = jnp.concatenate([xs[i][:, q * c:(q + 1) * c] for i in range(CHUNK)], axis=-1).astype(BF16)
            xq_ref[j * gb + q] = xq
            z = _dot(xq, ws_ref[q])
            za_ref[pl.ds(j * gb + q, nck, stride=gp), :] = z[:, :LANES]
            zb_ref[pl.ds(j * gb + q, nck, stride=gp), :] = z[:, LANES:]

    @pl.when(jnp.logical_and(phase == 1, j == 0))
    def _():
        a_re = a_ref[0]
        a_im = a_ref[1]
        lane = lax.broadcasted_iota(jnp.int32, (g_all, LANES), 1)
        is_f = lane < half
        for s in range(n_seq):
            def step(k, carry):
                s_a, s_b = carry
                rf = pl.multiple_of((s * n_chunk + k) * gp, SUBLANES)
                rb = pl.multiple_of((s * n_chunk + n_chunk - 1 - k) * gp, SUBLANES)
                zf_a = za_ref[pl.ds(rf, g_all), :]
                zf_b = zb_ref[pl.ds(rf, g_all), :]
                zb_a = za_ref[pl.ds(rb, g_all), :]
                zb_b = zb_ref[pl.ds(rb, g_all), :]
                za_ref[pl.ds(rf, g_all), :] = jnp.where(is_f, s_a, zf_a)
                zb_ref[pl.ds(rf, g_all), :] = jnp.where(is_f, s_b, zf_b)
                za_ref[pl.ds(rb, g_all), :] = jnp.where(is_f, zb_a, s_a)
                zb_ref[pl.ds(rb, g_all), :] = jnp.where(is_f, zb_b, s_b)
                z_a = jnp.where(is_f, zf_a, zb_a)
                z_b = jnp.where(is_f, zf_b, zb_b)
                n_a = a_re * s_a - a_im * s_b + z_a
                n_b = a_re * s_b + a_im * s_a + z_b
                return n_a, n_b

            init = s0_ref[s]
            f_a, f_b = lax.fori_loop(0, n_chunk, step, (init[:, :LANES], init[:, LANES:]))
            fin_ref[s] = jnp.concatenate([f_a, f_b], axis=-1)

    @pl.when(phase == 1)
    def _():
        ys = []
        for q in range(gb):
            st_a = za_ref[pl.ds(j * gb + q, nck, stride=gp), :].astype(BF16)
            st_b = zb_ref[pl.ds(j * gb + q, nck, stride=gp), :].astype(BF16)
            lhs = jnp.concatenate([xq_ref[j * gb + q], st_a, st_b], axis=-1)
            ys.append(_dot(lhs, wo_ref[q]))
        for i in range(CHUNK):
            yi = jnp.concatenate([ys[q][:, i * c:(i + 1) * c] for q in range(gb)], axis=-1)
            y_ref[pl.ds(i, nck, stride=CHUNK), :] = yi


def _ssm_call(u, row_blk0, n_rows, w_state, w_out, a_step, s0, n_seq, n_chunk, name):
    g_all, w, _ = w_state.shape
    nck = n_seq * n_chunk
    blk = nck * CHUNK
    n_blk = n_rows // blk
    gb = LANES * CHUNK // w
    ngb = g_all // gb
    gp = g_all + GROUP_PAD
    kern = functools.partial(_ssm_kernel, n_seq, n_chunk, gb, g_all)
    return pl.pallas_call(
        kern,
        out_shape=(jax.ShapeDtypeStruct((n_rows, u.shape[1]), F32),
                   jax.ShapeDtypeStruct((n_blk * n_seq, g_all, w), F32)),
        grid=(n_blk, 2, ngb),
        in_specs=[
            pl.BlockSpec((blk, LANES), lambda b, ph, j: (row_blk0 + b, jnp.where(ph == 0, j, ngb - 1))),
            pl.BlockSpec((gb, w, w), lambda b, ph, j: (jnp.where(ph == 0, j, ngb - 1), 0, 0)),
            pl.BlockSpec((gb, 2 * w, w), lambda b, ph, j: (jnp.where(ph == 0, 0, j), 0, 0)),
            pl.BlockSpec((2, g_all, LANES), lambda b, ph, j: (0, 0, 0)),
            pl.BlockSpec((n_seq, g_all, w), lambda b, ph, j: (b, 0, 0)),
        ],
        out_specs=(
            pl.BlockSpec((blk, LANES), lambda b, ph, j: (b, jnp.where(ph == 0, 0, j))),
            pl.BlockSpec((n_seq, g_all, w), lambda b, ph, j: (b, 0, 0)),
        ),
        scratch_shapes=[pltpu.VMEM((g_all, nck, w), BF16),
                        pltpu.VMEM((nck * gp, LANES), F32), pltpu.VMEM((nck * gp, LANES), F32)],
        compiler_params=_cp(("arbitrary", "arbitrary", "arbitrary")),
        name=name,
    )(u, w_state, w_out, a_step, s0)


def _glu_kernel(nct, yc_ref, yl_ref, wg_ref, bg_ref, wb_ref, o_ref):
    i = pl.program_id(0)
    y = jax.nn.gelu(jnp.where(i < nct, yc_ref[...], yl_ref[...]))
    z = _dot(y.astype(BF16), wg_ref[...]) + bg_ref[...]
    ya = y * jax.nn.sigmoid(z)
    o_ref[...] = _dot(ya.astype(BF16), wb_ref[...]).astype(o_ref.dtype)


def _glu_branch(y_ctx, y_lat, w_glu, b_glu, w_ba, tok):
    c = y_ctx.shape[1]
    d = w_ba.shape[1]
    tm = tok.tm
    return pl.pallas_call(
        functools.partial(_glu_kernel, tok.nct),
        out_shape=jax.ShapeDtypeStruct((y_ctx.shape[0] + y_lat.shape[0], d), BF16),
        grid=(tok.n,),
        in_specs=[pl.BlockSpec((tm, c), lambda i: (tok.ctx_idx(i), 0)),
                  pl.BlockSpec((tm, c), lambda i: (tok.lat_idx(i), 0)),
                  pl.BlockSpec((c, c), lambda i: (0, 0)),
                  pl.BlockSpec((1, c), lambda i: (0, 0)),
                  pl.BlockSpec((c, d), lambda i: (0, 0))],
        out_specs=pl.BlockSpec((tm, d), lambda i: (i, 0)),
        compiler_params=_cp(("parallel",)),
        name="glu_branch",
    )(y_ctx, y_lat, w_glu, b_glu, w_ba)


def _short_conv_kernel(nct, per_ctx, per_lat, v_ref, x0_ref, x1_ref, vp_ref, x0p_ref, x1p_ref,
                       vn_ref, x0n_ref, x1n_ref, w_ref, b_ref, vx_ref, x0c_ref):
    i = pl.program_id(0)
    tm = v_ref.shape[0]
    k = jnp.where(i < nct, i % per_ctx, (i - nct) % per_lat)
    per = jnp.where(i < nct, per_ctx, per_lat)
    first = k == 0
    last = k == per - 1
    row = lax.broadcasted_iota(jnp.int32, (tm, 1), 0)

    def conv(cur_ref, prev_ref, next_ref, part):
        x = cur_ref[...].astype(F32)
        hp = jnp.where(first, 0.0, prev_ref[SUBLANES - 1:SUBLANES, :].astype(F32))
        hn = jnp.where(last, 0.0, next_ref[0:1, :].astype(F32))
        xm = jnp.where(row == 0, hp, pltpu.roll(x, 1, axis=0))
        xq = jnp.where(row == tm - 1, hn, pltpu.roll(x, tm - 1, axis=0))
        w = w_ref[part]
        return b_ref[part] + xm * w[0:1, :] + x * w[1:2, :] + xq * w[2:3, :]

    v = conv(v_ref, vp_ref, vn_ref, 0)
    x0 = conv(x0_ref, x0p_ref, x0n_ref, 1)
    x1 = conv(x1_ref, x1p_ref, x1n_ref, 2)
    vx_ref[...] = (v * x1).astype(vx_ref.dtype)
    x0c_ref[...] = x0.astype(x0c_ref.dtype)


def _short_conv(proj, col_blk0, hw, w_short3, b_short3, n_ctx, seq, dseq, tm):
    t = proj.shape[0]
    assert seq % tm == 0 and dseq % tm == 0
    nt = t // tm
    r8 = tm // SUBLANES
    nb8 = t // SUBLANES

    def cur(k):
        return pl.BlockSpec((tm, hw), lambda i: (i, col_blk0 + k))

    def prev(k):
        return pl.BlockSpec((SUBLANES, hw), lambda i: (jnp.maximum(i * r8 - 1, 0), col_blk0 + k))

    def nxt(k):
        return pl.BlockSpec((SUBLANES, hw), lambda i: (jnp.minimum((i + 1) * r8, nb8 - 1), col_blk0 + k))

    kern = functools.partial(_short_conv_kernel, n_ctx // tm, seq // tm, dseq // tm)
    return pl.pallas_call(
        kern,
        out_shape=(jax.ShapeDtypeStruct((t, hw), BF16), jax.ShapeDtypeStruct((t, hw), BF16)),
        grid=(nt,),
        in_specs=[cur(0), cur(1), cur(2), prev(0), prev(1), prev(2), nxt(0), nxt(1), nxt(2),
                  pl.BlockSpec((3, 3, hw), lambda i: (0, 0, 0)),
                  pl.BlockSpec((3, 1, hw), lambda i: (0, 0, 0))],
        out_specs=(pl.BlockSpec((tm, hw), lambda i: (i, 0)), pl.BlockSpec((tm, hw), lambda i: (i, 0))),
        compiler_params=_cp(("parallel",)),
        name="short_conv",
    )(proj, proj, proj, proj, proj, proj, proj, proj, proj, w_short3, b_short3)


def _filt_gen_kernel(seq_len, n_bands, wt_ref, wc_ref, ws_ref, b1_ref, f1_ref, w2_ref, b2_ref, f2_ref,
                     w3_ref, b3_ref, rate_ref, a_ref, bm_ref, abs_ref):
    i = pl.program_id(0)
    tl = a_ref.shape[0]
    hw = a_ref.shape[1]
    pos = lax.broadcasted_iota(jnp.int32, (tl, 1), 0) + i * tl
    t = pos.astype(F32) / seq_len
    bands = (lax.broadcasted_iota(jnp.int32, (1, wc_ref.shape[0]), 1) + 1).astype(F32)
    ang = (2.0 * math.pi) * t * bands
    pre = t * wt_ref[...] + _dot3(jnp.cos(ang), wc_ref[...]) + _dot3(jnp.sin(ang), ws_ref[...])
    h = jnp.sin(f1_ref[...] * (pre + b1_ref[...]))
    h = jnp.sin(f2_ref[...] * (_dot3(h, w2_ref[...]) + b2_ref[...]))
    h = _dot3(h, w3_ref[...]) + b3_ref[...]
    h = h * jnp.exp(-t * rate_ref[...])
    h_f = h[:, :hw]
    h_b = h[:, hw:]
    colabs = jnp.sum(jnp.abs(h_f) + jnp.abs(h_b), axis=0, keepdims=True)
    h_bp = jnp.where(pos == 0, 0.0, h_b)
    a_ref[...] = (h_f + h_bp).astype(a_ref.dtype)
    bm_ref[...] = (h_bp - h_f).astype(bm_ref.dtype)

    @pl.when(i == 0)
    def _():
        abs_ref[...] = colabs

    @pl.when(i > 0)
    def _():
        abs_ref[...] += colabs


def _filt_gen(seq_len, w_f1, b_f1, freq1, w_f2, b_f2, freq2, w_f3, b_f3, rates2):
    n_emb, hid_raw = w_f1.shape
    n_bands = (n_emb - 1) // 2
    hw2 = w_f3.shape[1]
    hw = hw2 // 2
    tl = min(seq_len, 512)
    hid = LANES
    assert n_bands <= LANES and hid_raw <= LANES

    def pad(x, rows, cols):
        return jnp.zeros((rows, cols), F32).at[:x.shape[0], :x.shape[1]].set(x.astype(F32))

    wt = pad(w_f1[0:1], 1, hid)
    wc = pad(w_f1[1:1 + n_bands], LANES, hid)
    ws = pad(w_f1[1 + n_bands:], LANES, hid)
    b_f1, freq1, b_f2, freq2 = (pad(x, 1, hid) for x in (b_f1, freq1, b_f2, freq2))
    w_f2 = pad(w_f2, hid, hid)
    w_f3 = pad(w_f3, hid, hw2)
    full = lambda shape: pl.BlockSpec(shape, lambda i: (0,) * len(shape))
    kern = functools.partial(_filt_gen_kernel, seq_len, n_bands)
    return pl.pallas_call(
        kern,
        out_shape=(jax.ShapeDtypeStruct((seq_len, hw), BF16), jax.ShapeDtypeStruct((seq_len, hw), BF16),
                   jax.ShapeDtypeStruct((1, hw), F32)),
        grid=(seq_len // tl,),
        in_specs=[full((1, hid)), full((LANES, hid)), full((LANES, hid)), full((1, hid)), full((1, hid)),
                  full((hid, hid)), full((1, hid)), full((1, hid)), full((hid, hw2)), full((1, hw2)),
                  full((1, hw2))],
        out_specs=(pl.BlockSpec((tl, hw), lambda i: (i, 0)), pl.BlockSpec((tl, hw), lambda i: (i, 0)),
                   full((1, hw))),
        compiler_params=_cp(("arbitrary",)),
        name="filt_gen",
    )(wt, wc, ws, b_f1, freq1, w_f2, b_f2, freq2, w_f3, b_f3, rates2)


def _trig_tables(row_hi, row_lo, col, period):
    col = col[None, :]
    ang_lo = ((row_lo[:, None] * col) % period).astype(F32) * (2.0 * math.pi / period)
    ang_hi = ((row_hi[:, None] * col) % period).astype(F32) * (2.0 * math.pi / period)
    c_lo, s_lo = jnp.cos(ang_lo)[None], jnp.sin(ang_lo)[None]
    c_hi, s_hi = jnp.cos(ang_hi)[:, None], jnp.sin(ang_hi)[:, None]
    n = row_hi.shape[0] * row_lo.shape[0]
    ctab = (c_hi * c_lo - s_hi * s_lo).reshape(n, col.shape[1])
    stab = (s_hi * c_lo + c_hi * s_lo).reshape(n, col.shape[1])
    return ctab.astype(BF16), stab.astype(BF16)


def _dft_tables(seq_len):
    m = seq_len // 2
    r = 1
    while r * r < m:
        r *= 2
    assert m % r == 0
    i32 = jnp.int32
    idx = jnp.arange(m, dtype=i32)
    hi = jnp.arange(m // r, dtype=i32) * r
    lo = jnp.arange(r, dtype=i32)
    ce, se = _trig_tables(hi, lo, idx, 2 * m)
    co, so = _trig_tables(hi, lo, 2 * idx + 1, 2 * seq_len)
    cot, sot = _trig_tables(2 * hi, 2 * lo + 1, idx, 2 * seq_len)
    return ce, se, co, so, cot, sot


def _alt_row(n):
    return (1 - 2 * (lax.broadcasted_iota(jnp.int32, (SUBLANES, n), 1) % 2)).astype(BF16)


def _filt_dft_kernel(seq_len, ce_ref, se_ref, co_ref, so_ref, ae_ref, ao_ref, be_ref, bo_ref, abs_ref,
                     krl_ref, krh_ref, kil_ref, kih_ref, kmid_ref):
    i = pl.program_id(0)
    scale = (1.0 / seq_len) / (abs_ref[...] + EPS)
    ec = _dot(ce_ref[...], ae_ref[...])
    oc = _dot(co_ref[...], ao_ref[...])
    es = _dot(se_ref[...], be_ref[...])
    os_ = _dot(so_ref[...], bo_ref[...])
    krl_ref[...] = (ec + oc) * scale
    krh_ref[...] = (ec - oc) * scale
    kil_ref[...] = (es + os_) * scale
    kih_ref[...] = (os_ - es) * scale

    @pl.when(i == 0)
    def _():
        sgn = _alt_row(ae_ref.shape[0])
        kmid_ref[0:1, :] = _dot(sgn, ae_ref[...])[0:1] * scale
        kmid_ref[1:2, :] = _dot(sgn, bo_ref[...])[0:1] * scale


def _filt_dft(tabs, a2, bm2, colabs, seq_len):
    m = seq_len // 2
    hw = a2.shape[1] // 2
    ce, se, co, so = tabs[:4]
    tf = min(m, 256)
    kern = functools.partial(_filt_dft_kernel, seq_len)
    mode = pl.Buffered(1) if m * hw * 2 > BIG_BLOCK_BYTES // 2 else None
    tab = pl.BlockSpec((tf, m), lambda i: (i, 0))
    res = lambda c: pl.BlockSpec((m, hw), lambda i: (0, c), pipeline_mode=mode)
    out = pl.BlockSpec((tf, hw), lambda i: (i, 0))
    sds = jax.ShapeDtypeStruct((m, hw), F32)
    return pl.pallas_call(
        kern,
        out_shape=(sds, sds, sds, sds, jax.ShapeDtypeStruct((2, hw), F32)),
        grid=(m // tf,),
        in_specs=[tab, tab, tab, tab, res(0), res(1), res(0), res(1), pl.BlockSpec((1, hw), lambda i: (0, 0))],
        out_specs=(out, out, out, out, pl.BlockSpec((2, hw), lambda i: (0, 0))),
        compiler_params=_cp(("arbitrary",)),
        name="filt_dft",
    )(ce, se, co, so, a2, a2, bm2, bm2, colabs)


def _hy_fwd_kernel(ce_ref, se_ref, co_ref, so_ref, ve_ref, vo_ref, krl_ref, krh_ref, kil_ref, kih_ref, kmid_ref,
                   ae_ref, be_ref, ao_ref, bo_ref, ymid_ref):
    j = pl.program_id(1)
    tf, m = ce_ref.shape
    nb = ae_ref.shape[0]
    krl, krh, kil, kih = krl_ref[...], krh_ref[...], kil_ref[...], kih_ref[...]
    row = lax.broadcasted_iota(jnp.int32, (tf, 1), 0) + j * tf
    dc = jnp.where(row == 0, 0.5, 1.0)
    for s in range(nb):
        ve = ve_ref[s * m:(s + 1) * m, :]
        vo = vo_ref[s * m:(s + 1) * m, :]
        ec = _dot(ce_ref[...], ve)
        oc = _dot(co_ref[...], vo)
        es = _dot(se_ref[...], ve)
        os_ = _dot(so_ref[...], vo)
        p_lo, p_hi = ec + oc, ec - oc
        q_lo, q_hi = es + os_, os_ - es
        yre_lo = (p_lo * krl + q_lo * kil) * dc
        yim_lo = p_lo * kil - q_lo * krl
        yre_hi = (p_hi * krh + q_hi * kih) * dc
        yim_hi = p_hi * kih - q_hi * krh
        ae_ref[s] = (yre_lo + yre_hi).astype(ae_ref.dtype)
        be_ref[s] = (yim_lo - yim_hi).astype(be_ref.dtype)
        ao_ref[s] = (yre_lo - yre_hi).astype(ao_ref.dtype)
        bo_ref[s] = (yim_lo + yim_hi).astype(bo_ref.dtype)

    @pl.when(j == 0)
    def _():
        sgn = _alt_row(m)
        kr_m, ki_m = kmid_ref[0:1, :], kmid_ref[1:2, :]
        for s in range(nb):
            p_m = _dot(sgn, ve_ref[s * m:(s + 1) * m, :])[0:1]
            q_m = _dot(sgn, vo_ref[s * m:(s + 1) * m, :])[0:1]
            ymid_ref[s, 0:1, :] = p_m * kr_m + q_m * ki_m
            ymid_ref[s, 1:2, :] = p_m * ki_m - q_m * kr_m


BIG_BLOCK_BYTES = 4 * 1024 * 1024


def _seq_batch(seq_len, n_seq):
    nb = max(1, min(n_seq, 4 * MXU_SIDE // seq_len))
    while n_seq % nb:
        nb -= 1
    return nb


def _hy_fwd(tabs, vx2, kspec, n_seq, seq_len, row0_blk):
    m = seq_len // 2
    hw = vx2.shape[1] // 2
    ce, se, co, so = tabs[:4]
    krl, krh, kil, kih, kmid = kspec
    tf = min(m, 256)
    nb = _seq_batch(seq_len, n_seq)
    assert row0_blk % nb == 0
    mode = pl.Buffered(1) if nb * m * hw * 2 > BIG_BLOCK_BYTES // 2 else None
    tab = pl.BlockSpec((tf, m), lambda b, j: (j, 0))
    vspec = lambda c: pl.BlockSpec((nb * m, hw), lambda b, j: (row0_blk // nb + b, c), pipeline_mode=mode)
    kt = pl.BlockSpec((tf, hw), lambda b, j: (j, 0))
    out = pl.BlockSpec((nb, tf, hw), lambda b, j: (b, j, 0))
    sds = jax.ShapeDtypeStruct((n_seq, m, hw), BF16)
    return pl.pallas_call(
        _hy_fwd_kernel,
        out_shape=(sds, sds, sds, sds, jax.ShapeDtypeStruct((n_seq, 2, hw), F32)),
        grid=(n_seq // nb, m // tf),
        in_specs=[tab, tab, tab, tab, vspec(0), vspec(1), kt, kt, kt, kt,
                  pl.BlockSpec((2, hw), lambda b, j: (0, 0))],
        out_specs=(out, out, out, out, pl.BlockSpec((nb, 2, hw), lambda b, j: (b, 0, 0))),
        compiler_params=_cp(("arbitrary", "arbitrary")),
        name="hy_fwd",
    )(ce, se, co, so, vx2, vx2, krl, krh, kil, kih, kmid)


def _hy_inv_kernel(ce_ref, se_ref, cot_ref, sot_ref, ae_ref, be_ref, ao_ref, bo_ref, ymid_ref,
                   vx_ref, x0_ref, fb_ref, wb_ref, o_ref):
    j = pl.program_id(1)
    tt = ce_ref.shape[0]
    nb = ae_ref.shape[0]
    hw = fb_ref.shape[1]
    d = wb_ref.shape[1]
    row = lax.broadcasted_iota(jnp.int32, (tt, 1), 0) + j * tt
    sgn = (1 - 2 * (row % 2)).astype(F32)
    for s in range(nb):
        rows = slice(s * tt, (s + 1) * tt)
        ymid = ymid_ref[s]
        z_e = _dot(ce_ref[...], ae_ref[s]) - _dot(se_ref[...], be_ref[s]) + sgn * ymid[0:1]
        z_o = _dot(cot_ref[...], ao_ref[s]) - _dot(sot_ref[...], bo_ref[s]) - sgn * ymid[1:2]
        for par, z in ((0, z_e), (1, z_o)):
            cols = slice(par * hw, (par + 1) * hw)
            z = z + vx_ref[rows, cols].astype(F32) * fb_ref[...]
            yb = x0_ref[rows, cols].astype(F32) * z
            o_ref[rows, par * d:(par + 1) * d] = _dot(yb.astype(BF16), wb_ref[...]).astype(o_ref.dtype)


def _hy_inv(tabs, spec, vx2, x0c2, fbias, w_bb, n_seq, seq_len, row0_blk):
    m = seq_len // 2
    hw = fbias.shape[1]
    d = w_bb.shape[1]
    ce, se, _, _, cot, sot = tabs
    ae, be, ao, bo, ymid = spec
    tt = min(m, 128)
    per = m // tt
    nb = _seq_batch(seq_len, n_seq) if per == 1 else 1
    assert row0_blk % nb == 0
    mode = pl.Buffered(1) if nb * m * hw * 2 > BIG_BLOCK_BYTES // 2 else None
    tab = pl.BlockSpec((tt, m), lambda b, j: (j, 0))
    res = pl.BlockSpec((nb, m, hw), lambda b, j: (b, 0, 0), pipeline_mode=mode)
    rows_in = pl.BlockSpec((nb * tt, 2 * hw), lambda b, j: ((row0_blk // nb + b) * per + j, 0))
    return pl.pallas_call(
        _hy_inv_kernel,
        out_shape=jax.ShapeDtypeStruct((n_seq * m, 2 * d), BF16),
        grid=(n_seq // nb, per),
        in_specs=[tab, tab, tab, tab, res, res, res, res,
                  pl.BlockSpec((nb, 2, hw), lambda b, j: (b, 0, 0)),
                  rows_in, rows_in,
                  pl.BlockSpec((1, hw), lambda b, j: (0, 0)),
                  pl.BlockSpec((hw, d), lambda b, j: (0, 0), pipeline_mode=pl.Buffered(1))],
        out_specs=pl.BlockSpec((nb * tt, 2 * d), lambda b, j: (b * per + j, 0)),
        compiler_params=_cp(("arbitrary", "arbitrary")),
        name="hy_inv",
    )(ce, se, cot, sot, ae, be, ao, bo, ymid, vx2, x0c2, fbias, w_bb)


def _merge_kernel(tok, n_groups, xp_ref, xs_ref, pos_ref, g0_ref, g1_ref, pa_ref, pbc_ref, pbs_ref,
                  gate1_ref, sh2_ref, sc2_ref, gn_ref, wo_ref, wr_ref, br_ref,
                  x2_ref, h2_ref, rt_ref):
    i = pl.program_id(0)
    is_ctx = i < tok.nct
    pb = jnp.where(is_ctx, pbc_ref[...].astype(F32), pbs_ref[...].astype(F32))
    merged = g0_ref[...].astype(F32) * pa_ref[...].astype(F32) + g1_ref[...].astype(F32) * pb
    att = _dot(merged.astype(BF16), wo_ref[...])
    x = jnp.where(is_ctx, xp_ref[...], xs_ref[...] + pos_ref[...])
    x2 = x + gate1_ref[0] * att
    x2_ref[...] = x2
    h2 = _rms(x2, gn_ref[...]) * (1.0 + sc2_ref[0]) + sh2_ref[0]
    h2b = h2.astype(BF16)
    _store_row_tiles(h2_ref, 0, h2b)
    logits = _dot(h2b, wr_ref[...]) + br_ref[...]
    lane = lax.broadcasted_iota(jnp.int32, logits.shape, 1)
    gl = jnp.where(lane < n_groups, logits, -jnp.inf)
    gmax = jnp.max(gl, axis=-1, keepdims=True)
    gidx = jnp.min(jnp.where(gl == gmax, lane, n_groups), axis=-1, keepdims=True)
    rt_ref[...] = jnp.where(lane == LANES - 1, gidx.astype(F32), logits)


def _merge_out(xp, xs, pos, proj, gate_blk0, pa, pb_ctx, pb_lat, mod, g2, w_out, w_r, b_r, n_groups, tok):
    d = xp.shape[1]
    tm = tok.tm
    t_all = xp.shape[0] + xs.shape[0]
    kern = functools.partial(_merge_kernel, tok, n_groups)
    one = pl.Buffered(1)
    ns = d // 2 // LANES

    def modspec(k):
        return pl.BlockSpec((1, 1, d), lambda i: (tok.cond(i) * N_MOD + k, 0, 0))

    return pl.pallas_call(
        kern,
        out_shape=(jax.ShapeDtypeStruct((t_all, d), F32), jax.ShapeDtypeStruct((t_all * ns, LANES), jnp.uint32),
                   jax.ShapeDtypeStruct((t_all, LANES), F32)),
        grid=(tok.n,),
        in_specs=[
            pl.BlockSpec((tm, d), lambda i: (tok.ctx_idx(i), 0)),
            pl.BlockSpec((tm, d), lambda i: (tok.lat_idx(i), 0)),
            pl.BlockSpec((tm, d), lambda i: (tok.pos_idx(i), 0)),
            pl.BlockSpec((tm, d), lambda i: (i, gate_blk0)),
            pl.BlockSpec((tm, d), lambda i: (i, gate_blk0 + 1)),
            pl.BlockSpec((tm, d), lambda i: (i, 0)),
            pl.BlockSpec((tm, d), lambda i: (tok.ctx_idx(i), 0)),
            pl.BlockSpec((tm, d), lambda i: (tok.lat_idx(i), 0)),
            modspec(2), modspec(3), modspec(4),
            pl.BlockSpec((1, d), lambda i: (0, 0)),
            pl.BlockSpec((d, d), lambda i: (0, 0), pipeline_mode=one),
            pl.BlockSpec((d, LANES), lambda i: (0, 0)),
            pl.BlockSpec((1, LANES), lambda i: (0, 0)),
        ],
        out_specs=(pl.BlockSpec((tm, d), lambda i: (i, 0)), pl.BlockSpec((tm * ns, LANES), lambda i: (i, 0)),
                   pl.BlockSpec((tm, LANES), lambda i: (i, 0))),
        compiler_params=_cp(("parallel",)),
        name="merge_out",
    )(xp, xs, pos, proj, proj, pa, pb_ctx, pb_lat, mod, mod, mod, g2, w_out, w_r, b_r)


ROW_UNROLL = 8

def _moe_kernel(n_tiles, n_groups, epg, tg_ref, tnv_ref, rt_ref,
                h2_hbm, wr_ref, br_ref, wg_ref, wu_ref, wd_ref, out_hbm,
                xbuf, xb_ref, acc, obuf, gws, gsem, ssem):
    i = pl.program_id(0)
    j = pl.program_id(1)
    tm = xb_ref.shape[0]
    ns = h2_hbm.shape[1]
    nv = tnv_ref[i]
    slot = i % 2

    def buf_rows(buf, slot_, r, rows):
        return buf.at[pl.ds(pl.multiple_of((slot_ * tm + r) * ns, ns), rows * ns), :]

    def gather_copy(tok_row, slot_, r):
        return pltpu.make_async_copy(h2_hbm.at[tok_row], buf_rows(xbuf, slot_, r, 1), gsem.at[slot_])

    def gather_wait(slot_, rows):
        dst = buf_rows(xbuf, slot_, 0, rows)
        pltpu.make_async_copy(dst, dst, gsem.at[slot_]).wait()

    def scatter_copy(tok_row, slot_, r):
        return pltpu.make_async_copy(buf_rows(obuf, slot_, r, 1), out_hbm.at[tok_row], ssem.at[0])

    def scatter_wait(slot_, rows):
        src = buf_rows(obuf, slot_, 0, rows)
        pltpu.make_async_copy(src, src, ssem.at[0]).wait()

    def for_rows(count, fn):
        n_full = count // ROW_UNROLL

        def body(b, c):
            for k in range(ROW_UNROLL):
                fn(b * ROW_UNROLL + k)
            return c

        def tail(r, c):
            fn(r)
            return c

        lax.fori_loop(0, n_full, body, 0)
        lax.fori_loop(n_full * ROW_UNROLL, count, tail, 0)

    def wait_scatter(count, slot_):
        p = tm
        while p >= 1:
            @pl.when((count & p) != 0)
            def _(p=p):
                scatter_wait(slot_, p)
            p //= 2

    nv_prev = tnv_ref[jnp.maximum(i - 1, 0)]
    nv_next = tnv_ref[jnp.minimum(i + 1, n_tiles - 1)]
    prev_deferred = jnp.logical_and(i > 0, jnp.logical_and(nv_prev == tm, nv > 0))
    deferred = jnp.logical_and(nv == tm, nv_next > 0)

    @pl.when(j == 0)
    def _():
        @pl.when(i == 0)
        def _():
            for_rows(tm, lambda r: gather_copy(rt_ref[r], 0, r).start())

        @pl.when(jnp.logical_or(i == 0, nv_prev > 0))
        def _():
            gather_wait(slot, tm)

    @pl.when(jnp.logical_and(j == 0, nv > 0))
    def _():
        grp = tg_ref[i]
        xb = _load_row_tiles(xbuf, slot * tm * ns, tm, ns)
        xb_ref[...] = xb
        logits = _dot(xb, wr_ref[...]) + br_ref[...]
        lane = lax.broadcasted_iota(jnp.int32, logits.shape, 1)
        gl = jnp.where(lane < n_groups, logits, -jnp.inf)
        gmax = jnp.max(gl, axis=-1, keepdims=True)
        ge = jnp.exp(gl - gmax)
        g_w = jnp.sum(jnp.where(lane == grp, ge, 0.0), axis=-1, keepdims=True) / jnp.sum(ge, axis=-1, keepdims=True)
        base = n_groups + grp * epg
        e = [jnp.sum(jnp.where(lane == base + k, logits, 0.0), axis=-1, keepdims=True) for k in range(epg)]
        sel = []
        for k in range(epg):
            rank = jnp.zeros_like(e[k])
            for m in range(epg):
                if m == k:
                    continue
                ahead = (e[m] > e[k]) if m > k else (e[m] >= e[k])
                rank = rank + ahead.astype(F32)
            sel.append(rank < TOP_K_INNER)
        emax = functools.reduce(jnp.maximum, e)
        ex = [jnp.where(sel[k], jnp.exp(e[k] - emax), 0.0) for k in range(epg)]
        den = functools.reduce(lambda a, b: a + b, ex)
        for k in range(epg):
            gws[k] = ex[k] / den * g_w
        acc[...] = jnp.zeros(acc.shape, acc.dtype)

    def expert_step(scatter_prev):
        per = tm // epg
        for k in range(per):
            r = j * per + k
            if scatter_prev:
                scatter_copy(rt_ref[(i - 1) * tm + r], 1 - slot, r).start()
            gather_copy(rt_ref[(i + 1) * tm + r], 1 - slot, r).start()
        xb = xb_ref[...]
        g = _dot(xb, wg_ref[...])
        u = _dot(xb, wu_ref[...])
        hid = (g * jax.nn.sigmoid(g)) * u * gws[j]
        acc[...] += _dot(hid.astype(BF16), wd_ref[...])

    @pl.when(jnp.logical_and(nv > 0, prev_deferred))
    def _():
        expert_step(True)

    @pl.when(jnp.logical_and(nv > 0, jnp.logical_not(prev_deferred)))
    def _():
        expert_step(False)

    @pl.when(j == epg - 1)
    def _():
        @pl.when(prev_deferred)
        def _():
            scatter_wait(1 - slot, tm)

        @pl.when(nv > 0)
        def _():
            _store_row_tiles(obuf, slot * tm * ns, acc[...].astype(BF16))

        @pl.when(jnp.logical_and(nv > 0, jnp.logical_not(deferred)))
        def _():
            for_rows(nv, lambda r: scatter_copy(rt_ref[i * tm + r], slot, r).start())
            wait_scatter(nv, slot)


def _moe(h2p, w_r, b_r, w_g, w_u, w_d, tile_grp, tile_nv, row_tok, n_groups, epg, tm):
    t, ns, _ = h2p.shape
    d = 2 * ns * LANES
    ff = w_g.shape[-1]
    n_tiles = tile_grp.shape[0]
    assert tm & (tm - 1) == 0, "row-count waits decompose tm in binary"
    kern = functools.partial(_moe_kernel, n_tiles, n_groups, epg)

    def widx(i, j, tg, tnv, rt):
        return tg[i] * epg + jnp.where(tnv[i] > 0, j, epg - 1)

    grid_spec = pltpu.PrefetchScalarGridSpec(
        num_scalar_prefetch=3,
        grid=(n_tiles, epg),
        in_specs=[
            pl.BlockSpec(memory_space=pl.ANY),
            pl.BlockSpec((d, LANES), lambda i, j, tg, tnv, rt: (0, 0)),
            pl.BlockSpec((1, LANES), lambda i, j, tg, tnv, rt: (0, 0)),
            pl.BlockSpec((None, d, ff), lambda i, j, tg, tnv, rt: (widx(i, j, tg, tnv, rt), 0, 0)),
            pl.BlockSpec((None, d, ff), lambda i, j, tg, tnv, rt: (widx(i, j, tg, tnv, rt), 0, 0)),
            pl.BlockSpec((None, ff, d), lambda i, j, tg, tnv, rt: (widx(i, j, tg, tnv, rt), 0, 0)),
        ],
        out_specs=pl.BlockSpec(memory_space=pl.ANY),
        scratch_shapes=[
            pltpu.VMEM((2 * tm * ns, LANES), jnp.uint32),
            pltpu.VMEM((tm, d), BF16),
            pltpu.VMEM((tm, d), F32),
            pltpu.VMEM((2 * tm * ns, LANES), jnp.uint32),
            pltpu.VMEM((epg, tm, 1), F32),
            pltpu.SemaphoreType.DMA((2,)),
            pltpu.SemaphoreType.DMA((1,)),
        ],
    )
    return pl.pallas_call(
        kern,
        out_shape=jax.ShapeDtypeStruct((t, ns, LANES), jnp.uint32),
        grid_spec=grid_spec,
        compiler_params=_cp(("arbitrary", "arbitrary")),
        name="moe",
    )(tile_grp, tile_nv, row_tok, h2p, w_r, b_r, w_g, w_u, w_d)


def _moe_plan(grp, n_groups, tm, n_tiles):
    t = grp.shape[0]
    onehot = (grp[:, None] == jnp.arange(n_groups, dtype=jnp.int32)[None, :]).astype(jnp.int32)
    csum = jnp.cumsum(onehot, axis=0)
    counts = csum[-1]
    rank = jnp.take_along_axis(csum, grp[:, None], axis=1)[:, 0] - 1
    ntile_g = (counts + tm - 1) // tm
    tile_end = jnp.cumsum(ntile_g)
    tile_off = tile_end - ntile_g
    slot = tile_off[grp] * tm + rank
    row_tok = jnp.zeros((n_tiles * tm,), jnp.int32).at[slot].set(jnp.arange(t, dtype=jnp.int32))
    tiles = jnp.arange(n_tiles, dtype=jnp.int32)
    tile_grp = jnp.minimum(jnp.sum((tiles[:, None] >= tile_end[None, :]).astype(jnp.int32), axis=1),
                           n_groups - 1)
    tile_nv = jnp.clip(counts[tile_grp] - (tiles - tile_off[tile_grp]) * tm, 0, tm)
    tile_nv = jnp.where(tiles < tile_end[-1], tile_nv, 0)
    return tile_grp.astype(jnp.int32), tile_nv.astype(jnp.int32), row_tok


def _final_kernel(x2_ref, m_ref, gate2_ref, gf_ref, o_ref):
    tm, d = x2_ref.shape
    ns = d // 2 // LANES
    m = _load_row_tiles(m_ref, 0, tm, ns).astype(F32)
    x3 = x2_ref[...] + gate2_ref[0] * m
    o_ref[...] = _rms(x3, gf_ref[...])


def _final(x2, moe, mod, g_final, row_blk0, n_rows, cond_of_tile, tm):
    d = x2.shape[1]
    ns = d // 2 // LANES
    return pl.pallas_call(
        _final_kernel,
        out_shape=jax.ShapeDtypeStruct((n_rows, d), F32),
        grid=(n_rows // tm,),
        in_specs=[pl.BlockSpec((tm, d), lambda i: (row_blk0 + i, 0)),
                  pl.BlockSpec((tm * ns, LANES), lambda i: (row_blk0 + i, 0)),
                  pl.BlockSpec((1, 1, d), lambda i: (cond_of_tile(i) * N_MOD + 5, 0, 0)),
                  pl.BlockSpec((1, d), lambda i: (0, 0))],
        out_specs=pl.BlockSpec((tm, d), lambda i: (i, 0)),
        compiler_params=_cp(("parallel",)),
        name="final",
    )(x2, moe, mod, g_final)


def _grid_pos_embed(n_tokens, dim):
    rows_n = n_tokens // GRID_W
    quarter = dim // 4
    omega = 1.0 / (10000.0 ** (jnp.arange(quarter, dtype=F32) / quarter))

    def emb(n):
        ang = jnp.arange(n).astype(F32)[:, None] * omega[None, :]
        return jnp.concatenate([jnp.sin(ang), jnp.cos(ang)], axis=-1)

    e_r = jnp.broadcast_to(emb(rows_n)[:, None, :], (rows_n, GRID_W, dim // 2))
    e_c = jnp.broadcast_to(emb(GRID_W)[None, :, :], (rows_n, GRID_W, dim // 2))
    return jnp.concatenate([e_r, e_c], axis=-1).reshape(n_tokens, dim)


def _state_to_lanes(s):
    b, _, _, g, p = s.shape
    return jnp.transpose(s, (0, 3, 2, 1, 4)).reshape(b, g, 4 * p)


def _lanes_to_state(x, p):
    b, g, _ = x.shape
    return jnp.transpose(x.reshape(b, g, 2, 2, p), (0, 3, 2, 1, 4))


def kernel(x_prompt, x_sample, state_ssm, c, c_ctx, w_ada, b_ada, g_norm1, g_norm2, w_in, ssm_a_re, ssm_a_im, ssm_log_dt, ssm_b_re, ssm_b_im, ssm_c_re, ssm_c_im, ssm_d, w_glu, b_glu, w_short, b_short, w_f1, b_f1, freq1, w_f2, b_f2, freq2, w_f3, b_f3, filter_bias, w_branch_a, w_branch_b, w_out, w_router_group, b_router_group, w_router_expert, b_router_expert, w_exp_gate, w_exp_up, w_exp_down, g_final):
    depth = w_ada.shape[0]
    assert depth == 1, "single-layer pipeline"
    bsz, seq, d = x_prompt.shape
    dbsz, dseq, _ = x_sample.shape
    g_all, p_state = ssm_a_re.shape[2:]
    ssm_w = ssm_d.shape[1]
    hw = filter_bias.shape[1]
    n_groups, _, epg = w_router_expert.shape[1:]
    ff = w_exp_gate.shape[-1]
    assert ssm_w // g_all * CHUNK == MXU_SIDE and 4 * p_state == MXU_SIDE
    assert ssm_w == hw and d == 2 * hw
    n_ctx = bsz * seq
    n_lat = dbsz * dseq
    t_all = n_ctx + n_lat

    xp = x_prompt.reshape(n_ctx, d)
    xs = x_sample.reshape(n_lat, d)
    pos = _grid_pos_embed(dseq, d)

    n_cond = 1 + dbsz
    cond8 = jnp.zeros((SUBLANES, d), F32).at[0].set(c_ctx).at[1:n_cond].set(c)
    mod = _ada_mod(cond8, w_ada[0], b_ada[0][None])
    mod = mod[:n_cond].reshape(n_cond * N_MOD, 1, d)

    tm_e = min(256, seq, dseq)
    tok_e = _Tok(n_ctx, n_lat, dseq, tm_e)
    tm_l = min(512, n_ctx, dseq)
    tok_l = _Tok(n_ctx, n_lat, dseq, tm_l)
    gate_col0 = ssm_w + 3 * hw
    h1 = _norm_mod(xp, xs, pos, mod, g_norm1, tok_l)
    tm_p, tn_p = min(1024, t_all), min(1024, hw)
    n_in = w_in.shape[2]
    u_a = _in_proj(h1, w_in[0], (0, 0), (0, ssm_w), F32, tm_p, tn_p, "in_proj_a")
    proj = _in_proj(h1, w_in[0], (gate_col0, n_in), (ssm_w, gate_col0), BF16, tm_p, tn_p, "in_proj")
    n_gate_cols = n_in - gate_col0

    w_state, w_so, a_step = _ssm_operators(ssm_a_re[0], ssm_a_im[0], ssm_log_dt[0], ssm_b_re[0], ssm_b_im[0],
                                           ssm_c_re[0], ssm_c_im[0], ssm_d[0])
    nc_ctx = seq // CHUNK
    nc_lat = dseq // CHUNK
    seq_per_blk = max(1, min(bsz, MXU_SIDE // nc_ctx))
    assert bsz % seq_per_blk == 0 and n_ctx % dseq == 0
    s0_ctx = jnp.zeros((bsz, g_all, 4 * p_state), F32)
    y_ctx, fin_ctx = _ssm_call(u_a, 0, n_ctx, w_state, w_so, a_step, s0_ctx, seq_per_blk, nc_ctx, "ssm_ctx")
    s0_lat = _state_to_lanes(state_ssm[:, 0].astype(F32))
    y_lat, _ = _ssm_call(u_a, n_ctx // dseq, n_lat, w_state, w_so, a_step, s0_lat, 1, nc_lat, "ssm_lat")
    new_state = _lanes_to_state(fin_ctx, p_state)[:, None]

    pa = _glu_branch(y_ctx, y_lat, w_glu[0].astype(BF16), b_glu[0][None], w_branch_a[0].astype(BF16), tok_l)

    w_short3 = jnp.transpose(w_short[0].reshape(-1, 3, hw), (1, 0, 2))
    b_short3 = b_short[0].reshape(3, 1, hw)
    vx, x0c = _short_conv(proj, n_gate_cols // hw, hw, w_short3, b_short3, n_ctx, seq, dseq, tm_e)
    rates = jnp.abs(jnp.linspace(math.log(DECAY_TARGET) / DECAY_FAST, math.log(DECAY_TARGET) / DECAY_SLOW,
                                 hw, dtype=F32))
    rates2 = jnp.concatenate([rates, rates])[None]
    fbias = filter_bias[0][None].astype(F32)
    w_bb = w_branch_b[0].astype(BF16)
    vx2 = vx.reshape(t_all // 2, 2 * hw)
    x0c2 = x0c.reshape(t_all // 2, 2 * hw)
    pbs = []
    for seq_len, n_seq, row0 in ((seq, bsz, 0), (dseq, dbsz, n_ctx // dseq)):
        tabs = _dft_tables(seq_len)
        a_f, bm_f, colabs = _filt_gen(seq_len, w_f1[0], b_f1[0][None], freq1[0][None], w_f2[0],
                                      b_f2[0][None], freq2[0][None], w_f3[0], b_f3[0][None], rates2)
        kspec = _filt_dft(tabs, a_f.reshape(seq_len // 2, 2 * hw), bm_f.reshape(seq_len // 2, 2 * hw),
                          colabs, seq_len)
        spec = _hy_fwd(tabs, vx2, kspec, n_seq, seq_len, row0)
        pb2 = _hy_inv(tabs, spec, vx2, x0c2, fbias, w_bb, n_seq, seq_len, row0)
        pbs.append(pb2.reshape(n_seq * seq_len, d))
    pb_ctx, pb_lat = pbs

    w_r = jnp.zeros((d, LANES), F32)
    w_r = w_r.at[:, :n_groups].set(w_router_group[0])
    w_r = w_r.at[:, n_groups:n_groups + n_groups * epg].set(
        jnp.transpose(w_router_expert[0], (1, 0, 2)).reshape(d, n_groups * epg))
    b_r = jnp.zeros((1, LANES), F32)
    b_r = b_r.at[0, :n_groups].set(b_router_group[0])
    b_r = b_r.at[0, n_groups:n_groups + n_groups * epg].set(b_router_expert[0].reshape(-1))
    w_r = w_r.astype(BF16)
    x2, h2, rout = _merge_out(xp, xs, pos, proj, 0, pa, pb_ctx, pb_lat, mod, g_norm2,
                              w_out[0].astype(BF16), w_r, b_r, n_groups, tok_e)

    tm_m = min(512, t_all // n_groups)
    n_tiles = t_all // tm_m + n_groups
    grp = rout[:, LANES - 1].astype(jnp.int32)
    tile_grp, tile_nv, row_tok = _moe_plan(grp, n_groups, tm_m, n_tiles)
    ns = d // 2 // LANES
    moe = _moe(h2.reshape(t_all, ns, LANES), w_r, b_r, w_exp_gate[0].astype(BF16), w_exp_up[0].astype(BF16),
               w_exp_down[0].astype(BF16), tile_grp, tile_nv, row_tok, n_groups, epg, tm_m)
    moe = moe.reshape(t_all * ns, LANES)

    gf = g_final[None]
    y_prompt = _final(x2, moe, mod, gf, 0, n_ctx, lambda i: 0, tm_l)
    lat_tiles = dseq // tm_l
    y_sample = _final(x2, moe, mod, gf, n_ctx // tm_l, n_lat, lambda i: 1 + i // lat_tiles, tm_l)
    return (y_prompt.reshape(bsz, seq, d), y_sample.reshape(dbsz, dseq, d), new_state)
```

```python
import functools
import math

import numpy as np
import jax
import jax.numpy as jnp
from jax import lax
from jax.experimental import pallas as pl
from jax.experimental.pallas import tpu as pltpu

F32 = jnp.float32
BF16 = jnp.bfloat16
EPS = 1e-6
GRID_W = 64
N_MOD = 6
TOP_K_INNER = 2
DECAY_TARGET = 1e-2
DECAY_FAST = 0.3
DECAY_SLOW = 1.5
LANES = 128
SUBLANES = 8
MXU_SIDE = 256
VMEM_LIMIT = 56 * 1024 * 1024


def _cp(sem, vmem=VMEM_LIMIT):
    return pltpu.CompilerParams(dimension_semantics=sem, vmem_limit_bytes=vmem)


def _dot(a, b):
    return jnp.dot(a, b, preferred_element_type=F32)


def _split(a):
    hi = a.astype(BF16)
    lo = (a - hi.astype(F32)).astype(BF16)
    return hi, lo


def _dot3(a, b):
    a_hi, a_lo = _split(a)
    b_hi, b_lo = _split(b)
    return _dot(a_hi, b_hi) + _dot(a_lo, b_hi) + _dot(a_hi, b_lo)


def _rms(x, g):
    ms = jnp.mean(x * x, axis=-1, keepdims=True)
    return x * lax.rsqrt(ms + EPS) * g


def _pack_halves(xb):
    n = xb.shape[1] // 2
    lo = lax.bitcast_convert_type(xb[:, :n].astype(F32), jnp.uint32)
    hi = lax.bitcast_convert_type(xb[:, n:].astype(F32), jnp.uint32)
    return (lo >> 16) | (hi & jnp.uint32(0xFFFF0000))


def _unpack_halves(w):
    lo = lax.bitcast_convert_type(w << 16, F32)
    hi = lax.bitcast_convert_type(w & jnp.uint32(0xFFFF0000), F32)
    return jnp.concatenate([lo, hi], axis=1).astype(BF16)


def _to_pair_rows(x, scr):
    m, n = x.shape
    ev, od = [], []
    for c in range(n // LANES):
        scr[c * m:(c + 1) * m, :] = x[:, c * LANES:(c + 1) * LANES]
        ev.append(scr[pl.ds(c * m, m // 2, stride=2), :])
        od.append(scr[pl.ds(c * m + 1, m // 2, stride=2), :])
    return jnp.concatenate(ev + od, axis=1)


def _from_pair_rows(x2, scr):
    h, n2 = x2.shape
    n = n2 // 2
    out = []
    for c in range(n // LANES):
        scr[pl.ds(2 * c * h, h, stride=2), :] = x2[:, c * LANES:(c + 1) * LANES]
        scr[pl.ds(2 * c * h + 1, h, stride=2), :] = x2[:, n + c * LANES:n + (c + 1) * LANES]
        out.append(scr[2 * c * h:2 * (c + 1) * h, :])
    return jnp.concatenate(out, axis=1)


def _store_row_tiles(ref, row0, xb):
    w = _pack_halves(xb)
    m, n = w.shape
    ns = n // LANES
    for c in range(ns):
        ref[pl.ds(row0 + c, m, stride=ns), :] = w[:, c * LANES:(c + 1) * LANES]


def _load_row_tiles(ref, row0, m, ns):
    w = jnp.concatenate([ref[pl.ds(row0 + c, m, stride=ns), :] for c in range(ns)], axis=1)
    return _unpack_halves(w)


def _ada_kernel(c_ref, w_ref, b_ref, o_ref):
    c = c_ref[...]
    a = c * jax.nn.sigmoid(c)
    o_ref[...] = _dot3(a, w_ref[...]) + b_ref[...]


def _ada_mod(cond8, w_ada, b_ada):
    d, n = w_ada.shape
    tn = min(n, 1024)
    return pl.pallas_call(
        _ada_kernel,
        out_shape=jax.ShapeDtypeStruct((SUBLANES, n), F32),
        grid=(n // tn,),
        in_specs=[pl.BlockSpec((SUBLANES, d), lambda j: (0, 0)),
                  pl.BlockSpec((d, tn), lambda j: (0, j)),
                  pl.BlockSpec((1, tn), lambda j: (0, j))],
        out_specs=pl.BlockSpec((SUBLANES, tn), lambda j: (0, j)),
        compiler_params=_cp(("arbitrary",)),
        name="ada_mod",
    )(cond8, w_ada, b_ada)


class _Tok:
    def __init__(self, n_ctx, n_lat, lat_len, tm):
        assert n_ctx % tm == 0 and lat_len % tm == 0 and n_lat % lat_len == 0
        self.tm = tm
        self.nct = n_ctx // tm
        self.nst = n_lat // tm
        self.per_seq = lat_len // tm
        self.n = self.nct + self.nst

    def ctx_idx(self, i):
        return jnp.minimum(i, self.nct - 1)

    def lat_idx(self, i):
        return jnp.maximum(i - self.nct, 0)

    def pos_idx(self, i):
        return self.lat_idx(i) % self.per_seq

    def cond(self, i):
        return jnp.where(i < self.nct, 0, 1 + self.lat_idx(i) // self.per_seq)


def _norm_mod_kernel(tok, xp_ref, xs_ref, pos_ref, sh_ref, sc_ref, g_ref, h_ref):
    i = pl.program_id(0)

    def modulate(x):
        y = _rms(x, g_ref[...])
        return (y * (1.0 + sc_ref[0]) + sh_ref[0]).astype(h_ref.dtype)

    @pl.when(i < tok.nct)
    def _():
        h_ref[...] = modulate(xp_ref[...])

    @pl.when(i >= tok.nct)
    def _():
        h_ref[...] = modulate(xs_ref[...] + pos_ref[...])


def _norm_mod(xp, xs, pos, mod, g1, tok):
    d = xp.shape[1]
    tm = tok.tm
    t_all = xp.shape[0] + xs.shape[0]
    return pl.pallas_call(
        functools.partial(_norm_mod_kernel, tok),
        out_shape=jax.ShapeDtypeStruct((t_all, d), BF16),
        grid=(tok.n,),
        in_specs=[
            pl.BlockSpec((tm, d), lambda i: (tok.ctx_idx(i), 0)),
            pl.BlockSpec((tm, d), lambda i: (tok.lat_idx(i), 0)),
            pl.BlockSpec((tm, d), lambda i: (tok.pos_idx(i), 0)),
            pl.BlockSpec((1, 1, d), lambda i: (tok.cond(i) * N_MOD + 0, 0, 0)),
            pl.BlockSpec((1, 1, d), lambda i: (tok.cond(i) * N_MOD + 1, 0, 0)),
            pl.BlockSpec((1, d), lambda i: (0, 0)),
        ],
        out_specs=pl.BlockSpec((tm, d), lambda i: (i, 0)),
        compiler_params=_cp(("parallel",)),
        name="norm_mod",
    )(xp, xs, pos, mod, mod, g1)


def _in_proj_kernel(n_sig, h_ref, w_ref, o_ref, wb_ref):
    n = pl.program_id(0)
    m = pl.program_id(1)

    @pl.when(m == 0)
    def _():
        wb_ref[...] = w_ref[...].astype(BF16)

    acc = _dot(h_ref[...], wb_ref[...])

    @pl.when(n >= n_sig)
    def _():
        o_ref[...] = acc.astype(o_ref.dtype)

    @pl.when(n < n_sig)
    def _():
        o_ref[...] = jax.nn.sigmoid(acc).astype(o_ref.dtype)


def _in_proj(h, w_in, sig_cols, lin_cols, out_dtype, tm, tn, name):
    t_all, d = h.shape
    for c in sig_cols + lin_cols:
        assert c % tn == 0
    assert t_all % tm == 0
    n_sig = (sig_cols[1] - sig_cols[0]) // tn
    n_lin = (lin_cols[1] - lin_cols[0]) // tn
    sig0, lin0 = sig_cols[0] // tn, lin_cols[0] // tn

    def w_blk(n):
        return jnp.where(n < n_sig, sig0 + n, lin0 + n - n_sig)

    return pl.pallas_call(
        functools.partial(_in_proj_kernel, n_sig),
        out_shape=jax.ShapeDtypeStruct((t_all, (n_sig + n_lin) * tn), out_dtype),
        grid=(n_sig + n_lin, t_all // tm),
        in_specs=[pl.BlockSpec((tm, d), lambda n, m: (m, 0)),
                  pl.BlockSpec((d, tn), lambda n, m: (0, w_blk(n)))],
        out_specs=pl.BlockSpec((tm, tn), lambda n, m: (m, n)),
        scratch_shapes=[pltpu.VMEM((d, tn), BF16)],
        compiler_params=_cp(("arbitrary", "arbitrary")),
        name=name,
    )(h, w_in)


CHUNK = 16
GROUP_PAD = 8


def _ssm_operators(a_re, a_im, log_dt, b_re, b_im, c_re, c_im, d_skip):
    t = CHUNK
    g, p = a_re.shape[1:]
    c = b_re.shape[-1]
    lam = lax.complex(a_re.astype(F32), a_im.astype(F32))
    dt = jnp.exp(log_dt.astype(F32))[..., None]
    ld = lam * dt
    lam_bar = jnp.exp(ld)
    b_bar = ((lam_bar - 1.0) / lam)[..., None] * lax.complex(b_re.astype(F32), b_im.astype(F32))
    c_mat = lax.complex(c_re.astype(F32), c_im.astype(F32))
    taus = jnp.arange(t + 1, dtype=F32)
    e_pow = jnp.exp(taus[None, None, :, None] * ld[:, :, None, :])

    k_all = jnp.real(jnp.einsum('dgop,dgtp,dgpi->dgtoi', c_mat, e_pow[:, :, :t], b_bar))
    kf, kb = k_all[0], k_all[1]
    ii = np.arange(t)[None, :, None]
    jj = np.arange(t)[None, None, :]
    tt = np.arange(t)[:, None, None]
    place_f = jnp.asarray((jj - ii == tt).astype(np.float32))
    place_b = jnp.asarray((ii - jj == tt).astype(np.float32))
    kf = kf.at[:, 0].add(jnp.eye(c, dtype=F32)[None] * d_skip.astype(F32).reshape(g, c, 1))
    m = jnp.einsum('tij,gtoc->gicjo', place_f, kf) + jnp.einsum('tij,gtoc->gicjo', place_b, kb)
    m = m.reshape(g, t * c, t * c)

    e_f = e_pow[0][:, ::-1][:, 1:]
    e_b = e_pow[1][:, :t]
    ws_f = e_f[:, :, :, None] * b_bar[0][:, None]
    ws_b = e_b[:, :, :, None] * b_bar[1][:, None]

    def rows_ic(x):
        return jnp.transpose(x, (0, 1, 3, 2)).reshape(g, t * c, p)

    w_state = jnp.concatenate([rows_ic(jnp.real(ws_f)), rows_ic(jnp.real(ws_b)),
                               rows_ic(jnp.imag(ws_f)), rows_ic(jnp.imag(ws_b))], axis=-1)

    ce_f = c_mat[0][:, None] * e_pow[0][:, 1:, None, :]
    ce_b = c_mat[1][:, None] * e_pow[1][:, ::-1][:, :t, None, :]

    def cols_jc(x):
        return jnp.transpose(x, (0, 3, 1, 2)).reshape(g, p, t * c)

    w_so = jnp.concatenate([cols_jc(jnp.real(ce_f)), cols_jc(jnp.real(ce_b)),
                            cols_jc(-jnp.imag(ce_f)), cols_jc(-jnp.imag(ce_b))], axis=1)
    w_out = jnp.concatenate([m, w_so], axis=1)

    a_t = e_pow[:, :, t]
    a_step = jnp.stack([jnp.concatenate([jnp.real(a_t[0]), jnp.real(a_t[1])], axis=-1),
                        jnp.concatenate([jnp.imag(a_t[0]), jnp.imag(a_t[1])], axis=-1)])
    return w_state.astype(BF16), w_out.astype(BF16), a_step


def _ssm_kernel(n_seq, n_chunk, gb, g_all, x_ref, ws_ref, wo_ref, a_ref, s0_ref,
                y_ref, fin_ref, xq_ref, za_ref, zb_ref):
    phase = pl.program_id(1)
    j = pl.program_id(2)
    gp = g_all + GROUP_PAD
    nck = n_seq * n_chunk
    half = LANES // 2
    c = LANES // gb

    @pl.when(phase == 0)
    def _():
        xs = [x_ref[pl.ds(i, nck, stride=CHUNK), :] for i in range(CHUNK)]
        for q in range(gb):
            xq = jnp.concatenate([xs[i][:, q * c:(q + 1) * c] for i in range(CHUNK)], axis=-1).astype(BF16)
            xq_ref[j * gb + q] = xq
            z = _dot(xq, ws_ref[q])
            za_ref[pl.ds(j * gb + q, nck, stride=gp), :] = z[:, :LANES]
            zb_ref[pl.ds(j * gb + q, nck, stride=gp), :] = z[:, LANES:]

    @pl.when(jnp.logical_and(phase == 1, j == 0))
    def _():
        a_re = a_ref[0]
        a_im = a_ref[1]
        lane = lax.broadcasted_iota(jnp.int32, (g_all, LANES), 1)
        is_f = lane < half
        for s in range(n_seq):
            def step(k, carry):
                s_a, s_b = carry
                rf = pl.multiple_of((s * n_chunk + k) * gp, SUBLANES)
                rb = pl.multiple_of((s * n_chunk + n_chunk - 1 - k) * gp, SUBLANES)
                zf_a = za_ref[pl.ds(rf, g_all), :]
                zf_b = zb_ref[pl.ds(rf, g_all), :]
                zb_a = za_ref[pl.ds(rb, g_all), :]
                zb_b = zb_ref[pl.ds(rb, g_all), :]
                za_ref[pl.ds(rf, g_all), :] = jnp.where(is_f, s_a, zf_a)
                zb_ref[pl.ds(rf, g_all), :] = jnp.where(is_f, s_b, zf_b)
                za_ref[pl.ds(rb, g_all), :] = jnp.where(is_f, zb_a, s_a)
                zb_ref[pl.ds(rb, g_all), :] = jnp.where(is_f, zb_b, s_b)
                z_a = jnp.where(is_f, zf_a, zb_a)
                z_b = jnp.where(is_f, zf_b, zb_b)
                n_a = a_re * s_a - a_im * s_b + z_a
                n_b = a_re * s_b + a_im * s_a + z_b
                return n_a, n_b

            init = s0_ref[s]
            f_a, f_b = lax.fori_loop(0, n_chunk, step, (init[:, :LANES], init[:, LANES:]))
            fin_ref[s] = jnp.concatenate([f_a, f_b], axis=-1)

    @pl.when(phase == 1)
    def _():
        ys = []
        for q in range(gb):
            st_a = za_ref[pl.ds(j * gb + q, nck, stride=gp), :].astype(BF16)
            st_b = zb_ref[pl.ds(j * gb + q, nck, stride=gp), :].astype(BF16)
            lhs = jnp.concatenate([xq_ref[j * gb + q], st_a, st_b], axis=-1)
            ys.append(_dot(lhs, wo_ref[q]))
        for i in range(CHUNK):
            yi = jnp.concatenate([ys[q][:, i * c:(i + 1) * c] for q in range(gb)], axis=-1)
            y_ref[pl.ds(i, nck, stride=CHUNK), :] = yi


def _ssm_call(u, row_blk0, n_rows, w_state, w_out, a_step, s0, n_seq, n_chunk, name):
    g_all, w, _ = w_state.shape
    nck = n_seq * n_chunk
    blk = nck * CHUNK
    n_blk = n_rows // blk
    gb = LANES * CHUNK // w
    ngb = g_all // gb
    gp = g_all + GROUP_PAD
    kern = functools.partial(_ssm_kernel, n_seq, n_chunk, gb, g_all)
    return pl.pallas_call(
        kern,
        out_shape=(jax.ShapeDtypeStruct((n_rows, u.shape[1]), F32),
                   jax.ShapeDtypeStruct((n_blk * n_seq, g_all, w), F32)),
        grid=(n_blk, 2, ngb),
        in_specs=[
            pl.BlockSpec((blk, LANES), lambda b, ph, j: (row_blk0 + b, jnp.where(ph == 0, j, ngb - 1))),
            pl.BlockSpec((gb, w, w), lambda b, ph, j: (jnp.where(ph == 0, j, ngb - 1), 0, 0)),
            pl.BlockSpec((gb, 2 * w, w), lambda b, ph, j: (jnp.where(ph == 0, 0, j), 0, 0)),
            pl.BlockSpec((2, g_all, LANES), lambda b, ph, j: (0, 0, 0)),
            pl.BlockSpec((n_seq, g_all, w), lambda b, ph, j: (b, 0, 0)),
        ],
        out_specs=(
            pl.BlockSpec((blk, LANES), lambda b, ph, j: (b, jnp.where(ph == 0, 0, j))),
            pl.BlockSpec((n_seq, g_all, w), lambda b, ph, j: (b, 0, 0)),
        ),
        scratch_shapes=[pltpu.VMEM((g_all, nck, w), BF16),
                        pltpu.VMEM((nck * gp, LANES), F32), pltpu.VMEM((nck * gp, LANES), F32)],
        compiler_params=_cp(("arbitrary", "arbitrary", "arbitrary")),
        name=name,
    )(u, w_state, w_out, a_step, s0)


def _glu_kernel(nct, yc_ref, yl_ref, wg_ref, bg_ref, wb_ref, o_ref):
    i = pl.program_id(0)
    y = jax.nn.gelu(jnp.where(i < nct, yc_ref[...], yl_ref[...]))
    z = _dot(y.astype(BF16), wg_ref[...]) + bg_ref[...]
    ya = y * jax.nn.sigmoid(z)
    o_ref[...] = _dot(ya.astype(BF16), wb_ref[...]).astype(o_ref.dtype)


def _glu_branch(y_ctx, y_lat, w_glu, b_glu, w_ba, tok):
    c = y_ctx.shape[1]
    d = w_ba.shape[1]
    tm = tok.tm
    return pl.pallas_call(
        functools.partial(_glu_kernel, tok.nct),
        out_shape=jax.ShapeDtypeStruct((y_ctx.shape[0] + y_lat.shape[0], d), BF16),
        grid=(tok.n,),
        in_specs=[pl.BlockSpec((tm, c), lambda i: (tok.ctx_idx(i), 0)),
                  pl.BlockSpec((tm, c), lambda i: (tok.lat_idx(i), 0)),
                  pl.BlockSpec((c, c), lambda i: (0, 0)),
                  pl.BlockSpec((1, c), lambda i: (0, 0)),
                  pl.BlockSpec((c, d), lambda i: (0, 0))],
        out_specs=pl.BlockSpec((tm, d), lambda i: (i, 0)),
        compiler_params=_cp(("parallel",)),
        name="glu_branch",
    )(y_ctx, y_lat, w_glu, b_glu, w_ba)


def _short_conv_kernel(nct, per_ctx, per_lat, v_ref, x0_ref, x1_ref, vp_ref, x0p_ref, x1p_ref,
                       vn_ref, x0n_ref, x1n_ref, w_ref, b_ref, vx_ref, x0c_ref, scr):
    i = pl.program_id(0)
    tm = v_ref.shape[0]
    k = jnp.where(i < nct, i % per_ctx, (i - nct) % per_lat)
    per = jnp.where(i < nct, per_ctx, per_lat)
    first = k == 0
    last = k == per - 1
    row = lax.broadcasted_iota(jnp.int32, (tm, 1), 0)

    def conv(cur_ref, prev_ref, next_ref, part):
        x = cur_ref[...].astype(F32)
        hp = jnp.where(first, 0.0, prev_ref[SUBLANES - 1:SUBLANES, :].astype(F32))
        hn = jnp.where(last, 0.0, next_ref[0:1, :].astype(F32))
        xm = jnp.where(row == 0, hp, pltpu.roll(x, 1, axis=0))
        xq = jnp.where(row == tm - 1, hn, pltpu.roll(x, tm - 1, axis=0))
        w = w_ref[part]
        return b_ref[part] + xm * w[0:1, :] + x * w[1:2, :] + xq * w[2:3, :]

    v = conv(v_ref, vp_ref, vn_ref, 0)
    x0 = conv(x0_ref, x0p_ref, x0n_ref, 1)
    x1 = conv(x1_ref, x1p_ref, x1n_ref, 2)
    vx_ref[...] = _to_pair_rows(v * x1, scr).astype(vx_ref.dtype)
    x0c_ref[...] = _to_pair_rows(x0, scr).astype(x0c_ref.dtype)


def _short_conv(proj, col_blk0, hw, w_short3, b_short3, n_ctx, seq, dseq, tm):
    t = proj.shape[0]
    assert seq % tm == 0 and dseq % tm == 0
    nt = t // tm
    r8 = tm // SUBLANES
    nb8 = t // SUBLANES

    def cur(k):
        return pl.BlockSpec((tm, hw), lambda i: (i, col_blk0 + k))

    def prev(k):
        return pl.BlockSpec((SUBLANES, hw), lambda i: (jnp.maximum(i * r8 - 1, 0), col_blk0 + k))

    def nxt(k):
        return pl.BlockSpec((SUBLANES, hw), lambda i: (jnp.minimum((i + 1) * r8, nb8 - 1), col_blk0 + k))

    kern = functools.partial(_short_conv_kernel, n_ctx // tm, seq // tm, dseq // tm)
    return pl.pallas_call(
        kern,
        out_shape=(jax.ShapeDtypeStruct((t // 2, 2 * hw), BF16), jax.ShapeDtypeStruct((t // 2, 2 * hw), BF16)),
        grid=(nt,),
        in_specs=[cur(0), cur(1), cur(2), prev(0), prev(1), prev(2), nxt(0), nxt(1), nxt(2),
                  pl.BlockSpec((3, 3, hw), lambda i: (0, 0, 0)),
                  pl.BlockSpec((3, 1, hw), lambda i: (0, 0, 0))],
        out_specs=(pl.BlockSpec((tm // 2, 2 * hw), lambda i: (i, 0)),
                   pl.BlockSpec((tm // 2, 2 * hw), lambda i: (i, 0))),
        scratch_shapes=[pltpu.VMEM((hw // LANES * tm, LANES), F32)],
        compiler_params=_cp(("parallel",)),
        name="short_conv",
    )(proj, proj, proj, proj, proj, proj, proj, proj, proj, w_short3, b_short3)


def _filt_gen_kernel(seq_len, n_bands, wt_ref, wc_ref, ws_ref, b1_ref, f1_ref, w2_ref, b2_ref, f2_ref,
                     w3_ref, b3_ref, rate_ref, a_ref, bm_ref, abs_ref, scr):
    i = pl.program_id(0)
    tl = 2 * a_ref.shape[0]
    hw = a_ref.shape[1] // 2
    pos = lax.broadcasted_iota(jnp.int32, (tl, 1), 0) + i * tl
    t = pos.astype(F32) / seq_len
    bands = (lax.broadcasted_iota(jnp.int32, (1, wc_ref.shape[0]), 1) + 1).astype(F32)
    ang = (2.0 * math.pi) * t * bands
    pre = t * wt_ref[...] + _dot3(jnp.cos(ang), wc_ref[...]) + _dot3(jnp.sin(ang), ws_ref[...])
    h = jnp.sin(f1_ref[...] * (pre + b1_ref[...]))
    h = jnp.sin(f2_ref[...] * (_dot3(h, w2_ref[...]) + b2_ref[...]))
    h = _dot3(h, w3_ref[...]) + b3_ref[...]
    h = h * jnp.exp(-t * rate_ref[...])
    h_f = h[:, :hw]
    h_b = h[:, hw:]
    colabs = jnp.sum(jnp.abs(h_f) + jnp.abs(h_b), axis=0, keepdims=True)
    h_bp = jnp.where(pos == 0, 0.0, h_b)
    a_ref[...] = _to_pair_rows(h_f + h_bp, scr).astype(a_ref.dtype)
    bm_ref[...] = _to_pair_rows(h_bp - h_f, scr).astype(bm_ref.dtype)

    @pl.when(i == 0)
    def _():
        abs_ref[...] = colabs

    @pl.when(i > 0)
    def _():
        abs_ref[...] += colabs


def _filt_gen(seq_len, w_f1, b_f1, freq1, w_f2, b_f2, freq2, w_f3, b_f3, rates2):
    n_emb, hid_raw = w_f1.shape
    n_bands = (n_emb - 1) // 2
    hw2 = w_f3.shape[1]
    hw = hw2 // 2
    tl = min(seq_len, 512)
    hid = LANES
    assert n_bands <= LANES and hid_raw <= LANES

    def pad(x, rows, cols):
        return jnp.zeros((rows, cols), F32).at[:x.shape[0], :x.shape[1]].set(x.astype(F32))

    wt = pad(w_f1[0:1], 1, hid)
    wc = pad(w_f1[1:1 + n_bands], LANES, hid)
    ws = pad(w_f1[1 + n_bands:], LANES, hid)
    b_f1, freq1, b_f2, freq2 = (pad(x, 1, hid) for x in (b_f1, freq1, b_f2, freq2))
    w_f2 = pad(w_f2, hid, hid)
    w_f3 = pad(w_f3, hid, hw2)
    full = lambda shape: pl.BlockSpec(shape, lambda i: (0,) * len(shape))
    kern = functools.partial(_filt_gen_kernel, seq_len, n_bands)
    return pl.pallas_call(
        kern,
        out_shape=(jax.ShapeDtypeStruct((seq_len // 2, hw2), BF16), jax.ShapeDtypeStruct((seq_len // 2, hw2), BF16),
                   jax.ShapeDtypeStruct((1, hw), F32)),
        grid=(seq_len // tl,),
        in_specs=[full((1, hid)), full((LANES, hid)), full((LANES, hid)), full((1, hid)), full((1, hid)),
                  full((hid, hid)), full((1, hid)), full((1, hid)), full((hid, hw2)), full((1, hw2)),
                  full((1, hw2))],
        out_specs=(pl.BlockSpec((tl // 2, hw2), lambda i: (i, 0)), pl.BlockSpec((tl // 2, hw2), lambda i: (i, 0)),
                   full((1, hw))),
        scratch_shapes=[pltpu.VMEM((hw // LANES * tl, LANES), F32)],
        compiler_params=_cp(("arbitrary",)),
        name="filt_gen",
    )(wt, wc, ws, b_f1, freq1, w_f2, b_f2, freq2, w_f3, b_f3, rates2)


def _trig_tables(row_hi, row_lo, col, period):
    col = col[None, :]
    ang_lo = ((row_lo[:, None] * col) % period).astype(F32) * (2.0 * math.pi / period)
    ang_hi = ((row_hi[:, None] * col) % period).astype(F32) * (2.0 * math.pi / period)
    c_lo, s_lo = jnp.cos(ang_lo)[None], jnp.sin(ang_lo)[None]
    c_hi, s_hi = jnp.cos(ang_hi)[:, None], jnp.sin(ang_hi)[:, None]
    n = row_hi.shape[0] * row_lo.shape[0]
    ctab = (c_hi * c_lo - s_hi * s_lo).reshape(n, col.shape[1])
    stab = (s_hi * c_lo + c_hi * s_lo).reshape(n, col.shape[1])
    return ctab.astype(BF16), stab.astype(BF16)


def _dft_tables(seq_len):
    m = seq_len // 2
    r = 1
    while r * r < m:
        r *= 2
    assert m % r == 0
    i32 = jnp.int32
    idx = jnp.arange(m, dtype=i32)
    hi = jnp.arange(m // r, dtype=i32) * r
    lo = jnp.arange(r, dtype=i32)
    ce, se = _trig_tables(hi, lo, idx, 2 * m)
    co, so = _trig_tables(hi, lo, 2 * idx + 1, 2 * seq_len)
    cot, sot = _trig_tables(2 * hi, 2 * lo + 1, idx, 2 * seq_len)
    return ce, se, co, so, cot, sot


def _alt_row(n):
    return (1 - 2 * (lax.broadcasted_iota(jnp.int32, (SUBLANES, n), 1) % 2)).astype(BF16)


def _filt_dft_kernel(seq_len, ce_ref, se_ref, co_ref, so_ref, ae_ref, ao_ref, be_ref, bo_ref, abs_ref,
                     krl_ref, krh_ref, kil_ref, kih_ref, kmid_ref):
    i = pl.program_id(0)
    scale = (1.0 / seq_len) / (abs_ref[...] + EPS)
    ec = _dot(ce_ref[...], ae_ref[...])
    oc = _dot(co_ref[...], ao_ref[...])
    es = _dot(se_ref[...], be_ref[...])
    os_ = _dot(so_ref[...], bo_ref[...])
    krl_ref[...] = (ec + oc) * scale
    krh_ref[...] = (ec - oc) * scale
    kil_ref[...] = (es + os_) * scale
    kih_ref[...] = (os_ - es) * scale

    @pl.when(i == 0)
    def _():
        sgn = _alt_row(ae_ref.shape[0])
        kmid_ref[0:1, :] = _dot(sgn, ae_ref[...])[0:1] * scale
        kmid_ref[1:2, :] = _dot(sgn, bo_ref[...])[0:1] * scale


def _filt_dft(tabs, a2, bm2, colabs, seq_len):
    m = seq_len // 2
    hw = a2.shape[1] // 2
    ce, se, co, so = tabs[:4]
    tf = min(m, 256)
    kern = functools.partial(_filt_dft_kernel, seq_len)
    mode = pl.Buffered(1) if m * hw * 2 > BIG_BLOCK_BYTES // 2 else None
    tab = pl.BlockSpec((tf, m), lambda i: (i, 0))
    res = lambda c: pl.BlockSpec((m, hw), lambda i: (0, c), pipeline_mode=mode)
    out = pl.BlockSpec((tf, hw), lambda i: (i, 0))
    sds = jax.ShapeDtypeStruct((m, hw), F32)
    return pl.pallas_call(
        kern,
        out_shape=(sds, sds, sds, sds, jax.ShapeDtypeStruct((2, hw), F32)),
        grid=(m // tf,),
        in_specs=[tab, tab, tab, tab, res(0), res(1), res(0), res(1), pl.BlockSpec((1, hw), lambda i: (0, 0))],
        out_specs=(out, out, out, out, pl.BlockSpec((2, hw), lambda i: (0, 0))),
        compiler_params=_cp(("arbitrary",)),
        name="filt_dft",
    )(ce, se, co, so, a2, a2, bm2, bm2, colabs)


def _hy_fwd_kernel(ce_ref, se_ref, co_ref, so_ref, ve_ref, vo_ref, krl_ref, krh_ref, kil_ref, kih_ref, kmid_ref,
                   ae_ref, be_ref, ao_ref, bo_ref, ymid_ref):
    j = pl.program_id(1)
    tf, m = ce_ref.shape
    nb = ae_ref.shape[0]
    krl, krh, kil, kih = krl_ref[...], krh_ref[...], kil_ref[...], kih_ref[...]
    row = lax.broadcasted_iota(jnp.int32, (tf, 1), 0) + j * tf
    dc = jnp.where(row == 0, 0.5, 1.0)
    for s in range(nb):
        ve = ve_ref[s * m:(s + 1) * m, :]
        vo = vo_ref[s * m:(s + 1) * m, :]
        ec = _dot(ce_ref[...], ve)
        oc = _dot(co_ref[...], vo)
        es = _dot(se_ref[...], ve)
        os_ = _dot(so_ref[...], vo)
        p_lo, p_hi = ec + oc, ec - oc
        q_lo, q_hi = es + os_, os_ - es
        yre_lo = (p_lo * krl + q_lo * kil) * dc
        yim_lo = p_lo * kil - q_lo * krl
        yre_hi = (p_hi * krh + q_hi * kih) * dc
        yim_hi = p_hi * kih - q_hi * krh
        ae_ref[s] = (yre_lo + yre_hi).astype(ae_ref.dtype)
        be_ref[s] = (yim_lo - yim_hi).astype(be_ref.dtype)
        ao_ref[s] = (yre_lo - yre_hi).astype(ao_ref.dtype)
        bo_ref[s] = (yim_lo + yim_hi).astype(bo_ref.dtype)

    @pl.when(j == 0)
    def _():
        sgn = _alt_row(m)
        kr_m, ki_m = kmid_ref[0:1, :], kmid_ref[1:2, :]
        for s in range(nb):
            p_m = _dot(sgn, ve_ref[s * m:(s + 1) * m, :])[0:1]
            q_m = _dot(sgn, vo_ref[s * m:(s + 1) * m, :])[0:1]
            ymid_ref[s, 0:1, :] = p_m * kr_m + q_m * ki_m
            ymid_ref[s, 1:2, :] = p_m * ki_m - q_m * kr_m


BIG_BLOCK_BYTES = 4 * 1024 * 1024


def _seq_batch(seq_len, n_seq):
    nb = max(1, min(n_seq, 4 * MXU_SIDE // seq_len))
    while n_seq % nb:
        nb -= 1
    return nb


def _hy_fwd(tabs, vx2, kspec, n_seq, seq_len, row0_blk):
    m = seq_len // 2
    hw = vx2.shape[1] // 2
    ce, se, co, so = tabs[:4]
    krl, krh, kil, kih, kmid = kspec
    tf = min(m, 256)
    nb = _seq_batch(seq_len, n_seq)
    assert row0_blk % nb == 0
    mode = pl.Buffered(1) if nb * m * hw * 2 > BIG_BLOCK_BYTES // 2 else None
    tab = pl.BlockSpec((tf, m), lambda b, j: (j, 0))
    vspec = lambda c: pl.BlockSpec((nb * m, hw), lambda b, j: (row0_blk // nb + b, c), pipeline_mode=mode)
    kt = pl.BlockSpec((tf, hw), lambda b, j: (j, 0))
    out = pl.BlockSpec((nb, tf, hw), lambda b, j: (b, j, 0))
    sds = jax.ShapeDtypeStruct((n_seq, m, hw), BF16)
    return pl.pallas_call(
        _hy_fwd_kernel,
        out_shape=(sds, sds, sds, sds, jax.ShapeDtypeStruct((n_seq, 2, hw), F32)),
        grid=(n_seq // nb, m // tf),
        in_specs=[tab, tab, tab, tab, vspec(0), vspec(1), kt, kt, kt, kt,
                  pl.BlockSpec((2, hw), lambda b, j: (0, 0))],
        out_specs=(out, out, out, out, pl.BlockSpec((nb, 2, hw), lambda b, j: (b, 0, 0))),
        compiler_params=_cp(("arbitrary", "arbitrary")),
        name="hy_fwd",
    )(ce, se, co, so, vx2, vx2, krl, krh, kil, kih, kmid)


def _hy_inv_kernel(ce_ref, se_ref, cot_ref, sot_ref, ae_ref, be_ref, ao_ref, bo_ref, ymid_ref,
                   vx_ref, x0_ref, fb_ref, wb_ref, o_ref):
    j = pl.program_id(1)
    tt = ce_ref.shape[0]
    nb = ae_ref.shape[0]
    hw = fb_ref.shape[1]
    d = wb_ref.shape[1]
    row = lax.broadcasted_iota(jnp.int32, (tt, 1), 0) + j * tt
    sgn = (1 - 2 * (row % 2)).astype(F32)
    for s in range(nb):
        rows = slice(s * tt, (s + 1) * tt)
        ymid = ymid_ref[s]
        z_e = _dot(ce_ref[...], ae_ref[s]) - _dot(se_ref[...], be_ref[s]) + sgn * ymid[0:1]
        z_o = _dot(cot_ref[...], ao_ref[s]) - _dot(sot_ref[...], bo_ref[s]) - sgn * ymid[1:2]
        for par, z in ((0, z_e), (1, z_o)):
            cols = slice(par * hw, (par + 1) * hw)
            z = z + vx_ref[rows, cols].astype(F32) * fb_ref[...]
            yb = x0_ref[rows, cols].astype(F32) * z
            o_ref[rows, par * d:(par + 1) * d] = _dot(yb.astype(BF16), wb_ref[...]).astype(o_ref.dtype)


def _hy_inv(tabs, spec, vx2, x0c2, fbias, w_bb, n_seq, seq_len, row0_blk):
    m = seq_len // 2
    hw = fbias.shape[1]
    d = w_bb.shape[1]
    ce, se, _, _, cot, sot = tabs
    ae, be, ao, bo, ymid = spec
    tt = min(m, 128)
    per = m // tt
    nb = _seq_batch(seq_len, n_seq) if per == 1 else 1
    assert row0_blk % nb == 0
    mode = pl.Buffered(1) if nb * m * hw * 2 > BIG_BLOCK_BYTES // 2 else None
    tab = pl.BlockSpec((tt, m), lambda b, j: (j, 0))
    res = pl.BlockSpec((nb, m, hw), lambda b, j: (b, 0, 0), pipeline_mode=mode)
    rows_in = pl.BlockSpec((nb * tt, 2 * hw), lambda b, j: ((row0_blk // nb + b) * per + j, 0))
    return pl.pallas_call(
        _hy_inv_kernel,
        out_shape=jax.ShapeDtypeStruct((n_seq * m, 2 * d), BF16),
        grid=(n_seq // nb, per),
        in_specs=[tab, tab, tab, tab, res, res, res, res,
                  pl.BlockSpec((nb, 2, hw), lambda b, j: (b, 0, 0)),
                  rows_in, rows_in,
                  pl.BlockSpec((1, hw), lambda b, j: (0, 0)),
                  pl.BlockSpec((hw, d), lambda b, j: (0, 0), pipeline_mode=pl.Buffered(1))],
        out_specs=pl.BlockSpec((nb * tt, 2 * d), lambda b, j: (b * per + j, 0)),
        compiler_params=_cp(("arbitrary", "arbitrary")),
        name="hy_inv",
    )(ce, se, cot, sot, ae, be, ao, bo, ymid, vx2, x0c2, fbias, w_bb)


def _merge_kernel(tok, n_groups, xp_ref, xs_ref, pos_ref, g0_ref, g1_ref, pa_ref, pbc_ref, pbs_ref,
                  gate1_ref, sh2_ref, sc2_ref, gn_ref, wo_ref, wr_ref, br_ref,
                  x2_ref, h2_ref, rt_ref, scr):
    i = pl.program_id(0)
    is_ctx = i < tok.nct
    pb = _from_pair_rows(jnp.where(is_ctx, pbc_ref[...].astype(F32), pbs_ref[...].astype(F32)), scr)
    merged = g0_ref[...].astype(F32) * pa_ref[...].astype(F32) + g1_ref[...].astype(F32) * pb
    att = _dot(merged.astype(BF16), wo_ref[...])
    x = jnp.where(is_ctx, xp_ref[...], xs_ref[...] + pos_ref[...])
    x2 = x + gate1_ref[0] * att
    x2_ref[...] = x2
    h2 = _rms(x2, gn_ref[...]) * (1.0 + sc2_ref[0]) + sh2_ref[0]
    h2b = h2.astype(BF16)
    _store_row_tiles(h2_ref, 0, h2b)
    logits = _dot(h2b, wr_ref[...]) + br_ref[...]
    lane = lax.broadcasted_iota(jnp.int32, logits.shape, 1)
    gl = jnp.where(lane < n_groups, logits, -jnp.inf)
    gmax = jnp.max(gl, axis=-1, keepdims=True)
    gidx = jnp.min(jnp.where(gl == gmax, lane, n_groups), axis=-1, keepdims=True)
    rt_ref[...] = jnp.where(lane == LANES - 1, gidx.astype(F32), logits)


def _merge_out(xp, xs, pos, proj, gate_blk0, pa, pb_ctx, pb_lat, mod, g2, w_out, w_r, b_r, n_groups, tok):
    d = xp.shape[1]
    tm = tok.tm
    t_all = xp.shape[0] + xs.shape[0]
    kern = functools.partial(_merge_kernel, tok, n_groups)
    one = pl.Buffered(1)
    ns = d // 2 // LANES

    def modspec(k):
        return pl.BlockSpec((1, 1, d), lambda i: (tok.cond(i) * N_MOD + k, 0, 0))

    return pl.pallas_call(
        kern,
        out_shape=(jax.ShapeDtypeStruct((t_all, d), F32), jax.ShapeDtypeStruct((t_all * ns, LANES), jnp.uint32),
                   jax.ShapeDtypeStruct((t_all, LANES), F32)),
        grid=(tok.n,),
        in_specs=[
            pl.BlockSpec((tm, d), lambda i: (tok.ctx_idx(i), 0)),
            pl.BlockSpec((tm, d), lambda i: (tok.lat_idx(i), 0)),
            pl.BlockSpec((tm, d), lambda i: (tok.pos_idx(i), 0)),
            pl.BlockSpec((tm, d), lambda i: (i, gate_blk0)),
            pl.BlockSpec((tm, d), lambda i: (i, gate_blk0 + 1)),
            pl.BlockSpec((tm, d), lambda i: (i, 0)),
            pl.BlockSpec((tm // 2, 2 * d), lambda i: (tok.ctx_idx(i), 0)),
            pl.BlockSpec((tm // 2, 2 * d), lambda i: (tok.lat_idx(i), 0)),
            modspec(2), modspec(3), modspec(4),
            pl.BlockSpec((1, d), lambda i: (0, 0)),
            pl.BlockSpec((d, d), lambda i: (0, 0), pipeline_mode=one),
            pl.BlockSpec((d, LANES), lambda i: (0, 0)),
            pl.BlockSpec((1, LANES), lambda i: (0, 0)),
        ],
        out_specs=(pl.BlockSpec((tm, d), lambda i: (i, 0)), pl.BlockSpec((tm * ns, LANES), lambda i: (i, 0)),
                   pl.BlockSpec((tm, LANES), lambda i: (i, 0))),
        scratch_shapes=[pltpu.VMEM((d // LANES * tm, LANES), F32)],
        compiler_params=_cp(("parallel",)),
        name="merge_out",
    )(xp, xs, pos, proj, proj, pa, pb_ctx, pb_lat, mod, mod, mod, g2, w_out, w_r, b_r)


ROW_UNROLL = 8

def _moe_kernel(n_tiles, n_groups, epg, tg_ref, tnv_ref, rt_ref,
                h2_hbm, wr_ref, br_ref, wg_ref, wu_ref, wd_ref, out_hbm,
                xbuf, xb_ref, acc, obuf, gws, gsem, ssem):
    i = pl.program_id(0)
    j = pl.program_id(1)
    tm = xb_ref.shape[0]
    ns = h2_hbm.shape[1]
    nv = tnv_ref[i]
    slot = i % 2

    def buf_rows(buf, slot_, r, rows):
        return buf.at[pl.ds(pl.multiple_of((slot_ * tm + r) * ns, ns), rows * ns), :]

    def gather_copy(tok_row, slot_, r):
        return pltpu.make_async_copy(h2_hbm.at[tok_row], buf_rows(xbuf, slot_, r, 1), gsem.at[slot_])

    def gather_wait(slot_, rows):
        dst = buf_rows(xbuf, slot_, 0, rows)
        pltpu.make_async_copy(dst, dst, gsem.at[slot_]).wait()

    def scatter_copy(tok_row, slot_, r):
        return pltpu.make_async_copy(buf_rows(obuf, slot_, r, 1), out_hbm.at[tok_row], ssem.at[0])

    def scatter_wait(slot_, rows):
        src = buf_rows(obuf, slot_, 0, rows)
        pltpu.make_async_copy(src, src, ssem.at[0]).wait()

    def for_rows(count, fn):
        n_full = count // ROW_UNROLL

        def body(b, c):
            for k in range(ROW_UNROLL):
                fn(b * ROW_UNROLL + k)
            return c

        def tail(r, c):
            fn(r)
            return c

        lax.fori_loop(0, n_full, body, 0)
        lax.fori_loop(n_full * ROW_UNROLL, count, tail, 0)

    def wait_scatter(count, slot_):
        p = tm
        while p >= 1:
            @pl.when((count & p) != 0)
            def _(p=p):
                scatter_wait(slot_, p)
            p //= 2

    nv_prev = tnv_ref[jnp.maximum(i - 1, 0)]
    nv_next = tnv_ref[jnp.minimum(i + 1, n_tiles - 1)]
    prev_deferred = jnp.logical_and(i > 0, jnp.logical_and(nv_prev == tm, nv > 0))
    deferred = jnp.logical_and(nv == tm, nv_next > 0)

    @pl.when(j == 0)
    def _():
        @pl.when(i == 0)
        def _():
            for_rows(tm, lambda r: gather_copy(rt_ref[r], 0, r).start())

        @pl.when(jnp.logical_or(i == 0, nv_prev > 0))
        def _():
            gather_wait(slot, tm)

    @pl.when(jnp.logical_and(j == 0, nv > 0))
    def _():
        grp = tg_ref[i]
        xb = _load_row_tiles(xbuf, slot * tm * ns, tm, ns)
        xb_ref[...] = xb
        logits = _dot(xb, wr_ref[...]) + br_ref[...]
        lane = lax.broadcasted_iota(jnp.int32, logits.shape, 1)
        gl = jnp.where(lane < n_groups, logits, -jnp.inf)
        gmax = jnp.max(gl, axis=-1, keepdims=True)
        ge = jnp.exp(gl - gmax)
        g_w = jnp.sum(jnp.where(lane == grp, ge, 0.0), axis=-1, keepdims=True) / jnp.sum(ge, axis=-1, keepdims=True)
        base = n_groups + grp * epg
        e = [jnp.sum(jnp.where(lane == base + k, logits, 0.0), axis=-1, keepdims=True) for k in range(epg)]
        sel = []
        for k in range(epg):
            rank = jnp.zeros_like(e[k])
            for m in range(epg):
                if m == k:
                    continue
                ahead = (e[m] > e[k]) if m > k else (e[m] >= e[k])
                rank = rank + ahead.astype(F32)
            sel.append(rank < TOP_K_INNER)
        emax = functools.reduce(jnp.maximum, e)
        ex = [jnp.where(sel[k], jnp.exp(e[k] - emax), 0.0) for k in range(epg)]
        den = functools.reduce(lambda a, b: a + b, ex)
        for k in range(epg):
            gws[k] = ex[k] / den * g_w
        acc[...] = jnp.zeros(acc.shape, acc.dtype)

    def expert_step(scatter_prev):
        per = tm // epg
        for k in range(per):
            r = j * per + k
            if scatter_prev:
                scatter_copy(rt_ref[(i - 1) * tm + r], 1 - slot, r).start()
            gather_copy(rt_ref[(i + 1) * tm + r], 1 - slot, r).start()
        xb = xb_ref[...]
        g = _dot(xb, wg_ref[...])
        u = _dot(xb, wu_ref[...])
        hid = (g * jax.nn.sigmoid(g)) * u * gws[j]
        acc[...] += _dot(hid.astype(BF16), wd_ref[...])

    @pl.when(jnp.logical_and(nv > 0, prev_deferred))
    def _():
        expert_step(True)

    @pl.when(jnp.logical_and(nv > 0, jnp.logical_not(prev_deferred)))
    def _():
        expert_step(False)

    @pl.when(j == epg - 1)
    def _():
        @pl.when(prev_deferred)
        def _():
            scatter_wait(1 - slot, tm)

        @pl.when(nv > 0)
        def _():
            _store_row_tiles(obuf, slot * tm * ns, acc[...].astype(BF16))

        @pl.when(jnp.logical_and(nv > 0, jnp.logical_not(deferred)))
        def _():
            for_rows(nv, lambda r: scatter_copy(rt_ref[i * tm + r], slot, r).start())
            wait_scatter(nv, slot)


def _moe(h2p, w_r, b_r, w_g, w_u, w_d, tile_grp, tile_nv, row_tok, n_groups, epg, tm):
    t, ns, _ = h2p.shape
    d = 2 * ns * LANES
    ff = w_g.shape[-1]
    n_tiles = tile_grp.shape[0]
    assert tm & (tm - 1) == 0, "row-count waits decompose tm in binary"
    kern = functools.partial(_moe_kernel, n_tiles, n_groups, epg)

    def widx(i, j, tg, tnv, rt):
        return tg[i] * epg + jnp.where(tnv[i] > 0, j, epg - 1)

    grid_spec = pltpu.PrefetchScalarGridSpec(
        num_scalar_prefetch=3,
        grid=(n_tiles, epg),
        in_specs=[
            pl.BlockSpec(memory_space=pl.ANY),
            pl.BlockSpec((d, LANES), lambda i, j, tg, tnv, rt: (0, 0)),
            pl.BlockSpec((1, LANES), lambda i, j, tg, tnv, rt: (0, 0)),
            pl.BlockSpec((None, d, ff), lambda i, j, tg, tnv, rt: (widx(i, j, tg, tnv, rt), 0, 0)),
            pl.BlockSpec((None, d, ff), lambda i, j, tg, tnv, rt: (widx(i, j, tg, tnv, rt), 0, 0)),
            pl.BlockSpec((None, ff, d), lambda i, j, tg, tnv, rt: (widx(i, j, tg, tnv, rt), 0, 0)),
        ],
        out_specs=pl.BlockSpec(memory_space=pl.ANY),
        scratch_shapes=[
            pltpu.VMEM((2 * tm * ns, LANES), jnp.uint32),
            pltpu.VMEM((tm, d), BF16),
            pltpu.VMEM((tm, d), F32),
            pltpu.VMEM((2 * tm * ns, LANES), jnp.uint32),
            pltpu.VMEM((epg, tm, 1), F32),
            pltpu.SemaphoreType.DMA((2,)),
            pltpu.SemaphoreType.DMA((1,)),
        ],
    )
    return pl.pallas_call(
        kern,
        out_shape=jax.ShapeDtypeStruct((t, ns, LANES), jnp.uint32),
        grid_spec=grid_spec,
        compiler_params=_cp(("arbitrary", "arbitrary")),
        name="moe",
    )(tile_grp, tile_nv, row_tok, h2p, w_r, b_r, w_g, w_u, w_d)


def _moe_plan(grp, n_groups, tm, n_tiles):
    t = grp.shape[0]
    onehot = (grp[:, None] == jnp.arange(n_groups, dtype=jnp.int32)[None, :]).astype(jnp.int32)
    csum = jnp.cumsum(onehot, axis=0)
    counts = csum[-1]
    rank = jnp.take_along_axis(csum, grp[:, None], axis=1)[:, 0] - 1
    ntile_g = (counts + tm - 1) // tm
    tile_end = jnp.cumsum(ntile_g)
    tile_off = tile_end - ntile_g
    slot = tile_off[grp] * tm + rank
    row_tok = jnp.zeros((n_tiles * tm,), jnp.int32).at[slot].set(jnp.arange(t, dtype=jnp.int32))
    tiles = jnp.arange(n_tiles, dtype=jnp.int32)
    tile_grp = jnp.minimum(jnp.sum((tiles[:, None] >= tile_end[None, :]).astype(jnp.int32), axis=1),
                           n_groups - 1)
    tile_nv = jnp.clip(counts[tile_grp] - (tiles - tile_off[tile_grp]) * tm, 0, tm)
    tile_nv = jnp.where(tiles < tile_end[-1], tile_nv, 0)
    return tile_grp.astype(jnp.int32), tile_nv.astype(jnp.int32), row_tok


def _final_kernel(x2_ref, m_ref, gate2_ref, gf_ref, o_ref):
    tm, d = x2_ref.shape
    ns = d // 2 // LANES
    m = _load_row_tiles(m_ref, 0, tm, ns).astype(F32)
    x3 = x2_ref[...] + gate2_ref[0] * m
    o_ref[...] = _rms(x3, gf_ref[...])


def _final(x2, moe, mod, g_final, row_blk0, n_rows, cond_of_tile, tm):
    d = x2.shape[1]
    ns = d // 2 // LANES
    return pl.pallas_call(
        _final_kernel,
        out_shape=jax.ShapeDtypeStruct((n_rows, d), F32),
        grid=(n_rows // tm,),
        in_specs=[pl.BlockSpec((tm, d), lambda i: (row_blk0 + i, 0)),
                  pl.BlockSpec((tm * ns, LANES), lambda i: (row_blk0 + i, 0)),
                  pl.BlockSpec((1, 1, d), lambda i: (cond_of_tile(i) * N_MOD + 5, 0, 0)),
                  pl.BlockSpec((1, d), lambda i: (0, 0))],
        out_specs=pl.BlockSpec((tm, d), lambda i: (i, 0)),
        compiler_params=_cp(("parallel",)),
        name="final",
    )(x2, moe, mod, g_final)


def _grid_pos_embed(n_tokens, dim):
    rows_n = n_tokens // GRID_W
    quarter = dim // 4
    omega = 1.0 / (10000.0 ** (jnp.arange(quarter, dtype=F32) / quarter))

    def emb(n):
        ang = jnp.arange(n).astype(F32)[:, None] * omega[None, :]
        return jnp.concatenate([jnp.sin(ang), jnp.cos(ang)], axis=-1)

    e_r = jnp.broadcast_to(emb(rows_n)[:, None, :], (rows_n, GRID_W, dim // 2))
    e_c = jnp.broadcast_to(emb(GRID_W)[None, :, :], (rows_n, GRID_W, dim // 2))
    return jnp.concatenate([e_r, e_c], axis=-1).reshape(n_tokens, dim)


def _state_to_lanes(s):
    b, _, _, g, p = s.shape
    return jnp.transpose(s, (0, 3, 2, 1, 4)).reshape(b, g, 4 * p)


def _lanes_to_state(x, p):
    b, g, _ = x.shape
    return jnp.transpose(x.reshape(b, g, 2, 2, p), (0, 3, 2, 1, 4))


def kernel(x_prompt, x_sample, state_ssm, c, c_ctx, w_ada, b_ada, g_norm1, g_norm2, w_in, ssm_a_re, ssm_a_im, ssm_log_dt, ssm_b_re, ssm_b_im, ssm_c_re, ssm_c_im, ssm_d, w_glu, b_glu, w_short, b_short, w_f1, b_f1, freq1, w_f2, b_f2, freq2, w_f3, b_f3, filter_bias, w_branch_a, w_branch_b, w_out, w_router_group, b_router_group, w_router_expert, b_router_expert, w_exp_gate, w_exp_up, w_exp_down, g_final):
    depth = w_ada.shape[0]
    assert depth == 1, "single-layer pipeline"
    bsz, seq, d = x_prompt.shape
    dbsz, dseq, _ = x_sample.shape
    g_all, p_state = ssm_a_re.shape[2:]
    ssm_w = ssm_d.shape[1]
    hw = filter_bias.shape[1]
    n_groups, _, epg = w_router_expert.shape[1:]
    ff = w_exp_gate.shape[-1]
    assert ssm_w // g_all * CHUNK == MXU_SIDE and 4 * p_state == MXU_SIDE
    assert ssm_w == hw and d == 2 * hw
    n_ctx = bsz * seq
    n_lat = dbsz * dseq
    t_all = n_ctx + n_lat

    xp = x_prompt.reshape(n_ctx, d)
    xs = x_sample.reshape(n_lat, d)
    pos = _grid_pos_embed(dseq, d)

    n_cond = 1 + dbsz
    cond8 = jnp.zeros((SUBLANES, d), F32).at[0].set(c_ctx).at[1:n_cond].set(c)
    mod = _ada_mod(cond8, w_ada[0], b_ada[0][None])
    mod = mod[:n_cond].reshape(n_cond * N_MOD, 1, d)

    tm_e = min(256, seq, dseq)
    tok_e = _Tok(n_ctx, n_lat, dseq, tm_e)
    tm_l = min(512, n_ctx, dseq)
    tok_l = _Tok(n_ctx, n_lat, dseq, tm_l)
    gate_col0 = ssm_w + 3 * hw
    h1 = _norm_mod(xp, xs, pos, mod, g_norm1, tok_l)
    tm_p, tn_p = min(1024, t_all), min(1024, hw)
    n_in = w_in.shape[2]
    u_a = _in_proj(h1, w_in[0], (0, 0), (0, ssm_w), F32, tm_p, tn_p, "in_proj_a")
    proj = _in_proj(h1, w_in[0], (gate_col0, n_in), (ssm_w, gate_col0), BF16, tm_p, tn_p, "in_proj")
    n_gate_cols = n_in - gate_col0

    w_state, w_so, a_step = _ssm_operators(ssm_a_re[0], ssm_a_im[0], ssm_log_dt[0], ssm_b_re[0], ssm_b_im[0],
                                           ssm_c_re[0], ssm_c_im[0], ssm_d[0])
    nc_ctx = seq // CHUNK
    nc_lat = dseq // CHUNK
    seq_per_blk = max(1, min(bsz, MXU_SIDE // nc_ctx))
    assert bsz % seq_per_blk == 0 and n_ctx % dseq == 0
    s0_ctx = jnp.zeros((bsz, g_all, 4 * p_state), F32)
    y_ctx, fin_ctx = _ssm_call(u_a, 0, n_ctx, w_state, w_so, a_step, s0_ctx, seq_per_blk, nc_ctx, "ssm_ctx")
    s0_lat = _state_to_lanes(state_ssm[:, 0].astype(F32))
    y_lat, _ = _ssm_call(u_a, n_ctx // dseq, n_lat, w_state, w_so, a_step, s0_lat, 1, nc_lat, "ssm_lat")
    new_state = _lanes_to_state(fin_ctx, p_state)[:, None]

    pa = _glu_branch(y_ctx, y_lat, w_glu[0].astype(BF16), b_glu[0][None], w_branch_a[0].astype(BF16), tok_l)

    w_short3 = jnp.transpose(w_short[0].reshape(-1, 3, hw), (1, 0, 2))
    b_short3 = b_short[0].reshape(3, 1, hw)
    vx2, x0c2 = _short_conv(proj, n_gate_cols // hw, hw, w_short3, b_short3, n_ctx, seq, dseq, tm_e)
    rates = jnp.abs(jnp.linspace(math.log(DECAY_TARGET) / DECAY_FAST, math.log(DECAY_TARGET) / DECAY_SLOW,
                                 hw, dtype=F32))
    rates2 = jnp.concatenate([rates, rates])[None]
    fbias = filter_bias[0][None].astype(F32)
    w_bb = w_branch_b[0].astype(BF16)
    pbs = []
    for seq_len, n_seq, row0 in ((seq, bsz, 0), (dseq, dbsz, n_ctx // dseq)):
        tabs = _dft_tables(seq_len)
        a2, bm2, colabs = _filt_gen(seq_len, w_f1[0], b_f1[0][None], freq1[0][None], w_f2[0],
                                    b_f2[0][None], freq2[0][None], w_f3[0], b_f3[0][None], rates2)
        kspec = _filt_dft(tabs, a2, bm2, colabs, seq_len)
        spec = _hy_fwd(tabs, vx2, kspec, n_seq, seq_len, row0)
        pbs.append(_hy_inv(tabs, spec, vx2, x0c2, fbias, w_bb, n_seq, seq_len, row0))
    pb_ctx, pb_lat = pbs

    w_r = jnp.zeros((d, LANES), F32)
    w_r = w_r.at[:, :n_groups].set(w_router_group[0])
    w_r = w_r.at[:, n_groups:n_groups + n_groups * epg].set(
        jnp.transpose(w_router_expert[0], (1, 0, 2)).reshape(d, n_groups * epg))
    b_r = jnp.zeros((1, LANES), F32)
    b_r = b_r.at[0, :n_groups].set(b_router_group[0])
    b_r = b_r.at[0, n_groups:n_groups + n_groups * epg].set(b_router_expert[0].reshape(-1))
    w_r = w_r.astype(BF16)
    x2, h2, rout = _merge_out(xp, xs, pos, proj, 0, pa, pb_ctx, pb_lat, mod, g_norm2,
                              w_out[0].astype(BF16), w_r, b_r, n_groups, tok_e)

    tm_m = min(512, t_all // n_groups)
    n_tiles = t_all // tm_m + n_groups
    grp = rout[:, LANES - 1].astype(jnp.int32)
    tile_grp, tile_nv, row_tok = _moe_plan(grp, n_groups, tm_m, n_tiles)
    ns = d // 2 // LANES
    moe = _moe(h2.reshape(t_all, ns, LANES), w_r, b_r, w_exp_gate[0].astype(BF16), w_exp_up[0].astype(BF16),
               w_exp_down[0].astype(BF16), tile_grp, tile_nv, row_tok, n_groups, epg, tm_m)
    moe = moe.reshape(t_all * ns, LANES)

    gf = g_final[None]
    y_prompt = _final(x2, moe, mod, gf, 0, n_ctx, lambda i: 0, tm_l)
    lat_tiles = dseq // tm_l
    y_sample = _final(x2, moe, mod, gf, n_ctx // tm_l, n_lat, lambda i: 1 + i // lat_tiles, tm_l)
    return (y_prompt.reshape(bsz, seq, d), y_sample.reshape(dbsz, dseq, d), new_state)
```

```python
import functools
import math

import numpy as np
import jax
import jax.numpy as jnp
from jax import lax
from jax.experimental import pallas as pl
from jax.experimental.pallas import tpu as pltpu

F32 = jnp.float32
BF16 = jnp.bfloat16
EPS = 1e-6
GRID_W = 64
N_MOD = 6
TOP_K_INNER = 2
DECAY_TARGET = 1e-2
DECAY_FAST = 0.3
DECAY_SLOW = 1.5
LANES = 128
SUBLANES = 8
MXU_SIDE = 256
VMEM_LIMIT = 56 * 1024 * 1024


def _cp(sem, vmem=VMEM_LIMIT):
    return pltpu.CompilerParams(dimension_semantics=sem, vmem_limit_bytes=vmem)


def _dot(a, b):
    return jnp.dot(a, b, preferred_element_type=F32)


def _split(a):
    hi = a.astype(BF16)
    lo = (a - hi.astype(F32)).astype(BF16)
    return hi, lo


def _dot3(a, b):
    a_hi, a_lo = _split(a)
    b_hi, b_lo = _split(b)
    return _dot(a_hi, b_hi) + _dot(a_lo, b_hi) + _dot(a_hi, b_lo)


def _rms(x, g):
    ms = jnp.mean(x * x, axis=-1, keepdims=True)
    return x * lax.rsqrt(ms + EPS) * g


def _pack_halves(xb):
    n = xb.shape[1] // 2
    lo = lax.bitcast_convert_type(xb[:, :n].astype(F32), jnp.uint32)
    hi = lax.bitcast_convert_type(xb[:, n:].astype(F32), jnp.uint32)
    return (lo >> 16) | (hi & jnp.uint32(0xFFFF0000))


def _unpack_halves(w):
    lo = lax.bitcast_convert_type(w << 16, F32)
    hi = lax.bitcast_convert_type(w & jnp.uint32(0xFFFF0000), F32)
    return jnp.concatenate([lo, hi], axis=1).astype(BF16)


def _to_pair_rows(x, scr):
    m, n = x.shape
    ev, od = [], []
    for c in range(n // LANES):
        scr[c * m:(c + 1) * m, :] = x[:, c * LANES:(c + 1) * LANES]
        ev.append(scr[pl.ds(c * m, m // 2, stride=2), :])
        od.append(scr[pl.ds(c * m + 1, m // 2, stride=2), :])
    return jnp.concatenate(ev + od, axis=1)


def _from_pair_rows(x2, scr):
    h, n2 = x2.shape
    n = n2 // 2
    out = []
    for c in range(n // LANES):
        scr[pl.ds(2 * c * h, h, stride=2), :] = x2[:, c * LANES:(c + 1) * LANES]
        scr[pl.ds(2 * c * h + 1, h, stride=2), :] = x2[:, n + c * LANES:n + (c + 1) * LANES]
        out.append(scr[2 * c * h:2 * (c + 1) * h, :])
    return jnp.concatenate(out, axis=1)


def _store_row_tiles(ref, row0, xb):
    w = _pack_halves(xb)
    m, n = w.shape
    ns = n // LANES
    for c in range(ns):
        ref[pl.ds(row0 + c, m, stride=ns), :] = w[:, c * LANES:(c + 1) * LANES]


def _load_row_tiles(ref, row0, m, ns):
    w = jnp.concatenate([ref[pl.ds(row0 + c, m, stride=ns), :] for c in range(ns)], axis=1)
    return _unpack_halves(w)


def _ada_kernel(c_ref, w_ref, b_ref, o_ref):
    c = c_ref[...]
    a = c * jax.nn.sigmoid(c)
    o_ref[...] = _dot3(a, w_ref[...]) + b_ref[...]


def _ada_mod(cond8, w_ada, b_ada):
    d, n = w_ada.shape
    tn = min(n, 1024)
    return pl.pallas_call(
        _ada_kernel,
        out_shape=jax.ShapeDtypeStruct((SUBLANES, n), F32),
        grid=(n // tn,),
        in_specs=[pl.BlockSpec((SUBLANES, d), lambda j: (0, 0)),
                  pl.BlockSpec((d, tn), lambda j: (0, j)),
                  pl.BlockSpec((1, tn), lambda j: (0, j))],
        out_specs=pl.BlockSpec((SUBLANES, tn), lambda j: (0, j)),
        compiler_params=_cp(("arbitrary",)),
        name="ada_mod",
    )(cond8, w_ada, b_ada)


class _Tok:
    def __init__(self, n_ctx, n_lat, lat_len, tm):
        assert n_ctx % tm == 0 and lat_len % tm == 0 and n_lat % lat_len == 0
        self.tm = tm
        self.nct = n_ctx // tm
        self.nst = n_lat // tm
        self.per_seq = lat_len // tm
        self.n = self.nct + self.nst

    def ctx_idx(self, i):
        return jnp.minimum(i, self.nct - 1)

    def lat_idx(self, i):
        return jnp.maximum(i - self.nct, 0)

    def pos_idx(self, i):
        return self.lat_idx(i) % self.per_seq

    def cond(self, i):
        return jnp.where(i < self.nct, 0, 1 + self.lat_idx(i) // self.per_seq)


def _norm_mod_kernel(tok, xp_ref, xs_ref, pos_ref, sh_ref, sc_ref, g_ref, h_ref):
    i = pl.program_id(0)

    def modulate(x):
        y = _rms(x, g_ref[...])
        return (y * (1.0 + sc_ref[0]) + sh_ref[0]).astype(h_ref.dtype)

    @pl.when(i < tok.nct)
    def _():
        h_ref[...] = modulate(xp_ref[...])

    @pl.when(i >= tok.nct)
    def _():
        h_ref[...] = modulate(xs_ref[...] + pos_ref[...])


def _norm_mod(xp, xs, pos, mod, g1, tok):
    d = xp.shape[1]
    tm = tok.tm
    t_all = xp.shape[0] + xs.shape[0]
    return pl.pallas_call(
        functools.partial(_norm_mod_kernel, tok),
        out_shape=jax.ShapeDtypeStruct((t_all, d), BF16),
        grid=(tok.n,),
        in_specs=[
            pl.BlockSpec((tm, d), lambda i: (tok.ctx_idx(i), 0)),
            pl.BlockSpec((tm, d), lambda i: (tok.lat_idx(i), 0)),
            pl.BlockSpec((tm, d), lambda i: (tok.pos_idx(i), 0)),
            pl.BlockSpec((1, 1, d), lambda i: (tok.cond(i) * N_MOD + 0, 0, 0)),
            pl.BlockSpec((1, 1, d), lambda i: (tok.cond(i) * N_MOD + 1, 0, 0)),
            pl.BlockSpec((1, d), lambda i: (0, 0)),
        ],
        out_specs=pl.BlockSpec((tm, d), lambda i: (i, 0)),
        compiler_params=_cp(("parallel",)),
        name="norm_mod",
    )(xp, xs, pos, mod, mod, g1)


def _in_proj_kernel(n_sig, h_ref, w_ref, o_ref, wb_ref):
    n = pl.program_id(0)
    m = pl.program_id(1)

    @pl.when(m == 0)
    def _():
        wb_ref[...] = w_ref[...].astype(BF16)

    acc = _dot(h_ref[...], wb_ref[...])

    @pl.when(n >= n_sig)
    def _():
        o_ref[...] = acc.astype(o_ref.dtype)

    @pl.when(n < n_sig)
    def _():
        o_ref[...] = jax.nn.sigmoid(acc).astype(o_ref.dtype)


def _in_proj(h, w_in, sig_cols, lin_cols, out_dtype, tm, tn, name):
    t_all, d = h.shape
    for c in sig_cols + lin_cols:
        assert c % tn == 0
    assert t_all % tm == 0
    n_sig = (sig_cols[1] - sig_cols[0]) // tn
    n_lin = (lin_cols[1] - lin_cols[0]) // tn
    sig0, lin0 = sig_cols[0] // tn, lin_cols[0] // tn

    def w_blk(n):
        return jnp.where(n < n_sig, sig0 + n, lin0 + n - n_sig)

    return pl.pallas_call(
        functools.partial(_in_proj_kernel, n_sig),
        out_shape=jax.ShapeDtypeStruct((t_all, (n_sig + n_lin) * tn), out_dtype),
        grid=(n_sig + n_lin, t_all // tm),
        in_specs=[pl.BlockSpec((tm, d), lambda n, m: (m, 0)),
                  pl.BlockSpec((d, tn), lambda n, m: (0, w_blk(n)))],
        out_specs=pl.BlockSpec((tm, tn), lambda n, m: (m, n)),
        scratch_shapes=[pltpu.VMEM((d, tn), BF16)],
        compiler_params=_cp(("arbitrary", "arbitrary")),
        name=name,
    )(h, w_in)


CHUNK = 16
GROUP_PAD = 8


def _ssm_operators(a_re, a_im, log_dt, b_re, b_im, c_re, c_im, d_skip):
    t = CHUNK
    g, p = a_re.shape[1:]
    c = b_re.shape[-1]
    lam = lax.complex(a_re.astype(F32), a_im.astype(F32))
    dt = jnp.exp(log_dt.astype(F32))[..., None]
    ld = lam * dt
    lam_bar = jnp.exp(ld)
    b_bar = ((lam_bar - 1.0) / lam)[..., None] * lax.complex(b_re.astype(F32), b_im.astype(F32))
    c_mat = lax.complex(c_re.astype(F32), c_im.astype(F32))
    taus = jnp.arange(t + 1, dtype=F32)
    e_pow = jnp.exp(taus[None, None, :, None] * ld[:, :, None, :])

    k_all = jnp.real(jnp.einsum('dgop,dgtp,dgpi->dgtoi', c_mat, e_pow[:, :, :t], b_bar))
    kf, kb = k_all[0], k_all[1]
    ii = np.arange(t)[None, :, None]
    jj = np.arange(t)[None, None, :]
    tt = np.arange(t)[:, None, None]
    place_f = jnp.asarray((jj - ii == tt).astype(np.float32))
    place_b = jnp.asarray((ii - jj == tt).astype(np.float32))
    kf = kf.at[:, 0].add(jnp.eye(c, dtype=F32)[None] * d_skip.astype(F32).reshape(g, c, 1))
    m = jnp.einsum('tij,gtoc->gicjo', place_f, kf) + jnp.einsum('tij,gtoc->gicjo', place_b, kb)
    m = m.reshape(g, t * c, t * c)

    e_f = e_pow[0][:, ::-1][:, 1:]
    e_b = e_pow[1][:, :t]
    ws_f = e_f[:, :, :, None] * b_bar[0][:, None]
    ws_b = e_b[:, :, :, None] * b_bar[1][:, None]

    def rows_ic(x):
        return jnp.transpose(x, (0, 1, 3, 2)).reshape(g, t * c, p)

    w_state = jnp.concatenate([rows_ic(jnp.real(ws_f)), rows_ic(jnp.real(ws_b)),
                               rows_ic(jnp.imag(ws_f)), rows_ic(jnp.imag(ws_b))], axis=-1)

    ce_f = c_mat[0][:, None] * e_pow[0][:, 1:, None, :]
    ce_b = c_mat[1][:, None] * e_pow[1][:, ::-1][:, :t, None, :]

    def cols_jc(x):
        return jnp.transpose(x, (0, 3, 1, 2)).reshape(g, p, t * c)

    w_so = jnp.concatenate([cols_jc(jnp.real(ce_f)), cols_jc(jnp.real(ce_b)),
                            cols_jc(-jnp.imag(ce_f)), cols_jc(-jnp.imag(ce_b))], axis=1)
    w_out = jnp.concatenate([m, w_so], axis=1)

    a_t = e_pow[:, :, t]
    a_step = jnp.stack([jnp.concatenate([jnp.real(a_t[0]), jnp.real(a_t[1])], axis=-1),
                        jnp.concatenate([jnp.imag(a_t[0]), jnp.imag(a_t[1])], axis=-1)])
    return w_state.astype(BF16), w_out.astype(BF16), a_step


def _ssm_kernel(n_seq, n_chunk, gb, g_all, x_ref, ws_ref, wo_ref, a_ref, s0_ref,
                y_ref, fin_ref, xq_ref, za_ref, zb_ref):
    phase = pl.program_id(1)
    j = pl.program_id(2)
    gp = g_all + GROUP_PAD
    nck = n_seq * n_chunk
    half = LANES // 2
    c = LANES // gb

    @pl.when(phase == 0)
    def _():
        xs = [x_ref[pl.ds(i, nck, stride=CHUNK), :] for i in range(CHUNK)]
        for q in range(gb):
            xq = jnp.concatenate([xs[i][:, q * c:(q + 1) * c] for i in range(CHUNK)], axis=-1).astype(BF16)
            xq_ref[j * gb + q] = xq
            z = _dot(xq, ws_ref[q])
            za_ref[pl.ds(j * gb + q, nck, stride=gp), :] = z[:, :LANES]
            zb_ref[pl.ds(j * gb + q, nck, stride=gp), :] = z[:, LANES:]

    @pl.when(jnp.logical_and(phase == 1, j == 0))
    def _():
        a_re = a_ref[0]
        a_im = a_ref[1]
        lane = lax.broadcasted_iota(jnp.int32, (g_all, LANES), 1)
        is_f = lane < half
        for s in range(n_seq):
            def step(k, carry):
                s_a, s_b = carry
                rf = pl.multiple_of((s * n_chunk + k) * gp, SUBLANES)
                rb = pl.multiple_of((s * n_chunk + n_chunk - 1 - k) * gp, SUBLANES)
                zf_a = za_ref[pl.ds(rf, g_all), :]
                zf_b = zb_ref[pl.ds(rf, g_all), :]
                zb_a = za_ref[pl.ds(rb, g_all), :]
                zb_b = zb_ref[pl.ds(rb, g_all), :]
                za_ref[pl.ds(rf, g_all), :] = jnp.where(is_f, s_a, zf_a)
                zb_ref[pl.ds(rf, g_all), :] = jnp.where(is_f, s_b, zf_b)
                za_ref[pl.ds(rb, g_all), :] = jnp.where(is_f, zb_a, s_a)
                zb_ref[pl.ds(rb, g_all), :] = jnp.where(is_f, zb_b, s_b)
                z_a = jnp.where(is_f, zf_a, zb_a)
                z_b = jnp.where(is_f, zf_b, zb_b)
                n_a = a_re * s_a - a_im * s_b + z_a
                n_b = a_re * s_b + a_im * s_a + z_b
                return n_a, n_b

            init = s0_ref[s]
            f_a, f_b = lax.fori_loop(0, n_chunk, step, (init[:, :LANES], init[:, LANES:]))
            fin_ref[s] = jnp.concatenate([f_a, f_b], axis=-1)

    @pl.when(phase == 1)
    def _():
        ys = []
        for q in range(gb):
            st_a = za_ref[pl.ds(j * gb + q, nck, stride=gp), :].astype(BF16)
            st_b = zb_ref[pl.ds(j * gb + q, nck, stride=gp), :].astype(BF16)
            lhs = jnp.concatenate([xq_ref[j * gb + q], st_a, st_b], axis=-1)
            ys.append(_dot(lhs, wo_ref[q]))
        for i in range(CHUNK):
            yi = jnp.concatenate([ys[q][:, i * c:(i + 1) * c] for q in range(gb)], axis=-1)
            y_ref[pl.ds(i, nck, stride=CHUNK), :] = yi


def _ssm_call(u, row_blk0, n_rows, w_state, w_out, a_step, s0, n_seq, n_chunk, name):
    g_all, w, _ = w_state.shape
    nck = n_seq * n_chunk
    blk = nck * CHUNK
    n_blk = n_rows // blk
    gb = LANES * CHUNK // w
    ngb = g_all // gb
    gp = g_all + GROUP_PAD
    kern = functools.partial(_ssm_kernel, n_seq, n_chunk, gb, g_all)
    return pl.pallas_call(
        kern,
        out_shape=(jax.ShapeDtypeStruct((n_rows, u.shape[1]), F32),
                   jax.ShapeDtypeStruct((n_blk * n_seq, g_all, w), F32)),
        grid=(n_blk, 2, ngb),
        in_specs=[
            pl.BlockSpec((blk, LANES), lambda b, ph, j: (row_blk0 + b, jnp.where(ph == 0, j, ngb - 1))),
            pl.BlockSpec((gb, w, w), lambda b, ph, j: (jnp.where(ph == 0, j, ngb - 1), 0, 0)),
            pl.BlockSpec((gb, 2 * w, w), lambda b, ph, j: (jnp.where(ph == 0, 0, j), 0, 0)),
            pl.BlockSpec((2, g_all, LANES), lambda b, ph, j: (0, 0, 0)),
            pl.BlockSpec((n_seq, g_all, w), lambda b, ph, j: (b, 0, 0)),
        ],
        out_specs=(
            pl.BlockSpec((blk, LANES), lambda b, ph, j: (b, jnp.where(ph == 0, 0, j))),
            pl.BlockSpec((n_seq, g_all, w), lambda b, ph, j: (b, 0, 0)),
        ),
        scratch_shapes=[pltpu.VMEM((g_all, nck, w), BF16),
                        pltpu.VMEM((nck * gp, LANES), F32), pltpu.VMEM((nck * gp, LANES), F32)],
        compiler_params=_cp(("arbitrary", "arbitrary", "arbitrary")),
        name=name,
    )(u, w_state, w_out, a_step, s0)


def _glu_kernel(nct, yc_ref, yl_ref, wg_ref, bg_ref, wb_ref, o_ref):
    i = pl.program_id(0)
    y = jax.nn.gelu(jnp.where(i < nct, yc_ref[...], yl_ref[...]))
    z = _dot(y.astype(BF16), wg_ref[...]) + bg_ref[...]
    ya = y * jax.nn.sigmoid(z)
    o_ref[...] = _dot(ya.astype(BF16), wb_ref[...]).astype(o_ref.dtype)


def _glu_branch(y_ctx, y_lat, w_glu, b_glu, w_ba, tok):
    c = y_ctx.shape[1]
    d = w_ba.shape[1]
    tm = tok.tm
    return pl.pallas_call(
        functools.partial(_glu_kernel, tok.nct),
        out_shape=jax.ShapeDtypeStruct((y_ctx.shape[0] + y_lat.shape[0], d), BF16),
        grid=(tok.n,),
        in_specs=[pl.BlockSpec((tm, c), lambda i: (tok.ctx_idx(i), 0)),
                  pl.BlockSpec((tm, c), lambda i: (tok.lat_idx(i), 0)),
                  pl.BlockSpec((c, c), lambda i: (0, 0)),
                  pl.BlockSpec((1, c), lambda i: (0, 0)),
                  pl.BlockSpec((c, d), lambda i: (0, 0))],
        out_specs=pl.BlockSpec((tm, d), lambda i: (i, 0)),
        compiler_params=_cp(("parallel",)),
        name="glu_branch",
    )(y_ctx, y_lat, w_glu, b_glu, w_ba)


def _short_conv_kernel(nct, per_ctx, per_lat, v_ref, x0_ref, x1_ref, vp_ref, x0p_ref, x1p_ref,
                       vn_ref, x0n_ref, x1n_ref, w_ref, b_ref, vx_ref, x0c_ref, scr):
    i = pl.program_id(0)
    tm = v_ref.shape[0]
    k = jnp.where(i < nct, i % per_ctx, (i - nct) % per_lat)
    per = jnp.where(i < nct, per_ctx, per_lat)
    first = k == 0
    last = k == per - 1
    row = lax.broadcasted_iota(jnp.int32, (tm, 1), 0)

    def conv(cur_ref, prev_ref, next_ref, part):
        x = cur_ref[...].astype(F32)
        hp = jnp.where(first, 0.0, prev_ref[SUBLANES - 1:SUBLANES, :].astype(F32))
        hn = jnp.where(last, 0.0, next_ref[0:1, :].astype(F32))
        xm = jnp.where(row == 0, hp, pltpu.roll(x, 1, axis=0))
        xq = jnp.where(row == tm - 1, hn, pltpu.roll(x, tm - 1, axis=0))
        w = w_ref[part]
        return b_ref[part] + xm * w[0:1, :] + x * w[1:2, :] + xq * w[2:3, :]

    v = conv(v_ref, vp_ref, vn_ref, 0)
    x0 = conv(x0_ref, x0p_ref, x0n_ref, 1)
    x1 = conv(x1_ref, x1p_ref, x1n_ref, 2)
    vx_ref[...] = _to_pair_rows(v * x1, scr).astype(vx_ref.dtype)
    x0c_ref[...] = _to_pair_rows(x0, scr).astype(x0c_ref.dtype)


def _short_conv(proj, col_blk0, hw, w_short3, b_short3, n_ctx, seq, dseq, tm):
    t = proj.shape[0]
    assert seq % tm == 0 and dseq % tm == 0
    nt = t // tm
    r8 = tm // SUBLANES
    nb8 = t // SUBLANES

    def cur(k):
        return pl.BlockSpec((tm, hw), lambda i: (i, col_blk0 + k))

    def prev(k):
        return pl.BlockSpec((SUBLANES, hw), lambda i: (jnp.maximum(i * r8 - 1, 0), col_blk0 + k))

    def nxt(k):
        return pl.BlockSpec((SUBLANES, hw), lambda i: (jnp.minimum((i + 1) * r8, nb8 - 1), col_blk0 + k))

    kern = functools.partial(_short_conv_kernel, n_ctx // tm, seq // tm, dseq // tm)
    return pl.pallas_call(
        kern,
        out_shape=(jax.ShapeDtypeStruct((t // 2, 2 * hw), BF16), jax.ShapeDtypeStruct((t // 2, 2 * hw), BF16)),
        grid=(nt,),
        in_specs=[cur(0), cur(1), cur(2), prev(0), prev(1), prev(2), nxt(0), nxt(1), nxt(2),
                  pl.BlockSpec((3, 3, hw), lambda i: (0, 0, 0)),
                  pl.BlockSpec((3, 1, hw), lambda i: (0, 0, 0))],
        out_specs=(pl.BlockSpec((tm // 2, 2 * hw), lambda i: (i, 0)),
                   pl.BlockSpec((tm // 2, 2 * hw), lambda i: (i, 0))),
        scratch_shapes=[pltpu.VMEM((hw // LANES * tm, LANES), F32)],
        compiler_params=_cp(("parallel",)),
        name="short_conv",
    )(proj, proj, proj, proj, proj, proj, proj, proj, proj, w_short3, b_short3)


def _filt_gen_kernel(seq_len, n_bands, wt_ref, wc_ref, ws_ref, b1_ref, f1_ref, w2_ref, b2_ref, f2_ref,
                     w3_ref, b3_ref, rate_ref, a_ref, bm_ref, abs_ref, scr):
    i = pl.program_id(0)
    tl = 2 * a_ref.shape[0]
    hw = a_ref.shape[1] // 2
    pos = lax.broadcasted_iota(jnp.int32, (tl, 1), 0) + i * tl
    t = pos.astype(F32) / seq_len
    bands = (lax.broadcasted_iota(jnp.int32, (1, wc_ref.shape[0]), 1) + 1).astype(F32)
    ang = (2.0 * math.pi) * t * bands
    pre = t * wt_ref[...] + _dot3(jnp.cos(ang), wc_ref[...]) + _dot3(jnp.sin(ang), ws_ref[...])
    h = jnp.sin(f1_ref[...] * (pre + b1_ref[...]))
    h = jnp.sin(f2_ref[...] * (_dot3(h, w2_ref[...]) + b2_ref[...]))
    h = _dot3(h, w3_ref[...]) + b3_ref[...]
    h = h * jnp.exp(-t * rate_ref[...])
    h_f = h[:, :hw]
    h_b = h[:, hw:]
    colabs = jnp.sum(jnp.abs(h_f) + jnp.abs(h_b), axis=0, keepdims=True)
    h_bp = jnp.where(pos == 0, 0.0, h_b)
    a_ref[...] = _to_pair_rows(h_f + h_bp, scr).astype(a_ref.dtype)
    bm_ref[...] = _to_pair_rows(h_bp - h_f, scr).astype(bm_ref.dtype)

    @pl.when(i == 0)
    def _():
        abs_ref[...] = colabs

    @pl.when(i > 0)
    def _():
        abs_ref[...] += colabs


def _filt_gen(seq_len, w_f1, b_f1, freq1, w_f2, b_f2, freq2, w_f3, b_f3, rates2):
    n_emb, hid_raw = w_f1.shape
    n_bands = (n_emb - 1) // 2
    hw2 = w_f3.shape[1]
    hw = hw2 // 2
    tl = min(seq_len, 512)
    hid = LANES
    assert n_bands <= LANES and hid_raw <= LANES

    def pad(x, rows, cols):
        return jnp.zeros((rows, cols), F32).at[:x.shape[0], :x.shape[1]].set(x.astype(F32))

    wt = pad(w_f1[0:1], 1, hid)
    wc = pad(w_f1[1:1 + n_bands], LANES, hid)
    ws = pad(w_f1[1 + n_bands:], LANES, hid)
    b_f1, freq1, b_f2, freq2 = (pad(x, 1, hid) for x in (b_f1, freq1, b_f2, freq2))
    w_f2 = pad(w_f2, hid, hid)
    w_f3 = pad(w_f3, hid, hw2)
    full = lambda shape: pl.BlockSpec(shape, lambda i: (0,) * len(shape))
    kern = functools.partial(_filt_gen_kernel, seq_len, n_bands)
    return pl.pallas_call(
        kern,
        out_shape=(jax.ShapeDtypeStruct((seq_len // 2, hw2), BF16), jax.ShapeDtypeStruct((seq_len // 2, hw2), BF16),
                   jax.ShapeDtypeStruct((1, hw), F32)),
        grid=(seq_len // tl,),
        in_specs=[full((1, hid)), full((LANES, hid)), full((LANES, hid)), full((1, hid)), full((1, hid)),
                  full((hid, hid)), full((1, hid)), full((1, hid)), full((hid, hw2)), full((1, hw2)),
                  full((1, hw2))],
        out_specs=(pl.BlockSpec((tl // 2, hw2), lambda i: (i, 0)), pl.BlockSpec((tl // 2, hw2), lambda i: (i, 0)),
                   full((1, hw))),
        scratch_shapes=[pltpu.VMEM((hw // LANES * tl, LANES), F32)],
        compiler_params=_cp(("arbitrary",)),
        name="filt_gen",
    )(wt, wc, ws, b_f1, freq1, w_f2, b_f2, freq2, w_f3, b_f3, rates2)


def _trig_tables(row_hi, row_lo, col, period):
    col = col[None, :]
    ang_lo = ((row_lo[:, None] * col) % period).astype(F32) * (2.0 * math.pi / period)
    ang_hi = ((row_hi[:, None] * col) % period).astype(F32) * (2.0 * math.pi / period)
    c_lo, s_lo = jnp.cos(ang_lo)[None], jnp.sin(ang_lo)[None]
    c_hi, s_hi = jnp.cos(ang_hi)[:, None], jnp.sin(ang_hi)[:, None]
    n = row_hi.shape[0] * row_lo.shape[0]
    ctab = (c_hi * c_lo - s_hi * s_lo).reshape(n, col.shape[1])
    stab = (s_hi * c_lo + c_hi * s_lo).reshape(n, col.shape[1])
    return ctab.astype(BF16), stab.astype(BF16)


def _dft_tables(seq_len):
    m = seq_len // 2
    r = 1
    while r * r < m:
        r *= 2
    assert m % r == 0
    i32 = jnp.int32
    idx = jnp.arange(m, dtype=i32)
    hi = jnp.arange(m // r, dtype=i32) * r
    lo = jnp.arange(r, dtype=i32)
    ce, se = _trig_tables(hi, lo, idx, 2 * m)
    co, so = _trig_tables(hi, lo, 2 * idx + 1, 2 * seq_len)
    cot, sot = _trig_tables(2 * hi, 2 * lo + 1, idx, 2 * seq_len)
    return ce, se, co, so, cot, sot


def _alt_row(n):
    return (1 - 2 * (lax.broadcasted_iota(jnp.int32, (SUBLANES, n), 1) % 2)).astype(BF16)


def _filt_dft_kernel(seq_len, ce_ref, se_ref, co_ref, so_ref, ae_ref, ao_ref, be_ref, bo_ref, abs_ref,
                     krl_ref, krh_ref, kil_ref, kih_ref, kmid_ref):
    i = pl.program_id(0)
    scale = (1.0 / seq_len) / (abs_ref[...] + EPS)
    ec = _dot(ce_ref[...], ae_ref[...])
    oc = _dot(co_ref[...], ao_ref[...])
    es = _dot(se_ref[...], be_ref[...])
    os_ = _dot(so_ref[...], bo_ref[...])
    krl_ref[...] = (ec + oc) * scale
    krh_ref[...] = (ec - oc) * scale
    kil_ref[...] = (es + os_) * scale
    kih_ref[...] = (os_ - es) * scale

    @pl.when(i == 0)
    def _():
        sgn = _alt_row(ae_ref.shape[0])
        kmid_ref[0:1, :] = _dot(sgn, ae_ref[...])[0:1] * scale
        kmid_ref[1:2, :] = _dot(sgn, bo_ref[...])[0:1] * scale


def _filt_dft(tabs, a2, bm2, colabs, seq_len):
    m = seq_len // 2
    hw = a2.shape[1] // 2
    ce, se, co, so = tabs[:4]
    tf = min(m, 256)
    kern = functools.partial(_filt_dft_kernel, seq_len)
    mode = pl.Buffered(1) if m * hw * 2 > BIG_BLOCK_BYTES // 2 else None
    tab = pl.BlockSpec((tf, m), lambda i: (i, 0))
    res = lambda c: pl.BlockSpec((m, hw), lambda i: (0, c), pipeline_mode=mode)
    out = pl.BlockSpec((tf, hw), lambda i: (i, 0))
    sds = jax.ShapeDtypeStruct((m, hw), F32)
    return pl.pallas_call(
        kern,
        out_shape=(sds, sds, sds, sds, jax.ShapeDtypeStruct((2, hw), F32)),
        grid=(m // tf,),
        in_specs=[tab, tab, tab, tab, res(0), res(1), res(0), res(1), pl.BlockSpec((1, hw), lambda i: (0, 0))],
        out_specs=(out, out, out, out, pl.BlockSpec((2, hw), lambda i: (0, 0))),
        compiler_params=_cp(("arbitrary",)),
        name="filt_dft",
    )(ce, se, co, so, a2, a2, bm2, bm2, colabs)


def _hy_fwd_kernel(ce_ref, se_ref, co_ref, so_ref, ve_ref, vo_ref, krl_ref, krh_ref, kil_ref, kih_ref, kmid_ref,
                   ae_ref, be_ref, ao_ref, bo_ref, ymid_ref):
    j = pl.program_id(1)
    tf, m = ce_ref.shape
    nb = ae_ref.shape[0]
    krl, krh, kil, kih = krl_ref[...], krh_ref[...], kil_ref[...], kih_ref[...]
    row = lax.broadcasted_iota(jnp.int32, (tf, 1), 0) + j * tf
    dc = jnp.where(row == 0, 0.5, 1.0)
    for s in range(nb):
        ve = ve_ref[s * m:(s + 1) * m, :]
        vo = vo_ref[s * m:(s + 1) * m, :]
        ec = _dot(ce_ref[...], ve)
        oc = _dot(co_ref[...], vo)
        es = _dot(se_ref[...], ve)
        os_ = _dot(so_ref[...], vo)
        p_lo, p_hi = ec + oc, ec - oc
        q_lo, q_hi = es + os_, os_ - es
        yre_lo = (p_lo * krl + q_lo * kil) * dc
        yim_lo = p_lo * kil - q_lo * krl
        yre_hi = (p_hi * krh + q_hi * kih) * dc
        yim_hi = p_hi * kih - q_hi * krh
        ae_ref[s] = (yre_lo + yre_hi).astype(ae_ref.dtype)
        be_ref[s] = (yim_lo - yim_hi).astype(be_ref.dtype)
        ao_ref[s] = (yre_lo - yre_hi).astype(ao_ref.dtype)
        bo_ref[s] = (yim_lo + yim_hi).astype(bo_ref.dtype)

    @pl.when(j == 0)
    def _():
        sgn = _alt_row(m)
        kr_m, ki_m = kmid_ref[0:1, :], kmid_ref[1:2, :]
        for s in range(nb):
            p_m = _dot(sgn, ve_ref[s * m:(s + 1) * m, :])[0:1]
            q_m = _dot(sgn, vo_ref[s * m:(s + 1) * m, :])[0:1]
            ymid_ref[s, 0:1, :] = p_m * kr_m + q_m * ki_m
            ymid_ref[s, 1:2, :] = p_m * ki_m - q_m * kr_m


BIG_BLOCK_BYTES = 4 * 1024 * 1024


def _seq_batch(seq_len, n_seq):
    nb = max(1, min(n_seq, 4 * MXU_SIDE // seq_len))
    while n_seq % nb:
        nb -= 1
    return nb


def _hy_fwd(tabs, vx2, kspec, n_seq, seq_len, row0_blk):
    m = seq_len // 2
    hw = vx2.shape[1] // 2
    ce, se, co, so = tabs[:4]
    krl, krh, kil, kih, kmid = kspec
    tf = min(m, 256)
    nb = _seq_batch(seq_len, n_seq)
    assert row0_blk % nb == 0
    mode = pl.Buffered(1) if nb * m * hw * 2 > BIG_BLOCK_BYTES // 2 else None
    tab = pl.BlockSpec((tf, m), lambda b, j: (j, 0))
    vspec = lambda c: pl.BlockSpec((nb * m, hw), lambda b, j: (row0_blk // nb + b, c), pipeline_mode=mode)
    kt = pl.BlockSpec((tf, hw), lambda b, j: (j, 0))
    out = pl.BlockSpec((nb, tf, hw), lambda b, j: (b, j, 0))
    sds = jax.ShapeDtypeStruct((n_seq, m, hw), BF16)
    return pl.pallas_call(
        _hy_fwd_kernel,
        out_shape=(sds, sds, sds, sds, jax.ShapeDtypeStruct((n_seq, 2, hw), F32)),
        grid=(n_seq // nb, m // tf),
        in_specs=[tab, tab, tab, tab, vspec(0), vspec(1), kt, kt, kt, kt,
                  pl.BlockSpec((2, hw), lambda b, j: (0, 0))],
        out_specs=(out, out, out, out, pl.BlockSpec((nb, 2, hw), lambda b, j: (b, 0, 0))),
        compiler_params=_cp(("arbitrary", "arbitrary")),
        name="hy_fwd",
    )(ce, se, co, so, vx2, vx2, krl, krh, kil, kih, kmid)


def _hy_inv_kernel(ce_ref, se_ref, cot_ref, sot_ref, ae_ref, be_ref, ao_ref, bo_ref, ymid_ref,
                   vx_ref, x0_ref, fb_ref, wb_ref, o_ref, scr):
    j = pl.program_id(1)
    tt = ce_ref.shape[0]
    nb = ae_ref.shape[0]
    hw = fb_ref.shape[1]
    row = lax.broadcasted_iota(jnp.int32, (tt, 1), 0) + j * tt
    sgn = (1 - 2 * (row % 2)).astype(F32)
    for s in range(nb):
        rows = slice(s * tt, (s + 1) * tt)
        ymid = ymid_ref[s]
        z_e = _dot(ce_ref[...], ae_ref[s]) - _dot(se_ref[...], be_ref[s]) + sgn * ymid[0:1]
        z_o = _dot(cot_ref[...], ao_ref[s]) - _dot(sot_ref[...], bo_ref[s]) - sgn * ymid[1:2]
        proj = []
        for par, z in ((0, z_e), (1, z_o)):
            cols = slice(par * hw, (par + 1) * hw)
            z = z + vx_ref[rows, cols].astype(F32) * fb_ref[...]
            yb = x0_ref[rows, cols].astype(F32) * z
            proj.append(_dot(yb.astype(BF16), wb_ref[...]))
        o_ref[2 * s * tt:2 * (s + 1) * tt, :] = _from_pair_rows(
            jnp.concatenate(proj, axis=1), scr).astype(o_ref.dtype)


def _hy_inv(tabs, spec, vx2, x0c2, fbias, w_bb, n_seq, seq_len, row0_blk):
    m = seq_len // 2
    hw = fbias.shape[1]
    d = w_bb.shape[1]
    ce, se, _, _, cot, sot = tabs
    ae, be, ao, bo, ymid = spec
    tt = min(m, 128)
    per = m // tt
    nb = _seq_batch(seq_len, n_seq) if per == 1 else 1
    assert row0_blk % nb == 0
    mode = pl.Buffered(1) if nb * m * hw * 2 > BIG_BLOCK_BYTES // 2 else None
    tab = pl.BlockSpec((tt, m), lambda b, j: (j, 0))
    res = pl.BlockSpec((nb, m, hw), lambda b, j: (b, 0, 0), pipeline_mode=mode)
    rows_in = pl.BlockSpec((nb * tt, 2 * hw), lambda b, j: ((row0_blk // nb + b) * per + j, 0))
    return pl.pallas_call(
        _hy_inv_kernel,
        out_shape=jax.ShapeDtypeStruct((n_seq * seq_len, d), BF16),
        grid=(n_seq // nb, per),
        in_specs=[tab, tab, tab, tab, res, res, res, res,
                  pl.BlockSpec((nb, 2, hw), lambda b, j: (b, 0, 0)),
                  rows_in, rows_in,
                  pl.BlockSpec((1, hw), lambda b, j: (0, 0)),
                  pl.BlockSpec((hw, d), lambda b, j: (0, 0), pipeline_mode=pl.Buffered(1))],
        out_specs=pl.BlockSpec((nb * 2 * tt, d), lambda b, j: (b * per + j, 0)),
        scratch_shapes=[pltpu.VMEM((d // LANES * 2 * tt, LANES), F32)],
        compiler_params=_cp(("arbitrary", "arbitrary")),
        name="hy_inv",
    )(ce, se, cot, sot, ae, be, ao, bo, ymid, vx2, x0c2, fbias, w_bb)


MERGE_COL_BLOCKS = 4

def _merge_kernel(tok, n_groups, xp_ref, xs_ref, pos_ref, g0_ref, g1_ref, pa_ref, pbc_ref, pbs_ref,
                  gate1_ref, sh2_ref, sc2_ref, gn_ref, wo_ref, wr_ref, br_ref,
                  x2_ref, h2_ref, rt_ref):
    i = pl.program_id(0)
    is_ctx = i < tok.nct
    tm, d = x2_ref.shape
    pb = jnp.where(is_ctx, pbc_ref[...], pbs_ref[...])
    merged = g0_ref[...] * pa_ref[...] + g1_ref[...] * pb
    cb = d // MERGE_COL_BLOCKS
    ssq = jnp.zeros((tm, 1), F32)
    for c in range(MERGE_COL_BLOCKS):
        cols = slice(c * cb, (c + 1) * cb)
        att = _dot(merged, wo_ref[:, cols])
        x = jnp.where(is_ctx, xp_ref[:, cols], xs_ref[:, cols] + pos_ref[:, cols])
        x2c = x + gate1_ref[0][:, cols] * att
        x2_ref[:, cols] = x2c
        ssq = ssq + jnp.sum(x2c * x2c, axis=-1, keepdims=True)
    x2 = x2_ref[...]
    h2 = x2 * lax.rsqrt(ssq * (1.0 / d) + EPS) * gn_ref[...] * (1.0 + sc2_ref[0]) + sh2_ref[0]
    h2b = h2.astype(BF16)
    _store_row_tiles(h2_ref, 0, h2b)
    logits = _dot(h2b, wr_ref[...]) + br_ref[...]
    lane = lax.broadcasted_iota(jnp.int32, logits.shape, 1)
    gl = jnp.where(lane < n_groups, logits, -jnp.inf)
    gmax = jnp.max(gl, axis=-1, keepdims=True)
    gidx = jnp.min(jnp.where(gl == gmax, lane, n_groups), axis=-1, keepdims=True)
    rt_ref[...] = jnp.where(lane == LANES - 1, gidx.astype(F32), logits)


def _merge_out(xp, xs, pos, proj, gate_blk0, pa, pb_ctx, pb_lat, mod, g2, w_out, w_r, b_r, n_groups, tok):
    d = xp.shape[1]
    tm = tok.tm
    t_all = xp.shape[0] + xs.shape[0]
    kern = functools.partial(_merge_kernel, tok, n_groups)
    one = pl.Buffered(1)
    ns = d // 2 // LANES

    def modspec(k):
        return pl.BlockSpec((1, 1, d), lambda i: (tok.cond(i) * N_MOD + k, 0, 0))

    return pl.pallas_call(
        kern,
        out_shape=(jax.ShapeDtypeStruct((t_all, d), F32), jax.ShapeDtypeStruct((t_all * ns, LANES), jnp.uint32),
                   jax.ShapeDtypeStruct((t_all, LANES), F32)),
        grid=(tok.n,),
        in_specs=[
            pl.BlockSpec((tm, d), lambda i: (tok.ctx_idx(i), 0)),
            pl.BlockSpec((tm, d), lambda i: (tok.lat_idx(i), 0)),
            pl.BlockSpec((tm, d), lambda i: (tok.pos_idx(i), 0)),
            pl.BlockSpec((tm, d), lambda i: (i, gate_blk0)),
            pl.BlockSpec((tm, d), lambda i: (i, gate_blk0 + 1)),
            pl.BlockSpec((tm, d), lambda i: (i, 0)),
            pl.BlockSpec((tm, d), lambda i: (tok.ctx_idx(i), 0)),
            pl.BlockSpec((tm, d), lambda i: (tok.lat_idx(i), 0)),
            modspec(2), modspec(3), modspec(4),
            pl.BlockSpec((1, d), lambda i: (0, 0)),
            pl.BlockSpec((d, d), lambda i: (0, 0), pipeline_mode=one),
            pl.BlockSpec((d, LANES), lambda i: (0, 0)),
            pl.BlockSpec((1, LANES), lambda i: (0, 0)),
        ],
        out_specs=(pl.BlockSpec((tm, d), lambda i: (i, 0)), pl.BlockSpec((tm * ns, LANES), lambda i: (i, 0)),
                   pl.BlockSpec((tm, LANES), lambda i: (i, 0))),
        compiler_params=_cp(("parallel",)),
        name="merge_out",
    )(xp, xs, pos, proj, proj, pa, pb_ctx, pb_lat, mod, mod, mod, g2, w_out, w_r, b_r)


ROW_UNROLL = 8

def _moe_kernel(n_tiles, n_groups, epg, tg_ref, tnv_ref, rt_ref,
                h2_hbm, wr_ref, br_ref, wg_ref, wu_ref, wd_ref, out_hbm,
                xbuf, xb_ref, acc, obuf, gws, gsem, ssem):
    i = pl.program_id(0)
    j = pl.program_id(1)
    tm = xb_ref.shape[0]
    ns = h2_hbm.shape[1]
    nv = tnv_ref[i]
    slot = i % 2

    def buf_rows(buf, slot_, r, rows):
        return buf.at[pl.ds(pl.multiple_of((slot_ * tm + r) * ns, ns), rows * ns), :]

    def gather_copy(tok_row, slot_, r):
        return pltpu.make_async_copy(h2_hbm.at[tok_row], buf_rows(xbuf, slot_, r, 1), gsem.at[slot_])

    def gather_wait(slot_, rows):
        dst = buf_rows(xbuf, slot_, 0, rows)
        pltpu.make_async_copy(dst, dst, gsem.at[slot_]).wait()

    def scatter_copy(tok_row, slot_, r):
        return pltpu.make_async_copy(buf_rows(obuf, slot_, r, 1), out_hbm.at[tok_row], ssem.at[0])

    def scatter_wait(slot_, rows):
        src = buf_rows(obuf, slot_, 0, rows)
        pltpu.make_async_copy(src, src, ssem.at[0]).wait()

    def for_rows(count, fn):
        n_full = count // ROW_UNROLL

        def body(b, c):
            for k in range(ROW_UNROLL):
                fn(b * ROW_UNROLL + k)
            return c

        def tail(r, c):
            fn(r)
            return c

        lax.fori_loop(0, n_full, body, 0)
        lax.fori_loop(n_full * ROW_UNROLL, count, tail, 0)

    def wait_scatter(count, slot_):
        p = tm
        while p >= 1:
            @pl.when((count & p) != 0)
            def _(p=p):
                scatter_wait(slot_, p)
            p //= 2

    nv_prev = tnv_ref[jnp.maximum(i - 1, 0)]
    nv_next = tnv_ref[jnp.minimum(i + 1, n_tiles - 1)]
    prev_deferred = jnp.logical_and(i > 0, jnp.logical_and(nv_prev == tm, nv > 0))
    deferred = jnp.logical_and(nv == tm, nv_next > 0)

    @pl.when(j == 0)
    def _():
        @pl.when(i == 0)
        def _():
            for_rows(tm, lambda r: gather_copy(rt_ref[r], 0, r).start())

        @pl.when(jnp.logical_or(i == 0, nv_prev > 0))
        def _():
            gather_wait(slot, tm)

    @pl.when(jnp.logical_and(j == 0, nv > 0))
    def _():
        grp = tg_ref[i]
        xb = _load_row_tiles(xbuf, slot * tm * ns, tm, ns)
        xb_ref[...] = xb
        logits = _dot(xb, wr_ref[...]) + br_ref[...]
        lane = lax.broadcasted_iota(jnp.int32, logits.shape, 1)
        gl = jnp.where(lane < n_groups, logits, -jnp.inf)
        gmax = jnp.max(gl, axis=-1, keepdims=True)
        ge = jnp.exp(gl - gmax)
        g_w = jnp.sum(jnp.where(lane == grp, ge, 0.0), axis=-1, keepdims=True) / jnp.sum(ge, axis=-1, keepdims=True)
        base = n_groups + grp * epg
        e = [jnp.sum(jnp.where(lane == base + k, logits, 0.0), axis=-1, keepdims=True) for k in range(epg)]
        sel = []
        for k in range(epg):
            rank = jnp.zeros_like(e[k])
            for m in range(epg):
                if m == k:
                    continue
                ahead = (e[m] > e[k]) if m > k else (e[m] >= e[k])
                rank = rank + ahead.astype(F32)
            sel.append(rank < TOP_K_INNER)
        emax = functools.reduce(jnp.maximum, e)
        ex = [jnp.where(sel[k], jnp.exp(e[k] - emax), 0.0) for k in range(epg)]
        den = functools.reduce(lambda a, b: a + b, ex)
        for k in range(epg):
            gws[k] = ex[k] / den * g_w
        acc[...] = jnp.zeros(acc.shape, acc.dtype)

    def expert_step(scatter_prev):
        per = tm // epg
        for k in range(per):
            r = j * per + k
            if scatter_prev:
                scatter_copy(rt_ref[(i - 1) * tm + r], 1 - slot, r).start()
            gather_copy(rt_ref[(i + 1) * tm + r], 1 - slot, r).start()
        xb = xb_ref[...]
        g = _dot(xb, wg_ref[...])
        u = _dot(xb, wu_ref[...])
        hid = (g * jax.nn.sigmoid(g)) * u * gws[j]
        acc[...] += _dot(hid.astype(BF16), wd_ref[...])

    @pl.when(jnp.logical_and(nv > 0, prev_deferred))
    def _():
        expert_step(True)

    @pl.when(jnp.logical_and(nv > 0, jnp.logical_not(prev_deferred)))
    def _():
        expert_step(False)

    @pl.when(j == epg - 1)
    def _():
        @pl.when(prev_deferred)
        def _():
            scatter_wait(1 - slot, tm)

        @pl.when(nv > 0)
        def _():
            _store_row_tiles(obuf, slot * tm * ns, acc[...].astype(BF16))

        @pl.when(jnp.logical_and(nv > 0, jnp.logical_not(deferred)))
        def _():
            for_rows(nv, lambda r: scatter_copy(rt_ref[i * tm + r], slot, r).start())
            wait_scatter(nv, slot)


def _moe(h2p, w_r, b_r, w_g, w_u, w_d, tile_grp, tile_nv, row_tok, n_groups, epg, tm):
    t, ns, _ = h2p.shape
    d = 2 * ns * LANES
    ff = w_g.shape[-1]
    n_tiles = tile_grp.shape[0]
    assert tm & (tm - 1) == 0, "row-count waits decompose tm in binary"
    kern = functools.partial(_moe_kernel, n_tiles, n_groups, epg)

    def widx(i, j, tg, tnv, rt):
        return tg[i] * epg + jnp.where(tnv[i] > 0, j, epg - 1)

    grid_spec = pltpu.PrefetchScalarGridSpec(
        num_scalar_prefetch=3,
        grid=(n_tiles, epg),
        in_specs=[
            pl.BlockSpec(memory_space=pl.ANY),
            pl.BlockSpec((d, LANES), lambda i, j, tg, tnv, rt: (0, 0)),
            pl.BlockSpec((1, LANES), lambda i, j, tg, tnv, rt: (0, 0)),
            pl.BlockSpec((None, d, ff), lambda i, j, tg, tnv, rt: (widx(i, j, tg, tnv, rt), 0, 0)),
            pl.BlockSpec((None, d, ff), lambda i, j, tg, tnv, rt: (widx(i, j, tg, tnv, rt), 0, 0)),
            pl.BlockSpec((None, ff, d), lambda i, j, tg, tnv, rt: (widx(i, j, tg, tnv, rt), 0, 0)),
        ],
        out_specs=pl.BlockSpec(memory_space=pl.ANY),
        scratch_shapes=[
            pltpu.VMEM((2 * tm * ns, LANES), jnp.uint32),
            pltpu.VMEM((tm, d), BF16),
            pltpu.VMEM((tm, d), F32),
            pltpu.VMEM((2 * tm * ns, LANES), jnp.uint32),
            pltpu.VMEM((epg, tm, 1), F32),
            pltpu.SemaphoreType.DMA((2,)),
            pltpu.SemaphoreType.DMA((1,)),
        ],
    )
    return pl.pallas_call(
        kern,
        out_shape=jax.ShapeDtypeStruct((t, ns, LANES), jnp.uint32),
        grid_spec=grid_spec,
        compiler_params=_cp(("arbitrary", "arbitrary")),
        name="moe",
    )(tile_grp, tile_nv, row_tok, h2p, w_r, b_r, w_g, w_u, w_d)


def _moe_plan(grp, n_groups, tm, n_tiles):
    t = grp.shape[0]
    onehot = (grp[:, None] == jnp.arange(n_groups, dtype=jnp.int32)[None, :]).astype(jnp.int32)
    csum = jnp.cumsum(onehot, axis=0)
    counts = csum[-1]
    rank = jnp.take_along_axis(csum, grp[:, None], axis=1)[:, 0] - 1
    ntile_g = (counts + tm - 1) // tm
    tile_end = jnp.cumsum(ntile_g)
    tile_off = tile_end - ntile_g
    slot = tile_off[grp] * tm + rank
    row_tok = jnp.zeros((n_tiles * tm,), jnp.int32).at[slot].set(jnp.arange(t, dtype=jnp.int32))
    tiles = jnp.arange(n_tiles, dtype=jnp.int32)
    tile_grp = jnp.minimum(jnp.sum((tiles[:, None] >= tile_end[None, :]).astype(jnp.int32), axis=1),
                           n_groups - 1)
    tile_nv = jnp.clip(counts[tile_grp] - (tiles - tile_off[tile_grp]) * tm, 0, tm)
    tile_nv = jnp.where(tiles < tile_end[-1], tile_nv, 0)
    return tile_grp.astype(jnp.int32), tile_nv.astype(jnp.int32), row_tok


def _final_kernel(x2_ref, m_ref, gate2_ref, gf_ref, o_ref):
    tm, d = x2_ref.shape
    ns = d // 2 // LANES
    m = _load_row_tiles(m_ref, 0, tm, ns).astype(F32)
    x3 = x2_ref[...] + gate2_ref[0] * m
    o_ref[...] = _rms(x3, gf_ref[...])


def _final(x2, moe, mod, g_final, row_blk0, n_rows, cond_of_tile, tm):
    d = x2.shape[1]
    ns = d // 2 // LANES
    return pl.pallas_call(
        _final_kernel,
        out_shape=jax.ShapeDtypeStruct((n_rows, d), F32),
        grid=(n_rows // tm,),
        in_specs=[pl.BlockSpec((tm, d), lambda i: (row_blk0 + i, 0)),
                  pl.BlockSpec((tm * ns, LANES), lambda i: (row_blk0 + i, 0)),
                  pl.BlockSpec((1, 1, d), lambda i: (cond_of_tile(i) * N_MOD + 5, 0, 0)),
                  pl.BlockSpec((1, d), lambda i: (0, 0))],
        out_specs=pl.BlockSpec((tm, d), lambda i: (i, 0)),
        compiler_params=_cp(("parallel",)),
        name="final",
    )(x2, moe, mod, g_final)


def _grid_pos_embed(n_tokens, dim):
    rows_n = n_tokens // GRID_W
    quarter = dim // 4
    omega = 1.0 / (10000.0 ** (jnp.arange(quarter, dtype=F32) / quarter))

    def emb(n):
        ang = jnp.arange(n).astype(F32)[:, None] * omega[None, :]
        return jnp.concatenate([jnp.sin(ang), jnp.cos(ang)], axis=-1)

    e_r = jnp.broadcast_to(emb(rows_n)[:, None, :], (rows_n, GRID_W, dim // 2))
    e_c = jnp.broadcast_to(emb(GRID_W)[None, :, :], (rows_n, GRID_W, dim // 2))
    return jnp.concatenate([e_r, e_c], axis=-1).reshape(n_tokens, dim)


def _state_to_lanes(s):
    b, _, _, g, p = s.shape
    return jnp.transpose(s, (0, 3, 2, 1, 4)).reshape(b, g, 4 * p)


def _lanes_to_state(x, p):
    b, g, _ = x.shape
    return jnp.transpose(x.reshape(b, g, 2, 2, p), (0, 3, 2, 1, 4))


def kernel(x_prompt, x_sample, state_ssm, c, c_ctx, w_ada, b_ada, g_norm1, g_norm2, w_in, ssm_a_re, ssm_a_im, ssm_log_dt, ssm_b_re, ssm_b_im, ssm_c_re, ssm_c_im, ssm_d, w_glu, b_glu, w_short, b_short, w_f1, b_f1, freq1, w_f2, b_f2, freq2, w_f3, b_f3, filter_bias, w_branch_a, w_branch_b, w_out, w_router_group, b_router_group, w_router_expert, b_router_expert, w_exp_gate, w_exp_up, w_exp_down, g_final):
    depth = w_ada.shape[0]
    assert depth == 1, "single-layer pipeline"
    bsz, seq, d = x_prompt.shape
    dbsz, dseq, _ = x_sample.shape
    g_all, p_state = ssm_a_re.shape[2:]
    ssm_w = ssm_d.shape[1]
    hw = filter_bias.shape[1]
    n_groups, _, epg = w_router_expert.shape[1:]
    ff = w_exp_gate.shape[-1]
    assert ssm_w // g_all * CHUNK == MXU_SIDE and 4 * p_state == MXU_SIDE
    assert ssm_w == hw and d == 2 * hw
    n_ctx = bsz * seq
    n_lat = dbsz * dseq
    t_all = n_ctx + n_lat

    xp = x_prompt.reshape(n_ctx, d)
    xs = x_sample.reshape(n_lat, d)
    pos = _grid_pos_embed(dseq, d)

    n_cond = 1 + dbsz
    cond8 = jnp.zeros((SUBLANES, d), F32).at[0].set(c_ctx).at[1:n_cond].set(c)
    mod = _ada_mod(cond8, w_ada[0], b_ada[0][None])
    mod = mod[:n_cond].reshape(n_cond * N_MOD, 1, d)

    tm_e = min(256, seq, dseq)
    tok_e = _Tok(n_ctx, n_lat, dseq, tm_e)
    tm_l = min(512, n_ctx, dseq)
    tok_l = _Tok(n_ctx, n_lat, dseq, tm_l)
    gate_col0 = ssm_w + 3 * hw
    h1 = _norm_mod(xp, xs, pos, mod, g_norm1, tok_l)
    tm_p, tn_p = min(1024, t_all), min(1024, hw)
    n_in = w_in.shape[2]
    u_a = _in_proj(h1, w_in[0], (0, 0), (0, ssm_w), F32, tm_p, tn_p, "in_proj_a")
    proj = _in_proj(h1, w_in[0], (gate_col0, n_in), (ssm_w, gate_col0), BF16, tm_p, tn_p, "in_proj")
    n_gate_cols = n_in - gate_col0

    w_state, w_so, a_step = _ssm_operators(ssm_a_re[0], ssm_a_im[0], ssm_log_dt[0], ssm_b_re[0], ssm_b_im[0],
                                           ssm_c_re[0], ssm_c_im[0], ssm_d[0])
    nc_ctx = seq // CHUNK
    nc_lat = dseq // CHUNK
    seq_per_blk = max(1, min(bsz, MXU_SIDE // nc_ctx))
    assert bsz % seq_per_blk == 0 and n_ctx % dseq == 0
    s0_ctx = jnp.zeros((bsz, g_all, 4 * p_state), F32)
    y_ctx, fin_ctx = _ssm_call(u_a, 0, n_ctx, w_state, w_so, a_step, s0_ctx, seq_per_blk, nc_ctx, "ssm_ctx")
    s0_lat = _state_to_lanes(state_ssm[:, 0].astype(F32))
    y_lat, _ = _ssm_call(u_a, n_ctx // dseq, n_lat, w_state, w_so, a_step, s0_lat, 1, nc_lat, "ssm_lat")
    new_state = _lanes_to_state(fin_ctx, p_state)[:, None]

    pa = _glu_branch(y_ctx, y_lat, w_glu[0].astype(BF16), b_glu[0][None], w_branch_a[0].astype(BF16), tok_l)

    w_short3 = jnp.transpose(w_short[0].reshape(-1, 3, hw), (1, 0, 2))
    b_short3 = b_short[0].reshape(3, 1, hw)
    vx2, x0c2 = _short_conv(proj, n_gate_cols // hw, hw, w_short3, b_short3, n_ctx, seq, dseq, tm_e)
    rates = jnp.abs(jnp.linspace(math.log(DECAY_TARGET) / DECAY_FAST, math.log(DECAY_TARGET) / DECAY_SLOW,
                                 hw, dtype=F32))
    rates2 = jnp.concatenate([rates, rates])[None]
    fbias = filter_bias[0][None].astype(F32)
    w_bb = w_branch_b[0].astype(BF16)
    pbs = []
    for seq_len, n_seq, row0 in ((seq, bsz, 0), (dseq, dbsz, n_ctx // dseq)):
        tabs = _dft_tables(seq_len)
        a2, bm2, colabs = _filt_gen(seq_len, w_f1[0], b_f1[0][None], freq1[0][None], w_f2[0],
                                    b_f2[0][None], freq2[0][None], w_f3[0], b_f3[0][None], rates2)
        kspec = _filt_dft(tabs, a2, bm2, colabs, seq_len)
        spec = _hy_fwd(tabs, vx2, kspec, n_seq, seq_len, row0)
        pbs.append(_hy_inv(tabs, spec, vx2, x0c2, fbias, w_bb, n_seq, seq_len, row0))
    pb_ctx, pb_lat = pbs

    w_r = jnp.zeros((d, LANES), F32)
    w_r = w_r.at[:, :n_groups].set(w_router_group[0])
    w_r = w_r.at[:, n_groups:n_groups + n_groups * epg].set(
        jnp.transpose(w_router_expert[0], (1, 0, 2)).reshape(d, n_groups * epg))
    b_r = jnp.zeros((1, LANES), F32)
    b_r = b_r.at[0, :n_groups].set(b_router_group[0])
    b_r = b_r.at[0, n_groups:n_groups + n_groups * epg].set(b_router_expert[0].reshape(-1))
    w_r = w_r.astype(BF16)
    x2, h2, rout = _merge_out(xp, xs, pos, proj, 0, pa, pb_ctx, pb_lat, mod, g_norm2,
                              w_out[0].astype(BF16), w_r, b_r, n_groups, tok_e)

    tm_m = min(1024, t_all // n_groups)
    n_tiles = t_all // tm_m + n_groups
    grp = rout[:, LANES - 1].astype(jnp.int32)
    tile_grp, tile_nv, row_tok = _moe_plan(grp, n_groups, tm_m, n_tiles)
    ns = d // 2 // LANES
    moe = _moe(h2.reshape(t_all, ns, LANES), w_r, b_r, w_exp_gate[0].astype(BF16), w_exp_up[0].astype(BF16),
               w_exp_down[0].astype(BF16), tile_grp, tile_nv, row_tok, n_groups, epg, tm_m)
    moe = moe.reshape(t_all * ns, LANES)

    gf = g_final[None]
    y_prompt = _final(x2, moe, mod, gf, 0, n_ctx, lambda i: 0, tm_l)
    lat_tiles = dseq // tm_l
    y_sample = _final(x2, moe, mod, gf, n_ctx // tm_l, n_lat, lambda i: 1 + i // lat_tiles, tm_l)
    return (y_prompt.reshape(bsz, seq, d), y_sample.reshape(dbsz, dseq, d), new_state)
```

```python
import functools
import math

import numpy as np
import jax
import jax.numpy as jnp
from jax import lax
from jax.experimental import pallas as pl
from jax.experimental.pallas import tpu as pltpu

F32 = jnp.float32
BF16 = jnp.bfloat16
EPS = 1e-6
GRID_W = 64
N_MOD = 6
TOP_K_INNER = 2
DECAY_TARGET = 1e-2
DECAY_FAST = 0.3
DECAY_SLOW = 1.5
LANES = 128
SUBLANES = 8
MXU_SIDE = 256
VMEM_LIMIT = 56 * 1024 * 1024


def _cp(sem, vmem=VMEM_LIMIT):
    return pltpu.CompilerParams(dimension_semantics=sem, vmem_limit_bytes=vmem)


def _dot(a, b):
    return jnp.dot(a, b, preferred_element_type=F32)


def _split(a):
    hi = a.astype(BF16)
    lo = (a - hi.astype(F32)).astype(BF16)
    return hi, lo


def _dot3(a, b):
    a_hi, a_lo = _split(a)
    b_hi, b_lo = _split(b)
    return _dot(a_hi, b_hi) + _dot(a_lo, b_hi) + _dot(a_hi, b_lo)


def _rms(x, g):
    ms = jnp.mean(x * x, axis=-1, keepdims=True)
    return x * lax.rsqrt(ms + EPS) * g


def _pack_halves(xb):
    n = xb.shape[1] // 2
    lo = lax.bitcast_convert_type(xb[:, :n].astype(F32), jnp.uint32)
    hi = lax.bitcast_convert_type(xb[:, n:].astype(F32), jnp.uint32)
    return (lo >> 16) | (hi & jnp.uint32(0xFFFF0000))


def _unpack_halves(w):
    lo = lax.bitcast_convert_type(w << 16, F32)
    hi = lax.bitcast_convert_type(w & jnp.uint32(0xFFFF0000), F32)
    return jnp.concatenate([lo, hi], axis=1).astype(BF16)


def _to_pair_rows(x, scr):
    m, n = x.shape
    ev, od = [], []
    for c in range(n // LANES):
        scr[c * m:(c + 1) * m, :] = x[:, c * LANES:(c + 1) * LANES]
        ev.append(scr[pl.ds(c * m, m // 2, stride=2), :])
        od.append(scr[pl.ds(c * m + 1, m // 2, stride=2), :])
    return jnp.concatenate(ev + od, axis=1)


def _from_pair_rows(x2, scr):
    h, n2 = x2.shape
    n = n2 // 2
    out = []
    for c in range(n // LANES):
        scr[pl.ds(2 * c * h, h, stride=2), :] = x2[:, c * LANES:(c + 1) * LANES]
        scr[pl.ds(2 * c * h + 1, h, stride=2), :] = x2[:, n + c * LANES:n + (c + 1) * LANES]
        out.append(scr[2 * c * h:2 * (c + 1) * h, :])
    return jnp.concatenate(out, axis=1)


def _store_row_tiles(ref, row0, xb):
    w = _pack_halves(xb)
    m, n = w.shape
    ns = n // LANES
    for c in range(ns):
        ref[pl.ds(row0 + c, m, stride=ns), :] = w[:, c * LANES:(c + 1) * LANES]


def _load_row_tiles(ref, row0, m, ns):
    w = jnp.concatenate([ref[pl.ds(row0 + c, m, stride=ns), :] for c in range(ns)], axis=1)
    return _unpack_halves(w)


def _ada_kernel(c_ref, w_ref, b_ref, o_ref):
    c = c_ref[...]
    a = c * jax.nn.sigmoid(c)
    o_ref[...] = _dot3(a, w_ref[...]) + b_ref[...]


def _ada_mod(cond8, w_ada, b_ada):
    d, n = w_ada.shape
    tn = min(n, 1024)
    return pl.pallas_call(
        _ada_kernel,
        out_shape=jax.ShapeDtypeStruct((SUBLANES, n), F32),
        grid=(n // tn,),
        in_specs=[pl.BlockSpec((SUBLANES, d), lambda j: (0, 0)),
                  pl.BlockSpec((d, tn), lambda j: (0, j)),
                  pl.BlockSpec((1, tn), lambda j: (0, j))],
        out_specs=pl.BlockSpec((SUBLANES, tn), lambda j: (0, j)),
        compiler_params=_cp(("arbitrary",)),
        name="ada_mod",
    )(cond8, w_ada, b_ada)


class _Tok:
    def __init__(self, n_ctx, n_lat, lat_len, tm):
        assert n_ctx % tm == 0 and lat_len % tm == 0 and n_lat % lat_len == 0
        self.tm = tm
        self.nct = n_ctx // tm
        self.nst = n_lat // tm
        self.per_seq = lat_len // tm
        self.n = self.nct + self.nst

    def ctx_idx(self, i):
        return jnp.minimum(i, self.nct - 1)

    def lat_idx(self, i):
        return jnp.maximum(i - self.nct, 0)

    def pos_idx(self, i):
        return self.lat_idx(i) % self.per_seq

    def cond(self, i):
        return jnp.where(i < self.nct, 0, 1 + self.lat_idx(i) // self.per_seq)


def _norm_mod_kernel(tok, xp_ref, xs_ref, pos_ref, sh_ref, sc_ref, g_ref, h_ref):
    i = pl.program_id(0)

    def modulate(x):
        y = _rms(x, g_ref[...])
        return (y * (1.0 + sc_ref[0]) + sh_ref[0]).astype(h_ref.dtype)

    @pl.when(i < tok.nct)
    def _():
        h_ref[...] = modulate(xp_ref[...])

    @pl.when(i >= tok.nct)
    def _():
        h_ref[...] = modulate(xs_ref[...] + pos_ref[...])


def _norm_mod(xp, xs, pos, mod, g1, tok):
    d = xp.shape[1]
    tm = tok.tm
    t_all = xp.shape[0] + xs.shape[0]
    return pl.pallas_call(
        functools.partial(_norm_mod_kernel, tok),
        out_shape=jax.ShapeDtypeStruct((t_all, d), BF16),
        grid=(tok.n,),
        in_specs=[
            pl.BlockSpec((tm, d), lambda i: (tok.ctx_idx(i), 0)),
            pl.BlockSpec((tm, d), lambda i: (tok.lat_idx(i), 0)),
            pl.BlockSpec((tm, d), lambda i: (tok.pos_idx(i), 0)),
            pl.BlockSpec((1, 1, d), lambda i: (tok.cond(i) * N_MOD + 0, 0, 0)),
            pl.BlockSpec((1, 1, d), lambda i: (tok.cond(i) * N_MOD + 1, 0, 0)),
            pl.BlockSpec((1, d), lambda i: (0, 0)),
        ],
        out_specs=pl.BlockSpec((tm, d), lambda i: (i, 0)),
        compiler_params=_cp(("parallel",)),
        name="norm_mod",
    )(xp, xs, pos, mod, mod, g1)


def _in_proj_kernel(n_sig, h_ref, w_ref, o_ref, wb_ref):
    n = pl.program_id(0)
    m = pl.program_id(1)

    @pl.when(m == 0)
    def _():
        wb_ref[...] = w_ref[...].astype(BF16)

    acc = _dot(h_ref[...], wb_ref[...])

    @pl.when(n >= n_sig)
    def _():
        o_ref[...] = acc.astype(o_ref.dtype)

    @pl.when(n < n_sig)
    def _():
        o_ref[...] = jax.nn.sigmoid(acc).astype(o_ref.dtype)


def _in_proj(h, w_in, sig_cols, lin_cols, out_dtype, tm, tn, name):
    t_all, d = h.shape
    for c in sig_cols + lin_cols:
        assert c % tn == 0
    assert t_all % tm == 0
    n_sig = (sig_cols[1] - sig_cols[0]) // tn
    n_lin = (lin_cols[1] - lin_cols[0]) // tn
    sig0, lin0 = sig_cols[0] // tn, lin_cols[0] // tn

    def w_blk(n):
        return jnp.where(n < n_sig, sig0 + n, lin0 + n - n_sig)

    return pl.pallas_call(
        functools.partial(_in_proj_kernel, n_sig),
        out_shape=jax.ShapeDtypeStruct((t_all, (n_sig + n_lin) * tn), out_dtype),
        grid=(n_sig + n_lin, t_all // tm),
        in_specs=[pl.BlockSpec((tm, d), lambda n, m: (m, 0)),
                  pl.BlockSpec((d, tn), lambda n, m: (0, w_blk(n)))],
        out_specs=pl.BlockSpec((tm, tn), lambda n, m: (m, n)),
        scratch_shapes=[pltpu.VMEM((d, tn), BF16)],
        compiler_params=_cp(("arbitrary", "arbitrary")),
        name=name,
    )(h, w_in)


CHUNK = 16
GROUP_PAD = 8


def _ssm_operators(a_re, a_im, log_dt, b_re, b_im, c_re, c_im, d_skip):
    t = CHUNK
    g, p = a_re.shape[1:]
    c = b_re.shape[-1]
    lam = lax.complex(a_re.astype(F32), a_im.astype(F32))
    dt = jnp.exp(log_dt.astype(F32))[..., None]
    ld = lam * dt
    lam_bar = jnp.exp(ld)
    b_bar = ((lam_bar - 1.0) / lam)[..., None] * lax.complex(b_re.astype(F32), b_im.astype(F32))
    c_mat = lax.complex(c_re.astype(F32), c_im.astype(F32))
    taus = jnp.arange(t + 1, dtype=F32)
    e_pow = jnp.exp(taus[None, None, :, None] * ld[:, :, None, :])

    k_all = jnp.real(jnp.einsum('dgop,dgtp,dgpi->dgtoi', c_mat, e_pow[:, :, :t], b_bar))
    kf, kb = k_all[0], k_all[1]
    ii = np.arange(t)[None, :, None]
    jj = np.arange(t)[None, None, :]
    tt = np.arange(t)[:, None, None]
    place_f = jnp.asarray((jj - ii == tt).astype(np.float32))
    place_b = jnp.asarray((ii - jj == tt).astype(np.float32))
    kf = kf.at[:, 0].add(jnp.eye(c, dtype=F32)[None] * d_skip.astype(F32).reshape(g, c, 1))
    m = jnp.einsum('tij,gtoc->gicjo', place_f, kf) + jnp.einsum('tij,gtoc->gicjo', place_b, kb)
    m = m.reshape(g, t * c, t * c)

    e_f = e_pow[0][:, ::-1][:, 1:]
    e_b = e_pow[1][:, :t]
    ws_f = e_f[:, :, :, None] * b_bar[0][:, None]
    ws_b = e_b[:, :, :, None] * b_bar[1][:, None]

    def rows_ic(x):
        return jnp.transpose(x, (0, 1, 3, 2)).reshape(g, t * c, p)

    w_state = jnp.concatenate([rows_ic(jnp.real(ws_f)), rows_ic(jnp.real(ws_b)),
                               rows_ic(jnp.imag(ws_f)), rows_ic(jnp.imag(ws_b))], axis=-1)

    ce_f = c_mat[0][:, None] * e_pow[0][:, 1:, None, :]
    ce_b = c_mat[1][:, None] * e_pow[1][:, ::-1][:, :t, None, :]

    def cols_jc(x):
        return jnp.transpose(x, (0, 3, 1, 2)).reshape(g, p, t * c)

    w_so = jnp.concatenate([cols_jc(jnp.real(ce_f)), cols_jc(jnp.real(ce_b)),
                            cols_jc(-jnp.imag(ce_f)), cols_jc(-jnp.imag(ce_b))], axis=1)
    w_out = jnp.concatenate([m, w_so], axis=1)

    a_t = e_pow[:, :, t]
    a_step = jnp.stack([jnp.concatenate([jnp.real(a_t[0]), jnp.real(a_t[1])], axis=-1),
                        jnp.concatenate([jnp.imag(a_t[0]), jnp.imag(a_t[1])], axis=-1)])
    return w_state.astype(BF16), w_out.astype(BF16), a_step


def _lane_perm(n_i, gb, c):
    n = n_i * gb * c
    src = np.arange(n)
    i, q, ch = src // (gb * c), (src // c) % gb, src % c
    dst = q * (n_i * c) + i * c + ch
    p = np.zeros((n, n), np.float32)
    p[src, dst] = 1.0
    return p


def _ssm_kernel(n_seq, n_chunk, gb, g_all, x_ref, ws_ref, wo_ref, a_ref, s0_ref, pin_ref, pout_ref,
                y_ref, fin_ref, xq_ref, za_ref, zb_ref):
    phase = pl.program_id(1)
    j = pl.program_id(2)
    gp = g_all + GROUP_PAD
    nck = n_seq * n_chunk
    half = LANES // 2
    hc = CHUNK // 2

    @pl.when(phase == 0)
    def _():
        halves = []
        for h in range(2):
            xcat = jnp.concatenate([x_ref[pl.ds(h * hc + i, nck, stride=CHUNK), :].astype(BF16)
                                    for i in range(hc)], axis=1)
            halves.append(_dot(xcat, pin_ref[...]).astype(BF16))
        for q in range(gb):
            xq = jnp.concatenate([hv[:, q * LANES:(q + 1) * LANES] for hv in halves], axis=1)
            xq_ref[j * gb + q] = xq
            z = _dot(xq, ws_ref[q])
            za_ref[pl.ds(j * gb + q, nck, stride=gp), :] = z[:, :LANES]
            zb_ref[pl.ds(j * gb + q, nck, stride=gp), :] = z[:, LANES:]

    @pl.when(jnp.logical_and(phase == 1, j == 0))
    def _():
        a_re = a_ref[0]
        a_im = a_ref[1]
        lane = lax.broadcasted_iota(jnp.int32, (g_all, LANES), 1)
        is_f = lane < half
        for s in range(n_seq):
            def step(k, carry):
                s_a, s_b = carry
                rf = pl.multiple_of((s * n_chunk + k) * gp, SUBLANES)
                rb = pl.multiple_of((s * n_chunk + n_chunk - 1 - k) * gp, SUBLANES)
                zf_a = za_ref[pl.ds(rf, g_all), :]
                zf_b = zb_ref[pl.ds(rf, g_all), :]
                zb_a = za_ref[pl.ds(rb, g_all), :]
                zb_b = zb_ref[pl.ds(rb, g_all), :]
                za_ref[pl.ds(rf, g_all), :] = jnp.where(is_f, s_a, zf_a)
                zb_ref[pl.ds(rf, g_all), :] = jnp.where(is_f, s_b, zf_b)
                za_ref[pl.ds(rb, g_all), :] = jnp.where(is_f, zb_a, s_a)
                zb_ref[pl.ds(rb, g_all), :] = jnp.where(is_f, zb_b, s_b)
                z_a = jnp.where(is_f, zf_a, zb_a)
                z_b = jnp.where(is_f, zf_b, zb_b)
                n_a = a_re * s_a - a_im * s_b + z_a
                n_b = a_re * s_b + a_im * s_a + z_b
                return n_a, n_b

            init = s0_ref[s]
            f_a, f_b = lax.fori_loop(0, n_chunk, step, (init[:, :LANES], init[:, LANES:]))
            fin_ref[s] = jnp.concatenate([f_a, f_b], axis=-1)

    @pl.when(phase == 1)
    def _():
        ys = []
        for q in range(gb):
            st_a = za_ref[pl.ds(j * gb + q, nck, stride=gp), :].astype(BF16)
            st_b = zb_ref[pl.ds(j * gb + q, nck, stride=gp), :].astype(BF16)
            lhs = jnp.concatenate([xq_ref[j * gb + q], st_a, st_b], axis=-1)
            ys.append(_dot(lhs, wo_ref[q]).astype(BF16))
        for h in range(2):
            ycat = jnp.concatenate([y[:, h * LANES:(h + 1) * LANES] for y in ys], axis=1)
            yall = _dot(ycat, pout_ref[...])
            for i in range(hc):
                y_ref[pl.ds(h * hc + i, nck, stride=CHUNK), :] = yall[:, i * LANES:(i + 1) * LANES]


def _ssm_call(u, row_blk0, n_rows, w_state, w_out, a_step, s0, n_seq, n_chunk, name):
    g_all, w, _ = w_state.shape
    nck = n_seq * n_chunk
    blk = nck * CHUNK
    n_blk = n_rows // blk
    gb = LANES * CHUNK // w
    ngb = g_all // gb
    gp = g_all + GROUP_PAD
    kern = functools.partial(_ssm_kernel, n_seq, n_chunk, gb, g_all)
    assert w == 2 * LANES
    perm = _lane_perm(CHUNK // 2, gb, LANES // gb)
    p_in = jnp.asarray(perm, BF16)
    p_out = jnp.asarray(perm.T, BF16)
    n_perm = perm.shape[0]
    const = lambda b, ph, j: (0, 0)
    return pl.pallas_call(
        kern,
        out_shape=(jax.ShapeDtypeStruct((n_rows, u.shape[1]), F32),
                   jax.ShapeDtypeStruct((n_blk * n_seq, g_all, w), F32)),
        grid=(n_blk, 2, ngb),
        in_specs=[
            pl.BlockSpec((blk, LANES), lambda b, ph, j: (row_blk0 + b, jnp.where(ph == 0, j, ngb - 1))),
            pl.BlockSpec((gb, w, w), lambda b, ph, j: (jnp.where(ph == 0, j, ngb - 1), 0, 0)),
            pl.BlockSpec((gb, 2 * w, w), lambda b, ph, j: (jnp.where(ph == 0, 0, j), 0, 0)),
            pl.BlockSpec((2, g_all, LANES), lambda b, ph, j: (0, 0, 0)),
            pl.BlockSpec((n_seq, g_all, w), lambda b, ph, j: (b, 0, 0)),
            pl.BlockSpec((n_perm, n_perm), const, pipeline_mode=pl.Buffered(1)),
            pl.BlockSpec((n_perm, n_perm), const, pipeline_mode=pl.Buffered(1)),
        ],
        out_specs=(
            pl.BlockSpec((blk, LANES), lambda b, ph, j: (b, jnp.where(ph == 0, 0, j))),
            pl.BlockSpec((n_seq, g_all, w), lambda b, ph, j: (b, 0, 0)),
        ),
        scratch_shapes=[pltpu.VMEM((g_all, nck, w), BF16),
                        pltpu.VMEM((nck * gp, LANES), F32), pltpu.VMEM((nck * gp, LANES), F32)],
        compiler_params=_cp(("arbitrary", "arbitrary", "arbitrary")),
        name=name,
    )(u, w_state, w_out, a_step, s0, p_in, p_out)


def _glu_kernel(nct, yc_ref, yl_ref, wg_ref, bg_ref, wb_ref, o_ref):
    i = pl.program_id(0)
    y = jax.nn.gelu(jnp.where(i < nct, yc_ref[...], yl_ref[...]))
    z = _dot(y.astype(BF16), wg_ref[...]) + bg_ref[...]
    ya = y * jax.nn.sigmoid(z)
    o_ref[...] = _dot(ya.astype(BF16), wb_ref[...]).astype(o_ref.dtype)


def _glu_branch(y_ctx, y_lat, w_glu, b_glu, w_ba, tok):
    c = y_ctx.shape[1]
    d = w_ba.shape[1]
    tm = tok.tm
    return pl.pallas_call(
        functools.partial(_glu_kernel, tok.nct),
        out_shape=jax.ShapeDtypeStruct((y_ctx.shape[0] + y_lat.shape[0], d), BF16),
        grid=(tok.n,),
        in_specs=[pl.BlockSpec((tm, c), lambda i: (tok.ctx_idx(i), 0)),
                  pl.BlockSpec((tm, c), lambda i: (tok.lat_idx(i), 0)),
                  pl.BlockSpec((c, c), lambda i: (0, 0)),
                  pl.BlockSpec((1, c), lambda i: (0, 0)),
                  pl.BlockSpec((c, d), lambda i: (0, 0))],
        out_specs=pl.BlockSpec((tm, d), lambda i: (i, 0)),
        compiler_params=_cp(("parallel",)),
        name="glu_branch",
    )(y_ctx, y_lat, w_glu, b_glu, w_ba)


def _short_conv_kernel(nct, per_ctx, per_lat, v_ref, x0_ref, x1_ref, vp_ref, x0p_ref, x1p_ref,
                       vn_ref, x0n_ref, x1n_ref, w_ref, b_ref, vx_ref, x0c_ref, scr):
    i = pl.program_id(0)
    tm = v_ref.shape[0]
    k = jnp.where(i < nct, i % per_ctx, (i - nct) % per_lat)
    per = jnp.where(i < nct, per_ctx, per_lat)
    first = k == 0
    last = k == per - 1
    row = lax.broadcasted_iota(jnp.int32, (tm, 1), 0)

    def conv(cur_ref, prev_ref, next_ref, part):
        x = cur_ref[...].astype(F32)
        hp = jnp.where(first, 0.0, prev_ref[SUBLANES - 1:SUBLANES, :].astype(F32))
        hn = jnp.where(last, 0.0, next_ref[0:1, :].astype(F32))
        xm = jnp.where(row == 0, hp, pltpu.roll(x, 1, axis=0))
        xq = jnp.where(row == tm - 1, hn, pltpu.roll(x, tm - 1, axis=0))
        w = w_ref[part]
        return b_ref[part] + xm * w[0:1, :] + x * w[1:2, :] + xq * w[2:3, :]

    v = conv(v_ref, vp_ref, vn_ref, 0)
    x0 = conv(x0_ref, x0p_ref, x0n_ref, 1)
    x1 = conv(x1_ref, x1p_ref, x1n_ref, 2)
    vx_ref[...] = _to_pair_rows(v * x1, scr).astype(vx_ref.dtype)
    x0c_ref[...] = _to_pair_rows(x0, scr).astype(x0c_ref.dtype)


def _short_conv(proj, col_blk0, hw, w_short3, b_short3, n_ctx, seq, dseq, tm):
    t = proj.shape[0]
    assert seq % tm == 0 and dseq % tm == 0
    nt = t // tm
    r8 = tm // SUBLANES
    nb8 = t // SUBLANES

    def cur(k):
        return pl.BlockSpec((tm, hw), lambda i: (i, col_blk0 + k))

    def prev(k):
        return pl.BlockSpec((SUBLANES, hw), lambda i: (jnp.maximum(i * r8 - 1, 0), col_blk0 + k))

    def nxt(k):
        return pl.BlockSpec((SUBLANES, hw), lambda i: (jnp.minimum((i + 1) * r8, nb8 - 1), col_blk0 + k))

    kern = functools.partial(_short_conv_kernel, n_ctx // tm, seq // tm, dseq // tm)
    return pl.pallas_call(
        kern,
        out_shape=(jax.ShapeDtypeStruct((t // 2, 2 * hw), BF16), jax.ShapeDtypeStruct((t // 2, 2 * hw), BF16)),
        grid=(nt,),
        in_specs=[cur(0), cur(1), cur(2), prev(0), prev(1), prev(2), nxt(0), nxt(1), nxt(2),
                  pl.BlockSpec((3, 3, hw), lambda i: (0, 0, 0)),
                  pl.BlockSpec((3, 1, hw), lambda i: (0, 0, 0))],
        out_specs=(pl.BlockSpec((tm // 2, 2 * hw), lambda i: (i, 0)),
                   pl.BlockSpec((tm // 2, 2 * hw), lambda i: (i, 0))),
        scratch_shapes=[pltpu.VMEM((hw // LANES * tm, LANES), F32)],
        compiler_params=_cp(("parallel",)),
        name="short_conv",
    )(proj, proj, proj, proj, proj, proj, proj, proj, proj, w_short3, b_short3)


def _filt_gen_kernel(seq_len, n_bands, wt_ref, wc_ref, ws_ref, b1_ref, f1_ref, w2_ref, b2_ref, f2_ref,
                     w3_ref, b3_ref, rate_ref, a_ref, bm_ref, abs_ref, scr):
    i = pl.program_id(0)
    tl = 2 * a_ref.shape[0]
    hw = a_ref.shape[1] // 2
    pos = lax.broadcasted_iota(jnp.int32, (tl, 1), 0) + i * tl
    t = pos.astype(F32) / seq_len
    bands = (lax.broadcasted_iota(jnp.int32, (1, wc_ref.shape[0]), 1) + 1).astype(F32)
    ang = (2.0 * math.pi) * t * bands
    pre = t * wt_ref[...] + _dot3(jnp.cos(ang), wc_ref[...]) + _dot3(jnp.sin(ang), ws_ref[...])
    h = jnp.sin(f1_ref[...] * (pre + b1_ref[...]))
    h = jnp.sin(f2_ref[...] * (_dot3(h, w2_ref[...]) + b2_ref[...]))
    h = _dot3(h, w3_ref[...]) + b3_ref[...]
    h = h * jnp.exp(-t * rate_ref[...])
    h_f = h[:, :hw]
    h_b = h[:, hw:]
    colabs = jnp.sum(jnp.abs(h_f) + jnp.abs(h_b), axis=0, keepdims=True)
    h_bp = jnp.where(pos == 0, 0.0, h_b)
    a_ref[...] = _to_pair_rows(h_f + h_bp, scr).astype(a_ref.dtype)
    bm_ref[...] = _to_pair_rows(h_bp - h_f, scr).astype(bm_ref.dtype)

    @pl.when(i == 0)
    def _():
        abs_ref[...] = colabs

    @pl.when(i > 0)
    def _():
        abs_ref[...] += colabs


def _filt_gen(seq_len, w_f1, b_f1, freq1, w_f2, b_f2, freq2, w_f3, b_f3, rates2):
    n_emb, hid_raw = w_f1.shape
    n_bands = (n_emb - 1) // 2
    hw2 = w_f3.shape[1]
    hw = hw2 // 2
    tl = min(seq_len, 512)
    hid = LANES
    assert n_bands <= LANES and hid_raw <= LANES

    def pad(x, rows, cols):
        return jnp.zeros((rows, cols), F32).at[:x.shape[0], :x.shape[1]].set(x.astype(F32))

    wt = pad(w_f1[0:1], 1, hid)
    wc = pad(w_f1[1:1 + n_bands], LANES, hid)
    ws = pad(w_f1[1 + n_bands:], LANES, hid)
    b_f1, freq1, b_f2, freq2 = (pad(x, 1, hid) for x in (b_f1, freq1, b_f2, freq2))
    w_f2 = pad(w_f2, hid, hid)
    w_f3 = pad(w_f3, hid, hw2)
    full = lambda shape: pl.BlockSpec(shape, lambda i: (0,) * len(shape))
    kern = functools.partial(_filt_gen_kernel, seq_len, n_bands)
    return pl.pallas_call(
        kern,
        out_shape=(jax.ShapeDtypeStruct((seq_len // 2, hw2), BF16), jax.ShapeDtypeStruct((seq_len // 2, hw2), BF16),
                   jax.ShapeDtypeStruct((1, hw), F32)),
        grid=(seq_len // tl,),
        in_specs=[full((1, hid)), full((LANES, hid)), full((LANES, hid)), full((1, hid)), full((1, hid)),
                  full((hid, hid)), full((1, hid)), full((1, hid)), full((hid, hw2)), full((1, hw2)),
                  full((1, hw2))],
        out_specs=(pl.BlockSpec((tl // 2, hw2), lambda i: (i, 0)), pl.BlockSpec((tl // 2, hw2), lambda i: (i, 0)),
                   full((1, hw))),
        scratch_shapes=[pltpu.VMEM((hw // LANES * tl, LANES), F32)],
        compiler_params=_cp(("arbitrary",)),
        name="filt_gen",
    )(wt, wc, ws, b_f1, freq1, w_f2, b_f2, freq2, w_f3, b_f3, rates2)


def _trig_tables(row_hi, row_lo, col, period):
    col = col[None, :]
    ang_lo = ((row_lo[:, None] * col) % period).astype(F32) * (2.0 * math.pi / period)
    ang_hi = ((row_hi[:, None] * col) % period).astype(F32) * (2.0 * math.pi / period)
    c_lo, s_lo = jnp.cos(ang_lo)[None], jnp.sin(ang_lo)[None]
    c_hi, s_hi = jnp.cos(ang_hi)[:, None], jnp.sin(ang_hi)[:, None]
    n = row_hi.shape[0] * row_lo.shape[0]
    ctab = (c_hi * c_lo - s_hi * s_lo).reshape(n, col.shape[1])
    stab = (s_hi * c_lo + c_hi * s_lo).reshape(n, col.shape[1])
    return ctab.astype(BF16), stab.astype(BF16)


def _dft_tables(seq_len):
    m = seq_len // 2
    r = 1
    while r * r < m:
        r *= 2
    assert m % r == 0
    i32 = jnp.int32
    idx = jnp.arange(m, dtype=i32)
    hi = jnp.arange(m // r, dtype=i32) * r
    lo = jnp.arange(r, dtype=i32)
    ce, se = _trig_tables(hi, lo, idx, 2 * m)
    co, so = _trig_tables(hi, lo, 2 * idx + 1, 2 * seq_len)
    cot, sot = _trig_tables(2 * hi, 2 * lo + 1, idx, 2 * seq_len)
    return ce, se, co, so, cot, sot


def _alt_row(n):
    return (1 - 2 * (lax.broadcasted_iota(jnp.int32, (SUBLANES, n), 1) % 2)).astype(BF16)


def _filt_dft_kernel(seq_len, ce_ref, se_ref, co_ref, so_ref, ae_ref, ao_ref, be_ref, bo_ref, abs_ref,
                     krl_ref, krh_ref, kil_ref, kih_ref, kmid_ref):
    i = pl.program_id(0)
    scale = (1.0 / seq_len) / (abs_ref[...] + EPS)
    ec = _dot(ce_ref[...], ae_ref[...])
    oc = _dot(co_ref[...], ao_ref[...])
    es = _dot(se_ref[...], be_ref[...])
    os_ = _dot(so_ref[...], bo_ref[...])
    krl_ref[...] = (ec + oc) * scale
    krh_ref[...] = (ec - oc) * scale
    kil_ref[...] = (es + os_) * scale
    kih_ref[...] = (os_ - es) * scale

    @pl.when(i == 0)
    def _():
        sgn = _alt_row(ae_ref.shape[0])
        kmid_ref[0:1, :] = _dot(sgn, ae_ref[...])[0:1] * scale
        kmid_ref[1:2, :] = _dot(sgn, bo_ref[...])[0:1] * scale


def _filt_dft(tabs, a2, bm2, colabs, seq_len):
    m = seq_len // 2
    hw = a2.shape[1] // 2
    ce, se, co, so = tabs[:4]
    tf = min(m, 256)
    kern = functools.partial(_filt_dft_kernel, seq_len)
    mode = pl.Buffered(1) if m * hw * 2 > BIG_BLOCK_BYTES // 2 else None
    tab = pl.BlockSpec((tf, m), lambda i: (i, 0))
    res = lambda c: pl.BlockSpec((m, hw), lambda i: (0, c), pipeline_mode=mode)
    out = pl.BlockSpec((tf, hw), lambda i: (i, 0))
    sds = jax.ShapeDtypeStruct((m, hw), F32)
    return pl.pallas_call(
        kern,
        out_shape=(sds, sds, sds, sds, jax.ShapeDtypeStruct((2, hw), F32)),
        grid=(m // tf,),
        in_specs=[tab, tab, tab, tab, res(0), res(1), res(0), res(1), pl.BlockSpec((1, hw), lambda i: (0, 0))],
        out_specs=(out, out, out, out, pl.BlockSpec((2, hw), lambda i: (0, 0))),
        compiler_params=_cp(("arbitrary",)),
        name="filt_dft",
    )(ce, se, co, so, a2, a2, bm2, bm2, colabs)


def _hy_fwd_kernel(ce_ref, se_ref, co_ref, so_ref, ve_ref, vo_ref, krl_ref, krh_ref, kil_ref, kih_ref, kmid_ref,
                   ae_ref, be_ref, ao_ref, bo_ref, ymid_ref):
    j = pl.program_id(1)
    tf, m = ce_ref.shape
    nb = ae_ref.shape[0]
    krl, krh, kil, kih = krl_ref[...], krh_ref[...], kil_ref[...], kih_ref[...]
    row = lax.broadcasted_iota(jnp.int32, (tf, 1), 0) + j * tf
    dc = jnp.where(row == 0, 0.5, 1.0)
    for s in range(nb):
        ve = ve_ref[s * m:(s + 1) * m, :]
        vo = vo_ref[s * m:(s + 1) * m, :]
        ec = _dot(ce_ref[...], ve)
        oc = _dot(co_ref[...], vo)
        es = _dot(se_ref[...], ve)
        os_ = _dot(so_ref[...], vo)
        p_lo, p_hi = ec + oc, ec - oc
        q_lo, q_hi = es + os_, os_ - es
        yre_lo = (p_lo * krl + q_lo * kil) * dc
        yim_lo = p_lo * kil - q_lo * krl
        yre_hi = (p_hi * krh + q_hi * kih) * dc
        yim_hi = p_hi * kih - q_hi * krh
        ae_ref[s] = (yre_lo + yre_hi).astype(ae_ref.dtype)
        be_ref[s] = (yim_lo - yim_hi).astype(be_ref.dtype)
        ao_ref[s] = (yre_lo - yre_hi).astype(ao_ref.dtype)
        bo_ref[s] = (yim_lo + yim_hi).astype(bo_ref.dtype)

    @pl.when(j == 0)
    def _():
        sgn = _alt_row(m)
        kr_m, ki_m = kmid_ref[0:1, :], kmid_ref[1:2, :]
        for s in range(nb):
            p_m = _dot(sgn, ve_ref[s * m:(s + 1) * m, :])[0:1]
            q_m = _dot(sgn, vo_ref[s * m:(s + 1) * m, :])[0:1]
            ymid_ref[s, 0:1, :] = p_m * kr_m + q_m * ki_m
            ymid_ref[s, 1:2, :] = p_m * ki_m - q_m * kr_m


BIG_BLOCK_BYTES = 4 * 1024 * 1024


def _seq_batch(seq_len, n_seq):
    nb = max(1, min(n_seq, 4 * MXU_SIDE // seq_len))
    while n_seq % nb:
        nb -= 1
    return nb


def _hy_fwd(tabs, vx2, kspec, n_seq, seq_len, row0_blk):
    m = seq_len // 2
    hw = vx2.shape[1] // 2
    ce, se, co, so = tabs[:4]
    krl, krh, kil, kih, kmid = kspec
    tf = min(m, 256)
    nb = _seq_batch(seq_len, n_seq)
    assert row0_blk % nb == 0
    mode = pl.Buffered(1) if nb * m * hw * 2 > BIG_BLOCK_BYTES // 2 else None
    tab = pl.BlockSpec((tf, m), lambda b, j: (j, 0))
    vspec = lambda c: pl.BlockSpec((nb * m, hw), lambda b, j: (row0_blk // nb + b, c), pipeline_mode=mode)
    kt = pl.BlockSpec((tf, hw), lambda b, j: (j, 0))
    out = pl.BlockSpec((nb, tf, hw), lambda b, j: (b, j, 0))
    sds = jax.ShapeDtypeStruct((n_seq, m, hw), BF16)
    return pl.pallas_call(
        _hy_fwd_kernel,
        out_shape=(sds, sds, sds, sds, jax.ShapeDtypeStruct((n_seq, 2, hw), F32)),
        grid=(n_seq // nb, m // tf),
        in_specs=[tab, tab, tab, tab, vspec(0), vspec(1), kt, kt, kt, kt,
                  pl.BlockSpec((2, hw), lambda b, j: (0, 0))],
        out_specs=(out, out, out, out, pl.BlockSpec((nb, 2, hw), lambda b, j: (b, 0, 0))),
        compiler_params=_cp(("arbitrary", "arbitrary")),
        name="hy_fwd",
    )(ce, se, co, so, vx2, vx2, krl, krh, kil, kih, kmid)


def _hy_inv_kernel(ce_ref, se_ref, cot_ref, sot_ref, ae_ref, be_ref, ao_ref, bo_ref, ymid_ref,
                   vx_ref, x0_ref, fb_ref, wb_ref, o_ref, scr):
    j = pl.program_id(1)
    tt = ce_ref.shape[0]
    nb = ae_ref.shape[0]
    hw = fb_ref.shape[1]
    row = lax.broadcasted_iota(jnp.int32, (tt, 1), 0) + j * tt
    sgn = (1 - 2 * (row % 2)).astype(F32)
    for s in range(nb):
        rows = slice(s * tt, (s + 1) * tt)
        ymid = ymid_ref[s]
        z_e = _dot(ce_ref[...], ae_ref[s]) - _dot(se_ref[...], be_ref[s]) + sgn * ymid[0:1]
        z_o = _dot(cot_ref[...], ao_ref[s]) - _dot(sot_ref[...], bo_ref[s]) - sgn * ymid[1:2]
        proj = []
        for par, z in ((0, z_e), (1, z_o)):
            cols = slice(par * hw, (par + 1) * hw)
            z = z + vx_ref[rows, cols].astype(F32) * fb_ref[...]
            yb = x0_ref[rows, cols].astype(F32) * z
            proj.append(_dot(yb.astype(BF16), wb_ref[...]))
        o_ref[2 * s * tt:2 * (s + 1) * tt, :] = _from_pair_rows(
            jnp.concatenate(proj, axis=1), scr).astype(o_ref.dtype)


def _hy_inv(tabs, spec, vx2, x0c2, fbias, w_bb, n_seq, seq_len, row0_blk):
    m = seq_len // 2
    hw = fbias.shape[1]
    d = w_bb.shape[1]
    ce, se, _, _, cot, sot = tabs
    ae, be, ao, bo, ymid = spec
    tt = min(m, 128)
    per = m // tt
    nb = _seq_batch(seq_len, n_seq) if per == 1 else 1
    assert row0_blk % nb == 0
    mode = pl.Buffered(1) if nb * m * hw * 2 > BIG_BLOCK_BYTES // 2 else None
    tab = pl.BlockSpec((tt, m), lambda b, j: (j, 0))
    res = pl.BlockSpec((nb, m, hw), lambda b, j: (b, 0, 0), pipeline_mode=mode)
    rows_in = pl.BlockSpec((nb * tt, 2 * hw), lambda b, j: ((row0_blk // nb + b) * per + j, 0))
    return pl.pallas_call(
        _hy_inv_kernel,
        out_shape=jax.ShapeDtypeStruct((n_seq * seq_len, d), BF16),
        grid=(n_seq // nb, per),
        in_specs=[tab, tab, tab, tab, res, res, res, res,
                  pl.BlockSpec((nb, 2, hw), lambda b, j: (b, 0, 0)),
                  rows_in, rows_in,
                  pl.BlockSpec((1, hw), lambda b, j: (0, 0)),
                  pl.BlockSpec((hw, d), lambda b, j: (0, 0), pipeline_mode=pl.Buffered(1))],
        out_specs=pl.BlockSpec((nb * 2 * tt, d), lambda b, j: (b * per + j, 0)),
        scratch_shapes=[pltpu.VMEM((d // LANES * 2 * tt, LANES), F32)],
        compiler_params=_cp(("arbitrary", "arbitrary")),
        name="hy_inv",
    )(ce, se, cot, sot, ae, be, ao, bo, ymid, vx2, x0c2, fbias, w_bb)


MERGE_COL_BLOCKS = 4

def _merge_kernel(tok, n_groups, xp_ref, xs_ref, pos_ref, g0_ref, g1_ref, pa_ref, pbc_ref, pbs_ref,
                  gate1_ref, sh2_ref, sc2_ref, gn_ref, wo_ref, wr_ref, br_ref,
                  x2_ref, h2_ref, rt_ref):
    i = pl.program_id(0)
    is_ctx = i < tok.nct
    tm, d = x2_ref.shape
    pb = jnp.where(is_ctx, pbc_ref[...], pbs_ref[...])
    merged = g0_ref[...] * pa_ref[...] + g1_ref[...] * pb
    cb = d // MERGE_COL_BLOCKS
    ssq = jnp.zeros((tm, 1), F32)
    for c in range(MERGE_COL_BLOCKS):
        cols = slice(c * cb, (c + 1) * cb)
        att = _dot(merged, wo_ref[:, cols])
        x = jnp.where(is_ctx, xp_ref[:, cols], xs_ref[:, cols] + pos_ref[:, cols])
        x2c = x + gate1_ref[0][:, cols] * att
        x2_ref[:, cols] = x2c
        ssq = ssq + jnp.sum(x2c * x2c, axis=-1, keepdims=True)
    x2 = x2_ref[...]
    h2 = x2 * lax.rsqrt(ssq * (1.0 / d) + EPS) * gn_ref[...] * (1.0 + sc2_ref[0]) + sh2_ref[0]
    h2b = h2.astype(BF16)
    _store_row_tiles(h2_ref, 0, h2b)
    logits = _dot(h2b, wr_ref[...]) + br_ref[...]
    lane = lax.broadcasted_iota(jnp.int32, logits.shape, 1)
    gl = jnp.where(lane < n_groups, logits, -jnp.inf)
    gmax = jnp.max(gl, axis=-1, keepdims=True)
    gidx = jnp.min(jnp.where(gl == gmax, lane, n_groups), axis=-1, keepdims=True)
    rt_ref[...] = jnp.where(lane == LANES - 1, gidx.astype(F32), logits)


def _merge_out(xp, xs, pos, proj, gate_blk0, pa, pb_ctx, pb_lat, mod, g2, w_out, w_r, b_r, n_groups, tok):
    d = xp.shape[1]
    tm = tok.tm
    t_all = xp.shape[0] + xs.shape[0]
    kern = functools.partial(_merge_kernel, tok, n_groups)
    one = pl.Buffered(1)
    ns = d // 2 // LANES

    def modspec(k):
        return pl.BlockSpec((1, 1, d), lambda i: (tok.cond(i) * N_MOD + k, 0, 0))

    return pl.pallas_call(
        kern,
        out_shape=(jax.ShapeDtypeStruct((t_all, d), F32), jax.ShapeDtypeStruct((t_all * ns, LANES), jnp.uint32),
                   jax.ShapeDtypeStruct((t_all, LANES), F32)),
        grid=(tok.n,),
        in_specs=[
            pl.BlockSpec((tm, d), lambda i: (tok.ctx_idx(i), 0)),
            pl.BlockSpec((tm, d), lambda i: (tok.lat_idx(i), 0)),
            pl.BlockSpec((tm, d), lambda i: (tok.pos_idx(i), 0)),
            pl.BlockSpec((tm, d), lambda i: (i, gate_blk0)),
            pl.BlockSpec((tm, d), lambda i: (i, gate_blk0 + 1)),
            pl.BlockSpec((tm, d), lambda i: (i, 0)),
            pl.BlockSpec((tm, d), lambda i: (tok.ctx_idx(i), 0)),
            pl.BlockSpec((tm, d), lambda i: (tok.lat_idx(i), 0)),
            modspec(2), modspec(3), modspec(4),
            pl.BlockSpec((1, d), lambda i: (0, 0)),
            pl.BlockSpec((d, d), lambda i: (0, 0), pipeline_mode=one),
            pl.BlockSpec((d, LANES), lambda i: (0, 0)),
            pl.BlockSpec((1, LANES), lambda i: (0, 0)),
        ],
        out_specs=(pl.BlockSpec((tm, d), lambda i: (i, 0)), pl.BlockSpec((tm * ns, LANES), lambda i: (i, 0)),
                   pl.BlockSpec((tm, LANES), lambda i: (i, 0))),
        compiler_params=_cp(("parallel",)),
        name="merge_out",
    )(xp, xs, pos, proj, proj, pa, pb_ctx, pb_lat, mod, mod, mod, g2, w_out, w_r, b_r)


ROW_UNROLL = 8

def _moe_kernel(n_tiles, n_groups, epg, tg_ref, tnv_ref, rt_ref,
                h2_hbm, wr_ref, br_ref, wg_ref, wu_ref, wd_ref, out_hbm,
                xbuf, xb_ref, acc, obuf, gws, gsem, ssem):
    i = pl.program_id(0)
    j = pl.program_id(1)
    tm = xb_ref.shape[0]
    ns = h2_hbm.shape[1]
    nv = tnv_ref[i]
    slot = i % 2

    def buf_rows(buf, slot_, r, rows):
        return buf.at[pl.ds(pl.multiple_of((slot_ * tm + r) * ns, ns), rows * ns), :]

    def gather_copy(tok_row, slot_, r):
        return pltpu.make_async_copy(h2_hbm.at[tok_row], buf_rows(xbuf, slot_, r, 1), gsem.at[slot_])

    def gather_wait(slot_, rows):
        dst = buf_rows(xbuf, slot_, 0, rows)
        pltpu.make_async_copy(dst, dst, gsem.at[slot_]).wait()

    def scatter_copy(tok_row, slot_, r):
        return pltpu.make_async_copy(buf_rows(obuf, slot_, r, 1), out_hbm.at[tok_row], ssem.at[0])

    def scatter_wait(slot_, rows):
        src = buf_rows(obuf, slot_, 0, rows)
        pltpu.make_async_copy(src, src, ssem.at[0]).wait()

    def for_rows(count, fn):
        n_full = count // ROW_UNROLL

        def body(b, c):
            for k in range(ROW_UNROLL):
                fn(b * ROW_UNROLL + k)
            return c

        def tail(r, c):
            fn(r)
            return c

        lax.fori_loop(0, n_full, body, 0)
        lax.fori_loop(n_full * ROW_UNROLL, count, tail, 0)

    def wait_scatter(count, slot_):
        p = tm
        while p >= 1:
            @pl.when((count & p) != 0)
            def _(p=p):
                scatter_wait(slot_, p)
            p //= 2

    nv_prev = tnv_ref[jnp.maximum(i - 1, 0)]
    nv_next = tnv_ref[jnp.minimum(i + 1, n_tiles - 1)]
    prev_deferred = jnp.logical_and(i > 0, jnp.logical_and(nv_prev == tm, nv > 0))
    deferred = jnp.logical_and(nv == tm, nv_next > 0)

    @pl.when(j == 0)
    def _():
        @pl.when(i == 0)
        def _():
            for_rows(tm, lambda r: gather_copy(rt_ref[r], 0, r).start())

        @pl.when(jnp.logical_or(i == 0, nv_prev > 0))
        def _():
            gather_wait(slot, tm)

    @pl.when(jnp.logical_and(j == 0, nv > 0))
    def _():
        grp = tg_ref[i]
        xb = _load_row_tiles(xbuf, slot * tm * ns, tm, ns)
        xb_ref[...] = xb
        logits = _dot(xb, wr_ref[...]) + br_ref[...]
        lane = lax.broadcasted_iota(jnp.int32, logits.shape, 1)
        gl = jnp.where(lane < n_groups, logits, -jnp.inf)
        gmax = jnp.max(gl, axis=-1, keepdims=True)
        ge = jnp.exp(gl - gmax)
        g_w = jnp.sum(jnp.where(lane == grp, ge, 0.0), axis=-1, keepdims=True) / jnp.sum(ge, axis=-1, keepdims=True)
        base = n_groups + grp * epg
        e = [jnp.sum(jnp.where(lane == base + k, logits, 0.0), axis=-1, keepdims=True) for k in range(epg)]
        sel = []
        for k in range(epg):
            rank = jnp.zeros_like(e[k])
            for m in range(epg):
                if m == k:
                    continue
                ahead = (e[m] > e[k]) if m > k else (e[m] >= e[k])
                rank = rank + ahead.astype(F32)
            sel.append(rank < TOP_K_INNER)
        emax = functools.reduce(jnp.maximum, e)
        ex = [jnp.where(sel[k], jnp.exp(e[k] - emax), 0.0) for k in range(epg)]
        den = functools.reduce(lambda a, b: a + b, ex)
        for k in range(epg):
            gws[k] = ex[k] / den * g_w
        acc[...] = jnp.zeros(acc.shape, acc.dtype)

    def expert_step(scatter_prev):
        per = tm // epg
        for k in range(per):
            r = j * per + k
            if scatter_prev:
                scatter_copy(rt_ref[(i - 1) * tm + r], 1 - slot, r).start()
            gather_copy(rt_ref[(i + 1) * tm + r], 1 - slot, r).start()
        xb = xb_ref[...]
        g = _dot(xb, wg_ref[...])
        u = _dot(xb, wu_ref[...])
        hid = (g * jax.nn.sigmoid(g)) * u * gws[j]
        acc[...] += _dot(hid.astype(BF16), wd_ref[...])

    @pl.when(jnp.logical_and(nv > 0, prev_deferred))
    def _():
        expert_step(True)

    @pl.when(jnp.logical_and(nv > 0, jnp.logical_not(prev_deferred)))
    def _():
        expert_step(False)

    @pl.when(j == epg - 1)
    def _():
        @pl.when(prev_deferred)
        def _():
            scatter_wait(1 - slot, tm)

        @pl.when(nv > 0)
        def _():
            _store_row_tiles(obuf, slot * tm * ns, acc[...].astype(BF16))

        @pl.when(jnp.logical_and(nv > 0, jnp.logical_not(deferred)))
        def _():
            for_rows(nv, lambda r: scatter_copy(rt_ref[i * tm + r], slot, r).start())
            wait_scatter(nv, slot)


def _moe(h2p, w_r, b_r, w_g, w_u, w_d, tile_grp, tile_nv, row_tok, n_groups, epg, tm):
    t, ns, _ = h2p.shape
    d = 2 * ns * LANES
    ff = w_g.shape[-1]
    n_tiles = tile_grp.shape[0]
    assert tm & (tm - 1) == 0, "row-count waits decompose tm in binary"
    kern = functools.partial(_moe_kernel, n_tiles, n_groups, epg)

    def widx(i, j, tg, tnv, rt):
        return tg[i] * epg + jnp.where(tnv[i] > 0, j, epg - 1)

    grid_spec = pltpu.PrefetchScalarGridSpec(
        num_scalar_prefetch=3,
        grid=(n_tiles, epg),
        in_specs=[
            pl.BlockSpec(memory_space=pl.ANY),
            pl.BlockSpec((d, LANES), lambda i, j, tg, tnv, rt: (0, 0)),
            pl.BlockSpec((1, LANES), lambda i, j, tg, tnv, rt: (0, 0)),
            pl.BlockSpec((None, d, ff), lambda i, j, tg, tnv, rt: (widx(i, j, tg, tnv, rt), 0, 0)),
            pl.BlockSpec((None, d, ff), lambda i, j, tg, tnv, rt: (widx(i, j, tg, tnv, rt), 0, 0)),
            pl.BlockSpec((None, ff, d), lambda i, j, tg, tnv, rt: (widx(i, j, tg, tnv, rt), 0, 0)),
        ],
        out_specs=pl.BlockSpec(memory_space=pl.ANY),
        scratch_shapes=[
            pltpu.VMEM((2 * tm * ns, LANES), jnp.uint32),
            pltpu.VMEM((tm, d), BF16),
            pltpu.VMEM((tm, d), F32),
            pltpu.VMEM((2 * tm * ns, LANES), jnp.uint32),
            pltpu.VMEM((epg, tm, 1), F32),
            pltpu.SemaphoreType.DMA((2,)),
            pltpu.SemaphoreType.DMA((1,)),
        ],
    )
    return pl.pallas_call(
        kern,
        out_shape=jax.ShapeDtypeStruct((t, ns, LANES), jnp.uint32),
        grid_spec=grid_spec,
        compiler_params=_cp(("arbitrary", "arbitrary")),
        name="moe",
    )(tile_grp, tile_nv, row_tok, h2p, w_r, b_r, w_g, w_u, w_d)


def _moe_plan(grp, n_groups, tm, n_tiles):
    t = grp.shape[0]
    onehot = (grp[:, None] == jnp.arange(n_groups, dtype=jnp.int32)[None, :]).astype(jnp.int32)
    csum = jnp.cumsum(onehot, axis=0)
    counts = csum[-1]
    rank = jnp.take_along_axis(csum, grp[:, None], axis=1)[:, 0] - 1
    ntile_g = (counts + tm - 1) // tm
    tile_end = jnp.cumsum(ntile_g)
    tile_off = tile_end - ntile_g
    slot = tile_off[grp] * tm + rank
    row_tok = jnp.zeros((n_tiles * tm,), jnp.int32).at[slot].set(jnp.arange(t, dtype=jnp.int32))
    tiles = jnp.arange(n_tiles, dtype=jnp.int32)
    tile_grp = jnp.minimum(jnp.sum((tiles[:, None] >= tile_end[None, :]).astype(jnp.int32), axis=1),
                           n_groups - 1)
    tile_nv = jnp.clip(counts[tile_grp] - (tiles - tile_off[tile_grp]) * tm, 0, tm)
    tile_nv = jnp.where(tiles < tile_end[-1], tile_nv, 0)
    return tile_grp.astype(jnp.int32), tile_nv.astype(jnp.int32), row_tok


def _final_kernel(x2_ref, m_ref, gate2_ref, gf_ref, o_ref):
    tm, d = x2_ref.shape
    ns = d // 2 // LANES
    m = _load_row_tiles(m_ref, 0, tm, ns).astype(F32)
    x3 = x2_ref[...] + gate2_ref[0] * m
    o_ref[...] = _rms(x3, gf_ref[...])


def _final(x2, moe, mod, g_final, row_blk0, n_rows, cond_of_tile, tm):
    d = x2.shape[1]
    ns = d // 2 // LANES
    return pl.pallas_call(
        _final_kernel,
        out_shape=jax.ShapeDtypeStruct((n_rows, d), F32),
        grid=(n_rows // tm,),
        in_specs=[pl.BlockSpec((tm, d), lambda i: (row_blk0 + i, 0)),
                  pl.BlockSpec((tm * ns, LANES), lambda i: (row_blk0 + i, 0)),
                  pl.BlockSpec((1, 1, d), lambda i: (cond_of_tile(i) * N_MOD + 5, 0, 0)),
                  pl.BlockSpec((1, d), lambda i: (0, 0))],
        out_specs=pl.BlockSpec((tm, d), lambda i: (i, 0)),
        compiler_params=_cp(("parallel",)),
        name="final",
    )(x2, moe, mod, g_final)


def _grid_pos_embed(n_tokens, dim):
    rows_n = n_tokens // GRID_W
    quarter = dim // 4
    omega = 1.0 / (10000.0 ** (jnp.arange(quarter, dtype=F32) / quarter))

    def emb(n):
        ang = jnp.arange(n).astype(F32)[:, None] * omega[None, :]
        return jnp.concatenate([jnp.sin(ang), jnp.cos(ang)], axis=-1)

    e_r = jnp.broadcast_to(emb(rows_n)[:, None, :], (rows_n, GRID_W, dim // 2))
    e_c = jnp.broadcast_to(emb(GRID_W)[None, :, :], (rows_n, GRID_W, dim // 2))
    return jnp.concatenate([e_r, e_c], axis=-1).reshape(n_tokens, dim)


def _state_to_lanes(s):
    b, _, _, g, p = s.shape
    return jnp.transpose(s, (0, 3, 2, 1, 4)).reshape(b, g, 4 * p)


def _lanes_to_state(x, p):
    b, g, _ = x.shape
    return jnp.transpose(x.reshape(b, g, 2, 2, p), (0, 3, 2, 1, 4))


def kernel(x_prompt, x_sample, state_ssm, c, c_ctx, w_ada, b_ada, g_norm1, g_norm2, w_in, ssm_a_re, ssm_a_im, ssm_log_dt, ssm_b_re, ssm_b_im, ssm_c_re, ssm_c_im, ssm_d, w_glu, b_glu, w_short, b_short, w_f1, b_f1, freq1, w_f2, b_f2, freq2, w_f3, b_f3, filter_bias, w_branch_a, w_branch_b, w_out, w_router_group, b_router_group, w_router_expert, b_router_expert, w_exp_gate, w_exp_up, w_exp_down, g_final):
    depth = w_ada.shape[0]
    assert depth == 1, "single-layer pipeline"
    bsz, seq, d = x_prompt.shape
    dbsz, dseq, _ = x_sample.shape
    g_all, p_state = ssm_a_re.shape[2:]
    ssm_w = ssm_d.shape[1]
    hw = filter_bias.shape[1]
    n_groups, _, epg = w_router_expert.shape[1:]
    ff = w_exp_gate.shape[-1]
    assert ssm_w // g_all * CHUNK == MXU_SIDE and 4 * p_state == MXU_SIDE
    assert ssm_w == hw and d == 2 * hw
    n_ctx = bsz * seq
    n_lat = dbsz * dseq
    t_all = n_ctx + n_lat

    xp = x_prompt.reshape(n_ctx, d)
    xs = x_sample.reshape(n_lat, d)
    pos = _grid_pos_embed(dseq, d)

    n_cond = 1 + dbsz
    cond8 = jnp.zeros((SUBLANES, d), F32).at[0].set(c_ctx).at[1:n_cond].set(c)
    mod = _ada_mod(cond8, w_ada[0], b_ada[0][None])
    mod = mod[:n_cond].reshape(n_cond * N_MOD, 1, d)

    tm_e = min(256, seq, dseq)
    tok_e = _Tok(n_ctx, n_lat, dseq, tm_e)
    tm_l = min(512, n_ctx, dseq)
    tok_l = _Tok(n_ctx, n_lat, dseq, tm_l)
    gate_col0 = ssm_w + 3 * hw
    h1 = _norm_mod(xp, xs, pos, mod, g_norm1, tok_l)
    tm_p, tn_p = min(1024, t_all), min(1024, hw)
    n_in = w_in.shape[2]
    u_a = _in_proj(h1, w_in[0], (0, 0), (0, ssm_w), F32, tm_p, tn_p, "in_proj_a")
    proj = _in_proj(h1, w_in[0], (gate_col0, n_in), (ssm_w, gate_col0), BF16, tm_p, tn_p, "in_proj")
    n_gate_cols = n_in - gate_col0

    w_state, w_so, a_step = _ssm_operators(ssm_a_re[0], ssm_a_im[0], ssm_log_dt[0], ssm_b_re[0], ssm_b_im[0],
                                           ssm_c_re[0], ssm_c_im[0], ssm_d[0])
    nc_ctx = seq // CHUNK
    nc_lat = dseq // CHUNK
    seq_per_blk = max(1, min(bsz, MXU_SIDE // nc_ctx))
    assert bsz % seq_per_blk == 0 and n_ctx % dseq == 0
    s0_ctx = jnp.zeros((bsz, g_all, 4 * p_state), F32)
    y_ctx, fin_ctx = _ssm_call(u_a, 0, n_ctx, w_state, w_so, a_step, s0_ctx, seq_per_blk, nc_ctx, "ssm_ctx")
    s0_lat = _state_to_lanes(state_ssm[:, 0].astype(F32))
    y_lat, _ = _ssm_call(u_a, n_ctx // dseq, n_lat, w_state, w_so, a_step, s0_lat, 1, nc_lat, "ssm_lat")
    new_state = _lanes_to_state(fin_ctx, p_state)[:, None]

    pa = _glu_branch(y_ctx, y_lat, w_glu[0].astype(BF16), b_glu[0][None], w_branch_a[0].astype(BF16), tok_l)

    w_short3 = jnp.transpose(w_short[0].reshape(-1, 3, hw), (1, 0, 2))
    b_short3 = b_short[0].reshape(3, 1, hw)
    vx2, x0c2 = _short_conv(proj, n_gate_cols // hw, hw, w_short3, b_short3, n_ctx, seq, dseq, tm_e)
    rates = jnp.abs(jnp.linspace(math.log(DECAY_TARGET) / DECAY_FAST, math.log(DECAY_TARGET) / DECAY_SLOW,
                                 hw, dtype=F32))
    rates2 = jnp.concatenate([rates, rates])[None]
    fbias = filter_bias[0][None].astype(F32)
    w_bb = w_branch_b[0].astype(BF16)
    pbs = []
    for seq_len, n_seq, row0 in ((seq, bsz, 0), (dseq, dbsz, n_ctx // dseq)):
        tabs = _dft_tables(seq_len)
        a2, bm2, colabs = _filt_gen(seq_len, w_f1[0], b_f1[0][None], freq1[0][None], w_f2[0],
                                    b_f2[0][None], freq2[0][None], w_f3[0], b_f3[0][None], rates2)
        kspec = _filt_dft(tabs, a2, bm2, colabs, seq_len)
        spec = _hy_fwd(tabs, vx2, kspec, n_seq, seq_len, row0)
        pbs.append(_hy_inv(tabs, spec, vx2, x0c2, fbias, w_bb, n_seq, seq_len, row0))
    pb_ctx, pb_lat = pbs

    w_r = jnp.zeros((d, LANES), F32)
    w_r = w_r.at[:, :n_groups].set(w_router_group[0])
    w_r = w_r.at[:, n_groups:n_groups + n_groups * epg].set(
        jnp.transpose(w_router_expert[0], (1, 0, 2)).reshape(d, n_groups * epg))
    b_r = jnp.zeros((1, LANES), F32)
    b_r = b_r.at[0, :n_groups].set(b_router_group[0])
    b_r = b_r.at[0, n_groups:n_groups + n_groups * epg].set(b_router_expert[0].reshape(-1))
    w_r = w_r.astype(BF16)
    x2, h2, rout = _merge_out(xp, xs, pos, proj, 0, pa, pb_ctx, pb_lat, mod, g_norm2,
                              w_out[0].astype(BF16), w_r, b_r, n_groups, tok_e)

    tm_m = min(512, t_all // n_groups)
    n_tiles = t_all // tm_m + n_groups
    grp = rout[:, LANES - 1].astype(jnp.int32)
    tile_grp, tile_nv, row_tok = _moe_plan(grp, n_groups, tm_m, n_tiles)
    ns = d // 2 // LANES
    moe = _moe(h2.reshape(t_all, ns, LANES), w_r, b_r, w_exp_gate[0].astype(BF16), w_exp_up[0].astype(BF16),
               w_exp_down[0].astype(BF16), tile_grp, tile_nv, row_tok, n_groups, epg, tm_m)
    moe = moe.reshape(t_all * ns, LANES)

    gf = g_final[None]
    y_prompt = _final(x2, moe, mod, gf, 0, n_ctx, lambda i: 0, tm_l)
    lat_tiles = dseq // tm_l
    y_sample = _final(x2, moe, mod, gf, n_ctx // tm_l, n_lat, lambda i: 1 + i // lat_tiles, tm_l)
    return (y_prompt.reshape(bsz, seq, d), y_sample.reshape(dbsz, dseq, d), new_state)
```

```python
import functools
import math

import numpy as np
import jax
import jax.numpy as jnp
from jax import lax
from jax.experimental import pallas as pl
from jax.experimental.pallas import tpu as pltpu

F32 = jnp.float32
BF16 = jnp.bfloat16
EPS = 1e-6
GRID_W = 64
N_MOD = 6
TOP_K_INNER = 2
DECAY_TARGET = 1e-2
DECAY_FAST = 0.3
DECAY_SLOW = 1.5
LANES = 128
SUBLANES = 8
MXU_SIDE = 256
VMEM_LIMIT = 56 * 1024 * 1024


def _cp(sem, vmem=VMEM_LIMIT):
    return pltpu.CompilerParams(dimension_semantics=sem, vmem_limit_bytes=vmem)


def _dot(a, b):
    return jnp.dot(a, b, preferred_element_type=F32)


def _split(a):
    hi = a.astype(BF16)
    lo = (a - hi.astype(F32)).astype(BF16)
    return hi, lo


def _dot3(a, b):
    a_hi, a_lo = _split(a)
    b_hi, b_lo = _split(b)
    return _dot(a_hi, b_hi) + _dot(a_lo, b_hi) + _dot(a_hi, b_lo)


def _rms(x, g):
    ms = jnp.mean(x * x, axis=-1, keepdims=True)
    return x * lax.rsqrt(ms + EPS) * g


def _pack_halves(xb):
    n = xb.shape[1] // 2
    lo = lax.bitcast_convert_type(xb[:, :n].astype(F32), jnp.uint32)
    hi = lax.bitcast_convert_type(xb[:, n:].astype(F32), jnp.uint32)
    return (lo >> 16) | (hi & jnp.uint32(0xFFFF0000))


def _unpack_halves(w):
    lo = lax.bitcast_convert_type(w << 16, F32)
    hi = lax.bitcast_convert_type(w & jnp.uint32(0xFFFF0000), F32)
    return jnp.concatenate([lo, hi], axis=1).astype(BF16)


def _to_pair_rows(x, scr):
    m, n = x.shape
    ev, od = [], []
    for c in range(n // LANES):
        scr[c * m:(c + 1) * m, :] = x[:, c * LANES:(c + 1) * LANES]
        ev.append(scr[pl.ds(c * m, m // 2, stride=2), :])
        od.append(scr[pl.ds(c * m + 1, m // 2, stride=2), :])
    return jnp.concatenate(ev + od, axis=1)


def _from_pair_rows(x2, scr):
    h, n2 = x2.shape
    n = n2 // 2
    out = []
    for c in range(n // LANES):
        scr[pl.ds(2 * c * h, h, stride=2), :] = x2[:, c * LANES:(c + 1) * LANES]
        scr[pl.ds(2 * c * h + 1, h, stride=2), :] = x2[:, n + c * LANES:n + (c + 1) * LANES]
        out.append(scr[2 * c * h:2 * (c + 1) * h, :])
    return jnp.concatenate(out, axis=1)


def _store_row_tiles(ref, row0, xb):
    w = _pack_halves(xb)
    m, n = w.shape
    ns = n // LANES
    for c in range(ns):
        ref[pl.ds(row0 + c, m, stride=ns), :] = w[:, c * LANES:(c + 1) * LANES]


def _load_row_tiles(ref, row0, m, ns):
    w = jnp.concatenate([ref[pl.ds(row0 + c, m, stride=ns), :] for c in range(ns)], axis=1)
    return _unpack_halves(w)


def _ada_kernel(c_ref, w_ref, b_ref, o_ref):
    c = c_ref[...]
    a = c * jax.nn.sigmoid(c)
    o_ref[...] = _dot3(a, w_ref[...]) + b_ref[...]


def _ada_mod(cond8, w_ada, b_ada):
    d, n = w_ada.shape
    tn = min(n, 1024)
    return pl.pallas_call(
        _ada_kernel,
        out_shape=jax.ShapeDtypeStruct((SUBLANES, n), F32),
        grid=(n // tn,),
        in_specs=[pl.BlockSpec((SUBLANES, d), lambda j: (0, 0)),
                  pl.BlockSpec((d, tn), lambda j: (0, j)),
                  pl.BlockSpec((1, tn), lambda j: (0, j))],
        out_specs=pl.BlockSpec((SUBLANES, tn), lambda j: (0, j)),
        compiler_params=_cp(("arbitrary",)),
        name="ada_mod",
    )(cond8, w_ada, b_ada)


class _Tok:
    def __init__(self, n_ctx, n_lat, lat_len, tm):
        assert n_ctx % tm == 0 and lat_len % tm == 0 and n_lat % lat_len == 0
        self.tm = tm
        self.nct = n_ctx // tm
        self.nst = n_lat // tm
        self.per_seq = lat_len // tm
        self.n = self.nct + self.nst

    def ctx_idx(self, i):
        return jnp.minimum(i, self.nct - 1)

    def lat_idx(self, i):
        return jnp.maximum(i - self.nct, 0)

    def pos_idx(self, i):
        return self.lat_idx(i) % self.per_seq

    def cond(self, i):
        return jnp.where(i < self.nct, 0, 1 + self.lat_idx(i) // self.per_seq)


def _norm_mod_kernel(tok, xp_ref, xs_ref, pos_ref, sh_ref, sc_ref, g_ref, h_ref):
    i = pl.program_id(0)

    def modulate(x):
        y = _rms(x, g_ref[...])
        return (y * (1.0 + sc_ref[0]) + sh_ref[0]).astype(h_ref.dtype)

    @pl.when(i < tok.nct)
    def _():
        h_ref[...] = modulate(xp_ref[...])

    @pl.when(i >= tok.nct)
    def _():
        h_ref[...] = modulate(xs_ref[...] + pos_ref[...])


def _norm_mod(xp, xs, pos, mod, g1, tok):
    d = xp.shape[1]
    tm = tok.tm
    t_all = xp.shape[0] + xs.shape[0]
    return pl.pallas_call(
        functools.partial(_norm_mod_kernel, tok),
        out_shape=jax.ShapeDtypeStruct((t_all, d), BF16),
        grid=(tok.n,),
        in_specs=[
            pl.BlockSpec((tm, d), lambda i: (tok.ctx_idx(i), 0)),
            pl.BlockSpec((tm, d), lambda i: (tok.lat_idx(i), 0)),
            pl.BlockSpec((tm, d), lambda i: (tok.pos_idx(i), 0)),
            pl.BlockSpec((1, 1, d), lambda i: (tok.cond(i) * N_MOD + 0, 0, 0)),
            pl.BlockSpec((1, 1, d), lambda i: (tok.cond(i) * N_MOD + 1, 0, 0)),
            pl.BlockSpec((1, d), lambda i: (0, 0)),
        ],
        out_specs=pl.BlockSpec((tm, d), lambda i: (i, 0)),
        compiler_params=_cp(("parallel",)),
        name="norm_mod",
    )(xp, xs, pos, mod, mod, g1)


def _in_proj_kernel(n_sig, h_ref, w_ref, o_ref, wb_ref):
    n = pl.program_id(0)
    m = pl.program_id(1)

    @pl.when(m == 0)
    def _():
        wb_ref[...] = w_ref[...].astype(BF16)

    acc = _dot(h_ref[...], wb_ref[...])

    @pl.when(n >= n_sig)
    def _():
        o_ref[...] = acc.astype(o_ref.dtype)

    @pl.when(n < n_sig)
    def _():
        o_ref[...] = jax.nn.sigmoid(acc).astype(o_ref.dtype)


def _in_proj(h, w_in, sig_cols, lin_cols, out_dtype, tm, tn, name):
    t_all, d = h.shape
    for c in sig_cols + lin_cols:
        assert c % tn == 0
    assert t_all % tm == 0
    n_sig = (sig_cols[1] - sig_cols[0]) // tn
    n_lin = (lin_cols[1] - lin_cols[0]) // tn
    sig0, lin0 = sig_cols[0] // tn, lin_cols[0] // tn

    def w_blk(n):
        return jnp.where(n < n_sig, sig0 + n, lin0 + n - n_sig)

    return pl.pallas_call(
        functools.partial(_in_proj_kernel, n_sig),
        out_shape=jax.ShapeDtypeStruct((t_all, (n_sig + n_lin) * tn), out_dtype),
        grid=(n_sig + n_lin, t_all // tm),
        in_specs=[pl.BlockSpec((tm, d), lambda n, m: (m, 0)),
                  pl.BlockSpec((d, tn), lambda n, m: (0, w_blk(n)))],
        out_specs=pl.BlockSpec((tm, tn), lambda n, m: (m, n)),
        scratch_shapes=[pltpu.VMEM((d, tn), BF16)],
        compiler_params=_cp(("arbitrary", "arbitrary")),
        name=name,
    )(h, w_in)


CHUNK = 16
GROUP_PAD = 8


def _ssm_operators(a_re, a_im, log_dt, b_re, b_im, c_re, c_im, d_skip):
    t = CHUNK
    g, p = a_re.shape[1:]
    c = b_re.shape[-1]
    lam = lax.complex(a_re.astype(F32), a_im.astype(F32))
    dt = jnp.exp(log_dt.astype(F32))[..., None]
    ld = lam * dt
    lam_bar = jnp.exp(ld)
    b_bar = ((lam_bar - 1.0) / lam)[..., None] * lax.complex(b_re.astype(F32), b_im.astype(F32))
    c_mat = lax.complex(c_re.astype(F32), c_im.astype(F32))
    taus = jnp.arange(t + 1, dtype=F32)
    e_pow = jnp.exp(taus[None, None, :, None] * ld[:, :, None, :])

    k_all = jnp.real(jnp.einsum('dgop,dgtp,dgpi->dgtoi', c_mat, e_pow[:, :, :t], b_bar))
    kf, kb = k_all[0], k_all[1]
    ii = np.arange(t)[None, :, None]
    jj = np.arange(t)[None, None, :]
    tt = np.arange(t)[:, None, None]
    place_f = jnp.asarray((jj - ii == tt).astype(np.float32))
    place_b = jnp.asarray((ii - jj == tt).astype(np.float32))
    kf = kf.at[:, 0].add(jnp.eye(c, dtype=F32)[None] * d_skip.astype(F32).reshape(g, c, 1))
    m = jnp.einsum('tij,gtoc->gicjo', place_f, kf) + jnp.einsum('tij,gtoc->gicjo', place_b, kb)
    m = m.reshape(g, t * c, t * c)

    e_f = e_pow[0][:, ::-1][:, 1:]
    e_b = e_pow[1][:, :t]
    ws_f = e_f[:, :, :, None] * b_bar[0][:, None]
    ws_b = e_b[:, :, :, None] * b_bar[1][:, None]

    def rows_ic(x):
        return jnp.transpose(x, (0, 1, 3, 2)).reshape(g, t * c, p)

    w_state = jnp.concatenate([rows_ic(jnp.real(ws_f)), rows_ic(jnp.real(ws_b)),
                               rows_ic(jnp.imag(ws_f)), rows_ic(jnp.imag(ws_b))], axis=-1)

    ce_f = c_mat[0][:, None] * e_pow[0][:, 1:, None, :]
    ce_b = c_mat[1][:, None] * e_pow[1][:, ::-1][:, :t, None, :]

    def cols_jc(x):
        return jnp.transpose(x, (0, 3, 1, 2)).reshape(g, p, t * c)

    w_so = jnp.concatenate([cols_jc(jnp.real(ce_f)), cols_jc(jnp.real(ce_b)),
                            cols_jc(-jnp.imag(ce_f)), cols_jc(-jnp.imag(ce_b))], axis=1)
    w_out = jnp.concatenate([m, w_so], axis=1)

    a_t = e_pow[:, :, t]
    a_step = jnp.stack([jnp.concatenate([jnp.real(a_t[0]), jnp.real(a_t[1])], axis=-1),
                        jnp.concatenate([jnp.imag(a_t[0]), jnp.imag(a_t[1])], axis=-1)])
    return w_state.astype(BF16), w_out.astype(BF16), a_step


def _lane_perm(n_i, gb, c):
    n = n_i * gb * c
    src = np.arange(n)
    i, q, ch = src // (gb * c), (src // c) % gb, src % c
    dst = q * (n_i * c) + i * c + ch
    p = np.zeros((n, n), np.float32)
    p[src, dst] = 1.0
    return p


def _ssm_kernel(n_seq, n_chunk, gb, g_all, x_ref, ws_ref, wo_ref, a_ref, s0_ref, pin_ref, pout_ref,
                y_ref, fin_ref, xq_ref, za_ref, zb_ref):
    phase = pl.program_id(1)
    j = pl.program_id(2)
    gp = g_all + GROUP_PAD
    nck = n_seq * n_chunk
    half = LANES // 2
    hc = CHUNK // 2

    @pl.when(phase == 0)
    def _():
        halves = []
        for h in range(2):
            xcat = jnp.concatenate([x_ref[pl.ds(h * hc + i, nck, stride=CHUNK), :].astype(BF16)
                                    for i in range(hc)], axis=1)
            halves.append(_dot(xcat, pin_ref[...]).astype(BF16))
        for q in range(gb):
            xq = jnp.concatenate([hv[:, q * LANES:(q + 1) * LANES] for hv in halves], axis=1)
            xq_ref[j * gb + q] = xq
            z = _dot(xq, ws_ref[q])
            za_ref[pl.ds(j * gb + q, nck, stride=gp), :] = z[:, :LANES]
            zb_ref[pl.ds(j * gb + q, nck, stride=gp), :] = z[:, LANES:]

    @pl.when(jnp.logical_and(phase == 1, j == 0))
    def _():
        a_re = a_ref[0]
        a_im = a_ref[1]
        lane = lax.broadcasted_iota(jnp.int32, (g_all, LANES), 1)
        is_f = lane < half
        for s in range(n_seq):
            def step(k, carry):
                s_a, s_b = carry
                rf = pl.multiple_of((s * n_chunk + k) * gp, SUBLANES)
                rb = pl.multiple_of((s * n_chunk + n_chunk - 1 - k) * gp, SUBLANES)
                zf_a = za_ref[pl.ds(rf, g_all), :]
                zf_b = zb_ref[pl.ds(rf, g_all), :]
                zb_a = za_ref[pl.ds(rb, g_all), :]
                zb_b = zb_ref[pl.ds(rb, g_all), :]
                za_ref[pl.ds(rf, g_all), :] = jnp.where(is_f, s_a, zf_a)
                zb_ref[pl.ds(rf, g_all), :] = jnp.where(is_f, s_b, zf_b)
                za_ref[pl.ds(rb, g_all), :] = jnp.where(is_f, zb_a, s_a)
                zb_ref[pl.ds(rb, g_all), :] = jnp.where(is_f, zb_b, s_b)
                z_a = jnp.where(is_f, zf_a, zb_a)
                z_b = jnp.where(is_f, zf_b, zb_b)
                n_a = a_re * s_a - a_im * s_b + z_a
                n_b = a_re * s_b + a_im * s_a + z_b
                return n_a, n_b

            init = s0_ref[s]
            f_a, f_b = lax.fori_loop(0, n_chunk, step, (init[:, :LANES], init[:, LANES:]))
            fin_ref[s] = jnp.concatenate([f_a, f_b], axis=-1)

    @pl.when(phase == 1)
    def _():
        ys = []
        for q in range(gb):
            st_a = za_ref[pl.ds(j * gb + q, nck, stride=gp), :].astype(BF16)
            st_b = zb_ref[pl.ds(j * gb + q, nck, stride=gp), :].astype(BF16)
            lhs = jnp.concatenate([xq_ref[j * gb + q], st_a, st_b], axis=-1)
            ys.append(_dot(lhs, wo_ref[q]).astype(BF16))
        for h in range(2):
            ycat = jnp.concatenate([y[:, h * LANES:(h + 1) * LANES] for y in ys], axis=1)
            yall = _dot(ycat, pout_ref[...])
            for i in range(hc):
                y_ref[pl.ds(h * hc + i, nck, stride=CHUNK), :] = yall[:, i * LANES:(i + 1) * LANES]


def _ssm_call(u, row_blk0, n_rows, w_state, w_out, a_step, s0, n_seq, n_chunk, name):
    g_all, w, _ = w_state.shape
    nck = n_seq * n_chunk
    blk = nck * CHUNK
    n_blk = n_rows // blk
    gb = LANES * CHUNK // w
    ngb = g_all // gb
    gp = g_all + GROUP_PAD
    kern = functools.partial(_ssm_kernel, n_seq, n_chunk, gb, g_all)
    assert w == 2 * LANES
    perm = _lane_perm(CHUNK // 2, gb, LANES // gb)
    p_in = jnp.asarray(perm, BF16)
    p_out = jnp.asarray(perm.T, BF16)
    n_perm = perm.shape[0]
    const = lambda b, ph, j: (0, 0)
    return pl.pallas_call(
        kern,
        out_shape=(jax.ShapeDtypeStruct((n_rows, u.shape[1]), F32),
                   jax.ShapeDtypeStruct((n_blk * n_seq, g_all, w), F32)),
        grid=(n_blk, 2, ngb),
        in_specs=[
            pl.BlockSpec((blk, LANES), lambda b, ph, j: (row_blk0 + b, jnp.where(ph == 0, j, ngb - 1))),
            pl.BlockSpec((gb, w, w), lambda b, ph, j: (jnp.where(ph == 0, j, ngb - 1), 0, 0)),
            pl.BlockSpec((gb, 2 * w, w), lambda b, ph, j: (jnp.where(ph == 0, 0, j), 0, 0)),
            pl.BlockSpec((2, g_all, LANES), lambda b, ph, j: (0, 0, 0)),
            pl.BlockSpec((n_seq, g_all, w), lambda b, ph, j: (b, 0, 0)),
            pl.BlockSpec((n_perm, n_perm), const, pipeline_mode=pl.Buffered(1)),
            pl.BlockSpec((n_perm, n_perm), const, pipeline_mode=pl.Buffered(1)),
        ],
        out_specs=(
            pl.BlockSpec((blk, LANES), lambda b, ph, j: (b, jnp.where(ph == 0, 0, j))),
            pl.BlockSpec((n_seq, g_all, w), lambda b, ph, j: (b, 0, 0)),
        ),
        scratch_shapes=[pltpu.VMEM((g_all, nck, w), BF16),
                        pltpu.VMEM((nck * gp, LANES), F32), pltpu.VMEM((nck * gp, LANES), F32)],
        compiler_params=_cp(("arbitrary", "arbitrary", "arbitrary")),
        name=name,
    )(u, w_state, w_out, a_step, s0, p_in, p_out)


def _glu_kernel(nct, yc_ref, yl_ref, wg_ref, bg_ref, wb_ref, o_ref):
    i = pl.program_id(0)
    y = jax.nn.gelu(jnp.where(i < nct, yc_ref[...], yl_ref[...]))
    z = _dot(y.astype(BF16), wg_ref[...]) + bg_ref[...]
    ya = y * jax.nn.sigmoid(z)
    o_ref[...] = _dot(ya.astype(BF16), wb_ref[...]).astype(o_ref.dtype)


def _glu_branch(y_ctx, y_lat, w_glu, b_glu, w_ba, tok):
    c = y_ctx.shape[1]
    d = w_ba.shape[1]
    tm = tok.tm
    return pl.pallas_call(
        functools.partial(_glu_kernel, tok.nct),
        out_shape=jax.ShapeDtypeStruct((y_ctx.shape[0] + y_lat.shape[0], d), BF16),
        grid=(tok.n,),
        in_specs=[pl.BlockSpec((tm, c), lambda i: (tok.ctx_idx(i), 0)),
                  pl.BlockSpec((tm, c), lambda i: (tok.lat_idx(i), 0)),
                  pl.BlockSpec((c, c), lambda i: (0, 0)),
                  pl.BlockSpec((1, c), lambda i: (0, 0)),
                  pl.BlockSpec((c, d), lambda i: (0, 0))],
        out_specs=pl.BlockSpec((tm, d), lambda i: (i, 0)),
        compiler_params=_cp(("parallel",)),
        name="glu_branch",
    )(y_ctx, y_lat, w_glu, b_glu, w_ba)


def _short_conv_kernel(nct, per_ctx, per_lat, v_ref, x0_ref, x1_ref, vp_ref, x0p_ref, x1p_ref,
                       vn_ref, x0n_ref, x1n_ref, w_ref, b_ref, vx_ref, x0c_ref, scr):
    i = pl.program_id(0)
    tm = v_ref.shape[0]
    k = jnp.where(i < nct, i % per_ctx, (i - nct) % per_lat)
    per = jnp.where(i < nct, per_ctx, per_lat)
    first = k == 0
    last = k == per - 1
    row = lax.broadcasted_iota(jnp.int32, (tm, 1), 0)

    def conv(cur_ref, prev_ref, next_ref, part):
        x = cur_ref[...].astype(F32)
        hp = jnp.where(first, 0.0, prev_ref[SUBLANES - 1:SUBLANES, :].astype(F32))
        hn = jnp.where(last, 0.0, next_ref[0:1, :].astype(F32))
        xm = jnp.where(row == 0, hp, pltpu.roll(x, 1, axis=0))
        xq = jnp.where(row == tm - 1, hn, pltpu.roll(x, tm - 1, axis=0))
        w = w_ref[part]
        return b_ref[part] + xm * w[0:1, :] + x * w[1:2, :] + xq * w[2:3, :]

    v = conv(v_ref, vp_ref, vn_ref, 0)
    x0 = conv(x0_ref, x0p_ref, x0n_ref, 1)
    x1 = conv(x1_ref, x1p_ref, x1n_ref, 2)
    vx_ref[...] = _to_pair_rows(v * x1, scr).astype(vx_ref.dtype)
    x0c_ref[...] = _to_pair_rows(x0, scr).astype(x0c_ref.dtype)


def _short_conv(proj, col_blk0, hw, w_short3, b_short3, n_ctx, seq, dseq, tm):
    t = proj.shape[0]
    assert seq % tm == 0 and dseq % tm == 0
    nt = t // tm
    r8 = tm // SUBLANES
    nb8 = t // SUBLANES

    def cur(k):
        return pl.BlockSpec((tm, hw), lambda i: (i, col_blk0 + k))

    def prev(k):
        return pl.BlockSpec((SUBLANES, hw), lambda i: (jnp.maximum(i * r8 - 1, 0), col_blk0 + k))

    def nxt(k):
        return pl.BlockSpec((SUBLANES, hw), lambda i: (jnp.minimum((i + 1) * r8, nb8 - 1), col_blk0 + k))

    kern = functools.partial(_short_conv_kernel, n_ctx // tm, seq // tm, dseq // tm)
    return pl.pallas_call(
        kern,
        out_shape=(jax.ShapeDtypeStruct((t // 2, 2 * hw), BF16), jax.ShapeDtypeStruct((t // 2, 2 * hw), BF16)),
        grid=(nt,),
        in_specs=[cur(0), cur(1), cur(2), prev(0), prev(1), prev(2), nxt(0), nxt(1), nxt(2),
                  pl.BlockSpec((3, 3, hw), lambda i: (0, 0, 0)),
                  pl.BlockSpec((3, 1, hw), lambda i: (0, 0, 0))],
        out_specs=(pl.BlockSpec((tm // 2, 2 * hw), lambda i: (i, 0)),
                   pl.BlockSpec((tm // 2, 2 * hw), lambda i: (i, 0))),
        scratch_shapes=[pltpu.VMEM((hw // LANES * tm, LANES), F32)],
        compiler_params=_cp(("parallel",)),
        name="short_conv",
    )(proj, proj, proj, proj, proj, proj, proj, proj, proj, w_short3, b_short3)


def _filt_gen_kernel(seq_len, n_bands, wt_ref, wc_ref, ws_ref, b1_ref, f1_ref, w2_ref, b2_ref, f2_ref,
                     w3_ref, b3_ref, rate_ref, a_ref, bm_ref, abs_ref, scr):
    i = pl.program_id(0)
    tl = 2 * a_ref.shape[0]
    hw = a_ref.shape[1] // 2
    pos = lax.broadcasted_iota(jnp.int32, (tl, 1), 0) + i * tl
    t = pos.astype(F32) / seq_len
    bands = (lax.broadcasted_iota(jnp.int32, (1, wc_ref.shape[0]), 1) + 1).astype(F32)
    ang = (2.0 * math.pi) * t * bands
    pre = t * wt_ref[...] + _dot3(jnp.cos(ang), wc_ref[...]) + _dot3(jnp.sin(ang), ws_ref[...])
    h = jnp.sin(f1_ref[...] * (pre + b1_ref[...]))
    h = jnp.sin(f2_ref[...] * (_dot3(h, w2_ref[...]) + b2_ref[...]))
    h = _dot3(h, w3_ref[...]) + b3_ref[...]
    h = h * jnp.exp(-t * rate_ref[...])
    h_f = h[:, :hw]
    h_b = h[:, hw:]
    colabs = jnp.sum(jnp.abs(h_f) + jnp.abs(h_b), axis=0, keepdims=True)
    h_bp = jnp.where(pos == 0, 0.0, h_b)
    a_ref[...] = _to_pair_rows(h_f + h_bp, scr).astype(a_ref.dtype)
    bm_ref[...] = _to_pair_rows(h_bp - h_f, scr).astype(bm_ref.dtype)

    @pl.when(i == 0)
    def _():
        abs_ref[...] = colabs

    @pl.when(i > 0)
    def _():
        abs_ref[...] += colabs


def _filt_gen(seq_len, w_f1, b_f1, freq1, w_f2, b_f2, freq2, w_f3, b_f3, rates2):
    n_emb, hid_raw = w_f1.shape
    n_bands = (n_emb - 1) // 2
    hw2 = w_f3.shape[1]
    hw = hw2 // 2
    tl = min(seq_len, 512)
    hid = LANES
    assert n_bands <= LANES and hid_raw <= LANES

    def pad(x, rows, cols):
        return jnp.zeros((rows, cols), F32).at[:x.shape[0], :x.shape[1]].set(x.astype(F32))

    wt = pad(w_f1[0:1], 1, hid)
    wc = pad(w_f1[1:1 + n_bands], LANES, hid)
    ws = pad(w_f1[1 + n_bands:], LANES, hid)
    b_f1, freq1, b_f2, freq2 = (pad(x, 1, hid) for x in (b_f1, freq1, b_f2, freq2))
    w_f2 = pad(w_f2, hid, hid)
    w_f3 = pad(w_f3, hid, hw2)
    full = lambda shape: pl.BlockSpec(shape, lambda i: (0,) * len(shape))
    kern = functools.partial(_filt_gen_kernel, seq_len, n_bands)
    return pl.pallas_call(
        kern,
        out_shape=(jax.ShapeDtypeStruct((seq_len // 2, hw2), BF16), jax.ShapeDtypeStruct((seq_len // 2, hw2), BF16),
                   jax.ShapeDtypeStruct((1, hw), F32)),
        grid=(seq_len // tl,),
        in_specs=[full((1, hid)), full((LANES, hid)), full((LANES, hid)), full((1, hid)), full((1, hid)),
                  full((hid, hid)), full((1, hid)), full((1, hid)), full((hid, hw2)), full((1, hw2)),
                  full((1, hw2))],
        out_specs=(pl.BlockSpec((tl // 2, hw2), lambda i: (i, 0)), pl.BlockSpec((tl // 2, hw2), lambda i: (i, 0)),
                   full((1, hw))),
        scratch_shapes=[pltpu.VMEM((hw // LANES * tl, LANES), F32)],
        compiler_params=_cp(("arbitrary",)),
        name="filt_gen",
    )(wt, wc, ws, b_f1, freq1, w_f2, b_f2, freq2, w_f3, b_f3, rates2)


def _trig_tables(row_hi, row_lo, col, period):
    col = col[None, :]
    ang_lo = ((row_lo[:, None] * col) % period).astype(F32) * (2.0 * math.pi / period)
    ang_hi = ((row_hi[:, None] * col) % period).astype(F32) * (2.0 * math.pi / period)
    c_lo, s_lo = jnp.cos(ang_lo)[None], jnp.sin(ang_lo)[None]
    c_hi, s_hi = jnp.cos(ang_hi)[:, None], jnp.sin(ang_hi)[:, None]
    n = row_hi.shape[0] * row_lo.shape[0]
    ctab = (c_hi * c_lo - s_hi * s_lo).reshape(n, col.shape[1])
    stab = (s_hi * c_lo + c_hi * s_lo).reshape(n, col.shape[1])
    return ctab.astype(BF16), stab.astype(BF16)


def _dft_tables(seq_len):
    m = seq_len // 2
    r = 1
    while r * r < m:
        r *= 2
    assert m % r == 0
    i32 = jnp.int32
    idx = jnp.arange(m, dtype=i32)
    hi = jnp.arange(m // r, dtype=i32) * r
    lo = jnp.arange(r, dtype=i32)
    ce, se = _trig_tables(hi, lo, idx, 2 * m)
    co, so = _trig_tables(hi, lo, 2 * idx + 1, 2 * seq_len)
    cot, sot = _trig_tables(2 * hi, 2 * lo + 1, idx, 2 * seq_len)
    return ce, se, co, so, cot, sot


def _alt_row(n):
    return (1 - 2 * (lax.broadcasted_iota(jnp.int32, (SUBLANES, n), 1) % 2)).astype(BF16)


def _filt_dft_kernel(seq_len, ce_ref, se_ref, co_ref, so_ref, ae_ref, ao_ref, be_ref, bo_ref, abs_ref,
                     krl_ref, krh_ref, kil_ref, kih_ref, kmid_ref):
    i = pl.program_id(0)
    scale = (1.0 / seq_len) / (abs_ref[...] + EPS)
    ec = _dot(ce_ref[...], ae_ref[...])
    oc = _dot(co_ref[...], ao_ref[...])
    es = _dot(se_ref[...], be_ref[...])
    os_ = _dot(so_ref[...], bo_ref[...])
    krl_ref[...] = (ec + oc) * scale
    krh_ref[...] = (ec - oc) * scale
    kil_ref[...] = (es + os_) * scale
    kih_ref[...] = (os_ - es) * scale

    @pl.when(i == 0)
    def _():
        sgn = _alt_row(ae_ref.shape[0])
        kmid_ref[0:1, :] = _dot(sgn, ae_ref[...])[0:1] * scale
        kmid_ref[1:2, :] = _dot(sgn, bo_ref[...])[0:1] * scale


def _filt_dft(tabs, a2, bm2, colabs, seq_len):
    m = seq_len // 2
    hw = a2.shape[1] // 2
    ce, se, co, so = tabs[:4]
    tf = min(m, 256)
    kern = functools.partial(_filt_dft_kernel, seq_len)
    mode = pl.Buffered(1) if m * hw * 2 > BIG_BLOCK_BYTES // 2 else None
    tab = pl.BlockSpec((tf, m), lambda i: (i, 0))
    res = lambda c: pl.BlockSpec((m, hw), lambda i: (0, c), pipeline_mode=mode)
    out = pl.BlockSpec((tf, hw), lambda i: (i, 0))
    sds = jax.ShapeDtypeStruct((m, hw), F32)
    return pl.pallas_call(
        kern,
        out_shape=(sds, sds, sds, sds, jax.ShapeDtypeStruct((2, hw), F32)),
        grid=(m // tf,),
        in_specs=[tab, tab, tab, tab, res(0), res(1), res(0), res(1), pl.BlockSpec((1, hw), lambda i: (0, 0))],
        out_specs=(out, out, out, out, pl.BlockSpec((2, hw), lambda i: (0, 0))),
        compiler_params=_cp(("arbitrary",)),
        name="filt_dft",
    )(ce, se, co, so, a2, a2, bm2, bm2, colabs)


def _hy_fwd_kernel(ce_ref, se_ref, co_ref, so_ref, ve_ref, vo_ref, krl_ref, krh_ref, kil_ref, kih_ref, kmid_ref,
                   ae_ref, be_ref, ao_ref, bo_ref, ymid_ref):
    j = pl.program_id(1)
    tf, m = ce_ref.shape
    nb = ae_ref.shape[0]
    krl, krh, kil, kih = krl_ref[...], krh_ref[...], kil_ref[...], kih_ref[...]
    row = lax.broadcasted_iota(jnp.int32, (tf, 1), 0) + j * tf
    dc = jnp.where(row == 0, 0.5, 1.0)
    for s in range(nb):
        ve = ve_ref[s * m:(s + 1) * m, :]
        vo = vo_ref[s * m:(s + 1) * m, :]
        ec = _dot(ce_ref[...], ve)
        oc = _dot(co_ref[...], vo)
        es = _dot(se_ref[...], ve)
        os_ = _dot(so_ref[...], vo)
        p_lo, p_hi = ec + oc, ec - oc
        q_lo, q_hi = es + os_, os_ - es
        yre_lo = (p_lo * krl + q_lo * kil) * dc
        yim_lo = p_lo * kil - q_lo * krl
        yre_hi = (p_hi * krh + q_hi * kih) * dc
        yim_hi = p_hi * kih - q_hi * krh
        ae_ref[s] = (yre_lo + yre_hi).astype(ae_ref.dtype)
        be_ref[s] = (yim_lo - yim_hi).astype(be_ref.dtype)
        ao_ref[s] = (yre_lo - yre_hi).astype(ao_ref.dtype)
        bo_ref[s] = (yim_lo + yim_hi).astype(bo_ref.dtype)

    @pl.when(j == 0)
    def _():
        sgn = _alt_row(m)
        kr_m, ki_m = kmid_ref[0:1, :], kmid_ref[1:2, :]
        for s in range(nb):
            p_m = _dot(sgn, ve_ref[s * m:(s + 1) * m, :])[0:1]
            q_m = _dot(sgn, vo_ref[s * m:(s + 1) * m, :])[0:1]
            ymid_ref[s, 0:1, :] = p_m * kr_m + q_m * ki_m
            ymid_ref[s, 1:2, :] = p_m * ki_m - q_m * kr_m


BIG_BLOCK_BYTES = 4 * 1024 * 1024


def _seq_batch(seq_len, n_seq):
    nb = max(1, min(n_seq, 4 * MXU_SIDE // seq_len))
    while n_seq % nb:
        nb -= 1
    return nb


def _hy_fwd(tabs, vx2, kspec, n_seq, seq_len, row0_blk):
    m = seq_len // 2
    hw = vx2.shape[1] // 2
    ce, se, co, so = tabs[:4]
    krl, krh, kil, kih, kmid = kspec
    tf = min(m, 256)
    nb = _seq_batch(seq_len, n_seq)
    assert row0_blk % nb == 0
    mode = pl.Buffered(1) if nb * m * hw * 2 > BIG_BLOCK_BYTES // 2 else None
    tab = pl.BlockSpec((tf, m), lambda b, j: (j, 0))
    vspec = lambda c: pl.BlockSpec((nb * m, hw), lambda b, j: (row0_blk // nb + b, c), pipeline_mode=mode)
    kt = pl.BlockSpec((tf, hw), lambda b, j: (j, 0))
    out = pl.BlockSpec((nb, tf, hw), lambda b, j: (b, j, 0))
    sds = jax.ShapeDtypeStruct((n_seq, m, hw), BF16)
    return pl.pallas_call(
        _hy_fwd_kernel,
        out_shape=(sds, sds, sds, sds, jax.ShapeDtypeStruct((n_seq, 2, hw), F32)),
        grid=(n_seq // nb, m // tf),
        in_specs=[tab, tab, tab, tab, vspec(0), vspec(1), kt, kt, kt, kt,
                  pl.BlockSpec((2, hw), lambda b, j: (0, 0))],
        out_specs=(out, out, out, out, pl.BlockSpec((nb, 2, hw), lambda b, j: (b, 0, 0))),
        compiler_params=_cp(("arbitrary", "arbitrary")),
        name="hy_fwd",
    )(ce, se, co, so, vx2, vx2, krl, krh, kil, kih, kmid)


def _hy_inv_kernel(ce_ref, se_ref, cot_ref, sot_ref, ae_ref, be_ref, ao_ref, bo_ref, ymid_ref,
                   vx_ref, x0_ref, fb_ref, wb_ref, o_ref, scr):
    j = pl.program_id(1)
    tt = ce_ref.shape[0]
    nb = ae_ref.shape[0]
    hw = fb_ref.shape[1]
    row = lax.broadcasted_iota(jnp.int32, (tt, 1), 0) + j * tt
    sgn = (1 - 2 * (row % 2)).astype(F32)
    for s in range(nb):
        rows = slice(s * tt, (s + 1) * tt)
        ymid = ymid_ref[s]
        z_e = _dot(ce_ref[...], ae_ref[s]) - _dot(se_ref[...], be_ref[s]) + sgn * ymid[0:1]
        z_o = _dot(cot_ref[...], ao_ref[s]) - _dot(sot_ref[...], bo_ref[s]) - sgn * ymid[1:2]
        proj = []
        for par, z in ((0, z_e), (1, z_o)):
            cols = slice(par * hw, (par + 1) * hw)
            z = z + vx_ref[rows, cols].astype(F32) * fb_ref[...]
            yb = x0_ref[rows, cols].astype(F32) * z
            proj.append(_dot(yb.astype(BF16), wb_ref[...]))
        o_ref[2 * s * tt:2 * (s + 1) * tt, :] = _from_pair_rows(
            jnp.concatenate(proj, axis=1), scr).astype(o_ref.dtype)


def _hy_inv(tabs, spec, vx2, x0c2, fbias, w_bb, n_seq, seq_len, row0_blk):
    m = seq_len // 2
    hw = fbias.shape[1]
    d = w_bb.shape[1]
    ce, se, _, _, cot, sot = tabs
    ae, be, ao, bo, ymid = spec
    tt = min(m, 128)
    per = m // tt
    nb = _seq_batch(seq_len, n_seq) if per == 1 else 1
    assert row0_blk % nb == 0
    mode = pl.Buffered(1) if nb * m * hw * 2 > BIG_BLOCK_BYTES // 2 else None
    tab = pl.BlockSpec((tt, m), lambda b, j: (j, 0))
    res = pl.BlockSpec((nb, m, hw), lambda b, j: (b, 0, 0), pipeline_mode=mode)
    rows_in = pl.BlockSpec((nb * tt, 2 * hw), lambda b, j: ((row0_blk // nb + b) * per + j, 0))
    return pl.pallas_call(
        _hy_inv_kernel,
        out_shape=jax.ShapeDtypeStruct((n_seq * seq_len, d), BF16),
        grid=(n_seq // nb, per),
        in_specs=[tab, tab, tab, tab, res, res, res, res,
                  pl.BlockSpec((nb, 2, hw), lambda b, j: (b, 0, 0)),
                  rows_in, rows_in,
                  pl.BlockSpec((1, hw), lambda b, j: (0, 0)),
                  pl.BlockSpec((hw, d), lambda b, j: (0, 0), pipeline_mode=pl.Buffered(1))],
        out_specs=pl.BlockSpec((nb * 2 * tt, d), lambda b, j: (b * per + j, 0)),
        scratch_shapes=[pltpu.VMEM((d // LANES * 2 * tt, LANES), F32)],
        compiler_params=_cp(("arbitrary", "arbitrary")),
        name="hy_inv",
    )(ce, se, cot, sot, ae, be, ao, bo, ymid, vx2, x0c2, fbias, w_bb)


MERGE_COL_BLOCKS = 4

def _merge_kernel(tok, n_groups, xp_ref, xs_ref, pos_ref, g0_ref, g1_ref, pa_ref, pbc_ref, pbs_ref,
                  gate1_ref, sh2_ref, sc2_ref, gn_ref, wo_ref, wr_ref, br_ref,
                  x2_ref, h2_ref, rt_ref):
    i = pl.program_id(0)
    is_ctx = i < tok.nct
    tm, d = x2_ref.shape
    pb = jnp.where(is_ctx, pbc_ref[...], pbs_ref[...])
    merged = g0_ref[...] * pa_ref[...] + g1_ref[...] * pb
    cb = d // MERGE_COL_BLOCKS
    ssq = jnp.zeros((tm, 1), F32)
    for c in range(MERGE_COL_BLOCKS):
        cols = slice(c * cb, (c + 1) * cb)
        att = _dot(merged, wo_ref[:, cols])
        x = jnp.where(is_ctx, xp_ref[:, cols], xs_ref[:, cols] + pos_ref[:, cols])
        x2c = x + gate1_ref[0][:, cols] * att
        x2_ref[:, cols] = x2c
        ssq = ssq + jnp.sum(x2c * x2c, axis=-1, keepdims=True)
    x2 = x2_ref[...]
    h2 = x2 * lax.rsqrt(ssq * (1.0 / d) + EPS) * gn_ref[...] * (1.0 + sc2_ref[0]) + sh2_ref[0]
    h2b = h2.astype(BF16)
    _store_row_tiles(h2_ref, 0, h2b)
    logits = _dot(h2b, wr_ref[...]) + br_ref[...]
    lane = lax.broadcasted_iota(jnp.int32, logits.shape, 1)
    gl = jnp.where(lane < n_groups, logits, -jnp.inf)
    gmax = jnp.max(gl, axis=-1, keepdims=True)
    gidx = jnp.min(jnp.where(gl == gmax, lane, n_groups), axis=-1, keepdims=True)
    rt_ref[...] = jnp.where(lane == LANES - 1, gidx.astype(F32), logits)


def _merge_out(xp, xs, pos, proj, gate_blk0, pa, pb_ctx, pb_lat, mod, g2, w_out, w_r, b_r, n_groups, tok):
    d = xp.shape[1]
    tm = tok.tm
    t_all = xp.shape[0] + xs.shape[0]
    kern = functools.partial(_merge_kernel, tok, n_groups)
    one = pl.Buffered(1)
    ns = d // 2 // LANES

    def modspec(k):
        return pl.BlockSpec((1, 1, d), lambda i: (tok.cond(i) * N_MOD + k, 0, 0))

    return pl.pallas_call(
        kern,
        out_shape=(jax.ShapeDtypeStruct((t_all, d), F32), jax.ShapeDtypeStruct((t_all * ns, LANES), jnp.uint32),
                   jax.ShapeDtypeStruct((t_all, LANES), F32)),
        grid=(tok.n,),
        in_specs=[
            pl.BlockSpec((tm, d), lambda i: (tok.ctx_idx(i), 0)),
            pl.BlockSpec((tm, d), lambda i: (tok.lat_idx(i), 0)),
            pl.BlockSpec((tm, d), lambda i: (tok.pos_idx(i), 0)),
            pl.BlockSpec((tm, d), lambda i: (i, gate_blk0)),
            pl.BlockSpec((tm, d), lambda i: (i, gate_blk0 + 1)),
            pl.BlockSpec((tm, d), lambda i: (i, 0)),
            pl.BlockSpec((tm, d), lambda i: (tok.ctx_idx(i), 0)),
            pl.BlockSpec((tm, d), lambda i: (tok.lat_idx(i), 0)),
            modspec(2), modspec(3), modspec(4),
            pl.BlockSpec((1, d), lambda i: (0, 0)),
            pl.BlockSpec((d, d), lambda i: (0, 0), pipeline_mode=one),
            pl.BlockSpec((d, LANES), lambda i: (0, 0)),
            pl.BlockSpec((1, LANES), lambda i: (0, 0)),
        ],
        out_specs=(pl.BlockSpec((tm, d), lambda i: (i, 0)), pl.BlockSpec((tm * ns, LANES), lambda i: (i, 0)),
                   pl.BlockSpec((tm, LANES), lambda i: (i, 0))),
        compiler_params=_cp(("parallel",)),
        name="merge_out",
    )(xp, xs, pos, proj, proj, pa, pb_ctx, pb_lat, mod, mod, mod, g2, w_out, w_r, b_r)


ROW_UNROLL = 8

def _moe_kernel(n_tiles, n_groups, epg, tg_ref, tnv_ref, rt_ref,
                h2_hbm, wr_ref, br_ref, wg_ref, wu_ref, wd_ref, out_hbm,
                xbuf, xb_ref, acc, obuf, gws, gsem, ssem):
    i = pl.program_id(0)
    j = pl.program_id(1)
    tm = xb_ref.shape[0]
    ns = h2_hbm.shape[1]
    nv = tnv_ref[i]
    slot = i % 2

    def buf_rows(buf, slot_, r, rows):
        return buf.at[pl.ds(pl.multiple_of((slot_ * tm + r) * ns, ns), rows * ns), :]

    def gather_copy(tok_row, slot_, r):
        return pltpu.make_async_copy(h2_hbm.at[tok_row], buf_rows(xbuf, slot_, r, 1), gsem.at[slot_])

    def gather_wait(slot_, rows):
        dst = buf_rows(xbuf, slot_, 0, rows)
        pltpu.make_async_copy(dst, dst, gsem.at[slot_]).wait()

    def scatter_copy(tok_row, slot_, r):
        return pltpu.make_async_copy(buf_rows(obuf, slot_, r, 1), out_hbm.at[tok_row], ssem.at[0])

    def scatter_wait(slot_, rows):
        src = buf_rows(obuf, slot_, 0, rows)
        pltpu.make_async_copy(src, src, ssem.at[0]).wait()

    def for_rows(count, fn):
        n_full = count // ROW_UNROLL

        def body(b, c):
            for k in range(ROW_UNROLL):
                fn(b * ROW_UNROLL + k)
            return c

        def tail(r, c):
            fn(r)
            return c

        lax.fori_loop(0, n_full, body, 0)
        lax.fori_loop(n_full * ROW_UNROLL, count, tail, 0)

    def wait_scatter(count, slot_):
        p = tm
        while p >= 1:
            @pl.when((count & p) != 0)
            def _(p=p):
                scatter_wait(slot_, p)
            p //= 2

    nv_prev = tnv_ref[jnp.maximum(i - 1, 0)]
    nv_next = tnv_ref[jnp.minimum(i + 1, n_tiles - 1)]
    prev_deferred = jnp.logical_and(i > 0, jnp.logical_and(nv_prev == tm, nv > 0))
    deferred = jnp.logical_and(nv == tm, nv_next > 0)

    @pl.when(j == 0)
    def _():
        @pl.when(i == 0)
        def _():
            for_rows(tm, lambda r: gather_copy(rt_ref[r], 0, r).start())

        @pl.when(jnp.logical_or(i == 0, nv_prev > 0))
        def _():
            gather_wait(slot, tm)

    @pl.when(jnp.logical_and(j == 0, nv > 0))
    def _():
        grp = tg_ref[i]
        xb = _load_row_tiles(xbuf, slot * tm * ns, tm, ns)
        xb_ref[...] = xb
        logits = _dot(xb, wr_ref[...]) + br_ref[...]
        lane = lax.broadcasted_iota(jnp.int32, logits.shape, 1)
        gl = jnp.where(lane < n_groups, logits, -jnp.inf)
        gmax = jnp.max(gl, axis=-1, keepdims=True)
        ge = jnp.exp(gl - gmax)
        g_w = jnp.sum(jnp.where(lane == grp, ge, 0.0), axis=-1, keepdims=True) / jnp.sum(ge, axis=-1, keepdims=True)
        base = n_groups + grp * epg
        e = [jnp.sum(jnp.where(lane == base + k, logits, 0.0), axis=-1, keepdims=True) for k in range(epg)]
        sel = []
        for k in range(epg):
            rank = jnp.zeros_like(e[k])
            for m in range(epg):
                if m == k:
                    continue
                ahead = (e[m] > e[k]) if m > k else (e[m] >= e[k])
                rank = rank + ahead.astype(F32)
            sel.append(rank < TOP_K_INNER)
        emax = functools.reduce(jnp.maximum, e)
        ex = [jnp.where(sel[k], jnp.exp(e[k] - emax), 0.0) for k in range(epg)]
        den = functools.reduce(lambda a, b: a + b, ex)
        for k in range(epg):
            gws[k] = ex[k] / den * g_w
        acc[...] = jnp.zeros(acc.shape, acc.dtype)

    def expert_step(scatter_prev):
        per = tm // epg
        for k in range(per):
            r = j * per + k
            if scatter_prev:
                scatter_copy(rt_ref[(i - 1) * tm + r], 1 - slot, r).start()
            gather_copy(rt_ref[(i + 1) * tm + r], 1 - slot, r).start()
        xb = xb_ref[...]
        g = _dot(xb, wg_ref[...].astype(BF16))
        u = _dot(xb, wu_ref[...].astype(BF16))
        hid = (g * jax.nn.sigmoid(g)) * u * gws[j]
        acc[...] += _dot(hid.astype(BF16), wd_ref[...].astype(BF16))

    @pl.when(jnp.logical_and(nv > 0, prev_deferred))
    def _():
        expert_step(True)

    @pl.when(jnp.logical_and(nv > 0, jnp.logical_not(prev_deferred)))
    def _():
        expert_step(False)

    @pl.when(j == epg - 1)
    def _():
        @pl.when(prev_deferred)
        def _():
            scatter_wait(1 - slot, tm)

        @pl.when(nv > 0)
        def _():
            _store_row_tiles(obuf, slot * tm * ns, acc[...].astype(BF16))

        @pl.when(jnp.logical_and(nv > 0, jnp.logical_not(deferred)))
        def _():
            for_rows(nv, lambda r: scatter_copy(rt_ref[i * tm + r], slot, r).start())
            wait_scatter(nv, slot)


def _moe(h2p, w_r, b_r, w_g, w_u, w_d, tile_grp, tile_nv, row_tok, n_groups, epg, tm):
    t, ns, _ = h2p.shape
    d = 2 * ns * LANES
    ff = w_g.shape[-1]
    n_tiles = tile_grp.shape[0]
    assert tm & (tm - 1) == 0, "row-count waits decompose tm in binary"
    kern = functools.partial(_moe_kernel, n_tiles, n_groups, epg)

    def widx(i, j, tg, tnv, rt):
        return tg[i] * epg + jnp.where(tnv[i] > 0, j, epg - 1)

    grid_spec = pltpu.PrefetchScalarGridSpec(
        num_scalar_prefetch=3,
        grid=(n_tiles, epg),
        in_specs=[
            pl.BlockSpec(memory_space=pl.ANY),
            pl.BlockSpec((d, LANES), lambda i, j, tg, tnv, rt: (0, 0)),
            pl.BlockSpec((1, LANES), lambda i, j, tg, tnv, rt: (0, 0)),
            pl.BlockSpec((None, d, ff), lambda i, j, tg, tnv, rt: (widx(i, j, tg, tnv, rt), 0, 0)),
            pl.BlockSpec((None, d, ff), lambda i, j, tg, tnv, rt: (widx(i, j, tg, tnv, rt), 0, 0)),
            pl.BlockSpec((None, ff, d), lambda i, j, tg, tnv, rt: (widx(i, j, tg, tnv, rt), 0, 0)),
        ],
        out_specs=pl.BlockSpec(memory_space=pl.ANY),
        scratch_shapes=[
            pltpu.VMEM((2 * tm * ns, LANES), jnp.uint32),
            pltpu.VMEM((tm, d), BF16),
            pltpu.VMEM((tm, d), F32),
            pltpu.VMEM((2 * tm * ns, LANES), jnp.uint32),
            pltpu.VMEM((epg, tm, 1), F32),
            pltpu.SemaphoreType.DMA((2,)),
            pltpu.SemaphoreType.DMA((1,)),
        ],
    )
    return pl.pallas_call(
        kern,
        out_shape=jax.ShapeDtypeStruct((t, ns, LANES), jnp.uint32),
        grid_spec=grid_spec,
        compiler_params=_cp(("arbitrary", "arbitrary")),
        name="moe",
    )(tile_grp, tile_nv, row_tok, h2p, w_r, b_r, w_g, w_u, w_d)


def _moe_plan(grp, n_groups, tm, n_tiles):
    t = grp.shape[0]
    onehot = (grp[:, None] == jnp.arange(n_groups, dtype=jnp.int32)[None, :]).astype(jnp.int32)
    csum = jnp.cumsum(onehot, axis=0)
    counts = csum[-1]
    rank = jnp.take_along_axis(csum, grp[:, None], axis=1)[:, 0] - 1
    ntile_g = (counts + tm - 1) // tm
    tile_end = jnp.cumsum(ntile_g)
    tile_off = tile_end - ntile_g
    slot = tile_off[grp] * tm + rank
    row_tok = jnp.zeros((n_tiles * tm,), jnp.int32).at[slot].set(jnp.arange(t, dtype=jnp.int32))
    tiles = jnp.arange(n_tiles, dtype=jnp.int32)
    tile_grp = jnp.minimum(jnp.sum((tiles[:, None] >= tile_end[None, :]).astype(jnp.int32), axis=1),
                           n_groups - 1)
    tile_nv = jnp.clip(counts[tile_grp] - (tiles - tile_off[tile_grp]) * tm, 0, tm)
    tile_nv = jnp.where(tiles < tile_end[-1], tile_nv, 0)
    return tile_grp.astype(jnp.int32), tile_nv.astype(jnp.int32), row_tok


def _final_kernel(x2_ref, m_ref, gate2_ref, gf_ref, o_ref):
    tm, d = x2_ref.shape
    ns = d // 2 // LANES
    m = _load_row_tiles(m_ref, 0, tm, ns).astype(F32)
    x3 = x2_ref[...] + gate2_ref[0] * m
    o_ref[...] = _rms(x3, gf_ref[...])


def _final(x2, moe, mod, g_final, row_blk0, n_rows, cond_of_tile, tm):
    d = x2.shape[1]
    ns = d // 2 // LANES
    return pl.pallas_call(
        _final_kernel,
        out_shape=jax.ShapeDtypeStruct((n_rows, d), F32),
        grid=(n_rows // tm,),
        in_specs=[pl.BlockSpec((tm, d), lambda i: (row_blk0 + i, 0)),
                  pl.BlockSpec((tm * ns, LANES), lambda i: (row_blk0 + i, 0)),
                  pl.BlockSpec((1, 1, d), lambda i: (cond_of_tile(i) * N_MOD + 5, 0, 0)),
                  pl.BlockSpec((1, d), lambda i: (0, 0))],
        out_specs=pl.BlockSpec((tm, d), lambda i: (i, 0)),
        compiler_params=_cp(("parallel",)),
        name="final",
    )(x2, moe, mod, g_final)


def _grid_pos_embed(n_tokens, dim):
    rows_n = n_tokens // GRID_W
    quarter = dim // 4
    omega = 1.0 / (10000.0 ** (jnp.arange(quarter, dtype=F32) / quarter))

    def emb(n):
        ang = jnp.arange(n).astype(F32)[:, None] * omega[None, :]
        return jnp.concatenate([jnp.sin(ang), jnp.cos(ang)], axis=-1)

    e_r = jnp.broadcast_to(emb(rows_n)[:, None, :], (rows_n, GRID_W, dim // 2))
    e_c = jnp.broadcast_to(emb(GRID_W)[None, :, :], (rows_n, GRID_W, dim // 2))
    return jnp.concatenate([e_r, e_c], axis=-1).reshape(n_tokens, dim)


def _state_to_lanes(s):
    b, _, _, g, p = s.shape
    return jnp.transpose(s, (0, 3, 2, 1, 4)).reshape(b, g, 4 * p)


def _lanes_to_state(x, p):
    b, g, _ = x.shape
    return jnp.transpose(x.reshape(b, g, 2, 2, p), (0, 3, 2, 1, 4))


def kernel(x_prompt, x_sample, state_ssm, c, c_ctx, w_ada, b_ada, g_norm1, g_norm2, w_in, ssm_a_re, ssm_a_im, ssm_log_dt, ssm_b_re, ssm_b_im, ssm_c_re, ssm_c_im, ssm_d, w_glu, b_glu, w_short, b_short, w_f1, b_f1, freq1, w_f2, b_f2, freq2, w_f3, b_f3, filter_bias, w_branch_a, w_branch_b, w_out, w_router_group, b_router_group, w_router_expert, b_router_expert, w_exp_gate, w_exp_up, w_exp_down, g_final):
    depth = w_ada.shape[0]
    assert depth == 1, "single-layer pipeline"
    bsz, seq, d = x_prompt.shape
    dbsz, dseq, _ = x_sample.shape
    g_all, p_state = ssm_a_re.shape[2:]
    ssm_w = ssm_d.shape[1]
    hw = filter_bias.shape[1]
    n_groups, _, epg = w_router_expert.shape[1:]
    ff = w_exp_gate.shape[-1]
    assert ssm_w // g_all * CHUNK == MXU_SIDE and 4 * p_state == MXU_SIDE
    assert ssm_w == hw and d == 2 * hw
    n_ctx = bsz * seq
    n_lat = dbsz * dseq
    t_all = n_ctx + n_lat

    xp = x_prompt.reshape(n_ctx, d)
    xs = x_sample.reshape(n_lat, d)
    pos = _grid_pos_embed(dseq, d)

    n_cond = 1 + dbsz
    cond8 = jnp.zeros((SUBLANES, d), F32).at[0].set(c_ctx).at[1:n_cond].set(c)
    mod = _ada_mod(cond8, w_ada[0], b_ada[0][None])
    mod = mod[:n_cond].reshape(n_cond * N_MOD, 1, d)

    tm_e = min(256, seq, dseq)
    tok_e = _Tok(n_ctx, n_lat, dseq, tm_e)
    tm_l = min(512, n_ctx, dseq)
    tok_l = _Tok(n_ctx, n_lat, dseq, tm_l)
    gate_col0 = ssm_w + 3 * hw
    h1 = _norm_mod(xp, xs, pos, mod, g_norm1, tok_l)
    tm_p, tn_p = min(1024, t_all), min(1024, hw)
    n_in = w_in.shape[2]
    u_a = _in_proj(h1, w_in[0], (0, 0), (0, ssm_w), F32, tm_p, tn_p, "in_proj_a")
    proj = _in_proj(h1, w_in[0], (gate_col0, n_in), (ssm_w, gate_col0), BF16, tm_p, tn_p, "in_proj")
    n_gate_cols = n_in - gate_col0

    w_state, w_so, a_step = _ssm_operators(ssm_a_re[0], ssm_a_im[0], ssm_log_dt[0], ssm_b_re[0], ssm_b_im[0],
                                           ssm_c_re[0], ssm_c_im[0], ssm_d[0])
    nc_ctx = seq // CHUNK
    nc_lat = dseq // CHUNK
    seq_per_blk = max(1, min(bsz, MXU_SIDE // nc_ctx))
    assert bsz % seq_per_blk == 0 and n_ctx % dseq == 0
    s0_ctx = jnp.zeros((bsz, g_all, 4 * p_state), F32)
    y_ctx, fin_ctx = _ssm_call(u_a, 0, n_ctx, w_state, w_so, a_step, s0_ctx, seq_per_blk, nc_ctx, "ssm_ctx")
    s0_lat = _state_to_lanes(state_ssm[:, 0].astype(F32))
    y_lat, _ = _ssm_call(u_a, n_ctx // dseq, n_lat, w_state, w_so, a_step, s0_lat, 1, nc_lat, "ssm_lat")
    new_state = _lanes_to_state(fin_ctx, p_state)[:, None]

    pa = _glu_branch(y_ctx, y_lat, w_glu[0].astype(BF16), b_glu[0][None], w_branch_a[0].astype(BF16), tok_l)

    w_short3 = jnp.transpose(w_short[0].reshape(-1, 3, hw), (1, 0, 2))
    b_short3 = b_short[0].reshape(3, 1, hw)
    vx2, x0c2 = _short_conv(proj, n_gate_cols // hw, hw, w_short3, b_short3, n_ctx, seq, dseq, tm_e)
    rates = jnp.abs(jnp.linspace(math.log(DECAY_TARGET) / DECAY_FAST, math.log(DECAY_TARGET) / DECAY_SLOW,
                                 hw, dtype=F32))
    rates2 = jnp.concatenate([rates, rates])[None]
    fbias = filter_bias[0][None].astype(F32)
    w_bb = w_branch_b[0].astype(BF16)
    pbs = []
    for seq_len, n_seq, row0 in ((seq, bsz, 0), (dseq, dbsz, n_ctx // dseq)):
        tabs = _dft_tables(seq_len)
        a2, bm2, colabs = _filt_gen(seq_len, w_f1[0], b_f1[0][None], freq1[0][None], w_f2[0],
                                    b_f2[0][None], freq2[0][None], w_f3[0], b_f3[0][None], rates2)
        kspec = _filt_dft(tabs, a2, bm2, colabs, seq_len)
        spec = _hy_fwd(tabs, vx2, kspec, n_seq, seq_len, row0)
        pbs.append(_hy_inv(tabs, spec, vx2, x0c2, fbias, w_bb, n_seq, seq_len, row0))
    pb_ctx, pb_lat = pbs

    w_r = jnp.zeros((d, LANES), F32)
    w_r = w_r.at[:, :n_groups].set(w_router_group[0])
    w_r = w_r.at[:, n_groups:n_groups + n_groups * epg].set(
        jnp.transpose(w_router_expert[0], (1, 0, 2)).reshape(d, n_groups * epg))
    b_r = jnp.zeros((1, LANES), F32)
    b_r = b_r.at[0, :n_groups].set(b_router_group[0])
    b_r = b_r.at[0, n_groups:n_groups + n_groups * epg].set(b_router_expert[0].reshape(-1))
    w_r = w_r.astype(BF16)
    x2, h2, rout = _merge_out(xp, xs, pos, proj, 0, pa, pb_ctx, pb_lat, mod, g_norm2,
                              w_out[0].astype(BF16), w_r, b_r, n_groups, tok_e)

    tm_m = min(512, t_all // n_groups)
    n_tiles = t_all // tm_m + n_groups
    grp = rout[:, LANES - 1].astype(jnp.int32)
    tile_grp, tile_nv, row_tok = _moe_plan(grp, n_groups, tm_m, n_tiles)
    ns = d // 2 // LANES
    moe = _moe(h2.reshape(t_all, ns, LANES), w_r, b_r, w_exp_gate[0], w_exp_up[0], w_exp_down[0],
               tile_grp, tile_nv, row_tok, n_groups, epg, tm_m)
    moe = moe.reshape(t_all * ns, LANES)

    gf = g_final[None]
    y_prompt = _final(x2, moe, mod, gf, 0, n_ctx, lambda i: 0, tm_l)
    lat_tiles = dseq // tm_l
    y_sample = _final(x2, moe, mod, gf, n_ctx // tm_l, n_lat, lambda i: 1 + i // lat_tiles, tm_l)
    return (y_prompt.reshape(bsz, seq, d), y_sample.reshape(dbsz, dseq, d), new_state)
```

```python
import functools
import math

import numpy as np
import jax
import jax.numpy as jnp
from jax import lax
from jax.experimental import pallas as pl
from jax.experimental.pallas import tpu as pltpu

F32 = jnp.float32
BF16 = jnp.bfloat16
EPS = 1e-6
GRID_W = 64
N_MOD = 6
TOP_K_INNER = 2
DECAY_TARGET = 1e-2
DECAY_FAST = 0.3
DECAY_SLOW = 1.5
LANES = 128
SUBLANES = 8
MXU_SIDE = 256
VMEM_LIMIT = 56 * 1024 * 1024


def _cp(sem, vmem=VMEM_LIMIT):
    return pltpu.CompilerParams(dimension_semantics=sem, vmem_limit_bytes=vmem)


def _dot(a, b):
    return jnp.dot(a, b, preferred_element_type=F32)


def _split(a):
    hi = a.astype(BF16)
    lo = (a - hi.astype(F32)).astype(BF16)
    return hi, lo


def _dot3(a, b):
    a_hi, a_lo = _split(a)
    b_hi, b_lo = _split(b)
    return _dot(a_hi, b_hi) + _dot(a_lo, b_hi) + _dot(a_hi, b_lo)


def _rms(x, g):
    ms = jnp.mean(x * x, axis=-1, keepdims=True)
    return x * lax.rsqrt(ms + EPS) * g


def _pack_halves(xb):
    n = xb.shape[1] // 2
    lo = lax.bitcast_convert_type(xb[:, :n].astype(F32), jnp.uint32)
    hi = lax.bitcast_convert_type(xb[:, n:].astype(F32), jnp.uint32)
    return (lo >> 16) | (hi & jnp.uint32(0xFFFF0000))


def _unpack_halves(w):
    lo = lax.bitcast_convert_type(w << 16, F32)
    hi = lax.bitcast_convert_type(w & jnp.uint32(0xFFFF0000), F32)
    return jnp.concatenate([lo, hi], axis=1).astype(BF16)


def _to_pair_rows(x, scr):
    m, n = x.shape
    ev, od = [], []
    for c in range(n // LANES):
        scr[c * m:(c + 1) * m, :] = x[:, c * LANES:(c + 1) * LANES]
        ev.append(scr[pl.ds(c * m, m // 2, stride=2), :])
        od.append(scr[pl.ds(c * m + 1, m // 2, stride=2), :])
    return jnp.concatenate(ev + od, axis=1)


def _from_pair_rows(x2, scr):
    h, n2 = x2.shape
    n = n2 // 2
    out = []
    for c in range(n // LANES):
        scr[pl.ds(2 * c * h, h, stride=2), :] = x2[:, c * LANES:(c + 1) * LANES]
        scr[pl.ds(2 * c * h + 1, h, stride=2), :] = x2[:, n + c * LANES:n + (c + 1) * LANES]
        out.append(scr[2 * c * h:2 * (c + 1) * h, :])
    return jnp.concatenate(out, axis=1)


def _store_row_tiles(ref, row0, xb):
    w = _pack_halves(xb)
    m, n = w.shape
    ns = n // LANES
    for c in range(ns):
        ref[pl.ds(row0 + c, m, stride=ns), :] = w[:, c * LANES:(c + 1) * LANES]


def _load_row_tiles(ref, row0, m, ns):
    w = jnp.concatenate([ref[pl.ds(row0 + c, m, stride=ns), :] for c in range(ns)], axis=1)
    return _unpack_halves(w)


def _ada_kernel(c_ref, w_ref, b_ref, o_ref):
    c = c_ref[...]
    a = c * jax.nn.sigmoid(c)
    o_ref[...] = _dot3(a, w_ref[...]) + b_ref[...]


def _ada_mod(cond8, w_ada, b_ada):
    d, n = w_ada.shape
    tn = min(n, 1024)
    return pl.pallas_call(
        _ada_kernel,
        out_shape=jax.ShapeDtypeStruct((SUBLANES, n), F32),
        grid=(n // tn,),
        in_specs=[pl.BlockSpec((SUBLANES, d), lambda j: (0, 0)),
                  pl.BlockSpec((d, tn), lambda j: (0, j)),
                  pl.BlockSpec((1, tn), lambda j: (0, j))],
        out_specs=pl.BlockSpec((SUBLANES, tn), lambda j: (0, j)),
        compiler_params=_cp(("arbitrary",)),
        name="ada_mod",
    )(cond8, w_ada, b_ada)


class _Tok:
    def __init__(self, n_ctx, n_lat, lat_len, tm):
        assert n_ctx % tm == 0 and lat_len % tm == 0 and n_lat % lat_len == 0
        self.tm = tm
        self.nct = n_ctx // tm
        self.nst = n_lat // tm
        self.per_seq = lat_len // tm
        self.n = self.nct + self.nst

    def ctx_idx(self, i):
        return jnp.minimum(i, self.nct - 1)

    def lat_idx(self, i):
        return jnp.maximum(i - self.nct, 0)

    def pos_idx(self, i):
        return self.lat_idx(i) % self.per_seq

    def cond(self, i):
        return jnp.where(i < self.nct, 0, 1 + self.lat_idx(i) // self.per_seq)


def _norm_mod_kernel(tok, xp_ref, xs_ref, pos_ref, sh_ref, sc_ref, g_ref, h_ref):
    i = pl.program_id(0)

    def modulate(x):
        y = _rms(x, g_ref[...])
        return (y * (1.0 + sc_ref[0]) + sh_ref[0]).astype(h_ref.dtype)

    @pl.when(i < tok.nct)
    def _():
        h_ref[...] = modulate(xp_ref[...])

    @pl.when(i >= tok.nct)
    def _():
        h_ref[...] = modulate(xs_ref[...] + pos_ref[...])


def _norm_mod(xp, xs, pos, mod, g1, tok):
    d = xp.shape[1]
    tm = tok.tm
    t_all = xp.shape[0] + xs.shape[0]
    return pl.pallas_call(
        functools.partial(_norm_mod_kernel, tok),
        out_shape=jax.ShapeDtypeStruct((t_all, d), BF16),
        grid=(tok.n,),
        in_specs=[
            pl.BlockSpec((tm, d), lambda i: (tok.ctx_idx(i), 0)),
            pl.BlockSpec((tm, d), lambda i: (tok.lat_idx(i), 0)),
            pl.BlockSpec((tm, d), lambda i: (tok.pos_idx(i), 0)),
            pl.BlockSpec((1, 1, d), lambda i: (tok.cond(i) * N_MOD + 0, 0, 0)),
            pl.BlockSpec((1, 1, d), lambda i: (tok.cond(i) * N_MOD + 1, 0, 0)),
            pl.BlockSpec((1, d), lambda i: (0, 0)),
        ],
        out_specs=pl.BlockSpec((tm, d), lambda i: (i, 0)),
        compiler_params=_cp(("parallel",)),
        name="norm_mod",
    )(xp, xs, pos, mod, mod, g1)


def _in_proj_kernel(n_sig, h_ref, w_ref, o_ref, wb_ref):
    n = pl.program_id(0)
    m = pl.program_id(1)

    @pl.when(m == 0)
    def _():
        wb_ref[...] = w_ref[...].astype(BF16)

    acc = _dot(h_ref[...], wb_ref[...])

    @pl.when(n >= n_sig)
    def _():
        o_ref[...] = acc.astype(o_ref.dtype)

    @pl.when(n < n_sig)
    def _():
        o_ref[...] = jax.nn.sigmoid(acc).astype(o_ref.dtype)


def _in_proj(h, w_in, sig_cols, lin_cols, out_dtype, tm, tn, name):
    t_all, d = h.shape
    for c in sig_cols + lin_cols:
        assert c % tn == 0
    assert t_all % tm == 0
    n_sig = (sig_cols[1] - sig_cols[0]) // tn
    n_lin = (lin_cols[1] - lin_cols[0]) // tn
    sig0, lin0 = sig_cols[0] // tn, lin_cols[0] // tn

    def w_blk(n):
        return jnp.where(n < n_sig, sig0 + n, lin0 + n - n_sig)

    return pl.pallas_call(
        functools.partial(_in_proj_kernel, n_sig),
        out_shape=jax.ShapeDtypeStruct((t_all, (n_sig + n_lin) * tn), out_dtype),
        grid=(n_sig + n_lin, t_all // tm),
        in_specs=[pl.BlockSpec((tm, d), lambda n, m: (m, 0)),
                  pl.BlockSpec((d, tn), lambda n, m: (0, w_blk(n)))],
        out_specs=pl.BlockSpec((tm, tn), lambda n, m: (m, n)),
        scratch_shapes=[pltpu.VMEM((d, tn), BF16)],
        compiler_params=_cp(("arbitrary", "arbitrary")),
        name=name,
    )(h, w_in)


CHUNK = 16
GROUP_PAD = 8


def _ssm_operators(a_re, a_im, log_dt, b_re, b_im, c_re, c_im, d_skip):
    t = CHUNK
    g, p = a_re.shape[1:]
    c = b_re.shape[-1]
    lam = lax.complex(a_re.astype(F32), a_im.astype(F32))
    dt = jnp.exp(log_dt.astype(F32))[..., None]
    ld = lam * dt
    lam_bar = jnp.exp(ld)
    b_bar = ((lam_bar - 1.0) / lam)[..., None] * lax.complex(b_re.astype(F32), b_im.astype(F32))
    c_mat = lax.complex(c_re.astype(F32), c_im.astype(F32))
    taus = jnp.arange(t + 1, dtype=F32)
    e_pow = jnp.exp(taus[None, None, :, None] * ld[:, :, None, :])

    k_all = jnp.real(jnp.einsum('dgop,dgtp,dgpi->dgtoi', c_mat, e_pow[:, :, :t], b_bar))
    kf, kb = k_all[0], k_all[1]
    ii = np.arange(t)[None, :, None]
    jj = np.arange(t)[None, None, :]
    tt = np.arange(t)[:, None, None]
    place_f = jnp.asarray((jj - ii == tt).astype(np.float32))
    place_b = jnp.asarray((ii - jj == tt).astype(np.float32))
    kf = kf.at[:, 0].add(jnp.eye(c, dtype=F32)[None] * d_skip.astype(F32).reshape(g, c, 1))
    m = jnp.einsum('tij,gtoc->gicjo', place_f, kf) + jnp.einsum('tij,gtoc->gicjo', place_b, kb)
    m = m.reshape(g, t * c, t * c)

    e_f = e_pow[0][:, ::-1][:, 1:]
    e_b = e_pow[1][:, :t]
    ws_f = e_f[:, :, :, None] * b_bar[0][:, None]
    ws_b = e_b[:, :, :, None] * b_bar[1][:, None]

    def rows_ic(x):
        return jnp.transpose(x, (0, 1, 3, 2)).reshape(g, t * c, p)

    w_state = jnp.concatenate([rows_ic(jnp.real(ws_f)), rows_ic(jnp.real(ws_b)),
                               rows_ic(jnp.imag(ws_f)), rows_ic(jnp.imag(ws_b))], axis=-1)

    ce_f = c_mat[0][:, None] * e_pow[0][:, 1:, None, :]
    ce_b = c_mat[1][:, None] * e_pow[1][:, ::-1][:, :t, None, :]

    def cols_jc(x):
        return jnp.transpose(x, (0, 3, 1, 2)).reshape(g, p, t * c)

    w_so = jnp.concatenate([cols_jc(jnp.real(ce_f)), cols_jc(jnp.real(ce_b)),
                            cols_jc(-jnp.imag(ce_f)), cols_jc(-jnp.imag(ce_b))], axis=1)
    w_out = jnp.concatenate([m, w_so], axis=1)

    a_t = e_pow[:, :, t]
    a_step = jnp.stack([jnp.concatenate([jnp.real(a_t[0]), jnp.real(a_t[1])], axis=-1),
                        jnp.concatenate([jnp.imag(a_t[0]), jnp.imag(a_t[1])], axis=-1)])
    return w_state.astype(BF16), w_out.astype(BF16), a_step


def _lane_perm(n_i, gb, c):
    n = n_i * gb * c
    src = np.arange(n)
    i, q, ch = src // (gb * c), (src // c) % gb, src % c
    dst = q * (n_i * c) + i * c + ch
    p = np.zeros((n, n), np.float32)
    p[src, dst] = 1.0
    return p


def _ssm_kernel(n_seq, n_chunk, gb, g_all, x_ref, ws_ref, wo_ref, a_ref, s0_ref, pin_ref, pout_ref,
                y_ref, fin_ref, xq_ref, za_ref, zb_ref):
    phase = pl.program_id(1)
    j = pl.program_id(2)
    gp = g_all + GROUP_PAD
    nck = n_seq * n_chunk
    half = LANES // 2
    hc = CHUNK // 2

    @pl.when(phase == 0)
    def _():
        halves = []
        for h in range(2):
            xcat = jnp.concatenate([x_ref[pl.ds(h * hc + i, nck, stride=CHUNK), :].astype(BF16)
                                    for i in range(hc)], axis=1)
            halves.append(_dot(xcat, pin_ref[...]).astype(BF16))
        for q in range(gb):
            xq = jnp.concatenate([hv[:, q * LANES:(q + 1) * LANES] for hv in halves], axis=1)
            xq_ref[j * gb + q] = xq
            z = _dot(xq, ws_ref[q])
            za_ref[pl.ds(j * gb + q, nck, stride=gp), :] = z[:, :LANES]
            zb_ref[pl.ds(j * gb + q, nck, stride=gp), :] = z[:, LANES:]

    @pl.when(jnp.logical_and(phase == 1, j == 0))
    def _():
        a_re = a_ref[0]
        a_im = a_ref[1]
        lane = lax.broadcasted_iota(jnp.int32, (g_all, LANES), 1)
        is_f = lane < half
        for s in range(n_seq):
            def step(k, carry):
                s_a, s_b = carry
                rf = pl.multiple_of((s * n_chunk + k) * gp, SUBLANES)
                rb = pl.multiple_of((s * n_chunk + n_chunk - 1 - k) * gp, SUBLANES)
                zf_a = za_ref[pl.ds(rf, g_all), :]
                zf_b = zb_ref[pl.ds(rf, g_all), :]
                zb_a = za_ref[pl.ds(rb, g_all), :]
                zb_b = zb_ref[pl.ds(rb, g_all), :]
                za_ref[pl.ds(rf, g_all), :] = jnp.where(is_f, s_a, zf_a)
                zb_ref[pl.ds(rf, g_all), :] = jnp.where(is_f, s_b, zf_b)
                za_ref[pl.ds(rb, g_all), :] = jnp.where(is_f, zb_a, s_a)
                zb_ref[pl.ds(rb, g_all), :] = jnp.where(is_f, zb_b, s_b)
                z_a = jnp.where(is_f, zf_a, zb_a)
                z_b = jnp.where(is_f, zf_b, zb_b)
                n_a = a_re * s_a - a_im * s_b + z_a
                n_b = a_re * s_b + a_im * s_a + z_b
                return n_a, n_b

            init = s0_ref[s]
            f_a, f_b = lax.fori_loop(0, n_chunk, step, (init[:, :LANES], init[:, LANES:]))
            fin_ref[s] = jnp.concatenate([f_a, f_b], axis=-1)

    @pl.when(phase == 1)
    def _():
        ys = []
        for q in range(gb):
            st_a = za_ref[pl.ds(j * gb + q, nck, stride=gp), :].astype(BF16)
            st_b = zb_ref[pl.ds(j * gb + q, nck, stride=gp), :].astype(BF16)
            lhs = jnp.concatenate([xq_ref[j * gb + q], st_a, st_b], axis=-1)
            ys.append(_dot(lhs, wo_ref[q]).astype(BF16))
        for h in range(2):
            ycat = jnp.concatenate([y[:, h * LANES:(h + 1) * LANES] for y in ys], axis=1)
            yall = _dot(ycat, pout_ref[...])
            for i in range(hc):
                y_ref[pl.ds(h * hc + i, nck, stride=CHUNK), :] = yall[:, i * LANES:(i + 1) * LANES]


def _ssm_call(u, row_blk0, n_rows, w_state, w_out, a_step, s0, n_seq, n_chunk, name):
    g_all, w, _ = w_state.shape
    nck = n_seq * n_chunk
    blk = nck * CHUNK
    n_blk = n_rows // blk
    gb = LANES * CHUNK // w
    ngb = g_all // gb
    gp = g_all + GROUP_PAD
    kern = functools.partial(_ssm_kernel, n_seq, n_chunk, gb, g_all)
    assert w == 2 * LANES
    perm = _lane_perm(CHUNK // 2, gb, LANES // gb)
    p_in = jnp.asarray(perm, BF16)
    p_out = jnp.asarray(perm.T, BF16)
    n_perm = perm.shape[0]
    const = lambda b, ph, j: (0, 0)
    return pl.pallas_call(
        kern,
        out_shape=(jax.ShapeDtypeStruct((n_rows, u.shape[1]), F32),
                   jax.ShapeDtypeStruct((n_blk * n_seq, g_all, w), F32)),
        grid=(n_blk, 2, ngb),
        in_specs=[
            pl.BlockSpec((blk, LANES), lambda b, ph, j: (row_blk0 + b, jnp.where(ph == 0, j, ngb - 1))),
            pl.BlockSpec((gb, w, w), lambda b, ph, j: (jnp.where(ph == 0, j, ngb - 1), 0, 0)),
            pl.BlockSpec((gb, 2 * w, w), lambda b, ph, j: (jnp.where(ph == 0, 0, j), 0, 0)),
            pl.BlockSpec((2, g_all, LANES), lambda b, ph, j: (0, 0, 0)),
            pl.BlockSpec((n_seq, g_all, w), lambda b, ph, j: (b, 0, 0)),
            pl.BlockSpec((n_perm, n_perm), const, pipeline_mode=pl.Buffered(1)),
            pl.BlockSpec((n_perm, n_perm), const, pipeline_mode=pl.Buffered(1)),
        ],
        out_specs=(
            pl.BlockSpec((blk, LANES), lambda b, ph, j: (b, jnp.where(ph == 0, 0, j))),
            pl.BlockSpec((n_seq, g_all, w), lambda b, ph, j: (b, 0, 0)),
        ),
        scratch_shapes=[pltpu.VMEM((g_all, nck, w), BF16),
                        pltpu.VMEM((nck * gp, LANES), F32), pltpu.VMEM((nck * gp, LANES), F32)],
        compiler_params=_cp(("arbitrary", "arbitrary", "arbitrary")),
        name=name,
    )(u, w_state, w_out, a_step, s0, p_in, p_out)


def _glu_kernel(nct, yc_ref, yl_ref, wg_ref, bg_ref, wb_ref, o_ref):
    i = pl.program_id(0)
    y = jax.nn.gelu(jnp.where(i < nct, yc_ref[...], yl_ref[...]))
    z = _dot(y.astype(BF16), wg_ref[...]) + bg_ref[...]
    ya = y * jax.nn.sigmoid(z)
    o_ref[...] = _dot(ya.astype(BF16), wb_ref[...]).astype(o_ref.dtype)


def _glu_branch(y_ctx, y_lat, w_glu, b_glu, w_ba, tok):
    c = y_ctx.shape[1]
    d = w_ba.shape[1]
    tm = tok.tm
    return pl.pallas_call(
        functools.partial(_glu_kernel, tok.nct),
        out_shape=jax.ShapeDtypeStruct((y_ctx.shape[0] + y_lat.shape[0], d), BF16),
        grid=(tok.n,),
        in_specs=[pl.BlockSpec((tm, c), lambda i: (tok.ctx_idx(i), 0)),
                  pl.BlockSpec((tm, c), lambda i: (tok.lat_idx(i), 0)),
                  pl.BlockSpec((c, c), lambda i: (0, 0)),
                  pl.BlockSpec((1, c), lambda i: (0, 0)),
                  pl.BlockSpec((c, d), lambda i: (0, 0))],
        out_specs=pl.BlockSpec((tm, d), lambda i: (i, 0)),
        compiler_params=_cp(("parallel",)),
        name="glu_branch",
    )(y_ctx, y_lat, w_glu, b_glu, w_ba)


def _short_conv_kernel(nct, per_ctx, per_lat, v_ref, x0_ref, x1_ref, vp_ref, x0p_ref, x1p_ref,
                       vn_ref, x0n_ref, x1n_ref, w_ref, b_ref, vx_ref, x0c_ref, scr):
    i = pl.program_id(0)
    tm = v_ref.shape[0]
    k = jnp.where(i < nct, i % per_ctx, (i - nct) % per_lat)
    per = jnp.where(i < nct, per_ctx, per_lat)
    first = k == 0
    last = k == per - 1
    row = lax.broadcasted_iota(jnp.int32, (tm, 1), 0)

    def conv(cur_ref, prev_ref, next_ref, part):
        x = cur_ref[...].astype(F32)
        hp = jnp.where(first, 0.0, prev_ref[SUBLANES - 1:SUBLANES, :].astype(F32))
        hn = jnp.where(last, 0.0, next_ref[0:1, :].astype(F32))
        xm = jnp.where(row == 0, hp, pltpu.roll(x, 1, axis=0))
        xq = jnp.where(row == tm - 1, hn, pltpu.roll(x, tm - 1, axis=0))
        w = w_ref[part]
        return b_ref[part] + xm * w[0:1, :] + x * w[1:2, :] + xq * w[2:3, :]

    v = conv(v_ref, vp_ref, vn_ref, 0)
    x0 = conv(x0_ref, x0p_ref, x0n_ref, 1)
    x1 = conv(x1_ref, x1p_ref, x1n_ref, 2)
    vx_ref[...] = _to_pair_rows(v * x1, scr).astype(vx_ref.dtype)
    x0c_ref[...] = _to_pair_rows(x0, scr).astype(x0c_ref.dtype)


def _short_conv(proj, col_blk0, hw, w_short3, b_short3, n_ctx, seq, dseq, tm):
    t = proj.shape[0]
    assert seq % tm == 0 and dseq % tm == 0
    nt = t // tm
    r8 = tm // SUBLANES
    nb8 = t // SUBLANES

    def cur(k):
        return pl.BlockSpec((tm, hw), lambda i: (i, col_blk0 + k))

    def prev(k):
        return pl.BlockSpec((SUBLANES, hw), lambda i: (jnp.maximum(i * r8 - 1, 0), col_blk0 + k))

    def nxt(k):
        return pl.BlockSpec((SUBLANES, hw), lambda i: (jnp.minimum((i + 1) * r8, nb8 - 1), col_blk0 + k))

    kern = functools.partial(_short_conv_kernel, n_ctx // tm, seq // tm, dseq // tm)
    return pl.pallas_call(
        kern,
        out_shape=(jax.ShapeDtypeStruct((t // 2, 2 * hw), BF16), jax.ShapeDtypeStruct((t // 2, 2 * hw), BF16)),
        grid=(nt,),
        in_specs=[cur(0), cur(1), cur(2), prev(0), prev(1), prev(2), nxt(0), nxt(1), nxt(2),
                  pl.BlockSpec((3, 3, hw), lambda i: (0, 0, 0)),
                  pl.BlockSpec((3, 1, hw), lambda i: (0, 0, 0))],
        out_specs=(pl.BlockSpec((tm // 2, 2 * hw), lambda i: (i, 0)),
                   pl.BlockSpec((tm // 2, 2 * hw), lambda i: (i, 0))),
        scratch_shapes=[pltpu.VMEM((hw // LANES * tm, LANES), F32)],
        compiler_params=_cp(("parallel",)),
        name="short_conv",
    )(proj, proj, proj, proj, proj, proj, proj, proj, proj, w_short3, b_short3)


def _filt_gen_kernel(seq_len, n_bands, wt_ref, wc_ref, ws_ref, b1_ref, f1_ref, w2_ref, b2_ref, f2_ref,
                     w3_ref, b3_ref, rate_ref, a_ref, bm_ref, abs_ref, scr):
    i = pl.program_id(0)
    tl = 2 * a_ref.shape[0]
    hw = a_ref.shape[1] // 2
    pos = lax.broadcasted_iota(jnp.int32, (tl, 1), 0) + i * tl
    t = pos.astype(F32) / seq_len
    bands = (lax.broadcasted_iota(jnp.int32, (1, wc_ref.shape[0]), 1) + 1).astype(F32)
    ang = (2.0 * math.pi) * t * bands
    pre = t * wt_ref[...] + _dot3(jnp.cos(ang), wc_ref[...]) + _dot3(jnp.sin(ang), ws_ref[...])
    h = jnp.sin(f1_ref[...] * (pre + b1_ref[...]))
    h = jnp.sin(f2_ref[...] * (_dot3(h, w2_ref[...]) + b2_ref[...]))
    h = _dot3(h, w3_ref[...]) + b3_ref[...]
    h = h * jnp.exp(-t * rate_ref[...])
    h_f = h[:, :hw]
    h_b = h[:, hw:]
    colabs = jnp.sum(jnp.abs(h_f) + jnp.abs(h_b), axis=0, keepdims=True)
    h_bp = jnp.where(pos == 0, 0.0, h_b)
    a_ref[...] = _to_pair_rows(h_f + h_bp, scr).astype(a_ref.dtype)
    bm_ref[...] = _to_pair_rows(h_bp - h_f, scr).astype(bm_ref.dtype)

    @pl.when(i == 0)
    def _():
        abs_ref[...] = colabs

    @pl.when(i > 0)
    def _():
        abs_ref[...] += colabs


def _filt_gen(seq_len, w_f1, b_f1, freq1, w_f2, b_f2, freq2, w_f3, b_f3, rates2):
    n_emb, hid_raw = w_f1.shape
    n_bands = (n_emb - 1) // 2
    hw2 = w_f3.shape[1]
    hw = hw2 // 2
    tl = min(seq_len, 512)
    hid = LANES
    assert n_bands <= LANES and hid_raw <= LANES

    def pad(x, rows, cols):
        return jnp.zeros((rows, cols), F32).at[:x.shape[0], :x.shape[1]].set(x.astype(F32))

    wt = pad(w_f1[0:1], 1, hid)
    wc = pad(w_f1[1:1 + n_bands], LANES, hid)
    ws = pad(w_f1[1 + n_bands:], LANES, hid)
    b_f1, freq1, b_f2, freq2 = (pad(x, 1, hid) for x in (b_f1, freq1, b_f2, freq2))
    w_f2 = pad(w_f2, hid, hid)
    w_f3 = pad(w_f3, hid, hw2)
    full = lambda shape: pl.BlockSpec(shape, lambda i: (0,) * len(shape))
    kern = functools.partial(_filt_gen_kernel, seq_len, n_bands)
    return pl.pallas_call(
        kern,
        out_shape=(jax.ShapeDtypeStruct((seq_len // 2, hw2), BF16), jax.ShapeDtypeStruct((seq_len // 2, hw2), BF16),
                   jax.ShapeDtypeStruct((1, hw), F32)),
        grid=(seq_len // tl,),
        in_specs=[full((1, hid)), full((LANES, hid)), full((LANES, hid)), full((1, hid)), full((1, hid)),
                  full((hid, hid)), full((1, hid)), full((1, hid)), full((hid, hw2)), full((1, hw2)),
                  full((1, hw2))],
        out_specs=(pl.BlockSpec((tl // 2, hw2), lambda i: (i, 0)), pl.BlockSpec((tl // 2, hw2), lambda i: (i, 0)),
                   full((1, hw))),
        scratch_shapes=[pltpu.VMEM((hw // LANES * tl, LANES), F32)],
        compiler_params=_cp(("arbitrary",)),
        name="filt_gen",
    )(wt, wc, ws, b_f1, freq1, w_f2, b_f2, freq2, w_f3, b_f3, rates2)


def _trig_tables(row_hi, row_lo, col, period):
    col = col[None, :]
    ang_lo = ((row_lo[:, None] * col) % period).astype(F32) * (2.0 * math.pi / period)
    ang_hi = ((row_hi[:, None] * col) % period).astype(F32) * (2.0 * math.pi / period)
    c_lo, s_lo = jnp.cos(ang_lo)[None], jnp.sin(ang_lo)[None]
    c_hi, s_hi = jnp.cos(ang_hi)[:, None], jnp.sin(ang_hi)[:, None]
    n = row_hi.shape[0] * row_lo.shape[0]
    ctab = (c_hi * c_lo - s_hi * s_lo).reshape(n, col.shape[1])
    stab = (s_hi * c_lo + c_hi * s_lo).reshape(n, col.shape[1])
    return ctab.astype(BF16), stab.astype(BF16)


def _dft_tables(seq_len):
    m = seq_len // 2
    r = 1
    while r * r < m:
        r *= 2
    assert m % r == 0
    i32 = jnp.int32
    idx = jnp.arange(m, dtype=i32)
    hi = jnp.arange(m // r, dtype=i32) * r
    lo = jnp.arange(r, dtype=i32)
    ce, se = _trig_tables(hi, lo, idx, 2 * m)
    co, so = _trig_tables(hi, lo, 2 * idx + 1, 2 * seq_len)
    cot, sot = _trig_tables(2 * hi, 2 * lo + 1, idx, 2 * seq_len)
    return ce, se, co, so, cot, sot


def _alt_row(n):
    return (1 - 2 * (lax.broadcasted_iota(jnp.int32, (SUBLANES, n), 1) % 2)).astype(BF16)


def _filt_dft_kernel(seq_len, ce_ref, se_ref, co_ref, so_ref, ae_ref, ao_ref, be_ref, bo_ref, abs_ref,
                     krl_ref, krh_ref, kil_ref, kih_ref, kmid_ref):
    i = pl.program_id(0)
    scale = (1.0 / seq_len) / (abs_ref[...] + EPS)
    ec = _dot(ce_ref[...], ae_ref[...])
    oc = _dot(co_ref[...], ao_ref[...])
    es = _dot(se_ref[...], be_ref[...])
    os_ = _dot(so_ref[...], bo_ref[...])
    krl_ref[...] = (ec + oc) * scale
    krh_ref[...] = (ec - oc) * scale
    kil_ref[...] = (es + os_) * scale
    kih_ref[...] = (os_ - es) * scale

    @pl.when(i == 0)
    def _():
        sgn = _alt_row(ae_ref.shape[0])
        kmid_ref[0:1, :] = _dot(sgn, ae_ref[...])[0:1] * scale
        kmid_ref[1:2, :] = _dot(sgn, bo_ref[...])[0:1] * scale


def _filt_dft(tabs, a2, bm2, colabs, seq_len):
    m = seq_len // 2
    hw = a2.shape[1] // 2
    ce, se, co, so = tabs[:4]
    tf = min(m, 256)
    kern = functools.partial(_filt_dft_kernel, seq_len)
    mode = pl.Buffered(1) if m * hw * 2 > BIG_BLOCK_BYTES // 2 else None
    tab = pl.BlockSpec((tf, m), lambda i: (i, 0))
    res = lambda c: pl.BlockSpec((m, hw), lambda i: (0, c), pipeline_mode=mode)
    out = pl.BlockSpec((tf, hw), lambda i: (i, 0))
    sds = jax.ShapeDtypeStruct((m, hw), F32)
    return pl.pallas_call(
        kern,
        out_shape=(sds, sds, sds, sds, jax.ShapeDtypeStruct((2, hw), F32)),
        grid=(m // tf,),
        in_specs=[tab, tab, tab, tab, res(0), res(1), res(0), res(1), pl.BlockSpec((1, hw), lambda i: (0, 0))],
        out_specs=(out, out, out, out, pl.BlockSpec((2, hw), lambda i: (0, 0))),
        compiler_params=_cp(("arbitrary",)),
        name="filt_dft",
    )(ce, se, co, so, a2, a2, bm2, bm2, colabs)


def _hy_fwd_kernel(ce_ref, se_ref, co_ref, so_ref, ve_ref, vo_ref, krl_ref, krh_ref, kil_ref, kih_ref, kmid_ref,
                   ae_ref, be_ref, ao_ref, bo_ref, ymid_ref):
    j = pl.program_id(1)
    tf, m = ce_ref.shape
    nb = ae_ref.shape[0]
    krl, krh, kil, kih = krl_ref[...], krh_ref[...], kil_ref[...], kih_ref[...]
    row = lax.broadcasted_iota(jnp.int32, (tf, 1), 0) + j * tf
    dc = jnp.where(row == 0, 0.5, 1.0)
    for s in range(nb):
        ve = ve_ref[s * m:(s + 1) * m, :]
        vo = vo_ref[s * m:(s + 1) * m, :]
        ec = _dot(ce_ref[...], ve)
        oc = _dot(co_ref[...], vo)
        es = _dot(se_ref[...], ve)
        os_ = _dot(so_ref[...], vo)
        p_lo, p_hi = ec + oc, ec - oc
        q_lo, q_hi = es + os_, os_ - es
        yre_lo = (p_lo * krl + q_lo * kil) * dc
        yim_lo = p_lo * kil - q_lo * krl
        yre_hi = (p_hi * krh + q_hi * kih) * dc
        yim_hi = p_hi * kih - q_hi * krh
        ae_ref[s] = (yre_lo + yre_hi).astype(ae_ref.dtype)
        be_ref[s] = (yim_lo - yim_hi).astype(be_ref.dtype)
        ao_ref[s] = (yre_lo - yre_hi).astype(ao_ref.dtype)
        bo_ref[s] = (yim_lo + yim_hi).astype(bo_ref.dtype)

    @pl.when(j == 0)
    def _():
        sgn = _alt_row(m)
        kr_m, ki_m = kmid_ref[0:1, :], kmid_ref[1:2, :]
        for s in range(nb):
            p_m = _dot(sgn, ve_ref[s * m:(s + 1) * m, :])[0:1]
            q_m = _dot(sgn, vo_ref[s * m:(s + 1) * m, :])[0:1]
            ymid_ref[s, 0:1, :] = p_m * kr_m + q_m * ki_m
            ymid_ref[s, 1:2, :] = p_m * ki_m - q_m * kr_m


BIG_BLOCK_BYTES = 4 * 1024 * 1024


def _seq_batch(seq_len, n_seq):
    nb = max(1, min(n_seq, 4 * MXU_SIDE // seq_len))
    while n_seq % nb:
        nb -= 1
    return nb


def _hy_fwd(tabs, vx2, kspec, n_seq, seq_len, row0_blk):
    m = seq_len // 2
    hw = vx2.shape[1] // 2
    ce, se, co, so = tabs[:4]
    krl, krh, kil, kih, kmid = kspec
    tf = min(m, 256)
    nb = _seq_batch(seq_len, n_seq)
    assert row0_blk % nb == 0
    mode = pl.Buffered(1) if nb * m * hw * 2 > BIG_BLOCK_BYTES // 2 else None
    tab = pl.BlockSpec((tf, m), lambda b, j: (j, 0))
    vspec = lambda c: pl.BlockSpec((nb * m, hw), lambda b, j: (row0_blk // nb + b, c), pipeline_mode=mode)
    kt = pl.BlockSpec((tf, hw), lambda b, j: (j, 0))
    out = pl.BlockSpec((nb, tf, hw), lambda b, j: (b, j, 0))
    sds = jax.ShapeDtypeStruct((n_seq, m, hw), BF16)
    return pl.pallas_call(
        _hy_fwd_kernel,
        out_shape=(sds, sds, sds, sds, jax.ShapeDtypeStruct((n_seq, 2, hw), F32)),
        grid=(n_seq // nb, m // tf),
        in_specs=[tab, tab, tab, tab, vspec(0), vspec(1), kt, kt, kt, kt,
                  pl.BlockSpec((2, hw), lambda b, j: (0, 0))],
        out_specs=(out, out, out, out, pl.BlockSpec((nb, 2, hw), lambda b, j: (b, 0, 0))),
        compiler_params=_cp(("arbitrary", "arbitrary")),
        name="hy_fwd",
    )(ce, se, co, so, vx2, vx2, krl, krh, kil, kih, kmid)


def _hy_inv_kernel(ce_ref, se_ref, cot_ref, sot_ref, ae_ref, be_ref, ao_ref, bo_ref, ymid_ref,
                   vx_ref, x0_ref, fb_ref, wb_ref, o_ref, scr):
    j = pl.program_id(1)
    tt = ce_ref.shape[0]
    nb = ae_ref.shape[0]
    hw = fb_ref.shape[1]
    row = lax.broadcasted_iota(jnp.int32, (tt, 1), 0) + j * tt
    sgn = (1 - 2 * (row % 2)).astype(F32)
    for s in range(nb):
        rows = slice(s * tt, (s + 1) * tt)
        ymid = ymid_ref[s]
        z_e = _dot(ce_ref[...], ae_ref[s]) - _dot(se_ref[...], be_ref[s]) + sgn * ymid[0:1]
        z_o = _dot(cot_ref[...], ao_ref[s]) - _dot(sot_ref[...], bo_ref[s]) - sgn * ymid[1:2]
        proj = []
        for par, z in ((0, z_e), (1, z_o)):
            cols = slice(par * hw, (par + 1) * hw)
            z = z + vx_ref[rows, cols].astype(F32) * fb_ref[...]
            yb = x0_ref[rows, cols].astype(F32) * z
            proj.append(_dot(yb.astype(BF16), wb_ref[...]))
        o_ref[2 * s * tt:2 * (s + 1) * tt, :] = _from_pair_rows(
            jnp.concatenate(proj, axis=1), scr).astype(o_ref.dtype)


def _hy_inv(tabs, spec, vx2, x0c2, fbias, w_bb, n_seq, seq_len, row0_blk):
    m = seq_len // 2
    hw = fbias.shape[1]
    d = w_bb.shape[1]
    ce, se, _, _, cot, sot = tabs
    ae, be, ao, bo, ymid = spec
    tt = min(m, 128)
    per = m // tt
    nb = _seq_batch(seq_len, n_seq) if per == 1 else 1
    assert row0_blk % nb == 0
    mode = pl.Buffered(1) if nb * m * hw * 2 > BIG_BLOCK_BYTES // 2 else None
    tab = pl.BlockSpec((tt, m), lambda b, j: (j, 0))
    res = pl.BlockSpec((nb, m, hw), lambda b, j: (b, 0, 0), pipeline_mode=mode)
    rows_in = pl.BlockSpec((nb * tt, 2 * hw), lambda b, j: ((row0_blk // nb + b) * per + j, 0))
    return pl.pallas_call(
        _hy_inv_kernel,
        out_shape=jax.ShapeDtypeStruct((n_seq * seq_len, d), BF16),
        grid=(n_seq // nb, per),
        in_specs=[tab, tab, tab, tab, res, res, res, res,
                  pl.BlockSpec((nb, 2, hw), lambda b, j: (b, 0, 0)),
                  rows_in, rows_in,
                  pl.BlockSpec((1, hw), lambda b, j: (0, 0)),
                  pl.BlockSpec((hw, d), lambda b, j: (0, 0), pipeline_mode=pl.Buffered(1))],
        out_specs=pl.BlockSpec((nb * 2 * tt, d), lambda b, j: (b * per + j, 0)),
        scratch_shapes=[pltpu.VMEM((d // LANES * 2 * tt, LANES), F32)],
        compiler_params=_cp(("arbitrary", "arbitrary")),
        name="hy_inv",
    )(ce, se, cot, sot, ae, be, ao, bo, ymid, vx2, x0c2, fbias, w_bb)


MERGE_COL_BLOCKS = 4

def _expert_pairs(epg):
    pairs = [(a, b) for a in range(epg) for b in range(a + 1, epg)]
    order = [pairs.pop(0)]
    while pairs:
        last = order[-1]
        nxt = next((p for p in pairs if set(p) & set(last)), pairs[0])
        pairs.remove(nxt)
        order.append(nxt)
    return order


def _merge_kernel(tok, n_groups, epg, xp_ref, xs_ref, pos_ref, g0_ref, g1_ref, pa_ref, pbc_ref, pbs_ref,
                  gate1_ref, sh2_ref, sc2_ref, gn_ref, wo_ref, wr_ref, br_ref,
                  x2_ref, h2_ref, rt_ref):
    i = pl.program_id(0)
    is_ctx = i < tok.nct
    tm, d = x2_ref.shape
    pb = jnp.where(is_ctx, pbc_ref[...], pbs_ref[...])
    merged = g0_ref[...] * pa_ref[...] + g1_ref[...] * pb
    cb = d // MERGE_COL_BLOCKS
    ssq = jnp.zeros((tm, 1), F32)
    for c in range(MERGE_COL_BLOCKS):
        cols = slice(c * cb, (c + 1) * cb)
        att = _dot(merged, wo_ref[:, cols])
        x = jnp.where(is_ctx, xp_ref[:, cols], xs_ref[:, cols] + pos_ref[:, cols])
        x2c = x + gate1_ref[0][:, cols] * att
        x2_ref[:, cols] = x2c
        ssq = ssq + jnp.sum(x2c * x2c, axis=-1, keepdims=True)
    x2 = x2_ref[...]
    h2 = x2 * lax.rsqrt(ssq * (1.0 / d) + EPS) * gn_ref[...] * (1.0 + sc2_ref[0]) + sh2_ref[0]
    h2b = h2.astype(BF16)
    _store_row_tiles(h2_ref, 0, h2b)
    logits = _dot(h2b, wr_ref[...]) + br_ref[...]
    lane = lax.broadcasted_iota(jnp.int32, logits.shape, 1)
    gl = jnp.where(lane < n_groups, logits, -jnp.inf)
    gmax = jnp.max(gl, axis=-1, keepdims=True)
    gidx = jnp.min(jnp.where(gl == gmax, lane, n_groups), axis=-1, keepdims=True)
    base = n_groups + gidx * epg
    el = jnp.where(jnp.logical_and(lane >= base, lane < base + epg), logits, -jnp.inf)
    i1 = jnp.min(jnp.where(el == jnp.max(el, axis=-1, keepdims=True), lane, LANES), axis=-1, keepdims=True)
    el = jnp.where(lane == i1, -jnp.inf, el)
    i2 = jnp.min(jnp.where(el == jnp.max(el, axis=-1, keepdims=True), lane, LANES), axis=-1, keepdims=True)
    lo = jnp.minimum(i1, i2) - base
    hi = jnp.maximum(i1, i2) - base
    cls = jnp.zeros_like(gidx)
    for code, (pa_, pb_) in enumerate(_expert_pairs(epg)):
        cls = cls + code * jnp.logical_and(lo == pa_, hi == pb_).astype(jnp.int32)
    rt_ref[...] = jnp.where(lane == LANES - 1, gidx.astype(F32),
                            jnp.where(lane == LANES - 2, cls.astype(F32), logits))


def _merge_out(xp, xs, pos, proj, gate_blk0, pa, pb_ctx, pb_lat, mod, g2, w_out, w_r, b_r, n_groups, epg, tok):
    d = xp.shape[1]
    tm = tok.tm
    t_all = xp.shape[0] + xs.shape[0]
    kern = functools.partial(_merge_kernel, tok, n_groups, epg)
    one = pl.Buffered(1)
    ns = d // 2 // LANES

    def modspec(k):
        return pl.BlockSpec((1, 1, d), lambda i: (tok.cond(i) * N_MOD + k, 0, 0))

    return pl.pallas_call(
        kern,
        out_shape=(jax.ShapeDtypeStruct((t_all, d), F32), jax.ShapeDtypeStruct((t_all * ns, LANES), jnp.uint32),
                   jax.ShapeDtypeStruct((t_all, LANES), F32)),
        grid=(tok.n,),
        in_specs=[
            pl.BlockSpec((tm, d), lambda i: (tok.ctx_idx(i), 0)),
            pl.BlockSpec((tm, d), lambda i: (tok.lat_idx(i), 0)),
            pl.BlockSpec((tm, d), lambda i: (tok.pos_idx(i), 0)),
            pl.BlockSpec((tm, d), lambda i: (i, gate_blk0)),
            pl.BlockSpec((tm, d), lambda i: (i, gate_blk0 + 1)),
            pl.BlockSpec((tm, d), lambda i: (i, 0)),
            pl.BlockSpec((tm, d), lambda i: (tok.ctx_idx(i), 0)),
            pl.BlockSpec((tm, d), lambda i: (tok.lat_idx(i), 0)),
            modspec(2), modspec(3), modspec(4),
            pl.BlockSpec((1, d), lambda i: (0, 0)),
            pl.BlockSpec((d, d), lambda i: (0, 0), pipeline_mode=one),
            pl.BlockSpec((d, LANES), lambda i: (0, 0)),
            pl.BlockSpec((1, LANES), lambda i: (0, 0)),
        ],
        out_specs=(pl.BlockSpec((tm, d), lambda i: (i, 0)), pl.BlockSpec((tm * ns, LANES), lambda i: (i, 0)),
                   pl.BlockSpec((tm, LANES), lambda i: (i, 0))),
        compiler_params=_cp(("parallel",)),
        name="merge_out",
    )(xp, xs, pos, proj, proj, pa, pb_ctx, pb_lat, mod, mod, mod, g2, w_out, w_r, b_r)


ROW_UNROLL = 8

def _moe_kernel(n_tiles, n_groups, epg, tg_ref, tnv_ref, rt_ref, cb_ref, kx_ref, wi_ref,
                h2_hbm, wr_ref, br_ref, wg_ref, wu_ref, wd_ref, out_hbm,
                xbuf, xb_ref, acc, obuf, gws, gsem, ssem):
    i = pl.program_id(0)
    j = pl.program_id(1)
    tm = xb_ref.shape[0]
    ns = h2_hbm.shape[1]
    nv = tnv_ref[i]
    slot = i % 2

    def buf_rows(buf, slot_, r, rows):
        return buf.at[pl.ds(pl.multiple_of((slot_ * tm + r) * ns, ns), rows * ns), :]

    def gather_copy(tok_row, slot_, r):
        return pltpu.make_async_copy(h2_hbm.at[tok_row], buf_rows(xbuf, slot_, r, 1), gsem.at[slot_])

    def gather_wait(slot_, rows):
        dst = buf_rows(xbuf, slot_, 0, rows)
        pltpu.make_async_copy(dst, dst, gsem.at[slot_]).wait()

    def scatter_copy(tok_row, slot_, r):
        return pltpu.make_async_copy(buf_rows(obuf, slot_, r, 1), out_hbm.at[tok_row], ssem.at[0])

    def scatter_wait(slot_, rows):
        src = buf_rows(obuf, slot_, 0, rows)
        pltpu.make_async_copy(src, src, ssem.at[0]).wait()

    def for_rows(count, fn):
        n_full = count // ROW_UNROLL

        def body(b, c):
            for k in range(ROW_UNROLL):
                fn(b * ROW_UNROLL + k)
            return c

        def tail(r, c):
            fn(r)
            return c

        lax.fori_loop(0, n_full, body, 0)
        lax.fori_loop(n_full * ROW_UNROLL, count, tail, 0)

    def wait_scatter(count, slot_):
        p = tm
        while p >= 1:
            @pl.when((count & p) != 0)
            def _(p=p):
                scatter_wait(slot_, p)
            p //= 2

    nv_prev = tnv_ref[jnp.maximum(i - 1, 0)]
    nv_next = tnv_ref[jnp.minimum(i + 1, n_tiles - 1)]
    prev_deferred = jnp.logical_and(i > 0, jnp.logical_and(nv_prev == tm, nv > 0))
    deferred = jnp.logical_and(nv == tm, nv_next > 0)

    @pl.when(j == 0)
    def _():
        @pl.when(i == 0)
        def _():
            for_rows(tm, lambda r: gather_copy(rt_ref[r], 0, r).start())

        @pl.when(jnp.logical_or(i == 0, nv_prev > 0))
        def _():
            gather_wait(slot, tm)

    @pl.when(jnp.logical_and(j == 0, nv > 0))
    def _():
        grp = tg_ref[i]
        xb = _load_row_tiles(xbuf, slot * tm * ns, tm, ns)
        xb_ref[...] = xb
        logits = _dot(xb, wr_ref[...]) + br_ref[...]
        lane = lax.broadcasted_iota(jnp.int32, logits.shape, 1)
        gl = jnp.where(lane < n_groups, logits, -jnp.inf)
        gmax = jnp.max(gl, axis=-1, keepdims=True)
        ge = jnp.exp(gl - gmax)
        g_w = jnp.sum(jnp.where(lane == grp, ge, 0.0), axis=-1, keepdims=True) / jnp.sum(ge, axis=-1, keepdims=True)
        base = n_groups + grp * epg
        e = [jnp.sum(jnp.where(lane == base + k, logits, 0.0), axis=-1, keepdims=True) for k in range(epg)]
        pairs = _expert_pairs(epg)
        row = lax.broadcasted_iota(jnp.int32, (tm, 1), 0)
        cls = jnp.zeros((tm, 1), jnp.int32)
        for c in range(len(pairs) - 1):
            cls = cls + (row >= cb_ref[i * len(pairs) + c]).astype(jnp.int32)
        sel = [functools.reduce(jnp.logical_or, [cls == c for c, p in enumerate(pairs) if k in p])
               for k in range(epg)]
        emax = functools.reduce(jnp.maximum, [jnp.where(sel[k], e[k], -jnp.inf) for k in range(epg)])
        ex = [jnp.where(sel[k], jnp.exp(e[k] - emax), 0.0) for k in range(epg)]
        den = functools.reduce(lambda a, b: a + b, ex)
        for k in range(epg):
            gws[k] = ex[k] / den * g_w
        acc[...] = jnp.zeros(acc.shape, acc.dtype)

    kx = kx_ref[i * epg + j]

    def expert_step(issue, scatter_prev):
        per = tm // TOP_K_INNER
        if issue:
            for k in range(per):
                r = kx * per + k
                if scatter_prev:
                    scatter_copy(rt_ref[(i - 1) * tm + r], 1 - slot, r).start()
                gather_copy(rt_ref[(i + 1) * tm + r], 1 - slot, r).start()
        xb = xb_ref[...]
        g = _dot(xb, wg_ref[...].astype(BF16))
        u = _dot(xb, wu_ref[...].astype(BF16))
        hid = (g * jax.nn.sigmoid(g)) * u * gws[j]
        acc[...] += _dot(hid.astype(BF16), wd_ref[...].astype(BF16))

    run = jnp.logical_and(nv > 0, kx >= 0)
    carries = kx < TOP_K_INNER

    @pl.when(jnp.logical_and(run, jnp.logical_and(carries, prev_deferred)))
    def _():
        expert_step(True, True)

    @pl.when(jnp.logical_and(run, jnp.logical_and(carries, jnp.logical_not(prev_deferred))))
    def _():
        expert_step(True, False)

    @pl.when(jnp.logical_and(run, jnp.logical_not(carries)))
    def _():
        expert_step(False, False)

    @pl.when(j == epg - 1)
    def _():
        @pl.when(prev_deferred)
        def _():
            scatter_wait(1 - slot, tm)

        @pl.when(nv > 0)
        def _():
            _store_row_tiles(obuf, slot * tm * ns, acc[...].astype(BF16))

        @pl.when(jnp.logical_and(nv > 0, jnp.logical_not(deferred)))
        def _():
            for_rows(nv, lambda r: scatter_copy(rt_ref[i * tm + r], slot, r).start())
            wait_scatter(nv, slot)


def _moe(h2p, w_r, b_r, w_g, w_u, w_d, plan, n_groups, epg, tm):
    t, ns, _ = h2p.shape
    d = 2 * ns * LANES
    ff = w_g.shape[-1]
    n_tiles = plan[0].shape[0]
    assert tm & (tm - 1) == 0, "row-count waits decompose tm in binary"
    assert tm % TOP_K_INNER == 0
    kern = functools.partial(_moe_kernel, n_tiles, n_groups, epg)

    def const(i, j, *_):
        return (0, 0)

    def wblk(i, j, tg, tnv, rt, cb, kx, wi):
        return (wi[i * epg + j], 0, 0)

    grid_spec = pltpu.PrefetchScalarGridSpec(
        num_scalar_prefetch=len(plan),
        grid=(n_tiles, epg),
        in_specs=[
            pl.BlockSpec(memory_space=pl.ANY),
            pl.BlockSpec((d, LANES), const),
            pl.BlockSpec((1, LANES), const),
            pl.BlockSpec((None, d, ff), wblk),
            pl.BlockSpec((None, d, ff), wblk),
            pl.BlockSpec((None, ff, d), wblk),
        ],
        out_specs=pl.BlockSpec(memory_space=pl.ANY),
        scratch_shapes=[
            pltpu.VMEM((2 * tm * ns, LANES), jnp.uint32),
            pltpu.VMEM((tm, d), BF16),
            pltpu.VMEM((tm, d), F32),
            pltpu.VMEM((2 * tm * ns, LANES), jnp.uint32),
            pltpu.VMEM((epg, tm, 1), F32),
            pltpu.SemaphoreType.DMA((2,)),
            pltpu.SemaphoreType.DMA((1,)),
        ],
    )
    return pl.pallas_call(
        kern,
        out_shape=jax.ShapeDtypeStruct((t, ns, LANES), jnp.uint32),
        grid_spec=grid_spec,
        compiler_params=_cp(("arbitrary", "arbitrary")),
        name="moe",
    )(*plan, h2p, w_r, b_r, w_g, w_u, w_d)


def _moe_plan(grp, cls, n_groups, epg, tm, n_tiles):
    i32 = jnp.int32
    t = grp.shape[0]
    pairs = _expert_pairs(epg)
    n_cls = len(pairs)
    key = grp * n_cls + cls
    onehot = (key[:, None] == jnp.arange(n_groups * n_cls, dtype=i32)[None, :]).astype(i32)
    csum = jnp.cumsum(onehot, axis=0)
    kcount = csum[-1].reshape(n_groups, n_cls)
    rank = jnp.take_along_axis(csum, key[:, None], axis=1)[:, 0] - 1
    counts = jnp.sum(kcount, axis=1)
    cls_end = jnp.cumsum(kcount, axis=1)
    cls_off = cls_end - kcount
    ntile_g = (counts + tm - 1) // tm
    tile_end = jnp.cumsum(ntile_g)
    tile_off = tile_end - ntile_g
    slot = tile_off[grp] * tm + cls_off[grp, cls] + rank
    row_tok = jnp.zeros((n_tiles * tm,), i32).at[slot].set(jnp.arange(t, dtype=i32))
    tiles = jnp.arange(n_tiles, dtype=i32)
    tile_grp = jnp.minimum(jnp.sum((tiles[:, None] >= tile_end[None, :]).astype(i32), axis=1), n_groups - 1)
    row0 = (tiles - tile_off[tile_grp]) * tm
    tile_nv = jnp.clip(counts[tile_grp] - row0, 0, tm)
    tile_nv = jnp.where(tiles < tile_end[-1], tile_nv, 0)
    cb = jnp.clip(cls_end[tile_grp] - row0[:, None], 0, tm)
    rows_in = jnp.diff(jnp.minimum(cb, tile_nv[:, None]), axis=1, prepend=0)
    member = jnp.asarray(np.array([[int(k in p) for k in range(epg)] for p in pairs], np.int32))
    needed = ((rows_in > 0).astype(i32) @ member) > 0
    kx = jnp.where(needed, jnp.cumsum(needed.astype(i32), axis=1) - 1, -1)
    blk = (tile_grp[:, None] * epg + jnp.arange(epg, dtype=i32)[None, :]).reshape(-1)
    pos = jnp.arange(n_tiles * epg, dtype=i32)
    last = lax.cummax(jnp.where(needed.reshape(-1), pos, -1), axis=0)
    wi = jnp.where(last >= 0, blk[jnp.maximum(last, 0)], 0)
    return (tile_grp.astype(i32), tile_nv.astype(i32), row_tok, cb.reshape(-1).astype(i32),
            kx.reshape(-1).astype(i32), wi.astype(i32))


def _final_kernel(x2_ref, m_ref, gate2_ref, gf_ref, o_ref):
    tm, d = x2_ref.shape
    ns = d // 2 // LANES
    m = _load_row_tiles(m_ref, 0, tm, ns).astype(F32)
    x3 = x2_ref[...] + gate2_ref[0] * m
    o_ref[...] = _rms(x3, gf_ref[...])


def _final(x2, moe, mod, g_final, row_blk0, n_rows, cond_of_tile, tm):
    d = x2.shape[1]
    ns = d // 2 // LANES
    return pl.pallas_call(
        _final_kernel,
        out_shape=jax.ShapeDtypeStruct((n_rows, d), F32),
        grid=(n_rows // tm,),
        in_specs=[pl.BlockSpec((tm, d), lambda i: (row_blk0 + i, 0)),
                  pl.BlockSpec((tm * ns, LANES), lambda i: (row_blk0 + i, 0)),
                  pl.BlockSpec((1, 1, d), lambda i: (cond_of_tile(i) * N_MOD + 5, 0, 0)),
                  pl.BlockSpec((1, d), lambda i: (0, 0))],
        out_specs=pl.BlockSpec((tm, d), lambda i: (i, 0)),
        compiler_params=_cp(("parallel",)),
        name="final",
    )(x2, moe, mod, g_final)


def _grid_pos_embed(n_tokens, dim):
    rows_n = n_tokens // GRID_W
    quarter = dim // 4
    omega = 1.0 / (10000.0 ** (jnp.arange(quarter, dtype=F32) / quarter))

    def emb(n):
        ang = jnp.arange(n).astype(F32)[:, None] * omega[None, :]
        return jnp.concatenate([jnp.sin(ang), jnp.cos(ang)], axis=-1)

    e_r = jnp.broadcast_to(emb(rows_n)[:, None, :], (rows_n, GRID_W, dim // 2))
    e_c = jnp.broadcast_to(emb(GRID_W)[None, :, :], (rows_n, GRID_W, dim // 2))
    return jnp.concatenate([e_r, e_c], axis=-1).reshape(n_tokens, dim)


def _state_to_lanes(s):
    b, _, _, g, p = s.shape
    return jnp.transpose(s, (0, 3, 2, 1, 4)).reshape(b, g, 4 * p)


def _lanes_to_state(x, p):
    b, g, _ = x.shape
    return jnp.transpose(x.reshape(b, g, 2, 2, p), (0, 3, 2, 1, 4))


def kernel(x_prompt, x_sample, state_ssm, c, c_ctx, w_ada, b_ada, g_norm1, g_norm2, w_in, ssm_a_re, ssm_a_im, ssm_log_dt, ssm_b_re, ssm_b_im, ssm_c_re, ssm_c_im, ssm_d, w_glu, b_glu, w_short, b_short, w_f1, b_f1, freq1, w_f2, b_f2, freq2, w_f3, b_f3, filter_bias, w_branch_a, w_branch_b, w_out, w_router_group, b_router_group, w_router_expert, b_router_expert, w_exp_gate, w_exp_up, w_exp_down, g_final):
    depth = w_ada.shape[0]
    assert depth == 1, "single-layer pipeline"
    bsz, seq, d = x_prompt.shape
    dbsz, dseq, _ = x_sample.shape
    g_all, p_state = ssm_a_re.shape[2:]
    ssm_w = ssm_d.shape[1]
    hw = filter_bias.shape[1]
    n_groups, _, epg = w_router_expert.shape[1:]
    ff = w_exp_gate.shape[-1]
    assert ssm_w // g_all * CHUNK == MXU_SIDE and 4 * p_state == MXU_SIDE
    assert ssm_w == hw and d == 2 * hw
    n_ctx = bsz * seq
    n_lat = dbsz * dseq
    t_all = n_ctx + n_lat

    xp = x_prompt.reshape(n_ctx, d)
    xs = x_sample.reshape(n_lat, d)
    pos = _grid_pos_embed(dseq, d)

    n_cond = 1 + dbsz
    cond8 = jnp.zeros((SUBLANES, d), F32).at[0].set(c_ctx).at[1:n_cond].set(c)
    mod = _ada_mod(cond8, w_ada[0], b_ada[0][None])
    mod = mod[:n_cond].reshape(n_cond * N_MOD, 1, d)

    tm_e = min(256, seq, dseq)
    tok_e = _Tok(n_ctx, n_lat, dseq, tm_e)
    tm_l = min(512, n_ctx, dseq)
    tok_l = _Tok(n_ctx, n_lat, dseq, tm_l)
    gate_col0 = ssm_w + 3 * hw
    h1 = _norm_mod(xp, xs, pos, mod, g_norm1, tok_l)
    tm_p, tn_p = min(1024, t_all), min(1024, hw)
    n_in = w_in.shape[2]
    u_a = _in_proj(h1, w_in[0], (0, 0), (0, ssm_w), F32, tm_p, tn_p, "in_proj_a")
    proj = _in_proj(h1, w_in[0], (gate_col0, n_in), (ssm_w, gate_col0), BF16, tm_p, tn_p, "in_proj")
    n_gate_cols = n_in - gate_col0

    w_state, w_so, a_step = _ssm_operators(ssm_a_re[0], ssm_a_im[0], ssm_log_dt[0], ssm_b_re[0], ssm_b_im[0],
                                           ssm_c_re[0], ssm_c_im[0], ssm_d[0])
    nc_ctx = seq // CHUNK
    nc_lat = dseq // CHUNK
    seq_per_blk = max(1, min(bsz, MXU_SIDE // nc_ctx))
    assert bsz % seq_per_blk == 0 and n_ctx % dseq == 0
    s0_ctx = jnp.zeros((bsz, g_all, 4 * p_state), F32)
    y_ctx, fin_ctx = _ssm_call(u_a, 0, n_ctx, w_state, w_so, a_step, s0_ctx, seq_per_blk, nc_ctx, "ssm_ctx")
    s0_lat = _state_to_lanes(state_ssm[:, 0].astype(F32))
    y_lat, _ = _ssm_call(u_a, n_ctx // dseq, n_lat, w_state, w_so, a_step, s0_lat, 1, nc_lat, "ssm_lat")
    new_state = _lanes_to_state(fin_ctx, p_state)[:, None]

    pa = _glu_branch(y_ctx, y_lat, w_glu[0].astype(BF16), b_glu[0][None], w_branch_a[0].astype(BF16), tok_l)

    w_short3 = jnp.transpose(w_short[0].reshape(-1, 3, hw), (1, 0, 2))
    b_short3 = b_short[0].reshape(3, 1, hw)
    vx2, x0c2 = _short_conv(proj, n_gate_cols // hw, hw, w_short3, b_short3, n_ctx, seq, dseq, tm_e)
    rates = jnp.abs(jnp.linspace(math.log(DECAY_TARGET) / DECAY_FAST, math.log(DECAY_TARGET) / DECAY_SLOW,
                                 hw, dtype=F32))
    rates2 = jnp.concatenate([rates, rates])[None]
    fbias = filter_bias[0][None].astype(F32)
    w_bb = w_branch_b[0].astype(BF16)
    pbs = []
    for seq_len, n_seq, row0 in ((seq, bsz, 0), (dseq, dbsz, n_ctx // dseq)):
        tabs = _dft_tables(seq_len)
        a2, bm2, colabs = _filt_gen(seq_len, w_f1[0], b_f1[0][None], freq1[0][None], w_f2[0],
                                    b_f2[0][None], freq2[0][None], w_f3[0], b_f3[0][None], rates2)
        kspec = _filt_dft(tabs, a2, bm2, colabs, seq_len)
        spec = _hy_fwd(tabs, vx2, kspec, n_seq, seq_len, row0)
        pbs.append(_hy_inv(tabs, spec, vx2, x0c2, fbias, w_bb, n_seq, seq_len, row0))
    pb_ctx, pb_lat = pbs

    w_r = jnp.zeros((d, LANES), F32)
    w_r = w_r.at[:, :n_groups].set(w_router_group[0])
    w_r = w_r.at[:, n_groups:n_groups + n_groups * epg].set(
        jnp.transpose(w_router_expert[0], (1, 0, 2)).reshape(d, n_groups * epg))
    b_r = jnp.zeros((1, LANES), F32)
    b_r = b_r.at[0, :n_groups].set(b_router_group[0])
    b_r = b_r.at[0, n_groups:n_groups + n_groups * epg].set(b_router_expert[0].reshape(-1))
    w_r = w_r.astype(BF16)
    x2, h2, rout = _merge_out(xp, xs, pos, proj, 0, pa, pb_ctx, pb_lat, mod, g_norm2,
                              w_out[0].astype(BF16), w_r, b_r, n_groups, epg, tok_e)

    tm_m = min(512, t_all // n_groups)
    n_tiles = t_all // tm_m + n_groups
    grp = rout[:, LANES - 1].astype(jnp.int32)
    cls = rout[:, LANES - 2].astype(jnp.int32)
    plan = _moe_plan(grp, cls, n_groups, epg, tm_m, n_tiles)
    ns = d // 2 // LANES
    moe = _moe(h2.reshape(t_all, ns, LANES), w_r, b_r, w_exp_gate[0], w_exp_up[0], w_exp_down[0],
               plan, n_groups, epg, tm_m)
    moe = moe.reshape(t_all * ns, LANES)

    gf = g_final[None]
    y_prompt = _final(x2, moe, mod, gf, 0, n_ctx, lambda i: 0, tm_l)
    lat_tiles = dseq // tm_l
    y_sample = _final(x2, moe, mod, gf, n_ctx // tm_l, n_lat, lambda i: 1 + i // lat_tiles, tm_l)
    return (y_prompt.reshape(bsz, seq, d), y_sample.reshape(dbsz, dseq, d), new_state)
```

```python
import functools
import math

import numpy as np
import jax
import jax.numpy as jnp
from jax import lax
from jax.experimental import pallas as pl
from jax.experimental.pallas import tpu as pltpu

F32 = jnp.float32
BF16 = jnp.bfloat16
EPS = 1e-6
GRID_W = 64
N_MOD = 6
TOP_K_INNER = 2
DECAY_TARGET = 1e-2
DECAY_FAST = 0.3
DECAY_SLOW = 1.5
LANES = 128
SUBLANES = 8
MXU_SIDE = 256
VMEM_LIMIT = 56 * 1024 * 1024


def _cp(sem, vmem=VMEM_LIMIT):
    return pltpu.CompilerParams(dimension_semantics=sem, vmem_limit_bytes=vmem)


def _dot(a, b):
    return jnp.dot(a, b, preferred_element_type=F32)


def _split(a):
    hi = a.astype(BF16)
    lo = (a - hi.astype(F32)).astype(BF16)
    return hi, lo


def _dot3(a, b):
    a_hi, a_lo = _split(a)
    b_hi, b_lo = _split(b)
    return _dot(a_hi, b_hi) + _dot(a_lo, b_hi) + _dot(a_hi, b_lo)


def _rms(x, g):
    ms = jnp.mean(x * x, axis=-1, keepdims=True)
    return x * lax.rsqrt(ms + EPS) * g


def _pack_halves(xb):
    n = xb.shape[1] // 2
    lo = lax.bitcast_convert_type(xb[:, :n].astype(F32), jnp.uint32)
    hi = lax.bitcast_convert_type(xb[:, n:].astype(F32), jnp.uint32)
    return (lo >> 16) | (hi & jnp.uint32(0xFFFF0000))


def _unpack_halves(w):
    lo = lax.bitcast_convert_type(w << 16, F32)
    hi = lax.bitcast_convert_type(w & jnp.uint32(0xFFFF0000), F32)
    return jnp.concatenate([lo, hi], axis=1).astype(BF16)


def _to_pair_rows(x, scr):
    m, n = x.shape
    ev, od = [], []
    for c in range(n // LANES):
        scr[c * m:(c + 1) * m, :] = x[:, c * LANES:(c + 1) * LANES]
        ev.append(scr[pl.ds(c * m, m // 2, stride=2), :])
        od.append(scr[pl.ds(c * m + 1, m // 2, stride=2), :])
    return jnp.concatenate(ev + od, axis=1)


def _from_pair_rows(x2, scr):
    h, n2 = x2.shape
    n = n2 // 2
    out = []
    for c in range(n // LANES):
        scr[pl.ds(2 * c * h, h, stride=2), :] = x2[:, c * LANES:(c + 1) * LANES]
        scr[pl.ds(2 * c * h + 1, h, stride=2), :] = x2[:, n + c * LANES:n + (c + 1) * LANES]
        out.append(scr[2 * c * h:2 * (c + 1) * h, :])
    return jnp.concatenate(out, axis=1)


def _store_row_tiles(ref, row0, xb):
    w = _pack_halves(xb)
    m, n = w.shape
    ns = n // LANES
    for c in range(ns):
        ref[pl.ds(row0 + c, m, stride=ns), :] = w[:, c * LANES:(c + 1) * LANES]


def _load_row_tiles(ref, row0, m, ns):
    w = jnp.concatenate([ref[pl.ds(row0 + c, m, stride=ns), :] for c in range(ns)], axis=1)
    return _unpack_halves(w)


def _ada_kernel(c_ref, w_ref, b_ref, o_ref):
    c = c_ref[...]
    a = c * jax.nn.sigmoid(c)
    o_ref[...] = _dot3(a, w_ref[...]) + b_ref[...]


def _ada_mod(cond8, w_ada, b_ada):
    d, n = w_ada.shape
    tn = min(n, 1024)
    return pl.pallas_call(
        _ada_kernel,
        out_shape=jax.ShapeDtypeStruct((SUBLANES, n), F32),
        grid=(n // tn,),
        in_specs=[pl.BlockSpec((SUBLANES, d), lambda j: (0, 0)),
                  pl.BlockSpec((d, tn), lambda j: (0, j)),
                  pl.BlockSpec((1, tn), lambda j: (0, j))],
        out_specs=pl.BlockSpec((SUBLANES, tn), lambda j: (0, j)),
        compiler_params=_cp(("arbitrary",)),
        name="ada_mod",
    )(cond8, w_ada, b_ada)


class _Tok:
    def __init__(self, n_ctx, n_lat, lat_len, tm):
        assert n_ctx % tm == 0 and lat_len % tm == 0 and n_lat % lat_len == 0
        self.tm = tm
        self.nct = n_ctx // tm
        self.nst = n_lat // tm
        self.per_seq = lat_len // tm
        self.n = self.nct + self.nst

    def ctx_idx(self, i):
        return jnp.minimum(i, self.nct - 1)

    def lat_idx(self, i):
        return jnp.maximum(i - self.nct, 0)

    def pos_idx(self, i):
        return self.lat_idx(i) % self.per_seq

    def cond(self, i):
        return jnp.where(i < self.nct, 0, 1 + self.lat_idx(i) // self.per_seq)


def _norm_mod_kernel(tok, xp_ref, xs_ref, pos_ref, sh_ref, sc_ref, g_ref, h_ref):
    i = pl.program_id(0)

    def modulate(x):
        y = _rms(x, g_ref[...])
        return (y * (1.0 + sc_ref[0]) + sh_ref[0]).astype(h_ref.dtype)

    @pl.when(i < tok.nct)
    def _():
        h_ref[...] = modulate(xp_ref[...])

    @pl.when(i >= tok.nct)
    def _():
        h_ref[...] = modulate(xs_ref[...] + pos_ref[...])


def _norm_mod(xp, xs, pos, mod, g1, tok):
    d = xp.shape[1]
    tm = tok.tm
    t_all = xp.shape[0] + xs.shape[0]
    return pl.pallas_call(
        functools.partial(_norm_mod_kernel, tok),
        out_shape=jax.ShapeDtypeStruct((t_all, d), BF16),
        grid=(tok.n,),
        in_specs=[
            pl.BlockSpec((tm, d), lambda i: (tok.ctx_idx(i), 0)),
            pl.BlockSpec((tm, d), lambda i: (tok.lat_idx(i), 0)),
            pl.BlockSpec((tm, d), lambda i: (tok.pos_idx(i), 0)),
            pl.BlockSpec((1, 1, d), lambda i: (tok.cond(i) * N_MOD + 0, 0, 0)),
            pl.BlockSpec((1, 1, d), lambda i: (tok.cond(i) * N_MOD + 1, 0, 0)),
            pl.BlockSpec((1, d), lambda i: (0, 0)),
        ],
        out_specs=pl.BlockSpec((tm, d), lambda i: (i, 0)),
        compiler_params=_cp(("parallel",)),
        name="norm_mod",
    )(xp, xs, pos, mod, mod, g1)


def _in_proj_kernel(n_sig, h_ref, w_ref, o_ref, wb_ref):
    n = pl.program_id(0)
    m = pl.program_id(1)

    @pl.when(m == 0)
    def _():
        wb_ref[...] = w_ref[...].astype(BF16)

    acc = _dot(h_ref[...], wb_ref[...])

    @pl.when(n >= n_sig)
    def _():
        o_ref[...] = acc.astype(o_ref.dtype)

    @pl.when(n < n_sig)
    def _():
        o_ref[...] = jax.nn.sigmoid(acc).astype(o_ref.dtype)


def _in_proj(h, w_in, sig_cols, lin_cols, out_dtype, tm, tn, name):
    t_all, d = h.shape
    for c in sig_cols + lin_cols:
        assert c % tn == 0
    assert t_all % tm == 0
    n_sig = (sig_cols[1] - sig_cols[0]) // tn
    n_lin = (lin_cols[1] - lin_cols[0]) // tn
    sig0, lin0 = sig_cols[0] // tn, lin_cols[0] // tn

    def w_blk(n):
        return jnp.where(n < n_sig, sig0 + n, lin0 + n - n_sig)

    return pl.pallas_call(
        functools.partial(_in_proj_kernel, n_sig),
        out_shape=jax.ShapeDtypeStruct((t_all, (n_sig + n_lin) * tn), out_dtype),
        grid=(n_sig + n_lin, t_all // tm),
        in_specs=[pl.BlockSpec((tm, d), lambda n, m: (m, 0)),
                  pl.BlockSpec((d, tn), lambda n, m: (0, w_blk(n)))],
        out_specs=pl.BlockSpec((tm, tn), lambda n, m: (m, n)),
        scratch_shapes=[pltpu.VMEM((d, tn), BF16)],
        compiler_params=_cp(("arbitrary", "arbitrary")),
        name=name,
    )(h, w_in)


CHUNK = 16
GROUP_PAD = 8


def _ssm_operators(a_re, a_im, log_dt, b_re, b_im, c_re, c_im, d_skip):
    t = CHUNK
    g, p = a_re.shape[1:]
    c = b_re.shape[-1]
    lam = lax.complex(a_re.astype(F32), a_im.astype(F32))
    dt = jnp.exp(log_dt.astype(F32))[..., None]
    ld = lam * dt
    lam_bar = jnp.exp(ld)
    b_bar = ((lam_bar - 1.0) / lam)[..., None] * lax.complex(b_re.astype(F32), b_im.astype(F32))
    c_mat = lax.complex(c_re.astype(F32), c_im.astype(F32))
    taus = jnp.arange(t + 1, dtype=F32)
    e_pow = jnp.exp(taus[None, None, :, None] * ld[:, :, None, :])

    k_all = jnp.real(jnp.einsum('dgop,dgtp,dgpi->dgtoi', c_mat, e_pow[:, :, :t], b_bar))
    kf, kb = k_all[0], k_all[1]
    ii = np.arange(t)[None, :, None]
    jj = np.arange(t)[None, None, :]
    tt = np.arange(t)[:, None, None]
    place_f = jnp.asarray((jj - ii == tt).astype(np.float32))
    place_b = jnp.asarray((ii - jj == tt).astype(np.float32))
    kf = kf.at[:, 0].add(jnp.eye(c, dtype=F32)[None] * d_skip.astype(F32).reshape(g, c, 1))
    m = jnp.einsum('tij,gtoc->gicjo', place_f, kf) + jnp.einsum('tij,gtoc->gicjo', place_b, kb)
    m = m.reshape(g, t * c, t * c)

    e_f = e_pow[0][:, ::-1][:, 1:]
    e_b = e_pow[1][:, :t]
    ws_f = e_f[:, :, :, None] * b_bar[0][:, None]
    ws_b = e_b[:, :, :, None] * b_bar[1][:, None]

    def rows_ic(x):
        return jnp.transpose(x, (0, 1, 3, 2)).reshape(g, t * c, p)

    w_state = jnp.concatenate([rows_ic(jnp.real(ws_f)), rows_ic(jnp.real(ws_b)),
                               rows_ic(jnp.imag(ws_f)), rows_ic(jnp.imag(ws_b))], axis=-1)

    ce_f = c_mat[0][:, None] * e_pow[0][:, 1:, None, :]
    ce_b = c_mat[1][:, None] * e_pow[1][:, ::-1][:, :t, None, :]

    def cols_jc(x):
        return jnp.transpose(x, (0, 3, 1, 2)).reshape(g, p, t * c)

    w_so = jnp.concatenate([cols_jc(jnp.real(ce_f)), cols_jc(jnp.real(ce_b)),
                            cols_jc(-jnp.imag(ce_f)), cols_jc(-jnp.imag(ce_b))], axis=1)
    w_out = jnp.concatenate([m, w_so], axis=1)

    a_t = e_pow[:, :, t]
    a_step = jnp.stack([jnp.concatenate([jnp.real(a_t[0]), jnp.real(a_t[1])], axis=-1),
                        jnp.concatenate([jnp.imag(a_t[0]), jnp.imag(a_t[1])], axis=-1)])
    return w_state.astype(BF16), w_out.astype(BF16), a_step


def _lane_perm(n_i, gb, c):
    n = n_i * gb * c
    src = np.arange(n)
    i, q, ch = src // (gb * c), (src // c) % gb, src % c
    dst = q * (n_i * c) + i * c + ch
    p = np.zeros((n, n), np.float32)
    p[src, dst] = 1.0
    return p


def _ssm_kernel(n_seq, n_chunk, gb, g_all, x_ref, ws_ref, wo_ref, a_ref, s0_ref, pin_ref, pout_ref,
                y_ref, fin_ref, xq_ref, za_ref, zb_ref):
    phase = pl.program_id(1)
    j = pl.program_id(2)
    gp = g_all + GROUP_PAD
    nck = n_seq * n_chunk
    half = LANES // 2
    hc = CHUNK // 2

    @pl.when(phase == 0)
    def _():
        halves = []
        for h in range(2):
            xcat = jnp.concatenate([x_ref[pl.ds(h * hc + i, nck, stride=CHUNK), :].astype(BF16)
                                    for i in range(hc)], axis=1)
            halves.append(_dot(xcat, pin_ref[...]).astype(BF16))
        for q in range(gb):
            xq = jnp.concatenate([hv[:, q * LANES:(q + 1) * LANES] for hv in halves], axis=1)
            xq_ref[j * gb + q] = xq
            z = _dot(xq, ws_ref[q])
            za_ref[pl.ds(j * gb + q, nck, stride=gp), :] = z[:, :LANES]
            zb_ref[pl.ds(j * gb + q, nck, stride=gp), :] = z[:, LANES:]

    @pl.when(jnp.logical_and(phase == 1, j == 0))
    def _():
        a_re = a_ref[0]
        a_im = a_ref[1]
        lane = lax.broadcasted_iota(jnp.int32, (g_all, LANES), 1)
        is_f = lane < half
        for s in range(n_seq):
            def step(k, carry):
                s_a, s_b = carry
                rf = pl.multiple_of((s * n_chunk + k) * gp, SUBLANES)
                rb = pl.multiple_of((s * n_chunk + n_chunk - 1 - k) * gp, SUBLANES)
                zf_a = za_ref[pl.ds(rf, g_all), :]
                zf_b = zb_ref[pl.ds(rf, g_all), :]
                zb_a = za_ref[pl.ds(rb, g_all), :]
                zb_b = zb_ref[pl.ds(rb, g_all), :]
                za_ref[pl.ds(rf, g_all), :] = jnp.where(is_f, s_a, zf_a)
                zb_ref[pl.ds(rf, g_all), :] = jnp.where(is_f, s_b, zf_b)
                za_ref[pl.ds(rb, g_all), :] = jnp.where(is_f, zb_a, s_a)
                zb_ref[pl.ds(rb, g_all), :] = jnp.where(is_f, zb_b, s_b)
                z_a = jnp.where(is_f, zf_a, zb_a)
                z_b = jnp.where(is_f, zf_b, zb_b)
                n_a = a_re * s_a - a_im * s_b + z_a
                n_b = a_re * s_b + a_im * s_a + z_b
                return n_a, n_b

            init = s0_ref[s]
            f_a, f_b = lax.fori_loop(0, n_chunk, step, (init[:, :LANES], init[:, LANES:]))
            fin_ref[s] = jnp.concatenate([f_a, f_b], axis=-1)

    @pl.when(phase == 1)
    def _():
        ys = []
        for q in range(gb):
            st_a = za_ref[pl.ds(j * gb + q, nck, stride=gp), :].astype(BF16)
            st_b = zb_ref[pl.ds(j * gb + q, nck, stride=gp), :].astype(BF16)
            lhs = jnp.concatenate([xq_ref[j * gb + q], st_a, st_b], axis=-1)
            ys.append(_dot(lhs, wo_ref[q]).astype(BF16))
        for h in range(2):
            ycat = jnp.concatenate([y[:, h * LANES:(h + 1) * LANES] for y in ys], axis=1)
            yall = _dot(ycat, pout_ref[...])
            for i in range(hc):
                y_ref[pl.ds(h * hc + i, nck, stride=CHUNK), :] = yall[:, i * LANES:(i + 1) * LANES]


def _ssm_call(u, row_blk0, n_rows, w_state, w_out, a_step, s0, n_seq, n_chunk, name):
    g_all, w, _ = w_state.shape
    nck = n_seq * n_chunk
    blk = nck * CHUNK
    n_blk = n_rows // blk
    gb = LANES * CHUNK // w
    ngb = g_all // gb
    gp = g_all + GROUP_PAD
    kern = functools.partial(_ssm_kernel, n_seq, n_chunk, gb, g_all)
    assert w == 2 * LANES
    perm = _lane_perm(CHUNK // 2, gb, LANES // gb)
    p_in = jnp.asarray(perm, BF16)
    p_out = jnp.asarray(perm.T, BF16)
    n_perm = perm.shape[0]
    const = lambda b, ph, j: (0, 0)
    return pl.pallas_call(
        kern,
        out_shape=(jax.ShapeDtypeStruct((n_rows, u.shape[1]), F32),
                   jax.ShapeDtypeStruct((n_blk * n_seq, g_all, w), F32)),
        grid=(n_blk, 2, ngb),
        in_specs=[
            pl.BlockSpec((blk, LANES), lambda b, ph, j: (row_blk0 + b, jnp.where(ph == 0, j, ngb - 1))),
            pl.BlockSpec((gb, w, w), lambda b, ph, j: (jnp.where(ph == 0, j, ngb - 1), 0, 0)),
            pl.BlockSpec((gb, 2 * w, w), lambda b, ph, j: (jnp.where(ph == 0, 0, j), 0, 0)),
            pl.BlockSpec((2, g_all, LANES), lambda b, ph, j: (0, 0, 0)),
            pl.BlockSpec((n_seq, g_all, w), lambda b, ph, j: (b, 0, 0)),
            pl.BlockSpec((n_perm, n_perm), const, pipeline_mode=pl.Buffered(1)),
            pl.BlockSpec((n_perm, n_perm), const, pipeline_mode=pl.Buffered(1)),
        ],
        out_specs=(
            pl.BlockSpec((blk, LANES), lambda b, ph, j: (b, jnp.where(ph == 0, 0, j))),
            pl.BlockSpec((n_seq, g_all, w), lambda b, ph, j: (b, 0, 0)),
        ),
        scratch_shapes=[pltpu.VMEM((g_all, nck, w), BF16),
                        pltpu.VMEM((nck * gp, LANES), F32), pltpu.VMEM((nck * gp, LANES), F32)],
        compiler_params=_cp(("arbitrary", "arbitrary", "arbitrary")),
        name=name,
    )(u, w_state, w_out, a_step, s0, p_in, p_out)


def _glu_kernel(nct, yc_ref, yl_ref, wg_ref, bg_ref, wb_ref, o_ref):
    i = pl.program_id(0)
    y = jax.nn.gelu(jnp.where(i < nct, yc_ref[...], yl_ref[...]))
    z = _dot(y.astype(BF16), wg_ref[...]) + bg_ref[...]
    ya = y * jax.nn.sigmoid(z)
    o_ref[...] = _dot(ya.astype(BF16), wb_ref[...]).astype(o_ref.dtype)


def _glu_branch(y_ctx, y_lat, w_glu, b_glu, w_ba, tok):
    c = y_ctx.shape[1]
    d = w_ba.shape[1]
    tm = tok.tm
    return pl.pallas_call(
        functools.partial(_glu_kernel, tok.nct),
        out_shape=jax.ShapeDtypeStruct((y_ctx.shape[0] + y_lat.shape[0], d), BF16),
        grid=(tok.n,),
        in_specs=[pl.BlockSpec((tm, c), lambda i: (tok.ctx_idx(i), 0)),
                  pl.BlockSpec((tm, c), lambda i: (tok.lat_idx(i), 0)),
                  pl.BlockSpec((c, c), lambda i: (0, 0)),
                  pl.BlockSpec((1, c), lambda i: (0, 0)),
                  pl.BlockSpec((c, d), lambda i: (0, 0))],
        out_specs=pl.BlockSpec((tm, d), lambda i: (i, 0)),
        compiler_params=_cp(("parallel",)),
        name="glu_branch",
    )(y_ctx, y_lat, w_glu, b_glu, w_ba)


def _short_conv_kernel(nct, per_ctx, per_lat, v_ref, x0_ref, x1_ref, vp_ref, x0p_ref, x1p_ref,
                       vn_ref, x0n_ref, x1n_ref, w_ref, b_ref, vx_ref, x0c_ref, scr):
    i = pl.program_id(0)
    tm = v_ref.shape[0]
    k = jnp.where(i < nct, i % per_ctx, (i - nct) % per_lat)
    per = jnp.where(i < nct, per_ctx, per_lat)
    first = k == 0
    last = k == per - 1
    row = lax.broadcasted_iota(jnp.int32, (tm, 1), 0)

    def conv(cur_ref, prev_ref, next_ref, part):
        x = cur_ref[...].astype(F32)
        hp = jnp.where(first, 0.0, prev_ref[SUBLANES - 1:SUBLANES, :].astype(F32))
        hn = jnp.where(last, 0.0, next_ref[0:1, :].astype(F32))
        xm = jnp.where(row == 0, hp, pltpu.roll(x, 1, axis=0))
        xq = jnp.where(row == tm - 1, hn, pltpu.roll(x, tm - 1, axis=0))
        w = w_ref[part]
        return b_ref[part] + xm * w[0:1, :] + x * w[1:2, :] + xq * w[2:3, :]

    v = conv(v_ref, vp_ref, vn_ref, 0)
    x0 = conv(x0_ref, x0p_ref, x0n_ref, 1)
    x1 = conv(x1_ref, x1p_ref, x1n_ref, 2)
    vx_ref[...] = _to_pair_rows(v * x1, scr).astype(vx_ref.dtype)
    x0c_ref[...] = _to_pair_rows(x0, scr).astype(x0c_ref.dtype)


def _short_conv(proj, col_blk0, hw, w_short3, b_short3, n_ctx, seq, dseq, tm):
    t = proj.shape[0]
    assert seq % tm == 0 and dseq % tm == 0
    nt = t // tm
    r8 = tm // SUBLANES
    nb8 = t // SUBLANES

    def cur(k):
        return pl.BlockSpec((tm, hw), lambda i: (i, col_blk0 + k))

    def prev(k):
        return pl.BlockSpec((SUBLANES, hw), lambda i: (jnp.maximum(i * r8 - 1, 0), col_blk0 + k))

    def nxt(k):
        return pl.BlockSpec((SUBLANES, hw), lambda i: (jnp.minimum((i + 1) * r8, nb8 - 1), col_blk0 + k))

    kern = functools.partial(_short_conv_kernel, n_ctx // tm, seq // tm, dseq // tm)
    return pl.pallas_call(
        kern,
        out_shape=(jax.ShapeDtypeStruct((t // 2, 2 * hw), BF16), jax.ShapeDtypeStruct((t // 2, 2 * hw), BF16)),
        grid=(nt,),
        in_specs=[cur(0), cur(1), cur(2), prev(0), prev(1), prev(2), nxt(0), nxt(1), nxt(2),
                  pl.BlockSpec((3, 3, hw), lambda i: (0, 0, 0)),
                  pl.BlockSpec((3, 1, hw), lambda i: (0, 0, 0))],
        out_specs=(pl.BlockSpec((tm // 2, 2 * hw), lambda i: (i, 0)),
                   pl.BlockSpec((tm // 2, 2 * hw), lambda i: (i, 0))),
        scratch_shapes=[pltpu.VMEM((hw // LANES * tm, LANES), F32)],
        compiler_params=_cp(("parallel",)),
        name="short_conv",
    )(proj, proj, proj, proj, proj, proj, proj, proj, proj, w_short3, b_short3)


def _filt_gen_kernel(seq_len, n_bands, wt_ref, wc_ref, ws_ref, b1_ref, f1_ref, w2_ref, b2_ref, f2_ref,
                     w3_ref, b3_ref, rate_ref, a_ref, bm_ref, abs_ref, scr):
    i = pl.program_id(0)
    tl = 2 * a_ref.shape[0]
    hw = a_ref.shape[1] // 2
    pos = lax.broadcasted_iota(jnp.int32, (tl, 1), 0) + i * tl
    t = pos.astype(F32) / seq_len
    bands = (lax.broadcasted_iota(jnp.int32, (1, wc_ref.shape[0]), 1) + 1).astype(F32)
    ang = (2.0 * math.pi) * t * bands
    pre = t * wt_ref[...] + _dot3(jnp.cos(ang), wc_ref[...]) + _dot3(jnp.sin(ang), ws_ref[...])
    h = jnp.sin(f1_ref[...] * (pre + b1_ref[...]))
    h = jnp.sin(f2_ref[...] * (_dot3(h, w2_ref[...]) + b2_ref[...]))
    h = _dot3(h, w3_ref[...]) + b3_ref[...]
    h = h * jnp.exp(-t * rate_ref[...])
    h_f = h[:, :hw]
    h_b = h[:, hw:]
    colabs = jnp.sum(jnp.abs(h_f) + jnp.abs(h_b), axis=0, keepdims=True)
    h_bp = jnp.where(pos == 0, 0.0, h_b)
    a_ref[...] = _to_pair_rows(h_f + h_bp, scr).astype(a_ref.dtype)
    bm_ref[...] = _to_pair_rows(h_bp - h_f, scr).astype(bm_ref.dtype)

    @pl.when(i == 0)
    def _():
        abs_ref[...] = colabs

    @pl.when(i > 0)
    def _():
        abs_ref[...] += colabs


def _filt_gen(seq_len, w_f1, b_f1, freq1, w_f2, b_f2, freq2, w_f3, b_f3, rates2):
    n_emb, hid_raw = w_f1.shape
    n_bands = (n_emb - 1) // 2
    hw2 = w_f3.shape[1]
    hw = hw2 // 2
    tl = min(seq_len, 512)
    hid = LANES
    assert n_bands <= LANES and hid_raw <= LANES

    def pad(x, rows, cols):
        return jnp.zeros((rows, cols), F32).at[:x.shape[0], :x.shape[1]].set(x.astype(F32))

    wt = pad(w_f1[0:1], 1, hid)
    wc = pad(w_f1[1:1 + n_bands], LANES, hid)
    ws = pad(w_f1[1 + n_bands:], LANES, hid)
    b_f1, freq1, b_f2, freq2 = (pad(x, 1, hid) for x in (b_f1, freq1, b_f2, freq2))
    w_f2 = pad(w_f2, hid, hid)
    w_f3 = pad(w_f3, hid, hw2)
    full = lambda shape: pl.BlockSpec(shape, lambda i: (0,) * len(shape))
    kern = functools.partial(_filt_gen_kernel, seq_len, n_bands)
    return pl.pallas_call(
        kern,
        out_shape=(jax.ShapeDtypeStruct((seq_len // 2, hw2), BF16), jax.ShapeDtypeStruct((seq_len // 2, hw2), BF16),
                   jax.ShapeDtypeStruct((1, hw), F32)),
        grid=(seq_len // tl,),
        in_specs=[full((1, hid)), full((LANES, hid)), full((LANES, hid)), full((1, hid)), full((1, hid)),
                  full((hid, hid)), full((1, hid)), full((1, hid)), full((hid, hw2)), full((1, hw2)),
                  full((1, hw2))],
        out_specs=(pl.BlockSpec((tl // 2, hw2), lambda i: (i, 0)), pl.BlockSpec((tl // 2, hw2), lambda i: (i, 0)),
                   full((1, hw))),
        scratch_shapes=[pltpu.VMEM((hw // LANES * tl, LANES), F32)],
        compiler_params=_cp(("arbitrary",)),
        name="filt_gen",
    )(wt, wc, ws, b_f1, freq1, w_f2, b_f2, freq2, w_f3, b_f3, rates2)


def _trig_tables(row_hi, row_lo, col, period):
    col = col[None, :]
    ang_lo = ((row_lo[:, None] * col) % period).astype(F32) * (2.0 * math.pi / period)
    ang_hi = ((row_hi[:, None] * col) % period).astype(F32) * (2.0 * math.pi / period)
    c_lo, s_lo = jnp.cos(ang_lo)[None], jnp.sin(ang_lo)[None]
    c_hi, s_hi = jnp.cos(ang_hi)[:, None], jnp.sin(ang_hi)[:, None]
    n = row_hi.shape[0] * row_lo.shape[0]
    ctab = (c_hi * c_lo - s_hi * s_lo).reshape(n, col.shape[1])
    stab = (s_hi * c_lo + c_hi * s_lo).reshape(n, col.shape[1])
    return ctab.astype(BF16), stab.astype(BF16)


def _dft_tables(seq_len):
    m = seq_len // 2
    r = 1
    while r * r < m:
        r *= 2
    assert m % r == 0
    i32 = jnp.int32
    idx = jnp.arange(m, dtype=i32)
    hi = jnp.arange(m // r, dtype=i32) * r
    lo = jnp.arange(r, dtype=i32)
    ce, se = _trig_tables(hi, lo, idx, 2 * m)
    co, so = _trig_tables(hi, lo, 2 * idx + 1, 2 * seq_len)
    cot, sot = _trig_tables(2 * hi, 2 * lo + 1, idx, 2 * seq_len)
    return ce, se, co, so, cot, sot


def _alt_row(n):
    return (1 - 2 * (lax.broadcasted_iota(jnp.int32, (SUBLANES, n), 1) % 2)).astype(BF16)


def _filt_dft_kernel(seq_len, ce_ref, se_ref, co_ref, so_ref, ae_ref, ao_ref, be_ref, bo_ref, abs_ref,
                     krl_ref, krh_ref, kil_ref, kih_ref, kmid_ref):
    i = pl.program_id(0)
    scale = (1.0 / seq_len) / (abs_ref[...] + EPS)
    ec = _dot(ce_ref[...], ae_ref[...])
    oc = _dot(co_ref[...], ao_ref[...])
    es = _dot(se_ref[...], be_ref[...])
    os_ = _dot(so_ref[...], bo_ref[...])
    krl_ref[...] = (ec + oc) * scale
    krh_ref[...] = (ec - oc) * scale
    kil_ref[...] = (es + os_) * scale
    kih_ref[...] = (os_ - es) * scale

    @pl.when(i == 0)
    def _():
        sgn = _alt_row(ae_ref.shape[0])
        kmid_ref[0:1, :] = _dot(sgn, ae_ref[...])[0:1] * scale
        kmid_ref[1:2, :] = _dot(sgn, bo_ref[...])[0:1] * scale


def _filt_dft(tabs, a2, bm2, colabs, seq_len):
    m = seq_len // 2
    hw = a2.shape[1] // 2
    ce, se, co, so = tabs[:4]
    tf = min(m, 256)
    kern = functools.partial(_filt_dft_kernel, seq_len)
    mode = pl.Buffered(1) if m * hw * 2 > BIG_BLOCK_BYTES // 2 else None
    tab = pl.BlockSpec((tf, m), lambda i: (i, 0))
    res = lambda c: pl.BlockSpec((m, hw), lambda i: (0, c), pipeline_mode=mode)
    out = pl.BlockSpec((tf, hw), lambda i: (i, 0))
    sds = jax.ShapeDtypeStruct((m, hw), F32)
    return pl.pallas_call(
        kern,
        out_shape=(sds, sds, sds, sds, jax.ShapeDtypeStruct((2, hw), F32)),
        grid=(m // tf,),
        in_specs=[tab, tab, tab, tab, res(0), res(1), res(0), res(1), pl.BlockSpec((1, hw), lambda i: (0, 0))],
        out_specs=(out, out, out, out, pl.BlockSpec((2, hw), lambda i: (0, 0))),
        compiler_params=_cp(("arbitrary",)),
        name="filt_dft",
    )(ce, se, co, so, a2, a2, bm2, bm2, colabs)


def _hy_fwd_kernel(ce_ref, se_ref, co_ref, so_ref, ve_ref, vo_ref, krl_ref, krh_ref, kil_ref, kih_ref, kmid_ref,
                   ae_ref, be_ref, ao_ref, bo_ref, ymid_ref):
    j = pl.program_id(1)
    tf, m = ce_ref.shape
    nb = ae_ref.shape[0]
    krl, krh, kil, kih = krl_ref[...], krh_ref[...], kil_ref[...], kih_ref[...]
    row = lax.broadcasted_iota(jnp.int32, (tf, 1), 0) + j * tf
    dc = jnp.where(row == 0, 0.5, 1.0)
    for s in range(nb):
        ve = ve_ref[s * m:(s + 1) * m, :]
        vo = vo_ref[s * m:(s + 1) * m, :]
        ec = _dot(ce_ref[...], ve)
        oc = _dot(co_ref[...], vo)
        es = _dot(se_ref[...], ve)
        os_ = _dot(so_ref[...], vo)
        p_lo, p_hi = ec + oc, ec - oc
        q_lo, q_hi = es + os_, os_ - es
        yre_lo = (p_lo * krl + q_lo * kil) * dc
        yim_lo = p_lo * kil - q_lo * krl
        yre_hi = (p_hi * krh + q_hi * kih) * dc
        yim_hi = p_hi * kih - q_hi * krh
        ae_ref[s] = (yre_lo + yre_hi).astype(ae_ref.dtype)
        be_ref[s] = (yim_lo - yim_hi).astype(be_ref.dtype)
        ao_ref[s] = (yre_lo - yre_hi).astype(ao_ref.dtype)
        bo_ref[s] = (yim_lo + yim_hi).astype(bo_ref.dtype)

    @pl.when(j == 0)
    def _():
        sgn = _alt_row(m)
        kr_m, ki_m = kmid_ref[0:1, :], kmid_ref[1:2, :]
        for s in range(nb):
            p_m = _dot(sgn, ve_ref[s * m:(s + 1) * m, :])[0:1]
            q_m = _dot(sgn, vo_ref[s * m:(s + 1) * m, :])[0:1]
            ymid_ref[s, 0:1, :] = p_m * kr_m + q_m * ki_m
            ymid_ref[s, 1:2, :] = p_m * ki_m - q_m * kr_m


BIG_BLOCK_BYTES = 4 * 1024 * 1024


def _seq_batch(seq_len, n_seq):
    nb = max(1, min(n_seq, 4 * MXU_SIDE // seq_len))
    while n_seq % nb:
        nb -= 1
    return nb


def _hy_fwd(tabs, vx2, kspec, n_seq, seq_len, row0_blk):
    m = seq_len // 2
    hw = vx2.shape[1] // 2
    ce, se, co, so = tabs[:4]
    krl, krh, kil, kih, kmid = kspec
    tf = min(m, 256)
    nb = _seq_batch(seq_len, n_seq)
    assert row0_blk % nb == 0
    mode = pl.Buffered(1) if nb * m * hw * 2 > BIG_BLOCK_BYTES // 2 else None
    tab = pl.BlockSpec((tf, m), lambda b, j: (j, 0))
    vspec = lambda c: pl.BlockSpec((nb * m, hw), lambda b, j: (row0_blk // nb + b, c), pipeline_mode=mode)
    kt = pl.BlockSpec((tf, hw), lambda b, j: (j, 0))
    out = pl.BlockSpec((nb, tf, hw), lambda b, j: (b, j, 0))
    sds = jax.ShapeDtypeStruct((n_seq, m, hw), BF16)
    return pl.pallas_call(
        _hy_fwd_kernel,
        out_shape=(sds, sds, sds, sds, jax.ShapeDtypeStruct((n_seq, 2, hw), F32)),
        grid=(n_seq // nb, m // tf),
        in_specs=[tab, tab, tab, tab, vspec(0), vspec(1), kt, kt, kt, kt,
                  pl.BlockSpec((2, hw), lambda b, j: (0, 0))],
        out_specs=(out, out, out, out, pl.BlockSpec((nb, 2, hw), lambda b, j: (b, 0, 0))),
        compiler_params=_cp(("arbitrary", "arbitrary")),
        name="hy_fwd",
    )(ce, se, co, so, vx2, vx2, krl, krh, kil, kih, kmid)


def _hy_inv_kernel(ce_ref, se_ref, cot_ref, sot_ref, ae_ref, be_ref, ao_ref, bo_ref, ymid_ref,
                   vx_ref, x0_ref, fb_ref, wb_ref, o_ref, scr):
    j = pl.program_id(1)
    tt = ce_ref.shape[0]
    nb = ae_ref.shape[0]
    hw = fb_ref.shape[1]
    row = lax.broadcasted_iota(jnp.int32, (tt, 1), 0) + j * tt
    sgn = (1 - 2 * (row % 2)).astype(F32)
    for s in range(nb):
        rows = slice(s * tt, (s + 1) * tt)
        ymid = ymid_ref[s]
        z_e = _dot(ce_ref[...], ae_ref[s]) - _dot(se_ref[...], be_ref[s]) + sgn * ymid[0:1]
        z_o = _dot(cot_ref[...], ao_ref[s]) - _dot(sot_ref[...], bo_ref[s]) - sgn * ymid[1:2]
        proj = []
        for par, z in ((0, z_e), (1, z_o)):
            cols = slice(par * hw, (par + 1) * hw)
            z = z + vx_ref[rows, cols].astype(F32) * fb_ref[...]
            yb = x0_ref[rows, cols].astype(F32) * z
            proj.append(_dot(yb.astype(BF16), wb_ref[...]))
        o_ref[2 * s * tt:2 * (s + 1) * tt, :] = _from_pair_rows(
            jnp.concatenate(proj, axis=1), scr).astype(o_ref.dtype)


def _hy_inv(tabs, spec, vx2, x0c2, fbias, w_bb, n_seq, seq_len, row0_blk):
    m = seq_len // 2
    hw = fbias.shape[1]
    d = w_bb.shape[1]
    ce, se, _, _, cot, sot = tabs
    ae, be, ao, bo, ymid = spec
    tt = min(m, 128)
    per = m // tt
    nb = _seq_batch(seq_len, n_seq) if per == 1 else 1
    assert row0_blk % nb == 0
    mode = pl.Buffered(1) if nb * m * hw * 2 > BIG_BLOCK_BYTES // 2 else None
    tab = pl.BlockSpec((tt, m), lambda b, j: (j, 0))
    res = pl.BlockSpec((nb, m, hw), lambda b, j: (b, 0, 0), pipeline_mode=mode)
    rows_in = pl.BlockSpec((nb * tt, 2 * hw), lambda b, j: ((row0_blk // nb + b) * per + j, 0))
    return pl.pallas_call(
        _hy_inv_kernel,
        out_shape=jax.ShapeDtypeStruct((n_seq * seq_len, d), BF16),
        grid=(n_seq // nb, per),
        in_specs=[tab, tab, tab, tab, res, res, res, res,
                  pl.BlockSpec((nb, 2, hw), lambda b, j: (b, 0, 0)),
                  rows_in, rows_in,
                  pl.BlockSpec((1, hw), lambda b, j: (0, 0)),
                  pl.BlockSpec((hw, d), lambda b, j: (0, 0), pipeline_mode=pl.Buffered(1))],
        out_specs=pl.BlockSpec((nb * 2 * tt, d), lambda b, j: (b * per + j, 0)),
        scratch_shapes=[pltpu.VMEM((d // LANES * 2 * tt, LANES), F32)],
        compiler_params=_cp(("arbitrary", "arbitrary")),
        name="hy_inv",
    )(ce, se, cot, sot, ae, be, ao, bo, ymid, vx2, x0c2, fbias, w_bb)


MERGE_COL_BLOCKS = 4

def _expert_pairs(epg):
    pairs = [(a, b) for a in range(epg) for b in range(a + 1, epg)]
    order = [pairs.pop(0)]
    while pairs:
        last = order[-1]
        nxt = next((p for p in pairs if set(p) & set(last)), pairs[0])
        pairs.remove(nxt)
        order.append(nxt)
    return order


def _merge_kernel(tok, n_groups, epg, xp_ref, xs_ref, pos_ref, g0_ref, g1_ref, pa_ref, pbc_ref, pbs_ref,
                  gate1_ref, sh2_ref, sc2_ref, gn_ref, wo_ref, wr_ref, br_ref,
                  x2_ref, h2_ref, rt_ref):
    i = pl.program_id(0)
    is_ctx = i < tok.nct
    tm, d = x2_ref.shape
    pb = jnp.where(is_ctx, pbc_ref[...], pbs_ref[...])
    merged = g0_ref[...] * pa_ref[...] + g1_ref[...] * pb
    cb = d // MERGE_COL_BLOCKS
    ssq = jnp.zeros((tm, 1), F32)
    for c in range(MERGE_COL_BLOCKS):
        cols = slice(c * cb, (c + 1) * cb)
        att = _dot(merged, wo_ref[:, cols])
        x = jnp.where(is_ctx, xp_ref[:, cols], xs_ref[:, cols] + pos_ref[:, cols])
        x2c = x + gate1_ref[0][:, cols] * att
        x2_ref[:, cols] = x2c
        ssq = ssq + jnp.sum(x2c * x2c, axis=-1, keepdims=True)
    x2 = x2_ref[...]
    h2 = x2 * lax.rsqrt(ssq * (1.0 / d) + EPS) * gn_ref[...] * (1.0 + sc2_ref[0]) + sh2_ref[0]
    h2b = h2.astype(BF16)
    _store_row_tiles(h2_ref, 0, h2b)
    logits = _dot(h2b, wr_ref[...]) + br_ref[...]
    lane = lax.broadcasted_iota(jnp.int32, logits.shape, 1)
    gl = jnp.where(lane < n_groups, logits, -jnp.inf)
    gmax = jnp.max(gl, axis=-1, keepdims=True)
    gidx = jnp.min(jnp.where(gl == gmax, lane, n_groups), axis=-1, keepdims=True)
    base = n_groups + gidx * epg
    el = jnp.where(jnp.logical_and(lane >= base, lane < base + epg), logits, -jnp.inf)
    i1 = jnp.min(jnp.where(el == jnp.max(el, axis=-1, keepdims=True), lane, LANES), axis=-1, keepdims=True)
    el = jnp.where(lane == i1, -jnp.inf, el)
    i2 = jnp.min(jnp.where(el == jnp.max(el, axis=-1, keepdims=True), lane, LANES), axis=-1, keepdims=True)
    lo = jnp.minimum(i1, i2) - base
    hi = jnp.maximum(i1, i2) - base
    cls = jnp.zeros_like(gidx)
    for code, (pa_, pb_) in enumerate(_expert_pairs(epg)):
        cls = cls + code * jnp.logical_and(lo == pa_, hi == pb_).astype(jnp.int32)
    rt_ref[...] = jnp.where(lane == LANES - 1, gidx.astype(F32),
                            jnp.where(lane == LANES - 2, cls.astype(F32), logits))


def _merge_out(xp, xs, pos, proj, gate_blk0, pa, pb_ctx, pb_lat, mod, g2, w_out, w_r, b_r, n_groups, epg, tok):
    d = xp.shape[1]
    tm = tok.tm
    t_all = xp.shape[0] + xs.shape[0]
    kern = functools.partial(_merge_kernel, tok, n_groups, epg)
    one = pl.Buffered(1)
    ns = d // 2 // LANES

    def modspec(k):
        return pl.BlockSpec((1, 1, d), lambda i: (tok.cond(i) * N_MOD + k, 0, 0))

    return pl.pallas_call(
        kern,
        out_shape=(jax.ShapeDtypeStruct((t_all, d), F32), jax.ShapeDtypeStruct((t_all * ns, LANES), jnp.uint32),
                   jax.ShapeDtypeStruct((t_all, LANES), F32)),
        grid=(tok.n,),
        in_specs=[
            pl.BlockSpec((tm, d), lambda i: (tok.ctx_idx(i), 0)),
            pl.BlockSpec((tm, d), lambda i: (tok.lat_idx(i), 0)),
            pl.BlockSpec((tm, d), lambda i: (tok.pos_idx(i), 0)),
            pl.BlockSpec((tm, d), lambda i: (i, gate_blk0)),
            pl.BlockSpec((tm, d), lambda i: (i, gate_blk0 + 1)),
            pl.BlockSpec((tm, d), lambda i: (i, 0)),
            pl.BlockSpec((tm, d), lambda i: (tok.ctx_idx(i), 0)),
            pl.BlockSpec((tm, d), lambda i: (tok.lat_idx(i), 0)),
            modspec(2), modspec(3), modspec(4),
            pl.BlockSpec((1, d), lambda i: (0, 0)),
            pl.BlockSpec((d, d), lambda i: (0, 0), pipeline_mode=one),
            pl.BlockSpec((d, LANES), lambda i: (0, 0)),
            pl.BlockSpec((1, LANES), lambda i: (0, 0)),
        ],
        out_specs=(pl.BlockSpec((tm, d), lambda i: (i, 0)), pl.BlockSpec((tm * ns, LANES), lambda i: (i, 0)),
                   pl.BlockSpec((tm, LANES), lambda i: (i, 0))),
        compiler_params=_cp(("parallel",)),
        name="merge_out",
    )(xp, xs, pos, proj, proj, pa, pb_ctx, pb_lat, mod, mod, mod, g2, w_out, w_r, b_r)


ROW_UNROLL = 8

def _moe_kernel(n_tiles, n_groups, epg, tg_ref, tnv_ref, rt_ref, cb_ref, kx_ref, wi_ref,
                h2_hbm, wr_ref, br_ref, wg_ref, wu_ref, wd_ref, out_hbm,
                xbuf, xb_ref, acc, obuf, gws, gsem, ssem):
    i = pl.program_id(0)
    j = pl.program_id(1)
    tm = xb_ref.shape[0]
    ns = h2_hbm.shape[1]
    nv = tnv_ref[i]
    slot = i % 2

    def buf_rows(buf, slot_, r, rows):
        return buf.at[pl.ds(pl.multiple_of((slot_ * tm + r) * ns, ns), rows * ns), :]

    def gather_copy(tok_row, slot_, r):
        return pltpu.make_async_copy(h2_hbm.at[tok_row], buf_rows(xbuf, slot_, r, 1), gsem.at[slot_])

    def gather_wait(slot_, rows):
        dst = buf_rows(xbuf, slot_, 0, rows)
        pltpu.make_async_copy(dst, dst, gsem.at[slot_]).wait()

    def scatter_copy(tok_row, slot_, r):
        return pltpu.make_async_copy(buf_rows(obuf, slot_, r, 1), out_hbm.at[tok_row], ssem.at[0])

    def scatter_wait(slot_, rows):
        src = buf_rows(obuf, slot_, 0, rows)
        pltpu.make_async_copy(src, src, ssem.at[0]).wait()

    def for_rows(count, fn):
        n_full = count // ROW_UNROLL

        def body(b, c):
            for k in range(ROW_UNROLL):
                fn(b * ROW_UNROLL + k)
            return c

        def tail(r, c):
            fn(r)
            return c

        lax.fori_loop(0, n_full, body, 0)
        lax.fori_loop(n_full * ROW_UNROLL, count, tail, 0)

    def wait_scatter(count, slot_):
        p = tm
        while p >= 1:
            @pl.when((count & p) != 0)
            def _(p=p):
                scatter_wait(slot_, p)
            p //= 2

    nv_prev = tnv_ref[jnp.maximum(i - 1, 0)]
    nv_next = tnv_ref[jnp.minimum(i + 1, n_tiles - 1)]
    prev_deferred = jnp.logical_and(i > 0, jnp.logical_and(nv_prev == tm, nv > 0))
    deferred = jnp.logical_and(nv == tm, nv_next > 0)

    @pl.when(j == 0)
    def _():
        @pl.when(i == 0)
        def _():
            for_rows(tm, lambda r: gather_copy(rt_ref[r], 0, r).start())

        @pl.when(jnp.logical_or(i == 0, nv_prev > 0))
        def _():
            gather_wait(slot, tm)

    @pl.when(jnp.logical_and(j == 0, nv > 0))
    def _():
        grp = tg_ref[i]
        xb = _load_row_tiles(xbuf, slot * tm * ns, tm, ns)
        xb_ref[...] = xb
        logits = _dot(xb, wr_ref[...]) + br_ref[...]
        lane = lax.broadcasted_iota(jnp.int32, logits.shape, 1)
        gl = jnp.where(lane < n_groups, logits, -jnp.inf)
        gmax = jnp.max(gl, axis=-1, keepdims=True)
        ge = jnp.exp(gl - gmax)
        g_w = jnp.sum(jnp.where(lane == grp, ge, 0.0), axis=-1, keepdims=True) / jnp.sum(ge, axis=-1, keepdims=True)
        base = n_groups + grp * epg
        e = [jnp.sum(jnp.where(lane == base + k, logits, 0.0), axis=-1, keepdims=True) for k in range(epg)]
        pairs = _expert_pairs(epg)
        row = lax.broadcasted_iota(jnp.int32, (tm, 1), 0)
        cls = jnp.zeros((tm, 1), jnp.int32)
        for c in range(len(pairs) - 1):
            cls = cls + (row >= cb_ref[i * len(pairs) + c]).astype(jnp.int32)
        sel = [functools.reduce(jnp.logical_or, [cls == c for c, p in enumerate(pairs) if k in p])
               for k in range(epg)]
        emax = functools.reduce(jnp.maximum, [jnp.where(sel[k], e[k], -jnp.inf) for k in range(epg)])
        ex = [jnp.where(sel[k], jnp.exp(e[k] - emax), 0.0) for k in range(epg)]
        den = functools.reduce(lambda a, b: a + b, ex)
        for k in range(epg):
            gws[k] = ex[k] / den * g_w
        acc[...] = jnp.zeros(acc.shape, acc.dtype)

    kx = kx_ref[i * epg + j]

    def expert_step(issue, scatter_prev):
        per = tm // TOP_K_INNER
        if issue:
            for k in range(per):
                r = kx * per + k
                if scatter_prev:
                    scatter_copy(rt_ref[(i - 1) * tm + r], 1 - slot, r).start()
                gather_copy(rt_ref[(i + 1) * tm + r], 1 - slot, r).start()
        xb = xb_ref[...]
        g = _dot(xb, wg_ref[...])
        u = _dot(xb, wu_ref[...])
        hid = (g * jax.nn.sigmoid(g)) * u * gws[j]
        acc[...] += _dot(hid.astype(BF16), wd_ref[...])

    run = jnp.logical_and(nv > 0, kx >= 0)
    carries = kx < TOP_K_INNER

    @pl.when(jnp.logical_and(run, jnp.logical_and(carries, prev_deferred)))
    def _():
        expert_step(True, True)

    @pl.when(jnp.logical_and(run, jnp.logical_and(carries, jnp.logical_not(prev_deferred))))
    def _():
        expert_step(True, False)

    @pl.when(jnp.logical_and(run, jnp.logical_not(carries)))
    def _():
        expert_step(False, False)

    @pl.when(j == epg - 1)
    def _():
        @pl.when(prev_deferred)
        def _():
            scatter_wait(1 - slot, tm)

        @pl.when(nv > 0)
        def _():
            _store_row_tiles(obuf, slot * tm * ns, acc[...].astype(BF16))

        @pl.when(jnp.logical_and(nv > 0, jnp.logical_not(deferred)))
        def _():
            for_rows(nv, lambda r: scatter_copy(rt_ref[i * tm + r], slot, r).start())
            wait_scatter(nv, slot)


def _moe(h2p, w_r, b_r, w_g, w_u, w_d, plan, n_groups, epg, tm):
    t, ns, _ = h2p.shape
    d = 2 * ns * LANES
    ff = w_g.shape[-1]
    n_tiles = plan[0].shape[0]
    assert tm & (tm - 1) == 0, "row-count waits decompose tm in binary"
    assert tm % TOP_K_INNER == 0
    kern = functools.partial(_moe_kernel, n_tiles, n_groups, epg)

    def const(i, j, *_):
        return (0, 0)

    def wblk(i, j, tg, tnv, rt, cb, kx, wi):
        return (wi[i * epg + j], 0, 0)

    grid_spec = pltpu.PrefetchScalarGridSpec(
        num_scalar_prefetch=len(plan),
        grid=(n_tiles, epg),
        in_specs=[
            pl.BlockSpec(memory_space=pl.ANY),
            pl.BlockSpec((d, LANES), const),
            pl.BlockSpec((1, LANES), const),
            pl.BlockSpec((None, d, ff), wblk),
            pl.BlockSpec((None, d, ff), wblk),
            pl.BlockSpec((None, ff, d), wblk),
        ],
        out_specs=pl.BlockSpec(memory_space=pl.ANY),
        scratch_shapes=[
            pltpu.VMEM((2 * tm * ns, LANES), jnp.uint32),
            pltpu.VMEM((tm, d), BF16),
            pltpu.VMEM((tm, d), F32),
            pltpu.VMEM((2 * tm * ns, LANES), jnp.uint32),
            pltpu.VMEM((epg, tm, 1), F32),
            pltpu.SemaphoreType.DMA((2,)),
            pltpu.SemaphoreType.DMA((1,)),
        ],
    )
    return pl.pallas_call(
        kern,
        out_shape=jax.ShapeDtypeStruct((t, ns, LANES), jnp.uint32),
        grid_spec=grid_spec,
        compiler_params=_cp(("arbitrary", "arbitrary")),
        name="moe",
    )(*plan, h2p, w_r, b_r, w_g, w_u, w_d)


def _moe_plan(grp, cls, n_groups, epg, tm, n_tiles):
    i32 = jnp.int32
    t = grp.shape[0]
    pairs = _expert_pairs(epg)
    n_cls = len(pairs)
    key = grp * n_cls + cls
    onehot = (key[:, None] == jnp.arange(n_groups * n_cls, dtype=i32)[None, :]).astype(i32)
    csum = jnp.cumsum(onehot, axis=0)
    kcount = csum[-1].reshape(n_groups, n_cls)
    counts = jnp.sum(kcount, axis=1)
    cls_end = jnp.cumsum(kcount, axis=1)
    cls_off = cls_end - kcount
    ntile_g = (counts + tm - 1) // tm
    tile_end = jnp.cumsum(ntile_g)
    tile_off = tile_end - ntile_g
    key_base = (tile_off[:, None] * tm + cls_off).reshape(-1)
    slot = jnp.sum(onehot * (key_base[None, :] + csum - 1), axis=1)
    row_tok = jnp.zeros((n_tiles * tm,), i32).at[slot].set(jnp.arange(t, dtype=i32))
    tiles = jnp.arange(n_tiles, dtype=i32)
    tile_grp = jnp.minimum(jnp.sum((tiles[:, None] >= tile_end[None, :]).astype(i32), axis=1), n_groups - 1)
    row0 = (tiles - tile_off[tile_grp]) * tm
    tile_nv = jnp.clip(counts[tile_grp] - row0, 0, tm)
    tile_nv = jnp.where(tiles < tile_end[-1], tile_nv, 0)
    cb = jnp.clip(cls_end[tile_grp] - row0[:, None], 0, tm)
    rows_in = jnp.diff(jnp.minimum(cb, tile_nv[:, None]), axis=1, prepend=0)
    member = jnp.asarray(np.array([[int(k in p) for k in range(epg)] for p in pairs], np.int32))
    needed = ((rows_in > 0).astype(i32) @ member) > 0
    kx = jnp.where(needed, jnp.cumsum(needed.astype(i32), axis=1) - 1, -1)
    blk = (tile_grp[:, None] * epg + jnp.arange(epg, dtype=i32)[None, :]).reshape(-1)
    pos = jnp.arange(n_tiles * epg, dtype=i32)
    last = lax.cummax(jnp.where(needed.reshape(-1), pos, -1), axis=0)
    wi = jnp.where(last >= 0, blk[jnp.maximum(last, 0)], 0)
    return (tile_grp.astype(i32), tile_nv.astype(i32), row_tok, cb.reshape(-1).astype(i32),
            kx.reshape(-1).astype(i32), wi.astype(i32))


def _final_kernel(x2_ref, m_ref, gate2_ref, gf_ref, o_ref):
    tm, d = x2_ref.shape
    ns = d // 2 // LANES
    m = _load_row_tiles(m_ref, 0, tm, ns).astype(F32)
    x3 = x2_ref[...] + gate2_ref[0] * m
    o_ref[...] = _rms(x3, gf_ref[...])


def _final(x2, moe, mod, g_final, row_blk0, n_rows, cond_of_tile, tm):
    d = x2.shape[1]
    ns = d // 2 // LANES
    return pl.pallas_call(
        _final_kernel,
        out_shape=jax.ShapeDtypeStruct((n_rows, d), F32),
        grid=(n_rows // tm,),
        in_specs=[pl.BlockSpec((tm, d), lambda i: (row_blk0 + i, 0)),
                  pl.BlockSpec((tm * ns, LANES), lambda i: (row_blk0 + i, 0)),
                  pl.BlockSpec((1, 1, d), lambda i: (cond_of_tile(i) * N_MOD + 5, 0, 0)),
                  pl.BlockSpec((1, d), lambda i: (0, 0))],
        out_specs=pl.BlockSpec((tm, d), lambda i: (i, 0)),
        compiler_params=_cp(("parallel",)),
        name="final",
    )(x2, moe, mod, g_final)


def _grid_pos_embed(n_tokens, dim):
    rows_n = n_tokens // GRID_W
    quarter = dim // 4
    omega = 1.0 / (10000.0 ** (jnp.arange(quarter, dtype=F32) / quarter))

    def emb(n):
        ang = jnp.arange(n).astype(F32)[:, None] * omega[None, :]
        return jnp.concatenate([jnp.sin(ang), jnp.cos(ang)], axis=-1)

    e_r = jnp.broadcast_to(emb(rows_n)[:, None, :], (rows_n, GRID_W, dim // 2))
    e_c = jnp.broadcast_to(emb(GRID_W)[None, :, :], (rows_n, GRID_W, dim // 2))
    return jnp.concatenate([e_r, e_c], axis=-1).reshape(n_tokens, dim)


def _state_to_lanes(s):
    b, _, _, g, p = s.shape
    return jnp.transpose(s, (0, 3, 2, 1, 4)).reshape(b, g, 4 * p)


def _lanes_to_state(x, p):
    b, g, _ = x.shape
    return jnp.transpose(x.reshape(b, g, 2, 2, p), (0, 3, 2, 1, 4))


def kernel(x_prompt, x_sample, state_ssm, c, c_ctx, w_ada, b_ada, g_norm1, g_norm2, w_in, ssm_a_re, ssm_a_im, ssm_log_dt, ssm_b_re, ssm_b_im, ssm_c_re, ssm_c_im, ssm_d, w_glu, b_glu, w_short, b_short, w_f1, b_f1, freq1, w_f2, b_f2, freq2, w_f3, b_f3, filter_bias, w_branch_a, w_branch_b, w_out, w_router_group, b_router_group, w_router_expert, b_router_expert, w_exp_gate, w_exp_up, w_exp_down, g_final):
    depth = w_ada.shape[0]
    assert depth == 1, "single-layer pipeline"
    bsz, seq, d = x_prompt.shape
    dbsz, dseq, _ = x_sample.shape
    g_all, p_state = ssm_a_re.shape[2:]
    ssm_w = ssm_d.shape[1]
    hw = filter_bias.shape[1]
    n_groups, _, epg = w_router_expert.shape[1:]
    ff = w_exp_gate.shape[-1]
    assert ssm_w // g_all * CHUNK == MXU_SIDE and 4 * p_state == MXU_SIDE
    assert ssm_w == hw and d == 2 * hw
    n_ctx = bsz * seq
    n_lat = dbsz * dseq
    t_all = n_ctx + n_lat

    xp = x_prompt.reshape(n_ctx, d)
    xs = x_sample.reshape(n_lat, d)
    pos = _grid_pos_embed(dseq, d)

    n_cond = 1 + dbsz
    cond8 = jnp.zeros((SUBLANES, d), F32).at[0].set(c_ctx).at[1:n_cond].set(c)
    mod = _ada_mod(cond8, w_ada[0], b_ada[0][None])
    mod = mod[:n_cond].reshape(n_cond * N_MOD, 1, d)

    tm_e = min(256, seq, dseq)
    tok_e = _Tok(n_ctx, n_lat, dseq, tm_e)
    tm_l = min(512, n_ctx, dseq)
    tok_l = _Tok(n_ctx, n_lat, dseq, tm_l)
    gate_col0 = ssm_w + 3 * hw
    h1 = _norm_mod(xp, xs, pos, mod, g_norm1, tok_l)
    tm_p, tn_p = min(1024, t_all), min(1024, hw)
    n_in = w_in.shape[2]
    u_a = _in_proj(h1, w_in[0], (0, 0), (0, ssm_w), F32, tm_p, tn_p, "in_proj_a")
    proj = _in_proj(h1, w_in[0], (gate_col0, n_in), (ssm_w, gate_col0), BF16, tm_p, tn_p, "in_proj")
    n_gate_cols = n_in - gate_col0

    w_state, w_so, a_step = _ssm_operators(ssm_a_re[0], ssm_a_im[0], ssm_log_dt[0], ssm_b_re[0], ssm_b_im[0],
                                           ssm_c_re[0], ssm_c_im[0], ssm_d[0])
    nc_ctx = seq // CHUNK
    nc_lat = dseq // CHUNK
    seq_per_blk = max(1, min(bsz, MXU_SIDE // nc_ctx))
    assert bsz % seq_per_blk == 0 and n_ctx % dseq == 0
    s0_ctx = jnp.zeros((bsz, g_all, 4 * p_state), F32)
    y_ctx, fin_ctx = _ssm_call(u_a, 0, n_ctx, w_state, w_so, a_step, s0_ctx, seq_per_blk, nc_ctx, "ssm_ctx")
    s0_lat = _state_to_lanes(state_ssm[:, 0].astype(F32))
    y_lat, _ = _ssm_call(u_a, n_ctx // dseq, n_lat, w_state, w_so, a_step, s0_lat, 1, nc_lat, "ssm_lat")
    new_state = _lanes_to_state(fin_ctx, p_state)[:, None]

    pa = _glu_branch(y_ctx, y_lat, w_glu[0].astype(BF16), b_glu[0][None], w_branch_a[0].astype(BF16), tok_l)

    w_short3 = jnp.transpose(w_short[0].reshape(-1, 3, hw), (1, 0, 2))
    b_short3 = b_short[0].reshape(3, 1, hw)
    vx2, x0c2 = _short_conv(proj, n_gate_cols // hw, hw, w_short3, b_short3, n_ctx, seq, dseq, tm_e)
    rates = jnp.abs(jnp.linspace(math.log(DECAY_TARGET) / DECAY_FAST, math.log(DECAY_TARGET) / DECAY_SLOW,
                                 hw, dtype=F32))
    rates2 = jnp.concatenate([rates, rates])[None]
    fbias = filter_bias[0][None].astype(F32)
    w_bb = w_branch_b[0].astype(BF16)
    pbs = []
    for seq_len, n_seq, row0 in ((seq, bsz, 0), (dseq, dbsz, n_ctx // dseq)):
        tabs = _dft_tables(seq_len)
        a2, bm2, colabs = _filt_gen(seq_len, w_f1[0], b_f1[0][None], freq1[0][None], w_f2[0],
                                    b_f2[0][None], freq2[0][None], w_f3[0], b_f3[0][None], rates2)
        kspec = _filt_dft(tabs, a2, bm2, colabs, seq_len)
        spec = _hy_fwd(tabs, vx2, kspec, n_seq, seq_len, row0)
        pbs.append(_hy_inv(tabs, spec, vx2, x0c2, fbias, w_bb, n_seq, seq_len, row0))
    pb_ctx, pb_lat = pbs

    w_r = jnp.zeros((d, LANES), F32)
    w_r = w_r.at[:, :n_groups].set(w_router_group[0])
    w_r = w_r.at[:, n_groups:n_groups + n_groups * epg].set(
        jnp.transpose(w_router_expert[0], (1, 0, 2)).reshape(d, n_groups * epg))
    b_r = jnp.zeros((1, LANES), F32)
    b_r = b_r.at[0, :n_groups].set(b_router_group[0])
    b_r = b_r.at[0, n_groups:n_groups + n_groups * epg].set(b_router_expert[0].reshape(-1))
    w_r = w_r.astype(BF16)
    x2, h2, rout = _merge_out(xp, xs, pos, proj, 0, pa, pb_ctx, pb_lat, mod, g_norm2,
                              w_out[0].astype(BF16), w_r, b_r, n_groups, epg, tok_e)

    tm_m = min(512, t_all // n_groups)
    n_tiles = t_all // tm_m + n_groups
    grp = rout[:, LANES - 1].astype(jnp.int32)
    cls = rout[:, LANES - 2].astype(jnp.int32)
    plan = _moe_plan(grp, cls, n_groups, epg, tm_m, n_tiles)
    ns = d // 2 // LANES
    moe = _moe(h2.reshape(t_all, ns, LANES), w_r, b_r, w_exp_gate[0].astype(BF16), w_exp_up[0].astype(BF16),
               w_exp_down[0].astype(BF16), plan, n_groups, epg, tm_m)
    moe = moe.reshape(t_all * ns, LANES)

    gf = g_final[None]
    y_prompt = _final(x2, moe, mod, gf, 0, n_ctx, lambda i: 0, tm_l)
    lat_tiles = dseq // tm_l
    y_sample = _final(x2, moe, mod, gf, n_ctx // tm_l, n_lat, lambda i: 1 + i // lat_tiles, tm_l)
    return (y_prompt.reshape(bsz, seq, d), y_sample.reshape(dbsz, dseq, d), new_state)
```

```python
import functools
import math

import numpy as np
import jax
import jax.numpy as jnp
from jax import lax
from jax.experimental import pallas as pl
from jax.experimental.pallas import tpu as pltpu

F32 = jnp.float32
BF16 = jnp.bfloat16
EPS = 1e-6
GRID_W = 64
N_MOD = 6
TOP_K_INNER = 2
DECAY_TARGET = 1e-2
DECAY_FAST = 0.3
DECAY_SLOW = 1.5
LANES = 128
SUBLANES = 8
MXU_SIDE = 256
VMEM_LIMIT = 56 * 1024 * 1024


def _cp(sem, vmem=VMEM_LIMIT):
    return pltpu.CompilerParams(dimension_semantics=sem, vmem_limit_bytes=vmem)


def _dot(a, b):
    return jnp.dot(a, b, preferred_element_type=F32)


def _split(a):
    hi = a.astype(BF16)
    lo = (a - hi.astype(F32)).astype(BF16)
    return hi, lo


def _dot3(a, b):
    a_hi, a_lo = _split(a)
    b_hi, b_lo = _split(b)
    return _dot(a_hi, b_hi) + _dot(a_lo, b_hi) + _dot(a_hi, b_lo)


def _rms(x, g):
    ms = jnp.mean(x * x, axis=-1, keepdims=True)
    return x * lax.rsqrt(ms + EPS) * g


def _pack_halves(xb):
    n = xb.shape[1] // 2
    lo = lax.bitcast_convert_type(xb[:, :n].astype(F32), jnp.uint32)
    hi = lax.bitcast_convert_type(xb[:, n:].astype(F32), jnp.uint32)
    return (lo >> 16) | (hi & jnp.uint32(0xFFFF0000))


def _unpack_halves(w):
    lo = lax.bitcast_convert_type(w << 16, F32)
    hi = lax.bitcast_convert_type(w & jnp.uint32(0xFFFF0000), F32)
    return jnp.concatenate([lo, hi], axis=1).astype(BF16)


def _to_pair_rows(x, scr):
    m, n = x.shape
    ev, od = [], []
    for c in range(n // LANES):
        scr[c * m:(c + 1) * m, :] = x[:, c * LANES:(c + 1) * LANES]
        ev.append(scr[pl.ds(c * m, m // 2, stride=2), :])
        od.append(scr[pl.ds(c * m + 1, m // 2, stride=2), :])
    return jnp.concatenate(ev + od, axis=1)


def _from_pair_rows(x2, scr):
    h, n2 = x2.shape
    n = n2 // 2
    out = []
    for c in range(n // LANES):
        scr[pl.ds(2 * c * h, h, stride=2), :] = x2[:, c * LANES:(c + 1) * LANES]
        scr[pl.ds(2 * c * h + 1, h, stride=2), :] = x2[:, n + c * LANES:n + (c + 1) * LANES]
        out.append(scr[2 * c * h:2 * (c + 1) * h, :])
    return jnp.concatenate(out, axis=1)


def _store_row_tiles(ref, row0, xb):
    w = _pack_halves(xb)
    m, n = w.shape
    ns = n // LANES
    for c in range(ns):
        ref[pl.ds(row0 + c, m, stride=ns), :] = w[:, c * LANES:(c + 1) * LANES]


def _load_row_tiles(ref, row0, m, ns):
    w = jnp.concatenate([ref[pl.ds(row0 + c, m, stride=ns), :] for c in range(ns)], axis=1)
    return _unpack_halves(w)


def _ada_kernel(c_ref, w_ref, b_ref, o_ref):
    c = c_ref[...]
    a = c * jax.nn.sigmoid(c)
    o_ref[...] = _dot3(a, w_ref[...]) + b_ref[...]


def _ada_mod(cond8, w_ada, b_ada):
    d, n = w_ada.shape
    tn = min(n, 1024)
    return pl.pallas_call(
        _ada_kernel,
        out_shape=jax.ShapeDtypeStruct((SUBLANES, n), F32),
        grid=(n // tn,),
        in_specs=[pl.BlockSpec((SUBLANES, d), lambda j: (0, 0)),
                  pl.BlockSpec((d, tn), lambda j: (0, j)),
                  pl.BlockSpec((1, tn), lambda j: (0, j))],
        out_specs=pl.BlockSpec((SUBLANES, tn), lambda j: (0, j)),
        compiler_params=_cp(("arbitrary",)),
        name="ada_mod",
    )(cond8, w_ada, b_ada)


class _Tok:
    def __init__(self, n_ctx, n_lat, lat_len, tm):
        assert n_ctx % tm == 0 and lat_len % tm == 0 and n_lat % lat_len == 0
        self.tm = tm
        self.nct = n_ctx // tm
        self.nst = n_lat // tm
        self.per_seq = lat_len // tm
        self.n = self.nct + self.nst

    def ctx_idx(self, i):
        return jnp.minimum(i, self.nct - 1)

    def lat_idx(self, i):
        return jnp.maximum(i - self.nct, 0)

    def pos_idx(self, i):
        return self.lat_idx(i) % self.per_seq

    def cond(self, i):
        return jnp.where(i < self.nct, 0, 1 + self.lat_idx(i) // self.per_seq)


def _norm_mod_kernel(tok, xp_ref, xs_ref, pos_ref, sh_ref, sc_ref, g_ref, h_ref):
    i = pl.program_id(0)

    def modulate(x):
        y = _rms(x, g_ref[...])
        return (y * (1.0 + sc_ref[0]) + sh_ref[0]).astype(h_ref.dtype)

    @pl.when(i < tok.nct)
    def _():
        h_ref[...] = modulate(xp_ref[...])

    @pl.when(i >= tok.nct)
    def _():
        h_ref[...] = modulate(xs_ref[...] + pos_ref[...])


def _norm_mod(xp, xs, pos, mod, g1, tok):
    d = xp.shape[1]
    tm = tok.tm
    t_all = xp.shape[0] + xs.shape[0]
    return pl.pallas_call(
        functools.partial(_norm_mod_kernel, tok),
        out_shape=jax.ShapeDtypeStruct((t_all, d), BF16),
        grid=(tok.n,),
        in_specs=[
            pl.BlockSpec((tm, d), lambda i: (tok.ctx_idx(i), 0)),
            pl.BlockSpec((tm, d), lambda i: (tok.lat_idx(i), 0)),
            pl.BlockSpec((tm, d), lambda i: (tok.pos_idx(i), 0)),
            pl.BlockSpec((1, 1, d), lambda i: (tok.cond(i) * N_MOD + 0, 0, 0)),
            pl.BlockSpec((1, 1, d), lambda i: (tok.cond(i) * N_MOD + 1, 0, 0)),
            pl.BlockSpec((1, d), lambda i: (0, 0)),
        ],
        out_specs=pl.BlockSpec((tm, d), lambda i: (i, 0)),
        compiler_params=_cp(("parallel",)),
        name="norm_mod",
    )(xp, xs, pos, mod, mod, g1)


def _in_proj_kernel(n_sig, h_ref, w_ref, o_ref, wb_ref):
    n = pl.program_id(0)
    m = pl.program_id(1)

    @pl.when(m == 0)
    def _():
        wb_ref[...] = w_ref[...].astype(BF16)

    acc = _dot(h_ref[...], wb_ref[...])

    @pl.when(n >= n_sig)
    def _():
        o_ref[...] = acc.astype(o_ref.dtype)

    @pl.when(n < n_sig)
    def _():
        o_ref[...] = jax.nn.sigmoid(acc).astype(o_ref.dtype)


def _in_proj(h, w_in, sig_cols, lin_cols, out_dtype, tm, tn, name):
    t_all, d = h.shape
    for c in sig_cols + lin_cols:
        assert c % tn == 0
    assert t_all % tm == 0
    n_sig = (sig_cols[1] - sig_cols[0]) // tn
    n_lin = (lin_cols[1] - lin_cols[0]) // tn
    sig0, lin0 = sig_cols[0] // tn, lin_cols[0] // tn

    def w_blk(n):
        return jnp.where(n < n_sig, sig0 + n, lin0 + n - n_sig)

    return pl.pallas_call(
        functools.partial(_in_proj_kernel, n_sig),
        out_shape=jax.ShapeDtypeStruct((t_all, (n_sig + n_lin) * tn), out_dtype),
        grid=(n_sig + n_lin, t_all // tm),
        in_specs=[pl.BlockSpec((tm, d), lambda n, m: (m, 0)),
                  pl.BlockSpec((d, tn), lambda n, m: (0, w_blk(n)))],
        out_specs=pl.BlockSpec((tm, tn), lambda n, m: (m, n)),
        scratch_shapes=[pltpu.VMEM((d, tn), BF16)],
        compiler_params=_cp(("arbitrary", "arbitrary")),
        name=name,
    )(h, w_in)


CHUNK = 16
GROUP_PAD = 8


def _ssm_operators(a_re, a_im, log_dt, b_re, b_im, c_re, c_im, d_skip):
    t = CHUNK
    g, p = a_re.shape[1:]
    c = b_re.shape[-1]
    lam = lax.complex(a_re.astype(F32), a_im.astype(F32))
    dt = jnp.exp(log_dt.astype(F32))[..., None]
    ld = lam * dt
    lam_bar = jnp.exp(ld)
    b_bar = ((lam_bar - 1.0) / lam)[..., None] * lax.complex(b_re.astype(F32), b_im.astype(F32))
    c_mat = lax.complex(c_re.astype(F32), c_im.astype(F32))
    taus = jnp.arange(t + 1, dtype=F32)
    e_pow = jnp.exp(taus[None, None, :, None] * ld[:, :, None, :])

    k_all = jnp.real(jnp.einsum('dgop,dgtp,dgpi->dgtoi', c_mat, e_pow[:, :, :t], b_bar))
    kf, kb = k_all[0], k_all[1]
    ii = np.arange(t)[None, :, None]
    jj = np.arange(t)[None, None, :]
    tt = np.arange(t)[:, None, None]
    place_f = jnp.asarray((jj - ii == tt).astype(np.float32))
    place_b = jnp.asarray((ii - jj == tt).astype(np.float32))
    kf = kf.at[:, 0].add(jnp.eye(c, dtype=F32)[None] * d_skip.astype(F32).reshape(g, c, 1))
    m = jnp.einsum('tij,gtoc->gicjo', place_f, kf) + jnp.einsum('tij,gtoc->gicjo', place_b, kb)
    m = m.reshape(g, t * c, t * c)

    e_f = e_pow[0][:, ::-1][:, 1:]
    e_b = e_pow[1][:, :t]
    ws_f = e_f[:, :, :, None] * b_bar[0][:, None]
    ws_b = e_b[:, :, :, None] * b_bar[1][:, None]

    def rows_ic(x):
        return jnp.transpose(x, (0, 1, 3, 2)).reshape(g, t * c, p)

    w_state = jnp.concatenate([rows_ic(jnp.real(ws_f)), rows_ic(jnp.real(ws_b)),
                               rows_ic(jnp.imag(ws_f)), rows_ic(jnp.imag(ws_b))], axis=-1)

    ce_f = c_mat[0][:, None] * e_pow[0][:, 1:, None, :]
    ce_b = c_mat[1][:, None] * e_pow[1][:, ::-1][:, :t, None, :]

    def cols_jc(x):
        return jnp.transpose(x, (0, 3, 1, 2)).reshape(g, p, t * c)

    w_so = jnp.concatenate([cols_jc(jnp.real(ce_f)), cols_jc(jnp.real(ce_b)),
                            cols_jc(-jnp.imag(ce_f)), cols_jc(-jnp.imag(ce_b))], axis=1)
    w_out = jnp.concatenate([m, w_so], axis=1)

    a_t = e_pow[:, :, t]
    a_step = jnp.stack([jnp.concatenate([jnp.real(a_t[0]), jnp.real(a_t[1])], axis=-1),
                        jnp.concatenate([jnp.imag(a_t[0]), jnp.imag(a_t[1])], axis=-1)])
    return w_state.astype(BF16), w_out.astype(BF16), a_step


def _lane_perm(n_i, gb, c):
    n = n_i * gb * c
    src = np.arange(n)
    i, q, ch = src // (gb * c), (src // c) % gb, src % c
    dst = q * (n_i * c) + i * c + ch
    p = np.zeros((n, n), np.float32)
    p[src, dst] = 1.0
    return p


def _ssm_kernel(n_seq, n_chunk, gb, g_all, x_ref, ws_ref, wo_ref, a_ref, s0_ref, pin_ref, pout_ref,
                y_ref, fin_ref, xq_ref, za_ref, zb_ref):
    phase = pl.program_id(1)
    j = pl.program_id(2)
    gp = g_all + GROUP_PAD
    nck = n_seq * n_chunk
    half = LANES // 2
    hc = CHUNK // 2

    @pl.when(phase == 0)
    def _():
        halves = []
        for h in range(2):
            xcat = jnp.concatenate([x_ref[pl.ds(h * hc + i, nck, stride=CHUNK), :].astype(BF16)
                                    for i in range(hc)], axis=1)
            halves.append(_dot(xcat, pin_ref[...]).astype(BF16))
        for q in range(gb):
            xq = jnp.concatenate([hv[:, q * LANES:(q + 1) * LANES] for hv in halves], axis=1)
            xq_ref[j * gb + q] = xq
            z = _dot(xq, ws_ref[q])
            za_ref[pl.ds(j * gb + q, nck, stride=gp), :] = z[:, :LANES]
            zb_ref[pl.ds(j * gb + q, nck, stride=gp), :] = z[:, LANES:]

    @pl.when(jnp.logical_and(phase == 1, j == 0))
    def _():
        a_re = a_ref[0]
        a_im = a_ref[1]
        lane = lax.broadcasted_iota(jnp.int32, (g_all, LANES), 1)
        is_f = lane < half
        for s in range(n_seq):
            def step(k, carry):
                s_a, s_b = carry
                rf = pl.multiple_of((s * n_chunk + k) * gp, SUBLANES)
                rb = pl.multiple_of((s * n_chunk + n_chunk - 1 - k) * gp, SUBLANES)
                zf_a = za_ref[pl.ds(rf, g_all), :]
                zf_b = zb_ref[pl.ds(rf, g_all), :]
                zb_a = za_ref[pl.ds(rb, g_all), :]
                zb_b = zb_ref[pl.ds(rb, g_all), :]
                za_ref[pl.ds(rf, g_all), :] = jnp.where(is_f, s_a, zf_a)
                zb_ref[pl.ds(rf, g_all), :] = jnp.where(is_f, s_b, zf_b)
                za_ref[pl.ds(rb, g_all), :] = jnp.where(is_f, zb_a, s_a)
                zb_ref[pl.ds(rb, g_all), :] = jnp.where(is_f, zb_b, s_b)
                z_a = jnp.where(is_f, zf_a, zb_a)
                z_b = jnp.where(is_f, zf_b, zb_b)
                n_a = a_re * s_a - a_im * s_b + z_a
                n_b = a_re * s_b + a_im * s_a + z_b
                return n_a, n_b

            init = s0_ref[s]
            f_a, f_b = lax.fori_loop(0, n_chunk, step, (init[:, :LANES], init[:, LANES:]))
            fin_ref[s] = jnp.concatenate([f_a, f_b], axis=-1)

    @pl.when(phase == 1)
    def _():
        ys = []
        for q in range(gb):
            st_a = za_ref[pl.ds(j * gb + q, nck, stride=gp), :].astype(BF16)
            st_b = zb_ref[pl.ds(j * gb + q, nck, stride=gp), :].astype(BF16)
            lhs = jnp.concatenate([xq_ref[j * gb + q], st_a, st_b], axis=-1)
            ys.append(_dot(lhs, wo_ref[q]).astype(BF16))
        for h in range(2):
            ycat = jnp.concatenate([y[:, h * LANES:(h + 1) * LANES] for y in ys], axis=1)
            yall = _dot(ycat, pout_ref[...])
            for i in range(hc):
                y_ref[pl.ds(h * hc + i, nck, stride=CHUNK), :] = yall[:, i * LANES:(i + 1) * LANES]


def _ssm_call(u, row_blk0, n_rows, w_state, w_out, a_step, s0, n_seq, n_chunk, name):
    g_all, w, _ = w_state.shape
    nck = n_seq * n_chunk
    blk = nck * CHUNK
    n_blk = n_rows // blk
    gb = LANES * CHUNK // w
    ngb = g_all // gb
    gp = g_all + GROUP_PAD
    kern = functools.partial(_ssm_kernel, n_seq, n_chunk, gb, g_all)
    assert w == 2 * LANES
    perm = _lane_perm(CHUNK // 2, gb, LANES // gb)
    p_in = jnp.asarray(perm, BF16)
    p_out = jnp.asarray(perm.T, BF16)
    n_perm = perm.shape[0]
    const = lambda b, ph, j: (0, 0)
    return pl.pallas_call(
        kern,
        out_shape=(jax.ShapeDtypeStruct((n_rows, u.shape[1]), F32),
                   jax.ShapeDtypeStruct((n_blk * n_seq, g_all, w), F32)),
        grid=(n_blk, 2, ngb),
        in_specs=[
            pl.BlockSpec((blk, LANES), lambda b, ph, j: (row_blk0 + b, jnp.where(ph == 0, j, ngb - 1))),
            pl.BlockSpec((gb, w, w), lambda b, ph, j: (jnp.where(ph == 0, j, ngb - 1), 0, 0)),
            pl.BlockSpec((gb, 2 * w, w), lambda b, ph, j: (jnp.where(ph == 0, 0, j), 0, 0)),
            pl.BlockSpec((2, g_all, LANES), lambda b, ph, j: (0, 0, 0)),
            pl.BlockSpec((n_seq, g_all, w), lambda b, ph, j: (b, 0, 0)),
            pl.BlockSpec((n_perm, n_perm), const, pipeline_mode=pl.Buffered(1)),
            pl.BlockSpec((n_perm, n_perm), const, pipeline_mode=pl.Buffered(1)),
        ],
        out_specs=(
            pl.BlockSpec((blk, LANES), lambda b, ph, j: (b, jnp.where(ph == 0, 0, j))),
            pl.BlockSpec((n_seq, g_all, w), lambda b, ph, j: (b, 0, 0)),
        ),
        scratch_shapes=[pltpu.VMEM((g_all, nck, w), BF16),
                        pltpu.VMEM((nck * gp, LANES), F32), pltpu.VMEM((nck * gp, LANES), F32)],
        compiler_params=_cp(("arbitrary", "arbitrary", "arbitrary")),
        name=name,
    )(u, w_state, w_out, a_step, s0, p_in, p_out)


def _glu_kernel(nct, yc_ref, yl_ref, wg_ref, bg_ref, wb_ref, o_ref):
    i = pl.program_id(0)
    y = jax.nn.gelu(jnp.where(i < nct, yc_ref[...], yl_ref[...]))
    z = _dot(y.astype(BF16), wg_ref[...]) + bg_ref[...]
    ya = y * jax.nn.sigmoid(z)
    o_ref[...] = _dot(ya.astype(BF16), wb_ref[...]).astype(o_ref.dtype)


def _glu_branch(y_ctx, y_lat, w_glu, b_glu, w_ba, tok):
    c = y_ctx.shape[1]
    d = w_ba.shape[1]
    tm = tok.tm
    return pl.pallas_call(
        functools.partial(_glu_kernel, tok.nct),
        out_shape=jax.ShapeDtypeStruct((y_ctx.shape[0] + y_lat.shape[0], d), BF16),
        grid=(tok.n,),
        in_specs=[pl.BlockSpec((tm, c), lambda i: (tok.ctx_idx(i), 0)),
                  pl.BlockSpec((tm, c), lambda i: (tok.lat_idx(i), 0)),
                  pl.BlockSpec((c, c), lambda i: (0, 0)),
                  pl.BlockSpec((1, c), lambda i: (0, 0)),
                  pl.BlockSpec((c, d), lambda i: (0, 0))],
        out_specs=pl.BlockSpec((tm, d), lambda i: (i, 0)),
        compiler_params=_cp(("parallel",)),
        name="glu_branch",
    )(y_ctx, y_lat, w_glu, b_glu, w_ba)


def _short_conv_kernel(nct, per_ctx, per_lat, v_ref, x0_ref, x1_ref, vp_ref, x0p_ref, x1p_ref,
                       vn_ref, x0n_ref, x1n_ref, w_ref, b_ref, vx_ref, x0c_ref, scr):
    i = pl.program_id(0)
    tm = v_ref.shape[0]
    k = jnp.where(i < nct, i % per_ctx, (i - nct) % per_lat)
    per = jnp.where(i < nct, per_ctx, per_lat)
    first = k == 0
    last = k == per - 1
    row = lax.broadcasted_iota(jnp.int32, (tm, 1), 0)

    def conv(cur_ref, prev_ref, next_ref, part):
        x = cur_ref[...].astype(F32)
        hp = jnp.where(first, 0.0, prev_ref[SUBLANES - 1:SUBLANES, :].astype(F32))
        hn = jnp.where(last, 0.0, next_ref[0:1, :].astype(F32))
        xm = jnp.where(row == 0, hp, pltpu.roll(x, 1, axis=0))
        xq = jnp.where(row == tm - 1, hn, pltpu.roll(x, tm - 1, axis=0))
        w = w_ref[part]
        return b_ref[part] + xm * w[0:1, :] + x * w[1:2, :] + xq * w[2:3, :]

    v = conv(v_ref, vp_ref, vn_ref, 0)
    x0 = conv(x0_ref, x0p_ref, x0n_ref, 1)
    x1 = conv(x1_ref, x1p_ref, x1n_ref, 2)
    vx_ref[...] = _to_pair_rows(v * x1, scr).astype(vx_ref.dtype)
    x0c_ref[...] = _to_pair_rows(x0, scr).astype(x0c_ref.dtype)


def _short_conv(proj, col_blk0, hw, w_short3, b_short3, n_ctx, seq, dseq, tm):
    t = proj.shape[0]
    assert seq % tm == 0 and dseq % tm == 0
    nt = t // tm
    r8 = tm // SUBLANES
    nb8 = t // SUBLANES

    def cur(k):
        return pl.BlockSpec((tm, hw), lambda i: (i, col_blk0 + k))

    def prev(k):
        return pl.BlockSpec((SUBLANES, hw), lambda i: (jnp.maximum(i * r8 - 1, 0), col_blk0 + k))

    def nxt(k):
        return pl.BlockSpec((SUBLANES, hw), lambda i: (jnp.minimum((i + 1) * r8, nb8 - 1), col_blk0 + k))

    kern = functools.partial(_short_conv_kernel, n_ctx // tm, seq // tm, dseq // tm)
    return pl.pallas_call(
        kern,
        out_shape=(jax.ShapeDtypeStruct((t // 2, 2 * hw), BF16), jax.ShapeDtypeStruct((t // 2, 2 * hw), BF16)),
        grid=(nt,),
        in_specs=[cur(0), cur(1), cur(2), prev(0), prev(1), prev(2), nxt(0), nxt(1), nxt(2),
                  pl.BlockSpec((3, 3, hw), lambda i: (0, 0, 0)),
                  pl.BlockSpec((3, 1, hw), lambda i: (0, 0, 0))],
        out_specs=(pl.BlockSpec((tm // 2, 2 * hw), lambda i: (i, 0)),
                   pl.BlockSpec((tm // 2, 2 * hw), lambda i: (i, 0))),
        scratch_shapes=[pltpu.VMEM((hw // LANES * tm, LANES), F32)],
        compiler_params=_cp(("parallel",)),
        name="short_conv",
    )(proj, proj, proj, proj, proj, proj, proj, proj, proj, w_short3, b_short3)


def _filt_gen_kernel(seq_len, n_bands, wt_ref, wc_ref, ws_ref, b1_ref, f1_ref, w2_ref, b2_ref, f2_ref,
                     w3_ref, b3_ref, rate_ref, a_ref, bm_ref, abs_ref, scr):
    i = pl.program_id(0)
    tl = 2 * a_ref.shape[0]
    hw = a_ref.shape[1] // 2
    pos = lax.broadcasted_iota(jnp.int32, (tl, 1), 0) + i * tl
    t = pos.astype(F32) / seq_len
    bands = (lax.broadcasted_iota(jnp.int32, (1, wc_ref.shape[0]), 1) + 1).astype(F32)
    ang = (2.0 * math.pi) * t * bands
    pre = t * wt_ref[...] + _dot3(jnp.cos(ang), wc_ref[...]) + _dot3(jnp.sin(ang), ws_ref[...])
    h = jnp.sin(f1_ref[...] * (pre + b1_ref[...]))
    h = jnp.sin(f2_ref[...] * (_dot3(h, w2_ref[...]) + b2_ref[...]))
    h = _dot3(h, w3_ref[...]) + b3_ref[...]
    h = h * jnp.exp(-t * rate_ref[...])
    h_f = h[:, :hw]
    h_b = h[:, hw:]
    colabs = jnp.sum(jnp.abs(h_f) + jnp.abs(h_b), axis=0, keepdims=True)
    h_bp = jnp.where(pos == 0, 0.0, h_b)
    a_ref[...] = _to_pair_rows(h_f + h_bp, scr).astype(a_ref.dtype)
    bm_ref[...] = _to_pair_rows(h_bp - h_f, scr).astype(bm_ref.dtype)

    @pl.when(i == 0)
    def _():
        abs_ref[...] = colabs

    @pl.when(i > 0)
    def _():
        abs_ref[...] += colabs


def _filt_gen(seq_len, w_f1, b_f1, freq1, w_f2, b_f2, freq2, w_f3, b_f3, rates2):
    n_emb, hid_raw = w_f1.shape
    n_bands = (n_emb - 1) // 2
    hw2 = w_f3.shape[1]
    hw = hw2 // 2
    tl = min(seq_len, 512)
    hid = LANES
    assert n_bands <= LANES and hid_raw <= LANES

    def pad(x, rows, cols):
        return jnp.zeros((rows, cols), F32).at[:x.shape[0], :x.shape[1]].set(x.astype(F32))

    wt = pad(w_f1[0:1], 1, hid)
    wc = pad(w_f1[1:1 + n_bands], LANES, hid)
    ws = pad(w_f1[1 + n_bands:], LANES, hid)
    b_f1, freq1, b_f2, freq2 = (pad(x, 1, hid) for x in (b_f1, freq1, b_f2, freq2))
    w_f2 = pad(w_f2, hid, hid)
    w_f3 = pad(w_f3, hid, hw2)
    full = lambda shape: pl.BlockSpec(shape, lambda i: (0,) * len(shape))
    kern = functools.partial(_filt_gen_kernel, seq_len, n_bands)
    return pl.pallas_call(
        kern,
        out_shape=(jax.ShapeDtypeStruct((seq_len // 2, hw2), BF16), jax.ShapeDtypeStruct((seq_len // 2, hw2), BF16),
                   jax.ShapeDtypeStruct((1, hw), F32)),
        grid=(seq_len // tl,),
        in_specs=[full((1, hid)), full((LANES, hid)), full((LANES, hid)), full((1, hid)), full((1, hid)),
                  full((hid, hid)), full((1, hid)), full((1, hid)), full((hid, hw2)), full((1, hw2)),
                  full((1, hw2))],
        out_specs=(pl.BlockSpec((tl // 2, hw2), lambda i: (i, 0)), pl.BlockSpec((tl // 2, hw2), lambda i: (i, 0)),
                   full((1, hw))),
        scratch_shapes=[pltpu.VMEM((hw // LANES * tl, LANES), F32)],
        compiler_params=_cp(("arbitrary",)),
        name="filt_gen",
    )(wt, wc, ws, b_f1, freq1, w_f2, b_f2, freq2, w_f3, b_f3, rates2)


def _trig_tables(row_hi, row_lo, col, period):
    col = col[None, :]
    ang_lo = ((row_lo[:, None] * col) % period).astype(F32) * (2.0 * math.pi / period)
    ang_hi = ((row_hi[:, None] * col) % period).astype(F32) * (2.0 * math.pi / period)
    c_lo, s_lo = jnp.cos(ang_lo)[None], jnp.sin(ang_lo)[None]
    c_hi, s_hi = jnp.cos(ang_hi)[:, None], jnp.sin(ang_hi)[:, None]
    n = row_hi.shape[0] * row_lo.shape[0]
    ctab = (c_hi * c_lo - s_hi * s_lo).reshape(n, col.shape[1])
    stab = (s_hi * c_lo + c_hi * s_lo).reshape(n, col.shape[1])
    return ctab.astype(BF16), stab.astype(BF16)


def _dft_tables(seq_len):
    m = seq_len // 2
    r = 1
    while r * r < m:
        r *= 2
    assert m % r == 0
    i32 = jnp.int32
    idx = jnp.arange(m, dtype=i32)
    hi = jnp.arange(m // r, dtype=i32) * r
    lo = jnp.arange(r, dtype=i32)
    ce, se = _trig_tables(hi, lo, idx, 2 * m)
    co, so = _trig_tables(hi, lo, 2 * idx + 1, 2 * seq_len)
    cot, sot = _trig_tables(2 * hi, 2 * lo + 1, idx, 2 * seq_len)
    return ce, se, co, so, cot, sot


def _alt_row(n):
    return (1 - 2 * (lax.broadcasted_iota(jnp.int32, (SUBLANES, n), 1) % 2)).astype(BF16)


def _filt_dft_kernel(seq_len, ce_ref, se_ref, co_ref, so_ref, ae_ref, ao_ref, be_ref, bo_ref, abs_ref,
                     krl_ref, krh_ref, kil_ref, kih_ref, kmid_ref):
    i = pl.program_id(0)
    scale = (1.0 / seq_len) / (abs_ref[...] + EPS)
    ec = _dot(ce_ref[...], ae_ref[...])
    oc = _dot(co_ref[...], ao_ref[...])
    es = _dot(se_ref[...], be_ref[...])
    os_ = _dot(so_ref[...], bo_ref[...])
    krl_ref[...] = (ec + oc) * scale
    krh_ref[...] = (ec - oc) * scale
    kil_ref[...] = (es + os_) * scale
    kih_ref[...] = (os_ - es) * scale

    @pl.when(i == 0)
    def _():
        sgn = _alt_row(ae_ref.shape[0])
        kmid_ref[0:1, :] = _dot(sgn, ae_ref[...])[0:1] * scale
        kmid_ref[1:2, :] = _dot(sgn, bo_ref[...])[0:1] * scale


def _filt_dft(tabs, a2, bm2, colabs, seq_len):
    m = seq_len // 2
    hw = a2.shape[1] // 2
    ce, se, co, so = tabs[:4]
    tf = min(m, 256)
    kern = functools.partial(_filt_dft_kernel, seq_len)
    mode = pl.Buffered(1) if m * hw * 2 > BIG_BLOCK_BYTES // 2 else None
    tab = pl.BlockSpec((tf, m), lambda i: (i, 0))
    res = lambda c: pl.BlockSpec((m, hw), lambda i: (0, c), pipeline_mode=mode)
    out = pl.BlockSpec((tf, hw), lambda i: (i, 0))
    sds = jax.ShapeDtypeStruct((m, hw), F32)
    return pl.pallas_call(
        kern,
        out_shape=(sds, sds, sds, sds, jax.ShapeDtypeStruct((2, hw), F32)),
        grid=(m // tf,),
        in_specs=[tab, tab, tab, tab, res(0), res(1), res(0), res(1), pl.BlockSpec((1, hw), lambda i: (0, 0))],
        out_specs=(out, out, out, out, pl.BlockSpec((2, hw), lambda i: (0, 0))),
        compiler_params=_cp(("arbitrary",)),
        name="filt_dft",
    )(ce, se, co, so, a2, a2, bm2, bm2, colabs)


def _hy_fwd_kernel(ce_ref, se_ref, co_ref, so_ref, ve_ref, vo_ref, krl_ref, krh_ref, kil_ref, kih_ref, kmid_ref,
                   ae_ref, be_ref, ao_ref, bo_ref, ymid_ref):
    j = pl.program_id(1)
    tf, m = ce_ref.shape
    nb = ae_ref.shape[0]
    krl, krh, kil, kih = krl_ref[...], krh_ref[...], kil_ref[...], kih_ref[...]
    row = lax.broadcasted_iota(jnp.int32, (tf, 1), 0) + j * tf
    dc = jnp.where(row == 0, 0.5, 1.0)
    for s in range(nb):
        ve = ve_ref[s * m:(s + 1) * m, :]
        vo = vo_ref[s * m:(s + 1) * m, :]
        ec = _dot(ce_ref[...], ve)
        oc = _dot(co_ref[...], vo)
        es = _dot(se_ref[...], ve)
        os_ = _dot(so_ref[...], vo)
        p_lo, p_hi = ec + oc, ec - oc
        q_lo, q_hi = es + os_, os_ - es
        yre_lo = (p_lo * krl + q_lo * kil) * dc
        yim_lo = p_lo * kil - q_lo * krl
        yre_hi = (p_hi * krh + q_hi * kih) * dc
        yim_hi = p_hi * kih - q_hi * krh
        ae_ref[s] = (yre_lo + yre_hi).astype(ae_ref.dtype)
        be_ref[s] = (yim_lo - yim_hi).astype(be_ref.dtype)
        ao_ref[s] = (yre_lo - yre_hi).astype(ao_ref.dtype)
        bo_ref[s] = (yim_lo + yim_hi).astype(bo_ref.dtype)

    @pl.when(j == 0)
    def _():
        sgn = _alt_row(m)
        kr_m, ki_m = kmid_ref[0:1, :], kmid_ref[1:2, :]
        for s in range(nb):
            p_m = _dot(sgn, ve_ref[s * m:(s + 1) * m, :])[0:1]
            q_m = _dot(sgn, vo_ref[s * m:(s + 1) * m, :])[0:1]
            ymid_ref[s, 0:1, :] = p_m * kr_m + q_m * ki_m
            ymid_ref[s, 1:2, :] = p_m * ki_m - q_m * kr_m


BIG_BLOCK_BYTES = 4 * 1024 * 1024


def _seq_batch(seq_len, n_seq):
    nb = max(1, min(n_seq, 4 * MXU_SIDE // seq_len))
    while n_seq % nb:
        nb -= 1
    return nb


def _hy_fwd(tabs, vx2, kspec, n_seq, seq_len, row0_blk):
    m = seq_len // 2
    hw = vx2.shape[1] // 2
    ce, se, co, so = tabs[:4]
    krl, krh, kil, kih, kmid = kspec
    tf = min(m, 256)
    nb = _seq_batch(seq_len, n_seq)
    assert row0_blk % nb == 0
    mode = pl.Buffered(1) if nb * m * hw * 2 > BIG_BLOCK_BYTES // 2 else None
    tab = pl.BlockSpec((tf, m), lambda b, j: (j, 0))
    vspec = lambda c: pl.BlockSpec((nb * m, hw), lambda b, j: (row0_blk // nb + b, c), pipeline_mode=mode)
    kt = pl.BlockSpec((tf, hw), lambda b, j: (j, 0))
    out = pl.BlockSpec((nb, tf, hw), lambda b, j: (b, j, 0))
    sds = jax.ShapeDtypeStruct((n_seq, m, hw), BF16)
    return pl.pallas_call(
        _hy_fwd_kernel,
        out_shape=(sds, sds, sds, sds, jax.ShapeDtypeStruct((n_seq, 2, hw), F32)),
        grid=(n_seq // nb, m // tf),
        in_specs=[tab, tab, tab, tab, vspec(0), vspec(1), kt, kt, kt, kt,
                  pl.BlockSpec((2, hw), lambda b, j: (0, 0))],
        out_specs=(out, out, out, out, pl.BlockSpec((nb, 2, hw), lambda b, j: (b, 0, 0))),
        compiler_params=_cp(("arbitrary", "arbitrary")),
        name="hy_fwd",
    )(ce, se, co, so, vx2, vx2, krl, krh, kil, kih, kmid)


def _hy_inv_kernel(ce_ref, se_ref, cot_ref, sot_ref, ae_ref, be_ref, ao_ref, bo_ref, ymid_ref,
                   vx_ref, x0_ref, fb_ref, wb_ref, o_ref, scr):
    j = pl.program_id(1)
    tt = ce_ref.shape[0]
    nb = ae_ref.shape[0]
    hw = fb_ref.shape[1]
    row = lax.broadcasted_iota(jnp.int32, (tt, 1), 0) + j * tt
    sgn = (1 - 2 * (row % 2)).astype(F32)
    for s in range(nb):
        rows = slice(s * tt, (s + 1) * tt)
        ymid = ymid_ref[s]
        z_e = _dot(ce_ref[...], ae_ref[s]) - _dot(se_ref[...], be_ref[s]) + sgn * ymid[0:1]
        z_o = _dot(cot_ref[...], ao_ref[s]) - _dot(sot_ref[...], bo_ref[s]) - sgn * ymid[1:2]
        proj = []
        for par, z in ((0, z_e), (1, z_o)):
            cols = slice(par * hw, (par + 1) * hw)
            z = z + vx_ref[rows, cols].astype(F32) * fb_ref[...]
            yb = x0_ref[rows, cols].astype(F32) * z
            proj.append(_dot(yb.astype(BF16), wb_ref[...]))
        o_ref[2 * s * tt:2 * (s + 1) * tt, :] = _from_pair_rows(
            jnp.concatenate(proj, axis=1), scr).astype(o_ref.dtype)


def _hy_inv(tabs, spec, vx2, x0c2, fbias, w_bb, n_seq, seq_len, row0_blk):
    m = seq_len // 2
    hw = fbias.shape[1]
    d = w_bb.shape[1]
    ce, se, _, _, cot, sot = tabs
    ae, be, ao, bo, ymid = spec
    tt = min(m, 128)
    per = m // tt
    nb = _seq_batch(seq_len, n_seq) if per == 1 else 1
    assert row0_blk % nb == 0
    mode = pl.Buffered(1) if nb * m * hw * 2 > BIG_BLOCK_BYTES // 2 else None
    tab = pl.BlockSpec((tt, m), lambda b, j: (j, 0))
    res = pl.BlockSpec((nb, m, hw), lambda b, j: (b, 0, 0), pipeline_mode=mode)
    rows_in = pl.BlockSpec((nb * tt, 2 * hw), lambda b, j: ((row0_blk // nb + b) * per + j, 0))
    return pl.pallas_call(
        _hy_inv_kernel,
        out_shape=jax.ShapeDtypeStruct((n_seq * seq_len, d), BF16),
        grid=(n_seq // nb, per),
        in_specs=[tab, tab, tab, tab, res, res, res, res,
                  pl.BlockSpec((nb, 2, hw), lambda b, j: (b, 0, 0)),
                  rows_in, rows_in,
                  pl.BlockSpec((1, hw), lambda b, j: (0, 0)),
                  pl.BlockSpec((hw, d), lambda b, j: (0, 0), pipeline_mode=pl.Buffered(1))],
        out_specs=pl.BlockSpec((nb * 2 * tt, d), lambda b, j: (b * per + j, 0)),
        scratch_shapes=[pltpu.VMEM((d // LANES * 2 * tt, LANES), F32)],
        compiler_params=_cp(("arbitrary", "arbitrary")),
        name="hy_inv",
    )(ce, se, cot, sot, ae, be, ao, bo, ymid, vx2, x0c2, fbias, w_bb)


MERGE_COL_BLOCKS = 4

def _expert_pairs(epg):
    pairs = [(a, b) for a in range(epg) for b in range(a + 1, epg)]
    order = [pairs.pop(0)]
    while pairs:
        last = order[-1]
        nxt = next((p for p in pairs if set(p) & set(last)), pairs[0])
        pairs.remove(nxt)
        order.append(nxt)
    return order


def _merge_kernel(tok, n_groups, epg, xp_ref, xs_ref, pos_ref, g0_ref, g1_ref, pa_ref, pbc_ref, pbs_ref,
                  gate1_ref, sh2_ref, sc2_ref, gn_ref, wo_ref, wr_ref, br_ref,
                  x2_ref, h2_ref, rt_ref):
    i = pl.program_id(0)
    is_ctx = i < tok.nct
    tm, d = x2_ref.shape
    pb = jnp.where(is_ctx, pbc_ref[...], pbs_ref[...])
    merged = g0_ref[...] * pa_ref[...] + g1_ref[...] * pb
    cb = d // MERGE_COL_BLOCKS
    ssq = jnp.zeros((tm, 1), F32)
    for c in range(MERGE_COL_BLOCKS):
        cols = slice(c * cb, (c + 1) * cb)
        att = _dot(merged, wo_ref[:, cols])
        x = jnp.where(is_ctx, xp_ref[:, cols], xs_ref[:, cols] + pos_ref[:, cols])
        x2c = x + gate1_ref[0][:, cols] * att
        x2_ref[:, cols] = x2c
        ssq = ssq + jnp.sum(x2c * x2c, axis=-1, keepdims=True)
    x2 = x2_ref[...]
    h2 = x2 * lax.rsqrt(ssq * (1.0 / d) + EPS) * gn_ref[...] * (1.0 + sc2_ref[0]) + sh2_ref[0]
    h2b = h2.astype(BF16)
    _store_row_tiles(h2_ref, 0, h2b)
    logits = _dot(h2b, wr_ref[...]) + br_ref[...]
    lane = lax.broadcasted_iota(jnp.int32, logits.shape, 1)
    gl = jnp.where(lane < n_groups, logits, -jnp.inf)
    gmax = jnp.max(gl, axis=-1, keepdims=True)
    gidx = jnp.min(jnp.where(gl == gmax, lane, n_groups), axis=-1, keepdims=True)
    base = n_groups + gidx * epg
    el = jnp.where(jnp.logical_and(lane >= base, lane < base + epg), logits, -jnp.inf)
    i1 = jnp.min(jnp.where(el == jnp.max(el, axis=-1, keepdims=True), lane, LANES), axis=-1, keepdims=True)
    el = jnp.where(lane == i1, -jnp.inf, el)
    i2 = jnp.min(jnp.where(el == jnp.max(el, axis=-1, keepdims=True), lane, LANES), axis=-1, keepdims=True)
    lo = jnp.minimum(i1, i2) - base
    hi = jnp.maximum(i1, i2) - base
    cls = jnp.zeros_like(gidx)
    for code, (pa_, pb_) in enumerate(_expert_pairs(epg)):
        cls = cls + code * jnp.logical_and(lo == pa_, hi == pb_).astype(jnp.int32)
    rt_ref[...] = jnp.where(lane == LANES - 1, gidx.astype(F32),
                            jnp.where(lane == LANES - 2, cls.astype(F32), logits))


def _merge_out(xp, xs, pos, proj, gate_blk0, pa, pb_ctx, pb_lat, mod, g2, w_out, w_r, b_r, n_groups, epg, tok):
    d = xp.shape[1]
    tm = tok.tm
    t_all = xp.shape[0] + xs.shape[0]
    kern = functools.partial(_merge_kernel, tok, n_groups, epg)
    one = pl.Buffered(1)
    ns = d // 2 // LANES

    def modspec(k):
        return pl.BlockSpec((1, 1, d), lambda i: (tok.cond(i) * N_MOD + k, 0, 0))

    return pl.pallas_call(
        kern,
        out_shape=(jax.ShapeDtypeStruct((t_all, d), F32), jax.ShapeDtypeStruct((t_all * ns, LANES), jnp.uint32),
                   jax.ShapeDtypeStruct((t_all, LANES), F32)),
        grid=(tok.n,),
        in_specs=[
            pl.BlockSpec((tm, d), lambda i: (tok.ctx_idx(i), 0)),
            pl.BlockSpec((tm, d), lambda i: (tok.lat_idx(i), 0)),
            pl.BlockSpec((tm, d), lambda i: (tok.pos_idx(i), 0)),
            pl.BlockSpec((tm, d), lambda i: (i, gate_blk0)),
            pl.BlockSpec((tm, d), lambda i: (i, gate_blk0 + 1)),
            pl.BlockSpec((tm, d), lambda i: (i, 0)),
            pl.BlockSpec((tm, d), lambda i: (tok.ctx_idx(i), 0)),
            pl.BlockSpec((tm, d), lambda i: (tok.lat_idx(i), 0)),
            modspec(2), modspec(3), modspec(4),
            pl.BlockSpec((1, d), lambda i: (0, 0)),
            pl.BlockSpec((d, d), lambda i: (0, 0), pipeline_mode=one),
            pl.BlockSpec((d, LANES), lambda i: (0, 0)),
            pl.BlockSpec((1, LANES), lambda i: (0, 0)),
        ],
        out_specs=(pl.BlockSpec((tm, d), lambda i: (i, 0)), pl.BlockSpec((tm * ns, LANES), lambda i: (i, 0)),
                   pl.BlockSpec((tm, LANES), lambda i: (i, 0))),
        compiler_params=_cp(("parallel",)),
        name="merge_out",
    )(xp, xs, pos, proj, proj, pa, pb_ctx, pb_lat, mod, mod, mod, g2, w_out, w_r, b_r)


ROW_UNROLL = 8

def _moe_kernel(n_tiles, n_groups, epg, tg_ref, tnv_ref, rt_ref, cb_ref, kx_ref, wi_ref,
                h2_hbm, wr_ref, br_ref, wg_ref, wu_ref, wd_ref, out_hbm,
                xbuf, xb_ref, acc, obuf, gws, gsem, ssem):
    i = pl.program_id(0)
    j = pl.program_id(1)
    tm = xb_ref.shape[0]
    ns = h2_hbm.shape[1]
    nv = tnv_ref[i]
    slot = i % 2

    def buf_rows(buf, slot_, r, rows):
        return buf.at[pl.ds(pl.multiple_of((slot_ * tm + r) * ns, ns), rows * ns), :]

    def gather_copy(tok_row, slot_, r):
        return pltpu.make_async_copy(h2_hbm.at[tok_row], buf_rows(xbuf, slot_, r, 1), gsem.at[slot_])

    def gather_wait(slot_, rows):
        dst = buf_rows(xbuf, slot_, 0, rows)
        pltpu.make_async_copy(dst, dst, gsem.at[slot_]).wait()

    def scatter_copy(tok_row, slot_, r):
        return pltpu.make_async_copy(buf_rows(obuf, slot_, r, 1), out_hbm.at[tok_row], ssem.at[0])

    def scatter_wait(slot_, rows):
        src = buf_rows(obuf, slot_, 0, rows)
        pltpu.make_async_copy(src, src, ssem.at[0]).wait()

    def for_rows(count, fn):
        n_full = count // ROW_UNROLL

        def body(b, c):
            for k in range(ROW_UNROLL):
                fn(b * ROW_UNROLL + k)
            return c

        def tail(r, c):
            fn(r)
            return c

        lax.fori_loop(0, n_full, body, 0)
        lax.fori_loop(n_full * ROW_UNROLL, count, tail, 0)

    def wait_scatter(count, slot_):
        p = tm
        while p >= 1:
            @pl.when((count & p) != 0)
            def _(p=p):
                scatter_wait(slot_, p)
            p //= 2

    nv_prev = tnv_ref[jnp.maximum(i - 1, 0)]
    nv_next = tnv_ref[jnp.minimum(i + 1, n_tiles - 1)]
    prev_deferred = jnp.logical_and(i > 0, jnp.logical_and(nv_prev == tm, nv > 0))
    deferred = jnp.logical_and(nv == tm, nv_next > 0)

    @pl.when(j == 0)
    def _():
        @pl.when(i == 0)
        def _():
            for_rows(tm, lambda r: gather_copy(rt_ref[r], 0, r).start())

        @pl.when(jnp.logical_or(i == 0, nv_prev > 0))
        def _():
            gather_wait(slot, tm)

    @pl.when(jnp.logical_and(j == 0, nv > 0))
    def _():
        grp = tg_ref[i]
        xb = _load_row_tiles(xbuf, slot * tm * ns, tm, ns)
        xb_ref[...] = xb
        logits = _dot(xb, wr_ref[...]) + br_ref[...]
        lane = lax.broadcasted_iota(jnp.int32, logits.shape, 1)
        gl = jnp.where(lane < n_groups, logits, -jnp.inf)
        gmax = jnp.max(gl, axis=-1, keepdims=True)
        ge = jnp.exp(gl - gmax)
        g_w = jnp.sum(jnp.where(lane == grp, ge, 0.0), axis=-1, keepdims=True) / jnp.sum(ge, axis=-1, keepdims=True)
        base = n_groups + grp * epg
        e = [jnp.sum(jnp.where(lane == base + k, logits, 0.0), axis=-1, keepdims=True) for k in range(epg)]
        pairs = _expert_pairs(epg)
        row = lax.broadcasted_iota(jnp.int32, (tm, 1), 0)
        cls = jnp.zeros((tm, 1), jnp.int32)
        for c in range(len(pairs) - 1):
            cls = cls + (row >= cb_ref[i * len(pairs) + c]).astype(jnp.int32)
        sel = [functools.reduce(jnp.logical_or, [cls == c for c, p in enumerate(pairs) if k in p])
               for k in range(epg)]
        emax = functools.reduce(jnp.maximum, [jnp.where(sel[k], e[k], -jnp.inf) for k in range(epg)])
        ex = [jnp.where(sel[k], jnp.exp(e[k] - emax), 0.0) for k in range(epg)]
        den = functools.reduce(lambda a, b: a + b, ex)
        for k in range(epg):
            gws[k] = ex[k] / den * g_w
        acc[...] = jnp.zeros(acc.shape, acc.dtype)

    kx = kx_ref[i * epg + j]

    def expert_step(issue, scatter_prev):
        per = tm // TOP_K_INNER
        if issue:
            for k in range(per):
                r = kx * per + k
                if scatter_prev:
                    scatter_copy(rt_ref[(i - 1) * tm + r], 1 - slot, r).start()
                gather_copy(rt_ref[(i + 1) * tm + r], 1 - slot, r).start()
        xb = xb_ref[...]
        g = _dot(xb, wg_ref[...])
        u = _dot(xb, wu_ref[...])
        hid = (g * jax.nn.sigmoid(g)) * u * gws[j]
        acc[...] += _dot(hid.astype(BF16), wd_ref[...])

    run = jnp.logical_and(nv > 0, kx >= 0)
    carries = kx < TOP_K_INNER

    @pl.when(jnp.logical_and(run, jnp.logical_and(carries, prev_deferred)))
    def _():
        expert_step(True, True)

    @pl.when(jnp.logical_and(run, jnp.logical_and(carries, jnp.logical_not(prev_deferred))))
    def _():
        expert_step(True, False)

    @pl.when(jnp.logical_and(run, jnp.logical_not(carries)))
    def _():
        expert_step(False, False)

    @pl.when(j == epg - 1)
    def _():
        @pl.when(prev_deferred)
        def _():
            scatter_wait(1 - slot, tm)

        @pl.when(nv > 0)
        def _():
            _store_row_tiles(obuf, slot * tm * ns, acc[...].astype(BF16))

        @pl.when(jnp.logical_and(nv > 0, jnp.logical_not(deferred)))
        def _():
            for_rows(nv, lambda r: scatter_copy(rt_ref[i * tm + r], slot, r).start())
            wait_scatter(nv, slot)


def _moe(h2p, w_r, b_r, w_g, w_u, w_d, plan, n_groups, epg, tm):
    t, ns, _ = h2p.shape
    d = 2 * ns * LANES
    ff = w_g.shape[-1]
    n_tiles = plan[0].shape[0]
    assert tm & (tm - 1) == 0, "row-count waits decompose tm in binary"
    assert tm % TOP_K_INNER == 0
    kern = functools.partial(_moe_kernel, n_tiles, n_groups, epg)

    def const(i, j, *_):
        return (0, 0)

    def wblk(i, j, tg, tnv, rt, cb, kx, wi):
        return (wi[i * epg + j], 0, 0)

    grid_spec = pltpu.PrefetchScalarGridSpec(
        num_scalar_prefetch=len(plan),
        grid=(n_tiles, epg),
        in_specs=[
            pl.BlockSpec(memory_space=pl.ANY),
            pl.BlockSpec((d, LANES), const),
            pl.BlockSpec((1, LANES), const),
            pl.BlockSpec((None, d, ff), wblk),
            pl.BlockSpec((None, d, ff), wblk),
            pl.BlockSpec((None, ff, d), wblk),
        ],
        out_specs=pl.BlockSpec(memory_space=pl.ANY),
        scratch_shapes=[
            pltpu.VMEM((2 * tm * ns, LANES), jnp.uint32),
            pltpu.VMEM((tm, d), BF16),
            pltpu.VMEM((tm, d), F32),
            pltpu.VMEM((2 * tm * ns, LANES), jnp.uint32),
            pltpu.VMEM((epg, tm, 1), F32),
            pltpu.SemaphoreType.DMA((2,)),
            pltpu.SemaphoreType.DMA((1,)),
        ],
    )
    return pl.pallas_call(
        kern,
        out_shape=jax.ShapeDtypeStruct((t, ns, LANES), jnp.uint32),
        grid_spec=grid_spec,
        compiler_params=_cp(("arbitrary", "arbitrary")),
        name="moe",
    )(*plan, h2p, w_r, b_r, w_g, w_u, w_d)


def _moe_plan(grp, cls, n_groups, epg, tm, n_tiles):
    i32 = jnp.int32
    t = grp.shape[0]
    pairs = _expert_pairs(epg)
    n_cls = len(pairs)
    key = grp * n_cls + cls
    onehot = (key[:, None] == jnp.arange(n_groups * n_cls, dtype=i32)[None, :]).astype(i32)
    csum = jnp.cumsum(onehot, axis=0)
    kcount = csum[-1].reshape(n_groups, n_cls)
    counts = jnp.sum(kcount, axis=1)
    cls_end = jnp.cumsum(kcount, axis=1)
    cls_off = cls_end - kcount
    ntile_g = (counts + tm - 1) // tm
    tile_end = jnp.cumsum(ntile_g)
    tile_off = tile_end - ntile_g
    key_base = (tile_off[:, None] * tm + cls_off).reshape(-1)
    slot = jnp.sum(onehot * (key_base[None, :] + csum - 1), axis=1)
    row_tok = jnp.zeros((n_tiles * tm,), i32).at[slot].set(jnp.arange(t, dtype=i32))
    tiles = jnp.arange(n_tiles, dtype=i32)
    tile_grp = jnp.minimum(jnp.sum((tiles[:, None] >= tile_end[None, :]).astype(i32), axis=1), n_groups - 1)
    row0 = (tiles - tile_off[tile_grp]) * tm
    tile_nv = jnp.clip(counts[tile_grp] - row0, 0, tm)
    tile_nv = jnp.where(tiles < tile_end[-1], tile_nv, 0)
    cb = jnp.clip(cls_end[tile_grp] - row0[:, None], 0, tm)
    rows_in = jnp.diff(jnp.minimum(cb, tile_nv[:, None]), axis=1, prepend=0)
    member = jnp.asarray(np.array([[int(k in p) for k in range(epg)] for p in pairs], np.int32))
    needed = ((rows_in > 0).astype(i32) @ member) > 0
    kx = jnp.where(needed, jnp.cumsum(needed.astype(i32), axis=1) - 1, -1)
    blk = (tile_grp[:, None] * epg + jnp.arange(epg, dtype=i32)[None, :]).reshape(-1)
    n_pos = n_tiles * epg
    pos = jnp.arange(n_pos, dtype=i32)
    flat = needed.reshape(-1)
    nxt = lax.cummin(jnp.where(flat, pos, n_pos), axis=0, reverse=True)
    last = lax.cummax(jnp.where(flat, pos, -1), axis=0)
    hold = jnp.where(nxt < n_pos, nxt, jnp.maximum(last, 0))
    wi = blk[hold]
    return (tile_grp.astype(i32), tile_nv.astype(i32), row_tok, cb.reshape(-1).astype(i32),
            kx.reshape(-1).astype(i32), wi.astype(i32))


def _final_kernel(x2_ref, m_ref, gate2_ref, gf_ref, o_ref):
    tm, d = x2_ref.shape
    ns = d // 2 // LANES
    m = _load_row_tiles(m_ref, 0, tm, ns).astype(F32)
    x3 = x2_ref[...] + gate2_ref[0] * m
    o_ref[...] = _rms(x3, gf_ref[...])


def _final(x2, moe, mod, g_final, row_blk0, n_rows, cond_of_tile, tm):
    d = x2.shape[1]
    ns = d // 2 // LANES
    return pl.pallas_call(
        _final_kernel,
        out_shape=jax.ShapeDtypeStruct((n_rows, d), F32),
        grid=(n_rows // tm,),
        in_specs=[pl.BlockSpec((tm, d), lambda i: (row_blk0 + i, 0)),
                  pl.BlockSpec((tm * ns, LANES), lambda i: (row_blk0 + i, 0)),
                  pl.BlockSpec((1, 1, d), lambda i: (cond_of_tile(i) * N_MOD + 5, 0, 0)),
                  pl.BlockSpec((1, d), lambda i: (0, 0))],
        out_specs=pl.BlockSpec((tm, d), lambda i: (i, 0)),
        compiler_params=_cp(("parallel",)),
        name="final",
    )(x2, moe, mod, g_final)


def _grid_pos_embed(n_tokens, dim):
    rows_n = n_tokens // GRID_W
    quarter = dim // 4
    omega = 1.0 / (10000.0 ** (jnp.arange(quarter, dtype=F32) / quarter))

    def emb(n):
        ang = jnp.arange(n).astype(F32)[:, None] * omega[None, :]
        return jnp.concatenate([jnp.sin(ang), jnp.cos(ang)], axis=-1)

    e_r = jnp.broadcast_to(emb(rows_n)[:, None, :], (rows_n, GRID_W, dim // 2))
    e_c = jnp.broadcast_to(emb(GRID_W)[None, :, :], (rows_n, GRID_W, dim // 2))
    return jnp.concatenate([e_r, e_c], axis=-1).reshape(n_tokens, dim)


def _state_to_lanes(s):
    b, _, _, g, p = s.shape
    return jnp.transpose(s, (0, 3, 2, 1, 4)).reshape(b, g, 4 * p)


def _lanes_to_state(x, p):
    b, g, _ = x.shape
    return jnp.transpose(x.reshape(b, g, 2, 2, p), (0, 3, 2, 1, 4))


def kernel(x_prompt, x_sample, state_ssm, c, c_ctx, w_ada, b_ada, g_norm1, g_norm2, w_in, ssm_a_re, ssm_a_im, ssm_log_dt, ssm_b_re, ssm_b_im, ssm_c_re, ssm_c_im, ssm_d, w_glu, b_glu, w_short, b_short, w_f1, b_f1, freq1, w_f2, b_f2, freq2, w_f3, b_f3, filter_bias, w_branch_a, w_branch_b, w_out, w_router_group, b_router_group, w_router_expert, b_router_expert, w_exp_gate, w_exp_up, w_exp_down, g_final):
    depth = w_ada.shape[0]
    assert depth == 1, "single-layer pipeline"
    bsz, seq, d = x_prompt.shape
    dbsz, dseq, _ = x_sample.shape
    g_all, p_state = ssm_a_re.shape[2:]
    ssm_w = ssm_d.shape[1]
    hw = filter_bias.shape[1]
    n_groups, _, epg = w_router_expert.shape[1:]
    ff = w_exp_gate.shape[-1]
    assert ssm_w // g_all * CHUNK == MXU_SIDE and 4 * p_state == MXU_SIDE
    assert ssm_w == hw and d == 2 * hw
    n_ctx = bsz * seq
    n_lat = dbsz * dseq
    t_all = n_ctx + n_lat

    xp = x_prompt.reshape(n_ctx, d)
    xs = x_sample.reshape(n_lat, d)
    pos = _grid_pos_embed(dseq, d)

    n_cond = 1 + dbsz
    cond8 = jnp.zeros((SUBLANES, d), F32).at[0].set(c_ctx).at[1:n_cond].set(c)
    mod = _ada_mod(cond8, w_ada[0], b_ada[0][None])
    mod = mod[:n_cond].reshape(n_cond * N_MOD, 1, d)

    tm_e = min(256, seq, dseq)
    tok_e = _Tok(n_ctx, n_lat, dseq, tm_e)
    tm_l = min(512, n_ctx, dseq)
    tok_l = _Tok(n_ctx, n_lat, dseq, tm_l)
    gate_col0 = ssm_w + 3 * hw
    h1 = _norm_mod(xp, xs, pos, mod, g_norm1, tok_l)
    tm_p, tn_p = min(1024, t_all), min(1024, hw)
    n_in = w_in.shape[2]
    u_a = _in_proj(h1, w_in[0], (0, 0), (0, ssm_w), F32, tm_p, tn_p, "in_proj_a")
    proj = _in_proj(h1, w_in[0], (gate_col0, n_in), (ssm_w, gate_col0), BF16, tm_p, tn_p, "in_proj")
    n_gate_cols = n_in - gate_col0

    w_state, w_so, a_step = _ssm_operators(ssm_a_re[0], ssm_a_im[0], ssm_log_dt[0], ssm_b_re[0], ssm_b_im[0],
                                           ssm_c_re[0], ssm_c_im[0], ssm_d[0])
    nc_ctx = seq // CHUNK
    nc_lat = dseq // CHUNK
    seq_per_blk = max(1, min(bsz, MXU_SIDE // nc_ctx))
    assert bsz % seq_per_blk == 0 and n_ctx % dseq == 0
    s0_ctx = jnp.zeros((bsz, g_all, 4 * p_state), F32)
    y_ctx, fin_ctx = _ssm_call(u_a, 0, n_ctx, w_state, w_so, a_step, s0_ctx, seq_per_blk, nc_ctx, "ssm_ctx")
    s0_lat = _state_to_lanes(state_ssm[:, 0].astype(F32))
    y_lat, _ = _ssm_call(u_a, n_ctx // dseq, n_lat, w_state, w_so, a_step, s0_lat, 1, nc_lat, "ssm_lat")
    new_state = _lanes_to_state(fin_ctx, p_state)[:, None]

    pa = _glu_branch(y_ctx, y_lat, w_glu[0].astype(BF16), b_glu[0][None], w_branch_a[0].astype(BF16), tok_l)

    w_short3 = jnp.transpose(w_short[0].reshape(-1, 3, hw), (1, 0, 2))
    b_short3 = b_short[0].reshape(3, 1, hw)
    vx2, x0c2 = _short_conv(proj, n_gate_cols // hw, hw, w_short3, b_short3, n_ctx, seq, dseq, tm_e)
    rates = jnp.abs(jnp.linspace(math.log(DECAY_TARGET) / DECAY_FAST, math.log(DECAY_TARGET) / DECAY_SLOW,
                                 hw, dtype=F32))
    rates2 = jnp.concatenate([rates, rates])[None]
    fbias = filter_bias[0][None].astype(F32)
    w_bb = w_branch_b[0].astype(BF16)
    pbs = []
    for seq_len, n_seq, row0 in ((seq, bsz, 0), (dseq, dbsz, n_ctx // dseq)):
        tabs = _dft_tables(seq_len)
        a2, bm2, colabs = _filt_gen(seq_len, w_f1[0], b_f1[0][None], freq1[0][None], w_f2[0],
                                    b_f2[0][None], freq2[0][None], w_f3[0], b_f3[0][None], rates2)
        kspec = _filt_dft(tabs, a2, bm2, colabs, seq_len)
        spec = _hy_fwd(tabs, vx2, kspec, n_seq, seq_len, row0)
        pbs.append(_hy_inv(tabs, spec, vx2, x0c2, fbias, w_bb, n_seq, seq_len, row0))
    pb_ctx, pb_lat = pbs

    w_r = jnp.zeros((d, LANES), F32)
    w_r = w_r.at[:, :n_groups].set(w_router_group[0])
    w_r = w_r.at[:, n_groups:n_groups + n_groups * epg].set(
        jnp.transpose(w_router_expert[0], (1, 0, 2)).reshape(d, n_groups * epg))
    b_r = jnp.zeros((1, LANES), F32)
    b_r = b_r.at[0, :n_groups].set(b_router_group[0])
    b_r = b_r.at[0, n_groups:n_groups + n_groups * epg].set(b_router_expert[0].reshape(-1))
    w_r = w_r.astype(BF16)
    x2, h2, rout = _merge_out(xp, xs, pos, proj, 0, pa, pb_ctx, pb_lat, mod, g_norm2,
                              w_out[0].astype(BF16), w_r, b_r, n_groups, epg, tok_e)

    tm_m = min(512, t_all // n_groups)
    n_tiles = t_all // tm_m + n_groups
    grp = rout[:, LANES - 1].astype(jnp.int32)
    cls = rout[:, LANES - 2].astype(jnp.int32)
    plan = _moe_plan(grp, cls, n_groups, epg, tm_m, n_tiles)
    ns = d // 2 // LANES
    moe = _moe(h2.reshape(t_all, ns, LANES), w_r, b_r, w_exp_gate[0].astype(BF16), w_exp_up[0].astype(BF16),
               w_exp_down[0].astype(BF16), plan, n_groups, epg, tm_m)
    moe = moe.reshape(t_all * ns, LANES)

    gf = g_final[None]
    y_prompt = _final(x2, moe, mod, gf, 0, n_ctx, lambda i: 0, tm_l)
    lat_tiles = dseq // tm_l
    y_sample = _final(x2, moe, mod, gf, n_ctx // tm_l, n_lat, lambda i: 1 + i // lat_tiles, tm_l)
    return (y_prompt.reshape(bsz, seq, d), y_sample.reshape(dbsz, dseq, d), new_state)
```

```python
import functools
import math

import numpy as np
import jax
import jax.numpy as jnp
from jax import lax
from jax.experimental import pallas as pl
from jax.experimental.pallas import tpu as pltpu

F32 = jnp.float32
BF16 = jnp.bfloat16
EPS = 1e-6
GRID_W = 64
N_MOD = 6
TOP_K_INNER = 2
DECAY_TARGET = 1e-2
DECAY_FAST = 0.3
DECAY_SLOW = 1.5
LANES = 128
SUBLANES = 8
MXU_SIDE = 256
VMEM_LIMIT = 56 * 1024 * 1024


def _cp(sem, vmem=VMEM_LIMIT):
    return pltpu.CompilerParams(dimension_semantics=sem, vmem_limit_bytes=vmem)


def _dot(a, b):
    return jnp.dot(a, b, preferred_element_type=F32)


def _split(a):
    hi = a.astype(BF16)
    lo = (a - hi.astype(F32)).astype(BF16)
    return hi, lo


def _dot3(a, b):
    a_hi, a_lo = _split(a)
    b_hi, b_lo = _split(b)
    return _dot(a_hi, b_hi) + _dot(a_lo, b_hi) + _dot(a_hi, b_lo)


def _rms(x, g):
    ms = jnp.mean(x * x, axis=-1, keepdims=True)
    return x * lax.rsqrt(ms + EPS) * g


def _pack_halves(xb):
    n = xb.shape[1] // 2
    lo = lax.bitcast_convert_type(xb[:, :n].astype(F32), jnp.uint32)
    hi = lax.bitcast_convert_type(xb[:, n:].astype(F32), jnp.uint32)
    return (lo >> 16) | (hi & jnp.uint32(0xFFFF0000))


def _unpack_halves(w):
    lo = lax.bitcast_convert_type(w << 16, F32)
    hi = lax.bitcast_convert_type(w & jnp.uint32(0xFFFF0000), F32)
    return jnp.concatenate([lo, hi], axis=1).astype(BF16)


def _to_pair_rows(x, scr):
    m, n = x.shape
    ev, od = [], []
    for c in range(n // LANES):
        scr[c * m:(c + 1) * m, :] = x[:, c * LANES:(c + 1) * LANES]
        ev.append(scr[pl.ds(c * m, m // 2, stride=2), :])
        od.append(scr[pl.ds(c * m + 1, m // 2, stride=2), :])
    return jnp.concatenate(ev + od, axis=1)


def _from_pair_rows(x2, scr):
    h, n2 = x2.shape
    n = n2 // 2
    out = []
    for c in range(n // LANES):
        scr[pl.ds(2 * c * h, h, stride=2), :] = x2[:, c * LANES:(c + 1) * LANES]
        scr[pl.ds(2 * c * h + 1, h, stride=2), :] = x2[:, n + c * LANES:n + (c + 1) * LANES]
        out.append(scr[2 * c * h:2 * (c + 1) * h, :])
    return jnp.concatenate(out, axis=1)


def _store_row_tiles(ref, row0, xb):
    w = _pack_halves(xb)
    m, n = w.shape
    ns = n // LANES
    for c in range(ns):
        ref[pl.ds(row0 + c, m, stride=ns), :] = w[:, c * LANES:(c + 1) * LANES]


def _load_row_tiles(ref, row0, m, ns):
    w = jnp.concatenate([ref[pl.ds(row0 + c, m, stride=ns), :] for c in range(ns)], axis=1)
    return _unpack_halves(w)


def _ada_kernel(c_ref, w_ref, b_ref, o_ref):
    c = c_ref[...]
    a = c * jax.nn.sigmoid(c)
    o_ref[...] = _dot3(a, w_ref[...]) + b_ref[...]


def _ada_mod(cond8, w_ada, b_ada):
    d, n = w_ada.shape
    tn = min(n, 1024)
    return pl.pallas_call(
        _ada_kernel,
        out_shape=jax.ShapeDtypeStruct((SUBLANES, n), F32),
        grid=(n // tn,),
        in_specs=[pl.BlockSpec((SUBLANES, d), lambda j: (0, 0)),
                  pl.BlockSpec((d, tn), lambda j: (0, j)),
                  pl.BlockSpec((1, tn), lambda j: (0, j))],
        out_specs=pl.BlockSpec((SUBLANES, tn), lambda j: (0, j)),
        compiler_params=_cp(("arbitrary",)),
        name="ada_mod",
    )(cond8, w_ada, b_ada)


class _Tok:
    def __init__(self, n_ctx, n_lat, lat_len, tm):
        assert n_ctx % tm == 0 and lat_len % tm == 0 and n_lat % lat_len == 0
        self.tm = tm
        self.nct = n_ctx // tm
        self.nst = n_lat // tm
        self.per_seq = lat_len // tm
        self.n = self.nct + self.nst

    def ctx_idx(self, i):
        return jnp.minimum(i, self.nct - 1)

    def lat_idx(self, i):
        return jnp.maximum(i - self.nct, 0)

    def pos_idx(self, i):
        return self.lat_idx(i) % self.per_seq

    def cond(self, i):
        return jnp.where(i < self.nct, 0, 1 + self.lat_idx(i) // self.per_seq)


def _norm_mod_kernel(tok, xp_ref, xs_ref, pos_ref, sh_ref, sc_ref, g_ref, h_ref):
    i = pl.program_id(0)

    def modulate(x):
        y = _rms(x, g_ref[...])
        return (y * (1.0 + sc_ref[0]) + sh_ref[0]).astype(h_ref.dtype)

    @pl.when(i < tok.nct)
    def _():
        h_ref[...] = modulate(xp_ref[...])

    @pl.when(i >= tok.nct)
    def _():
        h_ref[...] = modulate(xs_ref[...] + pos_ref[...])


def _norm_mod(xp, xs, pos, mod, g1, tok):
    d = xp.shape[1]
    tm = tok.tm
    t_all = xp.shape[0] + xs.shape[0]
    return pl.pallas_call(
        functools.partial(_norm_mod_kernel, tok),
        out_shape=jax.ShapeDtypeStruct((t_all, d), BF16),
        grid=(tok.n,),
        in_specs=[
            pl.BlockSpec((tm, d), lambda i: (tok.ctx_idx(i), 0)),
            pl.BlockSpec((tm, d), lambda i: (tok.lat_idx(i), 0)),
            pl.BlockSpec((tm, d), lambda i: (tok.pos_idx(i), 0)),
            pl.BlockSpec((1, 1, d), lambda i: (tok.cond(i) * N_MOD + 0, 0, 0)),
            pl.BlockSpec((1, 1, d), lambda i: (tok.cond(i) * N_MOD + 1, 0, 0)),
            pl.BlockSpec((1, d), lambda i: (0, 0)),
        ],
        out_specs=pl.BlockSpec((tm, d), lambda i: (i, 0)),
        compiler_params=_cp(("parallel",)),
        name="norm_mod",
    )(xp, xs, pos, mod, mod, g1)


def _in_proj_kernel(n_sig, h_ref, w_ref, o_ref, wb_ref):
    n = pl.program_id(0)
    m = pl.program_id(1)

    @pl.when(m == 0)
    def _():
        wb_ref[...] = w_ref[...].astype(BF16)

    acc = _dot(h_ref[...], wb_ref[...])

    @pl.when(n >= n_sig)
    def _():
        o_ref[...] = acc.astype(o_ref.dtype)

    @pl.when(n < n_sig)
    def _():
        o_ref[...] = jax.nn.sigmoid(acc).astype(o_ref.dtype)


def _in_proj(h, w_in, sig_cols, lin_cols, out_dtype, tm, tn, name):
    t_all, d = h.shape
    for c in sig_cols + lin_cols:
        assert c % tn == 0
    assert t_all % tm == 0
    n_sig = (sig_cols[1] - sig_cols[0]) // tn
    n_lin = (lin_cols[1] - lin_cols[0]) // tn
    sig0, lin0 = sig_cols[0] // tn, lin_cols[0] // tn

    def w_blk(n):
        return jnp.where(n < n_sig, sig0 + n, lin0 + n - n_sig)

    return pl.pallas_call(
        functools.partial(_in_proj_kernel, n_sig),
        out_shape=jax.ShapeDtypeStruct((t_all, (n_sig + n_lin) * tn), out_dtype),
        grid=(n_sig + n_lin, t_all // tm),
        in_specs=[pl.BlockSpec((tm, d), lambda n, m: (m, 0)),
                  pl.BlockSpec((d, tn), lambda n, m: (0, w_blk(n)))],
        out_specs=pl.BlockSpec((tm, tn), lambda n, m: (m, n)),
        scratch_shapes=[pltpu.VMEM((d, tn), BF16)],
        compiler_params=_cp(("arbitrary", "arbitrary")),
        name=name,
    )(h, w_in)


CHUNK = 16
GROUP_PAD = 8


def _ssm_operators(a_re, a_im, log_dt, b_re, b_im, c_re, c_im, d_skip):
    t = CHUNK
    g, p = a_re.shape[1:]
    c = b_re.shape[-1]
    lam = lax.complex(a_re.astype(F32), a_im.astype(F32))
    dt = jnp.exp(log_dt.astype(F32))[..., None]
    ld = lam * dt
    lam_bar = jnp.exp(ld)
    b_bar = ((lam_bar - 1.0) / lam)[..., None] * lax.complex(b_re.astype(F32), b_im.astype(F32))
    c_mat = lax.complex(c_re.astype(F32), c_im.astype(F32))
    taus = jnp.arange(t + 1, dtype=F32)
    e_pow = jnp.exp(taus[None, None, :, None] * ld[:, :, None, :])

    k_all = jnp.real(jnp.einsum('dgop,dgtp,dgpi->dgtoi', c_mat, e_pow[:, :, :t], b_bar))
    kf, kb = k_all[0], k_all[1]
    ii = np.arange(t)[None, :, None]
    jj = np.arange(t)[None, None, :]
    tt = np.arange(t)[:, None, None]
    place_f = jnp.asarray((jj - ii == tt).astype(np.float32))
    place_b = jnp.asarray((ii - jj == tt).astype(np.float32))
    kf = kf.at[:, 0].add(jnp.eye(c, dtype=F32)[None] * d_skip.astype(F32).reshape(g, c, 1))
    m = jnp.einsum('tij,gtoc->gicjo', place_f, kf) + jnp.einsum('tij,gtoc->gicjo', place_b, kb)
    m = m.reshape(g, t * c, t * c)

    e_f = e_pow[0][:, ::-1][:, 1:]
    e_b = e_pow[1][:, :t]
    ws_f = e_f[:, :, :, None] * b_bar[0][:, None]
    ws_b = e_b[:, :, :, None] * b_bar[1][:, None]

    def rows_ic(x):
        return jnp.transpose(x, (0, 1, 3, 2)).reshape(g, t * c, p)

    w_state = jnp.concatenate([rows_ic(jnp.real(ws_f)), rows_ic(jnp.real(ws_b)),
                               rows_ic(jnp.imag(ws_f)), rows_ic(jnp.imag(ws_b))], axis=-1)

    ce_f = c_mat[0][:, None] * e_pow[0][:, 1:, None, :]
    ce_b = c_mat[1][:, None] * e_pow[1][:, ::-1][:, :t, None, :]

    def cols_jc(x):
        return jnp.transpose(x, (0, 3, 1, 2)).reshape(g, p, t * c)

    w_so = jnp.concatenate([cols_jc(jnp.real(ce_f)), cols_jc(jnp.real(ce_b)),
                            cols_jc(-jnp.imag(ce_f)), cols_jc(-jnp.imag(ce_b))], axis=1)
    w_out = jnp.concatenate([m, w_so], axis=1)

    a_t = e_pow[:, :, t]
    a_step = jnp.stack([jnp.concatenate([jnp.real(a_t[0]), jnp.real(a_t[1])], axis=-1),
                        jnp.concatenate([jnp.imag(a_t[0]), jnp.imag(a_t[1])], axis=-1)])
    return w_state.astype(BF16), w_out.astype(BF16), a_step


def _lane_perm(n_i, gb, c):
    n = n_i * gb * c
    src = np.arange(n)
    i, q, ch = src // (gb * c), (src // c) % gb, src % c
    dst = q * (n_i * c) + i * c + ch
    p = np.zeros((n, n), np.float32)
    p[src, dst] = 1.0
    return p


def _ssm_kernel(n_seq, n_chunk, gb, g_all, x_ref, ws_ref, wo_ref, a_ref, s0_ref, pin_ref, pout_ref,
                y_ref, fin_ref, xq_ref, za_ref, zb_ref):
    phase = pl.program_id(1)
    j = pl.program_id(2)
    gp = g_all + GROUP_PAD
    nck = n_seq * n_chunk
    half = LANES // 2
    hc = CHUNK // 2

    @pl.when(phase == 0)
    def _():
        halves = []
        for h in range(2):
            xcat = jnp.concatenate([x_ref[pl.ds(h * hc + i, nck, stride=CHUNK), :].astype(BF16)
                                    for i in range(hc)], axis=1)
            halves.append(_dot(xcat, pin_ref[...]).astype(BF16))
        for q in range(gb):
            xq = jnp.concatenate([hv[:, q * LANES:(q + 1) * LANES] for hv in halves], axis=1)
            xq_ref[j * gb + q] = xq
            z = _dot(xq, ws_ref[q])
            za_ref[pl.ds(j * gb + q, nck, stride=gp), :] = z[:, :LANES]
            zb_ref[pl.ds(j * gb + q, nck, stride=gp), :] = z[:, LANES:]

    @pl.when(jnp.logical_and(phase == 1, j == 0))
    def _():
        a_re = a_ref[0]
        a_im = a_ref[1]
        lane = lax.broadcasted_iota(jnp.int32, (g_all, LANES), 1)
        is_f = lane < half
        for s in range(n_seq):
            def step(k, carry):
                s_a, s_b = carry
                rf = pl.multiple_of((s * n_chunk + k) * gp, SUBLANES)
                rb = pl.multiple_of((s * n_chunk + n_chunk - 1 - k) * gp, SUBLANES)
                zf_a = za_ref[pl.ds(rf, g_all), :]
                zf_b = zb_ref[pl.ds(rf, g_all), :]
                zb_a = za_ref[pl.ds(rb, g_all), :]
                zb_b = zb_ref[pl.ds(rb, g_all), :]
                za_ref[pl.ds(rf, g_all), :] = jnp.where(is_f, s_a, zf_a)
                zb_ref[pl.ds(rf, g_all), :] = jnp.where(is_f, s_b, zf_b)
                za_ref[pl.ds(rb, g_all), :] = jnp.where(is_f, zb_a, s_a)
                zb_ref[pl.ds(rb, g_all), :] = jnp.where(is_f, zb_b, s_b)
                z_a = jnp.where(is_f, zf_a, zb_a)
                z_b = jnp.where(is_f, zf_b, zb_b)
                n_a = a_re * s_a - a_im * s_b + z_a
                n_b = a_re * s_b + a_im * s_a + z_b
                return n_a, n_b

            init = s0_ref[s]
            f_a, f_b = lax.fori_loop(0, n_chunk, step, (init[:, :LANES], init[:, LANES:]))
            fin_ref[s] = jnp.concatenate([f_a, f_b], axis=-1)

    @pl.when(phase == 1)
    def _():
        ys = []
        for q in range(gb):
            st_a = za_ref[pl.ds(j * gb + q, nck, stride=gp), :].astype(BF16)
            st_b = zb_ref[pl.ds(j * gb + q, nck, stride=gp), :].astype(BF16)
            lhs = jnp.concatenate([xq_ref[j * gb + q], st_a, st_b], axis=-1)
            ys.append(_dot(lhs, wo_ref[q]).astype(BF16))
        for h in range(2):
            ycat = jnp.concatenate([y[:, h * LANES:(h + 1) * LANES] for y in ys], axis=1)
            yall = _dot(ycat, pout_ref[...])
            for i in range(hc):
                y_ref[pl.ds(h * hc + i, nck, stride=CHUNK), :] = yall[:, i * LANES:(i + 1) * LANES]


def _ssm_call(u, row_blk0, n_rows, w_state, w_out, a_step, s0, n_seq, n_chunk, name):
    g_all, w, _ = w_state.shape
    nck = n_seq * n_chunk
    blk = nck * CHUNK
    n_blk = n_rows // blk
    gb = LANES * CHUNK // w
    ngb = g_all // gb
    gp = g_all + GROUP_PAD
    kern = functools.partial(_ssm_kernel, n_seq, n_chunk, gb, g_all)
    assert w == 2 * LANES
    perm = _lane_perm(CHUNK // 2, gb, LANES // gb)
    p_in = jnp.asarray(perm, BF16)
    p_out = jnp.asarray(perm.T, BF16)
    n_perm = perm.shape[0]
    const = lambda b, ph, j: (0, 0)
    return pl.pallas_call(
        kern,
        out_shape=(jax.ShapeDtypeStruct((n_rows, u.shape[1]), F32),
                   jax.ShapeDtypeStruct((n_blk * n_seq, g_all, w), F32)),
        grid=(n_blk, 2, ngb),
        in_specs=[
            pl.BlockSpec((blk, LANES), lambda b, ph, j: (row_blk0 + b, jnp.where(ph == 0, j, ngb - 1))),
            pl.BlockSpec((gb, w, w), lambda b, ph, j: (jnp.where(ph == 0, j, ngb - 1), 0, 0)),
            pl.BlockSpec((gb, 2 * w, w), lambda b, ph, j: (jnp.where(ph == 0, 0, j), 0, 0)),
            pl.BlockSpec((2, g_all, LANES), lambda b, ph, j: (0, 0, 0)),
            pl.BlockSpec((n_seq, g_all, w), lambda b, ph, j: (b, 0, 0)),
            pl.BlockSpec((n_perm, n_perm), const, pipeline_mode=pl.Buffered(1)),
            pl.BlockSpec((n_perm, n_perm), const, pipeline_mode=pl.Buffered(1)),
        ],
        out_specs=(
            pl.BlockSpec((blk, LANES), lambda b, ph, j: (b, jnp.where(ph == 0, 0, j))),
            pl.BlockSpec((n_seq, g_all, w), lambda b, ph, j: (b, 0, 0)),
        ),
        scratch_shapes=[pltpu.VMEM((g_all, nck, w), BF16),
                        pltpu.VMEM((nck * gp, LANES), F32), pltpu.VMEM((nck * gp, LANES), F32)],
        compiler_params=_cp(("arbitrary", "arbitrary", "arbitrary")),
        name=name,
    )(u, w_state, w_out, a_step, s0, p_in, p_out)


def _glu_kernel(nct, yc_ref, yl_ref, wg_ref, bg_ref, wb_ref, o_ref):
    i = pl.program_id(0)
    y = jax.nn.gelu(jnp.where(i < nct, yc_ref[...], yl_ref[...]))
    z = _dot(y.astype(BF16), wg_ref[...]) + bg_ref[...]
    ya = y * jax.nn.sigmoid(z)
    o_ref[...] = _dot(ya.astype(BF16), wb_ref[...]).astype(o_ref.dtype)


def _glu_branch(y_ctx, y_lat, w_glu, b_glu, w_ba, tok):
    c = y_ctx.shape[1]
    d = w_ba.shape[1]
    tm = tok.tm
    return pl.pallas_call(
        functools.partial(_glu_kernel, tok.nct),
        out_shape=jax.ShapeDtypeStruct((y_ctx.shape[0] + y_lat.shape[0], d), BF16),
        grid=(tok.n,),
        in_specs=[pl.BlockSpec((tm, c), lambda i: (tok.ctx_idx(i), 0)),
                  pl.BlockSpec((tm, c), lambda i: (tok.lat_idx(i), 0)),
                  pl.BlockSpec((c, c), lambda i: (0, 0)),
                  pl.BlockSpec((1, c), lambda i: (0, 0)),
                  pl.BlockSpec((c, d), lambda i: (0, 0))],
        out_specs=pl.BlockSpec((tm, d), lambda i: (i, 0)),
        compiler_params=_cp(("parallel",)),
        name="glu_branch",
    )(y_ctx, y_lat, w_glu, b_glu, w_ba)


def _short_conv_kernel(nct, per_ctx, per_lat, v_ref, x0_ref, x1_ref, vp_ref, x0p_ref, x1p_ref,
                       vn_ref, x0n_ref, x1n_ref, w_ref, b_ref, vx_ref, x0c_ref, scr):
    i = pl.program_id(0)
    tm = v_ref.shape[0]
    k = jnp.where(i < nct, i % per_ctx, (i - nct) % per_lat)
    per = jnp.where(i < nct, per_ctx, per_lat)
    first = k == 0
    last = k == per - 1
    row = lax.broadcasted_iota(jnp.int32, (tm, 1), 0)

    def conv(cur_ref, prev_ref, next_ref, part):
        x = cur_ref[...].astype(F32)
        hp = jnp.where(first, 0.0, prev_ref[SUBLANES - 1:SUBLANES, :].astype(F32))
        hn = jnp.where(last, 0.0, next_ref[0:1, :].astype(F32))
        xm = jnp.where(row == 0, hp, pltpu.roll(x, 1, axis=0))
        xq = jnp.where(row == tm - 1, hn, pltpu.roll(x, tm - 1, axis=0))
        w = w_ref[part]
        return b_ref[part] + xm * w[0:1, :] + x * w[1:2, :] + xq * w[2:3, :]

    v = conv(v_ref, vp_ref, vn_ref, 0)
    x0 = conv(x0_ref, x0p_ref, x0n_ref, 1)
    x1 = conv(x1_ref, x1p_ref, x1n_ref, 2)
    vx_ref[...] = _to_pair_rows(v * x1, scr).astype(vx_ref.dtype)
    x0c_ref[...] = _to_pair_rows(x0, scr).astype(x0c_ref.dtype)


def _short_conv(proj, col_blk0, hw, w_short3, b_short3, n_ctx, seq, dseq, tm):
    t = proj.shape[0]
    assert seq % tm == 0 and dseq % tm == 0
    nt = t // tm
    r8 = tm // SUBLANES
    nb8 = t // SUBLANES

    def cur(k):
        return pl.BlockSpec((tm, hw), lambda i: (i, col_blk0 + k))

    def prev(k):
        return pl.BlockSpec((SUBLANES, hw), lambda i: (jnp.maximum(i * r8 - 1, 0), col_blk0 + k))

    def nxt(k):
        return pl.BlockSpec((SUBLANES, hw), lambda i: (jnp.minimum((i + 1) * r8, nb8 - 1), col_blk0 + k))

    kern = functools.partial(_short_conv_kernel, n_ctx // tm, seq // tm, dseq // tm)
    return pl.pallas_call(
        kern,
        out_shape=(jax.ShapeDtypeStruct((t // 2, 2 * hw), BF16), jax.ShapeDtypeStruct((t // 2, 2 * hw), BF16)),
        grid=(nt,),
        in_specs=[cur(0), cur(1), cur(2), prev(0), prev(1), prev(2), nxt(0), nxt(1), nxt(2),
                  pl.BlockSpec((3, 3, hw), lambda i: (0, 0, 0)),
                  pl.BlockSpec((3, 1, hw), lambda i: (0, 0, 0))],
        out_specs=(pl.BlockSpec((tm // 2, 2 * hw), lambda i: (i, 0)),
                   pl.BlockSpec((tm // 2, 2 * hw), lambda i: (i, 0))),
        scratch_shapes=[pltpu.VMEM((hw // LANES * tm, LANES), F32)],
        compiler_params=_cp(("parallel",)),
        name="short_conv",
    )(proj, proj, proj, proj, proj, proj, proj, proj, proj, w_short3, b_short3)


def _filt_gen_kernel(seq_len, n_bands, wt_ref, wc_ref, ws_ref, b1_ref, f1_ref, w2_ref, b2_ref, f2_ref,
                     w3_ref, b3_ref, rate_ref, a_ref, bm_ref, abs_ref, scr):
    i = pl.program_id(0)
    tl = 2 * a_ref.shape[0]
    hw = a_ref.shape[1] // 2
    pos = lax.broadcasted_iota(jnp.int32, (tl, 1), 0) + i * tl
    t = pos.astype(F32) / seq_len
    bands = (lax.broadcasted_iota(jnp.int32, (1, wc_ref.shape[0]), 1) + 1).astype(F32)
    ang = (2.0 * math.pi) * t * bands
    pre = t * wt_ref[...] + _dot3(jnp.cos(ang), wc_ref[...]) + _dot3(jnp.sin(ang), ws_ref[...])
    h = jnp.sin(f1_ref[...] * (pre + b1_ref[...]))
    h = jnp.sin(f2_ref[...] * (_dot3(h, w2_ref[...]) + b2_ref[...]))
    h = _dot3(h, w3_ref[...]) + b3_ref[...]
    h = h * jnp.exp(-t * rate_ref[...])
    h_f = h[:, :hw]
    h_b = h[:, hw:]
    colabs = jnp.sum(jnp.abs(h_f) + jnp.abs(h_b), axis=0, keepdims=True)
    h_bp = jnp.where(pos == 0, 0.0, h_b)
    a_ref[...] = _to_pair_rows(h_f + h_bp, scr).astype(a_ref.dtype)
    bm_ref[...] = _to_pair_rows(h_bp - h_f, scr).astype(bm_ref.dtype)

    @pl.when(i == 0)
    def _():
        abs_ref[...] = colabs

    @pl.when(i > 0)
    def _():
        abs_ref[...] += colabs


def _filt_gen(seq_len, w_f1, b_f1, freq1, w_f2, b_f2, freq2, w_f3, b_f3, rates2):
    n_emb, hid_raw = w_f1.shape
    n_bands = (n_emb - 1) // 2
    hw2 = w_f3.shape[1]
    hw = hw2 // 2
    tl = min(seq_len, 512)
    hid = LANES
    assert n_bands <= LANES and hid_raw <= LANES

    def pad(x, rows, cols):
        return jnp.zeros((rows, cols), F32).at[:x.shape[0], :x.shape[1]].set(x.astype(F32))

    wt = pad(w_f1[0:1], 1, hid)
    wc = pad(w_f1[1:1 + n_bands], LANES, hid)
    ws = pad(w_f1[1 + n_bands:], LANES, hid)
    b_f1, freq1, b_f2, freq2 = (pad(x, 1, hid) for x in (b_f1, freq1, b_f2, freq2))
    w_f2 = pad(w_f2, hid, hid)
    w_f3 = pad(w_f3, hid, hw2)
    full = lambda shape: pl.BlockSpec(shape, lambda i: (0,) * len(shape))
    kern = functools.partial(_filt_gen_kernel, seq_len, n_bands)
    return pl.pallas_call(
        kern,
        out_shape=(jax.ShapeDtypeStruct((seq_len // 2, hw2), BF16), jax.ShapeDtypeStruct((seq_len // 2, hw2), BF16),
                   jax.ShapeDtypeStruct((1, hw), F32)),
        grid=(seq_len // tl,),
        in_specs=[full((1, hid)), full((LANES, hid)), full((LANES, hid)), full((1, hid)), full((1, hid)),
                  full((hid, hid)), full((1, hid)), full((1, hid)), full((hid, hw2)), full((1, hw2)),
                  full((1, hw2))],
        out_specs=(pl.BlockSpec((tl // 2, hw2), lambda i: (i, 0)), pl.BlockSpec((tl // 2, hw2), lambda i: (i, 0)),
                   full((1, hw))),
        scratch_shapes=[pltpu.VMEM((hw // LANES * tl, LANES), F32)],
        compiler_params=_cp(("arbitrary",)),
        name="filt_gen",
    )(wt, wc, ws, b_f1, freq1, w_f2, b_f2, freq2, w_f3, b_f3, rates2)


def _trig_tables(row_hi, row_lo, col, period):
    col = col[None, :]
    ang_lo = ((row_lo[:, None] * col) % period).astype(F32) * (2.0 * math.pi / period)
    ang_hi = ((row_hi[:, None] * col) % period).astype(F32) * (2.0 * math.pi / period)
    c_lo, s_lo = jnp.cos(ang_lo)[None], jnp.sin(ang_lo)[None]
    c_hi, s_hi = jnp.cos(ang_hi)[:, None], jnp.sin(ang_hi)[:, None]
    n = row_hi.shape[0] * row_lo.shape[0]
    ctab = (c_hi * c_lo - s_hi * s_lo).reshape(n, col.shape[1])
    stab = (s_hi * c_lo + c_hi * s_lo).reshape(n, col.shape[1])
    return ctab.astype(BF16), stab.astype(BF16)


def _dft_tables(seq_len):
    m = seq_len // 2
    r = 1
    while r * r < m:
        r *= 2
    assert m % r == 0
    i32 = jnp.int32
    idx = jnp.arange(m, dtype=i32)
    hi = jnp.arange(m // r, dtype=i32) * r
    lo = jnp.arange(r, dtype=i32)
    ce, se = _trig_tables(hi, lo, idx, 2 * m)
    co, so = _trig_tables(hi, lo, 2 * idx + 1, 2 * seq_len)
    cot, sot = _trig_tables(2 * hi, 2 * lo + 1, idx, 2 * seq_len)
    return ce, se, co, so, cot, sot


def _alt_row(n):
    return (1 - 2 * (lax.broadcasted_iota(jnp.int32, (SUBLANES, n), 1) % 2)).astype(BF16)


def _filt_dft_kernel(seq_len, ce_ref, se_ref, co_ref, so_ref, ae_ref, ao_ref, be_ref, bo_ref, abs_ref,
                     krl_ref, krh_ref, kil_ref, kih_ref, kmid_ref):
    i = pl.program_id(0)
    scale = (1.0 / seq_len) / (abs_ref[...] + EPS)
    ec = _dot(ce_ref[...], ae_ref[...])
    oc = _dot(co_ref[...], ao_ref[...])
    es = _dot(se_ref[...], be_ref[...])
    os_ = _dot(so_ref[...], bo_ref[...])
    krl_ref[...] = (ec + oc) * scale
    krh_ref[...] = (ec - oc) * scale
    kil_ref[...] = (es + os_) * scale
    kih_ref[...] = (os_ - es) * scale

    @pl.when(i == 0)
    def _():
        sgn = _alt_row(ae_ref.shape[0])
        kmid_ref[0:1, :] = _dot(sgn, ae_ref[...])[0:1] * scale
        kmid_ref[1:2, :] = _dot(sgn, bo_ref[...])[0:1] * scale


def _filt_dft(tabs, a2, bm2, colabs, seq_len):
    m = seq_len // 2
    hw = a2.shape[1] // 2
    ce, se, co, so = tabs[:4]
    tf = min(m, 256)
    kern = functools.partial(_filt_dft_kernel, seq_len)
    mode = pl.Buffered(1) if m * hw * 2 > BIG_BLOCK_BYTES // 2 else None
    tab = pl.BlockSpec((tf, m), lambda i: (i, 0))
    res = lambda c: pl.BlockSpec((m, hw), lambda i: (0, c), pipeline_mode=mode)
    out = pl.BlockSpec((tf, hw), lambda i: (i, 0))
    sds = jax.ShapeDtypeStruct((m, hw), F32)
    return pl.pallas_call(
        kern,
        out_shape=(sds, sds, sds, sds, jax.ShapeDtypeStruct((2, hw), F32)),
        grid=(m // tf,),
        in_specs=[tab, tab, tab, tab, res(0), res(1), res(0), res(1), pl.BlockSpec((1, hw), lambda i: (0, 0))],
        out_specs=(out, out, out, out, pl.BlockSpec((2, hw), lambda i: (0, 0))),
        compiler_params=_cp(("arbitrary",)),
        name="filt_dft",
    )(ce, se, co, so, a2, a2, bm2, bm2, colabs)


def _hy_fwd_kernel(ce_ref, se_ref, co_ref, so_ref, ve_ref, vo_ref, krl_ref, krh_ref, kil_ref, kih_ref, kmid_ref,
                   ae_ref, be_ref, ao_ref, bo_ref, ymid_ref):
    j = pl.program_id(1)
    tf, m = ce_ref.shape
    nb = ae_ref.shape[0]
    krl, krh, kil, kih = krl_ref[...], krh_ref[...], kil_ref[...], kih_ref[...]
    row = lax.broadcasted_iota(jnp.int32, (tf, 1), 0) + j * tf
    dc = jnp.where(row == 0, 0.5, 1.0)
    for s in range(nb):
        ve = ve_ref[s * m:(s + 1) * m, :]
        vo = vo_ref[s * m:(s + 1) * m, :]
        ec = _dot(ce_ref[...], ve)
        oc = _dot(co_ref[...], vo)
        es = _dot(se_ref[...], ve)
        os_ = _dot(so_ref[...], vo)
        p_lo, p_hi = ec + oc, ec - oc
        q_lo, q_hi = es + os_, os_ - es
        yre_lo = (p_lo * krl + q_lo * kil) * dc
        yim_lo = p_lo * kil - q_lo * krl
        yre_hi = (p_hi * krh + q_hi * kih) * dc
        yim_hi = p_hi * kih - q_hi * krh
        ae_ref[s] = (yre_lo + yre_hi).astype(ae_ref.dtype)
        be_ref[s] = (yim_lo - yim_hi).astype(be_ref.dtype)
        ao_ref[s] = (yre_lo - yre_hi).astype(ao_ref.dtype)
        bo_ref[s] = (yim_lo + yim_hi).astype(bo_ref.dtype)

    @pl.when(j == 0)
    def _():
        sgn = _alt_row(m)
        kr_m, ki_m = kmid_ref[0:1, :], kmid_ref[1:2, :]
        for s in range(nb):
            p_m = _dot(sgn, ve_ref[s * m:(s + 1) * m, :])[0:1]
            q_m = _dot(sgn, vo_ref[s * m:(s + 1) * m, :])[0:1]
            ymid_ref[s, 0:1, :] = p_m * kr_m + q_m * ki_m
            ymid_ref[s, 1:2, :] = p_m * ki_m - q_m * kr_m


BIG_BLOCK_BYTES = 4 * 1024 * 1024


def _seq_batch(seq_len, n_seq):
    nb = max(1, min(n_seq, 4 * MXU_SIDE // seq_len))
    while n_seq % nb:
        nb -= 1
    return nb


def _hy_fwd(tabs, vx2, kspec, n_seq, seq_len, row0_blk):
    m = seq_len // 2
    hw = vx2.shape[1] // 2
    ce, se, co, so = tabs[:4]
    krl, krh, kil, kih, kmid = kspec
    tf = min(m, 256)
    nb = _seq_batch(seq_len, n_seq)
    assert row0_blk % nb == 0
    mode = pl.Buffered(1) if nb * m * hw * 2 > BIG_BLOCK_BYTES // 2 else None
    tab = pl.BlockSpec((tf, m), lambda b, j: (j, 0))
    vspec = lambda c: pl.BlockSpec((nb * m, hw), lambda b, j: (row0_blk // nb + b, c), pipeline_mode=mode)
    kt = pl.BlockSpec((tf, hw), lambda b, j: (j, 0))
    out = pl.BlockSpec((nb, tf, hw), lambda b, j: (b, j, 0))
    sds = jax.ShapeDtypeStruct((n_seq, m, hw), BF16)
    return pl.pallas_call(
        _hy_fwd_kernel,
        out_shape=(sds, sds, sds, sds, jax.ShapeDtypeStruct((n_seq, 2, hw), F32)),
        grid=(n_seq // nb, m // tf),
        in_specs=[tab, tab, tab, tab, vspec(0), vspec(1), kt, kt, kt, kt,
                  pl.BlockSpec((2, hw), lambda b, j: (0, 0))],
        out_specs=(out, out, out, out, pl.BlockSpec((nb, 2, hw), lambda b, j: (b, 0, 0))),
        compiler_params=_cp(("arbitrary", "arbitrary")),
        name="hy_fwd",
    )(ce, se, co, so, vx2, vx2, krl, krh, kil, kih, kmid)


def _hy_inv_kernel(ce_ref, se_ref, cot_ref, sot_ref, ae_ref, be_ref, ao_ref, bo_ref, ymid_ref,
                   vx_ref, x0_ref, fb_ref, wb_ref, o_ref, scr):
    j = pl.program_id(1)
    tt = ce_ref.shape[0]
    nb = ae_ref.shape[0]
    hw = fb_ref.shape[1]
    row = lax.broadcasted_iota(jnp.int32, (tt, 1), 0) + j * tt
    sgn = (1 - 2 * (row % 2)).astype(F32)
    for s in range(nb):
        rows = slice(s * tt, (s + 1) * tt)
        ymid = ymid_ref[s]
        z_e = _dot(ce_ref[...], ae_ref[s]) - _dot(se_ref[...], be_ref[s]) + sgn * ymid[0:1]
        z_o = _dot(cot_ref[...], ao_ref[s]) - _dot(sot_ref[...], bo_ref[s]) - sgn * ymid[1:2]
        proj = []
        for par, z in ((0, z_e), (1, z_o)):
            cols = slice(par * hw, (par + 1) * hw)
            z = z + vx_ref[rows, cols].astype(F32) * fb_ref[...]
            yb = x0_ref[rows, cols].astype(F32) * z
            proj.append(_dot(yb.astype(BF16), wb_ref[...]))
        o_ref[2 * s * tt:2 * (s + 1) * tt, :] = _from_pair_rows(
            jnp.concatenate(proj, axis=1), scr).astype(o_ref.dtype)


def _hy_inv(tabs, spec, vx2, x0c2, fbias, w_bb, n_seq, seq_len, row0_blk):
    m = seq_len // 2
    hw = fbias.shape[1]
    d = w_bb.shape[1]
    ce, se, _, _, cot, sot = tabs
    ae, be, ao, bo, ymid = spec
    tt = min(m, 128)
    per = m // tt
    nb = _seq_batch(seq_len, n_seq) if per == 1 else 1
    assert row0_blk % nb == 0
    mode = pl.Buffered(1) if nb * m * hw * 2 > BIG_BLOCK_BYTES // 2 else None
    tab = pl.BlockSpec((tt, m), lambda b, j: (j, 0))
    res = pl.BlockSpec((nb, m, hw), lambda b, j: (b, 0, 0), pipeline_mode=mode)
    rows_in = pl.BlockSpec((nb * tt, 2 * hw), lambda b, j: ((row0_blk // nb + b) * per + j, 0))
    return pl.pallas_call(
        _hy_inv_kernel,
        out_shape=jax.ShapeDtypeStruct((n_seq * seq_len, d), BF16),
        grid=(n_seq // nb, per),
        in_specs=[tab, tab, tab, tab, res, res, res, res,
                  pl.BlockSpec((nb, 2, hw), lambda b, j: (b, 0, 0)),
                  rows_in, rows_in,
                  pl.BlockSpec((1, hw), lambda b, j: (0, 0)),
                  pl.BlockSpec((hw, d), lambda b, j: (0, 0), pipeline_mode=pl.Buffered(1))],
        out_specs=pl.BlockSpec((nb * 2 * tt, d), lambda b, j: (b * per + j, 0)),
        scratch_shapes=[pltpu.VMEM((d // LANES * 2 * tt, LANES), F32)],
        compiler_params=_cp(("arbitrary", "arbitrary")),
        name="hy_inv",
    )(ce, se, cot, sot, ae, be, ao, bo, ymid, vx2, x0c2, fbias, w_bb)


MERGE_COL_BLOCKS = 4

def _expert_pairs(epg):
    pairs = [(a, b) for a in range(epg) for b in range(a + 1, epg)]
    order = [pairs.pop(0)]
    while pairs:
        last = order[-1]
        nxt = next((p for p in pairs if set(p) & set(last)), pairs[0])
        pairs.remove(nxt)
        order.append(nxt)
    return order


def _merge_kernel(tok, n_groups, epg, xp_ref, xs_ref, pos_ref, g0_ref, g1_ref, pa_ref, pbc_ref, pbs_ref,
                  gate1_ref, sh2_ref, sc2_ref, gn_ref, wo_ref, wr_ref, br_ref,
                  x2_ref, h2_ref, rt_ref):
    i = pl.program_id(0)
    is_ctx = i < tok.nct
    tm, d = x2_ref.shape
    pb = jnp.where(is_ctx, pbc_ref[...], pbs_ref[...])
    merged = g0_ref[...] * pa_ref[...] + g1_ref[...] * pb
    cb = d // MERGE_COL_BLOCKS
    ssq = jnp.zeros((tm, 1), F32)
    for c in range(MERGE_COL_BLOCKS):
        cols = slice(c * cb, (c + 1) * cb)
        att = _dot(merged, wo_ref[:, cols])
        x = jnp.where(is_ctx, xp_ref[:, cols], xs_ref[:, cols] + pos_ref[:, cols])
        x2c = x + gate1_ref[0][:, cols] * att
        x2_ref[:, cols] = x2c
        ssq = ssq + jnp.sum(x2c * x2c, axis=-1, keepdims=True)
    x2 = x2_ref[...]
    h2 = x2 * lax.rsqrt(ssq * (1.0 / d) + EPS) * gn_ref[...] * (1.0 + sc2_ref[0]) + sh2_ref[0]
    h2b = h2.astype(BF16)
    _store_row_tiles(h2_ref, 0, h2b)
    logits = _dot(h2b, wr_ref[...]) + br_ref[...]
    lane = lax.broadcasted_iota(jnp.int32, logits.shape, 1)
    gl = jnp.where(lane < n_groups, logits, -jnp.inf)
    gmax = jnp.max(gl, axis=-1, keepdims=True)
    gidx = jnp.min(jnp.where(gl == gmax, lane, n_groups), axis=-1, keepdims=True)
    base = n_groups + gidx * epg
    el = jnp.where(jnp.logical_and(lane >= base, lane < base + epg), logits, -jnp.inf)
    i1 = jnp.min(jnp.where(el == jnp.max(el, axis=-1, keepdims=True), lane, LANES), axis=-1, keepdims=True)
    el = jnp.where(lane == i1, -jnp.inf, el)
    i2 = jnp.min(jnp.where(el == jnp.max(el, axis=-1, keepdims=True), lane, LANES), axis=-1, keepdims=True)
    lo = jnp.minimum(i1, i2) - base
    hi = jnp.maximum(i1, i2) - base
    cls = jnp.zeros_like(gidx)
    for code, (pa_, pb_) in enumerate(_expert_pairs(epg)):
        cls = cls + code * jnp.logical_and(lo == pa_, hi == pb_).astype(jnp.int32)
    rt_ref[...] = jnp.where(lane == LANES - 1, gidx.astype(F32),
                            jnp.where(lane == LANES - 2, cls.astype(F32), logits))


def _merge_out(xp, xs, pos, proj, gate_blk0, pa, pb_ctx, pb_lat, mod, g2, w_out, w_r, b_r, n_groups, epg, tok):
    d = xp.shape[1]
    tm = tok.tm
    t_all = xp.shape[0] + xs.shape[0]
    kern = functools.partial(_merge_kernel, tok, n_groups, epg)
    one = pl.Buffered(1)
    ns = d // 2 // LANES

    def modspec(k):
        return pl.BlockSpec((1, 1, d), lambda i: (tok.cond(i) * N_MOD + k, 0, 0))

    return pl.pallas_call(
        kern,
        out_shape=(jax.ShapeDtypeStruct((t_all, d), F32), jax.ShapeDtypeStruct((t_all * ns, LANES), jnp.uint32),
                   jax.ShapeDtypeStruct((t_all, LANES), F32)),
        grid=(tok.n,),
        in_specs=[
            pl.BlockSpec((tm, d), lambda i: (tok.ctx_idx(i), 0)),
            pl.BlockSpec((tm, d), lambda i: (tok.lat_idx(i), 0)),
            pl.BlockSpec((tm, d), lambda i: (tok.pos_idx(i), 0)),
            pl.BlockSpec((tm, d), lambda i: (i, gate_blk0)),
            pl.BlockSpec((tm, d), lambda i: (i, gate_blk0 + 1)),
            pl.BlockSpec((tm, d), lambda i: (i, 0)),
            pl.BlockSpec((tm, d), lambda i: (tok.ctx_idx(i), 0)),
            pl.BlockSpec((tm, d), lambda i: (tok.lat_idx(i), 0)),
            modspec(2), modspec(3), modspec(4),
            pl.BlockSpec((1, d), lambda i: (0, 0)),
            pl.BlockSpec((d, d), lambda i: (0, 0), pipeline_mode=one),
            pl.BlockSpec((d, LANES), lambda i: (0, 0)),
            pl.BlockSpec((1, LANES), lambda i: (0, 0)),
        ],
        out_specs=(pl.BlockSpec((tm, d), lambda i: (i, 0)), pl.BlockSpec((tm * ns, LANES), lambda i: (i, 0)),
                   pl.BlockSpec((tm, LANES), lambda i: (i, 0))),
        compiler_params=_cp(("parallel",)),
        name="merge_out",
    )(xp, xs, pos, proj, proj, pa, pb_ctx, pb_lat, mod, mod, mod, g2, w_out, w_r, b_r)


ROW_UNROLL = 8

def _moe_kernel(n_tiles, n_groups, epg, tg_ref, tnv_ref, rt_ref, cb_ref, kx_ref, wi_ref,
                h2_hbm, wr_ref, br_ref, wg_ref, wu_ref, wd_ref, out_hbm,
                xbuf, xb_ref, acc, obuf, gws, gsem, ssem):
    i = pl.program_id(0)
    j = pl.program_id(1)
    tm = xb_ref.shape[0]
    ns = h2_hbm.shape[1]
    nv = tnv_ref[i]
    slot = i % 2

    def buf_rows(buf, slot_, r, rows):
        return buf.at[pl.ds(pl.multiple_of((slot_ * tm + r) * ns, ns), rows * ns), :]

    def gather_copy(tok_row, slot_, r):
        return pltpu.make_async_copy(h2_hbm.at[tok_row], buf_rows(xbuf, slot_, r, 1), gsem.at[slot_])

    def gather_wait(slot_, rows):
        dst = buf_rows(xbuf, slot_, 0, rows)
        pltpu.make_async_copy(dst, dst, gsem.at[slot_]).wait()

    def scatter_copy(tok_row, slot_, r):
        return pltpu.make_async_copy(buf_rows(obuf, slot_, r, 1), out_hbm.at[tok_row], ssem.at[0])

    def scatter_wait(slot_, rows):
        src = buf_rows(obuf, slot_, 0, rows)
        pltpu.make_async_copy(src, src, ssem.at[0]).wait()

    def for_rows(count, fn):
        n_full = count // ROW_UNROLL

        def body(b, c):
            for k in range(ROW_UNROLL):
                fn(b * ROW_UNROLL + k)
            return c

        def tail(r, c):
            fn(r)
            return c

        lax.fori_loop(0, n_full, body, 0)
        lax.fori_loop(n_full * ROW_UNROLL, count, tail, 0)

    def wait_scatter(count, slot_):
        p = tm
        while p >= 1:
            @pl.when((count & p) != 0)
            def _(p=p):
                scatter_wait(slot_, p)
            p //= 2

    nv_prev = tnv_ref[jnp.maximum(i - 1, 0)]
    nv_next = tnv_ref[jnp.minimum(i + 1, n_tiles - 1)]
    prev_deferred = jnp.logical_and(i > 0, jnp.logical_and(nv_prev == tm, nv > 0))
    deferred = jnp.logical_and(nv == tm, nv_next > 0)

    @pl.when(j == 0)
    def _():
        @pl.when(i == 0)
        def _():
            for_rows(tm, lambda r: gather_copy(rt_ref[r], 0, r).start())

        @pl.when(jnp.logical_or(i == 0, nv_prev > 0))
        def _():
            gather_wait(slot, tm)

    @pl.when(jnp.logical_and(j == 0, nv > 0))
    def _():
        grp = tg_ref[i]
        xb = _load_row_tiles(xbuf, slot * tm * ns, tm, ns)
        xb_ref[...] = xb
        logits = _dot(xb, wr_ref[...]) + br_ref[...]
        lane = lax.broadcasted_iota(jnp.int32, logits.shape, 1)
        gl = jnp.where(lane < n_groups, logits, -jnp.inf)
        gmax = jnp.max(gl, axis=-1, keepdims=True)
        ge = jnp.exp(gl - gmax)
        g_w = jnp.sum(jnp.where(lane == grp, ge, 0.0), axis=-1, keepdims=True) / jnp.sum(ge, axis=-1, keepdims=True)
        base = n_groups + grp * epg
        e = [jnp.sum(jnp.where(lane == base + k, logits, 0.0), axis=-1, keepdims=True) for k in range(epg)]
        pairs = _expert_pairs(epg)
        row = lax.broadcasted_iota(jnp.int32, (tm, 1), 0)
        cls = jnp.zeros((tm, 1), jnp.int32)
        for c in range(len(pairs) - 1):
            cls = cls + (row >= cb_ref[i * len(pairs) + c]).astype(jnp.int32)
        sel = [functools.reduce(jnp.logical_or, [cls == c for c, p in enumerate(pairs) if k in p])
               for k in range(epg)]
        emax = functools.reduce(jnp.maximum, [jnp.where(sel[k], e[k], -jnp.inf) for k in range(epg)])
        ex = [jnp.where(sel[k], jnp.exp(e[k] - emax), 0.0) for k in range(epg)]
        den = functools.reduce(lambda a, b: a + b, ex)
        for k in range(epg):
            gws[k] = ex[k] / den * g_w
        acc[...] = jnp.zeros(acc.shape, acc.dtype)

    kx = kx_ref[i * epg + j]

    def expert_step(issue, scatter_prev):
        per = tm // TOP_K_INNER
        if issue:
            for k in range(per):
                r = kx * per + k
                if scatter_prev:
                    scatter_copy(rt_ref[(i - 1) * tm + r], 1 - slot, r).start()
                gather_copy(rt_ref[(i + 1) * tm + r], 1 - slot, r).start()
        xb = xb_ref[...]
        g = _dot(xb, wg_ref[...].astype(BF16))
        u = _dot(xb, wu_ref[...].astype(BF16))
        hid = (g * jax.nn.sigmoid(g)) * u * gws[j]
        acc[...] += _dot(hid.astype(BF16), wd_ref[...].astype(BF16))

    run = jnp.logical_and(nv > 0, kx >= 0)
    carries = kx < TOP_K_INNER

    @pl.when(jnp.logical_and(run, jnp.logical_and(carries, prev_deferred)))
    def _():
        expert_step(True, True)

    @pl.when(jnp.logical_and(run, jnp.logical_and(carries, jnp.logical_not(prev_deferred))))
    def _():
        expert_step(True, False)

    @pl.when(jnp.logical_and(run, jnp.logical_not(carries)))
    def _():
        expert_step(False, False)

    @pl.when(j == epg - 1)
    def _():
        @pl.when(prev_deferred)
        def _():
            scatter_wait(1 - slot, tm)

        @pl.when(nv > 0)
        def _():
            _store_row_tiles(obuf, slot * tm * ns, acc[...].astype(BF16))

        @pl.when(jnp.logical_and(nv > 0, jnp.logical_not(deferred)))
        def _():
            for_rows(nv, lambda r: scatter_copy(rt_ref[i * tm + r], slot, r).start())
            wait_scatter(nv, slot)


def _moe(h2p, w_r, b_r, w_g, w_u, w_d, plan, n_groups, epg, tm):
    t, ns, _ = h2p.shape
    d = 2 * ns * LANES
    ff = w_g.shape[-1]
    n_tiles = plan[0].shape[0]
    assert tm & (tm - 1) == 0, "row-count waits decompose tm in binary"
    assert tm % TOP_K_INNER == 0
    kern = functools.partial(_moe_kernel, n_tiles, n_groups, epg)

    def const(i, j, *_):
        return (0, 0)

    def wblk(i, j, tg, tnv, rt, cb, kx, wi):
        return (wi[i * epg + j], 0, 0)

    grid_spec = pltpu.PrefetchScalarGridSpec(
        num_scalar_prefetch=len(plan),
        grid=(n_tiles, epg),
        in_specs=[
            pl.BlockSpec(memory_space=pl.ANY),
            pl.BlockSpec((d, LANES), const),
            pl.BlockSpec((1, LANES), const),
            pl.BlockSpec((None, d, ff), wblk),
            pl.BlockSpec((None, d, ff), wblk),
            pl.BlockSpec((None, ff, d), wblk),
        ],
        out_specs=pl.BlockSpec(memory_space=pl.ANY),
        scratch_shapes=[
            pltpu.VMEM((2 * tm * ns, LANES), jnp.uint32),
            pltpu.VMEM((tm, d), BF16),
            pltpu.VMEM((tm, d), F32),
            pltpu.VMEM((2 * tm * ns, LANES), jnp.uint32),
            pltpu.VMEM((epg, tm, 1), F32),
            pltpu.SemaphoreType.DMA((2,)),
            pltpu.SemaphoreType.DMA((1,)),
        ],
    )
    return pl.pallas_call(
        kern,
        out_shape=jax.ShapeDtypeStruct((t, ns, LANES), jnp.uint32),
        grid_spec=grid_spec,
        compiler_params=_cp(("arbitrary", "arbitrary")),
        name="moe",
    )(*plan, h2p, w_r, b_r, w_g, w_u, w_d)


def _moe_plan(grp, cls, n_groups, epg, tm, n_tiles):
    i32 = jnp.int32
    t = grp.shape[0]
    pairs = _expert_pairs(epg)
    n_cls = len(pairs)
    key = grp * n_cls + cls
    onehot = (key[:, None] == jnp.arange(n_groups * n_cls, dtype=i32)[None, :]).astype(i32)
    csum = jnp.cumsum(onehot, axis=0)
    kcount = csum[-1].reshape(n_groups, n_cls)
    counts = jnp.sum(kcount, axis=1)
    cls_end = jnp.cumsum(kcount, axis=1)
    cls_off = cls_end - kcount
    ntile_g = (counts + tm - 1) // tm
    tile_end = jnp.cumsum(ntile_g)
    tile_off = tile_end - ntile_g
    key_base = (tile_off[:, None] * tm + cls_off).reshape(-1)
    slot = jnp.sum(onehot * (key_base[None, :] + csum - 1), axis=1)
    row_tok = jnp.zeros((n_tiles * tm,), i32).at[slot].set(jnp.arange(t, dtype=i32))
    tiles = jnp.arange(n_tiles, dtype=i32)
    tile_grp = jnp.minimum(jnp.sum((tiles[:, None] >= tile_end[None, :]).astype(i32), axis=1), n_groups - 1)
    row0 = (tiles - tile_off[tile_grp]) * tm
    tile_nv = jnp.clip(counts[tile_grp] - row0, 0, tm)
    tile_nv = jnp.where(tiles < tile_end[-1], tile_nv, 0)
    cb = jnp.clip(cls_end[tile_grp] - row0[:, None], 0, tm)
    rows_in = jnp.diff(jnp.minimum(cb, tile_nv[:, None]), axis=1, prepend=0)
    member = jnp.asarray(np.array([[int(k in p) for k in range(epg)] for p in pairs], np.int32))
    needed = ((rows_in > 0).astype(i32) @ member) > 0
    kx = jnp.where(needed, jnp.cumsum(needed.astype(i32), axis=1) - 1, -1)
    blk = (tile_grp[:, None] * epg + jnp.arange(epg, dtype=i32)[None, :]).reshape(-1)
    n_pos = n_tiles * epg
    pos = jnp.arange(n_pos, dtype=i32)
    flat = needed.reshape(-1)
    nxt = lax.cummin(jnp.where(flat, pos, n_pos), axis=0, reverse=True)
    last = lax.cummax(jnp.where(flat, pos, -1), axis=0)
    hold = jnp.where(nxt < n_pos, nxt, jnp.maximum(last, 0))
    wi = blk[hold]
    return (tile_grp.astype(i32), tile_nv.astype(i32), row_tok, cb.reshape(-1).astype(i32),
            kx.reshape(-1).astype(i32), wi.astype(i32))


def _final_kernel(x2_ref, m_ref, gate2_ref, gf_ref, o_ref):
    tm, d = x2_ref.shape
    ns = d // 2 // LANES
    m = _load_row_tiles(m_ref, 0, tm, ns).astype(F32)
    x3 = x2_ref[...] + gate2_ref[0] * m
    o_ref[...] = _rms(x3, gf_ref[...])


def _final(x2, moe, mod, g_final, row_blk0, n_rows, cond_of_tile, tm):
    d = x2.shape[1]
    ns = d // 2 // LANES
    return pl.pallas_call(
        _final_kernel,
        out_shape=jax.ShapeDtypeStruct((n_rows, d), F32),
        grid=(n_rows // tm,),
        in_specs=[pl.BlockSpec((tm, d), lambda i: (row_blk0 + i, 0)),
                  pl.BlockSpec((tm * ns, LANES), lambda i: (row_blk0 + i, 0)),
                  pl.BlockSpec((1, 1, d), lambda i: (cond_of_tile(i) * N_MOD + 5, 0, 0)),
                  pl.BlockSpec((1, d), lambda i: (0, 0))],
        out_specs=pl.BlockSpec((tm, d), lambda i: (i, 0)),
        compiler_params=_cp(("parallel",)),
        name="final",
    )(x2, moe, mod, g_final)


def _grid_pos_embed(n_tokens, dim):
    rows_n = n_tokens // GRID_W
    quarter = dim // 4
    omega = 1.0 / (10000.0 ** (jnp.arange(quarter, dtype=F32) / quarter))

    def emb(n):
        ang = jnp.arange(n).astype(F32)[:, None] * omega[None, :]
        return jnp.concatenate([jnp.sin(ang), jnp.cos(ang)], axis=-1)

    e_r = jnp.broadcast_to(emb(rows_n)[:, None, :], (rows_n, GRID_W, dim // 2))
    e_c = jnp.broadcast_to(emb(GRID_W)[None, :, :], (rows_n, GRID_W, dim // 2))
    return jnp.concatenate([e_r, e_c], axis=-1).reshape(n_tokens, dim)


def _state_to_lanes(s):
    b, _, _, g, p = s.shape
    return jnp.transpose(s, (0, 3, 2, 1, 4)).reshape(b, g, 4 * p)


def _lanes_to_state(x, p):
    b, g, _ = x.shape
    return jnp.transpose(x.reshape(b, g, 2, 2, p), (0, 3, 2, 1, 4))


def kernel(x_prompt, x_sample, state_ssm, c, c_ctx, w_ada, b_ada, g_norm1, g_norm2, w_in, ssm_a_re, ssm_a_im, ssm_log_dt, ssm_b_re, ssm_b_im, ssm_c_re, ssm_c_im, ssm_d, w_glu, b_glu, w_short, b_short, w_f1, b_f1, freq1, w_f2, b_f2, freq2, w_f3, b_f3, filter_bias, w_branch_a, w_branch_b, w_out, w_router_group, b_router_group, w_router_expert, b_router_expert, w_exp_gate, w_exp_up, w_exp_down, g_final):
    depth = w_ada.shape[0]
    assert depth == 1, "single-layer pipeline"
    bsz, seq, d = x_prompt.shape
    dbsz, dseq, _ = x_sample.shape
    g_all, p_state = ssm_a_re.shape[2:]
    ssm_w = ssm_d.shape[1]
    hw = filter_bias.shape[1]
    n_groups, _, epg = w_router_expert.shape[1:]
    ff = w_exp_gate.shape[-1]
    assert ssm_w // g_all * CHUNK == MXU_SIDE and 4 * p_state == MXU_SIDE
    assert ssm_w == hw and d == 2 * hw
    n_ctx = bsz * seq
    n_lat = dbsz * dseq
    t_all = n_ctx + n_lat

    xp = x_prompt.reshape(n_ctx, d)
    xs = x_sample.reshape(n_lat, d)
    pos = _grid_pos_embed(dseq, d)

    n_cond = 1 + dbsz
    cond8 = jnp.zeros((SUBLANES, d), F32).at[0].set(c_ctx).at[1:n_cond].set(c)
    mod = _ada_mod(cond8, w_ada[0], b_ada[0][None])
    mod = mod[:n_cond].reshape(n_cond * N_MOD, 1, d)

    tm_e = min(256, seq, dseq)
    tok_e = _Tok(n_ctx, n_lat, dseq, tm_e)
    tm_l = min(512, n_ctx, dseq)
    tok_l = _Tok(n_ctx, n_lat, dseq, tm_l)
    gate_col0 = ssm_w + 3 * hw
    h1 = _norm_mod(xp, xs, pos, mod, g_norm1, tok_l)
    tm_p, tn_p = min(1024, t_all), min(1024, hw)
    n_in = w_in.shape[2]
    u_a = _in_proj(h1, w_in[0], (0, 0), (0, ssm_w), F32, tm_p, tn_p, "in_proj_a")
    proj = _in_proj(h1, w_in[0], (gate_col0, n_in), (ssm_w, gate_col0), BF16, tm_p, tn_p, "in_proj")
    n_gate_cols = n_in - gate_col0

    w_state, w_so, a_step = _ssm_operators(ssm_a_re[0], ssm_a_im[0], ssm_log_dt[0], ssm_b_re[0], ssm_b_im[0],
                                           ssm_c_re[0], ssm_c_im[0], ssm_d[0])
    nc_ctx = seq // CHUNK
    nc_lat = dseq // CHUNK
    seq_per_blk = max(1, min(bsz, MXU_SIDE // nc_ctx))
    assert bsz % seq_per_blk == 0 and n_ctx % dseq == 0
    s0_ctx = jnp.zeros((bsz, g_all, 4 * p_state), F32)
    y_ctx, fin_ctx = _ssm_call(u_a, 0, n_ctx, w_state, w_so, a_step, s0_ctx, seq_per_blk, nc_ctx, "ssm_ctx")
    s0_lat = _state_to_lanes(state_ssm[:, 0].astype(F32))
    y_lat, _ = _ssm_call(u_a, n_ctx // dseq, n_lat, w_state, w_so, a_step, s0_lat, 1, nc_lat, "ssm_lat")
    new_state = _lanes_to_state(fin_ctx, p_state)[:, None]

    pa = _glu_branch(y_ctx, y_lat, w_glu[0].astype(BF16), b_glu[0][None], w_branch_a[0].astype(BF16), tok_l)

    w_short3 = jnp.transpose(w_short[0].reshape(-1, 3, hw), (1, 0, 2))
    b_short3 = b_short[0].reshape(3, 1, hw)
    vx2, x0c2 = _short_conv(proj, n_gate_cols // hw, hw, w_short3, b_short3, n_ctx, seq, dseq, tm_e)
    rates = jnp.abs(jnp.linspace(math.log(DECAY_TARGET) / DECAY_FAST, math.log(DECAY_TARGET) / DECAY_SLOW,
                                 hw, dtype=F32))
    rates2 = jnp.concatenate([rates, rates])[None]
    fbias = filter_bias[0][None].astype(F32)
    w_bb = w_branch_b[0].astype(BF16)
    pbs = []
    for seq_len, n_seq, row0 in ((seq, bsz, 0), (dseq, dbsz, n_ctx // dseq)):
        tabs = _dft_tables(seq_len)
        a2, bm2, colabs = _filt_gen(seq_len, w_f1[0], b_f1[0][None], freq1[0][None], w_f2[0],
                                    b_f2[0][None], freq2[0][None], w_f3[0], b_f3[0][None], rates2)
        kspec = _filt_dft(tabs, a2, bm2, colabs, seq_len)
        spec = _hy_fwd(tabs, vx2, kspec, n_seq, seq_len, row0)
        pbs.append(_hy_inv(tabs, spec, vx2, x0c2, fbias, w_bb, n_seq, seq_len, row0))
    pb_ctx, pb_lat = pbs

    w_r = jnp.zeros((d, LANES), F32)
    w_r = w_r.at[:, :n_groups].set(w_router_group[0])
    w_r = w_r.at[:, n_groups:n_groups + n_groups * epg].set(
        jnp.transpose(w_router_expert[0], (1, 0, 2)).reshape(d, n_groups * epg))
    b_r = jnp.zeros((1, LANES), F32)
    b_r = b_r.at[0, :n_groups].set(b_router_group[0])
    b_r = b_r.at[0, n_groups:n_groups + n_groups * epg].set(b_router_expert[0].reshape(-1))
    w_r = w_r.astype(BF16)
    x2, h2, rout = _merge_out(xp, xs, pos, proj, 0, pa, pb_ctx, pb_lat, mod, g_norm2,
                              w_out[0].astype(BF16), w_r, b_r, n_groups, epg, tok_e)

    tm_m = min(512, t_all // n_groups)
    n_tiles = t_all // tm_m + n_groups
    grp = rout[:, LANES - 1].astype(jnp.int32)
    cls = rout[:, LANES - 2].astype(jnp.int32)
    plan = _moe_plan(grp, cls, n_groups, epg, tm_m, n_tiles)
    ns = d // 2 // LANES
    moe = _moe(h2.reshape(t_all, ns, LANES), w_r, b_r, w_exp_gate[0], w_exp_up[0], w_exp_down[0],
               plan, n_groups, epg, tm_m)
    moe = moe.reshape(t_all * ns, LANES)

    gf = g_final[None]
    y_prompt = _final(x2, moe, mod, gf, 0, n_ctx, lambda i: 0, tm_l)
    lat_tiles = dseq // tm_l
    y_sample = _final(x2, moe, mod, gf, n_ctx // tm_l, n_lat, lambda i: 1 + i // lat_tiles, tm_l)
    return (y_prompt.reshape(bsz, seq, d), y_sample.reshape(dbsz, dseq, d), new_state)
```

```python
import functools
import math

import numpy as np
import jax
import jax.numpy as jnp
from jax import lax
from jax.experimental import pallas as pl
from jax.experimental.pallas import tpu as pltpu

F32 = jnp.float32
BF16 = jnp.bfloat16
EPS = 1e-6
GRID_W = 64
N_MOD = 6
TOP_K_INNER = 2
DECAY_TARGET = 1e-2
DECAY_FAST = 0.3
DECAY_SLOW = 1.5
LANES = 128
SUBLANES = 8
MXU_SIDE = 256
VMEM_LIMIT = 56 * 1024 * 1024


def _cp(sem, vmem=VMEM_LIMIT):
    return pltpu.CompilerParams(dimension_semantics=sem, vmem_limit_bytes=vmem)


def _dot(a, b):
    return jnp.dot(a, b, preferred_element_type=F32)


def _split(a):
    hi = a.astype(BF16)
    lo = (a - hi.astype(F32)).astype(BF16)
    return hi, lo


def _dot3(a, b):
    a_hi, a_lo = _split(a)
    b_hi, b_lo = _split(b)
    return _dot(a_hi, b_hi) + _dot(a_lo, b_hi) + _dot(a_hi, b_lo)


def _rms(x, g):
    ms = jnp.mean(x * x, axis=-1, keepdims=True)
    return x * lax.rsqrt(ms + EPS) * g


def _pack_halves(xb):
    n = xb.shape[1] // 2
    lo = lax.bitcast_convert_type(xb[:, :n].astype(F32), jnp.uint32)
    hi = lax.bitcast_convert_type(xb[:, n:].astype(F32), jnp.uint32)
    return (lo >> 16) | (hi & jnp.uint32(0xFFFF0000))


def _unpack_halves(w):
    lo = lax.bitcast_convert_type(w << 16, F32)
    hi = lax.bitcast_convert_type(w & jnp.uint32(0xFFFF0000), F32)
    return jnp.concatenate([lo, hi], axis=1).astype(BF16)


def _to_pair_rows(x, scr):
    m, n = x.shape
    ev, od = [], []
    for c in range(n // LANES):
        scr[c * m:(c + 1) * m, :] = x[:, c * LANES:(c + 1) * LANES]
        ev.append(scr[pl.ds(c * m, m // 2, stride=2), :])
        od.append(scr[pl.ds(c * m + 1, m // 2, stride=2), :])
    return jnp.concatenate(ev + od, axis=1)


def _from_pair_rows(x2, scr):
    h, n2 = x2.shape
    n = n2 // 2
    out = []
    for c in range(n // LANES):
        scr[pl.ds(2 * c * h, h, stride=2), :] = x2[:, c * LANES:(c + 1) * LANES]
        scr[pl.ds(2 * c * h + 1, h, stride=2), :] = x2[:, n + c * LANES:n + (c + 1) * LANES]
        out.append(scr[2 * c * h:2 * (c + 1) * h, :])
    return jnp.concatenate(out, axis=1)


def _store_row_tiles(ref, row0, xb):
    w = _pack_halves(xb)
    m, n = w.shape
    ns = n // LANES
    for c in range(ns):
        ref[pl.ds(row0 + c, m, stride=ns), :] = w[:, c * LANES:(c + 1) * LANES]


def _load_row_tiles(ref, row0, m, ns):
    w = jnp.concatenate([ref[pl.ds(row0 + c, m, stride=ns), :] for c in range(ns)], axis=1)
    return _unpack_halves(w)


def _ada_kernel(c_ref, w_ref, b_ref, o_ref):
    c = c_ref[...]
    a = c * jax.nn.sigmoid(c)
    o_ref[...] = _dot3(a, w_ref[...]) + b_ref[...]


def _ada_mod(cond8, w_ada, b_ada):
    d, n = w_ada.shape
    tn = min(n, 1024)
    return pl.pallas_call(
        _ada_kernel,
        out_shape=jax.ShapeDtypeStruct((SUBLANES, n), F32),
        grid=(n // tn,),
        in_specs=[pl.BlockSpec((SUBLANES, d), lambda j: (0, 0)),
                  pl.BlockSpec((d, tn), lambda j: (0, j)),
                  pl.BlockSpec((1, tn), lambda j: (0, j))],
        out_specs=pl.BlockSpec((SUBLANES, tn), lambda j: (0, j)),
        compiler_params=_cp(("arbitrary",)),
        name="ada_mod",
    )(cond8, w_ada, b_ada)


class _Tok:
    def __init__(self, n_ctx, n_lat, lat_len, tm):
        assert n_ctx % tm == 0 and lat_len % tm == 0 and n_lat % lat_len == 0
        self.tm = tm
        self.nct = n_ctx // tm
        self.nst = n_lat // tm
        self.per_seq = lat_len // tm
        self.n = self.nct + self.nst

    def ctx_idx(self, i):
        return jnp.minimum(i, self.nct - 1)

    def lat_idx(self, i):
        return jnp.maximum(i - self.nct, 0)

    def pos_idx(self, i):
        return self.lat_idx(i) % self.per_seq

    def cond(self, i):
        return jnp.where(i < self.nct, 0, 1 + self.lat_idx(i) // self.per_seq)


def _norm_mod_kernel(tok, xp_ref, xs_ref, pos_ref, sh_ref, sc_ref, g_ref, h_ref):
    i = pl.program_id(0)

    def modulate(x):
        y = _rms(x, g_ref[...])
        return (y * (1.0 + sc_ref[0]) + sh_ref[0]).astype(h_ref.dtype)

    @pl.when(i < tok.nct)
    def _():
        h_ref[...] = modulate(xp_ref[...])

    @pl.when(i >= tok.nct)
    def _():
        h_ref[...] = modulate(xs_ref[...] + pos_ref[...])


def _norm_mod(xp, xs, pos, mod, g1, tok):
    d = xp.shape[1]
    tm = tok.tm
    t_all = xp.shape[0] + xs.shape[0]
    return pl.pallas_call(
        functools.partial(_norm_mod_kernel, tok),
        out_shape=jax.ShapeDtypeStruct((t_all, d), BF16),
        grid=(tok.n,),
        in_specs=[
            pl.BlockSpec((tm, d), lambda i: (tok.ctx_idx(i), 0)),
            pl.BlockSpec((tm, d), lambda i: (tok.lat_idx(i), 0)),
            pl.BlockSpec((tm, d), lambda i: (tok.pos_idx(i), 0)),
            pl.BlockSpec((1, 1, d), lambda i: (tok.cond(i) * N_MOD + 0, 0, 0)),
            pl.BlockSpec((1, 1, d), lambda i: (tok.cond(i) * N_MOD + 1, 0, 0)),
            pl.BlockSpec((1, d), lambda i: (0, 0)),
        ],
        out_specs=pl.BlockSpec((tm, d), lambda i: (i, 0)),
        compiler_params=_cp(("parallel",)),
        name="norm_mod",
    )(xp, xs, pos, mod, mod, g1)


def _in_proj_kernel(n_sig, h_ref, w_ref, o_ref, wb_ref):
    n = pl.program_id(0)
    m = pl.program_id(1)

    @pl.when(m == 0)
    def _():
        wb_ref[...] = w_ref[...].astype(BF16)

    acc = _dot(h_ref[...], wb_ref[...])

    @pl.when(n >= n_sig)
    def _():
        o_ref[...] = acc.astype(o_ref.dtype)

    @pl.when(n < n_sig)
    def _():
        o_ref[...] = jax.nn.sigmoid(acc).astype(o_ref.dtype)


def _in_proj(h, w_in, sig_cols, lin_cols, out_dtype, tm, tn, name):
    t_all, d = h.shape
    for c in sig_cols + lin_cols:
        assert c % tn == 0
    assert t_all % tm == 0
    n_sig = (sig_cols[1] - sig_cols[0]) // tn
    n_lin = (lin_cols[1] - lin_cols[0]) // tn
    sig0, lin0 = sig_cols[0] // tn, lin_cols[0] // tn

    def w_blk(n):
        return jnp.where(n < n_sig, sig0 + n, lin0 + n - n_sig)

    return pl.pallas_call(
        functools.partial(_in_proj_kernel, n_sig),
        out_shape=jax.ShapeDtypeStruct((t_all, (n_sig + n_lin) * tn), out_dtype),
        grid=(n_sig + n_lin, t_all // tm),
        in_specs=[pl.BlockSpec((tm, d), lambda n, m: (m, 0)),
                  pl.BlockSpec((d, tn), lambda n, m: (0, w_blk(n)))],
        out_specs=pl.BlockSpec((tm, tn), lambda n, m: (m, n)),
        scratch_shapes=[pltpu.VMEM((d, tn), BF16)],
        compiler_params=_cp(("arbitrary", "arbitrary")),
        name=name,
    )(h, w_in)


CHUNK = 16
GROUP_PAD = 8


def _ssm_operators(a_re, a_im, log_dt, b_re, b_im, c_re, c_im, d_skip):
    t = CHUNK
    g, p = a_re.shape[1:]
    c = b_re.shape[-1]
    lam = lax.complex(a_re.astype(F32), a_im.astype(F32))
    dt = jnp.exp(log_dt.astype(F32))[..., None]
    ld = lam * dt
    lam_bar = jnp.exp(ld)
    b_bar = ((lam_bar - 1.0) / lam)[..., None] * lax.complex(b_re.astype(F32), b_im.astype(F32))
    c_mat = lax.complex(c_re.astype(F32), c_im.astype(F32))
    taus = jnp.arange(t + 1, dtype=F32)
    e_pow = jnp.exp(taus[None, None, :, None] * ld[:, :, None, :])

    k_all = jnp.real(jnp.einsum('dgop,dgtp,dgpi->dgtoi', c_mat, e_pow[:, :, :t], b_bar))
    kf, kb = k_all[0], k_all[1]
    ii = np.arange(t)[None, :, None]
    jj = np.arange(t)[None, None, :]
    tt = np.arange(t)[:, None, None]
    place_f = jnp.asarray((jj - ii == tt).astype(np.float32))
    place_b = jnp.asarray((ii - jj == tt).astype(np.float32))
    kf = kf.at[:, 0].add(jnp.eye(c, dtype=F32)[None] * d_skip.astype(F32).reshape(g, c, 1))
    m = jnp.einsum('tij,gtoc->gicjo', place_f, kf) + jnp.einsum('tij,gtoc->gicjo', place_b, kb)
    m = m.reshape(g, t * c, t * c)

    e_f = e_pow[0][:, ::-1][:, 1:]
    e_b = e_pow[1][:, :t]
    ws_f = e_f[:, :, :, None] * b_bar[0][:, None]
    ws_b = e_b[:, :, :, None] * b_bar[1][:, None]

    def rows_ic(x):
        return jnp.transpose(x, (0, 1, 3, 2)).reshape(g, t * c, p)

    w_state = jnp.concatenate([rows_ic(jnp.real(ws_f)), rows_ic(jnp.real(ws_b)),
                               rows_ic(jnp.imag(ws_f)), rows_ic(jnp.imag(ws_b))], axis=-1)

    ce_f = c_mat[0][:, None] * e_pow[0][:, 1:, None, :]
    ce_b = c_mat[1][:, None] * e_pow[1][:, ::-1][:, :t, None, :]

    def cols_jc(x):
        return jnp.transpose(x, (0, 3, 1, 2)).reshape(g, p, t * c)

    w_so = jnp.concatenate([cols_jc(jnp.real(ce_f)), cols_jc(jnp.real(ce_b)),
                            cols_jc(-jnp.imag(ce_f)), cols_jc(-jnp.imag(ce_b))], axis=1)
    w_out = jnp.concatenate([m, w_so], axis=1)

    a_t = e_pow[:, :, t]
    a_step = jnp.stack([jnp.concatenate([jnp.real(a_t[0]), jnp.real(a_t[1])], axis=-1),
                        jnp.concatenate([jnp.imag(a_t[0]), jnp.imag(a_t[1])], axis=-1)])
    return w_state.astype(BF16), w_out.astype(BF16), a_step


def _lane_perm(n_i, gb, c):
    n = n_i * gb * c
    src = np.arange(n)
    i, q, ch = src // (gb * c), (src // c) % gb, src % c
    dst = q * (n_i * c) + i * c + ch
    p = np.zeros((n, n), np.float32)
    p[src, dst] = 1.0
    return p


def _ssm_kernel(n_seq, n_chunk, gb, g_all, x_ref, ws_ref, wo_ref, a_ref, s0_ref, pin_ref, pout_ref,
                y_ref, fin_ref, xq_ref, za_ref, zb_ref):
    phase = pl.program_id(1)
    j = pl.program_id(2)
    gp = g_all + GROUP_PAD
    nck = n_seq * n_chunk
    half = LANES // 2
    hc = CHUNK // 2

    @pl.when(phase == 0)
    def _():
        halves = []
        for h in range(2):
            xcat = jnp.concatenate([x_ref[pl.ds(h * hc + i, nck, stride=CHUNK), :].astype(BF16)
                                    for i in range(hc)], axis=1)
            halves.append(_dot(xcat, pin_ref[...]).astype(BF16))
        for q in range(gb):
            xq = jnp.concatenate([hv[:, q * LANES:(q + 1) * LANES] for hv in halves], axis=1)
            xq_ref[j * gb + q] = xq
            z = _dot(xq, ws_ref[q])
            za_ref[pl.ds(j * gb + q, nck, stride=gp), :] = z[:, :LANES]
            zb_ref[pl.ds(j * gb + q, nck, stride=gp), :] = z[:, LANES:]

    @pl.when(jnp.logical_and(phase == 1, j == 0))
    def _():
        a_re = a_ref[0]
        a_im = a_ref[1]
        lane = lax.broadcasted_iota(jnp.int32, (g_all, LANES), 1)
        is_f = lane < half
        for s in range(n_seq):
            def step(k, carry):
                s_a, s_b = carry
                rf = pl.multiple_of((s * n_chunk + k) * gp, SUBLANES)
                rb = pl.multiple_of((s * n_chunk + n_chunk - 1 - k) * gp, SUBLANES)
                zf_a = za_ref[pl.ds(rf, g_all), :]
                zf_b = zb_ref[pl.ds(rf, g_all), :]
                zb_a = za_ref[pl.ds(rb, g_all), :]
                zb_b = zb_ref[pl.ds(rb, g_all), :]
                za_ref[pl.ds(rf, g_all), :] = jnp.where(is_f, s_a, zf_a)
                zb_ref[pl.ds(rf, g_all), :] = jnp.where(is_f, s_b, zf_b)
                za_ref[pl.ds(rb, g_all), :] = jnp.where(is_f, zb_a, s_a)
                zb_ref[pl.ds(rb, g_all), :] = jnp.where(is_f, zb_b, s_b)
                z_a = jnp.where(is_f, zf_a, zb_a)
                z_b = jnp.where(is_f, zf_b, zb_b)
                n_a = a_re * s_a - a_im * s_b + z_a
                n_b = a_re * s_b + a_im * s_a + z_b
                return n_a, n_b

            init = s0_ref[s]
            f_a, f_b = lax.fori_loop(0, n_chunk, step, (init[:, :LANES], init[:, LANES:]))
            fin_ref[s] = jnp.concatenate([f_a, f_b], axis=-1)

    @pl.when(phase == 1)
    def _():
        ys = []
        for q in range(gb):
            st_a = za_ref[pl.ds(j * gb + q, nck, stride=gp), :].astype(BF16)
            st_b = zb_ref[pl.ds(j * gb + q, nck, stride=gp), :].astype(BF16)
            lhs = jnp.concatenate([xq_ref[j * gb + q], st_a, st_b], axis=-1)
            ys.append(_dot(lhs, wo_ref[q]).astype(BF16))
        for h in range(2):
            ycat = jnp.concatenate([y[:, h * LANES:(h + 1) * LANES] for y in ys], axis=1)
            yall = _dot(ycat, pout_ref[...])
            for i in range(hc):
                y_ref[pl.ds(h * hc + i, nck, stride=CHUNK), :] = yall[:, i * LANES:(i + 1) * LANES]


def _ssm_call(u, row_blk0, n_rows, w_state, w_out, a_step, s0, n_seq, n_chunk, name):
    g_all, w, _ = w_state.shape
    nck = n_seq * n_chunk
    blk = nck * CHUNK
    n_blk = n_rows // blk
    gb = LANES * CHUNK // w
    ngb = g_all // gb
    gp = g_all + GROUP_PAD
    kern = functools.partial(_ssm_kernel, n_seq, n_chunk, gb, g_all)
    assert w == 2 * LANES
    perm = _lane_perm(CHUNK // 2, gb, LANES // gb)
    p_in = jnp.asarray(perm, BF16)
    p_out = jnp.asarray(perm.T, BF16)
    n_perm = perm.shape[0]
    const = lambda b, ph, j: (0, 0)
    return pl.pallas_call(
        kern,
        out_shape=(jax.ShapeDtypeStruct((n_rows, u.shape[1]), F32),
                   jax.ShapeDtypeStruct((n_blk * n_seq, g_all, w), F32)),
        grid=(n_blk, 2, ngb),
        in_specs=[
            pl.BlockSpec((blk, LANES), lambda b, ph, j: (row_blk0 + b, jnp.where(ph == 0, j, ngb - 1))),
            pl.BlockSpec((gb, w, w), lambda b, ph, j: (jnp.where(ph == 0, j, ngb - 1), 0, 0)),
            pl.BlockSpec((gb, 2 * w, w), lambda b, ph, j: (jnp.where(ph == 0, 0, j), 0, 0)),
            pl.BlockSpec((2, g_all, LANES), lambda b, ph, j: (0, 0, 0)),
            pl.BlockSpec((n_seq, g_all, w), lambda b, ph, j: (b, 0, 0)),
            pl.BlockSpec((n_perm, n_perm), const, pipeline_mode=pl.Buffered(1)),
            pl.BlockSpec((n_perm, n_perm), const, pipeline_mode=pl.Buffered(1)),
        ],
        out_specs=(
            pl.BlockSpec((blk, LANES), lambda b, ph, j: (b, jnp.where(ph == 0, 0, j))),
            pl.BlockSpec((n_seq, g_all, w), lambda b, ph, j: (b, 0, 0)),
        ),
        scratch_shapes=[pltpu.VMEM((g_all, nck, w), BF16),
                        pltpu.VMEM((nck * gp, LANES), F32), pltpu.VMEM((nck * gp, LANES), F32)],
        compiler_params=_cp(("arbitrary", "arbitrary", "arbitrary")),
        name=name,
    )(u, w_state, w_out, a_step, s0, p_in, p_out)


def _glu_kernel(nct, yc_ref, yl_ref, wg_ref, bg_ref, wb_ref, o_ref):
    i = pl.program_id(0)
    y = jax.nn.gelu(jnp.where(i < nct, yc_ref[...], yl_ref[...]))
    z = _dot(y.astype(BF16), wg_ref[...]) + bg_ref[...]
    ya = y * jax.nn.sigmoid(z)
    o_ref[...] = _dot(ya.astype(BF16), wb_ref[...]).astype(o_ref.dtype)


def _glu_branch(y_ctx, y_lat, w_glu, b_glu, w_ba, tok):
    c = y_ctx.shape[1]
    d = w_ba.shape[1]
    tm = tok.tm
    return pl.pallas_call(
        functools.partial(_glu_kernel, tok.nct),
        out_shape=jax.ShapeDtypeStruct((y_ctx.shape[0] + y_lat.shape[0], d), BF16),
        grid=(tok.n,),
        in_specs=[pl.BlockSpec((tm, c), lambda i: (tok.ctx_idx(i), 0)),
                  pl.BlockSpec((tm, c), lambda i: (tok.lat_idx(i), 0)),
                  pl.BlockSpec((c, c), lambda i: (0, 0)),
                  pl.BlockSpec((1, c), lambda i: (0, 0)),
                  pl.BlockSpec((c, d), lambda i: (0, 0))],
        out_specs=pl.BlockSpec((tm, d), lambda i: (i, 0)),
        compiler_params=_cp(("parallel",)),
        name="glu_branch",
    )(y_ctx, y_lat, w_glu, b_glu, w_ba)


def _short_conv_kernel(nct, per_ctx, per_lat, v_ref, x0_ref, x1_ref, vp_ref, x0p_ref, x1p_ref,
                       vn_ref, x0n_ref, x1n_ref, w_ref, b_ref, vx_ref, x0c_ref, scr):
    i = pl.program_id(0)
    tm = v_ref.shape[0]
    k = jnp.where(i < nct, i % per_ctx, (i - nct) % per_lat)
    per = jnp.where(i < nct, per_ctx, per_lat)
    first = k == 0
    last = k == per - 1
    row = lax.broadcasted_iota(jnp.int32, (tm, 1), 0)

    def conv(cur_ref, prev_ref, next_ref, part):
        x = cur_ref[...].astype(F32)
        hp = jnp.where(first, 0.0, prev_ref[SUBLANES - 1:SUBLANES, :].astype(F32))
        hn = jnp.where(last, 0.0, next_ref[0:1, :].astype(F32))
        xm = jnp.where(row == 0, hp, pltpu.roll(x, 1, axis=0))
        xq = jnp.where(row == tm - 1, hn, pltpu.roll(x, tm - 1, axis=0))
        w = w_ref[part]
        return b_ref[part] + xm * w[0:1, :] + x * w[1:2, :] + xq * w[2:3, :]

    v = conv(v_ref, vp_ref, vn_ref, 0)
    x0 = conv(x0_ref, x0p_ref, x0n_ref, 1)
    x1 = conv(x1_ref, x1p_ref, x1n_ref, 2)
    vx_ref[...] = _to_pair_rows(v * x1, scr).astype(vx_ref.dtype)
    x0c_ref[...] = _to_pair_rows(x0, scr).astype(x0c_ref.dtype)


def _short_conv(proj, col_blk0, hw, w_short3, b_short3, n_ctx, seq, dseq, tm):
    t = proj.shape[0]
    assert seq % tm == 0 and dseq % tm == 0
    nt = t // tm
    r8 = tm // SUBLANES
    nb8 = t // SUBLANES

    def cur(k):
        return pl.BlockSpec((tm, hw), lambda i: (i, col_blk0 + k))

    def prev(k):
        return pl.BlockSpec((SUBLANES, hw), lambda i: (jnp.maximum(i * r8 - 1, 0), col_blk0 + k))

    def nxt(k):
        return pl.BlockSpec((SUBLANES, hw), lambda i: (jnp.minimum((i + 1) * r8, nb8 - 1), col_blk0 + k))

    kern = functools.partial(_short_conv_kernel, n_ctx // tm, seq // tm, dseq // tm)
    return pl.pallas_call(
        kern,
        out_shape=(jax.ShapeDtypeStruct((t // 2, 2 * hw), BF16), jax.ShapeDtypeStruct((t // 2, 2 * hw), BF16)),
        grid=(nt,),
        in_specs=[cur(0), cur(1), cur(2), prev(0), prev(1), prev(2), nxt(0), nxt(1), nxt(2),
                  pl.BlockSpec((3, 3, hw), lambda i: (0, 0, 0)),
                  pl.BlockSpec((3, 1, hw), lambda i: (0, 0, 0))],
        out_specs=(pl.BlockSpec((tm // 2, 2 * hw), lambda i: (i, 0)),
                   pl.BlockSpec((tm // 2, 2 * hw), lambda i: (i, 0))),
        scratch_shapes=[pltpu.VMEM((hw // LANES * tm, LANES), F32)],
        compiler_params=_cp(("parallel",)),
        name="short_conv",
    )(proj, proj, proj, proj, proj, proj, proj, proj, proj, w_short3, b_short3)


def _filt_gen_kernel(seq_len, n_bands, wt_ref, wc_ref, ws_ref, b1_ref, f1_ref, w2_ref, b2_ref, f2_ref,
                     w3_ref, b3_ref, rate_ref, a_ref, bm_ref, abs_ref, scr):
    i = pl.program_id(0)
    tl = 2 * a_ref.shape[0]
    hw = a_ref.shape[1] // 2
    pos = lax.broadcasted_iota(jnp.int32, (tl, 1), 0) + i * tl
    t = pos.astype(F32) / seq_len
    bands = (lax.broadcasted_iota(jnp.int32, (1, wc_ref.shape[0]), 1) + 1).astype(F32)
    ang = (2.0 * math.pi) * t * bands
    pre = t * wt_ref[...] + _dot3(jnp.cos(ang), wc_ref[...]) + _dot3(jnp.sin(ang), ws_ref[...])
    h = jnp.sin(f1_ref[...] * (pre + b1_ref[...]))
    h = jnp.sin(f2_ref[...] * (_dot3(h, w2_ref[...]) + b2_ref[...]))
    h = _dot3(h, w3_ref[...]) + b3_ref[...]
    h = h * jnp.exp(-t * rate_ref[...])
    h_f = h[:, :hw]
    h_b = h[:, hw:]
    colabs = jnp.sum(jnp.abs(h_f) + jnp.abs(h_b), axis=0, keepdims=True)
    h_bp = jnp.where(pos == 0, 0.0, h_b)
    a_ref[...] = _to_pair_rows(h_f + h_bp, scr).astype(a_ref.dtype)
    bm_ref[...] = _to_pair_rows(h_bp - h_f, scr).astype(bm_ref.dtype)

    @pl.when(i == 0)
    def _():
        abs_ref[...] = colabs

    @pl.when(i > 0)
    def _():
        abs_ref[...] += colabs


def _filt_gen(seq_len, w_f1, b_f1, freq1, w_f2, b_f2, freq2, w_f3, b_f3, rates2):
    n_emb, hid_raw = w_f1.shape
    n_bands = (n_emb - 1) // 2
    hw2 = w_f3.shape[1]
    hw = hw2 // 2
    tl = min(seq_len, 512)
    hid = LANES
    assert n_bands <= LANES and hid_raw <= LANES

    def pad(x, rows, cols):
        return jnp.zeros((rows, cols), F32).at[:x.shape[0], :x.shape[1]].set(x.astype(F32))

    wt = pad(w_f1[0:1], 1, hid)
    wc = pad(w_f1[1:1 + n_bands], LANES, hid)
    ws = pad(w_f1[1 + n_bands:], LANES, hid)
    b_f1, freq1, b_f2, freq2 = (pad(x, 1, hid) for x in (b_f1, freq1, b_f2, freq2))
    w_f2 = pad(w_f2, hid, hid)
    w_f3 = pad(w_f3, hid, hw2)
    full = lambda shape: pl.BlockSpec(shape, lambda i: (0,) * len(shape))
    kern = functools.partial(_filt_gen_kernel, seq_len, n_bands)
    return pl.pallas_call(
        kern,
        out_shape=(jax.ShapeDtypeStruct((seq_len // 2, hw2), BF16), jax.ShapeDtypeStruct((seq_len // 2, hw2), BF16),
                   jax.ShapeDtypeStruct((1, hw), F32)),
        grid=(seq_len // tl,),
        in_specs=[full((1, hid)), full((LANES, hid)), full((LANES, hid)), full((1, hid)), full((1, hid)),
                  full((hid, hid)), full((1, hid)), full((1, hid)), full((hid, hw2)), full((1, hw2)),
                  full((1, hw2))],
        out_specs=(pl.BlockSpec((tl // 2, hw2), lambda i: (i, 0)), pl.BlockSpec((tl // 2, hw2), lambda i: (i, 0)),
                   full((1, hw))),
        scratch_shapes=[pltpu.VMEM((hw // LANES * tl, LANES), F32)],
        compiler_params=_cp(("arbitrary",)),
        name="filt_gen",
    )(wt, wc, ws, b_f1, freq1, w_f2, b_f2, freq2, w_f3, b_f3, rates2)


def _trig_tables(row_hi, row_lo, col, period):
    col = col[None, :]
    ang_lo = ((row_lo[:, None] * col) % period).astype(F32) * (2.0 * math.pi / period)
    ang_hi = ((row_hi[:, None] * col) % period).astype(F32) * (2.0 * math.pi / period)
    c_lo, s_lo = jnp.cos(ang_lo)[None], jnp.sin(ang_lo)[None]
    c_hi, s_hi = jnp.cos(ang_hi)[:, None], jnp.sin(ang_hi)[:, None]
    n = row_hi.shape[0] * row_lo.shape[0]
    ctab = (c_hi * c_lo - s_hi * s_lo).reshape(n, col.shape[1])
    stab = (s_hi * c_lo + c_hi * s_lo).reshape(n, col.shape[1])
    return ctab.astype(BF16), stab.astype(BF16)


def _dft_tables(seq_len):
    m = seq_len // 2
    r = 1
    while r * r < m:
        r *= 2
    assert m % r == 0
    i32 = jnp.int32
    idx = jnp.arange(m, dtype=i32)
    hi = jnp.arange(m // r, dtype=i32) * r
    lo = jnp.arange(r, dtype=i32)
    ce, se = _trig_tables(hi, lo, idx, 2 * m)
    co, so = _trig_tables(hi, lo, 2 * idx + 1, 2 * seq_len)
    cot, sot = _trig_tables(2 * hi, 2 * lo + 1, idx, 2 * seq_len)
    return ce, se, co, so, cot, sot


def _alt_row(n):
    return (1 - 2 * (lax.broadcasted_iota(jnp.int32, (SUBLANES, n), 1) % 2)).astype(BF16)


def _filt_dft_kernel(seq_len, ce_ref, se_ref, co_ref, so_ref, ae_ref, ao_ref, be_ref, bo_ref, abs_ref,
                     krl_ref, krh_ref, kil_ref, kih_ref, kmid_ref):
    i = pl.program_id(0)
    scale = (1.0 / seq_len) / (abs_ref[...] + EPS)
    ec = _dot(ce_ref[...], ae_ref[...])
    oc = _dot(co_ref[...], ao_ref[...])
    es = _dot(se_ref[...], be_ref[...])
    os_ = _dot(so_ref[...], bo_ref[...])
    krl_ref[...] = (ec + oc) * scale
    krh_ref[...] = (ec - oc) * scale
    kil_ref[...] = (es + os_) * scale
    kih_ref[...] = (os_ - es) * scale

    @pl.when(i == 0)
    def _():
        sgn = _alt_row(ae_ref.shape[0])
        kmid_ref[0:1, :] = _dot(sgn, ae_ref[...])[0:1] * scale
        kmid_ref[1:2, :] = _dot(sgn, bo_ref[...])[0:1] * scale


def _filt_dft(tabs, a2, bm2, colabs, seq_len):
    m = seq_len // 2
    hw = a2.shape[1] // 2
    ce, se, co, so = tabs[:4]
    tf = min(m, 256)
    kern = functools.partial(_filt_dft_kernel, seq_len)
    mode = pl.Buffered(1) if m * hw * 2 > BIG_BLOCK_BYTES // 2 else None
    tab = pl.BlockSpec((tf, m), lambda i: (i, 0))
    res = lambda c: pl.BlockSpec((m, hw), lambda i: (0, c), pipeline_mode=mode)
    out = pl.BlockSpec((tf, hw), lambda i: (i, 0))
    sds = jax.ShapeDtypeStruct((m, hw), F32)
    return pl.pallas_call(
        kern,
        out_shape=(sds, sds, sds, sds, jax.ShapeDtypeStruct((2, hw), F32)),
        grid=(m // tf,),
        in_specs=[tab, tab, tab, tab, res(0), res(1), res(0), res(1), pl.BlockSpec((1, hw), lambda i: (0, 0))],
        out_specs=(out, out, out, out, pl.BlockSpec((2, hw), lambda i: (0, 0))),
        compiler_params=_cp(("arbitrary",)),
        name="filt_dft",
    )(ce, se, co, so, a2, a2, bm2, bm2, colabs)


def _hy_fwd_kernel(ce_ref, se_ref, co_ref, so_ref, ve_ref, vo_ref, krl_ref, krh_ref, kil_ref, kih_ref, kmid_ref,
                   ae_ref, be_ref, ao_ref, bo_ref, ymid_ref):
    j = pl.program_id(1)
    tf, m = ce_ref.shape
    nb = ae_ref.shape[0]
    krl, krh, kil, kih = krl_ref[...], krh_ref[...], kil_ref[...], kih_ref[...]
    row = lax.broadcasted_iota(jnp.int32, (tf, 1), 0) + j * tf
    dc = jnp.where(row == 0, 0.5, 1.0)
    for s in range(nb):
        ve = ve_ref[s * m:(s + 1) * m, :]
        vo = vo_ref[s * m:(s + 1) * m, :]
        ec = _dot(ce_ref[...], ve)
        oc = _dot(co_ref[...], vo)
        es = _dot(se_ref[...], ve)
        os_ = _dot(so_ref[...], vo)
        p_lo, p_hi = ec + oc, ec - oc
        q_lo, q_hi = es + os_, os_ - es
        yre_lo = (p_lo * krl + q_lo * kil) * dc
        yim_lo = p_lo * kil - q_lo * krl
        yre_hi = (p_hi * krh + q_hi * kih) * dc
        yim_hi = p_hi * kih - q_hi * krh
        ae_ref[s] = (yre_lo + yre_hi).astype(ae_ref.dtype)
        be_ref[s] = (yim_lo - yim_hi).astype(be_ref.dtype)
        ao_ref[s] = (yre_lo - yre_hi).astype(ao_ref.dtype)
        bo_ref[s] = (yim_lo + yim_hi).astype(bo_ref.dtype)

    @pl.when(j == 0)
    def _():
        sgn = _alt_row(m)
        kr_m, ki_m = kmid_ref[0:1, :], kmid_ref[1:2, :]
        for s in range(nb):
            p_m = _dot(sgn, ve_ref[s * m:(s + 1) * m, :])[0:1]
            q_m = _dot(sgn, vo_ref[s * m:(s + 1) * m, :])[0:1]
            ymid_ref[s, 0:1, :] = p_m * kr_m + q_m * ki_m
            ymid_ref[s, 1:2, :] = p_m * ki_m - q_m * kr_m


BIG_BLOCK_BYTES = 4 * 1024 * 1024


def _seq_batch(seq_len, n_seq):
    nb = max(1, min(n_seq, 4 * MXU_SIDE // seq_len))
    while n_seq % nb:
        nb -= 1
    return nb


def _hy_fwd(tabs, vx2, kspec, n_seq, seq_len, row0_blk):
    m = seq_len // 2
    hw = vx2.shape[1] // 2
    ce, se, co, so = tabs[:4]
    krl, krh, kil, kih, kmid = kspec
    tf = min(m, 256)
    nb = _seq_batch(seq_len, n_seq)
    assert row0_blk % nb == 0
    mode = pl.Buffered(1) if nb * m * hw * 2 > BIG_BLOCK_BYTES // 2 else None
    tab = pl.BlockSpec((tf, m), lambda b, j: (j, 0))
    vspec = lambda c: pl.BlockSpec((nb * m, hw), lambda b, j: (row0_blk // nb + b, c), pipeline_mode=mode)
    kt = pl.BlockSpec((tf, hw), lambda b, j: (j, 0))
    out = pl.BlockSpec((nb, tf, hw), lambda b, j: (b, j, 0))
    sds = jax.ShapeDtypeStruct((n_seq, m, hw), BF16)
    return pl.pallas_call(
        _hy_fwd_kernel,
        out_shape=(sds, sds, sds, sds, jax.ShapeDtypeStruct((n_seq, 2, hw), F32)),
        grid=(n_seq // nb, m // tf),
        in_specs=[tab, tab, tab, tab, vspec(0), vspec(1), kt, kt, kt, kt,
                  pl.BlockSpec((2, hw), lambda b, j: (0, 0))],
        out_specs=(out, out, out, out, pl.BlockSpec((nb, 2, hw), lambda b, j: (b, 0, 0))),
        compiler_params=_cp(("arbitrary", "arbitrary")),
        name="hy_fwd",
    )(ce, se, co, so, vx2, vx2, krl, krh, kil, kih, kmid)


def _hy_inv_kernel(ce_ref, se_ref, cot_ref, sot_ref, ae_ref, be_ref, ao_ref, bo_ref, ymid_ref,
                   vx_ref, x0_ref, fb_ref, wb_ref, o_ref, scr):
    j = pl.program_id(1)
    tt = ce_ref.shape[0]
    nb = ae_ref.shape[0]
    hw = fb_ref.shape[1]
    row = lax.broadcasted_iota(jnp.int32, (tt, 1), 0) + j * tt
    sgn = (1 - 2 * (row % 2)).astype(F32)
    for s in range(nb):
        rows = slice(s * tt, (s + 1) * tt)
        ymid = ymid_ref[s]
        z_e = _dot(ce_ref[...], ae_ref[s]) - _dot(se_ref[...], be_ref[s]) + sgn * ymid[0:1]
        z_o = _dot(cot_ref[...], ao_ref[s]) - _dot(sot_ref[...], bo_ref[s]) - sgn * ymid[1:2]
        proj = []
        for par, z in ((0, z_e), (1, z_o)):
            cols = slice(par * hw, (par + 1) * hw)
            z = z + vx_ref[rows, cols].astype(F32) * fb_ref[...]
            yb = x0_ref[rows, cols].astype(F32) * z
            proj.append(_dot(yb.astype(BF16), wb_ref[...]))
        o_ref[2 * s * tt:2 * (s + 1) * tt, :] = _from_pair_rows(
            jnp.concatenate(proj, axis=1), scr).astype(o_ref.dtype)


def _hy_inv(tabs, spec, vx2, x0c2, fbias, w_bb, n_seq, seq_len, row0_blk):
    m = seq_len // 2
    hw = fbias.shape[1]
    d = w_bb.shape[1]
    ce, se, _, _, cot, sot = tabs
    ae, be, ao, bo, ymid = spec
    tt = min(m, 128)
    per = m // tt
    nb = _seq_batch(seq_len, n_seq) if per == 1 else 1
    assert row0_blk % nb == 0
    mode = pl.Buffered(1) if nb * m * hw * 2 > BIG_BLOCK_BYTES // 2 else None
    tab = pl.BlockSpec((tt, m), lambda b, j: (j, 0))
    res = pl.BlockSpec((nb, m, hw), lambda b, j: (b, 0, 0), pipeline_mode=mode)
    rows_in = pl.BlockSpec((nb * tt, 2 * hw), lambda b, j: ((row0_blk // nb + b) * per + j, 0))
    return pl.pallas_call(
        _hy_inv_kernel,
        out_shape=jax.ShapeDtypeStruct((n_seq * seq_len, d), BF16),
        grid=(n_seq // nb, per),
        in_specs=[tab, tab, tab, tab, res, res, res, res,
                  pl.BlockSpec((nb, 2, hw), lambda b, j: (b, 0, 0)),
                  rows_in, rows_in,
                  pl.BlockSpec((1, hw), lambda b, j: (0, 0)),
                  pl.BlockSpec((hw, d), lambda b, j: (0, 0), pipeline_mode=pl.Buffered(1))],
        out_specs=pl.BlockSpec((nb * 2 * tt, d), lambda b, j: (b * per + j, 0)),
        scratch_shapes=[pltpu.VMEM((d // LANES * 2 * tt, LANES), F32)],
        compiler_params=_cp(("arbitrary", "arbitrary")),
        name="hy_inv",
    )(ce, se, cot, sot, ae, be, ao, bo, ymid, vx2, x0c2, fbias, w_bb)


MERGE_COL_BLOCKS = 4

def _expert_pairs(epg):
    pairs = [(a, b) for a in range(epg) for b in range(a + 1, epg)]
    order = [pairs.pop(0)]
    while pairs:
        last = order[-1]
        nxt = next((p for p in pairs if set(p) & set(last)), pairs[0])
        pairs.remove(nxt)
        order.append(nxt)
    return order


def _merge_kernel(tok, n_groups, epg, xp_ref, xs_ref, pos_ref, g0_ref, g1_ref, pa_ref, pbc_ref, pbs_ref,
                  gate1_ref, sh2_ref, sc2_ref, gn_ref, wo_ref, wr_ref, br_ref,
                  x2_ref, h2_ref, rt_ref):
    i = pl.program_id(0)
    is_ctx = i < tok.nct
    tm, d = x2_ref.shape
    pb = jnp.where(is_ctx, pbc_ref[...], pbs_ref[...])
    merged = g0_ref[...] * pa_ref[...] + g1_ref[...] * pb
    cb = d // MERGE_COL_BLOCKS
    ssq = jnp.zeros((tm, 1), F32)
    for c in range(MERGE_COL_BLOCKS):
        cols = slice(c * cb, (c + 1) * cb)
        att = _dot(merged, wo_ref[:, cols])
        x = jnp.where(is_ctx, xp_ref[:, cols], xs_ref[:, cols] + pos_ref[:, cols])
        x2c = x + gate1_ref[0][:, cols] * att
        x2_ref[:, cols] = x2c
        ssq = ssq + jnp.sum(x2c * x2c, axis=-1, keepdims=True)
    x2 = x2_ref[...]
    h2 = x2 * lax.rsqrt(ssq * (1.0 / d) + EPS) * gn_ref[...] * (1.0 + sc2_ref[0]) + sh2_ref[0]
    h2b = h2.astype(BF16)
    _store_row_tiles(h2_ref, 0, h2b)
    logits = _dot(h2b, wr_ref[...]) + br_ref[...]
    lane = lax.broadcasted_iota(jnp.int32, logits.shape, 1)
    gl = jnp.where(lane < n_groups, logits, -jnp.inf)
    gmax = jnp.max(gl, axis=-1, keepdims=True)
    gidx = jnp.min(jnp.where(gl == gmax, lane, n_groups), axis=-1, keepdims=True)
    base = n_groups + gidx * epg
    el = jnp.where(jnp.logical_and(lane >= base, lane < base + epg), logits, -jnp.inf)
    i1 = jnp.min(jnp.where(el == jnp.max(el, axis=-1, keepdims=True), lane, LANES), axis=-1, keepdims=True)
    el = jnp.where(lane == i1, -jnp.inf, el)
    i2 = jnp.min(jnp.where(el == jnp.max(el, axis=-1, keepdims=True), lane, LANES), axis=-1, keepdims=True)
    lo = jnp.minimum(i1, i2) - base
    hi = jnp.maximum(i1, i2) - base
    cls = jnp.zeros_like(gidx)
    for code, (pa_, pb_) in enumerate(_expert_pairs(epg)):
        cls = cls + code * jnp.logical_and(lo == pa_, hi == pb_).astype(jnp.int32)
    rt_ref[...] = jnp.where(lane == LANES - 1, gidx.astype(F32),
                            jnp.where(lane == LANES - 2, cls.astype(F32), logits))


def _merge_out(xp, xs, pos, proj, gate_blk0, pa, pb_ctx, pb_lat, mod, g2, w_out, w_r, b_r, n_groups, epg, tok):
    d = xp.shape[1]
    tm = tok.tm
    t_all = xp.shape[0] + xs.shape[0]
    kern = functools.partial(_merge_kernel, tok, n_groups, epg)
    one = pl.Buffered(1)
    ns = d // 2 // LANES

    def modspec(k):
        return pl.BlockSpec((1, 1, d), lambda i: (tok.cond(i) * N_MOD + k, 0, 0))

    return pl.pallas_call(
        kern,
        out_shape=(jax.ShapeDtypeStruct((t_all, d), F32), jax.ShapeDtypeStruct((t_all * ns, LANES), jnp.uint32),
                   jax.ShapeDtypeStruct((t_all, LANES), F32)),
        grid=(tok.n,),
        in_specs=[
            pl.BlockSpec((tm, d), lambda i: (tok.ctx_idx(i), 0)),
            pl.BlockSpec((tm, d), lambda i: (tok.lat_idx(i), 0)),
            pl.BlockSpec((tm, d), lambda i: (tok.pos_idx(i), 0)),
            pl.BlockSpec((tm, d), lambda i: (i, gate_blk0)),
            pl.BlockSpec((tm, d), lambda i: (i, gate_blk0 + 1)),
            pl.BlockSpec((tm, d), lambda i: (i, 0)),
            pl.BlockSpec((tm, d), lambda i: (tok.ctx_idx(i), 0)),
            pl.BlockSpec((tm, d), lambda i: (tok.lat_idx(i), 0)),
            modspec(2), modspec(3), modspec(4),
            pl.BlockSpec((1, d), lambda i: (0, 0)),
            pl.BlockSpec((d, d), lambda i: (0, 0), pipeline_mode=one),
            pl.BlockSpec((d, LANES), lambda i: (0, 0)),
            pl.BlockSpec((1, LANES), lambda i: (0, 0)),
        ],
        out_specs=(pl.BlockSpec((tm, d), lambda i: (i, 0)), pl.BlockSpec((tm * ns, LANES), lambda i: (i, 0)),
                   pl.BlockSpec((tm, LANES), lambda i: (i, 0))),
        compiler_params=_cp(("parallel",)),
        name="merge_out",
    )(xp, xs, pos, proj, proj, pa, pb_ctx, pb_lat, mod, mod, mod, g2, w_out, w_r, b_r)


ROW_UNROLL = 8

def _moe_kernel(n_tiles, n_groups, epg, tg_ref, tnv_ref, rt_ref, cb_ref, kx_ref, wi_ref,
                h2_hbm, wr_ref, br_ref, wg_ref, wu_ref, wd_ref, out_hbm,
                xbuf, xb_ref, acc, obuf, gws, gsem, ssem):
    i = pl.program_id(0)
    j = pl.program_id(1)
    tm = xb_ref.shape[0]
    ns = h2_hbm.shape[1]
    nv = tnv_ref[i]
    slot = i % 2

    def buf_rows(buf, slot_, r, rows):
        return buf.at[pl.ds(pl.multiple_of((slot_ * tm + r) * ns, ns), rows * ns), :]

    def gather_copy(tok_row, slot_, r):
        return pltpu.make_async_copy(h2_hbm.at[tok_row], buf_rows(xbuf, slot_, r, 1), gsem.at[slot_])

    def gather_wait(slot_, rows):
        dst = buf_rows(xbuf, slot_, 0, rows)
        pltpu.make_async_copy(dst, dst, gsem.at[slot_]).wait()

    def scatter_copy(tok_row, slot_, r):
        return pltpu.make_async_copy(buf_rows(obuf, slot_, r, 1), out_hbm.at[tok_row], ssem.at[0])

    def scatter_wait(slot_, rows):
        src = buf_rows(obuf, slot_, 0, rows)
        pltpu.make_async_copy(src, src, ssem.at[0]).wait()

    def for_rows(count, fn):
        n_full = count // ROW_UNROLL

        def body(b, c):
            for k in range(ROW_UNROLL):
                fn(b * ROW_UNROLL + k)
            return c

        def tail(r, c):
            fn(r)
            return c

        lax.fori_loop(0, n_full, body, 0)
        lax.fori_loop(n_full * ROW_UNROLL, count, tail, 0)

    def wait_scatter(count, slot_):
        p = tm
        while p >= 1:
            @pl.when((count & p) != 0)
            def _(p=p):
                scatter_wait(slot_, p)
            p //= 2

    nv_prev = tnv_ref[jnp.maximum(i - 1, 0)]
    nv_next = tnv_ref[jnp.minimum(i + 1, n_tiles - 1)]
    prev_deferred = jnp.logical_and(i > 0, jnp.logical_and(nv_prev == tm, nv > 0))
    deferred = jnp.logical_and(nv == tm, nv_next > 0)

    @pl.when(j == 0)
    def _():
        @pl.when(i == 0)
        def _():
            for_rows(tm, lambda r: gather_copy(rt_ref[r], 0, r).start())

        @pl.when(jnp.logical_or(i == 0, nv_prev > 0))
        def _():
            gather_wait(slot, tm)

    @pl.when(jnp.logical_and(j == 0, nv > 0))
    def _():
        grp = tg_ref[i]
        xb = _load_row_tiles(xbuf, slot * tm * ns, tm, ns)
        xb_ref[...] = xb
        logits = _dot(xb, wr_ref[...]) + br_ref[...]
        lane = lax.broadcasted_iota(jnp.int32, logits.shape, 1)
        gl = jnp.where(lane < n_groups, logits, -jnp.inf)
        gmax = jnp.max(gl, axis=-1, keepdims=True)
        ge = jnp.exp(gl - gmax)
        g_w = jnp.sum(jnp.where(lane == grp, ge, 0.0), axis=-1, keepdims=True) / jnp.sum(ge, axis=-1, keepdims=True)
        base = n_groups + grp * epg
        e = [jnp.sum(jnp.where(lane == base + k, logits, 0.0), axis=-1, keepdims=True) for k in range(epg)]
        pairs = _expert_pairs(epg)
        row = lax.broadcasted_iota(jnp.int32, (tm, 1), 0)
        cls = jnp.zeros((tm, 1), jnp.int32)
        for c in range(len(pairs) - 1):
            cls = cls + (row >= cb_ref[i * len(pairs) + c]).astype(jnp.int32)
        sel = [functools.reduce(jnp.logical_or, [cls == c for c, p in enumerate(pairs) if k in p])
               for k in range(epg)]
        emax = functools.reduce(jnp.maximum, [jnp.where(sel[k], e[k], -jnp.inf) for k in range(epg)])
        ex = [jnp.where(sel[k], jnp.exp(e[k] - emax), 0.0) for k in range(epg)]
        den = functools.reduce(lambda a, b: a + b, ex)
        for k in range(epg):
            gws[k] = ex[k] / den * g_w
        acc[...] = jnp.zeros(acc.shape, acc.dtype)

    kx = kx_ref[i * epg + j]

    def expert_step(issue, scatter_prev):
        per = tm // TOP_K_INNER
        if issue:
            for k in range(per):
                r = kx * per + k
                if scatter_prev:
                    scatter_copy(rt_ref[(i - 1) * tm + r], 1 - slot, r).start()
                gather_copy(rt_ref[(i + 1) * tm + r], 1 - slot, r).start()
        xb = xb_ref[...]
        g = _dot(xb, wg_ref[...].astype(BF16))
        u = _dot(xb, wu_ref[...].astype(BF16))
        hid = (g * jax.nn.sigmoid(g)) * u * gws[j]
        acc[...] += _dot(hid.astype(BF16), wd_ref[...].astype(BF16))

    run = jnp.logical_and(nv > 0, kx >= 0)
    carries = kx < TOP_K_INNER

    @pl.when(jnp.logical_and(run, jnp.logical_and(carries, prev_deferred)))
    def _():
        expert_step(True, True)

    @pl.when(jnp.logical_and(run, jnp.logical_and(carries, jnp.logical_not(prev_deferred))))
    def _():
        expert_step(True, False)

    @pl.when(jnp.logical_and(run, jnp.logical_not(carries)))
    def _():
        expert_step(False, False)

    @pl.when(j == epg - 1)
    def _():
        @pl.when(prev_deferred)
        def _():
            scatter_wait(1 - slot, tm)

        @pl.when(nv > 0)
        def _():
            _store_row_tiles(obuf, slot * tm * ns, acc[...].astype(BF16))

        @pl.when(jnp.logical_and(nv > 0, jnp.logical_not(deferred)))
        def _():
            for_rows(nv, lambda r: scatter_copy(rt_ref[i * tm + r], slot, r).start())
            wait_scatter(nv, slot)


def _moe(h2p, w_r, b_r, w_g, w_u, w_d, plan, n_groups, epg, tm):
    t, ns, _ = h2p.shape
    d = 2 * ns * LANES
    ff = w_g.shape[-1]
    n_tiles = plan[0].shape[0]
    assert tm & (tm - 1) == 0, "row-count waits decompose tm in binary"
    assert tm % TOP_K_INNER == 0
    kern = functools.partial(_moe_kernel, n_tiles, n_groups, epg)

    def const(i, j, *_):
        return (0, 0)

    def wblk(i, j, tg, tnv, rt, cb, kx, wi):
        return (wi[i * epg + j], 0, 0)

    grid_spec = pltpu.PrefetchScalarGridSpec(
        num_scalar_prefetch=len(plan),
        grid=(n_tiles, epg),
        in_specs=[
            pl.BlockSpec(memory_space=pl.ANY),
            pl.BlockSpec((d, LANES), const),
            pl.BlockSpec((1, LANES), const),
            pl.BlockSpec((None, d, ff), wblk),
            pl.BlockSpec((None, d, ff), wblk),
            pl.BlockSpec((None, ff, d), wblk),
        ],
        out_specs=pl.BlockSpec(memory_space=pl.ANY),
        scratch_shapes=[
            pltpu.VMEM((2 * tm * ns, LANES), jnp.uint32),
            pltpu.VMEM((tm, d), BF16),
            pltpu.VMEM((tm, d), F32),
            pltpu.VMEM((2 * tm * ns, LANES), jnp.uint32),
            pltpu.VMEM((epg, tm, 1), F32),
            pltpu.SemaphoreType.DMA((2,)),
            pltpu.SemaphoreType.DMA((1,)),
        ],
    )
    return pl.pallas_call(
        kern,
        out_shape=jax.ShapeDtypeStruct((t, ns, LANES), jnp.uint32),
        grid_spec=grid_spec,
        compiler_params=_cp(("arbitrary", "arbitrary")),
        name="moe",
    )(*plan, h2p, w_r, b_r, w_g, w_u, w_d)


def _moe_plan(grp, cls, n_groups, epg, tm, n_tiles):
    i32 = jnp.int32
    t = grp.shape[0]
    pairs = _expert_pairs(epg)
    n_cls = len(pairs)
    key = grp * n_cls + cls
    onehot = (key[:, None] == jnp.arange(n_groups * n_cls, dtype=i32)[None, :]).astype(i32)
    csum = jnp.cumsum(onehot, axis=0)
    kcount = csum[-1].reshape(n_groups, n_cls)
    counts = jnp.sum(kcount, axis=1)
    cls_end = jnp.cumsum(kcount, axis=1)
    cls_off = cls_end - kcount
    ntile_g = (counts + tm - 1) // tm
    tile_end = jnp.cumsum(ntile_g)
    tile_off = tile_end - ntile_g
    key_base = (tile_off[:, None] * tm + cls_off).reshape(-1)
    slot = jnp.sum(onehot * (key_base[None, :] + csum - 1), axis=1)
    row_tok = jnp.zeros((n_tiles * tm,), i32).at[slot].set(jnp.arange(t, dtype=i32), unique_indices=True)
    tiles = jnp.arange(n_tiles, dtype=i32)
    tile_grp = jnp.minimum(jnp.sum((tiles[:, None] >= tile_end[None, :]).astype(i32), axis=1), n_groups - 1)
    row0 = (tiles - tile_off[tile_grp]) * tm
    tile_nv = jnp.clip(counts[tile_grp] - row0, 0, tm)
    tile_nv = jnp.where(tiles < tile_end[-1], tile_nv, 0)
    cb = jnp.clip(cls_end[tile_grp] - row0[:, None], 0, tm)
    rows_in = jnp.diff(jnp.minimum(cb, tile_nv[:, None]), axis=1, prepend=0)
    member = jnp.asarray(np.array([[int(k in p) for k in range(epg)] for p in pairs], np.int32))
    needed = ((rows_in > 0).astype(i32) @ member) > 0
    kx = jnp.where(needed, jnp.cumsum(needed.astype(i32), axis=1) - 1, -1)
    blk = (tile_grp[:, None] * epg + jnp.arange(epg, dtype=i32)[None, :]).reshape(-1)
    n_pos = n_tiles * epg
    pos = jnp.arange(n_pos, dtype=i32)
    flat = needed.reshape(-1)
    nxt = lax.cummin(jnp.where(flat, pos, n_pos), axis=0, reverse=True)
    last = lax.cummax(jnp.where(flat, pos, -1), axis=0)
    hold = jnp.where(nxt < n_pos, nxt, jnp.maximum(last, 0))
    wi = blk[hold]
    return (tile_grp.astype(i32), tile_nv.astype(i32), row_tok, cb.reshape(-1).astype(i32),
            kx.reshape(-1).astype(i32), wi.astype(i32))


def _final_kernel(x2_ref, m_ref, gate2_ref, gf_ref, o_ref):
    tm, d = x2_ref.shape
    ns = d // 2 // LANES
    m = _load_row_tiles(m_ref, 0, tm, ns).astype(F32)
    x3 = x2_ref[...] + gate2_ref[0] * m
    o_ref[...] = _rms(x3, gf_ref[...])


def _final(x2, moe, mod, g_final, row_blk0, n_rows, cond_of_tile, tm):
    d = x2.shape[1]
    ns = d // 2 // LANES
    return pl.pallas_call(
        _final_kernel,
        out_shape=jax.ShapeDtypeStruct((n_rows, d), F32),
        grid=(n_rows // tm,),
        in_specs=[pl.BlockSpec((tm, d), lambda i: (row_blk0 + i, 0)),
                  pl.BlockSpec((tm * ns, LANES), lambda i: (row_blk0 + i, 0)),
                  pl.BlockSpec((1, 1, d), lambda i: (cond_of_tile(i) * N_MOD + 5, 0, 0)),
                  pl.BlockSpec((1, d), lambda i: (0, 0))],
        out_specs=pl.BlockSpec((tm, d), lambda i: (i, 0)),
        compiler_params=_cp(("parallel",)),
        name="final",
    )(x2, moe, mod, g_final)


def _grid_pos_embed(n_tokens, dim):
    rows_n = n_tokens // GRID_W
    quarter = dim // 4
    omega = 1.0 / (10000.0 ** (jnp.arange(quarter, dtype=F32) / quarter))

    def emb(n):
        ang = jnp.arange(n).astype(F32)[:, None] * omega[None, :]
        return jnp.concatenate([jnp.sin(ang), jnp.cos(ang)], axis=-1)

    e_r = jnp.broadcast_to(emb(rows_n)[:, None, :], (rows_n, GRID_W, dim // 2))
    e_c = jnp.broadcast_to(emb(GRID_W)[None, :, :], (rows_n, GRID_W, dim // 2))
    return jnp.concatenate([e_r, e_c], axis=-1).reshape(n_tokens, dim)


def _state_to_lanes(s):
    b, _, _, g, p = s.shape
    return jnp.transpose(s, (0, 3, 2, 1, 4)).reshape(b, g, 4 * p)


def _lanes_to_state(x, p):
    b, g, _ = x.shape
    return jnp.transpose(x.reshape(b, g, 2, 2, p), (0, 3, 2, 1, 4))


def kernel(x_prompt, x_sample, state_ssm, c, c_ctx, w_ada, b_ada, g_norm1, g_norm2, w_in, ssm_a_re, ssm_a_im, ssm_log_dt, ssm_b_re, ssm_b_im, ssm_c_re, ssm_c_im, ssm_d, w_glu, b_glu, w_short, b_short, w_f1, b_f1, freq1, w_f2, b_f2, freq2, w_f3, b_f3, filter_bias, w_branch_a, w_branch_b, w_out, w_router_group, b_router_group, w_router_expert, b_router_expert, w_exp_gate, w_exp_up, w_exp_down, g_final):
    depth = w_ada.shape[0]
    assert depth == 1, "single-layer pipeline"
    bsz, seq, d = x_prompt.shape
    dbsz, dseq, _ = x_sample.shape
    g_all, p_state = ssm_a_re.shape[2:]
    ssm_w = ssm_d.shape[1]
    hw = filter_bias.shape[1]
    n_groups, _, epg = w_router_expert.shape[1:]
    ff = w_exp_gate.shape[-1]
    assert ssm_w // g_all * CHUNK == MXU_SIDE and 4 * p_state == MXU_SIDE
    assert ssm_w == hw and d == 2 * hw
    n_ctx = bsz * seq
    n_lat = dbsz * dseq
    t_all = n_ctx + n_lat

    xp = x_prompt.reshape(n_ctx, d)
    xs = x_sample.reshape(n_lat, d)
    pos = _grid_pos_embed(dseq, d)

    n_cond = 1 + dbsz
    cond8 = jnp.zeros((SUBLANES, d), F32).at[0].set(c_ctx).at[1:n_cond].set(c)
    mod = _ada_mod(cond8, w_ada[0], b_ada[0][None])
    mod = mod[:n_cond].reshape(n_cond * N_MOD, 1, d)

    tm_e = min(256, seq, dseq)
    tok_e = _Tok(n_ctx, n_lat, dseq, tm_e)
    tm_l = min(512, n_ctx, dseq)
    tok_l = _Tok(n_ctx, n_lat, dseq, tm_l)
    gate_col0 = ssm_w + 3 * hw
    h1 = _norm_mod(xp, xs, pos, mod, g_norm1, tok_l)
    tm_p, tn_p = min(1024, t_all), min(1024, hw)
    n_in = w_in.shape[2]
    u_a = _in_proj(h1, w_in[0], (0, 0), (0, ssm_w), F32, tm_p, tn_p, "in_proj_a")
    proj = _in_proj(h1, w_in[0], (gate_col0, n_in), (ssm_w, gate_col0), BF16, tm_p, tn_p, "in_proj")
    n_gate_cols = n_in - gate_col0

    w_state, w_so, a_step = _ssm_operators(ssm_a_re[0], ssm_a_im[0], ssm_log_dt[0], ssm_b_re[0], ssm_b_im[0],
                                           ssm_c_re[0], ssm_c_im[0], ssm_d[0])
    nc_ctx = seq // CHUNK
    nc_lat = dseq // CHUNK
    seq_per_blk = max(1, min(bsz, MXU_SIDE // nc_ctx))
    assert bsz % seq_per_blk == 0 and n_ctx % dseq == 0
    s0_ctx = jnp.zeros((bsz, g_all, 4 * p_state), F32)
    y_ctx, fin_ctx = _ssm_call(u_a, 0, n_ctx, w_state, w_so, a_step, s0_ctx, seq_per_blk, nc_ctx, "ssm_ctx")
    s0_lat = _state_to_lanes(state_ssm[:, 0].astype(F32))
    y_lat, _ = _ssm_call(u_a, n_ctx // dseq, n_lat, w_state, w_so, a_step, s0_lat, 1, nc_lat, "ssm_lat")
    new_state = _lanes_to_state(fin_ctx, p_state)[:, None]

    pa = _glu_branch(y_ctx, y_lat, w_glu[0].astype(BF16), b_glu[0][None], w_branch_a[0].astype(BF16), tok_l)

    w_short3 = jnp.transpose(w_short[0].reshape(-1, 3, hw), (1, 0, 2))
    b_short3 = b_short[0].reshape(3, 1, hw)
    vx2, x0c2 = _short_conv(proj, n_gate_cols // hw, hw, w_short3, b_short3, n_ctx, seq, dseq, tm_e)
    rates = jnp.abs(jnp.linspace(math.log(DECAY_TARGET) / DECAY_FAST, math.log(DECAY_TARGET) / DECAY_SLOW,
                                 hw, dtype=F32))
    rates2 = jnp.concatenate([rates, rates])[None]
    fbias = filter_bias[0][None].astype(F32)
    w_bb = w_branch_b[0].astype(BF16)
    pbs = []
    for seq_len, n_seq, row0 in ((seq, bsz, 0), (dseq, dbsz, n_ctx // dseq)):
        tabs = _dft_tables(seq_len)
        a2, bm2, colabs = _filt_gen(seq_len, w_f1[0], b_f1[0][None], freq1[0][None], w_f2[0],
                                    b_f2[0][None], freq2[0][None], w_f3[0], b_f3[0][None], rates2)
        kspec = _filt_dft(tabs, a2, bm2, colabs, seq_len)
        spec = _hy_fwd(tabs, vx2, kspec, n_seq, seq_len, row0)
        pbs.append(_hy_inv(tabs, spec, vx2, x0c2, fbias, w_bb, n_seq, seq_len, row0))
    pb_ctx, pb_lat = pbs

    w_r = jnp.zeros((d, LANES), F32)
    w_r = w_r.at[:, :n_groups].set(w_router_group[0])
    w_r = w_r.at[:, n_groups:n_groups + n_groups * epg].set(
        jnp.transpose(w_router_expert[0], (1, 0, 2)).reshape(d, n_groups * epg))
    b_r = jnp.zeros((1, LANES), F32)
    b_r = b_r.at[0, :n_groups].set(b_router_group[0])
    b_r = b_r.at[0, n_groups:n_groups + n_groups * epg].set(b_router_expert[0].reshape(-1))
    w_r = w_r.astype(BF16)
    x2, h2, rout = _merge_out(xp, xs, pos, proj, 0, pa, pb_ctx, pb_lat, mod, g_norm2,
                              w_out[0].astype(BF16), w_r, b_r, n_groups, epg, tok_e)

    tm_m = min(512, t_all // n_groups)
    n_tiles = t_all // tm_m + n_groups
    grp = rout[:, LANES - 1].astype(jnp.int32)
    cls = rout[:, LANES - 2].astype(jnp.int32)
    plan = _moe_plan(grp, cls, n_groups, epg, tm_m, n_tiles)
    ns = d // 2 // LANES
    moe = _moe(h2.reshape(t_all, ns, LANES), w_r, b_r, w_exp_gate[0], w_exp_up[0], w_exp_down[0],
               plan, n_groups, epg, tm_m)
    moe = moe.reshape(t_all * ns, LANES)

    gf = g_final[None]
    y_prompt = _final(x2, moe, mod, gf, 0, n_ctx, lambda i: 0, tm_l)
    lat_tiles = dseq // tm_l
    y_sample = _final(x2, moe, mod, gf, n_ctx // tm_l, n_lat, lambda i: 1 + i // lat_tiles, tm_l)
    return (y_prompt.reshape(bsz, seq, d), y_sample.reshape(dbsz, dseq, d), new_state)
```

```python
import functools
import math

import numpy as np
import jax
import jax.numpy as jnp
from jax import lax
from jax.experimental import pallas as pl
from jax.experimental.pallas import tpu as pltpu

F32 = jnp.float32
BF16 = jnp.bfloat16
EPS = 1e-6
GRID_W = 64
N_MOD = 6
TOP_K_INNER = 2
DECAY_TARGET = 1e-2
DECAY_FAST = 0.3
DECAY_SLOW = 1.5
LANES = 128
SUBLANES = 8
MXU_SIDE = 256
VMEM_LIMIT = 56 * 1024 * 1024


def _cp(sem, vmem=VMEM_LIMIT):
    return pltpu.CompilerParams(dimension_semantics=sem, vmem_limit_bytes=vmem)


def _dot(a, b):
    return jnp.dot(a, b, preferred_element_type=F32)


def _split(a):
    hi = a.astype(BF16)
    lo = (a - hi.astype(F32)).astype(BF16)
    return hi, lo


def _dot3(a, b):
    a_hi, a_lo = _split(a)
    b_hi, b_lo = _split(b)
    return _dot(a_hi, b_hi) + _dot(a_lo, b_hi) + _dot(a_hi, b_lo)


def _rms(x, g):
    ms = jnp.mean(x * x, axis=-1, keepdims=True)
    return x * lax.rsqrt(ms + EPS) * g


def _pack_halves(xb):
    n = xb.shape[1] // 2
    lo = lax.bitcast_convert_type(xb[:, :n].astype(F32), jnp.uint32)
    hi = lax.bitcast_convert_type(xb[:, n:].astype(F32), jnp.uint32)
    return (lo >> 16) | (hi & jnp.uint32(0xFFFF0000))


def _unpack_halves(w):
    lo = lax.bitcast_convert_type(w << 16, F32)
    hi = lax.bitcast_convert_type(w & jnp.uint32(0xFFFF0000), F32)
    return jnp.concatenate([lo, hi], axis=1).astype(BF16)


def _to_pair_rows(x, scr):
    m, n = x.shape
    ev, od = [], []
    for c in range(n // LANES):
        scr[c * m:(c + 1) * m, :] = x[:, c * LANES:(c + 1) * LANES]
        ev.append(scr[pl.ds(c * m, m // 2, stride=2), :])
        od.append(scr[pl.ds(c * m + 1, m // 2, stride=2), :])
    return jnp.concatenate(ev + od, axis=1)


def _from_pair_rows(x2, scr):
    h, n2 = x2.shape
    n = n2 // 2
    out = []
    for c in range(n // LANES):
        scr[pl.ds(2 * c * h, h, stride=2), :] = x2[:, c * LANES:(c + 1) * LANES]
        scr[pl.ds(2 * c * h + 1, h, stride=2), :] = x2[:, n + c * LANES:n + (c + 1) * LANES]
        out.append(scr[2 * c * h:2 * (c + 1) * h, :])
    return jnp.concatenate(out, axis=1)


def _store_row_tiles(ref, row0, xb):
    w = _pack_halves(xb)
    m, n = w.shape
    ns = n // LANES
    for c in range(ns):
        ref[pl.ds(row0 + c, m, stride=ns), :] = w[:, c * LANES:(c + 1) * LANES]


def _load_row_tiles(ref, row0, m, ns):
    w = jnp.concatenate([ref[pl.ds(row0 + c, m, stride=ns), :] for c in range(ns)], axis=1)
    return _unpack_halves(w)


def _ada_kernel(c_ref, w_ref, b_ref, o_ref):
    c = c_ref[...]
    a = c * jax.nn.sigmoid(c)
    o_ref[...] = _dot3(a, w_ref[...]) + b_ref[...]


def _ada_mod(cond8, w_ada, b_ada):
    d, n = w_ada.shape
    tn = min(n, 1024)
    return pl.pallas_call(
        _ada_kernel,
        out_shape=jax.ShapeDtypeStruct((SUBLANES, n), F32),
        grid=(n // tn,),
        in_specs=[pl.BlockSpec((SUBLANES, d), lambda j: (0, 0)),
                  pl.BlockSpec((d, tn), lambda j: (0, j)),
                  pl.BlockSpec((1, tn), lambda j: (0, j))],
        out_specs=pl.BlockSpec((SUBLANES, tn), lambda j: (0, j)),
        compiler_params=_cp(("arbitrary",)),
        name="ada_mod",
    )(cond8, w_ada, b_ada)


class _Tok:
    def __init__(self, n_ctx, n_lat, lat_len, tm):
        assert n_ctx % tm == 0 and lat_len % tm == 0 and n_lat % lat_len == 0
        self.tm = tm
        self.nct = n_ctx // tm
        self.nst = n_lat // tm
        self.per_seq = lat_len // tm
        self.n = self.nct + self.nst

    def ctx_idx(self, i):
        return jnp.minimum(i, self.nct - 1)

    def lat_idx(self, i):
        return jnp.maximum(i - self.nct, 0)

    def pos_idx(self, i):
        return self.lat_idx(i) % self.per_seq

    def cond(self, i):
        return jnp.where(i < self.nct, 0, 1 + self.lat_idx(i) // self.per_seq)


def _norm_mod_kernel(tok, xp_ref, xs_ref, pos_ref, sh_ref, sc_ref, g_ref, h_ref):
    i = pl.program_id(0)

    def modulate(x):
        y = _rms(x, g_ref[...])
        return (y * (1.0 + sc_ref[0]) + sh_ref[0]).astype(h_ref.dtype)

    @pl.when(i < tok.nct)
    def _():
        h_ref[...] = modulate(xp_ref[...])

    @pl.when(i >= tok.nct)
    def _():
        h_ref[...] = modulate(xs_ref[...] + pos_ref[...])


def _norm_mod(xp, xs, pos, mod, g1, tok):
    d = xp.shape[1]
    tm = tok.tm
    t_all = xp.shape[0] + xs.shape[0]
    return pl.pallas_call(
        functools.partial(_norm_mod_kernel, tok),
        out_shape=jax.ShapeDtypeStruct((t_all, d), BF16),
        grid=(tok.n,),
        in_specs=[
            pl.BlockSpec((tm, d), lambda i: (tok.ctx_idx(i), 0)),
            pl.BlockSpec((tm, d), lambda i: (tok.lat_idx(i), 0)),
            pl.BlockSpec((tm, d), lambda i: (tok.pos_idx(i), 0)),
            pl.BlockSpec((1, 1, d), lambda i: (tok.cond(i) * N_MOD + 0, 0, 0)),
            pl.BlockSpec((1, 1, d), lambda i: (tok.cond(i) * N_MOD + 1, 0, 0)),
            pl.BlockSpec((1, d), lambda i: (0, 0)),
        ],
        out_specs=pl.BlockSpec((tm, d), lambda i: (i, 0)),
        compiler_params=_cp(("parallel",)),
        name="norm_mod",
    )(xp, xs, pos, mod, mod, g1)


def _in_proj_kernel(n_sig, h_ref, w_ref, o_ref, wb_ref):
    n = pl.program_id(0)
    m = pl.program_id(1)

    @pl.when(m == 0)
    def _():
        wb_ref[...] = w_ref[...].astype(BF16)

    acc = _dot(h_ref[...], wb_ref[...])

    @pl.when(n >= n_sig)
    def _():
        o_ref[...] = acc.astype(o_ref.dtype)

    @pl.when(n < n_sig)
    def _():
        o_ref[...] = jax.nn.sigmoid(acc).astype(o_ref.dtype)


def _in_proj(h, w_in, sig_cols, lin_cols, out_dtype, tm, tn, name):
    t_all, d = h.shape
    for c in sig_cols + lin_cols:
        assert c % tn == 0
    assert t_all % tm == 0
    n_sig = (sig_cols[1] - sig_cols[0]) // tn
    n_lin = (lin_cols[1] - lin_cols[0]) // tn
    sig0, lin0 = sig_cols[0] // tn, lin_cols[0] // tn

    def w_blk(n):
        return jnp.where(n < n_sig, sig0 + n, lin0 + n - n_sig)

    return pl.pallas_call(
        functools.partial(_in_proj_kernel, n_sig),
        out_shape=jax.ShapeDtypeStruct((t_all, (n_sig + n_lin) * tn), out_dtype),
        grid=(n_sig + n_lin, t_all // tm),
        in_specs=[pl.BlockSpec((tm, d), lambda n, m: (m, 0)),
                  pl.BlockSpec((d, tn), lambda n, m: (0, w_blk(n)))],
        out_specs=pl.BlockSpec((tm, tn), lambda n, m: (m, n)),
        scratch_shapes=[pltpu.VMEM((d, tn), BF16)],
        compiler_params=_cp(("arbitrary", "arbitrary")),
        name=name,
    )(h, w_in)


CHUNK = 16
GROUP_PAD = 8


def _ssm_operators(a_re, a_im, log_dt, b_re, b_im, c_re, c_im, d_skip):
    t = CHUNK
    g, p = a_re.shape[1:]
    c = b_re.shape[-1]
    lam = lax.complex(a_re.astype(F32), a_im.astype(F32))
    dt = jnp.exp(log_dt.astype(F32))[..., None]
    ld = lam * dt
    lam_bar = jnp.exp(ld)
    b_bar = ((lam_bar - 1.0) / lam)[..., None] * lax.complex(b_re.astype(F32), b_im.astype(F32))
    c_mat = lax.complex(c_re.astype(F32), c_im.astype(F32))
    taus = jnp.arange(t + 1, dtype=F32)
    e_pow = jnp.exp(taus[None, None, :, None] * ld[:, :, None, :])

    k_all = jnp.real(jnp.einsum('dgop,dgtp,dgpi->dgtoi', c_mat, e_pow[:, :, :t], b_bar))
    kf, kb = k_all[0], k_all[1]
    ii = np.arange(t)[None, :, None]
    jj = np.arange(t)[None, None, :]
    tt = np.arange(t)[:, None, None]
    place_f = jnp.asarray((jj - ii == tt).astype(np.float32))
    place_b = jnp.asarray((ii - jj == tt).astype(np.float32))
    kf = kf.at[:, 0].add(jnp.eye(c, dtype=F32)[None] * d_skip.astype(F32).reshape(g, c, 1))
    m = jnp.einsum('tij,gtoc->gicjo', place_f, kf) + jnp.einsum('tij,gtoc->gicjo', place_b, kb)
    m = m.reshape(g, t * c, t * c)

    e_f = e_pow[0][:, ::-1][:, 1:]
    e_b = e_pow[1][:, :t]
    ws_f = e_f[:, :, :, None] * b_bar[0][:, None]
    ws_b = e_b[:, :, :, None] * b_bar[1][:, None]

    def rows_ic(x):
        return jnp.transpose(x, (0, 1, 3, 2)).reshape(g, t * c, p)

    w_state = jnp.concatenate([rows_ic(jnp.real(ws_f)), rows_ic(jnp.real(ws_b)),
                               rows_ic(jnp.imag(ws_f)), rows_ic(jnp.imag(ws_b))], axis=-1)

    ce_f = c_mat[0][:, None] * e_pow[0][:, 1:, None, :]
    ce_b = c_mat[1][:, None] * e_pow[1][:, ::-1][:, :t, None, :]

    def cols_jc(x):
        return jnp.transpose(x, (0, 3, 1, 2)).reshape(g, p, t * c)

    w_so = jnp.concatenate([cols_jc(jnp.real(ce_f)), cols_jc(jnp.real(ce_b)),
                            cols_jc(-jnp.imag(ce_f)), cols_jc(-jnp.imag(ce_b))], axis=1)
    w_out = jnp.concatenate([m, w_so], axis=1)

    a_t = e_pow[:, :, t]
    a_step = jnp.stack([jnp.concatenate([jnp.real(a_t[0]), jnp.real(a_t[1])], axis=-1),
                        jnp.concatenate([jnp.imag(a_t[0]), jnp.imag(a_t[1])], axis=-1)])
    return w_state.astype(BF16), w_out.astype(BF16), a_step


def _lane_perm(n_i, gb, c):
    n = n_i * gb * c
    src = np.arange(n)
    i, q, ch = src // (gb * c), (src // c) % gb, src % c
    dst = q * (n_i * c) + i * c + ch
    p = np.zeros((n, n), np.float32)
    p[src, dst] = 1.0
    return p


def _ssm_kernel(n_seq, n_chunk, gb, g_all, x_ref, ws_ref, wo_ref, a_ref, s0_ref, pin_ref, pout_ref,
                y_ref, fin_ref, xq_ref, za_ref, zb_ref):
    phase = pl.program_id(1)
    j = pl.program_id(2)
    gp = g_all + GROUP_PAD
    nck = n_seq * n_chunk
    half = LANES // 2
    hc = CHUNK // 2

    @pl.when(phase == 0)
    def _():
        halves = []
        for h in range(2):
            xcat = jnp.concatenate([x_ref[pl.ds(h * hc + i, nck, stride=CHUNK), :].astype(BF16)
                                    for i in range(hc)], axis=1)
            halves.append(_dot(xcat, pin_ref[...]).astype(BF16))
        for q in range(gb):
            xq = jnp.concatenate([hv[:, q * LANES:(q + 1) * LANES] for hv in halves], axis=1)
            xq_ref[j * gb + q] = xq
            z = _dot(xq, ws_ref[q])
            za_ref[pl.ds(j * gb + q, nck, stride=gp), :] = z[:, :LANES]
            zb_ref[pl.ds(j * gb + q, nck, stride=gp), :] = z[:, LANES:]

    @pl.when(jnp.logical_and(phase == 1, j == 0))
    def _():
        a_re = a_ref[0]
        a_im = a_ref[1]
        lane = lax.broadcasted_iota(jnp.int32, (g_all, LANES), 1)
        is_f = lane < half
        for s in range(n_seq):
            def step(k, carry):
                s_a, s_b = carry
                rf = pl.multiple_of((s * n_chunk + k) * gp, SUBLANES)
                rb = pl.multiple_of((s * n_chunk + n_chunk - 1 - k) * gp, SUBLANES)
                zf_a = za_ref[pl.ds(rf, g_all), :]
                zf_b = zb_ref[pl.ds(rf, g_all), :]
                zb_a = za_ref[pl.ds(rb, g_all), :]
                zb_b = zb_ref[pl.ds(rb, g_all), :]
                za_ref[pl.ds(rf, g_all), :] = jnp.where(is_f, s_a, zf_a)
                zb_ref[pl.ds(rf, g_all), :] = jnp.where(is_f, s_b, zf_b)
                za_ref[pl.ds(rb, g_all), :] = jnp.where(is_f, zb_a, s_a)
                zb_ref[pl.ds(rb, g_all), :] = jnp.where(is_f, zb_b, s_b)
                z_a = jnp.where(is_f, zf_a, zb_a)
                z_b = jnp.where(is_f, zf_b, zb_b)
                n_a = a_re * s_a - a_im * s_b + z_a
                n_b = a_re * s_b + a_im * s_a + z_b
                return n_a, n_b

            init = s0_ref[s]
            f_a, f_b = lax.fori_loop(0, n_chunk, step, (init[:, :LANES], init[:, LANES:]))
            fin_ref[s] = jnp.concatenate([f_a, f_b], axis=-1)

    @pl.when(phase == 1)
    def _():
        ys = []
        for q in range(gb):
            st_a = za_ref[pl.ds(j * gb + q, nck, stride=gp), :].astype(BF16)
            st_b = zb_ref[pl.ds(j * gb + q, nck, stride=gp), :].astype(BF16)
            lhs = jnp.concatenate([xq_ref[j * gb + q], st_a, st_b], axis=-1)
            ys.append(_dot(lhs, wo_ref[q]).astype(BF16))
        for h in range(2):
            ycat = jnp.concatenate([y[:, h * LANES:(h + 1) * LANES] for y in ys], axis=1)
            yall = _dot(ycat, pout_ref[...])
            for i in range(hc):
                y_ref[pl.ds(h * hc + i, nck, stride=CHUNK), :] = yall[:, i * LANES:(i + 1) * LANES]


def _ssm_call(u, row_blk0, n_rows, w_state, w_out, a_step, s0, n_seq, n_chunk, name):
    g_all, w, _ = w_state.shape
    nck = n_seq * n_chunk
    blk = nck * CHUNK
    n_blk = n_rows // blk
    gb = LANES * CHUNK // w
    ngb = g_all // gb
    gp = g_all + GROUP_PAD
    kern = functools.partial(_ssm_kernel, n_seq, n_chunk, gb, g_all)
    assert w == 2 * LANES
    perm = _lane_perm(CHUNK // 2, gb, LANES // gb)
    p_in = jnp.asarray(perm, BF16)
    p_out = jnp.asarray(perm.T, BF16)
    n_perm = perm.shape[0]
    const = lambda b, ph, j: (0, 0)
    return pl.pallas_call(
        kern,
        out_shape=(jax.ShapeDtypeStruct((n_rows, u.shape[1]), F32),
                   jax.ShapeDtypeStruct((n_blk * n_seq, g_all, w), F32)),
        grid=(n_blk, 2, ngb),
        in_specs=[
            pl.BlockSpec((blk, LANES), lambda b, ph, j: (row_blk0 + b, jnp.where(ph == 0, j, ngb - 1))),
            pl.BlockSpec((gb, w, w), lambda b, ph, j: (jnp.where(ph == 0, j, ngb - 1), 0, 0)),
            pl.BlockSpec((gb, 2 * w, w), lambda b, ph, j: (jnp.where(ph == 0, 0, j), 0, 0)),
            pl.BlockSpec((2, g_all, LANES), lambda b, ph, j: (0, 0, 0)),
            pl.BlockSpec((n_seq, g_all, w), lambda b, ph, j: (b, 0, 0)),
            pl.BlockSpec((n_perm, n_perm), const, pipeline_mode=pl.Buffered(1)),
            pl.BlockSpec((n_perm, n_perm), const, pipeline_mode=pl.Buffered(1)),
        ],
        out_specs=(
            pl.BlockSpec((blk, LANES), lambda b, ph, j: (b, jnp.where(ph == 0, 0, j))),
            pl.BlockSpec((n_seq, g_all, w), lambda b, ph, j: (b, 0, 0)),
        ),
        scratch_shapes=[pltpu.VMEM((g_all, nck, w), BF16),
                        pltpu.VMEM((nck * gp, LANES), F32), pltpu.VMEM((nck * gp, LANES), F32)],
        compiler_params=_cp(("arbitrary", "arbitrary", "arbitrary")),
        name=name,
    )(u, w_state, w_out, a_step, s0, p_in, p_out)


def _glu_kernel(nct, yc_ref, yl_ref, wg_ref, bg_ref, wb_ref, o_ref):
    i = pl.program_id(0)
    y = jax.nn.gelu(jnp.where(i < nct, yc_ref[...], yl_ref[...]))
    z = _dot(y.astype(BF16), wg_ref[...]) + bg_ref[...]
    ya = y * jax.nn.sigmoid(z)
    o_ref[...] = _dot(ya.astype(BF16), wb_ref[...]).astype(o_ref.dtype)


def _glu_branch(y_ctx, y_lat, w_glu, b_glu, w_ba, tok):
    c = y_ctx.shape[1]
    d = w_ba.shape[1]
    tm = tok.tm
    return pl.pallas_call(
        functools.partial(_glu_kernel, tok.nct),
        out_shape=jax.ShapeDtypeStruct((y_ctx.shape[0] + y_lat.shape[0], d), BF16),
        grid=(tok.n,),
        in_specs=[pl.BlockSpec((tm, c), lambda i: (tok.ctx_idx(i), 0)),
                  pl.BlockSpec((tm, c), lambda i: (tok.lat_idx(i), 0)),
                  pl.BlockSpec((c, c), lambda i: (0, 0)),
                  pl.BlockSpec((1, c), lambda i: (0, 0)),
                  pl.BlockSpec((c, d), lambda i: (0, 0))],
        out_specs=pl.BlockSpec((tm, d), lambda i: (i, 0)),
        compiler_params=_cp(("parallel",)),
        name="glu_branch",
    )(y_ctx, y_lat, w_glu, b_glu, w_ba)


def _short_conv_kernel(nct, per_ctx, per_lat, v_ref, x0_ref, x1_ref, vp_ref, x0p_ref, x1p_ref,
                       vn_ref, x0n_ref, x1n_ref, w_ref, b_ref, vx_ref, x0c_ref, scr):
    i = pl.program_id(0)
    tm = v_ref.shape[0]
    k = jnp.where(i < nct, i % per_ctx, (i - nct) % per_lat)
    per = jnp.where(i < nct, per_ctx, per_lat)
    first = k == 0
    last = k == per - 1
    row = lax.broadcasted_iota(jnp.int32, (tm, 1), 0)

    def conv(cur_ref, prev_ref, next_ref, part):
        x = cur_ref[...].astype(F32)
        hp = jnp.where(first, 0.0, prev_ref[SUBLANES - 1:SUBLANES, :].astype(F32))
        hn = jnp.where(last, 0.0, next_ref[0:1, :].astype(F32))
        xm = jnp.where(row == 0, hp, pltpu.roll(x, 1, axis=0))
        xq = jnp.where(row == tm - 1, hn, pltpu.roll(x, tm - 1, axis=0))
        w = w_ref[part]
        return b_ref[part] + xm * w[0:1, :] + x * w[1:2, :] + xq * w[2:3, :]

    v = conv(v_ref, vp_ref, vn_ref, 0)
    x0 = conv(x0_ref, x0p_ref, x0n_ref, 1)
    x1 = conv(x1_ref, x1p_ref, x1n_ref, 2)
    vx_ref[...] = _to_pair_rows(v * x1, scr).astype(vx_ref.dtype)
    x0c_ref[...] = _to_pair_rows(x0, scr).astype(x0c_ref.dtype)


def _short_conv(proj, col_blk0, hw, w_short3, b_short3, n_ctx, seq, dseq, tm):
    t = proj.shape[0]
    assert seq % tm == 0 and dseq % tm == 0
    nt = t // tm
    r8 = tm // SUBLANES
    nb8 = t // SUBLANES

    def cur(k):
        return pl.BlockSpec((tm, hw), lambda i: (i, col_blk0 + k))

    def prev(k):
        return pl.BlockSpec((SUBLANES, hw), lambda i: (jnp.maximum(i * r8 - 1, 0), col_blk0 + k))

    def nxt(k):
        return pl.BlockSpec((SUBLANES, hw), lambda i: (jnp.minimum((i + 1) * r8, nb8 - 1), col_blk0 + k))

    kern = functools.partial(_short_conv_kernel, n_ctx // tm, seq // tm, dseq // tm)
    return pl.pallas_call(
        kern,
        out_shape=(jax.ShapeDtypeStruct((t // 2, 2 * hw), BF16), jax.ShapeDtypeStruct((t // 2, 2 * hw), BF16)),
        grid=(nt,),
        in_specs=[cur(0), cur(1), cur(2), prev(0), prev(1), prev(2), nxt(0), nxt(1), nxt(2),
                  pl.BlockSpec((3, 3, hw), lambda i: (0, 0, 0)),
                  pl.BlockSpec((3, 1, hw), lambda i: (0, 0, 0))],
        out_specs=(pl.BlockSpec((tm // 2, 2 * hw), lambda i: (i, 0)),
                   pl.BlockSpec((tm // 2, 2 * hw), lambda i: (i, 0))),
        scratch_shapes=[pltpu.VMEM((hw // LANES * tm, LANES), F32)],
        compiler_params=_cp(("parallel",)),
        name="short_conv",
    )(proj, proj, proj, proj, proj, proj, proj, proj, proj, w_short3, b_short3)


def _filt_gen_kernel(seq_len, n_bands, wt_ref, wc_ref, ws_ref, b1_ref, f1_ref, w2_ref, b2_ref, f2_ref,
                     w3_ref, b3_ref, rate_ref, a_ref, bm_ref, abs_ref, scr):
    i = pl.program_id(0)
    tl = 2 * a_ref.shape[0]
    hw = a_ref.shape[1] // 2
    pos = lax.broadcasted_iota(jnp.int32, (tl, 1), 0) + i * tl
    t = pos.astype(F32) / seq_len
    bands = (lax.broadcasted_iota(jnp.int32, (1, wc_ref.shape[0]), 1) + 1).astype(F32)
    ang = (2.0 * math.pi) * t * bands
    pre = t * wt_ref[...] + _dot3(jnp.cos(ang), wc_ref[...]) + _dot3(jnp.sin(ang), ws_ref[...])
    h = jnp.sin(f1_ref[...] * (pre + b1_ref[...]))
    h = jnp.sin(f2_ref[...] * (_dot3(h, w2_ref[...]) + b2_ref[...]))
    h = _dot3(h, w3_ref[...]) + b3_ref[...]
    h = h * jnp.exp(-t * rate_ref[...])
    h_f = h[:, :hw]
    h_b = h[:, hw:]
    colabs = jnp.sum(jnp.abs(h_f) + jnp.abs(h_b), axis=0, keepdims=True)
    h_bp = jnp.where(pos == 0, 0.0, h_b)
    a_ref[...] = _to_pair_rows(h_f + h_bp, scr).astype(a_ref.dtype)
    bm_ref[...] = _to_pair_rows(h_bp - h_f, scr).astype(bm_ref.dtype)

    @pl.when(i == 0)
    def _():
        abs_ref[...] = colabs

    @pl.when(i > 0)
    def _():
        abs_ref[...] += colabs


def _filt_gen(seq_len, w_f1, b_f1, freq1, w_f2, b_f2, freq2, w_f3, b_f3, rates2):
    n_emb, hid_raw = w_f1.shape
    n_bands = (n_emb - 1) // 2
    hw2 = w_f3.shape[1]
    hw = hw2 // 2
    tl = min(seq_len, 512)
    hid = LANES
    assert n_bands <= LANES and hid_raw <= LANES

    def pad(x, rows, cols):
        return jnp.zeros((rows, cols), F32).at[:x.shape[0], :x.shape[1]].set(x.astype(F32))

    wt = pad(w_f1[0:1], 1, hid)
    wc = pad(w_f1[1:1 + n_bands], LANES, hid)
    ws = pad(w_f1[1 + n_bands:], LANES, hid)
    b_f1, freq1, b_f2, freq2 = (pad(x, 1, hid) for x in (b_f1, freq1, b_f2, freq2))
    w_f2 = pad(w_f2, hid, hid)
    w_f3 = pad(w_f3, hid, hw2)
    full = lambda shape: pl.BlockSpec(shape, lambda i: (0,) * len(shape))
    kern = functools.partial(_filt_gen_kernel, seq_len, n_bands)
    return pl.pallas_call(
        kern,
        out_shape=(jax.ShapeDtypeStruct((seq_len // 2, hw2), BF16), jax.ShapeDtypeStruct((seq_len // 2, hw2), BF16),
                   jax.ShapeDtypeStruct((1, hw), F32)),
        grid=(seq_len // tl,),
        in_specs=[full((1, hid)), full((LANES, hid)), full((LANES, hid)), full((1, hid)), full((1, hid)),
                  full((hid, hid)), full((1, hid)), full((1, hid)), full((hid, hw2)), full((1, hw2)),
                  full((1, hw2))],
        out_specs=(pl.BlockSpec((tl // 2, hw2), lambda i: (i, 0)), pl.BlockSpec((tl // 2, hw2), lambda i: (i, 0)),
                   full((1, hw))),
        scratch_shapes=[pltpu.VMEM((hw // LANES * tl, LANES), F32)],
        compiler_params=_cp(("arbitrary",)),
        name="filt_gen",
    )(wt, wc, ws, b_f1, freq1, w_f2, b_f2, freq2, w_f3, b_f3, rates2)


def _trig_tables(row_hi, row_lo, col, period):
    col = col[None, :]
    ang_lo = ((row_lo[:, None] * col) % period).astype(F32) * (2.0 * math.pi / period)
    ang_hi = ((row_hi[:, None] * col) % period).astype(F32) * (2.0 * math.pi / period)
    c_lo, s_lo = jnp.cos(ang_lo)[None], jnp.sin(ang_lo)[None]
    c_hi, s_hi = jnp.cos(ang_hi)[:, None], jnp.sin(ang_hi)[:, None]
    n = row_hi.shape[0] * row_lo.shape[0]
    ctab = (c_hi * c_lo - s_hi * s_lo).reshape(n, col.shape[1])
    stab = (s_hi * c_lo + c_hi * s_lo).reshape(n, col.shape[1])
    return ctab.astype(BF16), stab.astype(BF16)


def _dft_tables(seq_len):
    m = seq_len // 2
    r = 1
    while r * r < m:
        r *= 2
    assert m % r == 0
    i32 = jnp.int32
    idx = jnp.arange(m, dtype=i32)
    hi = jnp.arange(m // r, dtype=i32) * r
    lo = jnp.arange(r, dtype=i32)
    ce, se = _trig_tables(hi, lo, idx, 2 * m)
    co, so = _trig_tables(hi, lo, 2 * idx + 1, 2 * seq_len)
    cot, sot = _trig_tables(2 * hi, 2 * lo + 1, idx, 2 * seq_len)
    return ce, se, co, so, cot, sot


def _alt_row(n):
    return (1 - 2 * (lax.broadcasted_iota(jnp.int32, (SUBLANES, n), 1) % 2)).astype(BF16)


def _filt_dft_kernel(seq_len, ce_ref, se_ref, co_ref, so_ref, ae_ref, ao_ref, be_ref, bo_ref, abs_ref,
                     krl_ref, krh_ref, kil_ref, kih_ref, kmid_ref):
    i = pl.program_id(0)
    scale = (1.0 / seq_len) / (abs_ref[...] + EPS)
    ec = _dot(ce_ref[...], ae_ref[...])
    oc = _dot(co_ref[...], ao_ref[...])
    es = _dot(se_ref[...], be_ref[...])
    os_ = _dot(so_ref[...], bo_ref[...])
    krl_ref[...] = (ec + oc) * scale
    krh_ref[...] = (ec - oc) * scale
    kil_ref[...] = (es + os_) * scale
    kih_ref[...] = (os_ - es) * scale

    @pl.when(i == 0)
    def _():
        sgn = _alt_row(ae_ref.shape[0])
        kmid_ref[0:1, :] = _dot(sgn, ae_ref[...])[0:1] * scale
        kmid_ref[1:2, :] = _dot(sgn, bo_ref[...])[0:1] * scale


def _filt_dft(tabs, a2, bm2, colabs, seq_len):
    m = seq_len // 2
    hw = a2.shape[1] // 2
    ce, se, co, so = tabs[:4]
    tf = min(m, 256)
    kern = functools.partial(_filt_dft_kernel, seq_len)
    mode = pl.Buffered(1) if m * hw * 2 > BIG_BLOCK_BYTES // 2 else None
    tab = pl.BlockSpec((tf, m), lambda i: (i, 0))
    res = lambda c: pl.BlockSpec((m, hw), lambda i: (0, c), pipeline_mode=mode)
    out = pl.BlockSpec((tf, hw), lambda i: (i, 0))
    sds = jax.ShapeDtypeStruct((m, hw), F32)
    return pl.pallas_call(
        kern,
        out_shape=(sds, sds, sds, sds, jax.ShapeDtypeStruct((2, hw), F32)),
        grid=(m // tf,),
        in_specs=[tab, tab, tab, tab, res(0), res(1), res(0), res(1), pl.BlockSpec((1, hw), lambda i: (0, 0))],
        out_specs=(out, out, out, out, pl.BlockSpec((2, hw), lambda i: (0, 0))),
        compiler_params=_cp(("arbitrary",)),
        name="filt_dft",
    )(ce, se, co, so, a2, a2, bm2, bm2, colabs)


def _hy_fwd_kernel(ce_ref, se_ref, co_ref, so_ref, ve_ref, vo_ref, krl_ref, krh_ref, kil_ref, kih_ref, kmid_ref,
                   ae_ref, be_ref, ao_ref, bo_ref, ymid_ref):
    j = pl.program_id(1)
    tf, m = ce_ref.shape
    nb = ae_ref.shape[0]
    krl, krh, kil, kih = krl_ref[...], krh_ref[...], kil_ref[...], kih_ref[...]
    row = lax.broadcasted_iota(jnp.int32, (tf, 1), 0) + j * tf
    dc = jnp.where(row == 0, 0.5, 1.0)
    for s in range(nb):
        ve = ve_ref[s * m:(s + 1) * m, :]
        vo = vo_ref[s * m:(s + 1) * m, :]
        ec = _dot(ce_ref[...], ve)
        oc = _dot(co_ref[...], vo)
        es = _dot(se_ref[...], ve)
        os_ = _dot(so_ref[...], vo)
        p_lo, p_hi = ec + oc, ec - oc
        q_lo, q_hi = es + os_, os_ - es
        yre_lo = (p_lo * krl + q_lo * kil) * dc
        yim_lo = p_lo * kil - q_lo * krl
        yre_hi = (p_hi * krh + q_hi * kih) * dc
        yim_hi = p_hi * kih - q_hi * krh
        ae_ref[s] = (yre_lo + yre_hi).astype(ae_ref.dtype)
        be_ref[s] = (yim_lo - yim_hi).astype(be_ref.dtype)
        ao_ref[s] = (yre_lo - yre_hi).astype(ao_ref.dtype)
        bo_ref[s] = (yim_lo + yim_hi).astype(bo_ref.dtype)

    @pl.when(j == 0)
    def _():
        sgn = _alt_row(m)
        kr_m, ki_m = kmid_ref[0:1, :], kmid_ref[1:2, :]
        for s in range(nb):
            p_m = _dot(sgn, ve_ref[s * m:(s + 1) * m, :])[0:1]
            q_m = _dot(sgn, vo_ref[s * m:(s + 1) * m, :])[0:1]
            ymid_ref[s, 0:1, :] = p_m * kr_m + q_m * ki_m
            ymid_ref[s, 1:2, :] = p_m * ki_m - q_m * kr_m


BIG_BLOCK_BYTES = 4 * 1024 * 1024


def _seq_batch(seq_len, n_seq):
    nb = max(1, min(n_seq, 4 * MXU_SIDE // seq_len))
    while n_seq % nb:
        nb -= 1
    return nb


def _hy_fwd(tabs, vx2, kspec, n_seq, seq_len, row0_blk):
    m = seq_len // 2
    hw = vx2.shape[1] // 2
    ce, se, co, so = tabs[:4]
    krl, krh, kil, kih, kmid = kspec
    tf = min(m, 256)
    nb = _seq_batch(seq_len, n_seq)
    assert row0_blk % nb == 0
    mode = pl.Buffered(1) if nb * m * hw * 2 > BIG_BLOCK_BYTES // 2 else None
    tab = pl.BlockSpec((tf, m), lambda b, j: (j, 0))
    vspec = lambda c: pl.BlockSpec((nb * m, hw), lambda b, j: (row0_blk // nb + b, c), pipeline_mode=mode)
    kt = pl.BlockSpec((tf, hw), lambda b, j: (j, 0))
    out = pl.BlockSpec((nb, tf, hw), lambda b, j: (b, j, 0))
    sds = jax.ShapeDtypeStruct((n_seq, m, hw), BF16)
    return pl.pallas_call(
        _hy_fwd_kernel,
        out_shape=(sds, sds, sds, sds, jax.ShapeDtypeStruct((n_seq, 2, hw), F32)),
        grid=(n_seq // nb, m // tf),
        in_specs=[tab, tab, tab, tab, vspec(0), vspec(1), kt, kt, kt, kt,
                  pl.BlockSpec((2, hw), lambda b, j: (0, 0))],
        out_specs=(out, out, out, out, pl.BlockSpec((nb, 2, hw), lambda b, j: (b, 0, 0))),
        compiler_params=_cp(("arbitrary", "arbitrary")),
        name="hy_fwd",
    )(ce, se, co, so, vx2, vx2, krl, krh, kil, kih, kmid)


def _hy_inv_kernel(ce_ref, se_ref, cot_ref, sot_ref, ae_ref, be_ref, ao_ref, bo_ref, ymid_ref,
                   vx_ref, x0_ref, fb_ref, wb_ref, o_ref, scr):
    j = pl.program_id(1)
    tt = ce_ref.shape[0]
    nb = ae_ref.shape[0]
    hw = fb_ref.shape[1]
    row = lax.broadcasted_iota(jnp.int32, (tt, 1), 0) + j * tt
    sgn = (1 - 2 * (row % 2)).astype(F32)
    for s in range(nb):
        rows = slice(s * tt, (s + 1) * tt)
        ymid = ymid_ref[s]
        z_e = _dot(ce_ref[...], ae_ref[s]) - _dot(se_ref[...], be_ref[s]) + sgn * ymid[0:1]
        z_o = _dot(cot_ref[...], ao_ref[s]) - _dot(sot_ref[...], bo_ref[s]) - sgn * ymid[1:2]
        proj = []
        for par, z in ((0, z_e), (1, z_o)):
            cols = slice(par * hw, (par + 1) * hw)
            z = z + vx_ref[rows, cols].astype(F32) * fb_ref[...]
            yb = x0_ref[rows, cols].astype(F32) * z
            proj.append(_dot(yb.astype(BF16), wb_ref[...]))
        o_ref[2 * s * tt:2 * (s + 1) * tt, :] = _from_pair_rows(
            jnp.concatenate(proj, axis=1), scr).astype(o_ref.dtype)


def _hy_inv(tabs, spec, vx2, x0c2, fbias, w_bb, n_seq, seq_len, row0_blk):
    m = seq_len // 2
    hw = fbias.shape[1]
    d = w_bb.shape[1]
    ce, se, _, _, cot, sot = tabs
    ae, be, ao, bo, ymid = spec
    tt = min(m, 128)
    per = m // tt
    nb = _seq_batch(seq_len, n_seq) if per == 1 else 1
    assert row0_blk % nb == 0
    mode = pl.Buffered(1) if nb * m * hw * 2 > BIG_BLOCK_BYTES // 2 else None
    tab = pl.BlockSpec((tt, m), lambda b, j: (j, 0))
    res = pl.BlockSpec((nb, m, hw), lambda b, j: (b, 0, 0), pipeline_mode=mode)
    rows_in = pl.BlockSpec((nb * tt, 2 * hw), lambda b, j: ((row0_blk // nb + b) * per + j, 0))
    return pl.pallas_call(
        _hy_inv_kernel,
        out_shape=jax.ShapeDtypeStruct((n_seq * seq_len, d), BF16),
        grid=(n_seq // nb, per),
        in_specs=[tab, tab, tab, tab, res, res, res, res,
                  pl.BlockSpec((nb, 2, hw), lambda b, j: (b, 0, 0)),
                  rows_in, rows_in,
                  pl.BlockSpec((1, hw), lambda b, j: (0, 0)),
                  pl.BlockSpec((hw, d), lambda b, j: (0, 0), pipeline_mode=pl.Buffered(1))],
        out_specs=pl.BlockSpec((nb * 2 * tt, d), lambda b, j: (b * per + j, 0)),
        scratch_shapes=[pltpu.VMEM((d // LANES * 2 * tt, LANES), F32)],
        compiler_params=_cp(("arbitrary", "arbitrary")),
        name="hy_inv",
    )(ce, se, cot, sot, ae, be, ao, bo, ymid, vx2, x0c2, fbias, w_bb)


MERGE_COL_BLOCKS = 4

def _expert_pairs(epg):
    pairs = [(a, b) for a in range(epg) for b in range(a + 1, epg)]
    order = [pairs.pop(0)]
    while pairs:
        last = order[-1]
        nxt = next((p for p in pairs if set(p) & set(last)), pairs[0])
        pairs.remove(nxt)
        order.append(nxt)
    return order


def _merge_kernel(tok, n_groups, epg, xp_ref, xs_ref, pos_ref, g0_ref, g1_ref, pa_ref, pbc_ref, pbs_ref,
                  gate1_ref, sh2_ref, sc2_ref, gn_ref, wo_ref, wr_ref, br_ref,
                  x2_ref, h2_ref, rt_ref):
    i = pl.program_id(0)
    is_ctx = i < tok.nct
    tm, d = x2_ref.shape
    pb = jnp.where(is_ctx, pbc_ref[...], pbs_ref[...])
    merged = g0_ref[...] * pa_ref[...] + g1_ref[...] * pb
    cb = d // MERGE_COL_BLOCKS
    ssq = jnp.zeros((tm, 1), F32)
    for c in range(MERGE_COL_BLOCKS):
        cols = slice(c * cb, (c + 1) * cb)
        att = _dot(merged, wo_ref[:, cols])
        x = jnp.where(is_ctx, xp_ref[:, cols], xs_ref[:, cols] + pos_ref[:, cols])
        x2c = x + gate1_ref[0][:, cols] * att
        x2_ref[:, cols] = x2c
        ssq = ssq + jnp.sum(x2c * x2c, axis=-1, keepdims=True)
    x2 = x2_ref[...]
    h2 = x2 * lax.rsqrt(ssq * (1.0 / d) + EPS) * gn_ref[...] * (1.0 + sc2_ref[0]) + sh2_ref[0]
    h2b = h2.astype(BF16)
    _store_row_tiles(h2_ref, 0, h2b)
    logits = _dot(h2b, wr_ref[...]) + br_ref[...]
    lane = lax.broadcasted_iota(jnp.int32, logits.shape, 1)
    gl = jnp.where(lane < n_groups, logits, -jnp.inf)
    gmax = jnp.max(gl, axis=-1, keepdims=True)
    gidx = jnp.min(jnp.where(gl == gmax, lane, n_groups), axis=-1, keepdims=True)
    base = n_groups + gidx * epg
    el = jnp.where(jnp.logical_and(lane >= base, lane < base + epg), logits, -jnp.inf)
    i1 = jnp.min(jnp.where(el == jnp.max(el, axis=-1, keepdims=True), lane, LANES), axis=-1, keepdims=True)
    el = jnp.where(lane == i1, -jnp.inf, el)
    i2 = jnp.min(jnp.where(el == jnp.max(el, axis=-1, keepdims=True), lane, LANES), axis=-1, keepdims=True)
    lo = jnp.minimum(i1, i2) - base
    hi = jnp.maximum(i1, i2) - base
    cls = jnp.zeros_like(gidx)
    for code, (pa_, pb_) in enumerate(_expert_pairs(epg)):
        cls = cls + code * jnp.logical_and(lo == pa_, hi == pb_).astype(jnp.int32)
    rt_ref[...] = jnp.where(lane == LANES - 1, gidx.astype(F32),
                            jnp.where(lane == LANES - 2, cls.astype(F32), logits))


def _merge_out(xp, xs, pos, proj, gate_blk0, pa, pb_ctx, pb_lat, mod, g2, w_out, w_r, b_r, n_groups, epg, tok):
    d = xp.shape[1]
    tm = tok.tm
    t_all = xp.shape[0] + xs.shape[0]
    kern = functools.partial(_merge_kernel, tok, n_groups, epg)
    one = pl.Buffered(1)
    ns = d // 2 // LANES

    def modspec(k):
        return pl.BlockSpec((1, 1, d), lambda i: (tok.cond(i) * N_MOD + k, 0, 0))

    return pl.pallas_call(
        kern,
        out_shape=(jax.ShapeDtypeStruct((t_all, d), F32), jax.ShapeDtypeStruct((t_all * ns, LANES), jnp.uint32),
                   jax.ShapeDtypeStruct((t_all, LANES), F32)),
        grid=(tok.n,),
        in_specs=[
            pl.BlockSpec((tm, d), lambda i: (tok.ctx_idx(i), 0)),
            pl.BlockSpec((tm, d), lambda i: (tok.lat_idx(i), 0)),
            pl.BlockSpec((tm, d), lambda i: (tok.pos_idx(i), 0)),
            pl.BlockSpec((tm, d), lambda i: (i, gate_blk0)),
            pl.BlockSpec((tm, d), lambda i: (i, gate_blk0 + 1)),
            pl.BlockSpec((tm, d), lambda i: (i, 0)),
            pl.BlockSpec((tm, d), lambda i: (tok.ctx_idx(i), 0)),
            pl.BlockSpec((tm, d), lambda i: (tok.lat_idx(i), 0)),
            modspec(2), modspec(3), modspec(4),
            pl.BlockSpec((1, d), lambda i: (0, 0)),
            pl.BlockSpec((d, d), lambda i: (0, 0), pipeline_mode=one),
            pl.BlockSpec((d, LANES), lambda i: (0, 0)),
            pl.BlockSpec((1, LANES), lambda i: (0, 0)),
        ],
        out_specs=(pl.BlockSpec((tm, d), lambda i: (i, 0)), pl.BlockSpec((tm * ns, LANES), lambda i: (i, 0)),
                   pl.BlockSpec((tm, LANES), lambda i: (i, 0))),
        compiler_params=_cp(("parallel",)),
        name="merge_out",
    )(xp, xs, pos, proj, proj, pa, pb_ctx, pb_lat, mod, mod, mod, g2, w_out, w_r, b_r)


ROW_UNROLL = 8

def _moe_kernel(n_tiles, n_groups, epg, tg_ref, tnv_ref, rt_ref, cb_ref, kx_ref, wi_ref,
                h2_hbm, wr_ref, br_ref, wg_ref, wu_ref, wd_ref, out_hbm,
                xbuf, xb_ref, acc, obuf, gws, gsem, ssem):
    i = pl.program_id(0)
    j = pl.program_id(1)
    tm = xb_ref.shape[0]
    ns = h2_hbm.shape[1]
    nv = tnv_ref[i]
    slot = i % 2

    def buf_rows(buf, slot_, r, rows):
        return buf.at[pl.ds(pl.multiple_of((slot_ * tm + r) * ns, ns), rows * ns), :]

    def gather_copy(tok_row, slot_, r):
        return pltpu.make_async_copy(h2_hbm.at[tok_row], buf_rows(xbuf, slot_, r, 1), gsem.at[slot_])

    def gather_wait(slot_, rows):
        dst = buf_rows(xbuf, slot_, 0, rows)
        pltpu.make_async_copy(dst, dst, gsem.at[slot_]).wait()

    def scatter_copy(tok_row, slot_, r):
        return pltpu.make_async_copy(buf_rows(obuf, slot_, r, 1), out_hbm.at[tok_row], ssem.at[0])

    def scatter_wait(slot_, rows):
        src = buf_rows(obuf, slot_, 0, rows)
        pltpu.make_async_copy(src, src, ssem.at[0]).wait()

    def for_rows(count, fn):
        n_full = count // ROW_UNROLL

        def body(b, c):
            for k in range(ROW_UNROLL):
                fn(b * ROW_UNROLL + k)
            return c

        def tail(r, c):
            fn(r)
            return c

        lax.fori_loop(0, n_full, body, 0)
        lax.fori_loop(n_full * ROW_UNROLL, count, tail, 0)

    def wait_scatter(count, slot_):
        p = tm
        while p >= 1:
            @pl.when((count & p) != 0)
            def _(p=p):
                scatter_wait(slot_, p)
            p //= 2

    nv_prev = tnv_ref[jnp.maximum(i - 1, 0)]
    nv_next = tnv_ref[jnp.minimum(i + 1, n_tiles - 1)]
    prev_deferred = jnp.logical_and(i > 0, jnp.logical_and(nv_prev == tm, nv > 0))
    deferred = jnp.logical_and(nv == tm, nv_next > 0)

    @pl.when(j == 0)
    def _():
        @pl.when(i == 0)
        def _():
            for_rows(tm, lambda r: gather_copy(rt_ref[r], 0, r).start())

        @pl.when(jnp.logical_or(i == 0, nv_prev > 0))
        def _():
            gather_wait(slot, tm)

    @pl.when(jnp.logical_and(j == 0, nv > 0))
    def _():
        grp = tg_ref[i]
        xb = _load_row_tiles(xbuf, slot * tm * ns, tm, ns)
        xb_ref[...] = xb
        logits = _dot(xb, wr_ref[...]) + br_ref[...]
        lane = lax.broadcasted_iota(jnp.int32, logits.shape, 1)
        gl = jnp.where(lane < n_groups, logits, -jnp.inf)
        gmax = jnp.max(gl, axis=-1, keepdims=True)
        ge = jnp.exp(gl - gmax)
        g_w = jnp.sum(jnp.where(lane == grp, ge, 0.0), axis=-1, keepdims=True) / jnp.sum(ge, axis=-1, keepdims=True)
        base = n_groups + grp * epg
        e = [jnp.sum(jnp.where(lane == base + k, logits, 0.0), axis=-1, keepdims=True) for k in range(epg)]
        pairs = _expert_pairs(epg)
        row = lax.broadcasted_iota(jnp.int32, (tm, 1), 0)
        cls = jnp.zeros((tm, 1), jnp.int32)
        for c in range(len(pairs) - 1):
            cls = cls + (row >= cb_ref[i * len(pairs) + c]).astype(jnp.int32)
        sel = [functools.reduce(jnp.logical_or, [cls == c for c, p in enumerate(pairs) if k in p])
               for k in range(epg)]
        emax = functools.reduce(jnp.maximum, [jnp.where(sel[k], e[k], -jnp.inf) for k in range(epg)])
        ex = [jnp.where(sel[k], jnp.exp(e[k] - emax), 0.0) for k in range(epg)]
        den = functools.reduce(lambda a, b: a + b, ex)
        for k in range(epg):
            gws[k] = ex[k] / den * g_w
        acc[...] = jnp.zeros(acc.shape, acc.dtype)

    kx = kx_ref[i * epg + j]

    def expert_step(issue, scatter_prev):
        per = tm // TOP_K_INNER
        if issue:
            for k in range(per):
                r = kx * per + k
                if scatter_prev:
                    scatter_copy(rt_ref[(i - 1) * tm + r], 1 - slot, r).start(priority=k % 2)
                gather_copy(rt_ref[(i + 1) * tm + r], 1 - slot, r).start()
        xb = xb_ref[...]
        g = _dot(xb, wg_ref[...].astype(BF16))
        u = _dot(xb, wu_ref[...].astype(BF16))
        hid = (g * jax.nn.sigmoid(g)) * u * gws[j]
        acc[...] += _dot(hid.astype(BF16), wd_ref[...].astype(BF16))

    run = jnp.logical_and(nv > 0, kx >= 0)
    carries = kx < TOP_K_INNER

    @pl.when(jnp.logical_and(run, jnp.logical_and(carries, prev_deferred)))
    def _():
        expert_step(True, True)

    @pl.when(jnp.logical_and(run, jnp.logical_and(carries, jnp.logical_not(prev_deferred))))
    def _():
        expert_step(True, False)

    @pl.when(jnp.logical_and(run, jnp.logical_not(carries)))
    def _():
        expert_step(False, False)

    @pl.when(j == epg - 1)
    def _():
        @pl.when(prev_deferred)
        def _():
            scatter_wait(1 - slot, tm)

        @pl.when(nv > 0)
        def _():
            _store_row_tiles(obuf, slot * tm * ns, acc[...].astype(BF16))

        @pl.when(jnp.logical_and(nv > 0, jnp.logical_not(deferred)))
        def _():
            for_rows(nv, lambda r: scatter_copy(rt_ref[i * tm + r], slot, r).start())
            wait_scatter(nv, slot)


def _moe(h2p, w_r, b_r, w_g, w_u, w_d, plan, n_groups, epg, tm):
    t, ns, _ = h2p.shape
    d = 2 * ns * LANES
    ff = w_g.shape[-1]
    n_tiles = plan[0].shape[0]
    assert tm & (tm - 1) == 0, "row-count waits decompose tm in binary"
    assert tm % TOP_K_INNER == 0
    kern = functools.partial(_moe_kernel, n_tiles, n_groups, epg)

    def const(i, j, *_):
        return (0, 0)

    def wblk(i, j, tg, tnv, rt, cb, kx, wi):
        return (wi[i * epg + j], 0, 0)

    grid_spec = pltpu.PrefetchScalarGridSpec(
        num_scalar_prefetch=len(plan),
        grid=(n_tiles, epg),
        in_specs=[
            pl.BlockSpec(memory_space=pl.ANY),
            pl.BlockSpec((d, LANES), const),
            pl.BlockSpec((1, LANES), const),
            pl.BlockSpec((None, d, ff), wblk),
            pl.BlockSpec((None, d, ff), wblk),
            pl.BlockSpec((None, ff, d), wblk),
        ],
        out_specs=pl.BlockSpec(memory_space=pl.ANY),
        scratch_shapes=[
            pltpu.VMEM((2 * tm * ns, LANES), jnp.uint32),
            pltpu.VMEM((tm, d), BF16),
            pltpu.VMEM((tm, d), F32),
            pltpu.VMEM((2 * tm * ns, LANES), jnp.uint32),
            pltpu.VMEM((epg, tm, 1), F32),
            pltpu.SemaphoreType.DMA((2,)),
            pltpu.SemaphoreType.DMA((1,)),
        ],
    )
    return pl.pallas_call(
        kern,
        out_shape=jax.ShapeDtypeStruct((t, ns, LANES), jnp.uint32),
        grid_spec=grid_spec,
        compiler_params=_cp(("arbitrary", "arbitrary")),
        name="moe",
    )(*plan, h2p, w_r, b_r, w_g, w_u, w_d)


def _moe_plan(grp, cls, n_groups, epg, tm, n_tiles):
    i32 = jnp.int32
    t = grp.shape[0]
    pairs = _expert_pairs(epg)
    n_cls = len(pairs)
    key = grp * n_cls + cls
    onehot = (key[:, None] == jnp.arange(n_groups * n_cls, dtype=i32)[None, :]).astype(i32)
    csum = jnp.cumsum(onehot, axis=0)
    kcount = csum[-1].reshape(n_groups, n_cls)
    counts = jnp.sum(kcount, axis=1)
    cls_end = jnp.cumsum(kcount, axis=1)
    cls_off = cls_end - kcount
    ntile_g = (counts + tm - 1) // tm
    tile_end = jnp.cumsum(ntile_g)
    tile_off = tile_end - ntile_g
    key_base = (tile_off[:, None] * tm + cls_off).reshape(-1)
    slot = jnp.sum(onehot * (key_base[None, :] + csum - 1), axis=1)
    row_tok = jnp.zeros((n_tiles * tm,), i32).at[slot].set(jnp.arange(t, dtype=i32), unique_indices=True)
    tiles = jnp.arange(n_tiles, dtype=i32)
    tile_grp = jnp.minimum(jnp.sum((tiles[:, None] >= tile_end[None, :]).astype(i32), axis=1), n_groups - 1)
    row0 = (tiles - tile_off[tile_grp]) * tm
    tile_nv = jnp.clip(counts[tile_grp] - row0, 0, tm)
    tile_nv = jnp.where(tiles < tile_end[-1], tile_nv, 0)
    cb = jnp.clip(cls_end[tile_grp] - row0[:, None], 0, tm)
    rows_in = jnp.diff(jnp.minimum(cb, tile_nv[:, None]), axis=1, prepend=0)
    member = jnp.asarray(np.array([[int(k in p) for k in range(epg)] for p in pairs], np.int32))
    needed = ((rows_in > 0).astype(i32) @ member) > 0
    kx = jnp.where(needed, jnp.cumsum(needed.astype(i32), axis=1) - 1, -1)
    blk = (tile_grp[:, None] * epg + jnp.arange(epg, dtype=i32)[None, :]).reshape(-1)
    n_pos = n_tiles * epg
    pos = jnp.arange(n_pos, dtype=i32)
    flat = needed.reshape(-1)
    nxt = lax.cummin(jnp.where(flat, pos, n_pos), axis=0, reverse=True)
    last = lax.cummax(jnp.where(flat, pos, -1), axis=0)
    hold = jnp.where(nxt < n_pos, nxt, jnp.maximum(last, 0))
    wi = blk[hold]
    return (tile_grp.astype(i32), tile_nv.astype(i32), row_tok, cb.reshape(-1).astype(i32),
            kx.reshape(-1).astype(i32), wi.astype(i32))


def _final_kernel(x2_ref, m_ref, gate2_ref, gf_ref, o_ref):
    tm, d = x2_ref.shape
    ns = d // 2 // LANES
    m = _load_row_tiles(m_ref, 0, tm, ns).astype(F32)
    x3 = x2_ref[...] + gate2_ref[0] * m
    o_ref[...] = _rms(x3, gf_ref[...])


def _final(x2, moe, mod, g_final, row_blk0, n_rows, cond_of_tile, tm):
    d = x2.shape[1]
    ns = d // 2 // LANES
    return pl.pallas_call(
        _final_kernel,
        out_shape=jax.ShapeDtypeStruct((n_rows, d), F32),
        grid=(n_rows // tm,),
        in_specs=[pl.BlockSpec((tm, d), lambda i: (row_blk0 + i, 0)),
                  pl.BlockSpec((tm * ns, LANES), lambda i: (row_blk0 + i, 0)),
                  pl.BlockSpec((1, 1, d), lambda i: (cond_of_tile(i) * N_MOD + 5, 0, 0)),
                  pl.BlockSpec((1, d), lambda i: (0, 0))],
        out_specs=pl.BlockSpec((tm, d), lambda i: (i, 0)),
        compiler_params=_cp(("parallel",)),
        name="final",
    )(x2, moe, mod, g_final)


def _grid_pos_embed(n_tokens, dim):
    rows_n = n_tokens // GRID_W
    quarter = dim // 4
    omega = 1.0 / (10000.0 ** (jnp.arange(quarter, dtype=F32) / quarter))

    def emb(n):
        ang = jnp.arange(n).astype(F32)[:, None] * omega[None, :]
        return jnp.concatenate([jnp.sin(ang), jnp.cos(ang)], axis=-1)

    e_r = jnp.broadcast_to(emb(rows_n)[:, None, :], (rows_n, GRID_W, dim // 2))
    e_c = jnp.broadcast_to(emb(GRID_W)[None, :, :], (rows_n, GRID_W, dim // 2))
    return jnp.concatenate([e_r, e_c], axis=-1).reshape(n_tokens, dim)


def _state_to_lanes(s):
    b, _, _, g, p = s.shape
    return jnp.transpose(s, (0, 3, 2, 1, 4)).reshape(b, g, 4 * p)


def _lanes_to_state(x, p):
    b, g, _ = x.shape
    return jnp.transpose(x.reshape(b, g, 2, 2, p), (0, 3, 2, 1, 4))


def kernel(x_prompt, x_sample, state_ssm, c, c_ctx, w_ada, b_ada, g_norm1, g_norm2, w_in, ssm_a_re, ssm_a_im, ssm_log_dt, ssm_b_re, ssm_b_im, ssm_c_re, ssm_c_im, ssm_d, w_glu, b_glu, w_short, b_short, w_f1, b_f1, freq1, w_f2, b_f2, freq2, w_f3, b_f3, filter_bias, w_branch_a, w_branch_b, w_out, w_router_group, b_router_group, w_router_expert, b_router_expert, w_exp_gate, w_exp_up, w_exp_down, g_final):
    depth = w_ada.shape[0]
    assert depth == 1, "single-layer pipeline"
    bsz, seq, d = x_prompt.shape
    dbsz, dseq, _ = x_sample.shape
    g_all, p_state = ssm_a_re.shape[2:]
    ssm_w = ssm_d.shape[1]
    hw = filter_bias.shape[1]
    n_groups, _, epg = w_router_expert.shape[1:]
    ff = w_exp_gate.shape[-1]
    assert ssm_w // g_all * CHUNK == MXU_SIDE and 4 * p_state == MXU_SIDE
    assert ssm_w == hw and d == 2 * hw
    n_ctx = bsz * seq
    n_lat = dbsz * dseq
    t_all = n_ctx + n_lat

    xp = x_prompt.reshape(n_ctx, d)
    xs = x_sample.reshape(n_lat, d)
    pos = _grid_pos_embed(dseq, d)

    n_cond = 1 + dbsz
    cond8 = jnp.zeros((SUBLANES, d), F32).at[0].set(c_ctx).at[1:n_cond].set(c)
    mod = _ada_mod(cond8, w_ada[0], b_ada[0][None])
    mod = mod[:n_cond].reshape(n_cond * N_MOD, 1, d)

    tm_e = min(256, seq, dseq)
    tok_e = _Tok(n_ctx, n_lat, dseq, tm_e)
    tm_l = min(512, n_ctx, dseq)
    tok_l = _Tok(n_ctx, n_lat, dseq, tm_l)
    gate_col0 = ssm_w + 3 * hw
    h1 = _norm_mod(xp, xs, pos, mod, g_norm1, tok_l)
    tm_p, tn_p = min(1024, t_all), min(1024, hw)
    n_in = w_in.shape[2]
    u_a = _in_proj(h1, w_in[0], (0, 0), (0, ssm_w), F32, tm_p, tn_p, "in_proj_a")
    proj = _in_proj(h1, w_in[0], (gate_col0, n_in), (ssm_w, gate_col0), BF16, tm_p, tn_p, "in_proj")
    n_gate_cols = n_in - gate_col0

    w_state, w_so, a_step = _ssm_operators(ssm_a_re[0], ssm_a_im[0], ssm_log_dt[0], ssm_b_re[0], ssm_b_im[0],
                                           ssm_c_re[0], ssm_c_im[0], ssm_d[0])
    nc_ctx = seq // CHUNK
    nc_lat = dseq // CHUNK
    seq_per_blk = max(1, min(bsz, MXU_SIDE // nc_ctx))
    assert bsz % seq_per_blk == 0 and n_ctx % dseq == 0
    s0_ctx = jnp.zeros((bsz, g_all, 4 * p_state), F32)
    y_ctx, fin_ctx = _ssm_call(u_a, 0, n_ctx, w_state, w_so, a_step, s0_ctx, seq_per_blk, nc_ctx, "ssm_ctx")
    s0_lat = _state_to_lanes(state_ssm[:, 0].astype(F32))
    y_lat, _ = _ssm_call(u_a, n_ctx // dseq, n_lat, w_state, w_so, a_step, s0_lat, 1, nc_lat, "ssm_lat")
    new_state = _lanes_to_state(fin_ctx, p_state)[:, None]

    pa = _glu_branch(y_ctx, y_lat, w_glu[0].astype(BF16), b_glu[0][None], w_branch_a[0].astype(BF16), tok_l)

    w_short3 = jnp.transpose(w_short[0].reshape(-1, 3, hw), (1, 0, 2))
    b_short3 = b_short[0].reshape(3, 1, hw)
    vx2, x0c2 = _short_conv(proj, n_gate_cols // hw, hw, w_short3, b_short3, n_ctx, seq, dseq, tm_e)
    rates = jnp.abs(jnp.linspace(math.log(DECAY_TARGET) / DECAY_FAST, math.log(DECAY_TARGET) / DECAY_SLOW,
                                 hw, dtype=F32))
    rates2 = jnp.concatenate([rates, rates])[None]
    fbias = filter_bias[0][None].astype(F32)
    w_bb = w_branch_b[0].astype(BF16)
    pbs = []
    for seq_len, n_seq, row0 in ((seq, bsz, 0), (dseq, dbsz, n_ctx // dseq)):
        tabs = _dft_tables(seq_len)
        a2, bm2, colabs = _filt_gen(seq_len, w_f1[0], b_f1[0][None], freq1[0][None], w_f2[0],
                                    b_f2[0][None], freq2[0][None], w_f3[0], b_f3[0][None], rates2)
        kspec = _filt_dft(tabs, a2, bm2, colabs, seq_len)
        spec = _hy_fwd(tabs, vx2, kspec, n_seq, seq_len, row0)
        pbs.append(_hy_inv(tabs, spec, vx2, x0c2, fbias, w_bb, n_seq, seq_len, row0))
    pb_ctx, pb_lat = pbs

    w_r = jnp.zeros((d, LANES), F32)
    w_r = w_r.at[:, :n_groups].set(w_router_group[0])
    w_r = w_r.at[:, n_groups:n_groups + n_groups * epg].set(
        jnp.transpose(w_router_expert[0], (1, 0, 2)).reshape(d, n_groups * epg))
    b_r = jnp.zeros((1, LANES), F32)
    b_r = b_r.at[0, :n_groups].set(b_router_group[0])
    b_r = b_r.at[0, n_groups:n_groups + n_groups * epg].set(b_router_expert[0].reshape(-1))
    w_r = w_r.astype(BF16)
    x2, h2, rout = _merge_out(xp, xs, pos, proj, 0, pa, pb_ctx, pb_lat, mod, g_norm2,
                              w_out[0].astype(BF16), w_r, b_r, n_groups, epg, tok_e)

    tm_m = min(512, t_all // n_groups)
    n_tiles = t_all // tm_m + n_groups
    grp = rout[:, LANES - 1].astype(jnp.int32)
    cls = rout[:, LANES - 2].astype(jnp.int32)
    plan = _moe_plan(grp, cls, n_groups, epg, tm_m, n_tiles)
    ns = d // 2 // LANES
    moe = _moe(h2.reshape(t_all, ns, LANES), w_r, b_r, w_exp_gate[0], w_exp_up[0], w_exp_down[0],
               plan, n_groups, epg, tm_m)
    moe = moe.reshape(t_all * ns, LANES)

    gf = g_final[None]
    y_prompt = _final(x2, moe, mod, gf, 0, n_ctx, lambda i: 0, tm_l)
    lat_tiles = dseq // tm_l
    y_sample = _final(x2, moe, mod, gf, n_ctx // tm_l, n_lat, lambda i: 1 + i // lat_tiles, tm_l)
    return (y_prompt.reshape(bsz, seq, d), y_sample.reshape(dbsz, dseq, d), new_state)
```
